```python
import jax
import jax.numpy as jnp
from jax import lax
import numpy as np

D_MODEL = 1024
BATCH = 4
SEQ = 4096
DEPTH = 2

GRID_W = 64
CTX_LEN = 256
NORM_EPS = 1e-6
ROPE_BASE = 10000.0
N_MOD = 6

MLA_HEADS = 8
MLA_Q_LORA = 384
MLA_KV_LORA = 256
MLA_NOPE = 64
MLA_ROPE = 32
MLA_V = 64
MLA_WIDTH = MLA_HEADS * MLA_V
MLA_SCALE = (MLA_NOPE + MLA_ROPE) ** -0.5
ATTN_Q_BLOCK = 128

POOL_WINDOWS = (2, 4, 8, 16)
POOL_GROUPS = 4
POOL_GROUP_DIM = 128
POOL_WIDTH = POOL_GROUPS * POOL_GROUP_DIM

SWA_Q_HEADS = 8
SWA_KV_HEADS = 2
SWA_HEAD_DIM = 64
SWA_WINDOW = 128
SWA_BLOCK = 128
SWA_WIDTH = SWA_Q_HEADS * SWA_HEAD_DIM
SWA_SCALE = SWA_HEAD_DIM ** -0.5

N_BRANCHES = 3
IN_SPLITS = (MLA_Q_LORA, MLA_KV_LORA, MLA_ROPE, POOL_WIDTH, SWA_Q_HEADS * SWA_HEAD_DIM, SWA_KV_HEADS * SWA_HEAD_DIM, SWA_KV_HEADS * SWA_HEAD_DIM, N_BRANCHES * D_MODEL)
IN_OFFSETS = tuple(sum(IN_SPLITS[:j]) for j in range(len(IN_SPLITS) + 1))
IN_WIDTH = IN_OFFSETS[-1]

N_EXPERTS = 64
TOP_K = 8
N_GROUPS = 8
TOPK_GROUPS = 4
EXPERTS_PER_GROUP = N_EXPERTS // N_GROUPS
D_EXPERT = 256
D_SHARED = 256
ROUTED_SCALE = 2.5
MOE_BLOCK = 128

kernel_name = "hybrid_mla_pool_swa_moe_dit"


def rmsnorm(x, g):
    xf = x.astype(jnp.float32)
    y = xf * lax.rsqrt(jnp.mean(xf * xf, axis=-1, keepdims=True) + NORM_EPS)
    return (y * g.astype(jnp.float32)).astype(x.dtype)


def axial_rope_tables(n_tokens, rot_dim):
    rows = n_tokens // GRID_W
    t = jnp.arange(rows * GRID_W)
    row = (t // GRID_W).astype(jnp.float32)
    col = (t % GRID_W).astype(jnp.float32)
    n_freq = rot_dim // 4
    inv_freq = ROPE_BASE ** (-jnp.arange(n_freq, dtype=jnp.float32) / n_freq)
    ang_r = row[:, None] * inv_freq[None, :]
    ang_c = col[:, None] * inv_freq[None, :]
    return (jnp.cos(ang_r), jnp.sin(ang_r), jnp.cos(ang_c), jnp.sin(ang_c))


def apply_axial_rope(x, tabs):
    cos_r, sin_r, cos_c, sin_c = (t[None, :, None, :].astype(x.dtype) for t in tabs)
    r1, r2, c1, c2 = jnp.split(x, 4, axis=-1)
    return jnp.concatenate([r1 * cos_r - r2 * sin_r, r2 * cos_r + r1 * sin_r,
                            c1 * cos_c - c2 * sin_c, c2 * cos_c + c1 * sin_c], axis=-1)


def in_proj(h, w_in, first, last):
    y = h @ w_in[:, IN_OFFSETS[first]:IN_OFFSETS[last]]
    cuts = [IN_OFFSETS[j] - IN_OFFSETS[first] for j in range(first + 1, last)]
    return jnp.split(y, cuts, axis=-1)


def mla_queries(cq, g_q, w_uq):
    B, L, _ = cq.shape
    q = (rmsnorm(cq, g_q) @ w_uq).reshape(B, L, MLA_HEADS, MLA_NOPE + MLA_ROPE)
    return q[..., :MLA_NOPE], q[..., MLA_NOPE:]


def mla_keys(ckv, g_kv, w_ukv):
    B, L, _ = ckv.shape
    kv = (rmsnorm(ckv, g_kv) @ w_ukv).reshape(B, L, MLA_HEADS, MLA_NOPE + MLA_V)
    return kv[..., :MLA_NOPE], kv[..., MLA_NOPE:]


def mla_attend(q_nope, q_rope, k_nope, k_rope, v):
    s = jnp.einsum('bqhd,bkhd->bhqk', q_nope, k_nope) + jnp.einsum('bqhd,bkd->bhqk', q_rope, k_rope)
    p = jax.nn.softmax(s.astype(jnp.float32) * MLA_SCALE, axis=-1).astype(v.dtype)
    return jnp.einsum('bhqk,bkhd->bqhd', p, v)


def mla_latent_attention(q_nope, q_rope, k_nope, k_rope, v, ck_nope, ck_rope, cv):
    B, S = q_nope.shape[:2]
    nb = S // ATTN_Q_BLOCK
    kn = jnp.concatenate([ck_nope, k_nope], axis=1)
    kr = jnp.concatenate([ck_rope, k_rope], axis=1)
    vv = jnp.concatenate([cv, v], axis=1)

    def to_blocks(t):
        return jnp.moveaxis(t.reshape((B, nb, ATTN_Q_BLOCK) + t.shape[2:]), 1, 0)

    out = lax.map(lambda qb: mla_attend(qb[0], qb[1], kn, kr, vv), (to_blocks(q_nope), to_blocks(q_rope)))
    return jnp.moveaxis(out, 0, 1).reshape(B, S, MLA_WIDTH)


def multiscale_pool(u, w_pool, pool_scale):
    B, L, _ = u.shape
    uf = u.astype(jnp.float32)
    cs = jnp.pad(jnp.cumsum(uf, axis=1), ((0, 0), (1, 0), (0, 0)))
    t = jnp.arange(L)
    outs = []
    for j, w in enumerate(POOL_WINDOWS):
        lo = jnp.clip(t - w // 2, 0, L)
        hi = jnp.clip(t + w - w // 2, 0, L)
        cnt = (hi - lo).astype(jnp.float32)[None, :, None]
        sl = slice(j * POOL_GROUP_DIM, (j + 1) * POOL_GROUP_DIM)
        cs_g = cs[:, :, sl]
        outs.append((cs_g[:, hi] - cs_g[:, lo]) / cnt - uf[:, :, sl])
    pooled = jnp.concatenate(outs, axis=-1).astype(u.dtype).reshape(B, L, POOL_GROUPS, POOL_GROUP_DIM)
    mixed = jnp.einsum('blgc,gcd->blgd', pooled, w_pool).reshape(B, L, POOL_WIDTH)
    return mixed * pool_scale


def swa_latent_attention(q, k, v, kc, vc, sink):
    B, S = q.shape[:2]
    G = SWA_Q_HEADS // SWA_KV_HEADS
    nb = S // SWA_BLOCK
    qb = q.reshape(B, nb, SWA_BLOCK, SWA_KV_HEADS, G, SWA_HEAD_DIM)

    def band(t):
        tp = jnp.pad(t, ((0, 0), (SWA_BLOCK, SWA_BLOCK), (0, 0), (0, 0)))
        tp = tp.reshape(B, nb + 2, SWA_BLOCK, SWA_KV_HEADS, SWA_HEAD_DIM)
        return jnp.concatenate([tp[:, :-2], tp[:, 1:-1], tp[:, 2:]], axis=2)

    kb, vb = band(k), band(v)
    blk = jnp.arange(nb)[:, None, None]
    qpos = blk * SWA_BLOCK + jnp.arange(SWA_BLOCK)[None, :, None]
    kpos = (blk - 1) * SWA_BLOCK + jnp.arange(3 * SWA_BLOCK)[None, None, :]
    mask = (jnp.abs(qpos - kpos) <= SWA_WINDOW) & (kpos >= 0) & (kpos < S)
    s_band = jnp.einsum('bnqhgd,bnkhd->bnhgqk', qb, kb).astype(jnp.float32) * SWA_SCALE
    s_band = jnp.where(mask[None, :, None, None], s_band, -jnp.inf)
    s_ctx = jnp.einsum('bnqhgd,bchd->bnhgqc', qb, kc).astype(jnp.float32) * SWA_SCALE
    s_sink = jnp.broadcast_to(sink.astype(jnp.float32).reshape(1, 1, SWA_KV_HEADS, G, 1, 1), s_band.shape[:-1] + (1,))
    p = jax.nn.softmax(jnp.concatenate([s_band, s_ctx, s_sink], axis=-1), axis=-1).astype(v.dtype)
    n_band = 3 * SWA_BLOCK
    n_ctx = kc.shape[1]
    out = (jnp.einsum('bnhgqk,bnkhd->bnqhgd', p[..., :n_band], vb)
           + jnp.einsum('bnhgqc,bchd->bnqhgd', p[..., n_band:n_band + n_ctx], vc))
    return out.reshape(B, S, SWA_WIDTH)


def swa_context_attention(q, k, v, sink):
    B, C = q.shape[:2]
    G = SWA_Q_HEADS // SWA_KV_HEADS
    qg = q.reshape(B, C, SWA_KV_HEADS, G, SWA_HEAD_DIM)
    s = jnp.einsum('bqhgd,bkhd->bhgqk', qg, k).astype(jnp.float32) * SWA_SCALE
    s_sink = jnp.broadcast_to(sink.astype(jnp.float32).reshape(1, SWA_KV_HEADS, G, 1, 1), s.shape[:-1] + (1,))
    p = jax.nn.softmax(jnp.concatenate([s, s_sink], axis=-1), axis=-1)[..., :-1].astype(v.dtype)
    return jnp.einsum('bhgqk,bkhd->bqhgd', p, v).reshape(B, C, SWA_WIDTH)


def merge_branches(gate_logits, y_mla, y_pool, y_swa, w_br_mla, w_br_pool, w_br_swa, w_out):
    g_mla, g_pool, g_swa = jnp.split(jax.nn.sigmoid(gate_logits), N_BRANCHES, axis=-1)
    merged = g_mla * (y_mla @ w_br_mla) + g_pool * (y_pool @ w_br_pool) + g_swa * (y_swa @ w_br_swa)
    return merged @ w_out


def token_mixers(h_lat, h_ctx, ctx_out, w_in, g_mla_q, g_mla_kv, w_mla_uq, w_mla_ukv, w_pool, pool_scale,
                 swa_sink, w_br_mla, w_br_pool, w_br_swa, w_out, rope_mla, rope_swa):
    B, S, _ = h_lat.shape
    C = h_ctx.shape[1]
    cq, ckv, kr, u, qs, ks, vs, gl = in_proj(h_lat, w_in, 0, 8)
    q_nope, q_rope = mla_queries(cq, g_mla_q, w_mla_uq)
    k_nope, v_mla = mla_keys(ckv, g_mla_kv, w_mla_ukv)
    q_rope = apply_axial_rope(q_rope, rope_mla)
    k_rope = apply_axial_rope(kr[:, :, None, :], rope_mla)[:, :, 0, :]
    qs = apply_axial_rope(qs.reshape(B, S, SWA_Q_HEADS, SWA_HEAD_DIM), rope_swa)
    ks = apply_axial_rope(ks.reshape(B, S, SWA_KV_HEADS, SWA_HEAD_DIM), rope_swa)
    vs = vs.reshape(B, S, SWA_KV_HEADS, SWA_HEAD_DIM)
    if ctx_out:
        ccq, cckv, ckr, cu, cqs, cks, cvs, cgl = in_proj(h_ctx, w_in, 0, 8)
    else:
        cckv, ckr = in_proj(h_ctx, w_in, 1, 3)
        cks, cvs = in_proj(h_ctx, w_in, 5, 7)
    ck_nope, cv_mla = mla_keys(cckv, g_mla_kv, w_mla_ukv)
    cks = cks.reshape(B, C, SWA_KV_HEADS, SWA_HEAD_DIM)
    cvs = cvs.reshape(B, C, SWA_KV_HEADS, SWA_HEAD_DIM)
    y_mla = mla_latent_attention(q_nope, q_rope, k_nope, k_rope, v_mla, ck_nope, ckr, cv_mla)
    y_pool = multiscale_pool(u, w_pool, pool_scale)
    y_swa = swa_latent_attention(qs, ks, vs, cks, cvs, swa_sink)
    y_lat = merge_branches(gl, y_mla, y_pool, y_swa, w_br_mla, w_br_pool, w_br_swa, w_out)
    if not ctx_out:
        return y_lat, None
    cq_nope, cq_rope = mla_queries(ccq, g_mla_q, w_mla_uq)
    cy_mla = mla_attend(cq_nope, cq_rope, ck_nope, ckr, cv_mla).reshape(B, C, MLA_WIDTH)
    cy_pool = multiscale_pool(cu, w_pool, pool_scale)
    cy_swa = swa_context_attention(cqs.reshape(B, C, SWA_Q_HEADS, SWA_HEAD_DIM), cks, cvs, swa_sink)
    y_ctx = merge_branches(cgl, cy_mla, cy_pool, cy_swa, w_br_mla, w_br_pool, w_br_swa, w_out)
    return y_lat, y_ctx


def moe_ffn(h, w_router, router_bias, w_eg, w_eu, w_ed, w_sg, w_su, w_sd):
    T, D = h.shape
    scores = jax.nn.sigmoid((h @ w_router).astype(jnp.float32))
    sel = scores + router_bias.astype(jnp.float32)
    grp_score = lax.top_k(sel.reshape(T, N_GROUPS, EXPERTS_PER_GROUP), 2)[0].sum(-1)
    grp_idx = lax.top_k(grp_score, TOPK_GROUPS)[1]
    grp_mask = jnp.any(grp_idx[:, :, None] == jnp.arange(N_GROUPS)[None, None, :], axis=1)
    sel = jnp.where(jnp.repeat(grp_mask, EXPERTS_PER_GROUP, axis=1), sel, -jnp.inf)
    idx = lax.top_k(sel, TOP_K)[1]
    wts = jnp.take_along_axis(scores, idx, axis=1)
    wts = wts / jnp.sum(wts, axis=-1, keepdims=True) * ROUTED_SCALE
    A = T * TOP_K
    flat_e = idx.reshape(-1)
    flat_w = wts.reshape(-1)
    order = jnp.argsort(flat_e)
    sorted_e = flat_e[order]
    counts = jnp.bincount(flat_e, length=N_EXPERTS)
    padded = (counts + MOE_BLOCK - 1) // MOE_BLOCK * MOE_BLOCK
    pad_end = jnp.cumsum(padded)
    pad_start = pad_end - padded
    start = jnp.cumsum(counts) - counts
    dest = pad_start[sorted_e] + jnp.arange(A) - start[sorted_e]
    n_rows = -(-(A + N_EXPERTS * (MOE_BLOCK - 1)) // MOE_BLOCK) * MOE_BLOCK
    n_blocks = n_rows // MOE_BLOCK
    row_tok = jnp.full((n_rows,), T, jnp.int32).at[dest].set((order // TOP_K).astype(jnp.int32))
    row_w = jnp.zeros((n_rows,), h.dtype).at[dest].set(flat_w[order].astype(h.dtype))
    block_e = jnp.minimum(jnp.searchsorted(pad_end, jnp.arange(n_blocks) * MOE_BLOCK, side='right'), N_EXPERTS - 1)
    h_pad = jnp.concatenate([h, jnp.zeros((1, D), h.dtype)], axis=0)

    def expert_block(args):
        tok, e = args
        xb = h_pad[tok]
        return (jax.nn.silu(xb @ w_eg[e]) * (xb @ w_eu[e])) @ w_ed[e]

    y = lax.map(expert_block, (row_tok.reshape(n_blocks, MOE_BLOCK), block_e)).reshape(n_rows, D)
    routed = jnp.zeros((T + 1, D), h.dtype).at[row_tok].add(y * row_w[:, None])[:T]
    shared = (jax.nn.silu(h @ w_sg) * (h @ w_su)) @ w_sd
    return routed + shared


def setup_inputs(seed: int = 0) -> dict:
    key = jax.random.key(seed)
    ks = jax.random.split(key, 32)
    f32 = jnp.float32

    def nrm(k, shape, scale):
        return jax.random.normal(k, shape, f32) * scale

    L = DEPTH
    return {
        "x": nrm(ks[0], (BATCH, SEQ, D_MODEL), 1.0),
        "c": nrm(ks[1], (BATCH, D_MODEL), 1.0),
        "ctx": nrm(ks[2], (BATCH, CTX_LEN, D_MODEL), 1.0),
        "c_ctx": nrm(ks[3], (D_MODEL,), 1.0),
        "w_mod": nrm(ks[4], (L, D_MODEL, N_MOD * D_MODEL), 0.5 * D_MODEL ** -0.5),
        "b_mod": nrm(ks[5], (L, N_MOD * D_MODEL), 0.02),
        "g_mix": 1.0 + nrm(ks[6], (L, D_MODEL), 0.05),
        "g_ffn": 1.0 + nrm(ks[7], (L, D_MODEL), 0.05),
        "w_in": nrm(ks[8], (L, D_MODEL, IN_WIDTH), D_MODEL ** -0.5),
        "g_mla_q": 1.0 + nrm(ks[9], (L, MLA_Q_LORA), 0.05),
        "g_mla_kv": 1.0 + nrm(ks[10], (L, MLA_KV_LORA), 0.05),
        "w_mla_uq": nrm(ks[11], (L, MLA_Q_LORA, MLA_HEADS * (MLA_NOPE + MLA_ROPE)), MLA_Q_LORA ** -0.5),
        "w_mla_ukv": nrm(ks[12], (L, MLA_KV_LORA, MLA_HEADS * (MLA_NOPE + MLA_V)), MLA_KV_LORA ** -0.5),
        "w_pool": nrm(ks[13], (L, POOL_GROUPS, POOL_GROUP_DIM, POOL_GROUP_DIM), POOL_GROUP_DIM ** -0.5),
        "pool_scale": 1.0 + nrm(ks[14], (L, POOL_WIDTH), 0.1),
        "swa_sink": nrm(ks[15], (L, SWA_Q_HEADS), 0.5),
        "w_br_mla": nrm(ks[16], (L, MLA_WIDTH, D_MODEL), MLA_WIDTH ** -0.5),
        "w_br_pool": nrm(ks[17], (L, POOL_WIDTH, D_MODEL), POOL_WIDTH ** -0.5),
        "w_br_swa": nrm(ks[18], (L, SWA_WIDTH, D_MODEL), SWA_WIDTH ** -0.5),
        "w_out": nrm(ks[19], (L, D_MODEL, D_MODEL), D_MODEL ** -0.5),
        "w_router": nrm(ks[20], (L, D_MODEL, N_EXPERTS), D_MODEL ** -0.5),
        "router_bias": nrm(ks[21], (L, N_EXPERTS), 0.01),
        "w_exp_gate": nrm(ks[22], (L, N_EXPERTS, D_MODEL, D_EXPERT), D_MODEL ** -0.5),
        "w_exp_up": nrm(ks[23], (L, N_EXPERTS, D_MODEL, D_EXPERT), D_MODEL ** -0.5),
        "w_exp_down": nrm(ks[24], (L, N_EXPERTS, D_EXPERT, D_MODEL), D_EXPERT ** -0.5),
        "w_sh_gate": nrm(ks[25], (L, D_MODEL, D_SHARED), D_MODEL ** -0.5),
        "w_sh_up": nrm(ks[26], (L, D_MODEL, D_SHARED), D_MODEL ** -0.5),
        "w_sh_down": nrm(ks[27], (L, D_SHARED, D_MODEL), D_SHARED ** -0.5),
        "g_final": 1.0 + nrm(ks[28], (D_MODEL,), 0.05),
    }


def reference(x, c, ctx, c_ctx, w_mod, b_mod, g_mix, g_ffn, w_in, g_mla_q, g_mla_kv, w_mla_uq, w_mla_ukv,
              w_pool, pool_scale, swa_sink, w_br_mla, w_br_pool, w_br_swa, w_out, w_router, router_bias,
              w_exp_gate, w_exp_up, w_exp_down, w_sh_gate, w_sh_up, w_sh_down, g_final):
    B, S, D = x.shape
    C = ctx.shape[1]
    rope_mla = axial_rope_tables(S, MLA_ROPE)
    rope_swa = axial_rope_tables(S, SWA_HEAD_DIM)
    xc = ctx
    for i in range(DEPTH):
        last = i == DEPTH - 1
        mod_lat = (jax.nn.silu(c) @ w_mod[i] + b_mod[i])[:, None, :]
        mod_ctx = (jax.nn.silu(c_ctx) @ w_mod[i] + b_mod[i])[None, None, :]
        sh1, sc1, gt1, sh2, sc2, gt2 = jnp.split(mod_lat, N_MOD, axis=-1)
        csh1, csc1, cgt1, csh2, csc2, cgt2 = jnp.split(mod_ctx, N_MOD, axis=-1)
        h_lat = rmsnorm(x, g_mix[i]) * (1 + sc1) + sh1
        h_ctx = rmsnorm(xc, g_mix[i]) * (1 + csc1) + csh1
        y_lat, y_ctx = token_mixers(h_lat, h_ctx, not last, w_in[i], g_mla_q[i], g_mla_kv[i], w_mla_uq[i],
                                    w_mla_ukv[i], w_pool[i], pool_scale[i], swa_sink[i], w_br_mla[i],
                                    w_br_pool[i], w_br_swa[i], w_out[i], rope_mla, rope_swa)
        x = x + gt1 * y_lat
        f_lat = (rmsnorm(x, g_ffn[i]) * (1 + sc2) + sh2).reshape(B * S, D)
        moe_args = (w_router[i], router_bias[i], w_exp_gate[i], w_exp_up[i], w_exp_down[i],
                    w_sh_gate[i], w_sh_up[i], w_sh_down[i])
        if last:
            x = x + gt2 * moe_ffn(f_lat, *moe_args).reshape(B, S, D)
        else:
            xc = xc + cgt1 * y_ctx
            f_ctx = (rmsnorm(xc, g_ffn[i]) * (1 + csc2) + csh2).reshape(B * C, D)
            f_out = moe_ffn(jnp.concatenate([f_lat, f_ctx], axis=0), *moe_args)
            x = x + gt2 * f_out[:B * S].reshape(B, S, D)
            xc = xc + cgt2 * f_out[B * S:].reshape(B, C, D)
    return rmsnorm(x, g_final)
```

```python
import functools

import jax
import jax.numpy as jnp
from jax import lax
from jax.experimental import pallas as pl
from jax.experimental.pallas import tpu as pltpu

F32 = jnp.float32
BF16 = jnp.bfloat16
U32 = jnp.uint32
I32 = jnp.int32

NORM_EPS = 1e-6
ROPE_BASE = 10000.0
GRID_W = 64
N_MOD = 6

MLA_HEADS = 8
MLA_Q_LORA = 384
MLA_KV_LORA = 256
MLA_NOPE = 64
MLA_ROPE = 32
MLA_V = 64
MLA_SCALE = (MLA_NOPE + MLA_ROPE) ** -0.5
MLA_HEAD_PAD = 128

POOL_WINDOWS = (2, 4, 8, 16)
POOL_GROUP_DIM = 128
POOL_WIDTH = 512
POOL_HALO = 8

SWA_Q_HEADS = 8
SWA_KV_HEADS = 2
SWA_HEAD_DIM = 64
SWA_WINDOW = 128
SWA_BLOCK = 128
SWA_SCALE = SWA_HEAD_DIM ** -0.5
SWA_WIDTH = SWA_Q_HEADS * SWA_HEAD_DIM
SWA_KV_WIDTH = SWA_KV_HEADS * SWA_HEAD_DIM

N_EXPERTS = 64
TOP_K = 8
N_GROUPS = 8
TOPK_GROUPS = 4
EXPERTS_PER_GROUP = 8
D_EXPERT = 256
ROUTED_SCALE = 2.5

TOKEN_TILE = 256
MOE_TILE = 1024
MOE_CHUNK = 128
SEG_ALIGN = 8
MASK_VALUE = -1e30
HI16 = 0xFFFF0000

VMEM_LIMIT = 56 * 1024 * 1024

_SEG_WIDTHS = (("cq", 384), ("ckv", 256), ("u", 512), ("qs", 512), ("qs_rot", 512), ("ks", 128),
               ("ks_rot", 128), ("vs", 128), ("kr", 128), ("kr_rot", 128), ("gl", 3072))
_SEG = {}
_o = 0
for _n, _w in _SEG_WIDTHS:
    _SEG[_n] = (_o, _o + _w)
    _o += _w
FUSED_IN_WIDTH = _o


def _cparams(sem):
    return pltpu.CompilerParams(dimension_semantics=sem, vmem_limit_bytes=VMEM_LIMIT)


def _dot(a, b):
    return jnp.dot(a, b, preferred_element_type=F32)


def _dot_nt(a, b):
    return lax.dot_general(a, b, (((1,), (1,)), ((), ())), preferred_element_type=F32)


def _sigmoid(x):
    return 1.0 / (1.0 + jnp.exp(-x))


def _rms(x, g):
    return x * lax.rsqrt(jnp.mean(x * x, axis=-1, keepdims=True) + NORM_EPS) * g


def _pack_bf16_pair(v):
    n = v.shape[1] // 2
    bits = pltpu.bitcast(v.astype(BF16).astype(F32), U32)
    return (bits[:, :n] >> 16) | (bits[:, n:] & jnp.uint32(HI16))


def _unpack_lo(w):
    return pltpu.bitcast(w << 16, F32)


def _unpack_hi(w):
    return pltpu.bitcast(w & jnp.uint32(HI16), F32)


def _mod_kernel(c_ref, w_ref, b_ref, o_ref):
    c = c_ref[...]
    a = (c * _sigmoid(c)).astype(BF16)
    o_ref[0] = _dot(a, w_ref[0].astype(BF16)) + b_ref[0]


def _modulation(cvec, w_mod, b_mod):
    n_layers, d, width = w_mod.shape
    tn = width // 4
    return pl.pallas_call(
        _mod_kernel,
        grid=(n_layers, width // tn),
        in_specs=[pl.BlockSpec((8, d), lambda l, n: (0, 0)),
                  pl.BlockSpec((1, d, tn), lambda l, n: (l, 0, n)),
                  pl.BlockSpec((1, 1, tn), lambda l, n: (l, 0, n))],
        out_specs=pl.BlockSpec((1, 8, tn), lambda l, n: (l, 0, n)),
        out_shape=jax.ShapeDtypeStruct((n_layers, 8, width), F32),
        compiler_params=_cparams(("arbitrary", "arbitrary")),
        name="modulation",
    )(cvec, w_mod, b_mod.reshape(n_layers, 1, width))


def _in_kernel(x_ref, mod_ref, g_ref, w1_ref, gq_ref, wq_ref, gkv_ref, wkn_ref, wv_ref, tabm_ref, tabs_ref,
               q_ref, k_ref, v_ref, u_ref, qs_ref, ks_ref, vs_ref, gate_ref):
    x = x_ref[0]
    mod = mod_ref[0, 0]
    h = _rms(x, g_ref[...]) * (1.0 + mod[1:2]) + mod[0:1]
    hb = h.astype(BF16)

    def seg(name):
        a, b = _SEG[name]
        return _dot(hb, w1_ref[:, a:b])

    cos_m = tabm_ref[:, 0:128]
    sin_m = tabm_ref[:, 128:256]
    cos_s = tabs_ref[:, 0:128]
    sin_s = tabs_ref[:, 128:256]

    cqn = _rms(seg("cq"), gq_ref[...]).astype(BF16)
    qa = _dot(cqn, wq_ref[:, 0:1024])
    qb = _dot(cqn, wq_ref[:, 1024:2048])
    q_ref[0] = (qa * jnp.tile(cos_m, (1, MLA_HEADS)) + qb * jnp.tile(sin_m, (1, MLA_HEADS))).astype(BF16)

    ckvn = _rms(seg("ckv"), gkv_ref[...]).astype(BF16)
    kr = seg("kr") * cos_m + seg("kr_rot") * sin_m
    k_ref[0] = (_dot(ckvn, wkn_ref[...]) + jnp.tile(kr, (1, MLA_HEADS))).astype(BF16)
    v_ref[0] = _dot(ckvn, wv_ref[...]).astype(BF16)

    u_ref[0] = seg("u")

    qs_ref[0] = (seg("qs") * jnp.tile(cos_s, (1, 4)) + seg("qs_rot") * jnp.tile(sin_s, (1, 4))).astype(BF16)
    ks_ref[0] = (seg("ks") * cos_s + seg("ks_rot") * sin_s).astype(BF16)
    vs_ref[0] = seg("vs").astype(BF16)

    g0, _ = _SEG["gl"]
    for p in range(6):
        gate_ref[0, :, p * 512:(p + 1) * 512] = _sigmoid(
            _dot(hb, w1_ref[:, g0 + p * 512:g0 + (p + 1) * 512])).astype(BF16)


def _input_stage(x_all, modtab, g_mix, w1, g_q, wq, g_kv, wkn, wv, tabm, tabs):
    b, p, d = x_all.shape
    tm = TOKEN_TILE
    nj = p // tm
    tok = lambda w: pl.BlockSpec((1, tm, w), lambda bi, j: (bi, j, 0))
    full = lambda a: pl.BlockSpec(a.shape, lambda bi, j: (0,) * a.ndim)
    outs = [(1024, BF16), (1024, BF16), (512, BF16), (512, F32), (512, BF16), (128, BF16), (128, BF16),
            (3072, BF16)]
    return pl.pallas_call(
        _in_kernel,
        grid=(b, nj),
        in_specs=[tok(d),
                  pl.BlockSpec((1, 1, 8, d), lambda bi, j: (bi, jnp.minimum(j, 1), 0, 0)),
                  full(g_mix), full(w1), full(g_q), full(wq), full(g_kv), full(wkn), full(wv),
                  pl.BlockSpec((tm, 256), lambda bi, j: (j, 0)),
                  pl.BlockSpec((tm, 256), lambda bi, j: (j, 0))],
        out_specs=[tok(w) for w, _ in outs],
        out_shape=[jax.ShapeDtypeStruct((b, p, w), dt) for w, dt in outs],
        compiler_params=_cparams(("arbitrary", "arbitrary")),
        name="input_stage",
    )(x_all, modtab, g_mix, w1, g_q, wq, g_kv, wkn, wv, tabm, tabs)


def _mla_kernel(q_ref, k_ref, v_ref, y_ref, *, n_ctx, tk):
    j = pl.program_id(1)
    p_len = k_ref.shape[1]
    tq = q_ref.shape[1]
    n_steps = jnp.where(j == 0, n_ctx // tk, p_len // tk)
    for h in range(MLA_HEADS):
        q = q_ref[0, :, h * MLA_HEAD_PAD:(h + 1) * MLA_HEAD_PAD]

        def step(c, carry, h=h, q=q):
            m, l, acc = carry
            r0 = pl.multiple_of(c * tk, tk)
            k = k_ref[0, pl.ds(r0, tk), h * MLA_HEAD_PAD:(h + 1) * MLA_HEAD_PAD]
            v = v_ref[0, pl.ds(r0, tk), h * MLA_V:(h + 1) * MLA_V]
            s = _dot_nt(q, k)
            m_new = jnp.maximum(m, jnp.max(s, axis=1, keepdims=True))
            a = jnp.exp(m - m_new)
            pr = jnp.exp(s - m_new)
            l = a * l + jnp.sum(pr, axis=1, keepdims=True)
            acc = a * acc + _dot(pr.astype(BF16), v)
            return m_new, l, acc

        init = (jnp.full((tq, 1), MASK_VALUE, F32), jnp.zeros((tq, 1), F32), jnp.zeros((tq, MLA_V), F32))
        m, l, acc = lax.fori_loop(0, n_steps, step, init)
        y_ref[0, :, h * MLA_V:(h + 1) * MLA_V] = (acc / l).astype(BF16)


def _mla_attention(q, k, v, n_ctx):
    b, p, _ = q.shape
    tq = TOKEN_TILE
    return pl.pallas_call(
        functools.partial(_mla_kernel, n_ctx=n_ctx, tk=TOKEN_TILE),
        grid=(b, p // tq),
        in_specs=[pl.BlockSpec((1, tq, q.shape[2]), lambda bi, j: (bi, j, 0)),
                  pl.BlockSpec((1, p, k.shape[2]), lambda bi, j: (bi, 0, 0)),
                  pl.BlockSpec((1, p, v.shape[2]), lambda bi, j: (bi, 0, 0))],
        out_specs=pl.BlockSpec((1, tq, v.shape[2]), lambda bi, j: (bi, j, 0)),
        out_shape=jax.ShapeDtypeStruct((b, p, v.shape[2]), BF16),
        compiler_params=_cparams(("arbitrary", "arbitrary")),
        name="mla_attention",
    )(q, k, v)


def _swa_kernel(sink_ref, q_ref, k_ref, v_ref, y_ref, *, n_ctx):
    j = pl.program_id(1)
    p_len = k_ref.shape[1]
    s_len = p_len - n_ctx
    band = 3 * SWA_BLOCK
    n_ctx_tiles = n_ctx // SWA_BLOCK
    is_lat = j >= n_ctx_tiles
    n = jnp.maximum(j - n_ctx_tiles, 0)
    ws = jnp.clip((n - 1) * SWA_BLOCK, 0, s_len - band)
    kstart = pl.multiple_of(n_ctx + ws, SWA_BLOCK)
    qpos = n * SWA_BLOCK + lax.broadcasted_iota(I32, (SWA_BLOCK, band), 0)
    kpos = ws + lax.broadcasted_iota(I32, (SWA_BLOCK, band), 1)
    valid = (jnp.abs(qpos - kpos) <= SWA_WINDOW) & is_lat
    group = SWA_Q_HEADS // SWA_KV_HEADS
    for hk in range(SWA_KV_HEADS):
        cols = slice(hk * SWA_HEAD_DIM, (hk + 1) * SWA_HEAD_DIM)
        kc = k_ref[0, 0:n_ctx, cols]
        vc = v_ref[0, 0:n_ctx, cols]
        kb = k_ref[0, pl.ds(kstart, band), cols]
        vb = v_ref[0, pl.ds(kstart, band), cols]
        for g in range(group):
            h = hk * group + g
            q = q_ref[0, :, h * SWA_HEAD_DIM:(h + 1) * SWA_HEAD_DIM]
            s_c = _dot_nt(q, kc)
            s_b = jnp.where(valid, _dot_nt(q, kb), MASK_VALUE)
            sink = sink_ref[h]
            m = jnp.maximum(jnp.maximum(jnp.max(s_c, axis=1, keepdims=True), jnp.max(s_b, axis=1, keepdims=True)),
                            sink)
            p_c = jnp.exp(s_c - m)
            p_b = jnp.exp(s_b - m)
            l = jnp.sum(p_c, axis=1, keepdims=True) + jnp.sum(p_b, axis=1, keepdims=True) + jnp.exp(sink - m)
            o = _dot(p_c.astype(BF16), vc) + _dot(p_b.astype(BF16), vb)
            y_ref[0, :, h * SWA_HEAD_DIM:(h + 1) * SWA_HEAD_DIM] = (o / l).astype(BF16)


def _swa_attention(sink, qs, ks, vs, n_ctx):
    b, p, _ = qs.shape
    tq = SWA_BLOCK
    return pl.pallas_call(
        functools.partial(_swa_kernel, n_ctx=n_ctx),
        grid=(b, p // tq),
        in_specs=[pl.BlockSpec(memory_space=pltpu.SMEM),
                  pl.BlockSpec((1, tq, SWA_WIDTH), lambda bi, j: (bi, j, 0)),
                  pl.BlockSpec((1, p, SWA_KV_WIDTH), lambda bi, j: (bi, 0, 0)),
                  pl.BlockSpec((1, p, SWA_KV_WIDTH), lambda bi, j: (bi, 0, 0))],
        out_specs=pl.BlockSpec((1, tq, SWA_WIDTH), lambda bi, j: (bi, j, 0)),
        out_shape=jax.ShapeDtypeStruct((b, p, SWA_WIDTH), BF16),
        compiler_params=_cparams(("arbitrary", "arbitrary")),
        name="swa_attention",
    )(sink, qs, ks, vs)


def _pool_kernel(prev_ref, cur_ref, next_ref, w_ref, scale_ref, y_ref, ext_ref, *, n_ctx):
    j = pl.program_id(1)
    tm = cur_ref.shape[1]
    nj = pl.num_programs(1)
    s_len = (nj - 1) * tm
    has_prev = j >= 2
    has_next = (j >= 1) & (j < nj - 1)
    ext_ref[0:POOL_HALO, :] = jnp.where(has_prev, prev_ref[0], 0.0)
    ext_ref[POOL_HALO:POOL_HALO + tm, :] = cur_ref[0]
    ext_ref[POOL_HALO + tm:POOL_HALO + tm + POOL_HALO, :] = jnp.where(has_next, next_ref[0], 0.0)
    t = lax.broadcasted_iota(I32, (tm, 1), 0)
    pos = jnp.where(j == 0, t, (j - 1) * tm + t)
    seg_len = jnp.where(j == 0, n_ctx, s_len)
    for g, w in enumerate(POOL_WINDOWS):
        cols = slice(g * POOL_GROUP_DIM, (g + 1) * POOL_GROUP_DIM)
        acc = jnp.zeros((tm, POOL_GROUP_DIM), F32)
        for off in range(-(w // 2), w - w // 2):
            acc = acc + ext_ref[POOL_HALO + off:POOL_HALO + off + tm, cols]
        lo = jnp.maximum(pos - w // 2, 0)
        hi = jnp.minimum(pos + w - w // 2, seg_len)
        cnt = (hi - lo).astype(F32)
        pooled = acc / cnt - cur_ref[0, :, cols]
        y_ref[0, :, cols] = (_dot(pooled.astype(BF16), w_ref[g]) * scale_ref[:, cols]).astype(BF16)


def _pool_stage(u, w_pool, pool_scale, n_ctx):
    b, p, width = u.shape
    tm = TOKEN_TILE
    hb = tm // POOL_HALO
    n_halo_blocks = p // POOL_HALO
    return pl.pallas_call(
        functools.partial(_pool_kernel, n_ctx=n_ctx),
        grid=(b, p // tm),
        in_specs=[pl.BlockSpec((1, POOL_HALO, width), lambda bi, j: (bi, jnp.maximum(j * hb - 1, 0), 0)),
                  pl.BlockSpec((1, tm, width), lambda bi, j: (bi, j, 0)),
                  pl.BlockSpec((1, POOL_HALO, width),
                               lambda bi, j: (bi, jnp.minimum((j + 1) * hb, n_halo_blocks - 1), 0)),
                  pl.BlockSpec(w_pool.shape, lambda bi, j: (0, 0, 0)),
                  pl.BlockSpec(pool_scale.shape, lambda bi, j: (0, 0))],
        out_specs=pl.BlockSpec((1, tm, width), lambda bi, j: (bi, j, 0)),
        out_shape=jax.ShapeDtypeStruct((b, p, width), BF16),
        scratch_shapes=[pltpu.VMEM((tm + 2 * POOL_HALO, width), F32)],
        compiler_params=_cparams(("arbitrary", "arbitrary")),
        name="pool_stage",
    )(u, u, u, w_pool, pool_scale)


def _merge_kernel(ym_ref, yp_ref, ys_ref, gate_ref, x_ref, mod_ref, g_ref, wbm_ref, wbp_ref, wbs_ref, wout_ref,
                  wsg_ref, wsu_ref, wsd_ref, wrt_ref, xs_ref, fp_ref, lg_ref):
    d = x_ref.shape[2]
    mod = mod_ref[0, 0]
    gate = gate_ref[0]
    merged = (gate[:, 0:d].astype(F32) * _dot(ym_ref[0], wbm_ref[...])
              + gate[:, d:2 * d].astype(F32) * _dot(yp_ref[0], wbp_ref[...])
              + gate[:, 2 * d:3 * d].astype(F32) * _dot(ys_ref[0], wbs_ref[...]))
    x_mid = x_ref[0] + mod[2:3] * _dot(merged.astype(BF16), wout_ref[...])
    f = _rms(x_mid, g_ref[...]) * (1.0 + mod[4:5]) + mod[3:4]
    fb = f.astype(BF16)
    gsh = _dot(fb, wsg_ref[...])
    shared = _dot((gsh * _sigmoid(gsh) * _dot(fb, wsu_ref[...])).astype(BF16), wsd_ref[...])
    xs_ref[0] = x_mid + mod[5:6] * shared
    fp_ref[0] = _pack_bf16_pair(f)
    lg_ref[...] = _dot_nt(wrt_ref[...], fb)


def _merge_stage(ym, yp, ys, gates, x_all, modtab, g_ffn, wbm, wbp, wbs, wout, wsg, wsu, wsd, wrt):
    b, p, d = x_all.shape
    tm = TOKEN_TILE
    nj = p // tm
    tok = lambda w: pl.BlockSpec((1, tm, w), lambda bi, j: (bi, j, 0))
    full = lambda a: pl.BlockSpec(a.shape, lambda bi, j: (0,) * a.ndim)
    return pl.pallas_call(
        _merge_kernel,
        grid=(b, nj),
        in_specs=[tok(ym.shape[2]), tok(yp.shape[2]), tok(ys.shape[2]), tok(gates.shape[2]), tok(d),
                  pl.BlockSpec((1, 1, 8, d), lambda bi, j: (bi, jnp.minimum(j, 1), 0, 0)),
                  full(g_ffn), full(wbm), full(wbp), full(wbs), full(wout), full(wsg), full(wsu), full(wsd),
                  full(wrt)],
        out_specs=[tok(d), tok(d // 2), pl.BlockSpec((N_EXPERTS, tm), lambda bi, j: (0, bi * nj + j))],
        out_shape=[jax.ShapeDtypeStruct((b, p, d), F32),
                   jax.ShapeDtypeStruct((b, p, d // 2), U32),
                   jax.ShapeDtypeStruct((N_EXPERTS, b * p), F32)],
        compiler_params=_cparams(("arbitrary", "arbitrary")),
        name="merge_stage",
    )(ym, yp, ys, gates, x_all, modtab, g_ffn, wbm, wbp, wbs, wout, wsg, wsu, wsd, wrt)


def _route_kernel(lg_ref, bias_ref, dest_ref, wts_ref, seg_ref):
    tm = lg_ref.shape[1]
    ne = N_EXPERTS
    neg_inf = -jnp.inf
    scores = _sigmoid(lg_ref[...])
    sel = scores + bias_ref[...]
    iota_g = lax.broadcasted_iota(I32, (EXPERTS_PER_GROUP, tm), 0)
    gscore = []
    for g in range(N_GROUPS):
        sg = sel[g * EXPERTS_PER_GROUP:(g + 1) * EXPERTS_PER_GROUP]
        m1 = jnp.max(sg, axis=0, keepdims=True)
        i1 = jnp.min(jnp.where(sg == m1, iota_g, EXPERTS_PER_GROUP), axis=0, keepdims=True)
        m2 = jnp.max(jnp.where(iota_g == i1, neg_inf, sg), axis=0, keepdims=True)
        gscore.append(m1 + m2)
    rows = []
    for g in range(N_GROUPS):
        rank = jnp.zeros((1, tm), I32)
        for g2 in range(N_GROUPS):
            if g2 == g:
                continue
            beats = (gscore[g2] >= gscore[g]) if g2 < g else (gscore[g2] > gscore[g])
            rank = rank + beats.astype(I32)
        rows.append(jnp.where(rank < TOPK_GROUPS, sel[g * EXPERTS_PER_GROUP:(g + 1) * EXPERTS_PER_GROUP], neg_inf))
    cur = jnp.concatenate(rows, axis=0)
    iota_e = lax.broadcasted_iota(I32, (ne, tm), 0)
    picks = []
    member = jnp.zeros((ne, tm), F32)
    for _ in range(TOP_K):
        m = jnp.max(cur, axis=0, keepdims=True)
        idx = jnp.min(jnp.where(cur == m, iota_e, ne), axis=0, keepdims=True)
        hit = iota_e == idx
        picks.append(hit)
        member = member + hit.astype(F32)
        cur = jnp.where(hit, neg_inf, cur)
    earlier = (lax.broadcasted_iota(I32, (tm, tm), 0) < lax.broadcasted_iota(I32, (tm, tm), 1)).astype(BF16)
    pos = _dot(member.astype(BF16), earlier)
    cnt_col = jnp.sum(member, axis=1, keepdims=True)
    blocks_col = jnp.floor((cnt_col + (SEG_ALIGN - 1)) * (1.0 / SEG_ALIGN))
    lower = (lax.broadcasted_iota(I32, (ne, ne), 1) < lax.broadcasted_iota(I32, (ne, ne), 0)).astype(BF16)
    off_col = _dot(lower, jnp.broadcast_to(blocks_col, (ne, 128)).astype(BF16))[:, 0:1] * SEG_ALIGN
    base = off_col + pos
    w_rows = [jnp.sum(jnp.where(hit, scores, 0.0), axis=0, keepdims=True) for hit in picks]
    denom = w_rows[0]
    for w in w_rows[1:]:
        denom = denom + w
    for k, hit in enumerate(picks):
        dest_ref[0, k:k + 1, :] = jnp.sum(jnp.where(hit, base, 0.0), axis=0, keepdims=True).astype(I32)
        wts_ref[0, k:k + 1, :] = w_rows[k] / denom * ROUTED_SCALE
    member_pad = jnp.concatenate([member, jnp.zeros((128 - ne, tm), F32)], axis=0).astype(BF16)
    cnt_row = _dot_nt(jnp.ones((8, tm), BF16), member_pad)
    blocks_row = jnp.floor((cnt_row + (SEG_ALIGN - 1)) * (1.0 / SEG_ALIGN))
    before = (lax.broadcasted_iota(I32, (128, 128), 0) < lax.broadcasted_iota(I32, (128, 128), 1)).astype(BF16)
    off_row = _dot(blocks_row.astype(BF16), before) * SEG_ALIGN
    r = lax.broadcasted_iota(I32, (8, 128), 0)
    seg_ref[0] = jnp.where(r == 0, off_row, jnp.where(r == 1, cnt_row, 0.0)).astype(I32)


def _route_stage(lg_t, bias_col):
    ne, t_all = lg_t.shape
    tm = MOE_TILE
    nt = t_all // tm
    return pl.pallas_call(
        _route_kernel,
        grid=(nt,),
        in_specs=[pl.BlockSpec((ne, tm), lambda i: (0, i)),
                  pl.BlockSpec((ne, 1), lambda i: (0, 0))],
        out_specs=[pl.BlockSpec((1, TOP_K, tm), lambda i: (i, 0, 0)),
                   pl.BlockSpec((1, TOP_K, tm), lambda i: (i, 0, 0)),
                   pl.BlockSpec((1, 8, 128), lambda i: (i, 0, 0))],
        out_shape=[jax.ShapeDtypeStruct((nt, TOP_K, tm), I32),
                   jax.ShapeDtypeStruct((nt, TOP_K, tm), F32),
                   jax.ShapeDtypeStruct((nt, 8, 128), I32)],
        compiler_params=_cparams(("arbitrary",)),
        name="route_stage",
    )(lg_t, bias_col)


def _moe_row_stride(tm):
    cap = TOP_K * tm + N_EXPERTS * SEG_ALIGN + MOE_CHUNK
    blocks = cap // 8 + 1
    return 8 * (blocks + 1 - blocks % 2)


def _moe_kernel(dest_ref, wts_ref, seg_ref, fp_ref, xs_ref, gt_ref, wg_ref, wu_ref, wd_ref, out_ref, buf_ref,
                *, srow, sub, n_sub_per_batch):
    i = pl.program_id(0)
    e = pl.program_id(1)
    tm = dest_ref.shape[2]
    ch = MOE_CHUNK
    half = fp_ref.shape[1] * 4

    @pl.when(e == 0)
    def _group_rows():
        buf_ref[...] = jnp.zeros_like(buf_ref)

        def body(t8, carry):
            for u in range(8):
                t = t8 * 8 + u
                slab = fp_ref[pl.ds(pl.multiple_of(t * 4, 4), 4), :]
                for k in range(TOP_K):
                    buf_ref[pl.ds(dest_ref[0, k, t], 4, stride=srow), :] = slab
            return carry

        lax.fori_loop(0, tm // 8, body, 0)

    n_rows = seg_ref[0, 1, e]
    seg_off = seg_ref[0, 0, e]

    def chunk(c, carry):
        r0 = seg_off + c * ch
        words = [buf_ref[pl.ds(pl.multiple_of(q * srow + r0, 8), ch), :] for q in range(4)]
        w = jnp.concatenate(words, axis=1)
        xb = jnp.concatenate([_unpack_lo(w), _unpack_hi(w)], axis=1).astype(BF16)
        g = _dot(xb, wg_ref[0])
        hmid = g * _sigmoid(g) * _dot(xb, wu_ref[0])
        y = _dot(hmid.astype(BF16), wd_ref[0])
        packed = _pack_bf16_pair(y)
        keep_new = lax.broadcasted_iota(I32, (ch, 1), 0) < (n_rows - c * ch)
        for q in range(4):
            buf_ref[pl.ds(pl.multiple_of(q * srow + r0, 8), ch), :] = jnp.where(
                keep_new, packed[:, q * 128:(q + 1) * 128], words[q])
        return carry

    lax.fori_loop(0, (n_rows + ch - 1) // ch, chunk, 0)

    @pl.when(e == pl.num_programs(1) - 1)
    def _combine():
        def body(t8, carry):
            blk = (i * tm + t8 * 8) // sub
            bi = blk // n_sub_per_batch
            row = bi * 2 + jnp.minimum(blk - bi * n_sub_per_batch, 1)
            gate = gt_ref[row]
            for u in range(8):
                t = t8 * 8 + u
                acc_lo = jnp.zeros((4, 128), F32)
                acc_hi = jnp.zeros((4, 128), F32)
                for k in range(TOP_K):
                    words = buf_ref[pl.ds(dest_ref[0, k, t], 4, stride=srow), :]
                    wk = wts_ref[0, k, t]
                    acc_lo = acc_lo + wk * _unpack_lo(words)
                    acc_hi = acc_hi + wk * _unpack_hi(words)
                r_lo = pl.multiple_of(t * 8, 8)
                r_hi = pl.multiple_of(t * 8 + 4, 4)
                out_ref[pl.ds(r_lo, 4), :] = xs_ref[pl.ds(r_lo, 4), :] + gate[0:4] * acc_lo
                out_ref[pl.ds(r_hi, 4), :] = xs_ref[pl.ds(r_hi, 4), :] + gate[4:8] * acc_hi
            return carry

        lax.fori_loop(0, tm // 8, body, 0)
    del half


def _moe_stage(dest, wts, seg, fp4, xs8, gt2, wg, wu, wd, n_sub_per_batch):
    nt, _, tm = dest.shape
    ne = wg.shape[0]
    srow = _moe_row_stride(tm)
    smem = lambda shape: pl.BlockSpec(shape, lambda i, e: (i, 0, 0), memory_space=pltpu.SMEM)
    return pl.pallas_call(
        functools.partial(_moe_kernel, srow=srow, sub=TOKEN_TILE, n_sub_per_batch=n_sub_per_batch),
        grid=(nt, ne),
        in_specs=[smem((1, TOP_K, tm)), smem((1, TOP_K, tm)), smem((1, 8, 128)),
                  pl.BlockSpec((tm * 4, 128), lambda i, e: (i, 0)),
                  pl.BlockSpec((tm * 8, 128), lambda i, e: (i, 0)),
                  pl.BlockSpec(gt2.shape, lambda i, e: (0, 0, 0)),
                  pl.BlockSpec((1,) + wg.shape[1:], lambda i, e: (e, 0, 0)),
                  pl.BlockSpec((1,) + wu.shape[1:], lambda i, e: (e, 0, 0)),
                  pl.BlockSpec((1,) + wd.shape[1:], lambda i, e: (e, 0, 0))],
        out_specs=pl.BlockSpec((tm * 8, 128), lambda i, e: (i, 0)),
        out_shape=jax.ShapeDtypeStruct(xs8.shape, F32),
        scratch_shapes=[pltpu.VMEM((4 * srow, 128), U32)],
        compiler_params=_cparams(("arbitrary", "arbitrary")),
        name="moe_stage",
    )(dest, wts, seg, fp4, xs8, gt2, wg, wu, wd)


def _final_kernel(x_ref, g_ref, o_ref):
    o_ref[0] = _rms(x_ref[0], g_ref[...])


def _final_norm(x_all, g_final, n_ctx):
    b, p, d = x_all.shape
    tm = TOKEN_TILE
    skip = n_ctx // tm
    return pl.pallas_call(
        _final_kernel,
        grid=(b, (p - n_ctx) // tm),
        in_specs=[pl.BlockSpec((1, tm, d), lambda bi, j: (bi, j + skip, 0)),
                  pl.BlockSpec((1, d), lambda bi, j: (0, 0))],
        out_specs=pl.BlockSpec((1, tm, d), lambda bi, j: (bi, j, 0)),
        out_shape=jax.ShapeDtypeStruct((b, p - n_ctx, d), F32),
        compiler_params=_cparams(("arbitrary", "arbitrary")),
        name="final_norm",
    )(x_all, g_final)


def _rot_cols(w, n_heads, rot_dim):
    kdim = w.shape[0]
    w4 = w.reshape(kdim, n_heads, 4, rot_dim // 4)
    rot = jnp.stack([-w4[:, :, 1], w4[:, :, 0], -w4[:, :, 3], w4[:, :, 2]], axis=2)
    return rot.reshape(kdim, n_heads * rot_dim)


def _rope_pattern(s_len, n_ctx, rot_dim):
    t = jnp.arange(s_len)
    row = (t // GRID_W).astype(F32)
    col = (t % GRID_W).astype(F32)
    n_freq = rot_dim // 4
    inv_freq = ROPE_BASE ** (-jnp.arange(n_freq, dtype=F32) / n_freq)
    ang_r = row[:, None] * inv_freq[None, :]
    ang_c = col[:, None] * inv_freq[None, :]
    cos = jnp.concatenate([jnp.cos(ang_r), jnp.cos(ang_r), jnp.cos(ang_c), jnp.cos(ang_c)], axis=1)
    sin = jnp.concatenate([jnp.sin(ang_r), jnp.sin(ang_r), jnp.sin(ang_c), jnp.sin(ang_c)], axis=1)
    cos = jnp.concatenate([jnp.ones((n_ctx, rot_dim), F32), cos], axis=0)
    sin = jnp.concatenate([jnp.zeros((n_ctx, rot_dim), F32), sin], axis=0)
    return cos, sin


def _layer_weights(w_in, w_uq, w_ukv):
    d = w_in.shape[0]
    offs = [0]
    for w in (MLA_Q_LORA, MLA_KV_LORA, MLA_ROPE, POOL_WIDTH, SWA_WIDTH, SWA_KV_WIDTH, SWA_KV_WIDTH, 3 * d):
        offs.append(offs[-1] + w)
    cq, ckv, kr, u, qs, ks, vs, gl = (w_in[:, offs[i]:offs[i + 1]] for i in range(8))
    qs = qs * SWA_SCALE
    zeros = lambda n: jnp.zeros((d, n), F32)
    kr_slot = jnp.concatenate([zeros(MLA_NOPE), kr, zeros(MLA_HEAD_PAD - MLA_NOPE - MLA_ROPE)], axis=1)
    kr_rot_slot = jnp.concatenate([zeros(MLA_NOPE), _rot_cols(kr, 1, MLA_ROPE),
                                   zeros(MLA_HEAD_PAD - MLA_NOPE - MLA_ROPE)], axis=1)
    w1 = jnp.concatenate([cq, ckv, u, qs, _rot_cols(qs, SWA_Q_HEADS, SWA_HEAD_DIM), ks,
                          _rot_cols(ks, SWA_KV_HEADS, SWA_HEAD_DIM), vs, kr_slot, kr_rot_slot, gl],
                         axis=1).astype(BF16)
    lq = w_uq.shape[0]
    wq3 = (w_uq * MLA_SCALE).reshape(lq, MLA_HEADS, MLA_NOPE + MLA_ROPE)
    nope, rope = wq3[:, :, :MLA_NOPE], wq3[:, :, MLA_NOPE:]
    rope_rot = _rot_cols(rope.reshape(lq, MLA_HEADS * MLA_ROPE), MLA_HEADS, MLA_ROPE).reshape(lq, MLA_HEADS, MLA_ROPE)
    pad = jnp.zeros((lq, MLA_HEADS, MLA_HEAD_PAD - MLA_NOPE - MLA_ROPE), F32)
    wq_a = jnp.concatenate([nope, rope, pad], axis=2).reshape(lq, MLA_HEADS * MLA_HEAD_PAD)
    wq_b = jnp.concatenate([jnp.zeros_like(nope), rope_rot, pad], axis=2).reshape(lq, MLA_HEADS * MLA_HEAD_PAD)
    wq = jnp.concatenate([wq_a, wq_b], axis=1).astype(BF16)
    lkv = w_ukv.shape[0]
    wkv3 = w_ukv.reshape(lkv, MLA_HEADS, MLA_NOPE + MLA_V)
    wkn = jnp.concatenate([wkv3[:, :, :MLA_NOPE], jnp.zeros((lkv, MLA_HEADS, MLA_HEAD_PAD - MLA_NOPE), F32)],
                          axis=2).reshape(lkv, MLA_HEADS * MLA_HEAD_PAD).astype(BF16)
    wv = wkv3[:, :, MLA_NOPE:].reshape(lkv, MLA_HEADS * MLA_V).astype(BF16)
    return w1, wq, wkn, wv


def kernel(x, c, ctx, c_ctx, w_mod, b_mod, g_mix, g_ffn, w_in, g_mla_q, g_mla_kv, w_mla_uq, w_mla_ukv, w_pool,
           pool_scale, swa_sink, w_br_mla, w_br_pool, w_br_swa, w_out, w_router, router_bias, w_exp_gate,
           w_exp_up, w_exp_down, w_sh_gate, w_sh_up, w_sh_down, g_final):
    b, s_len, d = x.shape
    n_ctx = ctx.shape[1]
    n_layers = w_mod.shape[0]
    p = n_ctx + s_len
    assert n_ctx == TOKEN_TILE and s_len % TOKEN_TILE == 0 and (b * p) % MOE_TILE == 0 and b + 1 <= 8
    assert FUSED_IN_WIDTH == w_in.shape[2] + SWA_WIDTH + SWA_KV_WIDTH + 2 * MLA_HEAD_PAD - MLA_ROPE

    cm, sm = _rope_pattern(s_len, n_ctx, MLA_ROPE)
    tail = jnp.zeros((p, MLA_HEAD_PAD - MLA_NOPE - MLA_ROPE), F32)
    tabm = jnp.concatenate([jnp.ones((p, MLA_NOPE), F32), cm, tail, jnp.zeros((p, MLA_NOPE), F32), sm, tail], axis=1)
    cs, ss = _rope_pattern(s_len, n_ctx, SWA_HEAD_DIM)
    tabs = jnp.concatenate([cs, cs, ss, ss], axis=1)

    cvec = jnp.concatenate([c, c_ctx[None, :], jnp.zeros((8 - b - 1, d), F32)], axis=0)
    mod_all = _modulation(cvec, w_mod, b_mod).reshape(n_layers, 8, N_MOD, d)

    x_all = jnp.concatenate([ctx, x], axis=1)
    for i in range(n_layers):
        lat = mod_all[i, :b]
        ctx_rows = jnp.broadcast_to(mod_all[i, b][None], (b, N_MOD, d))
        modtab = jnp.pad(jnp.stack([ctx_rows, lat], axis=1), ((0, 0), (0, 0), (0, 8 - N_MOD), (0, 0)))
        w1, wq, wkn, wv = _layer_weights(w_in[i], w_mla_uq[i], w_mla_ukv[i])
        q, k, v, u, qs, ks, vs, gates = _input_stage(
            x_all, modtab, g_mix[i][None], w1, g_mla_q[i][None], wq, g_mla_kv[i][None], wkn, wv, tabm, tabs)
        y_mla = _mla_attention(q, k, v, n_ctx)
        y_swa = _swa_attention(swa_sink[i], qs, ks, vs, n_ctx)
        y_pool = _pool_stage(u, w_pool[i].astype(BF16), pool_scale[i][None], n_ctx)
        xs, fp, lg_t = _merge_stage(
            y_mla, y_pool, y_swa, gates, x_all, modtab, g_ffn[i][None], w_br_mla[i].astype(BF16),
            w_br_pool[i].astype(BF16), w_br_swa[i].astype(BF16), w_out[i].astype(BF16),
            w_sh_gate[i].astype(BF16), w_sh_up[i].astype(BF16), w_sh_down[i].astype(BF16),
            w_router[i].T.astype(BF16))
        dest, wts, seg = _route_stage(lg_t, router_bias[i][:, None])
        gt2 = modtab[:, :, 5, :].reshape(b * 2, 8, d // 8)
        x_new = _moe_stage(dest, wts, seg, fp.reshape(b * p * 4, 128), xs.reshape(b * p * 8, 128), gt2,
                           w_exp_gate[i].astype(BF16), w_exp_up[i].astype(BF16), w_exp_down[i].astype(BF16),
                           p // TOKEN_TILE)
        x_all = x_new.reshape(b, p, d)
    return _final_norm(x_all, g_final[None], n_ctx)
```

```python
import functools

import jax
import jax.numpy as jnp
from jax import lax
from jax.experimental import pallas as pl
from jax.experimental.pallas import tpu as pltpu

F32 = jnp.float32
BF16 = jnp.bfloat16
U32 = jnp.uint32
I32 = jnp.int32

NORM_EPS = 1e-6
ROPE_BASE = 10000.0
GRID_W = 64
N_MOD = 6

MLA_HEADS = 8
MLA_Q_LORA = 384
MLA_KV_LORA = 256
MLA_NOPE = 64
MLA_ROPE = 32
MLA_V = 64
MLA_SCALE = (MLA_NOPE + MLA_ROPE) ** -0.5
MLA_HEAD_PAD = 128
LOG2_E = 1.4426950408889634

POOL_WINDOWS = (2, 4, 8, 16)
POOL_GROUP_DIM = 128
POOL_WIDTH = 512
POOL_HALO = 8

SWA_Q_HEADS = 8
SWA_KV_HEADS = 2
SWA_HEAD_DIM = 64
SWA_WINDOW = 128
SWA_BLOCK = 128
SWA_SCALE = SWA_HEAD_DIM ** -0.5
SWA_WIDTH = SWA_Q_HEADS * SWA_HEAD_DIM
SWA_KV_WIDTH = SWA_KV_HEADS * SWA_HEAD_DIM

N_EXPERTS = 64
TOP_K = 8
N_GROUPS = 8
TOPK_GROUPS = 4
EXPERTS_PER_GROUP = 8
D_EXPERT = 256
ROUTED_SCALE = 2.5

TOKEN_TILE = 256
MLA_Q_TILE = 512
MLA_KEY_CHUNK = 512
MOE_TILE = 1024
MOE_CHUNK = 256
SEG_ALIGN = 8
MASK_VALUE = -1e30
HI16 = 0xFFFF0000

VMEM_LIMIT = 56 * 1024 * 1024

_SEG_WIDTHS = (("cq", 384), ("ckv", 256), ("u", 512), ("qs", 512), ("qs_rot", 512), ("ks", 128),
               ("ks_rot", 128), ("vs", 128), ("kr", 128), ("kr_rot", 128), ("gl", 3072))
_SEG = {}
_o = 0
for _n, _w in _SEG_WIDTHS:
    _SEG[_n] = (_o, _o + _w)
    _o += _w
FUSED_IN_WIDTH = _o


def _cparams(sem):
    return pltpu.CompilerParams(dimension_semantics=sem, vmem_limit_bytes=VMEM_LIMIT)


def _dot(a, b):
    return jnp.dot(a, b, preferred_element_type=F32)


def _dot_nt(a, b):
    return lax.dot_general(a, b, (((1,), (1,)), ((), ())), preferred_element_type=F32)


def _sigmoid(x):
    return 1.0 / (1.0 + jnp.exp(-x))


def _rms(x, g):
    return x * lax.rsqrt(jnp.mean(x * x, axis=-1, keepdims=True) + NORM_EPS) * g


def _pack_bf16_pair(v):
    n = v.shape[1] // 2
    bits = pltpu.bitcast(v.astype(BF16).astype(F32), U32)
    return (bits[:, :n] >> 16) | (bits[:, n:] & jnp.uint32(HI16))


def _unpack_lo(w):
    return pltpu.bitcast(w << 16, F32)


def _unpack_hi(w):
    return pltpu.bitcast(w & jnp.uint32(HI16), F32)


def _mod_kernel(c_ref, w_ref, b_ref, o_ref):
    c = c_ref[...]
    a = (c * _sigmoid(c)).astype(BF16)
    o_ref[0] = _dot(a, w_ref[0].astype(BF16)) + b_ref[0]


def _modulation(cvec, w_mod, b_mod):
    n_layers, d, width = w_mod.shape
    tn = width // 4
    return pl.pallas_call(
        _mod_kernel,
        grid=(n_layers, width // tn),
        in_specs=[pl.BlockSpec((8, d), lambda l, n: (0, 0)),
                  pl.BlockSpec((1, d, tn), lambda l, n: (l, 0, n)),
                  pl.BlockSpec((1, 1, tn), lambda l, n: (l, 0, n))],
        out_specs=pl.BlockSpec((1, 8, tn), lambda l, n: (l, 0, n)),
        out_shape=jax.ShapeDtypeStruct((n_layers, 8, width), F32),
        compiler_params=_cparams(("arbitrary", "arbitrary")),
        name="modulation",
    )(cvec, w_mod, b_mod.reshape(n_layers, 1, width))


def _in_kernel(x_ref, mod_ref, g_ref, w1_ref, gq_ref, wq_ref, gkv_ref, wkn_ref, wv_ref, tabm_ref, tabs_ref,
               q_ref, k_ref, v_ref, u_ref, qs_ref, ks_ref, vs_ref, gate_ref):
    x = x_ref[0]
    mod = mod_ref[0, 0]
    h = _rms(x, g_ref[...]) * (1.0 + mod[1:2]) + mod[0:1]
    hb = h.astype(BF16)

    def seg(name):
        a, b = _SEG[name]
        return _dot(hb, w1_ref[:, a:b])

    cos_m = tabm_ref[:, 0:128]
    sin_m = tabm_ref[:, 128:256]
    cos_s = tabs_ref[:, 0:128]
    sin_s = tabs_ref[:, 128:256]

    cqn = _rms(seg("cq"), gq_ref[...]).astype(BF16)
    qa = _dot(cqn, wq_ref[:, 0:1024])
    qb = _dot(cqn, wq_ref[:, 1024:2048])
    q_ref[0] = (qa * jnp.tile(cos_m, (1, MLA_HEADS)) + qb * jnp.tile(sin_m, (1, MLA_HEADS))).astype(BF16)

    ckvn = _rms(seg("ckv"), gkv_ref[...]).astype(BF16)
    kr = seg("kr") * cos_m + seg("kr_rot") * sin_m
    k_ref[0] = (_dot(ckvn, wkn_ref[...]) + jnp.tile(kr, (1, MLA_HEADS))).astype(BF16)
    ones_lane = (lax.broadcasted_iota(I32, (1, MLA_HEADS * MLA_HEAD_PAD), 1) % MLA_HEAD_PAD == MLA_V).astype(F32)
    v_ref[0] = (_dot(ckvn, wv_ref[...]) + ones_lane).astype(BF16)

    u_ref[0] = seg("u")

    qs_ref[0] = (seg("qs") * jnp.tile(cos_s, (1, 4)) + seg("qs_rot") * jnp.tile(sin_s, (1, 4))).astype(BF16)
    ks_ref[0] = (seg("ks") * cos_s + seg("ks_rot") * sin_s).astype(BF16)
    vs_ref[0] = seg("vs").astype(BF16)

    g0, _ = _SEG["gl"]
    for p in range(6):
        gate_ref[0, :, p * 512:(p + 1) * 512] = _sigmoid(
            _dot(hb, w1_ref[:, g0 + p * 512:g0 + (p + 1) * 512])).astype(BF16)


def _input_stage(x_all, modtab, g_mix, w1, g_q, wq, g_kv, wkn, wv, tabm, tabs):
    b, p, d = x_all.shape
    tm = TOKEN_TILE
    nj = p // tm
    tok = lambda w: pl.BlockSpec((1, tm, w), lambda bi, j: (bi, j, 0))
    full = lambda a: pl.BlockSpec(a.shape, lambda bi, j: (0,) * a.ndim)
    outs = [(1024, BF16), (1024, BF16), (1024, BF16), (512, F32), (512, BF16), (128, BF16), (128, BF16),
            (3072, BF16)]
    return pl.pallas_call(
        _in_kernel,
        grid=(b, nj),
        in_specs=[tok(d),
                  pl.BlockSpec((1, 1, 8, d), lambda bi, j: (bi, j // (nj - 1), 0, 0)),
                  full(g_mix), full(w1), full(g_q), full(wq), full(g_kv), full(wkn), full(wv),
                  pl.BlockSpec((tm, 256), lambda bi, j: (j, 0)),
                  pl.BlockSpec((tm, 256), lambda bi, j: (j, 0))],
        out_specs=[tok(w) for w, _ in outs],
        out_shape=[jax.ShapeDtypeStruct((b, p, w), dt) for w, dt in outs],
        compiler_params=_cparams(("arbitrary", "arbitrary")),
        name="input_stage",
    )(x_all, modtab, g_mix, w1, g_q, wq, g_kv, wkn, wv, tabm, tabs)


def _mla_kernel(q_ref, k_ref, v_ref, *rest, tk, n_main, tail_rows):
    y_ref, m_ref, acc_ref = rest[-3:]
    hw = MLA_HEAD_PAD
    m_ref[...] = jnp.full(m_ref.shape, MASK_VALUE, F32)
    acc_ref[...] = jnp.zeros(acc_ref.shape, F32)

    def attend(r0, rows):
        for h in range(MLA_HEADS):
            q = q_ref[0, :, h * hw:(h + 1) * hw]
            k = k_ref[0, pl.ds(r0, rows), h * hw:(h + 1) * hw]
            v = v_ref[0, pl.ds(r0, rows), h * hw:(h + 1) * hw]
            s = _dot_nt(q, k)
            m_prev = m_ref[h]
            m_new = jnp.maximum(m_prev, jnp.max(s, axis=1, keepdims=True))
            m_ref[h] = m_new
            pr = jnp.exp2(s - jnp.tile(m_new, (1, rows // hw)))
            acc_ref[h] = jnp.exp2(m_prev - m_new) * acc_ref[h] + _dot(pr.astype(BF16), v)

    if n_main:
        def step(c, carry):
            attend(pl.multiple_of(c * tk, tk), tk)
            return carry

        lax.fori_loop(0, n_main, step, 0)
    if tail_rows:
        attend(n_main * tk, tail_rows)

    first = lax.broadcasted_iota(I32, (q_ref.shape[1], hw), 1) < MLA_V
    for hp in range(MLA_HEADS // 2):
        o = []
        for h in (2 * hp, 2 * hp + 1):
            acc = acc_ref[h]
            o.append(acc / acc[:, MLA_V:MLA_V + 1])
        y_ref[0, :, hp * hw:(hp + 1) * hw] = jnp.where(first, o[0], pltpu.roll(o[1], MLA_V, 1)).astype(BF16)


def _mla_attention(q, k, v, n_ctx):
    b, p, width = q.shape
    s_len = p - n_ctx
    tq = MLA_Q_TILE
    tk = MLA_KEY_CHUNK
    assert s_len % tq == 0 and s_len % tk == 0 and s_len % n_ctx == 0
    out_w = MLA_HEADS * MLA_V
    state = lambda rows: pltpu.VMEM((MLA_HEADS, rows, MLA_HEAD_PAD), F32)
    y = pl.pallas_call(
        functools.partial(_mla_kernel, tk=tk, n_main=s_len // tk, tail_rows=n_ctx),
        grid=(b, s_len // tq),
        in_specs=[pl.BlockSpec((1, tq, width), lambda bi, j: (bi, j, 0)),
                  pl.BlockSpec((1, p, width), lambda bi, j: (bi, 0, 0), pipeline_mode=pl.Buffered(1)),
                  pl.BlockSpec((1, p, width), lambda bi, j: (bi, 0, 0), pipeline_mode=pl.Buffered(1))],
        out_specs=pl.BlockSpec((1, tq, out_w), lambda bi, j: (bi, j, 0)),
        out_shape=jax.ShapeDtypeStruct((b, p, out_w), BF16),
        scratch_shapes=[state(tq), state(tq)],
        compiler_params=_cparams(("arbitrary", "arbitrary")),
        name="mla_attention",
    )(q, k, v)
    cblk = s_len // n_ctx
    ctx_rows = lambda w: pl.BlockSpec((1, n_ctx, w), lambda bi: (bi, cblk, 0))
    return pl.pallas_call(
        functools.partial(_mla_kernel, tk=tk, n_main=0, tail_rows=n_ctx),
        grid=(b,),
        in_specs=[ctx_rows(width), ctx_rows(width), ctx_rows(width), pl.BlockSpec(memory_space=pl.ANY)],
        out_specs=ctx_rows(out_w),
        out_shape=jax.ShapeDtypeStruct((b, p, out_w), BF16),
        input_output_aliases={3: 0},
        scratch_shapes=[state(n_ctx), state(n_ctx)],
        compiler_params=_cparams(("arbitrary",)),
        name="mla_attention_ctx",
    )(q, k, v, y)


def _swa_kernel(sink_ref, q_ref, k_ref, v_ref, y_ref, *, n_ctx):
    j = pl.program_id(1)
    p_len = k_ref.shape[1]
    s_len = p_len - n_ctx
    band = 3 * SWA_BLOCK
    n_lat_tiles = s_len // SWA_BLOCK
    is_lat = j < n_lat_tiles
    n = jnp.minimum(j, n_lat_tiles - 1)
    ws = jnp.clip((n - 1) * SWA_BLOCK, 0, s_len - band)
    kstart = pl.multiple_of(ws, SWA_BLOCK)
    qpos = n * SWA_BLOCK + lax.broadcasted_iota(I32, (SWA_BLOCK, band), 0)
    kpos = ws + lax.broadcasted_iota(I32, (SWA_BLOCK, band), 1)
    valid = (jnp.abs(qpos - kpos) <= SWA_WINDOW) & is_lat
    group = SWA_Q_HEADS // SWA_KV_HEADS
    for hk in range(SWA_KV_HEADS):
        cols = slice(hk * SWA_HEAD_DIM, (hk + 1) * SWA_HEAD_DIM)
        kc = k_ref[0, s_len:p_len, cols]
        vc = v_ref[0, s_len:p_len, cols]
        kb = k_ref[0, pl.ds(kstart, band), cols]
        vb = v_ref[0, pl.ds(kstart, band), cols]
        for g in range(group):
            h = hk * group + g
            q = q_ref[0, :, h * SWA_HEAD_DIM:(h + 1) * SWA_HEAD_DIM]
            s_c = _dot_nt(q, kc)
            s_b = jnp.where(valid, _dot_nt(q, kb), MASK_VALUE)
            sink = sink_ref[h]
            m = jnp.maximum(jnp.maximum(jnp.max(s_c, axis=1, keepdims=True), jnp.max(s_b, axis=1, keepdims=True)),
                            sink)
            p_c = jnp.exp(s_c - m)
            p_b = jnp.exp(s_b - m)
            l = jnp.sum(p_c, axis=1, keepdims=True) + jnp.sum(p_b, axis=1, keepdims=True) + jnp.exp(sink - m)
            o = _dot(p_c.astype(BF16), vc) + _dot(p_b.astype(BF16), vb)
            y_ref[0, :, h * SWA_HEAD_DIM:(h + 1) * SWA_HEAD_DIM] = (o / l).astype(BF16)


def _swa_attention(sink, qs, ks, vs, n_ctx):
    b, p, _ = qs.shape
    tq = SWA_BLOCK
    return pl.pallas_call(
        functools.partial(_swa_kernel, n_ctx=n_ctx),
        grid=(b, p // tq),
        in_specs=[pl.BlockSpec(memory_space=pltpu.SMEM),
                  pl.BlockSpec((1, tq, SWA_WIDTH), lambda bi, j: (bi, j, 0)),
                  pl.BlockSpec((1, p, SWA_KV_WIDTH), lambda bi, j: (bi, 0, 0)),
                  pl.BlockSpec((1, p, SWA_KV_WIDTH), lambda bi, j: (bi, 0, 0))],
        out_specs=pl.BlockSpec((1, tq, SWA_WIDTH), lambda bi, j: (bi, j, 0)),
        out_shape=jax.ShapeDtypeStruct((b, p, SWA_WIDTH), BF16),
        compiler_params=_cparams(("arbitrary", "arbitrary")),
        name="swa_attention",
    )(sink, qs, ks, vs)


def _pool_kernel(prev_ref, cur_ref, next_ref, w_ref, scale_ref, y_ref, ext_ref, *, n_ctx):
    j = pl.program_id(1)
    tm = cur_ref.shape[1]
    nj = pl.num_programs(1)
    s_len = (nj - 1) * tm
    is_ctx = j == nj - 1
    has_prev = (j >= 1) & (j < nj - 1)
    has_next = j < nj - 2
    ext_ref[0:POOL_HALO, :] = jnp.where(has_prev, prev_ref[0], 0.0)
    ext_ref[POOL_HALO:POOL_HALO + tm, :] = cur_ref[0]
    ext_ref[POOL_HALO + tm:POOL_HALO + tm + POOL_HALO, :] = jnp.where(has_next, next_ref[0], 0.0)
    t = lax.broadcasted_iota(I32, (tm, 1), 0)
    pos = jnp.where(is_ctx, t, j * tm + t)
    seg_len = jnp.where(is_ctx, n_ctx, s_len)
    for g, w in enumerate(POOL_WINDOWS):
        cols = slice(g * POOL_GROUP_DIM, (g + 1) * POOL_GROUP_DIM)
        acc = jnp.zeros((tm, POOL_GROUP_DIM), F32)
        for off in range(-(w // 2), w - w // 2):
            acc = acc + ext_ref[POOL_HALO + off:POOL_HALO + off + tm, cols]
        lo = jnp.maximum(pos - w // 2, 0)
        hi = jnp.minimum(pos + w - w // 2, seg_len)
        cnt = (hi - lo).astype(F32)
        pooled = acc / cnt - cur_ref[0, :, cols]
        y_ref[0, :, cols] = (_dot(pooled.astype(BF16), w_ref[g]) * scale_ref[:, cols]).astype(BF16)


def _pool_stage(u, w_pool, pool_scale, n_ctx):
    b, p, width = u.shape
    tm = TOKEN_TILE
    hb = tm // POOL_HALO
    n_halo_blocks = p // POOL_HALO
    return pl.pallas_call(
        functools.partial(_pool_kernel, n_ctx=n_ctx),
        grid=(b, p // tm),
        in_specs=[pl.BlockSpec((1, POOL_HALO, width), lambda bi, j: (bi, jnp.maximum(j * hb - 1, 0), 0)),
                  pl.BlockSpec((1, tm, width), lambda bi, j: (bi, j, 0)),
                  pl.BlockSpec((1, POOL_HALO, width),
                               lambda bi, j: (bi, jnp.minimum((j + 1) * hb, n_halo_blocks - 1), 0)),
                  pl.BlockSpec(w_pool.shape, lambda bi, j: (0, 0, 0)),
                  pl.BlockSpec(pool_scale.shape, lambda bi, j: (0, 0))],
        out_specs=pl.BlockSpec((1, tm, width), lambda bi, j: (bi, j, 0)),
        out_shape=jax.ShapeDtypeStruct((b, p, width), BF16),
        scratch_shapes=[pltpu.VMEM((tm + 2 * POOL_HALO, width), F32)],
        compiler_params=_cparams(("arbitrary", "arbitrary")),
        name="pool_stage",
    )(u, u, u, w_pool, pool_scale)


def _merge_kernel(ym_ref, yp_ref, ys_ref, gate_ref, x_ref, mod_ref, g_ref, wbm_ref, wbp_ref, wbs_ref, wout_ref,
                  wsg_ref, wsu_ref, wsd_ref, wrt_ref, xs_ref, fp_ref, lg_ref):
    d = x_ref.shape[2]
    mod = mod_ref[0, 0]
    gate = gate_ref[0]
    merged = (gate[:, 0:d].astype(F32) * _dot(ym_ref[0], wbm_ref[...])
              + gate[:, d:2 * d].astype(F32) * _dot(yp_ref[0], wbp_ref[...])
              + gate[:, 2 * d:3 * d].astype(F32) * _dot(ys_ref[0], wbs_ref[...]))
    x_mid = x_ref[0] + mod[2:3] * _dot(merged.astype(BF16), wout_ref[...])
    f = _rms(x_mid, g_ref[...]) * (1.0 + mod[4:5]) + mod[3:4]
    fb = f.astype(BF16)
    gsh = _dot(fb, wsg_ref[...])
    shared = _dot((gsh * _sigmoid(gsh) * _dot(fb, wsu_ref[...])).astype(BF16), wsd_ref[...])
    xs_ref[0] = x_mid + mod[5:6] * shared
    fp_ref[0] = _pack_bf16_pair(f)
    lg_ref[...] = _dot_nt(wrt_ref[...], fb)


def _merge_stage(ym, yp, ys, gates, x_all, modtab, g_ffn, wbm, wbp, wbs, wout, wsg, wsu, wsd, wrt):
    b, p, d = x_all.shape
    tm = TOKEN_TILE
    nj = p // tm
    tok = lambda w: pl.BlockSpec((1, tm, w), lambda bi, j: (bi, j, 0))
    full = lambda a: pl.BlockSpec(a.shape, lambda bi, j: (0,) * a.ndim)
    return pl.pallas_call(
        _merge_kernel,
        grid=(b, nj),
        in_specs=[tok(ym.shape[2]), tok(yp.shape[2]), tok(ys.shape[2]), tok(gates.shape[2]), tok(d),
                  pl.BlockSpec((1, 1, 8, d), lambda bi, j: (bi, j // (nj - 1), 0, 0)),
                  full(g_ffn), full(wbm), full(wbp), full(wbs), full(wout), full(wsg), full(wsu), full(wsd),
                  full(wrt)],
        out_specs=[tok(d), tok(d // 2), pl.BlockSpec((N_EXPERTS, tm), lambda bi, j: (0, bi * nj + j))],
        out_shape=[jax.ShapeDtypeStruct((b, p, d), F32),
                   jax.ShapeDtypeStruct((b, p, d // 2), U32),
                   jax.ShapeDtypeStruct((N_EXPERTS, b * p), F32)],
        compiler_params=_cparams(("arbitrary", "arbitrary")),
        name="merge_stage",
    )(ym, yp, ys, gates, x_all, modtab, g_ffn, wbm, wbp, wbs, wout, wsg, wsu, wsd, wrt)


def _route_kernel(lg_ref, bias_ref, dest_ref, wts_ref, seg_ref):
    tm = lg_ref.shape[1]
    ne = N_EXPERTS
    neg_inf = -jnp.inf
    scores = _sigmoid(lg_ref[...])
    sel = scores + bias_ref[...]
    iota_g = lax.broadcasted_iota(I32, (EXPERTS_PER_GROUP, tm), 0)
    gscore = []
    for g in range(N_GROUPS):
        sg = sel[g * EXPERTS_PER_GROUP:(g + 1) * EXPERTS_PER_GROUP]
        m1 = jnp.max(sg, axis=0, keepdims=True)
        i1 = jnp.min(jnp.where(sg == m1, iota_g, EXPERTS_PER_GROUP), axis=0, keepdims=True)
        m2 = jnp.max(jnp.where(iota_g == i1, neg_inf, sg), axis=0, keepdims=True)
        gscore.append(m1 + m2)
    rows = []
    for g in range(N_GROUPS):
        rank = jnp.zeros((1, tm), I32)
        for g2 in range(N_GROUPS):
            if g2 == g:
                continue
            beats = (gscore[g2] >= gscore[g]) if g2 < g else (gscore[g2] > gscore[g])
            rank = rank + beats.astype(I32)
        rows.append(jnp.where(rank < TOPK_GROUPS, sel[g * EXPERTS_PER_GROUP:(g + 1) * EXPERTS_PER_GROUP], neg_inf))
    cur = jnp.concatenate(rows, axis=0)
    iota_e = lax.broadcasted_iota(I32, (ne, tm), 0)
    picks = []
    member = jnp.zeros((ne, tm), F32)
    for _ in range(TOP_K):
        m = jnp.max(cur, axis=0, keepdims=True)
        idx = jnp.min(jnp.where(cur == m, iota_e, ne), axis=0, keepdims=True)
        hit = iota_e == idx
        picks.append(hit)
        member = member + hit.astype(F32)
        cur = jnp.where(hit, neg_inf, cur)
    earlier = (lax.broadcasted_iota(I32, (tm, tm), 0) < lax.broadcasted_iota(I32, (tm, tm), 1)).astype(BF16)
    pos = _dot(member.astype(BF16), earlier)
    cnt_col = jnp.sum(member, axis=1, keepdims=True)
    blocks_col = jnp.floor((cnt_col + (SEG_ALIGN - 1)) * (1.0 / SEG_ALIGN))
    lower = (lax.broadcasted_iota(I32, (ne, ne), 1) < lax.broadcasted_iota(I32, (ne, ne), 0)).astype(BF16)
    off_col = _dot(lower, jnp.broadcast_to(blocks_col, (ne, 128)).astype(BF16))[:, 0:1] * SEG_ALIGN
    base = off_col + pos
    w_rows = [jnp.sum(jnp.where(hit, scores, 0.0), axis=0, keepdims=True) for hit in picks]
    denom = w_rows[0]
    for w in w_rows[1:]:
        denom = denom + w
    for k, hit in enumerate(picks):
        dest_ref[0, k:k + 1, :] = jnp.sum(jnp.where(hit, base, 0.0), axis=0, keepdims=True).astype(I32)
        wts_ref[0, k:k + 1, :] = w_rows[k] / denom * ROUTED_SCALE
    member_pad = jnp.concatenate([member, jnp.zeros((128 - ne, tm), F32)], axis=0).astype(BF16)
    cnt_row = _dot_nt(jnp.ones((8, tm), BF16), member_pad)
    blocks_row = jnp.floor((cnt_row + (SEG_ALIGN - 1)) * (1.0 / SEG_ALIGN))
    before = (lax.broadcasted_iota(I32, (128, 128), 0) < lax.broadcasted_iota(I32, (128, 128), 1)).astype(BF16)
    off_row = _dot(blocks_row.astype(BF16), before) * SEG_ALIGN
    r = lax.broadcasted_iota(I32, (8, 128), 0)
    seg_ref[0] = jnp.where(r == 0, off_row, jnp.where(r == 1, cnt_row, 0.0)).astype(I32)


def _route_stage(lg_t, bias_col):
    ne, t_all = lg_t.shape
    tm = MOE_TILE
    nt = t_all // tm
    return pl.pallas_call(
        _route_kernel,
        grid=(nt,),
        in_specs=[pl.BlockSpec((ne, tm), lambda i: (0, i)),
                  pl.BlockSpec((ne, 1), lambda i: (0, 0))],
        out_specs=[pl.BlockSpec((1, TOP_K, tm), lambda i: (i, 0, 0)),
                   pl.BlockSpec((1, TOP_K, tm), lambda i: (i, 0, 0)),
                   pl.BlockSpec((1, 8, 128), lambda i: (i, 0, 0))],
        out_shape=[jax.ShapeDtypeStruct((nt, TOP_K, tm), I32),
                   jax.ShapeDtypeStruct((nt, TOP_K, tm), F32),
                   jax.ShapeDtypeStruct((nt, 8, 128), I32)],
        compiler_params=_cparams(("arbitrary",)),
        name="route_stage",
    )(lg_t, bias_col)


def _moe_row_stride(tm):
    cap = TOP_K * tm + N_EXPERTS * SEG_ALIGN + MOE_CHUNK
    blocks = cap // 8 + 1
    return 8 * (blocks + 1 - blocks % 2)


def _moe_kernel(dest_ref, wts_ref, seg_ref, fp_ref, xs_ref, gt_ref, wg_ref, wu_ref, wd_ref, out_ref, buf_ref,
                *, srow, sub, n_sub_per_batch):
    i = pl.program_id(0)
    e = pl.program_id(1)
    tm = dest_ref.shape[2]
    ch = MOE_CHUNK

    @pl.when(e == 0)
    def _group_rows():
        buf_ref[...] = jnp.zeros_like(buf_ref)

        def body(t8, carry):
            for u in range(8):
                t = t8 * 8 + u
                slab = fp_ref[pl.ds(pl.multiple_of(t * 4, 4), 4), :]
                for k in range(TOP_K):
                    buf_ref[pl.ds(dest_ref[0, k, t], 4, stride=srow), :] = slab
            return carry

        lax.fori_loop(0, tm // 8, body, 0)

    n_rows = seg_ref[0, 1, e]
    seg_off = seg_ref[0, 0, e]

    def chunk(c, carry):
        r0 = seg_off + c * ch
        words = [buf_ref[pl.ds(pl.multiple_of(q * srow + r0, 8), ch), :] for q in range(4)]
        w = jnp.concatenate(words, axis=1)
        xb = jnp.concatenate([_unpack_lo(w), _unpack_hi(w)], axis=1).astype(BF16)
        g = _dot(xb, wg_ref[0])
        hmid = g * _sigmoid(g) * _dot(xb, wu_ref[0])
        y = _dot(hmid.astype(BF16), wd_ref[0])
        packed = _pack_bf16_pair(y)
        keep_new = lax.broadcasted_iota(I32, (ch, 1), 0) < (n_rows - c * ch)
        for q in range(4):
            buf_ref[pl.ds(pl.multiple_of(q * srow + r0, 8), ch), :] = jnp.where(
                keep_new, packed[:, q * 128:(q + 1) * 128], words[q])
        return carry

    lax.fori_loop(0, (n_rows + ch - 1) // ch, chunk, 0)

    @pl.when(e == pl.num_programs(1) - 1)
    def _combine():
        def body(t8, carry):
            blk = (i * tm + t8 * 8) // sub
            bi = blk // n_sub_per_batch
            row = bi * 2 + (blk - bi * n_sub_per_batch) // (n_sub_per_batch - 1)
            gate = gt_ref[row]
            for u in range(8):
                t = t8 * 8 + u
                acc_lo = jnp.zeros((4, 128), F32)
                acc_hi = jnp.zeros((4, 128), F32)
                for k in range(TOP_K):
                    words = buf_ref[pl.ds(dest_ref[0, k, t], 4, stride=srow), :]
                    wk = wts_ref[0, k, t]
                    acc_lo = acc_lo + wk * _unpack_lo(words)
                    acc_hi = acc_hi + wk * _unpack_hi(words)
                r_lo = pl.multiple_of(t * 8, 8)
                r_hi = pl.multiple_of(t * 8 + 4, 4)
                out_ref[pl.ds(r_lo, 4), :] = xs_ref[pl.ds(r_lo, 4), :] + gate[0:4] * acc_lo
                out_ref[pl.ds(r_hi, 4), :] = xs_ref[pl.ds(r_hi, 4), :] + gate[4:8] * acc_hi
            return carry

        lax.fori_loop(0, tm // 8, body, 0)


def _moe_stage(dest, wts, seg, fp4, xs8, gt2, wg, wu, wd, n_sub_per_batch):
    nt, _, tm = dest.shape
    ne = wg.shape[0]
    srow = _moe_row_stride(tm)
    smem = lambda shape: pl.BlockSpec(shape, lambda i, e: (i, 0, 0), memory_space=pltpu.SMEM)
    return pl.pallas_call(
        functools.partial(_moe_kernel, srow=srow, sub=TOKEN_TILE, n_sub_per_batch=n_sub_per_batch),
        grid=(nt, ne),
        in_specs=[smem((1, TOP_K, tm)), smem((1, TOP_K, tm)), smem((1, 8, 128)),
                  pl.BlockSpec((tm * 4, 128), lambda i, e: (i, 0)),
                  pl.BlockSpec((tm * 8, 128), lambda i, e: (i, 0)),
                  pl.BlockSpec(gt2.shape, lambda i, e: (0, 0, 0)),
                  pl.BlockSpec((1,) + wg.shape[1:], lambda i, e: (e, 0, 0)),
                  pl.BlockSpec((1,) + wu.shape[1:], lambda i, e: (e, 0, 0)),
                  pl.BlockSpec((1,) + wd.shape[1:], lambda i, e: (e, 0, 0))],
        out_specs=pl.BlockSpec((tm * 8, 128), lambda i, e: (i, 0)),
        out_shape=jax.ShapeDtypeStruct(xs8.shape, F32),
        scratch_shapes=[pltpu.VMEM((4 * srow, 128), U32)],
        compiler_params=_cparams(("arbitrary", "arbitrary")),
        name="moe_stage",
    )(dest, wts, seg, fp4, xs8, gt2, wg, wu, wd)


def _final_kernel(x_ref, g_ref, o_ref):
    o_ref[0] = _rms(x_ref[0], g_ref[...])


def _final_norm(x_all, g_final, n_ctx):
    b, p, d = x_all.shape
    tm = TOKEN_TILE
    return pl.pallas_call(
        _final_kernel,
        grid=(b, (p - n_ctx) // tm),
        in_specs=[pl.BlockSpec((1, tm, d), lambda bi, j: (bi, j, 0)),
                  pl.BlockSpec((1, d), lambda bi, j: (0, 0))],
        out_specs=pl.BlockSpec((1, tm, d), lambda bi, j: (bi, j, 0)),
        out_shape=jax.ShapeDtypeStruct((b, p - n_ctx, d), F32),
        compiler_params=_cparams(("arbitrary", "arbitrary")),
        name="final_norm",
    )(x_all, g_final)


def _rot_cols(w, n_heads, rot_dim):
    kdim = w.shape[0]
    w4 = w.reshape(kdim, n_heads, 4, rot_dim // 4)
    rot = jnp.stack([-w4[:, :, 1], w4[:, :, 0], -w4[:, :, 3], w4[:, :, 2]], axis=2)
    return rot.reshape(kdim, n_heads * rot_dim)


def _rope_pattern(s_len, n_ctx, rot_dim):
    t = jnp.arange(s_len)
    row = (t // GRID_W).astype(F32)
    col = (t % GRID_W).astype(F32)
    n_freq = rot_dim // 4
    inv_freq = ROPE_BASE ** (-jnp.arange(n_freq, dtype=F32) / n_freq)
    ang_r = row[:, None] * inv_freq[None, :]
    ang_c = col[:, None] * inv_freq[None, :]
    cos = jnp.concatenate([jnp.cos(ang_r), jnp.cos(ang_r), jnp.cos(ang_c), jnp.cos(ang_c)], axis=1)
    sin = jnp.concatenate([jnp.sin(ang_r), jnp.sin(ang_r), jnp.sin(ang_c), jnp.sin(ang_c)], axis=1)
    cos = jnp.concatenate([cos, jnp.ones((n_ctx, rot_dim), F32)], axis=0)
    sin = jnp.concatenate([sin, jnp.zeros((n_ctx, rot_dim), F32)], axis=0)
    return cos, sin


def _layer_weights(w_in, w_uq, w_ukv):
    d = w_in.shape[0]
    offs = [0]
    for w in (MLA_Q_LORA, MLA_KV_LORA, MLA_ROPE, POOL_WIDTH, SWA_WIDTH, SWA_KV_WIDTH, SWA_KV_WIDTH, 3 * d):
        offs.append(offs[-1] + w)
    cq, ckv, kr, u, qs, ks, vs, gl = (w_in[:, offs[i]:offs[i + 1]] for i in range(8))
    qs = qs * SWA_SCALE
    zeros = lambda n: jnp.zeros((d, n), F32)
    kr_slot = jnp.concatenate([zeros(MLA_NOPE), kr, zeros(MLA_HEAD_PAD - MLA_NOPE - MLA_ROPE)], axis=1)
    kr_rot_slot = jnp.concatenate([zeros(MLA_NOPE), _rot_cols(kr, 1, MLA_ROPE),
                                   zeros(MLA_HEAD_PAD - MLA_NOPE - MLA_ROPE)], axis=1)
    w1 = jnp.concatenate([cq, ckv, u, qs, _rot_cols(qs, SWA_Q_HEADS, SWA_HEAD_DIM), ks,
                          _rot_cols(ks, SWA_KV_HEADS, SWA_HEAD_DIM), vs, kr_slot, kr_rot_slot, gl],
                         axis=1).astype(BF16)
    lq = w_uq.shape[0]
    wq3 = (w_uq * (MLA_SCALE * LOG2_E)).reshape(lq, MLA_HEADS, MLA_NOPE + MLA_ROPE)
    nope, rope = wq3[:, :, :MLA_NOPE], wq3[:, :, MLA_NOPE:]
    rope_rot = _rot_cols(rope.reshape(lq, MLA_HEADS * MLA_ROPE), MLA_HEADS, MLA_ROPE).reshape(lq, MLA_HEADS, MLA_ROPE)
    pad = jnp.zeros((lq, MLA_HEADS, MLA_HEAD_PAD - MLA_NOPE - MLA_ROPE), F32)
    wq_a = jnp.concatenate([nope, rope, pad], axis=2).reshape(lq, MLA_HEADS * MLA_HEAD_PAD)
    wq_b = jnp.concatenate([jnp.zeros_like(nope), rope_rot, pad], axis=2).reshape(lq, MLA_HEADS * MLA_HEAD_PAD)
    wq = jnp.concatenate([wq_a, wq_b], axis=1).astype(BF16)
    lkv = w_ukv.shape[0]
    wkv3 = w_ukv.reshape(lkv, MLA_HEADS, MLA_NOPE + MLA_V)
    wkn = jnp.concatenate([wkv3[:, :, :MLA_NOPE], jnp.zeros((lkv, MLA_HEADS, MLA_HEAD_PAD - MLA_NOPE), F32)],
                          axis=2).reshape(lkv, MLA_HEADS * MLA_HEAD_PAD).astype(BF16)
    wv = jnp.concatenate([wkv3[:, :, MLA_NOPE:], jnp.zeros((lkv, MLA_HEADS, MLA_HEAD_PAD - MLA_V), F32)],
                         axis=2).reshape(lkv, MLA_HEADS * MLA_HEAD_PAD).astype(BF16)
    return w1, wq, wkn, wv


def kernel(x, c, ctx, c_ctx, w_mod, b_mod, g_mix, g_ffn, w_in, g_mla_q, g_mla_kv, w_mla_uq, w_mla_ukv, w_pool,
           pool_scale, swa_sink, w_br_mla, w_br_pool, w_br_swa, w_out, w_router, router_bias, w_exp_gate,
           w_exp_up, w_exp_down, w_sh_gate, w_sh_up, w_sh_down, g_final):
    b, s_len, d = x.shape
    n_ctx = ctx.shape[1]
    n_layers = w_mod.shape[0]
    p = n_ctx + s_len
    assert n_ctx == TOKEN_TILE and s_len % TOKEN_TILE == 0 and (b * p) % MOE_TILE == 0 and b + 1 <= 8
    assert FUSED_IN_WIDTH == w_in.shape[2] + SWA_WIDTH + SWA_KV_WIDTH + 2 * MLA_HEAD_PAD - MLA_ROPE

    cm, sm = _rope_pattern(s_len, n_ctx, MLA_ROPE)
    tail = jnp.zeros((p, MLA_HEAD_PAD - MLA_NOPE - MLA_ROPE), F32)
    tabm = jnp.concatenate([jnp.ones((p, MLA_NOPE), F32), cm, tail, jnp.zeros((p, MLA_NOPE), F32), sm, tail], axis=1)
    cs, ss = _rope_pattern(s_len, n_ctx, SWA_HEAD_DIM)
    tabs = jnp.concatenate([cs, cs, ss, ss], axis=1)

    cvec = jnp.concatenate([c, c_ctx[None, :], jnp.zeros((8 - b - 1, d), F32)], axis=0)
    mod_all = _modulation(cvec, w_mod, b_mod).reshape(n_layers, 8, N_MOD, d)

    x_all = jnp.concatenate([x, ctx], axis=1)
    for i in range(n_layers):
        lat = mod_all[i, :b]
        ctx_rows = jnp.broadcast_to(mod_all[i, b][None], (b, N_MOD, d))
        modtab = jnp.pad(jnp.stack([lat, ctx_rows], axis=1), ((0, 0), (0, 0), (0, 8 - N_MOD), (0, 0)))
        w1, wq, wkn, wv = _layer_weights(w_in[i], w_mla_uq[i], w_mla_ukv[i])
        q, k, v, u, qs, ks, vs, gates = _input_stage(
            x_all, modtab, g_mix[i][None], w1, g_mla_q[i][None], wq, g_mla_kv[i][None], wkn, wv, tabm, tabs)
        y_mla = _mla_attention(q, k, v, n_ctx)
        y_swa = _swa_attention(swa_sink[i], qs, ks, vs, n_ctx)
        y_pool = _pool_stage(u, w_pool[i].astype(BF16), pool_scale[i][None], n_ctx)
        xs, fp, lg_t = _merge_stage(
            y_mla, y_pool, y_swa, gates, x_all, modtab, g_ffn[i][None], w_br_mla[i].astype(BF16),
            w_br_pool[i].astype(BF16), w_br_swa[i].astype(BF16), w_out[i].astype(BF16),
            w_sh_gate[i].astype(BF16), w_sh_up[i].astype(BF16), w_sh_down[i].astype(BF16),
            w_router[i].T.astype(BF16))
        dest, wts, seg = _route_stage(lg_t, router_bias[i][:, None])
        gt2 = modtab[:, :, 5, :].reshape(b * 2, 8, d // 8)
        x_new = _moe_stage(dest, wts, seg, fp.reshape(b * p * 4, 128), xs.reshape(b * p * 8, 128), gt2,
                           w_exp_gate[i].astype(BF16), w_exp_up[i].astype(BF16), w_exp_down[i].astype(BF16),
                           p // TOKEN_TILE)
        x_all = x_new.reshape(b, p, d)
    return _final_norm(x_all, g_final[None], n_ctx)
```

```python
import functools

import jax
import jax.numpy as jnp
from jax import lax
from jax.experimental import pallas as pl
from jax.experimental.pallas import tpu as pltpu

F32 = jnp.float32
BF16 = jnp.bfloat16
U32 = jnp.uint32
I32 = jnp.int32

NORM_EPS = 1e-6
ROPE_BASE = 10000.0
GRID_W = 64
N_MOD = 6

MLA_HEADS = 8
MLA_Q_LORA = 384
MLA_KV_LORA = 256
MLA_NOPE = 64
MLA_ROPE = 32
MLA_V = 64
MLA_SCALE = (MLA_NOPE + MLA_ROPE) ** -0.5
MLA_HEAD_PAD = 128
LOG2_E = 1.4426950408889634

POOL_WINDOWS = (2, 4, 8, 16)
POOL_GROUP_DIM = 128
POOL_WIDTH = 512
POOL_HALO = 8

SWA_Q_HEADS = 8
SWA_KV_HEADS = 2
SWA_HEAD_DIM = 64
SWA_WINDOW = 128
SWA_BLOCK = 128
SWA_SCALE = SWA_HEAD_DIM ** -0.5
SWA_WIDTH = SWA_Q_HEADS * SWA_HEAD_DIM
SWA_KV_WIDTH = SWA_KV_HEADS * SWA_HEAD_DIM

N_EXPERTS = 64
TOP_K = 8
N_GROUPS = 8
TOPK_GROUPS = 4
EXPERTS_PER_GROUP = 8
D_EXPERT = 256
ROUTED_SCALE = 2.5

TOKEN_TILE = 256
MLA_Q_TILE = 1024
MLA_KEY_CHUNK = 256
MOE_TILE = 1024
MOE_CHUNK = 256
MOE_EXPERTS_PER_STEP = 4
SEG_ALIGN = 8
MASK_VALUE = -1e30
HI16 = 0xFFFF0000

VMEM_LIMIT = 56 * 1024 * 1024

_SEG_WIDTHS = (("cq", 384), ("ckv", 256), ("u", 512), ("qs", 512), ("qs_rot", 512), ("ks", 256),
               ("ks_rot", 256), ("vs", 256), ("kr", 128), ("kr_rot", 128), ("gl", 3072))
_SEG = {}
_o = 0
for _n, _w in _SEG_WIDTHS:
    _SEG[_n] = (_o, _o + _w)
    _o += _w
FUSED_IN_WIDTH = _o


def _cparams(sem):
    return pltpu.CompilerParams(dimension_semantics=sem, vmem_limit_bytes=VMEM_LIMIT)


def _dot(a, b):
    return jnp.dot(a, b, preferred_element_type=F32)


def _dot_nt(a, b):
    return lax.dot_general(a, b, (((1,), (1,)), ((), ())), preferred_element_type=F32)


def _sigmoid(x):
    return 1.0 / (1.0 + jnp.exp(-x))


def _rms(x, g):
    return x * lax.rsqrt(jnp.mean(x * x, axis=-1, keepdims=True) + NORM_EPS) * g


def _pack_bf16_pair(v):
    n = v.shape[1] // 2
    bits = pltpu.bitcast(v.astype(BF16).astype(F32), U32)
    return (bits[:, :n] >> 16) | (bits[:, n:] & jnp.uint32(HI16))


def _unpack_lo(w):
    return pltpu.bitcast(w << 16, F32)


def _unpack_hi(w):
    return pltpu.bitcast(w & jnp.uint32(HI16), F32)


def _mod_kernel(c_ref, w_ref, b_ref, o_ref):
    c = c_ref[...]
    a = (c * _sigmoid(c)).astype(BF16)
    o_ref[0] = _dot(a, w_ref[0].astype(BF16)) + b_ref[0]


def _modulation(cvec, w_mod, b_mod):
    n_layers, d, width = w_mod.shape
    tn = width // 4
    return pl.pallas_call(
        _mod_kernel,
        grid=(n_layers, width // tn),
        in_specs=[pl.BlockSpec((8, d), lambda l, n: (0, 0)),
                  pl.BlockSpec((1, d, tn), lambda l, n: (l, 0, n)),
                  pl.BlockSpec((1, 1, tn), lambda l, n: (l, 0, n))],
        out_specs=pl.BlockSpec((1, 8, tn), lambda l, n: (l, 0, n)),
        out_shape=jax.ShapeDtypeStruct((n_layers, 8, width), F32),
        compiler_params=_cparams(("arbitrary", "arbitrary")),
        name="modulation",
    )(cvec, w_mod, b_mod.reshape(n_layers, 1, width))


def _in_kernel(x_ref, mod_ref, g_ref, w1_ref, gq_ref, wq_ref, gkv_ref, wkn_ref, wv_ref, tabm_ref, tabs_ref,
               q_ref, k_ref, v_ref, u_ref, qs_ref, ks_ref, vs_ref, gate_ref):
    x = x_ref[0]
    mod = mod_ref[0, 0]
    h = _rms(x, g_ref[...]) * (1.0 + mod[1:2]) + mod[0:1]
    hb = h.astype(BF16)

    def seg(name):
        a, b = _SEG[name]
        return _dot(hb, w1_ref[:, a:b])

    cos_m = tabm_ref[:, 0:128]
    sin_m = tabm_ref[:, 128:256]
    cos_s = tabs_ref[:, 0:128]
    sin_s = tabs_ref[:, 128:256]

    cqn = _rms(seg("cq"), gq_ref[...]).astype(BF16)
    qa = _dot(cqn, wq_ref[:, 0:1024])
    qb = _dot(cqn, wq_ref[:, 1024:2048])
    q_ref[0] = (qa * jnp.tile(cos_m, (1, MLA_HEADS)) + qb * jnp.tile(sin_m, (1, MLA_HEADS))).astype(BF16)

    ckvn = _rms(seg("ckv"), gkv_ref[...]).astype(BF16)
    kr = seg("kr") * cos_m + seg("kr_rot") * sin_m
    k_ref[0] = (_dot(ckvn, wkn_ref[...]) + jnp.tile(kr, (1, MLA_HEADS))).astype(BF16)
    ones_lane = (lax.broadcasted_iota(I32, (1, MLA_HEADS * MLA_HEAD_PAD), 1) % MLA_HEAD_PAD == MLA_V).astype(F32)
    v_ref[0] = (_dot(ckvn, wv_ref[...]) + ones_lane).astype(BF16)

    u_ref[0] = seg("u")

    qs_ref[0] = (seg("qs") * jnp.tile(cos_s, (1, 4)) + seg("qs_rot") * jnp.tile(sin_s, (1, 4))).astype(BF16)
    ks = seg("ks") * jnp.tile(cos_s, (1, 2)) + seg("ks_rot") * jnp.tile(sin_s, (1, 2))
    for hk in range(SWA_KV_HEADS):
        k_lo = ks[:, hk * 128:(hk + 1) * 128]
        ks_ref[0, :, (2 * hk) * 128:(2 * hk + 1) * 128] = k_lo.astype(BF16)
        ks_ref[0, :, (2 * hk + 1) * 128:(2 * hk + 2) * 128] = pltpu.roll(k_lo, SWA_HEAD_DIM, 1).astype(BF16)
    ones_s = (lax.broadcasted_iota(I32, (1, SWA_KV_HEADS * 128), 1) % 128 == SWA_HEAD_DIM).astype(F32)
    vs_ref[0] = (seg("vs") + ones_s).astype(BF16)

    g0, _ = _SEG["gl"]
    for p in range(6):
        gate_ref[0, :, p * 512:(p + 1) * 512] = _sigmoid(
            _dot(hb, w1_ref[:, g0 + p * 512:g0 + (p + 1) * 512])).astype(BF16)


def _input_stage(x_all, modtab, g_mix, w1, g_q, wq, g_kv, wkn, wv, tabm, tabs):
    b, p, d = x_all.shape
    tm = TOKEN_TILE
    nj = p // tm
    tok = lambda w: pl.BlockSpec((1, tm, w), lambda bi, j: (bi, j, 0))
    full = lambda a: pl.BlockSpec(a.shape, lambda bi, j: (0,) * a.ndim)
    outs = [(1024, BF16), (1024, BF16), (1024, BF16), (512, F32), (512, BF16), (512, BF16), (256, BF16),
            (3072, BF16)]
    return pl.pallas_call(
        _in_kernel,
        grid=(b, nj),
        in_specs=[tok(d),
                  pl.BlockSpec((1, 1, 8, d), lambda bi, j: (bi, j // (nj - 1), 0, 0)),
                  full(g_mix), full(w1), full(g_q), full(wq), full(g_kv), full(wkn), full(wv),
                  pl.BlockSpec((tm, 256), lambda bi, j: (j, 0)),
                  pl.BlockSpec((tm, 256), lambda bi, j: (j, 0))],
        out_specs=[tok(w) for w, _ in outs],
        out_shape=[jax.ShapeDtypeStruct((b, p, w), dt) for w, dt in outs],
        compiler_params=_cparams(("arbitrary", "arbitrary")),
        name="input_stage",
    )(x_all, modtab, g_mix, w1, g_q, wq, g_kv, wkn, wv, tabm, tabs)


def _mla_kernel(q_ref, k_ref, v_ref, *rest, tk, n_main, tail_rows):
    y_ref, m_ref, acc_ref = rest[-3:]
    hw = MLA_HEAD_PAD
    m_ref[...] = jnp.full(m_ref.shape, MASK_VALUE, F32)
    acc_ref[...] = jnp.zeros(acc_ref.shape, F32)

    def attend(r0, rows):
        for h in range(MLA_HEADS):
            q = q_ref[0, :, h * hw:(h + 1) * hw]
            k = k_ref[0, pl.ds(r0, rows), h * hw:(h + 1) * hw]
            v = v_ref[0, pl.ds(r0, rows), h * hw:(h + 1) * hw]
            s = _dot_nt(q, k)
            m_prev = m_ref[h]
            m_new = jnp.maximum(m_prev, jnp.max(s, axis=1, keepdims=True))
            m_ref[h] = m_new
            pr = jnp.exp2(s - jnp.tile(m_new, (1, rows // hw)))
            acc_ref[h] = jnp.exp2(m_prev - m_new) * acc_ref[h] + _dot(pr.astype(BF16), v)

    if n_main:
        def step(c, carry):
            attend(pl.multiple_of(c * tk, tk), tk)
            return carry

        lax.fori_loop(0, n_main, step, 0)
    if tail_rows:
        attend(n_main * tk, tail_rows)

    first = lax.broadcasted_iota(I32, (q_ref.shape[1], hw), 1) < MLA_V
    for hp in range(MLA_HEADS // 2):
        o = []
        for h in (2 * hp, 2 * hp + 1):
            acc = acc_ref[h]
            o.append(acc / acc[:, MLA_V:MLA_V + 1])
        y_ref[0, :, hp * hw:(hp + 1) * hw] = jnp.where(first, o[0], pltpu.roll(o[1], MLA_V, 1)).astype(BF16)


def _mla_attention(q, k, v, n_ctx):
    b, p, width = q.shape
    s_len = p - n_ctx
    tq = MLA_Q_TILE
    tk = MLA_KEY_CHUNK
    assert s_len % tq == 0 and s_len % tk == 0 and s_len % n_ctx == 0
    out_w = MLA_HEADS * MLA_V
    state = lambda rows: pltpu.VMEM((MLA_HEADS, rows, MLA_HEAD_PAD), F32)
    y = pl.pallas_call(
        functools.partial(_mla_kernel, tk=tk, n_main=p // tk, tail_rows=p % tk),
        grid=(b, s_len // tq),
        in_specs=[pl.BlockSpec((1, tq, width), lambda bi, j: (bi, j, 0)),
                  pl.BlockSpec((1, p, width), lambda bi, j: (bi, 0, 0), pipeline_mode=pl.Buffered(1)),
                  pl.BlockSpec((1, p, width), lambda bi, j: (bi, 0, 0), pipeline_mode=pl.Buffered(1))],
        out_specs=pl.BlockSpec((1, tq, out_w), lambda bi, j: (bi, j, 0)),
        out_shape=jax.ShapeDtypeStruct((b, p, out_w), BF16),
        scratch_shapes=[state(tq), state(tq)],
        compiler_params=_cparams(("arbitrary", "arbitrary")),
        name="mla_attention",
    )(q, k, v)
    cblk = s_len // n_ctx
    ctx_rows = lambda w: pl.BlockSpec((1, n_ctx, w), lambda bi: (bi, cblk, 0))
    return pl.pallas_call(
        functools.partial(_mla_kernel, tk=tk, n_main=0, tail_rows=n_ctx),
        grid=(b,),
        in_specs=[ctx_rows(width), ctx_rows(width), ctx_rows(width), pl.BlockSpec(memory_space=pl.ANY)],
        out_specs=ctx_rows(out_w),
        out_shape=jax.ShapeDtypeStruct((b, p, out_w), BF16),
        input_output_aliases={3: 0},
        scratch_shapes=[state(n_ctx), state(n_ctx)],
        compiler_params=_cparams(("arbitrary",)),
        name="mla_attention_ctx",
    )(q, k, v, y)


def _swa_kernel(sink_ref, q_ref, k_ref, v_ref, y_ref, *, n_ctx):
    j = pl.program_id(1)
    p_len = k_ref.shape[1]
    s_len = p_len - n_ctx
    band = 3 * SWA_BLOCK
    n_lat_tiles = s_len // SWA_BLOCK
    is_lat = j < n_lat_tiles
    n = jnp.minimum(j, n_lat_tiles - 1)
    ws = jnp.clip((n - 1) * SWA_BLOCK, 0, s_len - band)
    kstart = pl.multiple_of(ws, SWA_BLOCK)
    qpos = n * SWA_BLOCK + lax.broadcasted_iota(I32, (2 * SWA_BLOCK, band), 0) % SWA_BLOCK
    kpos = ws + lax.broadcasted_iota(I32, (2 * SWA_BLOCK, band), 1)
    valid = (jnp.abs(qpos - kpos) <= SWA_WINDOW) & is_lat
    first = lax.broadcasted_iota(I32, (SWA_BLOCK, 128), 1) < SWA_HEAD_DIM
    top = lax.broadcasted_iota(I32, (2 * SWA_BLOCK, 1), 0) < SWA_BLOCK
    for hk in range(SWA_KV_HEADS):
        qpair = jnp.concatenate([q_ref[0, :, (2 * hk) * 128:(2 * hk + 1) * 128],
                                 q_ref[0, :, (2 * hk + 1) * 128:(2 * hk + 2) * 128]], axis=0)
        vcols = slice(hk * 128, (hk + 1) * 128)
        vc = v_ref[0, s_len:p_len, vcols]
        vb = v_ref[0, pl.ds(kstart, band), vcols]
        res = {}
        for par in range(2):
            kcols = slice((2 * hk + par) * 128, (2 * hk + par + 1) * 128)
            s_c = _dot_nt(qpair, k_ref[0, s_len:p_len, kcols])
            s_b = jnp.where(valid, _dot_nt(qpair, k_ref[0, pl.ds(kstart, band), kcols]), MASK_VALUE)
            h_top = 4 * hk + par
            h_bot = 4 * hk + 2 + par
            sink = jnp.where(top, sink_ref[h_top] * LOG2_E, sink_ref[h_bot] * LOG2_E)
            m = jnp.maximum(jnp.maximum(jnp.max(s_c, axis=1, keepdims=True), jnp.max(s_b, axis=1, keepdims=True)),
                            sink)
            o = _dot(jnp.exp2(s_c - m).astype(BF16), vc) + _dot(jnp.exp2(s_b - m).astype(BF16), vb)
            o = o / (o[:, SWA_HEAD_DIM:SWA_HEAD_DIM + 1] + jnp.exp2(sink - m))
            res[h_top] = o[0:SWA_BLOCK]
            res[h_bot] = o[SWA_BLOCK:2 * SWA_BLOCK]
        for pair in (2 * hk, 2 * hk + 1):
            y_ref[0, :, pair * 128:(pair + 1) * 128] = jnp.where(
                first, res[2 * pair], pltpu.roll(res[2 * pair + 1], SWA_HEAD_DIM, 1)).astype(BF16)


def _swa_attention(sink, qs, ks, vs, n_ctx):
    b, p, _ = qs.shape
    tq = SWA_BLOCK
    return pl.pallas_call(
        functools.partial(_swa_kernel, n_ctx=n_ctx),
        grid=(b, p // tq),
        in_specs=[pl.BlockSpec(memory_space=pltpu.SMEM),
                  pl.BlockSpec((1, tq, SWA_WIDTH), lambda bi, j: (bi, j, 0)),
                  pl.BlockSpec((1, p, ks.shape[2]), lambda bi, j: (bi, 0, 0)),
                  pl.BlockSpec((1, p, vs.shape[2]), lambda bi, j: (bi, 0, 0))],
        out_specs=pl.BlockSpec((1, tq, SWA_WIDTH), lambda bi, j: (bi, j, 0)),
        out_shape=jax.ShapeDtypeStruct((b, p, SWA_WIDTH), BF16),
        compiler_params=_cparams(("arbitrary", "arbitrary")),
        name="swa_attention",
    )(sink, qs, ks, vs)


def _pool_kernel(prev_ref, cur_ref, next_ref, w_ref, scale_ref, y_ref, ext_ref, *, n_ctx):
    j = pl.program_id(1)
    tm = cur_ref.shape[1]
    nj = pl.num_programs(1)
    s_len = (nj - 1) * tm
    is_ctx = j == nj - 1
    has_prev = (j >= 1) & (j < nj - 1)
    has_next = j < nj - 2
    ext_ref[0:POOL_HALO, :] = jnp.where(has_prev, prev_ref[0], 0.0)
    ext_ref[POOL_HALO:POOL_HALO + tm, :] = cur_ref[0]
    ext_ref[POOL_HALO + tm:POOL_HALO + tm + POOL_HALO, :] = jnp.where(has_next, next_ref[0], 0.0)
    t = lax.broadcasted_iota(I32, (tm, 1), 0)
    pos = jnp.where(is_ctx, t, j * tm + t)
    seg_len = jnp.where(is_ctx, n_ctx, s_len)
    for g, w in enumerate(POOL_WINDOWS):
        cols = slice(g * POOL_GROUP_DIM, (g + 1) * POOL_GROUP_DIM)
        acc = jnp.zeros((tm, POOL_GROUP_DIM), F32)
        for off in range(-(w // 2), w - w // 2):
            acc = acc + ext_ref[POOL_HALO + off:POOL_HALO + off + tm, cols]
        lo = jnp.maximum(pos - w // 2, 0)
        hi = jnp.minimum(pos + w - w // 2, seg_len)
        cnt = (hi - lo).astype(F32)
        pooled = acc / cnt - cur_ref[0, :, cols]
        y_ref[0, :, cols] = (_dot(pooled.astype(BF16), w_ref[g]) * scale_ref[:, cols]).astype(BF16)


def _pool_stage(u, w_pool, pool_scale, n_ctx):
    b, p, width = u.shape
    tm = TOKEN_TILE
    hb = tm // POOL_HALO
    n_halo_blocks = p // POOL_HALO
    return pl.pallas_call(
        functools.partial(_pool_kernel, n_ctx=n_ctx),
        grid=(b, p // tm),
        in_specs=[pl.BlockSpec((1, POOL_HALO, width), lambda bi, j: (bi, jnp.maximum(j * hb - 1, 0), 0)),
                  pl.BlockSpec((1, tm, width), lambda bi, j: (bi, j, 0)),
                  pl.BlockSpec((1, POOL_HALO, width),
                               lambda bi, j: (bi, jnp.minimum((j + 1) * hb, n_halo_blocks - 1), 0)),
                  pl.BlockSpec(w_pool.shape, lambda bi, j: (0, 0, 0)),
                  pl.BlockSpec(pool_scale.shape, lambda bi, j: (0, 0))],
        out_specs=pl.BlockSpec((1, tm, width), lambda bi, j: (bi, j, 0)),
        out_shape=jax.ShapeDtypeStruct((b, p, width), BF16),
        scratch_shapes=[pltpu.VMEM((tm + 2 * POOL_HALO, width), F32)],
        compiler_params=_cparams(("arbitrary", "arbitrary")),
        name="pool_stage",
    )(u, u, u, w_pool, pool_scale)


def _merge_kernel(ym_ref, yp_ref, ys_ref, gate_ref, x_ref, mod_ref, g_ref, wbm_ref, wbp_ref, wbs_ref, wout_ref,
                  wsg_ref, wsu_ref, wsd_ref, wrt_ref, xs_ref, fp_ref, lg_ref):
    d = x_ref.shape[2]
    mod = mod_ref[0, 0]
    gate = gate_ref[0]
    merged = (gate[:, 0:d].astype(F32) * _dot(ym_ref[0], wbm_ref[...])
              + gate[:, d:2 * d].astype(F32) * _dot(yp_ref[0], wbp_ref[...])
              + gate[:, 2 * d:3 * d].astype(F32) * _dot(ys_ref[0], wbs_ref[...]))
    x_mid = x_ref[0] + mod[2:3] * _dot(merged.astype(BF16), wout_ref[...])
    f = _rms(x_mid, g_ref[...]) * (1.0 + mod[4:5]) + mod[3:4]
    fb = f.astype(BF16)
    gsh = _dot(fb, wsg_ref[...])
    shared = _dot((gsh * _sigmoid(gsh) * _dot(fb, wsu_ref[...])).astype(BF16), wsd_ref[...])
    xs_ref[0] = x_mid + mod[5:6] * shared
    fp_ref[0] = _pack_bf16_pair(f)
    lg_ref[...] = _dot_nt(wrt_ref[...], fb)


def _merge_stage(ym, yp, ys, gates, x_all, modtab, g_ffn, wbm, wbp, wbs, wout, wsg, wsu, wsd, wrt):
    b, p, d = x_all.shape
    tm = TOKEN_TILE
    nj = p // tm
    tok = lambda w: pl.BlockSpec((1, tm, w), lambda bi, j: (bi, j, 0))
    full = lambda a: pl.BlockSpec(a.shape, lambda bi, j: (0,) * a.ndim)
    return pl.pallas_call(
        _merge_kernel,
        grid=(b, nj),
        in_specs=[tok(ym.shape[2]), tok(yp.shape[2]), tok(ys.shape[2]), tok(gates.shape[2]), tok(d),
                  pl.BlockSpec((1, 1, 8, d), lambda bi, j: (bi, j // (nj - 1), 0, 0)),
                  full(g_ffn), full(wbm), full(wbp), full(wbs), full(wout), full(wsg), full(wsu), full(wsd),
                  full(wrt)],
        out_specs=[tok(d), tok(d // 2), pl.BlockSpec((N_EXPERTS, tm), lambda bi, j: (0, bi * nj + j))],
        out_shape=[jax.ShapeDtypeStruct((b, p, d), F32),
                   jax.ShapeDtypeStruct((b, p, d // 2), U32),
                   jax.ShapeDtypeStruct((N_EXPERTS, b * p), F32)],
        compiler_params=_cparams(("arbitrary", "arbitrary")),
        name="merge_stage",
    )(ym, yp, ys, gates, x_all, modtab, g_ffn, wbm, wbp, wbs, wout, wsg, wsu, wsd, wrt)


def _route_kernel(lg_ref, bias_ref, dest_ref, wts_ref, seg_ref):
    tm = lg_ref.shape[1]
    ne = N_EXPERTS
    neg_inf = -jnp.inf
    scores = _sigmoid(lg_ref[...])
    sel = scores + bias_ref[...]
    iota_g = lax.broadcasted_iota(I32, (EXPERTS_PER_GROUP, tm), 0)
    gscore = []
    for g in range(N_GROUPS):
        sg = sel[g * EXPERTS_PER_GROUP:(g + 1) * EXPERTS_PER_GROUP]
        m1 = jnp.max(sg, axis=0, keepdims=True)
        i1 = jnp.min(jnp.where(sg == m1, iota_g, EXPERTS_PER_GROUP), axis=0, keepdims=True)
        m2 = jnp.max(jnp.where(iota_g == i1, neg_inf, sg), axis=0, keepdims=True)
        gscore.append(m1 + m2)
    rows = []
    for g in range(N_GROUPS):
        rank = jnp.zeros((1, tm), I32)
        for g2 in range(N_GROUPS):
            if g2 == g:
                continue
            beats = (gscore[g2] >= gscore[g]) if g2 < g else (gscore[g2] > gscore[g])
            rank = rank + beats.astype(I32)
        rows.append(jnp.where(rank < TOPK_GROUPS, sel[g * EXPERTS_PER_GROUP:(g + 1) * EXPERTS_PER_GROUP], neg_inf))
    cur = jnp.concatenate(rows, axis=0)
    iota_e = lax.broadcasted_iota(I32, (ne, tm), 0)
    picks = []
    member = jnp.zeros((ne, tm), F32)
    for _ in range(TOP_K):
        m = jnp.max(cur, axis=0, keepdims=True)
        idx = jnp.min(jnp.where(cur == m, iota_e, ne), axis=0, keepdims=True)
        hit = iota_e == idx
        picks.append(hit)
        member = member + hit.astype(F32)
        cur = jnp.where(hit, neg_inf, cur)
    earlier = (lax.broadcasted_iota(I32, (tm, tm), 0) < lax.broadcasted_iota(I32, (tm, tm), 1)).astype(BF16)
    pos = _dot(member.astype(BF16), earlier)
    cnt_col = jnp.sum(member, axis=1, keepdims=True)
    blocks_col = jnp.floor((cnt_col + (SEG_ALIGN - 1)) * (1.0 / SEG_ALIGN))
    lower = (lax.broadcasted_iota(I32, (ne, ne), 1) < lax.broadcasted_iota(I32, (ne, ne), 0)).astype(BF16)
    off_col = _dot(lower, jnp.broadcast_to(blocks_col, (ne, 128)).astype(BF16))[:, 0:1] * SEG_ALIGN
    base = off_col + pos
    w_rows = [jnp.sum(jnp.where(hit, scores, 0.0), axis=0, keepdims=True) for hit in picks]
    denom = w_rows[0]
    for w in w_rows[1:]:
        denom = denom + w
    for k, hit in enumerate(picks):
        dest_ref[0, k:k + 1, :] = jnp.sum(jnp.where(hit, base, 0.0), axis=0, keepdims=True).astype(I32)
        wts_ref[0, k:k + 1, :] = w_rows[k] / denom * ROUTED_SCALE
    member_pad = jnp.concatenate([member, jnp.zeros((128 - ne, tm), F32)], axis=0).astype(BF16)
    cnt_row = _dot_nt(jnp.ones((8, tm), BF16), member_pad)
    blocks_row = jnp.floor((cnt_row + (SEG_ALIGN - 1)) * (1.0 / SEG_ALIGN))
    before = (lax.broadcasted_iota(I32, (128, 128), 0) < lax.broadcasted_iota(I32, (128, 128), 1)).astype(BF16)
    off_row = _dot(blocks_row.astype(BF16), before) * SEG_ALIGN
    r = lax.broadcasted_iota(I32, (8, 128), 0)
    seg_ref[0] = jnp.where(r == 0, off_row, jnp.where(r == 1, cnt_row, 0.0)).astype(I32)


def _route_stage(lg_t, bias_col):
    ne, t_all = lg_t.shape
    tm = MOE_TILE
    nt = t_all // tm
    return pl.pallas_call(
        _route_kernel,
        grid=(nt,),
        in_specs=[pl.BlockSpec((ne, tm), lambda i: (0, i)),
                  pl.BlockSpec((ne, 1), lambda i: (0, 0))],
        out_specs=[pl.BlockSpec((1, TOP_K, tm), lambda i: (i, 0, 0)),
                   pl.BlockSpec((1, TOP_K, tm), lambda i: (i, 0, 0)),
                   pl.BlockSpec((1, 8, 128), lambda i: (i, 0, 0))],
        out_shape=[jax.ShapeDtypeStruct((nt, TOP_K, tm), I32),
                   jax.ShapeDtypeStruct((nt, TOP_K, tm), F32),
                   jax.ShapeDtypeStruct((nt, 8, 128), I32)],
        compiler_params=_cparams(("arbitrary",)),
        name="route_stage",
    )(lg_t, bias_col)


def _moe_row_stride(tm):
    cap = TOP_K * tm + N_EXPERTS * SEG_ALIGN + MOE_CHUNK
    blocks = cap // 8 + 1
    return 8 * (blocks + 1 - blocks % 2)


def _moe_kernel(dest_ref, wts_ref, seg_ref, fp_ref, xs_ref, gt_ref, wg_ref, wu_ref, wd_ref, out_ref, buf_ref,
                *, srow, sub, n_sub_per_batch):
    i = pl.program_id(0)
    eg = pl.program_id(1)
    tm = dest_ref.shape[2]
    ch = MOE_CHUNK
    group = wg_ref.shape[0]
    lane_blk = 128

    def table_block(ref, blk):
        return ref.at[0, :, pl.ds(pl.multiple_of(blk * lane_blk, lane_blk), lane_blk)]

    @pl.when(eg == 0)
    def _group_rows():
        buf_ref[...] = jnp.zeros_like(buf_ref)

        def body(blk, carry):
            dest = table_block(dest_ref, blk)
            rows = fp_ref.at[pl.ds(pl.multiple_of(blk * (lane_blk * 4), lane_blk * 4), lane_blk * 4), :]
            for u in range(lane_blk):
                slab = rows[u * 4:(u + 1) * 4, :]
                for k in range(TOP_K):
                    buf_ref[pl.ds(dest[k, u], 4, stride=srow), :] = slab
            return carry

        lax.fori_loop(0, tm // lane_blk, body, 0)

    for ge in range(group):
        e = eg * group + ge
        n_rows = seg_ref[0, 1, e]
        seg_off = seg_ref[0, 0, e]

        def ffn_rows(r0, n_valid, rows, ge=ge):
            words = [buf_ref[pl.ds(pl.multiple_of(q * srow + r0, 8), rows), :] for q in range(4)]
            w = jnp.concatenate(words, axis=1)
            xb = jnp.concatenate([_unpack_lo(w), _unpack_hi(w)], axis=1).astype(BF16)
            g = _dot(xb, wg_ref[ge])
            hmid = g * _sigmoid(g) * _dot(xb, wu_ref[ge])
            y = _dot(hmid.astype(BF16), wd_ref[ge])
            packed = _pack_bf16_pair(y)
            keep_new = lax.broadcasted_iota(I32, (rows, 1), 0) < n_valid
            for q in range(4):
                buf_ref[pl.ds(pl.multiple_of(q * srow + r0, 8), rows), :] = jnp.where(
                    keep_new, packed[:, q * 128:(q + 1) * 128], words[q])

        n_full = n_rows // ch
        rem = n_rows - n_full * ch

        def chunk(c, carry, ffn_rows=ffn_rows, seg_off=seg_off):
            ffn_rows(seg_off + c * ch, ch, ch)
            return carry

        lax.fori_loop(0, n_full, chunk, 0)
        pl.when(rem > ch // 2)(functools.partial(ffn_rows, seg_off + n_full * ch, rem, ch))
        pl.when((rem > 0) & (rem <= ch // 2))(functools.partial(ffn_rows, seg_off + n_full * ch, rem, ch // 2))

    @pl.when(eg == pl.num_programs(1) - 1)
    def _combine():
        def body(blk, carry):
            sub_blk = (i * tm + blk * lane_blk) // sub
            bi = sub_blk // n_sub_per_batch
            gate = gt_ref[bi * 2 + (sub_blk - bi * n_sub_per_batch) // (n_sub_per_batch - 1)]
            dest = table_block(dest_ref, blk)
            wts = table_block(wts_ref, blk)
            base = pl.multiple_of(blk * (lane_blk * 8), lane_blk * 8)
            xs = xs_ref.at[pl.ds(base, lane_blk * 8), :]
            out = out_ref.at[pl.ds(base, lane_blk * 8), :]
            for u in range(lane_blk):
                acc_lo = jnp.zeros((4, 128), F32)
                acc_hi = jnp.zeros((4, 128), F32)
                for k in range(TOP_K):
                    words = buf_ref[pl.ds(dest[k, u], 4, stride=srow), :]
                    wk = wts[k, u]
                    acc_lo = acc_lo + wk * _unpack_lo(words)
                    acc_hi = acc_hi + wk * _unpack_hi(words)
                out[u * 8:u * 8 + 4, :] = xs[u * 8:u * 8 + 4, :] + gate[0:4] * acc_lo
                out[u * 8 + 4:u * 8 + 8, :] = xs[u * 8 + 4:u * 8 + 8, :] + gate[4:8] * acc_hi
            return carry

        lax.fori_loop(0, tm // lane_blk, body, 0)


def _moe_stage(dest, wts, seg, fp4, xs8, gt2, wg, wu, wd, n_sub_per_batch):
    nt, _, tm = dest.shape
    ne = wg.shape[0]
    group = MOE_EXPERTS_PER_STEP
    assert ne % group == 0 and tm % 128 == 0 and TOKEN_TILE % 128 == 0
    srow = _moe_row_stride(tm)
    smem = lambda shape: pl.BlockSpec(shape, lambda i, e: (i, 0, 0), memory_space=pltpu.SMEM)
    once = dict(pipeline_mode=pl.Buffered(1))
    return pl.pallas_call(
        functools.partial(_moe_kernel, srow=srow, sub=TOKEN_TILE, n_sub_per_batch=n_sub_per_batch),
        grid=(nt, ne // group),
        in_specs=[smem((1, TOP_K, tm)), smem((1, TOP_K, tm)), smem((1, 8, 128)),
                  pl.BlockSpec((tm * 4, 128), lambda i, e: (i, 0), **once),
                  pl.BlockSpec((tm * 8, 128), lambda i, e: (i, 0), **once),
                  pl.BlockSpec(gt2.shape, lambda i, e: (0, 0, 0)),
                  pl.BlockSpec((group,) + wg.shape[1:], lambda i, e: (e, 0, 0)),
                  pl.BlockSpec((group,) + wu.shape[1:], lambda i, e: (e, 0, 0)),
                  pl.BlockSpec((group,) + wd.shape[1:], lambda i, e: (e, 0, 0))],
        out_specs=pl.BlockSpec((tm * 8, 128), lambda i, e: (i, 0)),
        out_shape=jax.ShapeDtypeStruct(xs8.shape, F32),
        scratch_shapes=[pltpu.VMEM((4 * srow, 128), U32)],
        compiler_params=_cparams(("arbitrary", "arbitrary")),
        name="moe_stage",
    )(dest, wts, seg, fp4, xs8, gt2, wg, wu, wd)


def _final_kernel(x_ref, g_ref, o_ref):
    o_ref[0] = _rms(x_ref[0], g_ref[...])


def _final_norm(x_all, g_final, n_ctx):
    b, p, d = x_all.shape
    tm = TOKEN_TILE
    return pl.pallas_call(
        _final_kernel,
        grid=(b, (p - n_ctx) // tm),
        in_specs=[pl.BlockSpec((1, tm, d), lambda bi, j: (bi, j, 0)),
                  pl.BlockSpec((1, d), lambda bi, j: (0, 0))],
        out_specs=pl.BlockSpec((1, tm, d), lambda bi, j: (bi, j, 0)),
        out_shape=jax.ShapeDtypeStruct((b, p - n_ctx, d), F32),
        compiler_params=_cparams(("arbitrary", "arbitrary")),
        name="final_norm",
    )(x_all, g_final)


def _rot_cols(w, n_heads, rot_dim):
    kdim = w.shape[0]
    w4 = w.reshape(kdim, n_heads, 4, rot_dim // 4)
    rot = jnp.stack([-w4[:, :, 1], w4[:, :, 0], -w4[:, :, 3], w4[:, :, 2]], axis=2)
    return rot.reshape(kdim, n_heads * rot_dim)


def _rope_pattern(s_len, n_ctx, rot_dim):
    t = jnp.arange(s_len)
    row = (t // GRID_W).astype(F32)
    col = (t % GRID_W).astype(F32)
    n_freq = rot_dim // 4
    inv_freq = ROPE_BASE ** (-jnp.arange(n_freq, dtype=F32) / n_freq)
    ang_r = row[:, None] * inv_freq[None, :]
    ang_c = col[:, None] * inv_freq[None, :]
    cos = jnp.concatenate([jnp.cos(ang_r), jnp.cos(ang_r), jnp.cos(ang_c), jnp.cos(ang_c)], axis=1)
    sin = jnp.concatenate([jnp.sin(ang_r), jnp.sin(ang_r), jnp.sin(ang_c), jnp.sin(ang_c)], axis=1)
    cos = jnp.concatenate([cos, jnp.ones((n_ctx, rot_dim), F32)], axis=0)
    sin = jnp.concatenate([sin, jnp.zeros((n_ctx, rot_dim), F32)], axis=0)
    return cos, sin


def _layer_weights(w_in, w_uq, w_ukv):
    d = w_in.shape[0]
    offs = [0]
    for w in (MLA_Q_LORA, MLA_KV_LORA, MLA_ROPE, POOL_WIDTH, SWA_WIDTH, SWA_KV_WIDTH, SWA_KV_WIDTH, 3 * d):
        offs.append(offs[-1] + w)
    cq, ckv, kr, u, qs, ks, vs, gl = (w_in[:, offs[i]:offs[i + 1]] for i in range(8))
    qs = qs * (SWA_SCALE * LOG2_E)
    zeros = lambda n: jnp.zeros((d, n), F32)
    kr_slot = jnp.concatenate([zeros(MLA_NOPE), kr, zeros(MLA_HEAD_PAD - MLA_NOPE - MLA_ROPE)], axis=1)
    kr_rot_slot = jnp.concatenate([zeros(MLA_NOPE), _rot_cols(kr, 1, MLA_ROPE),
                                   zeros(MLA_HEAD_PAD - MLA_NOPE - MLA_ROPE)], axis=1)

    def kv_slots(w):
        w3 = w.reshape(d, SWA_KV_HEADS, SWA_HEAD_DIM)
        return jnp.concatenate([w3, jnp.zeros_like(w3)], axis=2).reshape(d, SWA_KV_HEADS * 128)

    w1 = jnp.concatenate([cq, ckv, u, qs, _rot_cols(qs, SWA_Q_HEADS, SWA_HEAD_DIM), kv_slots(ks),
                          kv_slots(_rot_cols(ks, SWA_KV_HEADS, SWA_HEAD_DIM)), kv_slots(vs), kr_slot, kr_rot_slot,
                          gl], axis=1).astype(BF16)
    lq = w_uq.shape[0]
    wq3 = (w_uq * (MLA_SCALE * LOG2_E)).reshape(lq, MLA_HEADS, MLA_NOPE + MLA_ROPE)
    nope, rope = wq3[:, :, :MLA_NOPE], wq3[:, :, MLA_NOPE:]
    rope_rot = _rot_cols(rope.reshape(lq, MLA_HEADS * MLA_ROPE), MLA_HEADS, MLA_ROPE).reshape(lq, MLA_HEADS, MLA_ROPE)
    pad = jnp.zeros((lq, MLA_HEADS, MLA_HEAD_PAD - MLA_NOPE - MLA_ROPE), F32)
    wq_a = jnp.concatenate([nope, rope, pad], axis=2).reshape(lq, MLA_HEADS * MLA_HEAD_PAD)
    wq_b = jnp.concatenate([jnp.zeros_like(nope), rope_rot, pad], axis=2).reshape(lq, MLA_HEADS * MLA_HEAD_PAD)
    wq = jnp.concatenate([wq_a, wq_b], axis=1).astype(BF16)
    lkv = w_ukv.shape[0]
    wkv3 = w_ukv.reshape(lkv, MLA_HEADS, MLA_NOPE + MLA_V)
    wkn = jnp.concatenate([wkv3[:, :, :MLA_NOPE], jnp.zeros((lkv, MLA_HEADS, MLA_HEAD_PAD - MLA_NOPE), F32)],
                          axis=2).reshape(lkv, MLA_HEADS * MLA_HEAD_PAD).astype(BF16)
    wv = jnp.concatenate([wkv3[:, :, MLA_NOPE:], jnp.zeros((lkv, MLA_HEADS, MLA_HEAD_PAD - MLA_V), F32)],
                         axis=2).reshape(lkv, MLA_HEADS * MLA_HEAD_PAD).astype(BF16)
    return w1, wq, wkn, wv


def kernel(x, c, ctx, c_ctx, w_mod, b_mod, g_mix, g_ffn, w_in, g_mla_q, g_mla_kv, w_mla_uq, w_mla_ukv, w_pool,
           pool_scale, swa_sink, w_br_mla, w_br_pool, w_br_swa, w_out, w_router, router_bias, w_exp_gate,
           w_exp_up, w_exp_down, w_sh_gate, w_sh_up, w_sh_down, g_final):
    b, s_len, d = x.shape
    n_ctx = ctx.shape[1]
    n_layers = w_mod.shape[0]
    p = n_ctx + s_len
    assert n_ctx == TOKEN_TILE and s_len % TOKEN_TILE == 0 and (b * p) % MOE_TILE == 0 and b + 1 <= 8
    assert d == 1024 and w_in.shape[2] == (MLA_Q_LORA + MLA_KV_LORA + MLA_ROPE + POOL_WIDTH + SWA_WIDTH
                                           + 2 * SWA_KV_WIDTH + 3 * d)

    cm, sm = _rope_pattern(s_len, n_ctx, MLA_ROPE)
    tail = jnp.zeros((p, MLA_HEAD_PAD - MLA_NOPE - MLA_ROPE), F32)
    tabm = jnp.concatenate([jnp.ones((p, MLA_NOPE), F32), cm, tail, jnp.zeros((p, MLA_NOPE), F32), sm, tail], axis=1)
    cs, ss = _rope_pattern(s_len, n_ctx, SWA_HEAD_DIM)
    tabs = jnp.concatenate([cs, cs, ss, ss], axis=1)

    cvec = jnp.concatenate([c, c_ctx[None, :], jnp.zeros((8 - b - 1, d), F32)], axis=0)
    mod_all = _modulation(cvec, w_mod, b_mod).reshape(n_layers, 8, N_MOD, d)

    x_all = jnp.concatenate([x, ctx], axis=1)
    for i in range(n_layers):
        lat = mod_all[i, :b]
        ctx_rows = jnp.broadcast_to(mod_all[i, b][None], (b, N_MOD, d))
        modtab = jnp.pad(jnp.stack([lat, ctx_rows], axis=1), ((0, 0), (0, 0), (0, 8 - N_MOD), (0, 0)))
        w1, wq, wkn, wv = _layer_weights(w_in[i], w_mla_uq[i], w_mla_ukv[i])
        q, k, v, u, qs, ks, vs, gates = _input_stage(
            x_all, modtab, g_mix[i][None], w1, g_mla_q[i][None], wq, g_mla_kv[i][None], wkn, wv, tabm, tabs)
        y_mla = _mla_attention(q, k, v, n_ctx)
        y_swa = _swa_attention(swa_sink[i], qs, ks, vs, n_ctx)
        y_pool = _pool_stage(u, w_pool[i].astype(BF16), pool_scale[i][None], n_ctx)
        xs, fp, lg_t = _merge_stage(
            y_mla, y_pool, y_swa, gates, x_all, modtab, g_ffn[i][None], w_br_mla[i].astype(BF16),
            w_br_pool[i].astype(BF16), w_br_swa[i].astype(BF16), w_out[i].astype(BF16),
            w_sh_gate[i].astype(BF16), w_sh_up[i].astype(BF16), w_sh_down[i].astype(BF16),
            w_router[i].T.astype(BF16))
        dest, wts, seg = _route_stage(lg_t, router_bias[i][:, None])
        gt2 = modtab[:, :, 5, :].reshape(b * 2, 8, d // 8)
        x_new = _moe_stage(dest, wts, seg, fp.reshape(b * p * 4, 128), xs.reshape(b * p * 8, 128), gt2,
                           w_exp_gate[i].astype(BF16), w_exp_up[i].astype(BF16), w_exp_down[i].astype(BF16),
                           p // TOKEN_TILE)
        x_all = x_new.reshape(b, p, d)
    return _final_norm(x_all, g_final[None], n_ctx)
```

```python
import functools

import jax
import jax.numpy as jnp
from jax import lax
from jax.experimental import pallas as pl
from jax.experimental.pallas import tpu as pltpu

F32 = jnp.float32
BF16 = jnp.bfloat16
U32 = jnp.uint32
I32 = jnp.int32

NORM_EPS = 1e-6
ROPE_BASE = 10000.0
GRID_W = 64
N_MOD = 6

MLA_HEADS = 8
MLA_Q_LORA = 384
MLA_KV_LORA = 256
MLA_NOPE = 64
MLA_ROPE = 32
MLA_V = 64
MLA_SCALE = (MLA_NOPE + MLA_ROPE) ** -0.5
MLA_HEAD_PAD = 128
LOG2_E = 1.4426950408889634

POOL_WINDOWS = (2, 4, 8, 16)
POOL_GROUP_DIM = 128
POOL_WIDTH = 512
POOL_HALO = 8

SWA_Q_HEADS = 8
SWA_KV_HEADS = 2
SWA_HEAD_DIM = 64
SWA_WINDOW = 128
SWA_BLOCK = 128
SWA_SCALE = SWA_HEAD_DIM ** -0.5
SWA_WIDTH = SWA_Q_HEADS * SWA_HEAD_DIM
SWA_KV_WIDTH = SWA_KV_HEADS * SWA_HEAD_DIM

N_EXPERTS = 64
TOP_K = 8
N_GROUPS = 8
TOPK_GROUPS = 4
EXPERTS_PER_GROUP = 8
D_EXPERT = 256
ROUTED_SCALE = 2.5

TOKEN_TILE = 256
MLA_Q_TILE = 1024
MLA_KEY_CHUNK = 256
MOE_TILE = 1024
MOE_CHUNK = 256
MOE_EXPERTS_PER_STEP = 4
SEG_ALIGN = 8
SLAB_ROWS = 8
PACKED_ROWS = 4
MASK_VALUE = -1e30
HI16 = 0xFFFF0000

VMEM_LIMIT = 56 * 1024 * 1024

_SEG_WIDTHS = (("cq", 384), ("ckv", 256), ("u", 512), ("qs", 512), ("qs_rot", 512), ("ks", 256),
               ("ks_rot", 256), ("vs", 256), ("kr", 128), ("kr_rot", 128), ("gl", 3072))
_SEG = {}
_o = 0
for _n, _w in _SEG_WIDTHS:
    _SEG[_n] = (_o, _o + _w)
    _o += _w
FUSED_IN_WIDTH = _o


def _cparams(sem):
    return pltpu.CompilerParams(dimension_semantics=sem, vmem_limit_bytes=VMEM_LIMIT)


def _dot(a, b):
    return jnp.dot(a, b, preferred_element_type=F32)


def _dot_nt(a, b):
    return lax.dot_general(a, b, (((1,), (1,)), ((), ())), preferred_element_type=F32)


def _sigmoid(x):
    return 1.0 / (1.0 + jnp.exp(-x))


def _rms(x, g):
    return x * lax.rsqrt(jnp.mean(x * x, axis=-1, keepdims=True) + NORM_EPS) * g


def _pack_bf16_pair(v):
    n = v.shape[1] // 2
    bits = pltpu.bitcast(v.astype(BF16).astype(F32), U32)
    return (bits[:, :n] >> 16) | (bits[:, n:] & jnp.uint32(HI16))


def _slab_rows_to_matrix(ref, n_tok, per_tok):
    return jnp.concatenate([ref[pl.ds(c, n_tok, stride=per_tok), :] for c in range(per_tok)], axis=1)


def _matrix_to_slab_rows(ref, val, per_tok):
    n_tok = val.shape[0]
    for c in range(per_tok):
        ref[pl.ds(c, n_tok, stride=per_tok), :] = val[:, c * 128:(c + 1) * 128]


def _unpack_lo(w):
    return pltpu.bitcast(w << 16, F32)


def _unpack_hi(w):
    return pltpu.bitcast(w & jnp.uint32(HI16), F32)


def _mod_kernel(c_ref, w_ref, b_ref, o_ref):
    c = c_ref[...]
    a = (c * _sigmoid(c)).astype(BF16)
    o_ref[0] = _dot(a, w_ref[0].astype(BF16)) + b_ref[0]


def _modulation(cvec, w_mod, b_mod):
    n_layers, d, width = w_mod.shape
    tn = width // 4
    return pl.pallas_call(
        _mod_kernel,
        grid=(n_layers, width // tn),
        in_specs=[pl.BlockSpec((8, d), lambda l, n: (0, 0)),
                  pl.BlockSpec((1, d, tn), lambda l, n: (l, 0, n)),
                  pl.BlockSpec((1, 1, tn), lambda l, n: (l, 0, n))],
        out_specs=pl.BlockSpec((1, 8, tn), lambda l, n: (l, 0, n)),
        out_shape=jax.ShapeDtypeStruct((n_layers, 8, width), F32),
        compiler_params=_cparams(("arbitrary", "arbitrary")),
        name="modulation",
    )(cvec, w_mod, b_mod.reshape(n_layers, 1, width))


def _in_kernel(x_ref, mod_ref, g_ref, w1_ref, gq_ref, wq_ref, gkv_ref, wkn_ref, wv_ref, tabm_ref, tabs_ref,
               q_ref, k_ref, v_ref, u_ref, qs_ref, ks_ref, vs_ref, gate_ref):
    x = _slab_rows_to_matrix(x_ref, q_ref.shape[1], SLAB_ROWS)
    mod = mod_ref[0, 0]
    h = _rms(x, g_ref[...]) * (1.0 + mod[1:2]) + mod[0:1]
    hb = h.astype(BF16)

    def seg(name):
        a, b = _SEG[name]
        return _dot(hb, w1_ref[:, a:b])

    cos_m = tabm_ref[:, 0:128]
    sin_m = tabm_ref[:, 128:256]
    cos_s = tabs_ref[:, 0:128]
    sin_s = tabs_ref[:, 128:256]

    cqn = _rms(seg("cq"), gq_ref[...]).astype(BF16)
    qa = _dot(cqn, wq_ref[:, 0:1024])
    qb = _dot(cqn, wq_ref[:, 1024:2048])
    q_ref[0] = (qa * jnp.tile(cos_m, (1, MLA_HEADS)) + qb * jnp.tile(sin_m, (1, MLA_HEADS))).astype(BF16)

    ckvn = _rms(seg("ckv"), gkv_ref[...]).astype(BF16)
    kr = seg("kr") * cos_m + seg("kr_rot") * sin_m
    k_ref[0] = (_dot(ckvn, wkn_ref[...]) + jnp.tile(kr, (1, MLA_HEADS))).astype(BF16)
    ones_lane = (lax.broadcasted_iota(I32, (1, MLA_HEADS * MLA_HEAD_PAD), 1) % MLA_HEAD_PAD == MLA_V).astype(F32)
    v_ref[0] = (_dot(ckvn, wv_ref[...]) + ones_lane).astype(BF16)

    u_ref[0] = seg("u")

    qs_ref[0] = (seg("qs") * jnp.tile(cos_s, (1, 4)) + seg("qs_rot") * jnp.tile(sin_s, (1, 4))).astype(BF16)
    ks = seg("ks") * jnp.tile(cos_s, (1, 2)) + seg("ks_rot") * jnp.tile(sin_s, (1, 2))
    for hk in range(SWA_KV_HEADS):
        k_lo = ks[:, hk * 128:(hk + 1) * 128]
        ks_ref[0, :, (2 * hk) * 128:(2 * hk + 1) * 128] = k_lo.astype(BF16)
        ks_ref[0, :, (2 * hk + 1) * 128:(2 * hk + 2) * 128] = pltpu.roll(k_lo, SWA_HEAD_DIM, 1).astype(BF16)
    ones_s = (lax.broadcasted_iota(I32, (1, SWA_KV_HEADS * 128), 1) % 128 == SWA_HEAD_DIM).astype(F32)
    vs_ref[0] = (seg("vs") + ones_s).astype(BF16)

    g0, _ = _SEG["gl"]
    for p in range(6):
        gate_ref[0, :, p * 512:(p + 1) * 512] = _sigmoid(
            _dot(hb, w1_ref[:, g0 + p * 512:g0 + (p + 1) * 512])).astype(BF16)


def _input_stage(x_slab, b, modtab, g_mix, w1, g_q, wq, g_kv, wkn, wv, tabm, tabs):
    d = SLAB_ROWS * 128
    p = x_slab.shape[0] // (b * SLAB_ROWS)
    tm = TOKEN_TILE
    nj = p // tm
    tok = lambda w: pl.BlockSpec((1, tm, w), lambda bi, j: (bi, j, 0))
    full = lambda a: pl.BlockSpec(a.shape, lambda bi, j: (0,) * a.ndim)
    outs = [(1024, BF16), (1024, BF16), (1024, BF16), (512, F32), (512, BF16), (512, BF16), (256, BF16),
            (3072, BF16)]
    return pl.pallas_call(
        _in_kernel,
        grid=(b, nj),
        in_specs=[pl.BlockSpec((tm * SLAB_ROWS, 128), lambda bi, j: (bi * nj + j, 0)),
                  pl.BlockSpec((1, 1, 8, d), lambda bi, j: (bi, j // (nj - 1), 0, 0)),
                  full(g_mix), full(w1), full(g_q), full(wq), full(g_kv), full(wkn), full(wv),
                  pl.BlockSpec((tm, 256), lambda bi, j: (j, 0)),
                  pl.BlockSpec((tm, 256), lambda bi, j: (j, 0))],
        out_specs=[tok(w) for w, _ in outs],
        out_shape=[jax.ShapeDtypeStruct((b, p, w), dt) for w, dt in outs],
        compiler_params=_cparams(("arbitrary", "arbitrary")),
        name="input_stage",
    )(x_slab, modtab, g_mix, w1, g_q, wq, g_kv, wkn, wv, tabm, tabs)


def _mla_kernel(q_ref, k_ref, v_ref, *rest, tk, n_main, tail_rows):
    y_ref, m_ref, acc_ref = rest[-3:]
    hw = MLA_HEAD_PAD
    m_ref[...] = jnp.full(m_ref.shape, MASK_VALUE, F32)
    acc_ref[...] = jnp.zeros(acc_ref.shape, F32)

    def attend(r0, rows):
        for h in range(MLA_HEADS):
            q = q_ref[0, :, h * hw:(h + 1) * hw]
            k = k_ref[0, pl.ds(r0, rows), h * hw:(h + 1) * hw]
            v = v_ref[0, pl.ds(r0, rows), h * hw:(h + 1) * hw]
            s = _dot_nt(q, k)
            m_prev = m_ref[h]
            m_new = jnp.maximum(m_prev, jnp.max(s, axis=1, keepdims=True))
            m_ref[h] = m_new
            pr = jnp.exp2(s - jnp.tile(m_new, (1, rows // hw)))
            acc_ref[h] = jnp.exp2(m_prev - m_new) * acc_ref[h] + _dot(pr.astype(BF16), v)

    if n_main:
        def step(c, carry):
            attend(pl.multiple_of(c * tk, tk), tk)
            return carry

        lax.fori_loop(0, n_main, step, 0)
    if tail_rows:
        attend(n_main * tk, tail_rows)

    first = lax.broadcasted_iota(I32, (q_ref.shape[1], hw), 1) < MLA_V
    for hp in range(MLA_HEADS // 2):
        o = []
        for h in (2 * hp, 2 * hp + 1):
            acc = acc_ref[h]
            o.append(acc / acc[:, MLA_V:MLA_V + 1])
        y_ref[0, :, hp * hw:(hp + 1) * hw] = jnp.where(first, o[0], pltpu.roll(o[1], MLA_V, 1)).astype(BF16)


def _mla_attention(q, k, v, n_ctx):
    b, p, width = q.shape
    s_len = p - n_ctx
    tq = MLA_Q_TILE
    tk = MLA_KEY_CHUNK
    assert s_len % tq == 0 and s_len % tk == 0 and s_len % n_ctx == 0
    out_w = MLA_HEADS * MLA_V
    state = lambda rows: pltpu.VMEM((MLA_HEADS, rows, MLA_HEAD_PAD), F32)
    y = pl.pallas_call(
        functools.partial(_mla_kernel, tk=tk, n_main=p // tk, tail_rows=p % tk),
        grid=(b, s_len // tq),
        in_specs=[pl.BlockSpec((1, tq, width), lambda bi, j: (bi, j, 0)),
                  pl.BlockSpec((1, p, width), lambda bi, j: (bi, 0, 0), pipeline_mode=pl.Buffered(1)),
                  pl.BlockSpec((1, p, width), lambda bi, j: (bi, 0, 0), pipeline_mode=pl.Buffered(1))],
        out_specs=pl.BlockSpec((1, tq, out_w), lambda bi, j: (bi, j, 0)),
        out_shape=jax.ShapeDtypeStruct((b, p, out_w), BF16),
        scratch_shapes=[state(tq), state(tq)],
        compiler_params=_cparams(("arbitrary", "arbitrary")),
        name="mla_attention",
    )(q, k, v)
    cblk = s_len // n_ctx
    ctx_rows = lambda w: pl.BlockSpec((1, n_ctx, w), lambda bi: (bi, cblk, 0))
    return pl.pallas_call(
        functools.partial(_mla_kernel, tk=tk, n_main=0, tail_rows=n_ctx),
        grid=(b,),
        in_specs=[ctx_rows(width), ctx_rows(width), ctx_rows(width), pl.BlockSpec(memory_space=pl.ANY)],
        out_specs=ctx_rows(out_w),
        out_shape=jax.ShapeDtypeStruct((b, p, out_w), BF16),
        input_output_aliases={3: 0},
        scratch_shapes=[state(n_ctx), state(n_ctx)],
        compiler_params=_cparams(("arbitrary",)),
        name="mla_attention_ctx",
    )(q, k, v, y)


def _swa_kernel(sink_ref, q_ref, k_ref, v_ref, y_ref, *, n_ctx):
    j = pl.program_id(1)
    p_len = k_ref.shape[1]
    s_len = p_len - n_ctx
    band = 3 * SWA_BLOCK
    n_lat_tiles = s_len // SWA_BLOCK
    is_lat = j < n_lat_tiles
    n = jnp.minimum(j, n_lat_tiles - 1)
    ws = jnp.clip((n - 1) * SWA_BLOCK, 0, s_len - band)
    kstart = pl.multiple_of(ws, SWA_BLOCK)
    qpos = n * SWA_BLOCK + lax.broadcasted_iota(I32, (2 * SWA_BLOCK, band), 0) % SWA_BLOCK
    kpos = ws + lax.broadcasted_iota(I32, (2 * SWA_BLOCK, band), 1)
    valid = (jnp.abs(qpos - kpos) <= SWA_WINDOW) & is_lat
    first = lax.broadcasted_iota(I32, (SWA_BLOCK, 128), 1) < SWA_HEAD_DIM
    top = lax.broadcasted_iota(I32, (2 * SWA_BLOCK, 1), 0) < SWA_BLOCK
    for hk in range(SWA_KV_HEADS):
        qpair = jnp.concatenate([q_ref[0, :, (2 * hk) * 128:(2 * hk + 1) * 128],
                                 q_ref[0, :, (2 * hk + 1) * 128:(2 * hk + 2) * 128]], axis=0)
        vcols = slice(hk * 128, (hk + 1) * 128)
        vc = v_ref[0, s_len:p_len, vcols]
        vb = v_ref[0, pl.ds(kstart, band), vcols]
        res = {}
        for par in range(2):
            kcols = slice((2 * hk + par) * 128, (2 * hk + par + 1) * 128)
            s_c = _dot_nt(qpair, k_ref[0, s_len:p_len, kcols])
            s_b = jnp.where(valid, _dot_nt(qpair, k_ref[0, pl.ds(kstart, band), kcols]), MASK_VALUE)
            h_top = 4 * hk + par
            h_bot = 4 * hk + 2 + par
            sink = jnp.where(top, sink_ref[h_top] * LOG2_E, sink_ref[h_bot] * LOG2_E)
            m = jnp.maximum(jnp.maximum(jnp.max(s_c, axis=1, keepdims=True), jnp.max(s_b, axis=1, keepdims=True)),
                            sink)
            o = _dot(jnp.exp2(s_c - m).astype(BF16), vc) + _dot(jnp.exp2(s_b - m).astype(BF16), vb)
            o = o / (o[:, SWA_HEAD_DIM:SWA_HEAD_DIM + 1] + jnp.exp2(sink - m))
            res[h_top] = o[0:SWA_BLOCK]
            res[h_bot] = o[SWA_BLOCK:2 * SWA_BLOCK]
        for pair in (2 * hk, 2 * hk + 1):
            y_ref[0, :, pair * 128:(pair + 1) * 128] = jnp.where(
                first, res[2 * pair], pltpu.roll(res[2 * pair + 1], SWA_HEAD_DIM, 1)).astype(BF16)


def _swa_attention(sink, qs, ks, vs, n_ctx):
    b, p, _ = qs.shape
    tq = SWA_BLOCK
    return pl.pallas_call(
        functools.partial(_swa_kernel, n_ctx=n_ctx),
        grid=(b, p // tq),
        in_specs=[pl.BlockSpec(memory_space=pltpu.SMEM),
                  pl.BlockSpec((1, tq, SWA_WIDTH), lambda bi, j: (bi, j, 0)),
                  pl.BlockSpec((1, p, ks.shape[2]), lambda bi, j: (bi, 0, 0)),
                  pl.BlockSpec((1, p, vs.shape[2]), lambda bi, j: (bi, 0, 0))],
        out_specs=pl.BlockSpec((1, tq, SWA_WIDTH), lambda bi, j: (bi, j, 0)),
        out_shape=jax.ShapeDtypeStruct((b, p, SWA_WIDTH), BF16),
        compiler_params=_cparams(("arbitrary", "arbitrary")),
        name="swa_attention",
    )(sink, qs, ks, vs)


def _pool_kernel(prev_ref, cur_ref, next_ref, w_ref, scale_ref, y_ref, ext_ref, *, n_ctx):
    j = pl.program_id(1)
    tm = cur_ref.shape[1]
    nj = pl.num_programs(1)
    s_len = (nj - 1) * tm
    is_ctx = j == nj - 1
    has_prev = (j >= 1) & (j < nj - 1)
    has_next = j < nj - 2
    ext_ref[0:POOL_HALO, :] = jnp.where(has_prev, prev_ref[0], 0.0)
    ext_ref[POOL_HALO:POOL_HALO + tm, :] = cur_ref[0]
    ext_ref[POOL_HALO + tm:POOL_HALO + tm + POOL_HALO, :] = jnp.where(has_next, next_ref[0], 0.0)
    t = lax.broadcasted_iota(I32, (tm, 1), 0)
    pos = jnp.where(is_ctx, t, j * tm + t)
    seg_len = jnp.where(is_ctx, n_ctx, s_len)
    for g, w in enumerate(POOL_WINDOWS):
        cols = slice(g * POOL_GROUP_DIM, (g + 1) * POOL_GROUP_DIM)
        acc = jnp.zeros((tm, POOL_GROUP_DIM), F32)
        for off in range(-(w // 2), w - w // 2):
            acc = acc + ext_ref[POOL_HALO + off:POOL_HALO + off + tm, cols]
        lo = jnp.maximum(pos - w // 2, 0)
        hi = jnp.minimum(pos + w - w // 2, seg_len)
        cnt = (hi - lo).astype(F32)
        pooled = acc / cnt - cur_ref[0, :, cols]
        y_ref[0, :, cols] = (_dot(pooled.astype(BF16), w_ref[g]) * scale_ref[:, cols]).astype(BF16)


def _pool_stage(u, w_pool, pool_scale, n_ctx):
    b, p, width = u.shape
    tm = TOKEN_TILE
    hb = tm // POOL_HALO
    n_halo_blocks = p // POOL_HALO
    return pl.pallas_call(
        functools.partial(_pool_kernel, n_ctx=n_ctx),
        grid=(b, p // tm),
        in_specs=[pl.BlockSpec((1, POOL_HALO, width), lambda bi, j: (bi, jnp.maximum(j * hb - 1, 0), 0)),
                  pl.BlockSpec((1, tm, width), lambda bi, j: (bi, j, 0)),
                  pl.BlockSpec((1, POOL_HALO, width),
                               lambda bi, j: (bi, jnp.minimum((j + 1) * hb, n_halo_blocks - 1), 0)),
                  pl.BlockSpec(w_pool.shape, lambda bi, j: (0, 0, 0)),
                  pl.BlockSpec(pool_scale.shape, lambda bi, j: (0, 0))],
        out_specs=pl.BlockSpec((1, tm, width), lambda bi, j: (bi, j, 0)),
        out_shape=jax.ShapeDtypeStruct((b, p, width), BF16),
        scratch_shapes=[pltpu.VMEM((tm + 2 * POOL_HALO, width), F32)],
        compiler_params=_cparams(("arbitrary", "arbitrary")),
        name="pool_stage",
    )(u, u, u, w_pool, pool_scale)


def _merge_kernel(ym_ref, yp_ref, ys_ref, gate_ref, x_ref, mod_ref, g_ref, wbm_ref, wbp_ref, wbs_ref, wout_ref,
                  wsg_ref, wsu_ref, wsd_ref, wrt_ref, xs_ref, fp_ref, lg_ref):
    d = SLAB_ROWS * 128
    mod = mod_ref[0, 0]
    gate = gate_ref[0]
    merged = (gate[:, 0:d].astype(F32) * _dot(ym_ref[0], wbm_ref[...])
              + gate[:, d:2 * d].astype(F32) * _dot(yp_ref[0], wbp_ref[...])
              + gate[:, 2 * d:3 * d].astype(F32) * _dot(ys_ref[0], wbs_ref[...]))
    x = _slab_rows_to_matrix(x_ref, gate.shape[0], SLAB_ROWS)
    x_mid = x + mod[2:3] * _dot(merged.astype(BF16), wout_ref[...])
    f = _rms(x_mid, g_ref[...]) * (1.0 + mod[4:5]) + mod[3:4]
    fb = f.astype(BF16)
    gsh = _dot(fb, wsg_ref[...])
    shared = _dot((gsh * _sigmoid(gsh) * _dot(fb, wsu_ref[...])).astype(BF16), wsd_ref[...])
    _matrix_to_slab_rows(xs_ref, x_mid + mod[5:6] * shared, SLAB_ROWS)
    _matrix_to_slab_rows(fp_ref, _pack_bf16_pair(f), PACKED_ROWS)
    lg_ref[...] = _dot_nt(wrt_ref[...], fb)


def _merge_stage(ym, yp, ys, gates, x_slab, modtab, g_ffn, wbm, wbp, wbs, wout, wsg, wsu, wsd, wrt):
    b, p, _ = ym.shape
    d = SLAB_ROWS * 128
    tm = TOKEN_TILE
    nj = p // tm
    tok = lambda w: pl.BlockSpec((1, tm, w), lambda bi, j: (bi, j, 0))
    slab = lambda rows: pl.BlockSpec((tm * rows, 128), lambda bi, j: (bi * nj + j, 0))
    full = lambda a: pl.BlockSpec(a.shape, lambda bi, j: (0,) * a.ndim)
    return pl.pallas_call(
        _merge_kernel,
        grid=(b, nj),
        in_specs=[tok(ym.shape[2]), tok(yp.shape[2]), tok(ys.shape[2]), tok(gates.shape[2]), slab(SLAB_ROWS),
                  pl.BlockSpec((1, 1, 8, d), lambda bi, j: (bi, j // (nj - 1), 0, 0)),
                  full(g_ffn), full(wbm), full(wbp), full(wbs), full(wout), full(wsg), full(wsu), full(wsd),
                  full(wrt)],
        out_specs=[slab(SLAB_ROWS), slab(PACKED_ROWS),
                   pl.BlockSpec((N_EXPERTS, tm), lambda bi, j: (0, bi * nj + j))],
        out_shape=[jax.ShapeDtypeStruct((b * p * SLAB_ROWS, 128), F32),
                   jax.ShapeDtypeStruct((b * p * PACKED_ROWS, 128), U32),
                   jax.ShapeDtypeStruct((N_EXPERTS, b * p), F32)],
        compiler_params=_cparams(("arbitrary", "arbitrary")),
        name="merge_stage",
    )(ym, yp, ys, gates, x_slab, modtab, g_ffn, wbm, wbp, wbs, wout, wsg, wsu, wsd, wrt)


def _route_kernel(lg_ref, bias_ref, dest_ref, wts_ref, seg_ref):
    tm = lg_ref.shape[1]
    ne = N_EXPERTS
    neg_inf = -jnp.inf
    scores = _sigmoid(lg_ref[...])
    sel = scores + bias_ref[...]
    iota_g = lax.broadcasted_iota(I32, (EXPERTS_PER_GROUP, tm), 0)
    gscore = []
    for g in range(N_GROUPS):
        sg = sel[g * EXPERTS_PER_GROUP:(g + 1) * EXPERTS_PER_GROUP]
        m1 = jnp.max(sg, axis=0, keepdims=True)
        i1 = jnp.min(jnp.where(sg == m1, iota_g, EXPERTS_PER_GROUP), axis=0, keepdims=True)
        m2 = jnp.max(jnp.where(iota_g == i1, neg_inf, sg), axis=0, keepdims=True)
        gscore.append(m1 + m2)
    rows = []
    for g in range(N_GROUPS):
        rank = jnp.zeros((1, tm), I32)
        for g2 in range(N_GROUPS):
            if g2 == g:
                continue
            beats = (gscore[g2] >= gscore[g]) if g2 < g else (gscore[g2] > gscore[g])
            rank = rank + beats.astype(I32)
        rows.append(jnp.where(rank < TOPK_GROUPS, sel[g * EXPERTS_PER_GROUP:(g + 1) * EXPERTS_PER_GROUP], neg_inf))
    cur = jnp.concatenate(rows, axis=0)
    iota_e = lax.broadcasted_iota(I32, (ne, tm), 0)
    picks = []
    member = jnp.zeros((ne, tm), F32)
    for _ in range(TOP_K):
        m = jnp.max(cur, axis=0, keepdims=True)
        idx = jnp.min(jnp.where(cur == m, iota_e, ne), axis=0, keepdims=True)
        hit = iota_e == idx
        picks.append(hit)
        member = member + hit.astype(F32)
        cur = jnp.where(hit, neg_inf, cur)
    earlier = (lax.broadcasted_iota(I32, (tm, tm), 0) < lax.broadcasted_iota(I32, (tm, tm), 1)).astype(BF16)
    pos = _dot(member.astype(BF16), earlier)
    cnt_col = jnp.sum(member, axis=1, keepdims=True)
    blocks_col = jnp.floor((cnt_col + (SEG_ALIGN - 1)) * (1.0 / SEG_ALIGN))
    lower = (lax.broadcasted_iota(I32, (ne, ne), 1) < lax.broadcasted_iota(I32, (ne, ne), 0)).astype(BF16)
    off_col = _dot(lower, jnp.broadcast_to(blocks_col, (ne, 128)).astype(BF16))[:, 0:1] * SEG_ALIGN
    base = off_col + pos
    w_rows = [jnp.sum(jnp.where(hit, scores, 0.0), axis=0, keepdims=True) for hit in picks]
    denom = w_rows[0]
    for w in w_rows[1:]:
        denom = denom + w
    for k, hit in enumerate(picks):
        dest_ref[0, k:k + 1, :] = jnp.sum(jnp.where(hit, base, 0.0), axis=0, keepdims=True).astype(I32)
        wts_ref[0, k:k + 1, :] = w_rows[k] / denom * ROUTED_SCALE
    member_pad = jnp.concatenate([member, jnp.zeros((128 - ne, tm), F32)], axis=0).astype(BF16)
    cnt_row = _dot_nt(jnp.ones((8, tm), BF16), member_pad)
    blocks_row = jnp.floor((cnt_row + (SEG_ALIGN - 1)) * (1.0 / SEG_ALIGN))
    before = (lax.broadcasted_iota(I32, (128, 128), 0) < lax.broadcasted_iota(I32, (128, 128), 1)).astype(BF16)
    off_row = _dot(blocks_row.astype(BF16), before) * SEG_ALIGN
    r = lax.broadcasted_iota(I32, (8, 128), 0)
    seg_ref[0] = jnp.where(r == 0, off_row, jnp.where(r == 1, cnt_row, 0.0)).astype(I32)


def _route_stage(lg_t, bias_col):
    ne, t_all = lg_t.shape
    tm = MOE_TILE
    nt = t_all // tm
    return pl.pallas_call(
        _route_kernel,
        grid=(nt,),
        in_specs=[pl.BlockSpec((ne, tm), lambda i: (0, i)),
                  pl.BlockSpec((ne, 1), lambda i: (0, 0))],
        out_specs=[pl.BlockSpec((1, TOP_K, tm), lambda i: (i, 0, 0)),
                   pl.BlockSpec((1, TOP_K, tm), lambda i: (i, 0, 0)),
                   pl.BlockSpec((1, 8, 128), lambda i: (i, 0, 0))],
        out_shape=[jax.ShapeDtypeStruct((nt, TOP_K, tm), I32),
                   jax.ShapeDtypeStruct((nt, TOP_K, tm), F32),
                   jax.ShapeDtypeStruct((nt, 8, 128), I32)],
        compiler_params=_cparams(("arbitrary",)),
        name="route_stage",
    )(lg_t, bias_col)


def _moe_row_stride(tm):
    cap = TOP_K * tm + N_EXPERTS * SEG_ALIGN + MOE_CHUNK
    blocks = cap // 8 + 1
    return 8 * (blocks + 1 - blocks % 2)


def _moe_kernel(dest_ref, wts_ref, seg_ref, fp_ref, xs_ref, gt_ref, wg_ref, wu_ref, wd_ref, out_ref, buf_ref,
                *, srow, sub, n_sub_per_batch):
    i = pl.program_id(0)
    eg = pl.program_id(1)
    tm = dest_ref.shape[2]
    ch = MOE_CHUNK
    group = wg_ref.shape[0]
    lane_blk = 128

    def table_block(ref, blk):
        return ref.at[0, :, pl.ds(pl.multiple_of(blk * lane_blk, lane_blk), lane_blk)]

    @pl.when(eg == 0)
    def _group_rows():
        buf_ref[...] = jnp.zeros_like(buf_ref)

        def body(blk, carry):
            dest = table_block(dest_ref, blk)
            rows = fp_ref.at[pl.ds(pl.multiple_of(blk * (lane_blk * 4), lane_blk * 4), lane_blk * 4), :]
            for u in range(lane_blk):
                slab = rows[u * 4:(u + 1) * 4, :]
                for k in range(TOP_K):
                    buf_ref[pl.ds(dest[k, u], 4, stride=srow), :] = slab
            return carry

        lax.fori_loop(0, tm // lane_blk, body, 0)

    for ge in range(group):
        e = eg * group + ge
        n_rows = seg_ref[0, 1, e]
        seg_off = seg_ref[0, 0, e]

        def ffn_rows(r0, n_valid, rows, ge=ge):
            words = [buf_ref[pl.ds(pl.multiple_of(q * srow + r0, 8), rows), :] for q in range(4)]
            w = jnp.concatenate(words, axis=1)
            xb = jnp.concatenate([_unpack_lo(w), _unpack_hi(w)], axis=1).astype(BF16)
            g = _dot(xb, wg_ref[ge])
            hmid = g * _sigmoid(g) * _dot(xb, wu_ref[ge])
            y = _dot(hmid.astype(BF16), wd_ref[ge])
            packed = _pack_bf16_pair(y)
            keep_new = lax.broadcasted_iota(I32, (rows, 1), 0) < n_valid
            for q in range(4):
                buf_ref[pl.ds(pl.multiple_of(q * srow + r0, 8), rows), :] = jnp.where(
                    keep_new, packed[:, q * 128:(q + 1) * 128], words[q])

        n_full = n_rows // ch
        rem = n_rows - n_full * ch

        def chunk(c, carry, ffn_rows=ffn_rows, seg_off=seg_off):
            ffn_rows(seg_off + c * ch, ch, ch)
            return carry

        lax.fori_loop(0, n_full, chunk, 0)
        lo = 0
        for rows in (ch // 2, 3 * ch // 4, ch):
            pl.when((rem > lo) & (rem <= rows))(functools.partial(ffn_rows, seg_off + n_full * ch, rem, rows))
            lo = rows

    @pl.when(eg == pl.num_programs(1) - 1)
    def _combine():
        def body(blk, carry):
            sub_blk = (i * tm + blk * lane_blk) // sub
            bi = sub_blk // n_sub_per_batch
            gate = gt_ref[bi * 2 + (sub_blk - bi * n_sub_per_batch) // (n_sub_per_batch - 1)]
            dest = table_block(dest_ref, blk)
            wts = table_block(wts_ref, blk)
            base = pl.multiple_of(blk * (lane_blk * 8), lane_blk * 8)
            xs = xs_ref.at[pl.ds(base, lane_blk * 8), :]
            out = out_ref.at[pl.ds(base, lane_blk * 8), :]
            for u in range(lane_blk):
                acc_lo = jnp.zeros((4, 128), F32)
                acc_hi = jnp.zeros((4, 128), F32)
                for k in range(TOP_K):
                    words = buf_ref[pl.ds(dest[k, u], 4, stride=srow), :]
                    wk = wts[k, u]
                    acc_lo = acc_lo + wk * _unpack_lo(words)
                    acc_hi = acc_hi + wk * _unpack_hi(words)
                out[u * 8:u * 8 + 4, :] = xs[u * 8:u * 8 + 4, :] + gate[0:4] * acc_lo
                out[u * 8 + 4:u * 8 + 8, :] = xs[u * 8 + 4:u * 8 + 8, :] + gate[4:8] * acc_hi
            return carry

        lax.fori_loop(0, tm // lane_blk, body, 0)


def _moe_stage(dest, wts, seg, fp4, xs8, gt2, wg, wu, wd, n_sub_per_batch):
    nt, _, tm = dest.shape
    ne = wg.shape[0]
    group = MOE_EXPERTS_PER_STEP
    assert ne % group == 0 and tm % 128 == 0 and TOKEN_TILE % 128 == 0
    srow = _moe_row_stride(tm)
    smem = lambda shape: pl.BlockSpec(shape, lambda i, e: (i, 0, 0), memory_space=pltpu.SMEM)
    once = dict(pipeline_mode=pl.Buffered(1))
    return pl.pallas_call(
        functools.partial(_moe_kernel, srow=srow, sub=TOKEN_TILE, n_sub_per_batch=n_sub_per_batch),
        grid=(nt, ne // group),
        in_specs=[smem((1, TOP_K, tm)), smem((1, TOP_K, tm)), smem((1, 8, 128)),
                  pl.BlockSpec((tm * 4, 128), lambda i, e: (i, 0), **once),
                  pl.BlockSpec((tm * 8, 128), lambda i, e: (i, 0), **once),
                  pl.BlockSpec(gt2.shape, lambda i, e: (0, 0, 0)),
                  pl.BlockSpec((group,) + wg.shape[1:], lambda i, e: (e, 0, 0)),
                  pl.BlockSpec((group,) + wu.shape[1:], lambda i, e: (e, 0, 0)),
                  pl.BlockSpec((group,) + wd.shape[1:], lambda i, e: (e, 0, 0))],
        out_specs=pl.BlockSpec((tm * 8, 128), lambda i, e: (i, 0)),
        out_shape=jax.ShapeDtypeStruct(xs8.shape, F32),
        scratch_shapes=[pltpu.VMEM((4 * srow, 128), U32)],
        compiler_params=_cparams(("arbitrary", "arbitrary")),
        name="moe_stage",
    )(dest, wts, seg, fp4, xs8, gt2, wg, wu, wd)


def _final_kernel(x_ref, g_ref, o_ref):
    o_ref[0] = _rms(_slab_rows_to_matrix(x_ref, o_ref.shape[1], SLAB_ROWS), g_ref[...])


def _final_norm(x_slab, b, g_final, n_ctx):
    d = SLAB_ROWS * 128
    p = x_slab.shape[0] // (b * SLAB_ROWS)
    tm = TOKEN_TILE
    nj = p // tm
    return pl.pallas_call(
        _final_kernel,
        grid=(b, (p - n_ctx) // tm),
        in_specs=[pl.BlockSpec((tm * SLAB_ROWS, 128), lambda bi, j: (bi * nj + j, 0)),
                  pl.BlockSpec((1, d), lambda bi, j: (0, 0))],
        out_specs=pl.BlockSpec((1, tm, d), lambda bi, j: (bi, j, 0)),
        out_shape=jax.ShapeDtypeStruct((b, p - n_ctx, d), F32),
        compiler_params=_cparams(("arbitrary", "arbitrary")),
        name="final_norm",
    )(x_slab, g_final)


def _rot_cols(w, n_heads, rot_dim):
    kdim = w.shape[0]
    w4 = w.reshape(kdim, n_heads, 4, rot_dim // 4)
    rot = jnp.stack([-w4[:, :, 1], w4[:, :, 0], -w4[:, :, 3], w4[:, :, 2]], axis=2)
    return rot.reshape(kdim, n_heads * rot_dim)


def _rope_pattern(s_len, n_ctx, rot_dim):
    t = jnp.arange(s_len)
    row = (t // GRID_W).astype(F32)
    col = (t % GRID_W).astype(F32)
    n_freq = rot_dim // 4
    inv_freq = ROPE_BASE ** (-jnp.arange(n_freq, dtype=F32) / n_freq)
    ang_r = row[:, None] * inv_freq[None, :]
    ang_c = col[:, None] * inv_freq[None, :]
    cos = jnp.concatenate([jnp.cos(ang_r), jnp.cos(ang_r), jnp.cos(ang_c), jnp.cos(ang_c)], axis=1)
    sin = jnp.concatenate([jnp.sin(ang_r), jnp.sin(ang_r), jnp.sin(ang_c), jnp.sin(ang_c)], axis=1)
    cos = jnp.concatenate([cos, jnp.ones((n_ctx, rot_dim), F32)], axis=0)
    sin = jnp.concatenate([sin, jnp.zeros((n_ctx, rot_dim), F32)], axis=0)
    return cos, sin


def _layer_weights(w_in, w_uq, w_ukv):
    d = w_in.shape[0]
    offs = [0]
    for w in (MLA_Q_LORA, MLA_KV_LORA, MLA_ROPE, POOL_WIDTH, SWA_WIDTH, SWA_KV_WIDTH, SWA_KV_WIDTH, 3 * d):
        offs.append(offs[-1] + w)
    cq, ckv, kr, u, qs, ks, vs, gl = (w_in[:, offs[i]:offs[i + 1]] for i in range(8))
    qs = qs * (SWA_SCALE * LOG2_E)
    zeros = lambda n: jnp.zeros((d, n), F32)
    kr_slot = jnp.concatenate([zeros(MLA_NOPE), kr, zeros(MLA_HEAD_PAD - MLA_NOPE - MLA_ROPE)], axis=1)
    kr_rot_slot = jnp.concatenate([zeros(MLA_NOPE), _rot_cols(kr, 1, MLA_ROPE),
                                   zeros(MLA_HEAD_PAD - MLA_NOPE - MLA_ROPE)], axis=1)

    def kv_slots(w):
        w3 = w.reshape(d, SWA_KV_HEADS, SWA_HEAD_DIM)
        return jnp.concatenate([w3, jnp.zeros_like(w3)], axis=2).reshape(d, SWA_KV_HEADS * 128)

    w1 = jnp.concatenate([cq, ckv, u, qs, _rot_cols(qs, SWA_Q_HEADS, SWA_HEAD_DIM), kv_slots(ks),
                          kv_slots(_rot_cols(ks, SWA_KV_HEADS, SWA_HEAD_DIM)), kv_slots(vs), kr_slot, kr_rot_slot,
                          gl], axis=1).astype(BF16)
    lq = w_uq.shape[0]
    wq3 = (w_uq * (MLA_SCALE * LOG2_E)).reshape(lq, MLA_HEADS, MLA_NOPE + MLA_ROPE)
    nope, rope = wq3[:, :, :MLA_NOPE], wq3[:, :, MLA_NOPE:]
    rope_rot = _rot_cols(rope.reshape(lq, MLA_HEADS * MLA_ROPE), MLA_HEADS, MLA_ROPE).reshape(lq, MLA_HEADS, MLA_ROPE)
    pad = jnp.zeros((lq, MLA_HEADS, MLA_HEAD_PAD - MLA_NOPE - MLA_ROPE), F32)
    wq_a = jnp.concatenate([nope, rope, pad], axis=2).reshape(lq, MLA_HEADS * MLA_HEAD_PAD)
    wq_b = jnp.concatenate([jnp.zeros_like(nope), rope_rot, pad], axis=2).reshape(lq, MLA_HEADS * MLA_HEAD_PAD)
    wq = jnp.concatenate([wq_a, wq_b], axis=1).astype(BF16)
    lkv = w_ukv.shape[0]
    wkv3 = w_ukv.reshape(lkv, MLA_HEADS, MLA_NOPE + MLA_V)
    wkn = jnp.concatenate([wkv3[:, :, :MLA_NOPE], jnp.zeros((lkv, MLA_HEADS, MLA_HEAD_PAD - MLA_NOPE), F32)],
                          axis=2).reshape(lkv, MLA_HEADS * MLA_HEAD_PAD).astype(BF16)
    wv = jnp.concatenate([wkv3[:, :, MLA_NOPE:], jnp.zeros((lkv, MLA_HEADS, MLA_HEAD_PAD - MLA_V), F32)],
                         axis=2).reshape(lkv, MLA_HEADS * MLA_HEAD_PAD).astype(BF16)
    return w1, wq, wkn, wv


def kernel(x, c, ctx, c_ctx, w_mod, b_mod, g_mix, g_ffn, w_in, g_mla_q, g_mla_kv, w_mla_uq, w_mla_ukv, w_pool,
           pool_scale, swa_sink, w_br_mla, w_br_pool, w_br_swa, w_out, w_router, router_bias, w_exp_gate,
           w_exp_up, w_exp_down, w_sh_gate, w_sh_up, w_sh_down, g_final):
    b, s_len, d = x.shape
    n_ctx = ctx.shape[1]
    n_layers = w_mod.shape[0]
    p = n_ctx + s_len
    assert n_ctx == TOKEN_TILE and s_len % TOKEN_TILE == 0 and (b * p) % MOE_TILE == 0 and b + 1 <= 8
    assert d == 1024 and w_in.shape[2] == (MLA_Q_LORA + MLA_KV_LORA + MLA_ROPE + POOL_WIDTH + SWA_WIDTH
                                           + 2 * SWA_KV_WIDTH + 3 * d)

    cm, sm = _rope_pattern(s_len, n_ctx, MLA_ROPE)
    tail = jnp.zeros((p, MLA_HEAD_PAD - MLA_NOPE - MLA_ROPE), F32)
    tabm = jnp.concatenate([jnp.ones((p, MLA_NOPE), F32), cm, tail, jnp.zeros((p, MLA_NOPE), F32), sm, tail], axis=1)
    cs, ss = _rope_pattern(s_len, n_ctx, SWA_HEAD_DIM)
    tabs = jnp.concatenate([cs, cs, ss, ss], axis=1)

    cvec = jnp.concatenate([c, c_ctx[None, :], jnp.zeros((8 - b - 1, d), F32)], axis=0)
    mod_all = _modulation(cvec, w_mod, b_mod).reshape(n_layers, 8, N_MOD, d)

    x_slab = jnp.concatenate([x, ctx], axis=1).reshape(b * p * SLAB_ROWS, 128)
    for i in range(n_layers):
        lat = mod_all[i, :b]
        ctx_rows = jnp.broadcast_to(mod_all[i, b][None], (b, N_MOD, d))
        modtab = jnp.pad(jnp.stack([lat, ctx_rows], axis=1), ((0, 0), (0, 0), (0, 8 - N_MOD), (0, 0)))
        w1, wq, wkn, wv = _layer_weights(w_in[i], w_mla_uq[i], w_mla_ukv[i])
        q, k, v, u, qs, ks, vs, gates = _input_stage(
            x_slab, b, modtab, g_mix[i][None], w1, g_mla_q[i][None], wq, g_mla_kv[i][None], wkn, wv, tabm, tabs)
        y_mla = _mla_attention(q, k, v, n_ctx)
        y_swa = _swa_attention(swa_sink[i], qs, ks, vs, n_ctx)
        y_pool = _pool_stage(u, w_pool[i].astype(BF16), pool_scale[i][None], n_ctx)
        xs, fp, lg_t = _merge_stage(
            y_mla, y_pool, y_swa, gates, x_slab, modtab, g_ffn[i][None], w_br_mla[i].astype(BF16),
            w_br_pool[i].astype(BF16), w_br_swa[i].astype(BF16), w_out[i].astype(BF16),
            w_sh_gate[i].astype(BF16), w_sh_up[i].astype(BF16), w_sh_down[i].astype(BF16),
            w_router[i].T.astype(BF16))
        dest, wts, seg = _route_stage(lg_t, router_bias[i][:, None])
        gt2 = modtab[:, :, 5, :].reshape(b * 2, 8, d // 8)
        x_slab = _moe_stage(dest, wts, seg, fp, xs, gt2, w_exp_gate[i].astype(BF16), w_exp_up[i].astype(BF16),
                            w_exp_down[i].astype(BF16), p // TOKEN_TILE)
    return _final_norm(x_slab, b, g_final[None], n_ctx)
```

```python
import functools

import jax
import jax.numpy as jnp
from jax import lax
from jax.experimental import pallas as pl
from jax.experimental.pallas import tpu as pltpu

F32 = jnp.float32
BF16 = jnp.bfloat16
U32 = jnp.uint32
I32 = jnp.int32

NORM_EPS = 1e-6
ROPE_BASE = 10000.0
GRID_W = 64
N_MOD = 6

MLA_HEADS = 8
MLA_Q_LORA = 384
MLA_KV_LORA = 256
MLA_NOPE = 64
MLA_ROPE = 32
MLA_V = 64
MLA_SCALE = (MLA_NOPE + MLA_ROPE) ** -0.5
MLA_HEAD_PAD = 128
LOG2_E = 1.4426950408889634

POOL_WINDOWS = (2, 4, 8, 16)
POOL_GROUP_DIM = 128
POOL_WIDTH = 512
POOL_HALO = 8

SWA_Q_HEADS = 8
SWA_KV_HEADS = 2
SWA_HEAD_DIM = 64
SWA_WINDOW = 128
SWA_BLOCK = 128
SWA_SCALE = SWA_HEAD_DIM ** -0.5
SWA_WIDTH = SWA_Q_HEADS * SWA_HEAD_DIM
SWA_KV_WIDTH = SWA_KV_HEADS * SWA_HEAD_DIM

N_EXPERTS = 64
TOP_K = 8
N_GROUPS = 8
TOPK_GROUPS = 4
EXPERTS_PER_GROUP = 8
D_EXPERT = 256
ROUTED_SCALE = 2.5

TOKEN_TILE = 256
MLA_Q_TILE = 1024
MLA_KEY_CHUNK = 256
MOE_TILE = 1024
MOE_CHUNK = 256
MOE_EXPERTS_PER_STEP = 2
MOE_TILES_PER_STEP = 2
SEG_ALIGN = 8
SLAB_ROWS = 8
PACKED_ROWS = 4
MASK_VALUE = -1e30
HI16 = 0xFFFF0000

VMEM_LIMIT = 56 * 1024 * 1024

_SEG_WIDTHS = (("cq", 384), ("ckv", 256), ("u", 512), ("qs", 512), ("qs_rot", 512), ("ks", 256),
               ("ks_rot", 256), ("vs", 256), ("kr", 128), ("kr_rot", 128), ("gl", 3072))
_SEG = {}
_o = 0
for _n, _w in _SEG_WIDTHS:
    _SEG[_n] = (_o, _o + _w)
    _o += _w
FUSED_IN_WIDTH = _o


def _cparams(sem):
    return pltpu.CompilerParams(dimension_semantics=sem, vmem_limit_bytes=VMEM_LIMIT)


def _dot(a, b):
    return jnp.dot(a, b, preferred_element_type=F32)


def _dot_nt(a, b):
    return lax.dot_general(a, b, (((1,), (1,)), ((), ())), preferred_element_type=F32)


def _sigmoid(x):
    return 1.0 / (1.0 + jnp.exp(-x))


def _rms(x, g):
    return x * lax.rsqrt(jnp.mean(x * x, axis=-1, keepdims=True) + NORM_EPS) * g


def _pack_bf16_pair(v):
    n = v.shape[1] // 2
    bits = pltpu.bitcast(v.astype(BF16).astype(F32), U32)
    return (bits[:, :n] >> 16) | (bits[:, n:] & jnp.uint32(HI16))


def _slab_rows_to_matrix(ref, n_tok, per_tok):
    return jnp.concatenate([ref[pl.ds(c, n_tok, stride=per_tok), :] for c in range(per_tok)], axis=1)


def _matrix_to_slab_rows(ref, val, per_tok):
    n_tok = val.shape[0]
    for c in range(per_tok):
        ref[pl.ds(c, n_tok, stride=per_tok), :] = val[:, c * 128:(c + 1) * 128]


def _unpack_lo(w):
    return pltpu.bitcast(w << 16, F32)


def _unpack_hi(w):
    return pltpu.bitcast(w & jnp.uint32(HI16), F32)


def _mod_kernel(c_ref, w_ref, b_ref, o_ref):
    c = c_ref[...]
    a = (c * _sigmoid(c)).astype(BF16)
    o_ref[0] = _dot(a, w_ref[0].astype(BF16)) + b_ref[0]


def _modulation(cvec, w_mod, b_mod):
    n_layers, d, width = w_mod.shape
    tn = width // 4
    return pl.pallas_call(
        _mod_kernel,
        grid=(n_layers, width // tn),
        in_specs=[pl.BlockSpec((8, d), lambda l, n: (0, 0)),
                  pl.BlockSpec((1, d, tn), lambda l, n: (l, 0, n)),
                  pl.BlockSpec((1, 1, tn), lambda l, n: (l, 0, n))],
        out_specs=pl.BlockSpec((1, 8, tn), lambda l, n: (l, 0, n)),
        out_shape=jax.ShapeDtypeStruct((n_layers, 8, width), F32),
        compiler_params=_cparams(("arbitrary", "arbitrary")),
        name="modulation",
    )(cvec, w_mod, b_mod.reshape(n_layers, 1, width))


def _in_kernel(x_ref, mod_ref, g_ref, w1_ref, gq_ref, wq_ref, gkv_ref, wkn_ref, wv_ref, tabm_ref, tabs_ref,
               q_ref, k_ref, v_ref, u_ref, qs_ref, ks_ref, vs_ref, gate_ref):
    x = _slab_rows_to_matrix(x_ref, q_ref.shape[1], SLAB_ROWS)
    mod = mod_ref[0, 0]
    h = _rms(x, g_ref[...]) * (1.0 + mod[1:2]) + mod[0:1]
    hb = h.astype(BF16)

    def seg(name):
        a, b = _SEG[name]
        return _dot(hb, w1_ref[:, a:b])

    cos_m = tabm_ref[:, 0:128]
    sin_m = tabm_ref[:, 128:256]
    cos_s = tabs_ref[:, 0:128]
    sin_s = tabs_ref[:, 128:256]

    cqn = _rms(seg("cq"), gq_ref[...]).astype(BF16)
    qa = _dot(cqn, wq_ref[:, 0:1024])
    qb = _dot(cqn, wq_ref[:, 1024:2048])
    q_ref[0] = (qa * jnp.tile(cos_m, (1, MLA_HEADS)) + qb * jnp.tile(sin_m, (1, MLA_HEADS))).astype(BF16)

    ckvn = _rms(seg("ckv"), gkv_ref[...]).astype(BF16)
    kr = seg("kr") * cos_m + seg("kr_rot") * sin_m
    k_ref[0] = (_dot(ckvn, wkn_ref[...]) + jnp.tile(kr, (1, MLA_HEADS))).astype(BF16)
    ones_lane = (lax.broadcasted_iota(I32, (1, MLA_HEADS * MLA_HEAD_PAD), 1) % MLA_HEAD_PAD == MLA_V).astype(F32)
    v_ref[0] = (_dot(ckvn, wv_ref[...]) + ones_lane).astype(BF16)

    u_ref[0] = seg("u")

    qs_ref[0] = (seg("qs") * jnp.tile(cos_s, (1, 4)) + seg("qs_rot") * jnp.tile(sin_s, (1, 4))).astype(BF16)
    ks = seg("ks") * jnp.tile(cos_s, (1, 2)) + seg("ks_rot") * jnp.tile(sin_s, (1, 2))
    for hk in range(SWA_KV_HEADS):
        k_lo = ks[:, hk * 128:(hk + 1) * 128]
        ks_ref[0, :, (2 * hk) * 128:(2 * hk + 1) * 128] = k_lo.astype(BF16)
        ks_ref[0, :, (2 * hk + 1) * 128:(2 * hk + 2) * 128] = pltpu.roll(k_lo, SWA_HEAD_DIM, 1).astype(BF16)
    ones_s = (lax.broadcasted_iota(I32, (1, SWA_KV_HEADS * 128), 1) % 128 == SWA_HEAD_DIM).astype(F32)
    vs_ref[0] = (seg("vs") + ones_s).astype(BF16)

    g0, _ = _SEG["gl"]
    for p in range(6):
        gate_ref[0, :, p * 512:(p + 1) * 512] = _sigmoid(
            _dot(hb, w1_ref[:, g0 + p * 512:g0 + (p + 1) * 512])).astype(BF16)


def _input_stage(x_slab, b, modtab, g_mix, w1, g_q, wq, g_kv, wkn, wv, tabm, tabs):
    d = SLAB_ROWS * 128
    p = x_slab.shape[0] // (b * SLAB_ROWS)
    tm = TOKEN_TILE
    nj = p // tm
    tok = lambda w: pl.BlockSpec((1, tm, w), lambda bi, j: (bi, j, 0))
    full = lambda a: pl.BlockSpec(a.shape, lambda bi, j: (0,) * a.ndim)
    outs = [(1024, BF16), (1024, BF16), (1024, BF16), (512, F32), (512, BF16), (512, BF16), (256, BF16),
            (3072, BF16)]
    return pl.pallas_call(
        _in_kernel,
        grid=(b, nj),
        in_specs=[pl.BlockSpec((tm * SLAB_ROWS, 128), lambda bi, j: (bi * nj + j, 0)),
                  pl.BlockSpec((1, 1, 8, d), lambda bi, j: (bi, j // (nj - 1), 0, 0)),
                  full(g_mix), full(w1), full(g_q), full(wq), full(g_kv), full(wkn), full(wv),
                  pl.BlockSpec((tm, 256), lambda bi, j: (j, 0)),
                  pl.BlockSpec((tm, 256), lambda bi, j: (j, 0))],
        out_specs=[tok(w) for w, _ in outs],
        out_shape=[jax.ShapeDtypeStruct((b, p, w), dt) for w, dt in outs],
        compiler_params=_cparams(("arbitrary", "arbitrary")),
        name="input_stage",
    )(x_slab, modtab, g_mix, w1, g_q, wq, g_kv, wkn, wv, tabm, tabs)


def _mla_kernel(q_ref, k_ref, v_ref, *rest, tk, n_main, tail_rows):
    y_ref, m_ref, acc_ref = rest[-3:]
    hw = MLA_HEAD_PAD
    m_ref[...] = jnp.full(m_ref.shape, MASK_VALUE, F32)
    acc_ref[...] = jnp.zeros(acc_ref.shape, F32)

    def attend(r0, rows):
        for h in range(MLA_HEADS):
            q = q_ref[0, :, h * hw:(h + 1) * hw]
            k = k_ref[0, pl.ds(r0, rows), h * hw:(h + 1) * hw]
            v = v_ref[0, pl.ds(r0, rows), h * hw:(h + 1) * hw]
            s = _dot_nt(q, k)
            m_prev = m_ref[h]
            m_new = jnp.maximum(m_prev, jnp.max(s, axis=1, keepdims=True))
            m_ref[h] = m_new
            pr = jnp.exp2(s - jnp.tile(m_new, (1, rows // hw)))
            acc_ref[h] = jnp.exp2(m_prev - m_new) * acc_ref[h] + _dot(pr.astype(BF16), v)

    if n_main:
        def step(c, carry):
            attend(pl.multiple_of(c * tk, tk), tk)
            return carry

        lax.fori_loop(0, n_main, step, 0)
    if tail_rows:
        attend(n_main * tk, tail_rows)

    first = lax.broadcasted_iota(I32, (q_ref.shape[1], hw), 1) < MLA_V
    for hp in range(MLA_HEADS // 2):
        o = []
        for h in (2 * hp, 2 * hp + 1):
            acc = acc_ref[h]
            o.append(acc / acc[:, MLA_V:MLA_V + 1])
        y_ref[0, :, hp * hw:(hp + 1) * hw] = jnp.where(first, o[0], pltpu.roll(o[1], MLA_V, 1)).astype(BF16)


def _mla_attention(q, k, v, n_ctx):
    b, p, width = q.shape
    s_len = p - n_ctx
    tq = MLA_Q_TILE
    tk = MLA_KEY_CHUNK
    assert s_len % tq == 0 and s_len % tk == 0 and s_len % n_ctx == 0
    out_w = MLA_HEADS * MLA_V
    state = lambda rows: pltpu.VMEM((MLA_HEADS, rows, MLA_HEAD_PAD), F32)
    y_lat = pl.pallas_call(
        functools.partial(_mla_kernel, tk=tk, n_main=p // tk, tail_rows=p % tk),
        grid=(b, s_len // tq),
        in_specs=[pl.BlockSpec((1, tq, width), lambda bi, j: (bi, j, 0)),
                  pl.BlockSpec((1, p, width), lambda bi, j: (bi, 0, 0), pipeline_mode=pl.Buffered(1)),
                  pl.BlockSpec((1, p, width), lambda bi, j: (bi, 0, 0), pipeline_mode=pl.Buffered(1))],
        out_specs=pl.BlockSpec((1, tq, out_w), lambda bi, j: (bi, j, 0)),
        out_shape=jax.ShapeDtypeStruct((b, s_len, out_w), BF16),
        scratch_shapes=[state(tq), state(tq)],
        compiler_params=_cparams(("arbitrary", "arbitrary")),
        name="mla_attention",
    )(q, k, v)
    cblk = s_len // n_ctx
    ctx_rows = lambda w: pl.BlockSpec((1, n_ctx, w), lambda bi: (bi, cblk, 0))
    y_ctx = pl.pallas_call(
        functools.partial(_mla_kernel, tk=tk, n_main=0, tail_rows=n_ctx),
        grid=(b,),
        in_specs=[ctx_rows(width), ctx_rows(width), ctx_rows(width)],
        out_specs=pl.BlockSpec((1, n_ctx, out_w), lambda bi: (bi, 0, 0)),
        out_shape=jax.ShapeDtypeStruct((b, n_ctx, out_w), BF16),
        scratch_shapes=[state(n_ctx), state(n_ctx)],
        compiler_params=_cparams(("arbitrary",)),
        name="mla_attention_ctx",
    )(q, k, v)
    return y_lat, y_ctx


def _swa_kernel(sink_ref, q_ref, k_ref, v_ref, y_ref, *, n_ctx):
    j = pl.program_id(1)
    p_len = k_ref.shape[1]
    s_len = p_len - n_ctx
    band = 3 * SWA_BLOCK
    n_lat_tiles = s_len // SWA_BLOCK
    is_lat = j < n_lat_tiles
    n = jnp.minimum(j, n_lat_tiles - 1)
    ws = jnp.clip((n - 1) * SWA_BLOCK, 0, s_len - band)
    kstart = pl.multiple_of(ws, SWA_BLOCK)
    qpos = n * SWA_BLOCK + lax.broadcasted_iota(I32, (2 * SWA_BLOCK, band), 0) % SWA_BLOCK
    kpos = ws + lax.broadcasted_iota(I32, (2 * SWA_BLOCK, band), 1)
    valid = (jnp.abs(qpos - kpos) <= SWA_WINDOW) & is_lat
    first = lax.broadcasted_iota(I32, (SWA_BLOCK, 128), 1) < SWA_HEAD_DIM
    top = lax.broadcasted_iota(I32, (2 * SWA_BLOCK, 1), 0) < SWA_BLOCK
    for hk in range(SWA_KV_HEADS):
        qpair = jnp.concatenate([q_ref[0, :, (2 * hk) * 128:(2 * hk + 1) * 128],
                                 q_ref[0, :, (2 * hk + 1) * 128:(2 * hk + 2) * 128]], axis=0)
        vcols = slice(hk * 128, (hk + 1) * 128)
        vc = v_ref[0, s_len:p_len, vcols]
        vb = v_ref[0, pl.ds(kstart, band), vcols]
        res = {}
        for par in range(2):
            kcols = slice((2 * hk + par) * 128, (2 * hk + par + 1) * 128)
            s_c = _dot_nt(qpair, k_ref[0, s_len:p_len, kcols])
            s_b = jnp.where(valid, _dot_nt(qpair, k_ref[0, pl.ds(kstart, band), kcols]), MASK_VALUE)
            h_top = 4 * hk + par
            h_bot = 4 * hk + 2 + par
            sink = jnp.where(top, sink_ref[h_top] * LOG2_E, sink_ref[h_bot] * LOG2_E)
            m = jnp.maximum(jnp.maximum(jnp.max(s_c, axis=1, keepdims=True), jnp.max(s_b, axis=1, keepdims=True)),
                            sink)
            o = _dot(jnp.exp2(s_c - m).astype(BF16), vc) + _dot(jnp.exp2(s_b - m).astype(BF16), vb)
            o = o / (o[:, SWA_HEAD_DIM:SWA_HEAD_DIM + 1] + jnp.exp2(sink - m))
            res[h_top] = o[0:SWA_BLOCK]
            res[h_bot] = o[SWA_BLOCK:2 * SWA_BLOCK]
        for pair in (2 * hk, 2 * hk + 1):
            y_ref[0, :, pair * 128:(pair + 1) * 128] = jnp.where(
                first, res[2 * pair], pltpu.roll(res[2 * pair + 1], SWA_HEAD_DIM, 1)).astype(BF16)


def _swa_attention(sink, qs, ks, vs, n_ctx):
    b, p, _ = qs.shape
    tq = SWA_BLOCK
    return pl.pallas_call(
        functools.partial(_swa_kernel, n_ctx=n_ctx),
        grid=(b, p // tq),
        in_specs=[pl.BlockSpec(memory_space=pltpu.SMEM),
                  pl.BlockSpec((1, tq, SWA_WIDTH), lambda bi, j: (bi, j, 0)),
                  pl.BlockSpec((1, p, ks.shape[2]), lambda bi, j: (bi, 0, 0)),
                  pl.BlockSpec((1, p, vs.shape[2]), lambda bi, j: (bi, 0, 0))],
        out_specs=pl.BlockSpec((1, tq, SWA_WIDTH), lambda bi, j: (bi, j, 0)),
        out_shape=jax.ShapeDtypeStruct((b, p, SWA_WIDTH), BF16),
        compiler_params=_cparams(("arbitrary", "arbitrary")),
        name="swa_attention",
    )(sink, qs, ks, vs)


def _pool_kernel(prev_ref, cur_ref, next_ref, w_ref, scale_ref, y_ref, ext_ref, *, n_ctx):
    j = pl.program_id(1)
    tm = cur_ref.shape[1]
    nj = pl.num_programs(1)
    s_len = (nj - 1) * tm
    is_ctx = j == nj - 1
    has_prev = (j >= 1) & (j < nj - 1)
    has_next = j < nj - 2
    ext_ref[0:POOL_HALO, :] = jnp.where(has_prev, prev_ref[0], 0.0)
    ext_ref[POOL_HALO:POOL_HALO + tm, :] = cur_ref[0]
    ext_ref[POOL_HALO + tm:POOL_HALO + tm + POOL_HALO, :] = jnp.where(has_next, next_ref[0], 0.0)
    t = lax.broadcasted_iota(I32, (tm, 1), 0)
    pos = jnp.where(is_ctx, t, j * tm + t)
    seg_len = jnp.where(is_ctx, n_ctx, s_len)
    for g, w in enumerate(POOL_WINDOWS):
        cols = slice(g * POOL_GROUP_DIM, (g + 1) * POOL_GROUP_DIM)
        acc = jnp.zeros((tm, POOL_GROUP_DIM), F32)
        for off in range(-(w // 2), w - w // 2):
            acc = acc + ext_ref[POOL_HALO + off:POOL_HALO + off + tm, cols]
        lo = jnp.maximum(pos - w // 2, 0)
        hi = jnp.minimum(pos + w - w // 2, seg_len)
        cnt = (hi - lo).astype(F32)
        pooled = acc / cnt - cur_ref[0, :, cols]
        y_ref[0, :, cols] = (_dot(pooled.astype(BF16), w_ref[g]) * scale_ref[:, cols]).astype(BF16)


def _pool_stage(u, w_pool, pool_scale, n_ctx):
    b, p, width = u.shape
    tm = TOKEN_TILE
    hb = tm // POOL_HALO
    n_halo_blocks = p // POOL_HALO
    return pl.pallas_call(
        functools.partial(_pool_kernel, n_ctx=n_ctx),
        grid=(b, p // tm),
        in_specs=[pl.BlockSpec((1, POOL_HALO, width), lambda bi, j: (bi, jnp.maximum(j * hb - 1, 0), 0)),
                  pl.BlockSpec((1, tm, width), lambda bi, j: (bi, j, 0)),
                  pl.BlockSpec((1, POOL_HALO, width),
                               lambda bi, j: (bi, jnp.minimum((j + 1) * hb, n_halo_blocks - 1), 0)),
                  pl.BlockSpec(w_pool.shape, lambda bi, j: (0, 0, 0)),
                  pl.BlockSpec(pool_scale.shape, lambda bi, j: (0, 0))],
        out_specs=pl.BlockSpec((1, tm, width), lambda bi, j: (bi, j, 0)),
        out_shape=jax.ShapeDtypeStruct((b, p, width), BF16),
        scratch_shapes=[pltpu.VMEM((tm + 2 * POOL_HALO, width), F32)],
        compiler_params=_cparams(("arbitrary", "arbitrary")),
        name="pool_stage",
    )(u, u, u, w_pool, pool_scale)


def _merge_kernel(yml_ref, ymc_ref, yp_ref, ys_ref, gate_ref, x_ref, mod_ref, g_ref, wbm_ref, wbp_ref, wbs_ref,
                  wout_ref, wsg_ref, wsu_ref, wsd_ref, wrt_ref, xs_ref, fp_ref, lg_ref):
    d = SLAB_ROWS * 128
    mod = mod_ref[0, 0]
    gate = gate_ref[0]
    ym = jnp.where(pl.program_id(1) == pl.num_programs(1) - 1, ymc_ref[0], yml_ref[0])
    merged = (gate[:, 0:d].astype(F32) * _dot(ym, wbm_ref[...])
              + gate[:, d:2 * d].astype(F32) * _dot(yp_ref[0], wbp_ref[...])
              + gate[:, 2 * d:3 * d].astype(F32) * _dot(ys_ref[0], wbs_ref[...]))
    x = _slab_rows_to_matrix(x_ref, gate.shape[0], SLAB_ROWS)
    x_mid = x + mod[2:3] * _dot(merged.astype(BF16), wout_ref[...])
    f = _rms(x_mid, g_ref[...]) * (1.0 + mod[4:5]) + mod[3:4]
    fb = f.astype(BF16)
    gsh = _dot(fb, wsg_ref[...])
    shared = _dot((gsh * _sigmoid(gsh) * _dot(fb, wsu_ref[...])).astype(BF16), wsd_ref[...])
    _matrix_to_slab_rows(xs_ref, x_mid + mod[5:6] * shared, SLAB_ROWS)
    _matrix_to_slab_rows(fp_ref, _pack_bf16_pair(f), PACKED_ROWS)
    lg_ref[...] = _dot_nt(wrt_ref[...], fb)


def _merge_stage(ym_lat, ym_ctx, yp, ys, gates, x_slab, modtab, g_ffn, wbm, wbp, wbs, wout, wsg, wsu, wsd, wrt):
    b, p, _ = yp.shape
    assert ym_ctx.shape[1] == TOKEN_TILE
    d = SLAB_ROWS * 128
    tm = TOKEN_TILE
    nj = p // tm
    tok = lambda w: pl.BlockSpec((1, tm, w), lambda bi, j: (bi, j, 0))
    slab = lambda rows: pl.BlockSpec((tm * rows, 128), lambda bi, j: (bi * nj + j, 0))
    full = lambda a: pl.BlockSpec(a.shape, lambda bi, j: (0,) * a.ndim)
    return pl.pallas_call(
        _merge_kernel,
        grid=(b, nj),
        in_specs=[pl.BlockSpec((1, tm, ym_lat.shape[2]), lambda bi, j: (bi, jnp.minimum(j, nj - 2), 0)),
                  pl.BlockSpec((1, tm, ym_ctx.shape[2]), lambda bi, j: (bi, 0, 0)),
                  tok(yp.shape[2]), tok(ys.shape[2]), tok(gates.shape[2]), slab(SLAB_ROWS),
                  pl.BlockSpec((1, 1, 8, d), lambda bi, j: (bi, j // (nj - 1), 0, 0)),
                  full(g_ffn), full(wbm), full(wbp), full(wbs), full(wout), full(wsg), full(wsu), full(wsd),
                  full(wrt)],
        out_specs=[slab(SLAB_ROWS), slab(PACKED_ROWS),
                   pl.BlockSpec((N_EXPERTS, tm), lambda bi, j: (0, bi * nj + j))],
        out_shape=[jax.ShapeDtypeStruct((b * p * SLAB_ROWS, 128), F32),
                   jax.ShapeDtypeStruct((b * p * PACKED_ROWS, 128), U32),
                   jax.ShapeDtypeStruct((N_EXPERTS, b * p), F32)],
        compiler_params=_cparams(("arbitrary", "arbitrary")),
        name="merge_stage",
    )(ym_lat, ym_ctx, yp, ys, gates, x_slab, modtab, g_ffn, wbm, wbp, wbs, wout, wsg, wsu, wsd, wrt)


def _route_kernel(lg_ref, bias_ref, dest_ref, wts_ref, seg_ref):
    tm = lg_ref.shape[1]
    ne = N_EXPERTS
    neg_inf = -jnp.inf
    scores = _sigmoid(lg_ref[...])
    sel = scores + bias_ref[...]
    iota_g = lax.broadcasted_iota(I32, (EXPERTS_PER_GROUP, tm), 0)
    gscore = []
    for g in range(N_GROUPS):
        sg = sel[g * EXPERTS_PER_GROUP:(g + 1) * EXPERTS_PER_GROUP]
        m1 = jnp.max(sg, axis=0, keepdims=True)
        i1 = jnp.min(jnp.where(sg == m1, iota_g, EXPERTS_PER_GROUP), axis=0, keepdims=True)
        m2 = jnp.max(jnp.where(iota_g == i1, neg_inf, sg), axis=0, keepdims=True)
        gscore.append(m1 + m2)
    rows = []
    for g in range(N_GROUPS):
        rank = jnp.zeros((1, tm), I32)
        for g2 in range(N_GROUPS):
            if g2 == g:
                continue
            beats = (gscore[g2] >= gscore[g]) if g2 < g else (gscore[g2] > gscore[g])
            rank = rank + beats.astype(I32)
        rows.append(jnp.where(rank < TOPK_GROUPS, sel[g * EXPERTS_PER_GROUP:(g + 1) * EXPERTS_PER_GROUP], neg_inf))
    cur = jnp.concatenate(rows, axis=0)
    iota_e = lax.broadcasted_iota(I32, (ne, tm), 0)
    picks = []
    member = jnp.zeros((ne, tm), F32)
    for _ in range(TOP_K):
        m = jnp.max(cur, axis=0, keepdims=True)
        idx = jnp.min(jnp.where(cur == m, iota_e, ne), axis=0, keepdims=True)
        hit = iota_e == idx
        picks.append(hit)
        member = member + hit.astype(F32)
        cur = jnp.where(hit, neg_inf, cur)
    earlier = (lax.broadcasted_iota(I32, (tm, tm), 0) < lax.broadcasted_iota(I32, (tm, tm), 1)).astype(BF16)
    pos = _dot(member.astype(BF16), earlier)
    cnt_col = jnp.sum(member, axis=1, keepdims=True)
    blocks_col = jnp.floor((cnt_col + (SEG_ALIGN - 1)) * (1.0 / SEG_ALIGN))
    lower = (lax.broadcasted_iota(I32, (ne, ne), 1) < lax.broadcasted_iota(I32, (ne, ne), 0)).astype(BF16)
    off_col = _dot(lower, jnp.broadcast_to(blocks_col, (ne, 128)).astype(BF16))[:, 0:1] * SEG_ALIGN
    base = off_col + pos
    w_rows = [jnp.sum(jnp.where(hit, scores, 0.0), axis=0, keepdims=True) for hit in picks]
    denom = w_rows[0]
    for w in w_rows[1:]:
        denom = denom + w
    for k, hit in enumerate(picks):
        dest_ref[0, k:k + 1, :] = jnp.sum(jnp.where(hit, base, 0.0), axis=0, keepdims=True).astype(I32)
        wts_ref[0, k:k + 1, :] = w_rows[k] / denom * ROUTED_SCALE
    member_pad = jnp.concatenate([member, jnp.zeros((128 - ne, tm), F32)], axis=0).astype(BF16)
    cnt_row = _dot_nt(jnp.ones((8, tm), BF16), member_pad)
    blocks_row = jnp.floor((cnt_row + (SEG_ALIGN - 1)) * (1.0 / SEG_ALIGN))
    before = (lax.broadcasted_iota(I32, (128, 128), 0) < lax.broadcasted_iota(I32, (128, 128), 1)).astype(BF16)
    off_row = _dot(blocks_row.astype(BF16), before) * SEG_ALIGN
    r = lax.broadcasted_iota(I32, (8, 128), 0)
    seg_ref[0] = jnp.where(r == 0, off_row, jnp.where(r == 1, cnt_row, 0.0)).astype(I32)


def _route_stage(lg_t, bias_col):
    ne, t_all = lg_t.shape
    tm = MOE_TILE
    nt = t_all // tm
    return pl.pallas_call(
        _route_kernel,
        grid=(nt,),
        in_specs=[pl.BlockSpec((ne, tm), lambda i: (0, i)),
                  pl.BlockSpec((ne, 1), lambda i: (0, 0))],
        out_specs=[pl.BlockSpec((1, TOP_K, tm), lambda i: (i, 0, 0)),
                   pl.BlockSpec((1, TOP_K, tm), lambda i: (i, 0, 0)),
                   pl.BlockSpec((1, 8, 128), lambda i: (i, 0, 0))],
        out_shape=[jax.ShapeDtypeStruct((nt, TOP_K, tm), I32),
                   jax.ShapeDtypeStruct((nt, TOP_K, tm), F32),
                   jax.ShapeDtypeStruct((nt, 8, 128), I32)],
        compiler_params=_cparams(("arbitrary",)),
        name="route_stage",
    )(lg_t, bias_col)


def _moe_row_stride(tm):
    cap = TOP_K * tm + N_EXPERTS * SEG_ALIGN + MOE_CHUNK
    blocks = cap // 8 + 1
    return 8 * (blocks + 1 - blocks % 2)


def _moe_kernel(dest_ref, wts_ref, seg_ref, fp_hbm, xs_hbm, gt_ref, wg_ref, wu_ref, wd_ref, out_hbm,
                bufs_ref, fp_stage, xs_stage, out_stage, fp_sem, xs_sem, out_sem,
                *, srow, sub, n_sub_per_batch, n_tiles):
    pr = pl.program_id(0)
    eg = pl.program_id(1)
    tm = dest_ref.shape[2]
    ch = MOE_CHUNK
    group = wg_ref.shape[0]
    lane_blk = 128
    n_blk = tm // lane_blk
    per = bufs_ref.shape[0]
    n_here = jnp.minimum(per, n_tiles - pr * per)

    def table_block(ref, half, blk):
        return ref.at[half, :, pl.ds(pl.multiple_of(blk * lane_blk, lane_blk), lane_blk)]

    def token_rows(half, blk, per_tok):
        first = ((pr * per + half) * tm + blk * lane_blk) * per_tok
        return pl.ds(pl.multiple_of(first, lane_blk * per_tok), lane_blk * per_tok)

    def fp_copy(half, blk, slot):
        return pltpu.make_async_copy(fp_hbm.at[token_rows(half, blk, PACKED_ROWS), :], fp_stage.at[slot],
                                     fp_sem.at[slot])

    def xs_copy(half, blk, slot):
        return pltpu.make_async_copy(xs_hbm.at[token_rows(half, blk, SLAB_ROWS), :], xs_stage.at[slot],
                                     xs_sem.at[slot])

    def out_copy(half, blk, slot):
        return pltpu.make_async_copy(out_stage.at[slot], out_hbm.at[token_rows(half, blk, SLAB_ROWS), :],
                                     out_sem.at[slot])

    def group_rows(half):
        buf_ref = bufs_ref.at[half]
        buf_ref[...] = jnp.zeros(buf_ref.shape, U32)
        fp_copy(half, 0, 0).start()

        def body(blk, carry):
            slot = blk % 2
            fp_copy(half, blk, slot).wait()

            @pl.when(blk + 1 < n_blk)
            def _prefetch():
                fp_copy(half, blk + 1, 1 - slot).start()

            dest = table_block(dest_ref, half, blk)
            rows = fp_stage.at[slot]
            for u in range(lane_blk):
                slab = rows[u * PACKED_ROWS:(u + 1) * PACKED_ROWS, :]
                for k in range(TOP_K):
                    buf_ref[pl.ds(dest[k, u], PACKED_ROWS, stride=srow), :] = slab
            return carry

        lax.fori_loop(0, n_blk, body, 0)

    def experts(half):
        buf_ref = bufs_ref.at[half]

        def one_expert(ge, carry):
            e = eg * group + ge
            n_rows = seg_ref[half, 1, e]
            seg_off = seg_ref[half, 0, e]

            def ffn_rows(r0, n_valid, rows):
                words = [buf_ref[pl.ds(pl.multiple_of(q * srow + r0, 8), rows), :] for q in range(4)]
                w = jnp.concatenate(words, axis=1)
                xb = jnp.concatenate([_unpack_lo(w), _unpack_hi(w)], axis=1).astype(BF16)
                g = _dot(xb, wg_ref[ge])
                hmid = g * _sigmoid(g) * _dot(xb, wu_ref[ge])
                y = _dot(hmid.astype(BF16), wd_ref[ge])
                packed = _pack_bf16_pair(y)
                keep_new = lax.broadcasted_iota(I32, (rows, 1), 0) < n_valid
                for q in range(4):
                    buf_ref[pl.ds(pl.multiple_of(q * srow + r0, 8), rows), :] = jnp.where(
                        keep_new, packed[:, q * 128:(q + 1) * 128], words[q])

            n_full = n_rows // ch
            rem = n_rows - n_full * ch

            def chunk(c, carry):
                ffn_rows(seg_off + c * ch, ch, ch)
                return carry

            lax.fori_loop(0, n_full, chunk, 0)
            lo = 0
            for rows in (ch // 2, 3 * ch // 4, ch):
                pl.when((rem > lo) & (rem <= rows))(functools.partial(ffn_rows, seg_off + n_full * ch, rem, rows))
                lo = rows
            return carry

        lax.fori_loop(0, group, one_expert, 0)

    def combine(half):
        buf_ref = bufs_ref.at[half]
        xs_copy(half, 0, 0).start()

        def body(blk, carry):
            slot = blk % 2
            xs_copy(half, blk, slot).wait()

            @pl.when(blk + 1 < n_blk)
            def _prefetch():
                xs_copy(half, blk + 1, 1 - slot).start()

            @pl.when(blk >= 2)
            def _slot_free():
                out_copy(half, blk - 2, slot).wait()

            sub_blk = ((pr * per + half) * tm + blk * lane_blk) // sub
            bi = sub_blk // n_sub_per_batch
            gate = gt_ref[bi * 2 + (sub_blk - bi * n_sub_per_batch) // (n_sub_per_batch - 1)]
            dest = table_block(dest_ref, half, blk)
            wts = table_block(wts_ref, half, blk)
            xs = xs_stage.at[slot]
            out = out_stage.at[slot]
            for u in range(lane_blk):
                acc_lo = jnp.zeros((4, 128), F32)
                acc_hi = jnp.zeros((4, 128), F32)
                for k in range(TOP_K):
                    words = buf_ref[pl.ds(dest[k, u], PACKED_ROWS, stride=srow), :]
                    wk = wts[k, u]
                    acc_lo = acc_lo + wk * _unpack_lo(words)
                    acc_hi = acc_hi + wk * _unpack_hi(words)
                out[u * 8:u * 8 + 4, :] = xs[u * 8:u * 8 + 4, :] + gate[0:4] * acc_lo
                out[u * 8 + 4:u * 8 + 8, :] = xs[u * 8 + 4:u * 8 + 8, :] + gate[4:8] * acc_hi
            out_copy(half, blk, slot).start()
            return carry

        lax.fori_loop(0, n_blk, body, 0)
        for blk in (n_blk - 2, n_blk - 1):
            out_copy(half, blk, blk % 2).wait()

    last = pl.num_programs(1) - 1

    def one_tile(half, carry):
        pl.when(eg == 0)(functools.partial(group_rows, half))
        experts(half)
        pl.when(eg == last)(functools.partial(combine, half))
        return carry

    lax.fori_loop(0, n_here, one_tile, 0)


def _moe_stage(dest, wts, seg, fp4, xs8, gt2, wg, wu, wd, n_sub_per_batch):
    nt, _, tm = dest.shape
    ne = wg.shape[0]
    group = MOE_EXPERTS_PER_STEP
    per = MOE_TILES_PER_STEP
    assert ne % group == 0 and tm % 256 == 0 and TOKEN_TILE % 128 == 0 and per == 2
    n_rows = -(-nt // per)
    pad = ((0, n_rows * per - nt), (0, 0), (0, 0))
    dest, wts, seg = jnp.pad(dest, pad), jnp.pad(wts, pad), jnp.pad(seg, pad)
    srow = _moe_row_stride(tm)
    smem = lambda shape: pl.BlockSpec(shape, lambda i, e: (i, 0, 0), memory_space=pltpu.SMEM)
    hbm = pl.BlockSpec(memory_space=pl.ANY)
    stage = lambda per_tok, dt: pltpu.VMEM((2, 128 * per_tok, 128), dt)
    return pl.pallas_call(
        functools.partial(_moe_kernel, srow=srow, sub=TOKEN_TILE, n_sub_per_batch=n_sub_per_batch, n_tiles=nt),
        grid=(n_rows, ne // group),
        in_specs=[smem((per, TOP_K, tm)), smem((per, TOP_K, tm)), smem((per, 8, 128)),
                  hbm, hbm,
                  pl.BlockSpec(gt2.shape, lambda i, e: (0, 0, 0)),
                  pl.BlockSpec((group,) + wg.shape[1:], lambda i, e: (e, 0, 0)),
                  pl.BlockSpec((group,) + wu.shape[1:], lambda i, e: (e, 0, 0)),
                  pl.BlockSpec((group,) + wd.shape[1:], lambda i, e: (e, 0, 0))],
        out_specs=hbm,
        out_shape=jax.ShapeDtypeStruct(xs8.shape, F32),
        scratch_shapes=[pltpu.VMEM((per, 4 * srow, 128), U32),
                        stage(PACKED_ROWS, U32), stage(SLAB_ROWS, F32), stage(SLAB_ROWS, F32),
                        pltpu.SemaphoreType.DMA((2,)), pltpu.SemaphoreType.DMA((2,)),
                        pltpu.SemaphoreType.DMA((2,))],
        compiler_params=_cparams(("arbitrary", "arbitrary")),
        name="moe_stage",
    )(dest, wts, seg, fp4, xs8, gt2, wg, wu, wd)


def _final_kernel(x_ref, g_ref, o_ref):
    o_ref[0] = _rms(_slab_rows_to_matrix(x_ref, o_ref.shape[1], SLAB_ROWS), g_ref[...])


def _final_norm(x_slab, b, g_final, n_ctx):
    d = SLAB_ROWS * 128
    p = x_slab.shape[0] // (b * SLAB_ROWS)
    tm = TOKEN_TILE
    nj = p // tm
    return pl.pallas_call(
        _final_kernel,
        grid=(b, (p - n_ctx) // tm),
        in_specs=[pl.BlockSpec((tm * SLAB_ROWS, 128), lambda bi, j: (bi * nj + j, 0)),
                  pl.BlockSpec((1, d), lambda bi, j: (0, 0))],
        out_specs=pl.BlockSpec((1, tm, d), lambda bi, j: (bi, j, 0)),
        out_shape=jax.ShapeDtypeStruct((b, p - n_ctx, d), F32),
        compiler_params=_cparams(("arbitrary", "arbitrary")),
        name="final_norm",
    )(x_slab, g_final)


def _rot_cols(w, n_heads, rot_dim):
    kdim = w.shape[0]
    w4 = w.reshape(kdim, n_heads, 4, rot_dim // 4)
    rot = jnp.stack([-w4[:, :, 1], w4[:, :, 0], -w4[:, :, 3], w4[:, :, 2]], axis=2)
    return rot.reshape(kdim, n_heads * rot_dim)


def _rope_pattern(s_len, n_ctx, rot_dim):
    t = jnp.arange(s_len)
    row = (t // GRID_W).astype(F32)
    col = (t % GRID_W).astype(F32)
    n_freq = rot_dim // 4
    inv_freq = ROPE_BASE ** (-jnp.arange(n_freq, dtype=F32) / n_freq)
    ang_r = row[:, None] * inv_freq[None, :]
    ang_c = col[:, None] * inv_freq[None, :]
    cos = jnp.concatenate([jnp.cos(ang_r), jnp.cos(ang_r), jnp.cos(ang_c), jnp.cos(ang_c)], axis=1)
    sin = jnp.concatenate([jnp.sin(ang_r), jnp.sin(ang_r), jnp.sin(ang_c), jnp.sin(ang_c)], axis=1)
    cos = jnp.concatenate([cos, jnp.ones((n_ctx, rot_dim), F32)], axis=0)
    sin = jnp.concatenate([sin, jnp.zeros((n_ctx, rot_dim), F32)], axis=0)
    return cos, sin


def _layer_weights(w_in, w_uq, w_ukv):
    d = w_in.shape[0]
    offs = [0]
    for w in (MLA_Q_LORA, MLA_KV_LORA, MLA_ROPE, POOL_WIDTH, SWA_WIDTH, SWA_KV_WIDTH, SWA_KV_WIDTH, 3 * d):
        offs.append(offs[-1] + w)
    cq, ckv, kr, u, qs, ks, vs, gl = (w_in[:, offs[i]:offs[i + 1]] for i in range(8))
    qs = qs * (SWA_SCALE * LOG2_E)
    zeros = lambda n: jnp.zeros((d, n), F32)
    kr_slot = jnp.concatenate([zeros(MLA_NOPE), kr, zeros(MLA_HEAD_PAD - MLA_NOPE - MLA_ROPE)], axis=1)
    kr_rot_slot = jnp.concatenate([zeros(MLA_NOPE), _rot_cols(kr, 1, MLA_ROPE),
                                   zeros(MLA_HEAD_PAD - MLA_NOPE - MLA_ROPE)], axis=1)

    def kv_slots(w):
        w3 = w.reshape(d, SWA_KV_HEADS, SWA_HEAD_DIM)
        return jnp.concatenate([w3, jnp.zeros_like(w3)], axis=2).reshape(d, SWA_KV_HEADS * 128)

    w1 = jnp.concatenate([cq, ckv, u, qs, _rot_cols(qs, SWA_Q_HEADS, SWA_HEAD_DIM), kv_slots(ks),
                          kv_slots(_rot_cols(ks, SWA_KV_HEADS, SWA_HEAD_DIM)), kv_slots(vs), kr_slot, kr_rot_slot,
                          gl], axis=1).astype(BF16)
    lq = w_uq.shape[0]
    wq3 = (w_uq * (MLA_SCALE * LOG2_E)).reshape(lq, MLA_HEADS, MLA_NOPE + MLA_ROPE)
    nope, rope = wq3[:, :, :MLA_NOPE], wq3[:, :, MLA_NOPE:]
    rope_rot = _rot_cols(rope.reshape(lq, MLA_HEADS * MLA_ROPE), MLA_HEADS, MLA_ROPE).reshape(lq, MLA_HEADS, MLA_ROPE)
    pad = jnp.zeros((lq, MLA_HEADS, MLA_HEAD_PAD - MLA_NOPE - MLA_ROPE), F32)
    wq_a = jnp.concatenate([nope, rope, pad], axis=2).reshape(lq, MLA_HEADS * MLA_HEAD_PAD)
    wq_b = jnp.concatenate([jnp.zeros_like(nope), rope_rot, pad], axis=2).reshape(lq, MLA_HEADS * MLA_HEAD_PAD)
    wq = jnp.concatenate([wq_a, wq_b], axis=1).astype(BF16)
    lkv = w_ukv.shape[0]
    wkv3 = w_ukv.reshape(lkv, MLA_HEADS, MLA_NOPE + MLA_V)
    wkn = jnp.concatenate([wkv3[:, :, :MLA_NOPE], jnp.zeros((lkv, MLA_HEADS, MLA_HEAD_PAD - MLA_NOPE), F32)],
                          axis=2).reshape(lkv, MLA_HEADS * MLA_HEAD_PAD).astype(BF16)
    wv = jnp.concatenate([wkv3[:, :, MLA_NOPE:], jnp.zeros((lkv, MLA_HEADS, MLA_HEAD_PAD - MLA_V), F32)],
                         axis=2).reshape(lkv, MLA_HEADS * MLA_HEAD_PAD).astype(BF16)
    return w1, wq, wkn, wv


def kernel(x, c, ctx, c_ctx, w_mod, b_mod, g_mix, g_ffn, w_in, g_mla_q, g_mla_kv, w_mla_uq, w_mla_ukv, w_pool,
           pool_scale, swa_sink, w_br_mla, w_br_pool, w_br_swa, w_out, w_router, router_bias, w_exp_gate,
           w_exp_up, w_exp_down, w_sh_gate, w_sh_up, w_sh_down, g_final):
    b, s_len, d = x.shape
    n_ctx = ctx.shape[1]
    n_layers = w_mod.shape[0]
    p = n_ctx + s_len
    assert n_ctx == TOKEN_TILE and s_len % TOKEN_TILE == 0 and (b * p) % MOE_TILE == 0 and b + 1 <= 8
    assert d == 1024 and w_in.shape[2] == (MLA_Q_LORA + MLA_KV_LORA + MLA_ROPE + POOL_WIDTH + SWA_WIDTH
                                           + 2 * SWA_KV_WIDTH + 3 * d)

    cm, sm = _rope_pattern(s_len, n_ctx, MLA_ROPE)
    tail = jnp.zeros((p, MLA_HEAD_PAD - MLA_NOPE - MLA_ROPE), F32)
    tabm = jnp.concatenate([jnp.ones((p, MLA_NOPE), F32), cm, tail, jnp.zeros((p, MLA_NOPE), F32), sm, tail], axis=1)
    cs, ss = _rope_pattern(s_len, n_ctx, SWA_HEAD_DIM)
    tabs = jnp.concatenate([cs, cs, ss, ss], axis=1)

    cvec = jnp.concatenate([c, c_ctx[None, :], jnp.zeros((8 - b - 1, d), F32)], axis=0)
    mod_all = _modulation(cvec, w_mod, b_mod).reshape(n_layers, 8, N_MOD, d)

    x_slab = jnp.concatenate([x, ctx], axis=1).reshape(b * p * SLAB_ROWS, 128)
    for i in range(n_layers):
        lat = mod_all[i, :b]
        ctx_rows = jnp.broadcast_to(mod_all[i, b][None], (b, N_MOD, d))
        modtab = jnp.pad(jnp.stack([lat, ctx_rows], axis=1), ((0, 0), (0, 0), (0, 8 - N_MOD), (0, 0)))
        w1, wq, wkn, wv = _layer_weights(w_in[i], w_mla_uq[i], w_mla_ukv[i])
        q, k, v, u, qs, ks, vs, gates = _input_stage(
            x_slab, b, modtab, g_mix[i][None], w1, g_mla_q[i][None], wq, g_mla_kv[i][None], wkn, wv, tabm, tabs)
        y_mla, y_mla_ctx = _mla_attention(q, k, v, n_ctx)
        y_swa = _swa_attention(swa_sink[i], qs, ks, vs, n_ctx)
        y_pool = _pool_stage(u, w_pool[i].astype(BF16), pool_scale[i][None], n_ctx)
        xs, fp, lg_t = _merge_stage(
            y_mla, y_mla_ctx, y_pool, y_swa, gates, x_slab, modtab, g_ffn[i][None], w_br_mla[i].astype(BF16),
            w_br_pool[i].astype(BF16), w_br_swa[i].astype(BF16), w_out[i].astype(BF16),
            w_sh_gate[i].astype(BF16), w_sh_up[i].astype(BF16), w_sh_down[i].astype(BF16),
            w_router[i].T.astype(BF16))
        dest, wts, seg = _route_stage(lg_t, router_bias[i][:, None])
        gt2 = modtab[:, :, 5, :].reshape(b * 2, 8, d // 8)
        x_slab = _moe_stage(dest, wts, seg, fp, xs, gt2, w_exp_gate[i].astype(BF16), w_exp_up[i].astype(BF16),
                            w_exp_down[i].astype(BF16), p // TOKEN_TILE)
    return _final_norm(x_slab, b, g_final[None], n_ctx)
```

```python
import functools

import jax
import jax.numpy as jnp
from jax import lax
from jax.experimental import pallas as pl
from jax.experimental.pallas import tpu as pltpu

F32 = jnp.float32
BF16 = jnp.bfloat16
U32 = jnp.uint32
I32 = jnp.int32

NORM_EPS = 1e-6
ROPE_BASE = 10000.0
GRID_W = 64
N_MOD = 6

MLA_HEADS = 8
MLA_Q_LORA = 384
MLA_KV_LORA = 256
MLA_NOPE = 64
MLA_ROPE = 32
MLA_V = 64
MLA_SCALE = (MLA_NOPE + MLA_ROPE) ** -0.5
MLA_HEAD_PAD = 128
LOG2_E = 1.4426950408889634

POOL_WINDOWS = (2, 4, 8, 16)
POOL_GROUP_DIM = 128
POOL_WIDTH = 512
POOL_HALO = 8

SWA_Q_HEADS = 8
SWA_KV_HEADS = 2
SWA_HEAD_DIM = 64
SWA_WINDOW = 128
SWA_BLOCK = 128
SWA_SCALE = SWA_HEAD_DIM ** -0.5
SWA_WIDTH = SWA_Q_HEADS * SWA_HEAD_DIM
SWA_KV_WIDTH = SWA_KV_HEADS * SWA_HEAD_DIM

N_EXPERTS = 64
TOP_K = 8
N_GROUPS = 8
TOPK_GROUPS = 4
EXPERTS_PER_GROUP = 8
D_EXPERT = 256
ROUTED_SCALE = 2.5

TOKEN_TILE = 256
MLA_Q_TILE = 1024
MLA_KEY_CHUNK = 256
MOE_TILE = 1024
MOE_CHUNK = 256
MOE_BODY_ROWS = (128, 160, 192, 224, 256)
MOE_EXPERTS_PER_STEP = 2
MOE_TILES_PER_STEP = 2
SEG_ALIGN = 8
SLAB_ROWS = 8
PACKED_ROWS = 4
MASK_VALUE = -1e30
HI16 = 0xFFFF0000

VMEM_LIMIT = 56 * 1024 * 1024

_SEG_WIDTHS = (("cq", 384), ("ckv", 256), ("u", 512), ("qs", 512), ("qs_rot", 512), ("ks", 256),
               ("ks_rot", 256), ("vs", 256), ("kr", 128), ("kr_rot", 128), ("gl", 3072))
_SEG = {}
_o = 0
for _n, _w in _SEG_WIDTHS:
    _SEG[_n] = (_o, _o + _w)
    _o += _w
FUSED_IN_WIDTH = _o


def _cparams(sem):
    return pltpu.CompilerParams(dimension_semantics=sem, vmem_limit_bytes=VMEM_LIMIT)


def _dot(a, b):
    return jnp.dot(a, b, preferred_element_type=F32)


def _dot_nt(a, b):
    return lax.dot_general(a, b, (((1,), (1,)), ((), ())), preferred_element_type=F32)


def _sigmoid(x):
    return 1.0 / (1.0 + jnp.exp(-x))


def _rms(x, g):
    return x * lax.rsqrt(jnp.mean(x * x, axis=-1, keepdims=True) + NORM_EPS) * g


def _pack_bf16_pair(v):
    n = v.shape[1] // 2
    bits = pltpu.bitcast(v.astype(BF16).astype(F32), U32)
    return (bits[:, :n] >> 16) | (bits[:, n:] & jnp.uint32(HI16))


def _slab_rows_to_matrix(ref, n_tok, per_tok):
    return jnp.concatenate([ref[pl.ds(c, n_tok, stride=per_tok), :] for c in range(per_tok)], axis=1)


def _matrix_to_slab_rows(ref, val, per_tok):
    n_tok = val.shape[0]
    for c in range(per_tok):
        ref[pl.ds(c, n_tok, stride=per_tok), :] = val[:, c * 128:(c + 1) * 128]


def _unpack_lo(w):
    return pltpu.bitcast(w << 16, F32)


def _unpack_hi(w):
    return pltpu.bitcast(w & jnp.uint32(HI16), F32)


def _mod_kernel(c_ref, w_ref, b_ref, o_ref):
    c = c_ref[...]
    a = (c * _sigmoid(c)).astype(BF16)
    o_ref[0] = _dot(a, w_ref[0].astype(BF16)) + b_ref[0]


def _modulation(cvec, w_mod, b_mod):
    n_layers, d, width = w_mod.shape
    tn = width // 4
    return pl.pallas_call(
        _mod_kernel,
        grid=(n_layers, width // tn),
        in_specs=[pl.BlockSpec((8, d), lambda l, n: (0, 0)),
                  pl.BlockSpec((1, d, tn), lambda l, n: (l, 0, n)),
                  pl.BlockSpec((1, 1, tn), lambda l, n: (l, 0, n))],
        out_specs=pl.BlockSpec((1, 8, tn), lambda l, n: (l, 0, n)),
        out_shape=jax.ShapeDtypeStruct((n_layers, 8, width), F32),
        compiler_params=_cparams(("arbitrary", "arbitrary")),
        name="modulation",
    )(cvec, w_mod, b_mod.reshape(n_layers, 1, width))


def _in_kernel(x_ref, mod_ref, g_ref, w1_ref, gq_ref, wq_ref, gkv_ref, wkn_ref, wv_ref, tabm_ref, tabs_ref,
               q_ref, k_ref, v_ref, u_ref, qs_ref, ks_ref, vs_ref, gate_ref):
    x = _slab_rows_to_matrix(x_ref, q_ref.shape[1], SLAB_ROWS)
    mod = mod_ref[0, 0]
    h = _rms(x, g_ref[...]) * (1.0 + mod[1:2]) + mod[0:1]
    hb = h.astype(BF16)

    def seg(name):
        a, b = _SEG[name]
        return _dot(hb, w1_ref[:, a:b])

    cos_m = tabm_ref[:, 0:128]
    sin_m = tabm_ref[:, 128:256]
    cos_s = tabs_ref[:, 0:128]
    sin_s = tabs_ref[:, 128:256]

    cqn = _rms(seg("cq"), gq_ref[...]).astype(BF16)
    qa = _dot(cqn, wq_ref[:, 0:1024])
    qb = _dot(cqn, wq_ref[:, 1024:2048])
    q_ref[0] = (qa * jnp.tile(cos_m, (1, MLA_HEADS)) + qb * jnp.tile(sin_m, (1, MLA_HEADS))).astype(BF16)

    ckvn = _rms(seg("ckv"), gkv_ref[...]).astype(BF16)
    kr = seg("kr") * cos_m + seg("kr_rot") * sin_m
    k_ref[0] = (_dot(ckvn, wkn_ref[...]) + jnp.tile(kr, (1, MLA_HEADS))).astype(BF16)
    ones_lane = (lax.broadcasted_iota(I32, (1, MLA_HEADS * MLA_HEAD_PAD), 1) % MLA_HEAD_PAD == MLA_V).astype(F32)
    v_ref[0] = (_dot(ckvn, wv_ref[...]) + ones_lane).astype(BF16)

    u_ref[0] = seg("u")

    qs_ref[0] = (seg("qs") * jnp.tile(cos_s, (1, 4)) + seg("qs_rot") * jnp.tile(sin_s, (1, 4))).astype(BF16)
    ks = seg("ks") * jnp.tile(cos_s, (1, 2)) + seg("ks_rot") * jnp.tile(sin_s, (1, 2))
    for hk in range(SWA_KV_HEADS):
        k_lo = ks[:, hk * 128:(hk + 1) * 128]
        ks_ref[0, :, (2 * hk) * 128:(2 * hk + 1) * 128] = k_lo.astype(BF16)
        ks_ref[0, :, (2 * hk + 1) * 128:(2 * hk + 2) * 128] = pltpu.roll(k_lo, SWA_HEAD_DIM, 1).astype(BF16)
    ones_s = (lax.broadcasted_iota(I32, (1, SWA_KV_HEADS * 128), 1) % 128 == SWA_HEAD_DIM).astype(F32)
    vs_ref[0] = (seg("vs") + ones_s).astype(BF16)

    g0, _ = _SEG["gl"]
    for p in range(6):
        gate_ref[0, :, p * 512:(p + 1) * 512] = _sigmoid(
            _dot(hb, w1_ref[:, g0 + p * 512:g0 + (p + 1) * 512])).astype(BF16)


def _input_stage(x_slab, b, modtab, g_mix, w1, g_q, wq, g_kv, wkn, wv, tabm, tabs):
    d = SLAB_ROWS * 128
    p = x_slab.shape[0] // (b * SLAB_ROWS)
    tm = TOKEN_TILE
    nj = p // tm
    tok = lambda w: pl.BlockSpec((1, tm, w), lambda bi, j: (bi, j, 0))
    full = lambda a: pl.BlockSpec(a.shape, lambda bi, j: (0,) * a.ndim)
    outs = [(1024, BF16), (1024, BF16), (1024, BF16), (512, F32), (512, BF16), (512, BF16), (256, BF16),
            (3072, BF16)]
    return pl.pallas_call(
        _in_kernel,
        grid=(b, nj),
        in_specs=[pl.BlockSpec((tm * SLAB_ROWS, 128), lambda bi, j: (bi * nj + j, 0)),
                  pl.BlockSpec((1, 1, 8, d), lambda bi, j: (bi, j // (nj - 1), 0, 0)),
                  full(g_mix), full(w1), full(g_q), full(wq), full(g_kv), full(wkn), full(wv),
                  pl.BlockSpec((tm, 256), lambda bi, j: (j, 0)),
                  pl.BlockSpec((tm, 256), lambda bi, j: (j, 0))],
        out_specs=[tok(w) for w, _ in outs],
        out_shape=[jax.ShapeDtypeStruct((b, p, w), dt) for w, dt in outs],
        compiler_params=_cparams(("arbitrary", "arbitrary")),
        name="input_stage",
    )(x_slab, modtab, g_mix, w1, g_q, wq, g_kv, wkn, wv, tabm, tabs)


def _mla_kernel(q_ref, k_ref, v_ref, *rest, tk, n_main, tail_rows):
    y_ref, m_ref, acc_ref = rest[-3:]
    hw = MLA_HEAD_PAD
    m_ref[...] = jnp.full(m_ref.shape, MASK_VALUE, F32)
    acc_ref[...] = jnp.zeros(acc_ref.shape, F32)

    def attend(r0, rows):
        for h in range(MLA_HEADS):
            q = q_ref[0, :, h * hw:(h + 1) * hw]
            k = k_ref[0, pl.ds(r0, rows), h * hw:(h + 1) * hw]
            v = v_ref[0, pl.ds(r0, rows), h * hw:(h + 1) * hw]
            s = _dot_nt(q, k)
            m_prev = m_ref[h]
            m_new = jnp.maximum(m_prev, jnp.max(s, axis=1, keepdims=True))
            m_ref[h] = m_new
            pr = jnp.exp2(s - jnp.tile(m_new, (1, rows // hw)))
            acc_ref[h] = jnp.exp2(m_prev - m_new) * acc_ref[h] + _dot(pr.astype(BF16), v)

    if n_main:
        def step(c, carry):
            attend(pl.multiple_of(c * tk, tk), tk)
            return carry

        lax.fori_loop(0, n_main, step, 0)
    if tail_rows:
        attend(n_main * tk, tail_rows)

    first = lax.broadcasted_iota(I32, (q_ref.shape[1], hw), 1) < MLA_V
    for hp in range(MLA_HEADS // 2):
        o = []
        for h in (2 * hp, 2 * hp + 1):
            acc = acc_ref[h]
            o.append(acc / acc[:, MLA_V:MLA_V + 1])
        y_ref[0, :, hp * hw:(hp + 1) * hw] = jnp.where(first, o[0], pltpu.roll(o[1], MLA_V, 1)).astype(BF16)


def _mla_attention(q, k, v, n_ctx):
    b, p, width = q.shape
    s_len = p - n_ctx
    tq = MLA_Q_TILE
    tk = MLA_KEY_CHUNK
    assert s_len % tq == 0 and s_len % tk == 0 and s_len % n_ctx == 0
    out_w = MLA_HEADS * MLA_V
    state = lambda rows: pltpu.VMEM((MLA_HEADS, rows, MLA_HEAD_PAD), F32)
    y_lat = pl.pallas_call(
        functools.partial(_mla_kernel, tk=tk, n_main=p // tk, tail_rows=p % tk),
        grid=(b, s_len // tq),
        in_specs=[pl.BlockSpec((1, tq, width), lambda bi, j: (bi, j, 0)),
                  pl.BlockSpec((1, p, width), lambda bi, j: (bi, 0, 0), pipeline_mode=pl.Buffered(1)),
                  pl.BlockSpec((1, p, width), lambda bi, j: (bi, 0, 0), pipeline_mode=pl.Buffered(1))],
        out_specs=pl.BlockSpec((1, tq, out_w), lambda bi, j: (bi, j, 0)),
        out_shape=jax.ShapeDtypeStruct((b, s_len, out_w), BF16),
        scratch_shapes=[state(tq), state(tq)],
        compiler_params=_cparams(("arbitrary", "arbitrary")),
        name="mla_attention",
    )(q, k, v)
    cblk = s_len // n_ctx
    ctx_rows = lambda w: pl.BlockSpec((1, n_ctx, w), lambda bi: (bi, cblk, 0))
    y_ctx = pl.pallas_call(
        functools.partial(_mla_kernel, tk=tk, n_main=0, tail_rows=n_ctx),
        grid=(b,),
        in_specs=[ctx_rows(width), ctx_rows(width), ctx_rows(width)],
        out_specs=pl.BlockSpec((1, n_ctx, out_w), lambda bi: (bi, 0, 0)),
        out_shape=jax.ShapeDtypeStruct((b, n_ctx, out_w), BF16),
        scratch_shapes=[state(n_ctx), state(n_ctx)],
        compiler_params=_cparams(("arbitrary",)),
        name="mla_attention_ctx",
    )(q, k, v)
    return y_lat, y_ctx


def _swa_kernel(sink_ref, q_ref, k_ref, v_ref, y_ref, *, n_ctx):
    j = pl.program_id(1)
    p_len = k_ref.shape[1]
    s_len = p_len - n_ctx
    band = 3 * SWA_BLOCK
    n_lat_tiles = s_len // SWA_BLOCK
    is_lat = j < n_lat_tiles
    n = jnp.minimum(j, n_lat_tiles - 1)
    ws = jnp.clip((n - 1) * SWA_BLOCK, 0, s_len - band)
    kstart = pl.multiple_of(ws, SWA_BLOCK)
    qpos = n * SWA_BLOCK + lax.broadcasted_iota(I32, (2 * SWA_BLOCK, band), 0) % SWA_BLOCK
    kpos = ws + lax.broadcasted_iota(I32, (2 * SWA_BLOCK, band), 1)
    valid = (jnp.abs(qpos - kpos) <= SWA_WINDOW) & is_lat
    first = lax.broadcasted_iota(I32, (SWA_BLOCK, 128), 1) < SWA_HEAD_DIM
    top = lax.broadcasted_iota(I32, (2 * SWA_BLOCK, 1), 0) < SWA_BLOCK
    for hk in range(SWA_KV_HEADS):
        qpair = jnp.concatenate([q_ref[0, :, (2 * hk) * 128:(2 * hk + 1) * 128],
                                 q_ref[0, :, (2 * hk + 1) * 128:(2 * hk + 2) * 128]], axis=0)
        vcols = slice(hk * 128, (hk + 1) * 128)
        vc = v_ref[0, s_len:p_len, vcols]
        vb = v_ref[0, pl.ds(kstart, band), vcols]
        res = {}
        for par in range(2):
            kcols = slice((2 * hk + par) * 128, (2 * hk + par + 1) * 128)
            s_c = _dot_nt(qpair, k_ref[0, s_len:p_len, kcols])
            s_b = jnp.where(valid, _dot_nt(qpair, k_ref[0, pl.ds(kstart, band), kcols]), MASK_VALUE)
            h_top = 4 * hk + par
            h_bot = 4 * hk + 2 + par
            sink = jnp.where(top, sink_ref[h_top] * LOG2_E, sink_ref[h_bot] * LOG2_E)
            m = jnp.maximum(jnp.maximum(jnp.max(s_c, axis=1, keepdims=True), jnp.max(s_b, axis=1, keepdims=True)),
                            sink)
            o = _dot(jnp.exp2(s_c - m).astype(BF16), vc) + _dot(jnp.exp2(s_b - m).astype(BF16), vb)
            o = o / (o[:, SWA_HEAD_DIM:SWA_HEAD_DIM + 1] + jnp.exp2(sink - m))
            res[h_top] = o[0:SWA_BLOCK]
            res[h_bot] = o[SWA_BLOCK:2 * SWA_BLOCK]
        for pair in (2 * hk, 2 * hk + 1):
            y_ref[0, :, pair * 128:(pair + 1) * 128] = jnp.where(
                first, res[2 * pair], pltpu.roll(res[2 * pair + 1], SWA_HEAD_DIM, 1)).astype(BF16)


def _swa_attention(sink, qs, ks, vs, n_ctx):
    b, p, _ = qs.shape
    tq = SWA_BLOCK
    return pl.pallas_call(
        functools.partial(_swa_kernel, n_ctx=n_ctx),
        grid=(b, p // tq),
        in_specs=[pl.BlockSpec(memory_space=pltpu.SMEM),
                  pl.BlockSpec((1, tq, SWA_WIDTH), lambda bi, j: (bi, j, 0)),
                  pl.BlockSpec((1, p, ks.shape[2]), lambda bi, j: (bi, 0, 0)),
                  pl.BlockSpec((1, p, vs.shape[2]), lambda bi, j: (bi, 0, 0))],
        out_specs=pl.BlockSpec((1, tq, SWA_WIDTH), lambda bi, j: (bi, j, 0)),
        out_shape=jax.ShapeDtypeStruct((b, p, SWA_WIDTH), BF16),
        compiler_params=_cparams(("arbitrary", "arbitrary")),
        name="swa_attention",
    )(sink, qs, ks, vs)


def _pool_kernel(prev_ref, cur_ref, next_ref, w_ref, scale_ref, y_ref, ext_ref, *, n_ctx):
    j = pl.program_id(1)
    tm = cur_ref.shape[1]
    nj = pl.num_programs(1)
    s_len = (nj - 1) * tm
    is_ctx = j == nj - 1
    has_prev = (j >= 1) & (j < nj - 1)
    has_next = j < nj - 2
    ext_ref[0:POOL_HALO, :] = jnp.where(has_prev, prev_ref[0], 0.0)
    ext_ref[POOL_HALO:POOL_HALO + tm, :] = cur_ref[0]
    ext_ref[POOL_HALO + tm:POOL_HALO + tm + POOL_HALO, :] = jnp.where(has_next, next_ref[0], 0.0)
    t = lax.broadcasted_iota(I32, (tm, 1), 0)
    pos = jnp.where(is_ctx, t, j * tm + t)
    seg_len = jnp.where(is_ctx, n_ctx, s_len)
    for g, w in enumerate(POOL_WINDOWS):
        cols = slice(g * POOL_GROUP_DIM, (g + 1) * POOL_GROUP_DIM)
        acc = jnp.zeros((tm, POOL_GROUP_DIM), F32)
        for off in range(-(w // 2), w - w // 2):
            acc = acc + ext_ref[POOL_HALO + off:POOL_HALO + off + tm, cols]
        lo = jnp.maximum(pos - w // 2, 0)
        hi = jnp.minimum(pos + w - w // 2, seg_len)
        cnt = (hi - lo).astype(F32)
        pooled = acc / cnt - cur_ref[0, :, cols]
        y_ref[0, :, cols] = (_dot(pooled.astype(BF16), w_ref[g]) * scale_ref[:, cols]).astype(BF16)


def _pool_stage(u, w_pool, pool_scale, n_ctx):
    b, p, width = u.shape
    tm = TOKEN_TILE
    hb = tm // POOL_HALO
    n_halo_blocks = p // POOL_HALO
    return pl.pallas_call(
        functools.partial(_pool_kernel, n_ctx=n_ctx),
        grid=(b, p // tm),
        in_specs=[pl.BlockSpec((1, POOL_HALO, width), lambda bi, j: (bi, jnp.maximum(j * hb - 1, 0), 0)),
                  pl.BlockSpec((1, tm, width), lambda bi, j: (bi, j, 0)),
                  pl.BlockSpec((1, POOL_HALO, width),
                               lambda bi, j: (bi, jnp.minimum((j + 1) * hb, n_halo_blocks - 1), 0)),
                  pl.BlockSpec(w_pool.shape, lambda bi, j: (0, 0, 0)),
                  pl.BlockSpec(pool_scale.shape, lambda bi, j: (0, 0))],
        out_specs=pl.BlockSpec((1, tm, width), lambda bi, j: (bi, j, 0)),
        out_shape=jax.ShapeDtypeStruct((b, p, width), BF16),
        scratch_shapes=[pltpu.VMEM((tm + 2 * POOL_HALO, width), F32)],
        compiler_params=_cparams(("arbitrary", "arbitrary")),
        name="pool_stage",
    )(u, u, u, w_pool, pool_scale)


def _merge_kernel(yml_ref, ymc_ref, yp_ref, ys_ref, gate_ref, x_ref, mod_ref, g_ref, wbm_ref, wbp_ref, wbs_ref,
                  wout_ref, wsg_ref, wsu_ref, wsd_ref, wrt_ref, xs_ref, fp_ref, lg_ref):
    d = SLAB_ROWS * 128
    mod = mod_ref[0, 0]
    gate = gate_ref[0]
    ym = jnp.where(pl.program_id(1) == pl.num_programs(1) - 1, ymc_ref[0], yml_ref[0])
    merged = (gate[:, 0:d].astype(F32) * _dot(ym, wbm_ref[...])
              + gate[:, d:2 * d].astype(F32) * _dot(yp_ref[0], wbp_ref[...])
              + gate[:, 2 * d:3 * d].astype(F32) * _dot(ys_ref[0], wbs_ref[...]))
    x = _slab_rows_to_matrix(x_ref, gate.shape[0], SLAB_ROWS)
    x_mid = x + mod[2:3] * _dot(merged.astype(BF16), wout_ref[...])
    f = _rms(x_mid, g_ref[...]) * (1.0 + mod[4:5]) + mod[3:4]
    fb = f.astype(BF16)
    gsh = _dot(fb, wsg_ref[...])
    shared = _dot((gsh * _sigmoid(gsh) * _dot(fb, wsu_ref[...])).astype(BF16), wsd_ref[...])
    _matrix_to_slab_rows(xs_ref, x_mid + mod[5:6] * shared, SLAB_ROWS)
    _matrix_to_slab_rows(fp_ref, _pack_bf16_pair(f), PACKED_ROWS)
    lg_ref[...] = _dot_nt(wrt_ref[...], fb)


def _merge_stage(ym_lat, ym_ctx, yp, ys, gates, x_slab, modtab, g_ffn, wbm, wbp, wbs, wout, wsg, wsu, wsd, wrt):
    b, p, _ = yp.shape
    assert ym_ctx.shape[1] == TOKEN_TILE
    d = SLAB_ROWS * 128
    tm = TOKEN_TILE
    nj = p // tm
    tok = lambda w: pl.BlockSpec((1, tm, w), lambda bi, j: (bi, j, 0))
    slab = lambda rows: pl.BlockSpec((tm * rows, 128), lambda bi, j: (bi * nj + j, 0))
    full = lambda a: pl.BlockSpec(a.shape, lambda bi, j: (0,) * a.ndim)
    return pl.pallas_call(
        _merge_kernel,
        grid=(b, nj),
        in_specs=[pl.BlockSpec((1, tm, ym_lat.shape[2]), lambda bi, j: (bi, jnp.minimum(j, nj - 2), 0)),
                  pl.BlockSpec((1, tm, ym_ctx.shape[2]), lambda bi, j: (bi, 0, 0)),
                  tok(yp.shape[2]), tok(ys.shape[2]), tok(gates.shape[2]), slab(SLAB_ROWS),
                  pl.BlockSpec((1, 1, 8, d), lambda bi, j: (bi, j // (nj - 1), 0, 0)),
                  full(g_ffn), full(wbm), full(wbp), full(wbs), full(wout), full(wsg), full(wsu), full(wsd),
                  full(wrt)],
        out_specs=[slab(SLAB_ROWS), slab(PACKED_ROWS),
                   pl.BlockSpec((N_EXPERTS, tm), lambda bi, j: (0, bi * nj + j))],
        out_shape=[jax.ShapeDtypeStruct((b * p * SLAB_ROWS, 128), F32),
                   jax.ShapeDtypeStruct((b * p * PACKED_ROWS, 128), U32),
                   jax.ShapeDtypeStruct((N_EXPERTS, b * p), F32)],
        compiler_params=_cparams(("arbitrary", "arbitrary")),
        name="merge_stage",
    )(ym_lat, ym_ctx, yp, ys, gates, x_slab, modtab, g_ffn, wbm, wbp, wbs, wout, wsg, wsu, wsd, wrt)


def _route_kernel(lg_ref, bias_ref, dest_ref, wts_ref, seg_ref):
    tm = lg_ref.shape[1]
    ne = N_EXPERTS
    neg_inf = -jnp.inf
    scores = _sigmoid(lg_ref[...])
    sel = scores + bias_ref[...]
    iota_g = lax.broadcasted_iota(I32, (EXPERTS_PER_GROUP, tm), 0)
    gscore = []
    for g in range(N_GROUPS):
        sg = sel[g * EXPERTS_PER_GROUP:(g + 1) * EXPERTS_PER_GROUP]
        m1 = jnp.max(sg, axis=0, keepdims=True)
        i1 = jnp.min(jnp.where(sg == m1, iota_g, EXPERTS_PER_GROUP), axis=0, keepdims=True)
        m2 = jnp.max(jnp.where(iota_g == i1, neg_inf, sg), axis=0, keepdims=True)
        gscore.append(m1 + m2)
    rows = []
    for g in range(N_GROUPS):
        rank = jnp.zeros((1, tm), I32)
        for g2 in range(N_GROUPS):
            if g2 == g:
                continue
            beats = (gscore[g2] >= gscore[g]) if g2 < g else (gscore[g2] > gscore[g])
            rank = rank + beats.astype(I32)
        rows.append(jnp.where(rank < TOPK_GROUPS, sel[g * EXPERTS_PER_GROUP:(g + 1) * EXPERTS_PER_GROUP], neg_inf))
    cur = jnp.concatenate(rows, axis=0)
    iota_e = lax.broadcasted_iota(I32, (ne, tm), 0)
    picks = []
    member = jnp.zeros((ne, tm), F32)
    for _ in range(TOP_K):
        m = jnp.max(cur, axis=0, keepdims=True)
        idx = jnp.min(jnp.where(cur == m, iota_e, ne), axis=0, keepdims=True)
        hit = iota_e == idx
        picks.append(hit)
        member = member + hit.astype(F32)
        cur = jnp.where(hit, neg_inf, cur)
    earlier = (lax.broadcasted_iota(I32, (tm, tm), 0) < lax.broadcasted_iota(I32, (tm, tm), 1)).astype(BF16)
    pos = _dot(member.astype(BF16), earlier)
    cnt_col = jnp.sum(member, axis=1, keepdims=True)
    blocks_col = jnp.floor((cnt_col + (SEG_ALIGN - 1)) * (1.0 / SEG_ALIGN))
    lower = (lax.broadcasted_iota(I32, (ne, ne), 1) < lax.broadcasted_iota(I32, (ne, ne), 0)).astype(BF16)
    off_col = _dot(lower, jnp.broadcast_to(blocks_col, (ne, 128)).astype(BF16))[:, 0:1] * SEG_ALIGN
    base = off_col + pos
    w_rows = [jnp.sum(jnp.where(hit, scores, 0.0), axis=0, keepdims=True) for hit in picks]
    denom = w_rows[0]
    for w in w_rows[1:]:
        denom = denom + w
    for k, hit in enumerate(picks):
        dest_ref[0, k:k + 1, :] = jnp.sum(jnp.where(hit, base, 0.0), axis=0, keepdims=True).astype(I32)
        wts_ref[0, k:k + 1, :] = w_rows[k] / denom * ROUTED_SCALE
    member_pad = jnp.concatenate([member, jnp.zeros((128 - ne, tm), F32)], axis=0).astype(BF16)
    cnt_row = _dot_nt(jnp.ones((8, tm), BF16), member_pad)
    blocks_row = jnp.floor((cnt_row + (SEG_ALIGN - 1)) * (1.0 / SEG_ALIGN))
    before = (lax.broadcasted_iota(I32, (128, 128), 0) < lax.broadcasted_iota(I32, (128, 128), 1)).astype(BF16)
    off_row = _dot(blocks_row.astype(BF16), before) * SEG_ALIGN
    r = lax.broadcasted_iota(I32, (8, 128), 0)
    seg_ref[0] = jnp.where(r == 0, off_row, jnp.where(r == 1, cnt_row, 0.0)).astype(I32)


def _route_stage(lg_t, bias_col):
    ne, t_all = lg_t.shape
    tm = MOE_TILE
    nt = t_all // tm
    return pl.pallas_call(
        _route_kernel,
        grid=(nt,),
        in_specs=[pl.BlockSpec((ne, tm), lambda i: (0, i)),
                  pl.BlockSpec((ne, 1), lambda i: (0, 0))],
        out_specs=[pl.BlockSpec((1, TOP_K, tm), lambda i: (i, 0, 0)),
                   pl.BlockSpec((1, TOP_K, tm), lambda i: (i, 0, 0)),
                   pl.BlockSpec((1, 8, 128), lambda i: (i, 0, 0))],
        out_shape=[jax.ShapeDtypeStruct((nt, TOP_K, tm), I32),
                   jax.ShapeDtypeStruct((nt, TOP_K, tm), F32),
                   jax.ShapeDtypeStruct((nt, 8, 128), I32)],
        compiler_params=_cparams(("arbitrary",)),
        name="route_stage",
    )(lg_t, bias_col)


def _moe_row_stride(tm):
    cap = TOP_K * tm + N_EXPERTS * SEG_ALIGN + MOE_CHUNK
    blocks = cap // 8 + 1
    return 8 * (blocks + 1 - blocks % 2)


def _moe_kernel(dest_ref, wts_ref, seg_ref, fp_hbm, xs_hbm, gt_ref, wg_ref, wu_ref, wd_ref, out_hbm,
                bufs_ref, fp_stage, xs_stage, out_stage, fp_sem, xs_sem, out_sem,
                *, srow, sub, n_sub_per_batch, n_tiles):
    pr = pl.program_id(0)
    eg = pl.program_id(1)
    tm = dest_ref.shape[2]
    ch = MOE_CHUNK
    group = wg_ref.shape[0]
    lane_blk = 128
    n_blk = tm // lane_blk
    per = bufs_ref.shape[0]
    n_here = jnp.minimum(per, n_tiles - pr * per)

    def table_block(ref, half, blk):
        return ref.at[half, :, pl.ds(pl.multiple_of(blk * lane_blk, lane_blk), lane_blk)]

    def token_rows(half, blk, per_tok):
        first = ((pr * per + half) * tm + blk * lane_blk) * per_tok
        return pl.ds(pl.multiple_of(first, lane_blk * per_tok), lane_blk * per_tok)

    def fp_copy(half, blk, slot):
        return pltpu.make_async_copy(fp_hbm.at[token_rows(half, blk, PACKED_ROWS), :], fp_stage.at[slot],
                                     fp_sem.at[slot])

    def xs_copy(half, blk, slot):
        return pltpu.make_async_copy(xs_hbm.at[token_rows(half, blk, SLAB_ROWS), :], xs_stage.at[slot],
                                     xs_sem.at[slot])

    def out_copy(half, blk, slot):
        return pltpu.make_async_copy(out_stage.at[slot], out_hbm.at[token_rows(half, blk, SLAB_ROWS), :],
                                     out_sem.at[slot])

    def group_rows(half):
        buf_ref = bufs_ref.at[half]
        buf_ref[...] = jnp.zeros(buf_ref.shape, U32)
        fp_copy(half, 0, 0).start()

        def body(blk, carry):
            slot = blk % 2
            fp_copy(half, blk, slot).wait()

            @pl.when(blk + 1 < n_blk)
            def _prefetch():
                fp_copy(half, blk + 1, 1 - slot).start()

            dest = table_block(dest_ref, half, blk)
            rows = fp_stage.at[slot]
            for u in range(lane_blk):
                slab = rows[u * PACKED_ROWS:(u + 1) * PACKED_ROWS, :]
                for k in range(TOP_K):
                    buf_ref[pl.ds(dest[k, u], PACKED_ROWS, stride=srow), :] = slab
            return carry

        lax.fori_loop(0, n_blk, body, 0)

    def experts():
        def one_expert(ge, carry):
            e = eg * group + ge
            counts = [seg_ref[half, 1, e] for half in range(per)]
            offsets = [seg_ref[half, 0, e] for half in range(per)]

            def ffn_rows(halves, rows, chunk_idx):
                words, keep = [], []
                for h in halves:
                    r0 = offsets[h] + chunk_idx * rows
                    words.append([bufs_ref[h, pl.ds(pl.multiple_of(q * srow + r0, 8), rows), :] for q in range(4)])
                    keep.append(lax.broadcasted_iota(I32, (rows, 1), 0) < counts[h] - chunk_idx * rows)
                w = jnp.concatenate([jnp.concatenate(ws, axis=1) for ws in words], axis=0)
                xb = jnp.concatenate([_unpack_lo(w), _unpack_hi(w)], axis=1).astype(BF16)
                g = _dot(xb, wg_ref[ge])
                hmid = g * _sigmoid(g) * _dot(xb, wu_ref[ge])
                packed = _pack_bf16_pair(_dot(hmid.astype(BF16), wd_ref[ge]))
                for n, h in enumerate(halves):
                    r0 = offsets[h] + chunk_idx * rows
                    for q in range(4):
                        bufs_ref[h, pl.ds(pl.multiple_of(q * srow + r0, 8), rows), :] = jnp.where(
                            keep[n], packed[n * rows:(n + 1) * rows, q * 128:(q + 1) * 128], words[n][q])

            most = counts[0]
            for c in counts[1:]:
                most = jnp.maximum(most, c)
            lo = 0
            for rows in MOE_BODY_ROWS:
                pl.when((most > lo) & (most <= rows))(functools.partial(ffn_rows, tuple(range(per)), rows, 0))
                lo = rows

            @pl.when(most > ch)
            def _long_segments():
                for h in range(per):
                    def chunk(c, carry2, h=h):
                        ffn_rows((h,), ch, c)
                        return carry2

                    lax.fori_loop(0, (counts[h] + ch - 1) // ch, chunk, 0)

            return carry

        lax.fori_loop(0, group, one_expert, 0)

    def combine(half):
        buf_ref = bufs_ref.at[half]
        xs_copy(half, 0, 0).start()

        def body(blk, carry):
            slot = blk % 2
            xs_copy(half, blk, slot).wait()

            @pl.when(blk + 1 < n_blk)
            def _prefetch():
                xs_copy(half, blk + 1, 1 - slot).start()

            @pl.when(blk >= 2)
            def _slot_free():
                out_copy(half, blk - 2, slot).wait()

            sub_blk = ((pr * per + half) * tm + blk * lane_blk) // sub
            bi = sub_blk // n_sub_per_batch
            gate = gt_ref[bi * 2 + (sub_blk - bi * n_sub_per_batch) // (n_sub_per_batch - 1)]
            dest = table_block(dest_ref, half, blk)
            wts = table_block(wts_ref, half, blk)
            xs = xs_stage.at[slot]
            out = out_stage.at[slot]
            for u in range(lane_blk):
                acc_lo = jnp.zeros((4, 128), F32)
                acc_hi = jnp.zeros((4, 128), F32)
                for k in range(TOP_K):
                    words = buf_ref[pl.ds(dest[k, u], PACKED_ROWS, stride=srow), :]
                    wk = wts[k, u]
                    acc_lo = acc_lo + wk * _unpack_lo(words)
                    acc_hi = acc_hi + wk * _unpack_hi(words)
                out[u * 8:u * 8 + 4, :] = xs[u * 8:u * 8 + 4, :] + gate[0:4] * acc_lo
                out[u * 8 + 4:u * 8 + 8, :] = xs[u * 8 + 4:u * 8 + 8, :] + gate[4:8] * acc_hi
            out_copy(half, blk, slot).start()
            return carry

        lax.fori_loop(0, n_blk, body, 0)
        for blk in (n_blk - 2, n_blk - 1):
            out_copy(half, blk, blk % 2).wait()

    def each_tile(fn):
        def body(half, carry):
            fn(half)
            return carry

        lax.fori_loop(0, n_here, body, 0)

    @pl.when(eg == 0)
    def _first_step():
        each_tile(group_rows)
        for half in range(1, per):
            @pl.when(half >= n_here)
            def _clear(half=half):
                bufs_ref[half] = jnp.zeros(bufs_ref.shape[1:], U32)

    experts()
    pl.when(eg == pl.num_programs(1) - 1)(functools.partial(each_tile, combine))


def _moe_stage(dest, wts, seg, fp4, xs8, gt2, wg, wu, wd, n_sub_per_batch):
    nt, _, tm = dest.shape
    ne = wg.shape[0]
    group = MOE_EXPERTS_PER_STEP
    per = MOE_TILES_PER_STEP
    assert ne % group == 0 and tm % 256 == 0 and TOKEN_TILE % 128 == 0 and per == 2
    n_rows = -(-nt // per)
    pad = ((0, n_rows * per - nt), (0, 0), (0, 0))
    dest, wts, seg = jnp.pad(dest, pad), jnp.pad(wts, pad), jnp.pad(seg, pad)
    srow = _moe_row_stride(tm)
    smem = lambda shape: pl.BlockSpec(shape, lambda i, e: (i, 0, 0), memory_space=pltpu.SMEM)
    hbm = pl.BlockSpec(memory_space=pl.ANY)
    stage = lambda per_tok, dt: pltpu.VMEM((2, 128 * per_tok, 128), dt)
    return pl.pallas_call(
        functools.partial(_moe_kernel, srow=srow, sub=TOKEN_TILE, n_sub_per_batch=n_sub_per_batch, n_tiles=nt),
        grid=(n_rows, ne // group),
        in_specs=[smem((per, TOP_K, tm)), smem((per, TOP_K, tm)), smem((per, 8, 128)),
                  hbm, hbm,
                  pl.BlockSpec(gt2.shape, lambda i, e: (0, 0, 0)),
                  pl.BlockSpec((group,) + wg.shape[1:], lambda i, e: (e, 0, 0)),
                  pl.BlockSpec((group,) + wu.shape[1:], lambda i, e: (e, 0, 0)),
                  pl.BlockSpec((group,) + wd.shape[1:], lambda i, e: (e, 0, 0))],
        out_specs=hbm,
        out_shape=jax.ShapeDtypeStruct(xs8.shape, F32),
        scratch_shapes=[pltpu.VMEM((per, 4 * srow, 128), U32),
                        stage(PACKED_ROWS, U32), stage(SLAB_ROWS, F32), stage(SLAB_ROWS, F32),
                        pltpu.SemaphoreType.DMA((2,)), pltpu.SemaphoreType.DMA((2,)),
                        pltpu.SemaphoreType.DMA((2,))],
        compiler_params=_cparams(("arbitrary", "arbitrary")),
        name="moe_stage",
    )(dest, wts, seg, fp4, xs8, gt2, wg, wu, wd)


def _final_kernel(x_ref, g_ref, o_ref):
    o_ref[0] = _rms(_slab_rows_to_matrix(x_ref, o_ref.shape[1], SLAB_ROWS), g_ref[...])


def _final_norm(x_slab, b, g_final, n_ctx):
    d = SLAB_ROWS * 128
    p = x_slab.shape[0] // (b * SLAB_ROWS)
    tm = TOKEN_TILE
    nj = p // tm
    return pl.pallas_call(
        _final_kernel,
        grid=(b, (p - n_ctx) // tm),
        in_specs=[pl.BlockSpec((tm * SLAB_ROWS, 128), lambda bi, j: (bi * nj + j, 0)),
                  pl.BlockSpec((1, d), lambda bi, j: (0, 0))],
        out_specs=pl.BlockSpec((1, tm, d), lambda bi, j: (bi, j, 0)),
        out_shape=jax.ShapeDtypeStruct((b, p - n_ctx, d), F32),
        compiler_params=_cparams(("arbitrary", "arbitrary")),
        name="final_norm",
    )(x_slab, g_final)


def _rot_cols(w, n_heads, rot_dim):
    kdim = w.shape[0]
    w4 = w.reshape(kdim, n_heads, 4, rot_dim // 4)
    rot = jnp.stack([-w4[:, :, 1], w4[:, :, 0], -w4[:, :, 3], w4[:, :, 2]], axis=2)
    return rot.reshape(kdim, n_heads * rot_dim)


def _rope_pattern(s_len, n_ctx, rot_dim):
    t = jnp.arange(s_len)
    row = (t // GRID_W).astype(F32)
    col = (t % GRID_W).astype(F32)
    n_freq = rot_dim // 4
    inv_freq = ROPE_BASE ** (-jnp.arange(n_freq, dtype=F32) / n_freq)
    ang_r = row[:, None] * inv_freq[None, :]
    ang_c = col[:, None] * inv_freq[None, :]
    cos = jnp.concatenate([jnp.cos(ang_r), jnp.cos(ang_r), jnp.cos(ang_c), jnp.cos(ang_c)], axis=1)
    sin = jnp.concatenate([jnp.sin(ang_r), jnp.sin(ang_r), jnp.sin(ang_c), jnp.sin(ang_c)], axis=1)
    cos = jnp.concatenate([cos, jnp.ones((n_ctx, rot_dim), F32)], axis=0)
    sin = jnp.concatenate([sin, jnp.zeros((n_ctx, rot_dim), F32)], axis=0)
    return cos, sin


def _layer_weights(w_in, w_uq, w_ukv):
    d = w_in.shape[0]
    offs = [0]
    for w in (MLA_Q_LORA, MLA_KV_LORA, MLA_ROPE, POOL_WIDTH, SWA_WIDTH, SWA_KV_WIDTH, SWA_KV_WIDTH, 3 * d):
        offs.append(offs[-1] + w)
    cq, ckv, kr, u, qs, ks, vs, gl = (w_in[:, offs[i]:offs[i + 1]] for i in range(8))
    qs = qs * (SWA_SCALE * LOG2_E)
    zeros = lambda n: jnp.zeros((d, n), F32)
    kr_slot = jnp.concatenate([zeros(MLA_NOPE), kr, zeros(MLA_HEAD_PAD - MLA_NOPE - MLA_ROPE)], axis=1)
    kr_rot_slot = jnp.concatenate([zeros(MLA_NOPE), _rot_cols(kr, 1, MLA_ROPE),
                                   zeros(MLA_HEAD_PAD - MLA_NOPE - MLA_ROPE)], axis=1)

    def kv_slots(w):
        w3 = w.reshape(d, SWA_KV_HEADS, SWA_HEAD_DIM)
        return jnp.concatenate([w3, jnp.zeros_like(w3)], axis=2).reshape(d, SWA_KV_HEADS * 128)

    w1 = jnp.concatenate([cq, ckv, u, qs, _rot_cols(qs, SWA_Q_HEADS, SWA_HEAD_DIM), kv_slots(ks),
                          kv_slots(_rot_cols(ks, SWA_KV_HEADS, SWA_HEAD_DIM)), kv_slots(vs), kr_slot, kr_rot_slot,
                          gl], axis=1).astype(BF16)
    lq = w_uq.shape[0]
    wq3 = (w_uq * (MLA_SCALE * LOG2_E)).reshape(lq, MLA_HEADS, MLA_NOPE + MLA_ROPE)
    nope, rope = wq3[:, :, :MLA_NOPE], wq3[:, :, MLA_NOPE:]
    rope_rot = _rot_cols(rope.reshape(lq, MLA_HEADS * MLA_ROPE), MLA_HEADS, MLA_ROPE).reshape(lq, MLA_HEADS, MLA_ROPE)
    pad = jnp.zeros((lq, MLA_HEADS, MLA_HEAD_PAD - MLA_NOPE - MLA_ROPE), F32)
    wq_a = jnp.concatenate([nope, rope, pad], axis=2).reshape(lq, MLA_HEADS * MLA_HEAD_PAD)
    wq_b = jnp.concatenate([jnp.zeros_like(nope), rope_rot, pad], axis=2).reshape(lq, MLA_HEADS * MLA_HEAD_PAD)
    wq = jnp.concatenate([wq_a, wq_b], axis=1).astype(BF16)
    lkv = w_ukv.shape[0]
    wkv3 = w_ukv.reshape(lkv, MLA_HEADS, MLA_NOPE + MLA_V)
    wkn = jnp.concatenate([wkv3[:, :, :MLA_NOPE], jnp.zeros((lkv, MLA_HEADS, MLA_HEAD_PAD - MLA_NOPE), F32)],
                          axis=2).reshape(lkv, MLA_HEADS * MLA_HEAD_PAD).astype(BF16)
    wv = jnp.concatenate([wkv3[:, :, MLA_NOPE:], jnp.zeros((lkv, MLA_HEADS, MLA_HEAD_PAD - MLA_V), F32)],
                         axis=2).reshape(lkv, MLA_HEADS * MLA_HEAD_PAD).astype(BF16)
    return w1, wq, wkn, wv


def kernel(x, c, ctx, c_ctx, w_mod, b_mod, g_mix, g_ffn, w_in, g_mla_q, g_mla_kv, w_mla_uq, w_mla_ukv, w_pool,
           pool_scale, swa_sink, w_br_mla, w_br_pool, w_br_swa, w_out, w_router, router_bias, w_exp_gate,
           w_exp_up, w_exp_down, w_sh_gate, w_sh_up, w_sh_down, g_final):
    b, s_len, d = x.shape
    n_ctx = ctx.shape[1]
    n_layers = w_mod.shape[0]
    p = n_ctx + s_len
    assert n_ctx == TOKEN_TILE and s_len % TOKEN_TILE == 0 and (b * p) % MOE_TILE == 0 and b + 1 <= 8
    assert d == 1024 and w_in.shape[2] == (MLA_Q_LORA + MLA_KV_LORA + MLA_ROPE + POOL_WIDTH + SWA_WIDTH
                                           + 2 * SWA_KV_WIDTH + 3 * d)

    cm, sm = _rope_pattern(s_len, n_ctx, MLA_ROPE)
    tail = jnp.zeros((p, MLA_HEAD_PAD - MLA_NOPE - MLA_ROPE), F32)
    tabm = jnp.concatenate([jnp.ones((p, MLA_NOPE), F32), cm, tail, jnp.zeros((p, MLA_NOPE), F32), sm, tail], axis=1)
    cs, ss = _rope_pattern(s_len, n_ctx, SWA_HEAD_DIM)
    tabs = jnp.concatenate([cs, cs, ss, ss], axis=1)

    cvec = jnp.concatenate([c, c_ctx[None, :], jnp.zeros((8 - b - 1, d), F32)], axis=0)
    mod_all = _modulation(cvec, w_mod, b_mod).reshape(n_layers, 8, N_MOD, d)

    x_slab = jnp.concatenate([x, ctx], axis=1).reshape(b * p * SLAB_ROWS, 128)
    for i in range(n_layers):
        lat = mod_all[i, :b]
        ctx_rows = jnp.broadcast_to(mod_all[i, b][None], (b, N_MOD, d))
        modtab = jnp.pad(jnp.stack([lat, ctx_rows], axis=1), ((0, 0), (0, 0), (0, 8 - N_MOD), (0, 0)))
        w1, wq, wkn, wv = _layer_weights(w_in[i], w_mla_uq[i], w_mla_ukv[i])
        q, k, v, u, qs, ks, vs, gates = _input_stage(
            x_slab, b, modtab, g_mix[i][None], w1, g_mla_q[i][None], wq, g_mla_kv[i][None], wkn, wv, tabm, tabs)
        y_mla, y_mla_ctx = _mla_attention(q, k, v, n_ctx)
        y_swa = _swa_attention(swa_sink[i], qs, ks, vs, n_ctx)
        y_pool = _pool_stage(u, w_pool[i].astype(BF16), pool_scale[i][None], n_ctx)
        xs, fp, lg_t = _merge_stage(
            y_mla, y_mla_ctx, y_pool, y_swa, gates, x_slab, modtab, g_ffn[i][None], w_br_mla[i].astype(BF16),
            w_br_pool[i].astype(BF16), w_br_swa[i].astype(BF16), w_out[i].astype(BF16),
            w_sh_gate[i].astype(BF16), w_sh_up[i].astype(BF16), w_sh_down[i].astype(BF16),
            w_router[i].T.astype(BF16))
        dest, wts, seg = _route_stage(lg_t, router_bias[i][:, None])
        gt2 = modtab[:, :, 5, :].reshape(b * 2, 8, d // 8)
        x_slab = _moe_stage(dest, wts, seg, fp, xs, gt2, w_exp_gate[i].astype(BF16), w_exp_up[i].astype(BF16),
                            w_exp_down[i].astype(BF16), p // TOKEN_TILE)
    return _final_norm(x_slab, b, g_final[None], n_ctx)
```

```python
import functools

import jax
import jax.numpy as jnp
from jax import lax
from jax.experimental import pallas as pl
from jax.experimental.pallas import tpu as pltpu

F32 = jnp.float32
BF16 = jnp.bfloat16
U32 = jnp.uint32
I32 = jnp.int32

NORM_EPS = 1e-6
ROPE_BASE = 10000.0
GRID_W = 64
N_MOD = 6

MLA_HEADS = 8
MLA_Q_LORA = 384
MLA_KV_LORA = 256
MLA_NOPE = 64
MLA_ROPE = 32
MLA_V = 64
MLA_SCALE = (MLA_NOPE + MLA_ROPE) ** -0.5
MLA_HEAD_PAD = 128
LOG2_E = 1.4426950408889634

POOL_WINDOWS = (2, 4, 8, 16)
POOL_GROUP_DIM = 128
POOL_WIDTH = 512
POOL_HALO = 8

SWA_Q_HEADS = 8
SWA_KV_HEADS = 2
SWA_HEAD_DIM = 64
SWA_WINDOW = 128
SWA_BLOCK = 128
SWA_SCALE = SWA_HEAD_DIM ** -0.5
SWA_WIDTH = SWA_Q_HEADS * SWA_HEAD_DIM
SWA_KV_WIDTH = SWA_KV_HEADS * SWA_HEAD_DIM

N_EXPERTS = 64
TOP_K = 8
N_GROUPS = 8
TOPK_GROUPS = 4
EXPERTS_PER_GROUP = 8
D_EXPERT = 256
ROUTED_SCALE = 2.5

TOKEN_TILE = 256
SWA_Q_TILE = 256
MLA_Q_TILE = 1024
MLA_KEY_CHUNK = 256
MOE_TILE = 1024
MOE_CHUNK = 256
MOE_BODY_ROWS = (128, 160, 192, 224, 256)
MOE_EXPERTS_PER_STEP = 2
MOE_TILES_PER_STEP = 2
SEG_ALIGN = 8
SLAB_ROWS = 8
PACKED_ROWS = 4
MASK_VALUE = -1e30
HI16 = 0xFFFF0000

VMEM_LIMIT = 56 * 1024 * 1024

_SEG_WIDTHS = (("cq", 384), ("ckv", 256), ("u", 512), ("qs", 512), ("qs_rot", 512), ("ks", 256),
               ("ks_rot", 256), ("vs", 256), ("kr", 128), ("kr_rot", 128), ("gl", 3072))
_SEG = {}
_o = 0
for _n, _w in _SEG_WIDTHS:
    _SEG[_n] = (_o, _o + _w)
    _o += _w
FUSED_IN_WIDTH = _o


def _cparams(sem):
    return pltpu.CompilerParams(dimension_semantics=sem, vmem_limit_bytes=VMEM_LIMIT)


def _dot(a, b):
    return jnp.dot(a, b, preferred_element_type=F32)


def _dot_nt(a, b):
    return lax.dot_general(a, b, (((1,), (1,)), ((), ())), preferred_element_type=F32)


def _sigmoid(x):
    return 1.0 / (1.0 + jnp.exp(-x))


def _rms(x, g):
    return x * lax.rsqrt(jnp.mean(x * x, axis=-1, keepdims=True) + NORM_EPS) * g


def _pack_bf16_pair(v):
    n = v.shape[1] // 2
    bits = pltpu.bitcast(v.astype(BF16).astype(F32), U32)
    return (bits[:, :n] >> 16) | (bits[:, n:] & jnp.uint32(HI16))


def _slab_rows_to_matrix(ref, n_tok, per_tok):
    return jnp.concatenate([ref[pl.ds(c, n_tok, stride=per_tok), :] for c in range(per_tok)], axis=1)


def _matrix_to_slab_rows(ref, val, per_tok):
    n_tok = val.shape[0]
    for c in range(per_tok):
        ref[pl.ds(c, n_tok, stride=per_tok), :] = val[:, c * 128:(c + 1) * 128]


def _unpack_lo(w):
    return pltpu.bitcast(w << 16, F32)


def _unpack_hi(w):
    return pltpu.bitcast(w & jnp.uint32(HI16), F32)


def _mod_kernel(c_ref, w_ref, b_ref, o_ref):
    c = c_ref[...]
    a = (c * _sigmoid(c)).astype(BF16)
    o_ref[0] = _dot(a, w_ref[0].astype(BF16)) + b_ref[0]


def _modulation(cvec, w_mod, b_mod):
    n_layers, d, width = w_mod.shape
    tn = width // 4
    return pl.pallas_call(
        _mod_kernel,
        grid=(n_layers, width // tn),
        in_specs=[pl.BlockSpec((8, d), lambda l, n: (0, 0)),
                  pl.BlockSpec((1, d, tn), lambda l, n: (l, 0, n)),
                  pl.BlockSpec((1, 1, tn), lambda l, n: (l, 0, n))],
        out_specs=pl.BlockSpec((1, 8, tn), lambda l, n: (l, 0, n)),
        out_shape=jax.ShapeDtypeStruct((n_layers, 8, width), F32),
        compiler_params=_cparams(("arbitrary", "arbitrary")),
        name="modulation",
    )(cvec, w_mod, b_mod.reshape(n_layers, 1, width))


def _assemble_kernel(x_ref, ctx_ref, o_ref):
    val = jnp.where(pl.program_id(1) == pl.num_programs(1) - 1, ctx_ref[0], x_ref[0])
    _matrix_to_slab_rows(o_ref, val, SLAB_ROWS)


def _assemble_stream(x, ctx):
    b, s_len, d = x.shape
    tm = TOKEN_TILE
    assert ctx.shape[1] == tm and d == SLAB_ROWS * 128
    nj = s_len // tm + 1
    return pl.pallas_call(
        _assemble_kernel,
        grid=(b, nj),
        in_specs=[pl.BlockSpec((1, tm, d), lambda bi, j: (bi, jnp.minimum(j, nj - 2), 0)),
                  pl.BlockSpec((1, tm, d), lambda bi, j: (bi, 0, 0))],
        out_specs=pl.BlockSpec((tm * SLAB_ROWS, 128), lambda bi, j: (bi * nj + j, 0)),
        out_shape=jax.ShapeDtypeStruct((b * nj * tm * SLAB_ROWS, 128), F32),
        compiler_params=_cparams(("arbitrary", "arbitrary")),
        name="assemble_stream",
    )(x, ctx)


def _in_kernel(x_ref, mod_ref, g_ref, w1_ref, gq_ref, wq_ref, gkv_ref, wkn_ref, wv_ref, tabm_ref, tabs_ref,
               q_ref, k_ref, v_ref, u_ref, qs_ref, ks_ref, vs_ref, gate_ref):
    x = _slab_rows_to_matrix(x_ref, q_ref.shape[1], SLAB_ROWS)
    mod = mod_ref[0, 0]
    h = _rms(x, g_ref[...]) * (1.0 + mod[1:2]) + mod[0:1]
    hb = h.astype(BF16)

    def seg(name):
        a, b = _SEG[name]
        return _dot(hb, w1_ref[:, a:b])

    cos_m = tabm_ref[:, 0:128]
    sin_m = tabm_ref[:, 128:256]
    cos_s = tabs_ref[:, 0:128]
    sin_s = tabs_ref[:, 128:256]

    cqn = _rms(seg("cq"), gq_ref[...]).astype(BF16)
    qa = _dot(cqn, wq_ref[:, 0:1024])
    qb = _dot(cqn, wq_ref[:, 1024:2048])
    q_ref[0] = (qa * jnp.tile(cos_m, (1, MLA_HEADS)) + qb * jnp.tile(sin_m, (1, MLA_HEADS))).astype(BF16)

    ckvn = _rms(seg("ckv"), gkv_ref[...]).astype(BF16)
    kr = seg("kr") * cos_m + seg("kr_rot") * sin_m
    k_ref[0] = (_dot(ckvn, wkn_ref[...]) + jnp.tile(kr, (1, MLA_HEADS))).astype(BF16)
    ones_lane = (lax.broadcasted_iota(I32, (1, MLA_HEADS * MLA_HEAD_PAD), 1) % MLA_HEAD_PAD == MLA_V).astype(F32)
    v_ref[0] = (_dot(ckvn, wv_ref[...]) + ones_lane).astype(BF16)

    u_ref[0] = seg("u")

    qs_ref[0] = (seg("qs") * jnp.tile(cos_s, (1, 4)) + seg("qs_rot") * jnp.tile(sin_s, (1, 4))).astype(BF16)
    ks = seg("ks") * jnp.tile(cos_s, (1, 2)) + seg("ks_rot") * jnp.tile(sin_s, (1, 2))
    for hk in range(SWA_KV_HEADS):
        k_lo = ks[:, hk * 128:(hk + 1) * 128]
        ks_ref[0, :, (2 * hk) * 128:(2 * hk + 1) * 128] = k_lo.astype(BF16)
        ks_ref[0, :, (2 * hk + 1) * 128:(2 * hk + 2) * 128] = pltpu.roll(k_lo, SWA_HEAD_DIM, 1).astype(BF16)
    ones_s = (lax.broadcasted_iota(I32, (1, SWA_KV_HEADS * 128), 1) % 128 == SWA_HEAD_DIM).astype(F32)
    vs_ref[0] = (seg("vs") + ones_s).astype(BF16)

    g0, _ = _SEG["gl"]
    for p in range(6):
        gate_ref[0, :, p * 512:(p + 1) * 512] = _sigmoid(
            _dot(hb, w1_ref[:, g0 + p * 512:g0 + (p + 1) * 512])).astype(BF16)


def _input_stage(x_slab, b, modtab, g_mix, w1, g_q, wq, g_kv, wkn, wv, tabm, tabs):
    d = SLAB_ROWS * 128
    p = x_slab.shape[0] // (b * SLAB_ROWS)
    tm = TOKEN_TILE
    nj = p // tm
    tok = lambda w: pl.BlockSpec((1, tm, w), lambda bi, j: (bi, j, 0))
    full = lambda a: pl.BlockSpec(a.shape, lambda bi, j: (0,) * a.ndim)
    outs = [(1024, BF16), (1024, BF16), (1024, BF16), (512, F32), (512, BF16), (512, BF16), (256, BF16),
            (3072, BF16)]
    return pl.pallas_call(
        _in_kernel,
        grid=(b, nj),
        in_specs=[pl.BlockSpec((tm * SLAB_ROWS, 128), lambda bi, j: (bi * nj + j, 0)),
                  pl.BlockSpec((1, 1, 8, d), lambda bi, j: (bi, j // (nj - 1), 0, 0)),
                  full(g_mix), full(w1), full(g_q), full(wq), full(g_kv), full(wkn), full(wv),
                  pl.BlockSpec((tm, 256), lambda bi, j: (j, 0)),
                  pl.BlockSpec((tm, 256), lambda bi, j: (j, 0))],
        out_specs=[tok(w) for w, _ in outs],
        out_shape=[jax.ShapeDtypeStruct((b, p, w), dt) for w, dt in outs],
        compiler_params=_cparams(("arbitrary", "arbitrary")),
        name="input_stage",
    )(x_slab, modtab, g_mix, w1, g_q, wq, g_kv, wkn, wv, tabm, tabs)


def _mla_kernel(q_ref, k_ref, v_ref, *rest, tk, n_main, tail_rows):
    y_ref, m_ref, acc_ref = rest[-3:]
    hw = MLA_HEAD_PAD
    m_ref[...] = jnp.full(m_ref.shape, MASK_VALUE, F32)
    acc_ref[...] = jnp.zeros(acc_ref.shape, F32)

    def attend(r0, rows):
        for h in range(MLA_HEADS):
            q = q_ref[0, :, h * hw:(h + 1) * hw]
            k = k_ref[0, pl.ds(r0, rows), h * hw:(h + 1) * hw]
            v = v_ref[0, pl.ds(r0, rows), h * hw:(h + 1) * hw]
            s = _dot_nt(q, k)
            m_prev = m_ref[h]
            m_new = jnp.maximum(m_prev, jnp.max(s, axis=1, keepdims=True))
            m_ref[h] = m_new
            pr = jnp.exp2(s - jnp.tile(m_new, (1, rows // hw)))
            acc_ref[h] = jnp.exp2(m_prev - m_new) * acc_ref[h] + _dot(pr.astype(BF16), v)

    if n_main:
        def step(c, carry):
            attend(pl.multiple_of(c * tk, tk), tk)
            return carry

        lax.fori_loop(0, n_main, step, 0)
    if tail_rows:
        attend(n_main * tk, tail_rows)

    first = lax.broadcasted_iota(I32, (q_ref.shape[1], hw), 1) < MLA_V
    for hp in range(MLA_HEADS // 2):
        o = []
        for h in (2 * hp, 2 * hp + 1):
            acc = acc_ref[h]
            o.append(acc / acc[:, MLA_V:MLA_V + 1])
        y_ref[0, :, hp * hw:(hp + 1) * hw] = jnp.where(first, o[0], pltpu.roll(o[1], MLA_V, 1)).astype(BF16)


def _mla_attention(q, k, v, n_ctx):
    b, p, width = q.shape
    s_len = p - n_ctx
    tq = MLA_Q_TILE
    tk = MLA_KEY_CHUNK
    assert s_len % tq == 0 and s_len % tk == 0 and s_len % n_ctx == 0
    out_w = MLA_HEADS * MLA_V
    state = lambda rows: pltpu.VMEM((MLA_HEADS, rows, MLA_HEAD_PAD), F32)
    y_lat = pl.pallas_call(
        functools.partial(_mla_kernel, tk=tk, n_main=p // tk, tail_rows=p % tk),
        grid=(b, s_len // tq),
        in_specs=[pl.BlockSpec((1, tq, width), lambda bi, j: (bi, j, 0)),
                  pl.BlockSpec((1, p, width), lambda bi, j: (bi, 0, 0), pipeline_mode=pl.Buffered(1)),
                  pl.BlockSpec((1, p, width), lambda bi, j: (bi, 0, 0), pipeline_mode=pl.Buffered(1))],
        out_specs=pl.BlockSpec((1, tq, out_w), lambda bi, j: (bi, j, 0)),
        out_shape=jax.ShapeDtypeStruct((b, s_len, out_w), BF16),
        scratch_shapes=[state(tq), state(tq)],
        compiler_params=_cparams(("arbitrary", "arbitrary")),
        name="mla_attention",
    )(q, k, v)
    cblk = s_len // n_ctx
    ctx_rows = lambda w: pl.BlockSpec((1, n_ctx, w), lambda bi: (bi, cblk, 0))
    y_ctx = pl.pallas_call(
        functools.partial(_mla_kernel, tk=tk, n_main=0, tail_rows=n_ctx),
        grid=(b,),
        in_specs=[ctx_rows(width), ctx_rows(width), ctx_rows(width)],
        out_specs=pl.BlockSpec((1, n_ctx, out_w), lambda bi: (bi, 0, 0)),
        out_shape=jax.ShapeDtypeStruct((b, n_ctx, out_w), BF16),
        scratch_shapes=[state(n_ctx), state(n_ctx)],
        compiler_params=_cparams(("arbitrary",)),
        name="mla_attention_ctx",
    )(q, k, v)
    return y_lat, y_ctx


def _swa_kernel(sink_ref, q_ref, k_ref, v_ref, y_ref, *, n_ctx):
    p_len = k_ref.shape[1]
    s_len = p_len - n_ctx
    band = 3 * SWA_BLOCK
    n_lat_tiles = s_len // SWA_BLOCK
    first = lax.broadcasted_iota(I32, (SWA_BLOCK, 128), 1) < SWA_HEAD_DIM
    top = lax.broadcasted_iota(I32, (2 * SWA_BLOCK, 1), 0) < SWA_BLOCK
    for t in range(q_ref.shape[1] // SWA_BLOCK):
        qrows = slice(t * SWA_BLOCK, (t + 1) * SWA_BLOCK)
        j = pl.program_id(1) * (q_ref.shape[1] // SWA_BLOCK) + t
        is_lat = j < n_lat_tiles
        n = jnp.minimum(j, n_lat_tiles - 1)
        ws = jnp.clip((n - 1) * SWA_BLOCK, 0, s_len - band)
        kstart = pl.multiple_of(ws, SWA_BLOCK)
        qpos = n * SWA_BLOCK + lax.broadcasted_iota(I32, (2 * SWA_BLOCK, band), 0) % SWA_BLOCK
        kpos = ws + lax.broadcasted_iota(I32, (2 * SWA_BLOCK, band), 1)
        valid = (jnp.abs(qpos - kpos) <= SWA_WINDOW) & is_lat
        for hk in range(SWA_KV_HEADS):
            qpair = jnp.concatenate([q_ref[0, qrows, (2 * hk) * 128:(2 * hk + 1) * 128],
                                     q_ref[0, qrows, (2 * hk + 1) * 128:(2 * hk + 2) * 128]], axis=0)
            vcols = slice(hk * 128, (hk + 1) * 128)
            vc = v_ref[0, s_len:p_len, vcols]
            vb = v_ref[0, pl.ds(kstart, band), vcols]
            res = {}
            for par in range(2):
                kcols = slice((2 * hk + par) * 128, (2 * hk + par + 1) * 128)
                s_c = _dot_nt(qpair, k_ref[0, s_len:p_len, kcols])
                s_b = jnp.where(valid, _dot_nt(qpair, k_ref[0, pl.ds(kstart, band), kcols]), MASK_VALUE)
                h_top = 4 * hk + par
                h_bot = 4 * hk + 2 + par
                sink = jnp.where(top, sink_ref[h_top] * LOG2_E, sink_ref[h_bot] * LOG2_E)
                m = jnp.maximum(
                    jnp.maximum(jnp.max(s_c, axis=1, keepdims=True), jnp.max(s_b, axis=1, keepdims=True)), sink)
                o = _dot(jnp.exp2(s_c - m).astype(BF16), vc) + _dot(jnp.exp2(s_b - m).astype(BF16), vb)
                o = o / (o[:, SWA_HEAD_DIM:SWA_HEAD_DIM + 1] + jnp.exp2(sink - m))
                res[h_top] = o[0:SWA_BLOCK]
                res[h_bot] = o[SWA_BLOCK:2 * SWA_BLOCK]
            for pair in (2 * hk, 2 * hk + 1):
                y_ref[0, qrows, pair * 128:(pair + 1) * 128] = jnp.where(
                    first, res[2 * pair], pltpu.roll(res[2 * pair + 1], SWA_HEAD_DIM, 1)).astype(BF16)


def _swa_attention(sink, qs, ks, vs, n_ctx):
    b, p, _ = qs.shape
    tq = SWA_Q_TILE
    assert n_ctx % tq == 0 and (p - n_ctx) % tq == 0
    return pl.pallas_call(
        functools.partial(_swa_kernel, n_ctx=n_ctx),
        grid=(b, p // tq),
        in_specs=[pl.BlockSpec(memory_space=pltpu.SMEM),
                  pl.BlockSpec((1, tq, SWA_WIDTH), lambda bi, j: (bi, j, 0)),
                  pl.BlockSpec((1, p, ks.shape[2]), lambda bi, j: (bi, 0, 0)),
                  pl.BlockSpec((1, p, vs.shape[2]), lambda bi, j: (bi, 0, 0))],
        out_specs=pl.BlockSpec((1, tq, SWA_WIDTH), lambda bi, j: (bi, j, 0)),
        out_shape=jax.ShapeDtypeStruct((b, p, SWA_WIDTH), BF16),
        compiler_params=_cparams(("arbitrary", "arbitrary")),
        name="swa_attention",
    )(sink, qs, ks, vs)


def _pool_kernel(prev_ref, cur_ref, next_ref, w_ref, scale_ref, y_ref, ext_ref, *, n_ctx):
    j = pl.program_id(1)
    tm = cur_ref.shape[1]
    nj = pl.num_programs(1)
    s_len = (nj - 1) * tm
    is_ctx = j == nj - 1
    has_prev = (j >= 1) & (j < nj - 1)
    has_next = j < nj - 2
    ext_ref[0:POOL_HALO, :] = jnp.where(has_prev, prev_ref[0], 0.0)
    ext_ref[POOL_HALO:POOL_HALO + tm, :] = cur_ref[0]
    ext_ref[POOL_HALO + tm:POOL_HALO + tm + POOL_HALO, :] = jnp.where(has_next, next_ref[0], 0.0)
    t = lax.broadcasted_iota(I32, (tm, 1), 0)
    pos = jnp.where(is_ctx, t, j * tm + t)
    seg_len = jnp.where(is_ctx, n_ctx, s_len)
    for g, w in enumerate(POOL_WINDOWS):
        cols = slice(g * POOL_GROUP_DIM, (g + 1) * POOL_GROUP_DIM)
        acc = jnp.zeros((tm, POOL_GROUP_DIM), F32)
        for off in range(-(w // 2), w - w // 2):
            acc = acc + ext_ref[POOL_HALO + off:POOL_HALO + off + tm, cols]
        lo = jnp.maximum(pos - w // 2, 0)
        hi = jnp.minimum(pos + w - w // 2, seg_len)
        cnt = (hi - lo).astype(F32)
        pooled = acc / cnt - cur_ref[0, :, cols]
        y_ref[0, :, cols] = (_dot(pooled.astype(BF16), w_ref[g]) * scale_ref[:, cols]).astype(BF16)


def _pool_stage(u, w_pool, pool_scale, n_ctx):
    b, p, width = u.shape
    tm = TOKEN_TILE
    hb = tm // POOL_HALO
    n_halo_blocks = p // POOL_HALO
    return pl.pallas_call(
        functools.partial(_pool_kernel, n_ctx=n_ctx),
        grid=(b, p // tm),
        in_specs=[pl.BlockSpec((1, POOL_HALO, width), lambda bi, j: (bi, jnp.maximum(j * hb - 1, 0), 0)),
                  pl.BlockSpec((1, tm, width), lambda bi, j: (bi, j, 0)),
                  pl.BlockSpec((1, POOL_HALO, width),
                               lambda bi, j: (bi, jnp.minimum((j + 1) * hb, n_halo_blocks - 1), 0)),
                  pl.BlockSpec(w_pool.shape, lambda bi, j: (0, 0, 0)),
                  pl.BlockSpec(pool_scale.shape, lambda bi, j: (0, 0))],
        out_specs=pl.BlockSpec((1, tm, width), lambda bi, j: (bi, j, 0)),
        out_shape=jax.ShapeDtypeStruct((b, p, width), BF16),
        scratch_shapes=[pltpu.VMEM((tm + 2 * POOL_HALO, width), F32)],
        compiler_params=_cparams(("arbitrary", "arbitrary")),
        name="pool_stage",
    )(u, u, u, w_pool, pool_scale)


def _merge_kernel(yml_ref, ymc_ref, yp_ref, ys_ref, gate_ref, x_ref, mod_ref, g_ref, wbm_ref, wbp_ref, wbs_ref,
                  wout_ref, wsg_ref, wsu_ref, wsd_ref, wrt_ref, xs_ref, fp_ref, lg_ref, *, ctx_tile):
    d = SLAB_ROWS * 128
    mod = mod_ref[0, 0]
    gate = gate_ref[0]
    ym = jnp.where(pl.program_id(1) == ctx_tile, ymc_ref[0], yml_ref[0])
    merged = (gate[:, 0:d].astype(F32) * _dot(ym, wbm_ref[...])
              + gate[:, d:2 * d].astype(F32) * _dot(yp_ref[0], wbp_ref[...])
              + gate[:, 2 * d:3 * d].astype(F32) * _dot(ys_ref[0], wbs_ref[...]))
    x = _slab_rows_to_matrix(x_ref, gate.shape[0], SLAB_ROWS)
    x_mid = x + mod[2:3] * _dot(merged.astype(BF16), wout_ref[...])
    f = _rms(x_mid, g_ref[...]) * (1.0 + mod[4:5]) + mod[3:4]
    fb = f.astype(BF16)
    gsh = _dot(fb, wsg_ref[...])
    shared = _dot((gsh * _sigmoid(gsh) * _dot(fb, wsu_ref[...])).astype(BF16), wsd_ref[...])
    _matrix_to_slab_rows(xs_ref, x_mid + mod[5:6] * shared, SLAB_ROWS)
    _matrix_to_slab_rows(fp_ref, _pack_bf16_pair(f), PACKED_ROWS)
    lg_ref[...] = _dot_nt(wrt_ref[...], fb)


def _merge_stage(ym_lat, ym_ctx, yp, ys, gates, x_slab, modtab, g_ffn, wbm, wbp, wbs, wout, wsg, wsu, wsd, wrt,
                 keep_ctx):
    b, p, _ = yp.shape
    assert ym_ctx.shape[1] == TOKEN_TILE
    d = SLAB_ROWS * 128
    tm = TOKEN_TILE
    nj = p // tm
    nk = nj if keep_ctx else nj - 1
    tok = lambda w: pl.BlockSpec((1, tm, w), lambda bi, j: (bi, j, 0))
    slab_out = lambda rows: pl.BlockSpec((tm * rows, 128), lambda bi, j: (bi * nk + j, 0))
    full = lambda a: pl.BlockSpec(a.shape, lambda bi, j: (0,) * a.ndim)
    return pl.pallas_call(
        functools.partial(_merge_kernel, ctx_tile=nj - 1),
        grid=(b, nk),
        in_specs=[pl.BlockSpec((1, tm, ym_lat.shape[2]), lambda bi, j: (bi, jnp.minimum(j, nj - 2), 0)),
                  pl.BlockSpec((1, tm, ym_ctx.shape[2]), lambda bi, j: (bi, 0, 0)),
                  tok(yp.shape[2]), tok(ys.shape[2]), tok(gates.shape[2]),
                  pl.BlockSpec((tm * SLAB_ROWS, 128), lambda bi, j: (bi * nj + j, 0)),
                  pl.BlockSpec((1, 1, 8, d), lambda bi, j: (bi, j // (nj - 1), 0, 0)),
                  full(g_ffn), full(wbm), full(wbp), full(wbs), full(wout), full(wsg), full(wsu), full(wsd),
                  full(wrt)],
        out_specs=[slab_out(SLAB_ROWS), slab_out(PACKED_ROWS),
                   pl.BlockSpec((N_EXPERTS, tm), lambda bi, j: (0, bi * nk + j))],
        out_shape=[jax.ShapeDtypeStruct((b * nk * tm * SLAB_ROWS, 128), F32),
                   jax.ShapeDtypeStruct((b * nk * tm * PACKED_ROWS, 128), U32),
                   jax.ShapeDtypeStruct((N_EXPERTS, b * nk * tm), F32)],
        compiler_params=_cparams(("arbitrary", "arbitrary")),
        name="merge_stage",
    )(ym_lat, ym_ctx, yp, ys, gates, x_slab, modtab, g_ffn, wbm, wbp, wbs, wout, wsg, wsu, wsd, wrt)


def _route_kernel(lg_ref, bias_ref, dest_ref, wts_ref, seg_ref):
    tm = lg_ref.shape[1]
    ne = N_EXPERTS
    neg_inf = -jnp.inf
    scores = _sigmoid(lg_ref[...])
    sel = scores + bias_ref[...]
    iota_g = lax.broadcasted_iota(I32, (EXPERTS_PER_GROUP, tm), 0)
    gscore = []
    for g in range(N_GROUPS):
        sg = sel[g * EXPERTS_PER_GROUP:(g + 1) * EXPERTS_PER_GROUP]
        m1 = jnp.max(sg, axis=0, keepdims=True)
        i1 = jnp.min(jnp.where(sg == m1, iota_g, EXPERTS_PER_GROUP), axis=0, keepdims=True)
        m2 = jnp.max(jnp.where(iota_g == i1, neg_inf, sg), axis=0, keepdims=True)
        gscore.append(m1 + m2)
    rows = []
    for g in range(N_GROUPS):
        rank = jnp.zeros((1, tm), I32)
        for g2 in range(N_GROUPS):
            if g2 == g:
                continue
            beats = (gscore[g2] >= gscore[g]) if g2 < g else (gscore[g2] > gscore[g])
            rank = rank + beats.astype(I32)
        rows.append(jnp.where(rank < TOPK_GROUPS, sel[g * EXPERTS_PER_GROUP:(g + 1) * EXPERTS_PER_GROUP], neg_inf))
    cur = jnp.concatenate(rows, axis=0)
    iota_e = lax.broadcasted_iota(I32, (ne, tm), 0)
    picks = []
    member = jnp.zeros((ne, tm), F32)
    for _ in range(TOP_K):
        m = jnp.max(cur, axis=0, keepdims=True)
        idx = jnp.min(jnp.where(cur == m, iota_e, ne), axis=0, keepdims=True)
        hit = iota_e == idx
        picks.append(hit)
        member = member + hit.astype(F32)
        cur = jnp.where(hit, neg_inf, cur)
    earlier = (lax.broadcasted_iota(I32, (tm, tm), 0) < lax.broadcasted_iota(I32, (tm, tm), 1)).astype(BF16)
    pos = _dot(member.astype(BF16), earlier)
    cnt_col = jnp.sum(member, axis=1, keepdims=True)
    blocks_col = jnp.floor((cnt_col + (SEG_ALIGN - 1)) * (1.0 / SEG_ALIGN))
    lower = (lax.broadcasted_iota(I32, (ne, ne), 1) < lax.broadcasted_iota(I32, (ne, ne), 0)).astype(BF16)
    off_col = _dot(lower, jnp.broadcast_to(blocks_col, (ne, 128)).astype(BF16))[:, 0:1] * SEG_ALIGN
    base = off_col + pos
    w_rows = [jnp.sum(jnp.where(hit, scores, 0.0), axis=0, keepdims=True) for hit in picks]
    denom = w_rows[0]
    for w in w_rows[1:]:
        denom = denom + w
    for k, hit in enumerate(picks):
        dest_ref[0, k:k + 1, :] = jnp.sum(jnp.where(hit, base, 0.0), axis=0, keepdims=True).astype(I32)
        wts_ref[0, k:k + 1, :] = w_rows[k] / denom * ROUTED_SCALE
    member_pad = jnp.concatenate([member, jnp.zeros((128 - ne, tm), F32)], axis=0).astype(BF16)
    cnt_row = _dot_nt(jnp.ones((8, tm), BF16), member_pad)
    blocks_row = jnp.floor((cnt_row + (SEG_ALIGN - 1)) * (1.0 / SEG_ALIGN))
    before = (lax.broadcasted_iota(I32, (128, 128), 0) < lax.broadcasted_iota(I32, (128, 128), 1)).astype(BF16)
    off_row = _dot(blocks_row.astype(BF16), before) * SEG_ALIGN
    r = lax.broadcasted_iota(I32, (8, 128), 0)
    seg_ref[0] = jnp.where(r == 0, off_row, jnp.where(r == 1, cnt_row, 0.0)).astype(I32)


def _route_stage(lg_t, bias_col):
    ne, t_all = lg_t.shape
    tm = MOE_TILE
    nt = t_all // tm
    return pl.pallas_call(
        _route_kernel,
        grid=(nt,),
        in_specs=[pl.BlockSpec((ne, tm), lambda i: (0, i)),
                  pl.BlockSpec((ne, 1), lambda i: (0, 0))],
        out_specs=[pl.BlockSpec((1, TOP_K, tm), lambda i: (i, 0, 0)),
                   pl.BlockSpec((1, TOP_K, tm), lambda i: (i, 0, 0)),
                   pl.BlockSpec((1, 8, 128), lambda i: (i, 0, 0))],
        out_shape=[jax.ShapeDtypeStruct((nt, TOP_K, tm), I32),
                   jax.ShapeDtypeStruct((nt, TOP_K, tm), F32),
                   jax.ShapeDtypeStruct((nt, 8, 128), I32)],
        compiler_params=_cparams(("arbitrary",)),
        name="route_stage",
    )(lg_t, bias_col)


def _moe_row_stride(tm):
    cap = TOP_K * tm + N_EXPERTS * SEG_ALIGN + MOE_CHUNK
    blocks = cap // 8 + 1
    return 8 * (blocks + 1 - blocks % 2)


def _moe_kernel(dest_ref, wts_ref, seg_ref, fp_hbm, xs_hbm, gt_ref, wg_ref, wu_ref, wd_ref, out_hbm,
                bufs_ref, fp_stage, xs_stage, out_stage, fp_sem, xs_sem, out_sem,
                *, srow, sub, n_sub_per_batch, ctx_sub, n_tiles):
    pr = pl.program_id(0)
    eg = pl.program_id(1)
    tm = dest_ref.shape[2]
    ch = MOE_CHUNK
    group = wg_ref.shape[0]
    lane_blk = 128
    n_blk = tm // lane_blk
    per = bufs_ref.shape[0]
    n_here = jnp.minimum(per, n_tiles - pr * per)

    def table_block(ref, half, blk):
        return ref.at[half, :, pl.ds(pl.multiple_of(blk * lane_blk, lane_blk), lane_blk)]

    def token_rows(half, blk, per_tok):
        first = ((pr * per + half) * tm + blk * lane_blk) * per_tok
        return pl.ds(pl.multiple_of(first, lane_blk * per_tok), lane_blk * per_tok)

    def fp_copy(half, blk, slot):
        return pltpu.make_async_copy(fp_hbm.at[token_rows(half, blk, PACKED_ROWS), :], fp_stage.at[slot],
                                     fp_sem.at[slot])

    def xs_copy(half, blk, slot):
        return pltpu.make_async_copy(xs_hbm.at[token_rows(half, blk, SLAB_ROWS), :], xs_stage.at[slot],
                                     xs_sem.at[slot])

    def out_copy(half, blk, slot):
        return pltpu.make_async_copy(out_stage.at[slot], out_hbm.at[token_rows(half, blk, SLAB_ROWS), :],
                                     out_sem.at[slot])

    def group_rows(half):
        buf_ref = bufs_ref.at[half]
        buf_ref[...] = jnp.zeros(buf_ref.shape, U32)
        fp_copy(half, 0, 0).start()

        def body(blk, carry):
            slot = blk % 2
            fp_copy(half, blk, slot).wait()

            @pl.when(blk + 1 < n_blk)
            def _prefetch():
                fp_copy(half, blk + 1, 1 - slot).start()

            dest = table_block(dest_ref, half, blk)
            rows = fp_stage.at[slot]
            for u in range(lane_blk):
                slab = rows[u * PACKED_ROWS:(u + 1) * PACKED_ROWS, :]
                for k in range(TOP_K):
                    buf_ref[pl.ds(dest[k, u], PACKED_ROWS, stride=srow), :] = slab
            return carry

        lax.fori_loop(0, n_blk, body, 0)

    def experts():
        def one_expert(ge, carry):
            e = eg * group + ge
            counts = [seg_ref[half, 1, e] for half in range(per)]
            offsets = [seg_ref[half, 0, e] for half in range(per)]

            def ffn_rows(halves, rows, chunk_idx):
                words, keep = [], []
                for h in halves:
                    r0 = offsets[h] + chunk_idx * rows
                    words.append([bufs_ref[h, pl.ds(pl.multiple_of(q * srow + r0, 8), rows), :] for q in range(4)])
                    keep.append(lax.broadcasted_iota(I32, (rows, 1), 0) < counts[h] - chunk_idx * rows)
                w = jnp.concatenate([jnp.concatenate(ws, axis=1) for ws in words], axis=0)
                xb = jnp.concatenate([_unpack_lo(w), _unpack_hi(w)], axis=1).astype(BF16)
                g = _dot(xb, wg_ref[ge])
                hmid = g * _sigmoid(g) * _dot(xb, wu_ref[ge])
                packed = _pack_bf16_pair(_dot(hmid.astype(BF16), wd_ref[ge]))
                for n, h in enumerate(halves):
                    r0 = offsets[h] + chunk_idx * rows
                    for q in range(4):
                        bufs_ref[h, pl.ds(pl.multiple_of(q * srow + r0, 8), rows), :] = jnp.where(
                            keep[n], packed[n * rows:(n + 1) * rows, q * 128:(q + 1) * 128], words[n][q])

            most = counts[0]
            for c in counts[1:]:
                most = jnp.maximum(most, c)
            lo = 0
            for rows in MOE_BODY_ROWS:
                pl.when((most > lo) & (most <= rows))(functools.partial(ffn_rows, tuple(range(per)), rows, 0))
                lo = rows

            @pl.when(most > ch)
            def _long_segments():
                for h in range(per):
                    def chunk(c, carry2, h=h):
                        ffn_rows((h,), ch, c)
                        return carry2

                    lax.fori_loop(0, (counts[h] + ch - 1) // ch, chunk, 0)

            return carry

        lax.fori_loop(0, group, one_expert, 0)

    def combine(half):
        buf_ref = bufs_ref.at[half]
        xs_copy(half, 0, 0).start()

        def body(blk, carry):
            slot = blk % 2
            xs_copy(half, blk, slot).wait()

            @pl.when(blk + 1 < n_blk)
            def _prefetch():
                xs_copy(half, blk + 1, 1 - slot).start()

            @pl.when(blk >= 2)
            def _slot_free():
                out_copy(half, blk - 2, slot).wait()

            sub_blk = ((pr * per + half) * tm + blk * lane_blk) // sub
            bi = sub_blk // n_sub_per_batch
            gate = gt_ref[bi * 2 + (sub_blk - bi * n_sub_per_batch) // ctx_sub]
            dest = table_block(dest_ref, half, blk)
            wts = table_block(wts_ref, half, blk)
            xs = xs_stage.at[slot]
            out = out_stage.at[slot]
            for u in range(lane_blk):
                acc_lo = jnp.zeros((4, 128), F32)
                acc_hi = jnp.zeros((4, 128), F32)
                for k in range(TOP_K):
                    words = buf_ref[pl.ds(dest[k, u], PACKED_ROWS, stride=srow), :]
                    wk = wts[k, u]
                    acc_lo = acc_lo + wk * _unpack_lo(words)
                    acc_hi = acc_hi + wk * _unpack_hi(words)
                out[u * 8:u * 8 + 4, :] = xs[u * 8:u * 8 + 4, :] + gate[0:4] * acc_lo
                out[u * 8 + 4:u * 8 + 8, :] = xs[u * 8 + 4:u * 8 + 8, :] + gate[4:8] * acc_hi
            out_copy(half, blk, slot).start()
            return carry

        lax.fori_loop(0, n_blk, body, 0)
        for blk in (n_blk - 2, n_blk - 1):
            out_copy(half, blk, blk % 2).wait()

    def each_tile(fn):
        def body(half, carry):
            fn(half)
            return carry

        lax.fori_loop(0, n_here, body, 0)

    @pl.when(eg == 0)
    def _first_step():
        each_tile(group_rows)
        for half in range(1, per):
            @pl.when(half >= n_here)
            def _clear(half=half):
                bufs_ref[half] = jnp.zeros(bufs_ref.shape[1:], U32)

    experts()
    pl.when(eg == pl.num_programs(1) - 1)(functools.partial(each_tile, combine))


def _moe_stage(dest, wts, seg, fp4, xs8, gt2, wg, wu, wd, n_sub_per_batch, keep_ctx):
    ctx_sub = n_sub_per_batch - 1 if keep_ctx else n_sub_per_batch
    nt, _, tm = dest.shape
    ne = wg.shape[0]
    group = MOE_EXPERTS_PER_STEP
    per = MOE_TILES_PER_STEP
    assert ne % group == 0 and tm % 256 == 0 and TOKEN_TILE % 128 == 0 and per == 2
    n_rows = -(-nt // per)
    pad = ((0, n_rows * per - nt), (0, 0), (0, 0))
    dest, wts, seg = jnp.pad(dest, pad), jnp.pad(wts, pad), jnp.pad(seg, pad)
    srow = _moe_row_stride(tm)
    smem = lambda shape: pl.BlockSpec(shape, lambda i, e: (i, 0, 0), memory_space=pltpu.SMEM)
    hbm = pl.BlockSpec(memory_space=pl.ANY)
    stage = lambda per_tok, dt: pltpu.VMEM((2, 128 * per_tok, 128), dt)
    return pl.pallas_call(
        functools.partial(_moe_kernel, srow=srow, sub=TOKEN_TILE, n_sub_per_batch=n_sub_per_batch, ctx_sub=ctx_sub,
                          n_tiles=nt),
        grid=(n_rows, ne // group),
        in_specs=[smem((per, TOP_K, tm)), smem((per, TOP_K, tm)), smem((per, 8, 128)),
                  hbm, hbm,
                  pl.BlockSpec(gt2.shape, lambda i, e: (0, 0, 0)),
                  pl.BlockSpec((group,) + wg.shape[1:], lambda i, e: (e, 0, 0)),
                  pl.BlockSpec((group,) + wu.shape[1:], lambda i, e: (e, 0, 0)),
                  pl.BlockSpec((group,) + wd.shape[1:], lambda i, e: (e, 0, 0))],
        out_specs=hbm,
        out_shape=jax.ShapeDtypeStruct(xs8.shape, F32),
        scratch_shapes=[pltpu.VMEM((per, 4 * srow, 128), U32),
                        stage(PACKED_ROWS, U32), stage(SLAB_ROWS, F32), stage(SLAB_ROWS, F32),
                        pltpu.SemaphoreType.DMA((2,)), pltpu.SemaphoreType.DMA((2,)),
                        pltpu.SemaphoreType.DMA((2,))],
        compiler_params=_cparams(("arbitrary", "arbitrary")),
        name="moe_stage",
    )(dest, wts, seg, fp4, xs8, gt2, wg, wu, wd)


def _final_kernel(x_ref, g_ref, o_ref):
    o_ref[0] = _rms(_slab_rows_to_matrix(x_ref, o_ref.shape[1], SLAB_ROWS), g_ref[...])


def _final_norm(x_slab, b, g_final, n_ctx):
    d = SLAB_ROWS * 128
    p = x_slab.shape[0] // (b * SLAB_ROWS)
    tm = TOKEN_TILE
    nj = p // tm
    return pl.pallas_call(
        _final_kernel,
        grid=(b, (p - n_ctx) // tm),
        in_specs=[pl.BlockSpec((tm * SLAB_ROWS, 128), lambda bi, j: (bi * nj + j, 0)),
                  pl.BlockSpec((1, d), lambda bi, j: (0, 0))],
        out_specs=pl.BlockSpec((1, tm, d), lambda bi, j: (bi, j, 0)),
        out_shape=jax.ShapeDtypeStruct((b, p - n_ctx, d), F32),
        compiler_params=_cparams(("arbitrary", "arbitrary")),
        name="final_norm",
    )(x_slab, g_final)


def _rot_cols(w, n_heads, rot_dim):
    kdim = w.shape[0]
    w4 = w.reshape(kdim, n_heads, 4, rot_dim // 4)
    rot = jnp.stack([-w4[:, :, 1], w4[:, :, 0], -w4[:, :, 3], w4[:, :, 2]], axis=2)
    return rot.reshape(kdim, n_heads * rot_dim)


def _rope_pattern(s_len, n_ctx, rot_dim):
    t = jnp.arange(s_len)
    row = (t // GRID_W).astype(F32)
    col = (t % GRID_W).astype(F32)
    n_freq = rot_dim // 4
    inv_freq = ROPE_BASE ** (-jnp.arange(n_freq, dtype=F32) / n_freq)
    ang_r = row[:, None] * inv_freq[None, :]
    ang_c = col[:, None] * inv_freq[None, :]
    cos = jnp.concatenate([jnp.cos(ang_r), jnp.cos(ang_r), jnp.cos(ang_c), jnp.cos(ang_c)], axis=1)
    sin = jnp.concatenate([jnp.sin(ang_r), jnp.sin(ang_r), jnp.sin(ang_c), jnp.sin(ang_c)], axis=1)
    cos = jnp.concatenate([cos, jnp.ones((n_ctx, rot_dim), F32)], axis=0)
    sin = jnp.concatenate([sin, jnp.zeros((n_ctx, rot_dim), F32)], axis=0)
    return cos, sin


def _layer_weights(w_in, w_uq, w_ukv):
    d = w_in.shape[0]
    offs = [0]
    for w in (MLA_Q_LORA, MLA_KV_LORA, MLA_ROPE, POOL_WIDTH, SWA_WIDTH, SWA_KV_WIDTH, SWA_KV_WIDTH, 3 * d):
        offs.append(offs[-1] + w)
    cq, ckv, kr, u, qs, ks, vs, gl = (w_in[:, offs[i]:offs[i + 1]] for i in range(8))
    qs = qs * (SWA_SCALE * LOG2_E)
    zeros = lambda n: jnp.zeros((d, n), F32)
    kr_slot = jnp.concatenate([zeros(MLA_NOPE), kr, zeros(MLA_HEAD_PAD - MLA_NOPE - MLA_ROPE)], axis=1)
    kr_rot_slot = jnp.concatenate([zeros(MLA_NOPE), _rot_cols(kr, 1, MLA_ROPE),
                                   zeros(MLA_HEAD_PAD - MLA_NOPE - MLA_ROPE)], axis=1)

    def kv_slots(w):
        w3 = w.reshape(d, SWA_KV_HEADS, SWA_HEAD_DIM)
        return jnp.concatenate([w3, jnp.zeros_like(w3)], axis=2).reshape(d, SWA_KV_HEADS * 128)

    w1 = jnp.concatenate([cq, ckv, u, qs, _rot_cols(qs, SWA_Q_HEADS, SWA_HEAD_DIM), kv_slots(ks),
                          kv_slots(_rot_cols(ks, SWA_KV_HEADS, SWA_HEAD_DIM)), kv_slots(vs), kr_slot, kr_rot_slot,
                          gl], axis=1).astype(BF16)
    lq = w_uq.shape[0]
    wq3 = (w_uq * (MLA_SCALE * LOG2_E)).reshape(lq, MLA_HEADS, MLA_NOPE + MLA_ROPE)
    nope, rope = wq3[:, :, :MLA_NOPE], wq3[:, :, MLA_NOPE:]
    rope_rot = _rot_cols(rope.reshape(lq, MLA_HEADS * MLA_ROPE), MLA_HEADS, MLA_ROPE).reshape(lq, MLA_HEADS, MLA_ROPE)
    pad = jnp.zeros((lq, MLA_HEADS, MLA_HEAD_PAD - MLA_NOPE - MLA_ROPE), F32)
    wq_a = jnp.concatenate([nope, rope, pad], axis=2).reshape(lq, MLA_HEADS * MLA_HEAD_PAD)
    wq_b = jnp.concatenate([jnp.zeros_like(nope), rope_rot, pad], axis=2).reshape(lq, MLA_HEADS * MLA_HEAD_PAD)
    wq = jnp.concatenate([wq_a, wq_b], axis=1).astype(BF16)
    lkv = w_ukv.shape[0]
    wkv3 = w_ukv.reshape(lkv, MLA_HEADS, MLA_NOPE + MLA_V)
    wkn = jnp.concatenate([wkv3[:, :, :MLA_NOPE], jnp.zeros((lkv, MLA_HEADS, MLA_HEAD_PAD - MLA_NOPE), F32)],
                          axis=2).reshape(lkv, MLA_HEADS * MLA_HEAD_PAD).astype(BF16)
    wv = jnp.concatenate([wkv3[:, :, MLA_NOPE:], jnp.zeros((lkv, MLA_HEADS, MLA_HEAD_PAD - MLA_V), F32)],
                         axis=2).reshape(lkv, MLA_HEADS * MLA_HEAD_PAD).astype(BF16)
    return w1, wq, wkn, wv


def kernel(x, c, ctx, c_ctx, w_mod, b_mod, g_mix, g_ffn, w_in, g_mla_q, g_mla_kv, w_mla_uq, w_mla_ukv, w_pool,
           pool_scale, swa_sink, w_br_mla, w_br_pool, w_br_swa, w_out, w_router, router_bias, w_exp_gate,
           w_exp_up, w_exp_down, w_sh_gate, w_sh_up, w_sh_down, g_final):
    b, s_len, d = x.shape
    n_ctx = ctx.shape[1]
    n_layers = w_mod.shape[0]
    p = n_ctx + s_len
    assert n_ctx == TOKEN_TILE and s_len % TOKEN_TILE == 0 and (b * p) % MOE_TILE == 0 and b + 1 <= 8
    assert d == 1024 and w_in.shape[2] == (MLA_Q_LORA + MLA_KV_LORA + MLA_ROPE + POOL_WIDTH + SWA_WIDTH
                                           + 2 * SWA_KV_WIDTH + 3 * d)

    cm, sm = _rope_pattern(s_len, n_ctx, MLA_ROPE)
    tail = jnp.zeros((p, MLA_HEAD_PAD - MLA_NOPE - MLA_ROPE), F32)
    tabm = jnp.concatenate([jnp.ones((p, MLA_NOPE), F32), cm, tail, jnp.zeros((p, MLA_NOPE), F32), sm, tail], axis=1)
    cs, ss = _rope_pattern(s_len, n_ctx, SWA_HEAD_DIM)
    tabs = jnp.concatenate([cs, cs, ss, ss], axis=1)

    cvec = jnp.concatenate([c, c_ctx[None, :], jnp.zeros((8 - b - 1, d), F32)], axis=0)
    mod_all = _modulation(cvec, w_mod, b_mod).reshape(n_layers, 8, N_MOD, d)

    x_slab = _assemble_stream(x, ctx)
    for i in range(n_layers):
        keep_ctx = i < n_layers - 1
        lat = mod_all[i, :b]
        ctx_rows = jnp.broadcast_to(mod_all[i, b][None], (b, N_MOD, d))
        modtab = jnp.pad(jnp.stack([lat, ctx_rows], axis=1), ((0, 0), (0, 0), (0, 8 - N_MOD), (0, 0)))
        w1, wq, wkn, wv = _layer_weights(w_in[i], w_mla_uq[i], w_mla_ukv[i])
        q, k, v, u, qs, ks, vs, gates = _input_stage(
            x_slab, b, modtab, g_mix[i][None], w1, g_mla_q[i][None], wq, g_mla_kv[i][None], wkn, wv, tabm, tabs)
        y_mla, y_mla_ctx = _mla_attention(q, k, v, n_ctx)
        y_swa = _swa_attention(swa_sink[i], qs, ks, vs, n_ctx)
        y_pool = _pool_stage(u, w_pool[i].astype(BF16), pool_scale[i][None], n_ctx)
        xs, fp, lg_t = _merge_stage(
            y_mla, y_mla_ctx, y_pool, y_swa, gates, x_slab, modtab, g_ffn[i][None], w_br_mla[i].astype(BF16),
            w_br_pool[i].astype(BF16), w_br_swa[i].astype(BF16), w_out[i].astype(BF16),
            w_sh_gate[i].astype(BF16), w_sh_up[i].astype(BF16), w_sh_down[i].astype(BF16),
            w_router[i].T.astype(BF16), keep_ctx)
        dest, wts, seg = _route_stage(lg_t, router_bias[i][:, None])
        gt2 = modtab[:, :, 5, :].reshape(b * 2, 8, d // 8)
        x_slab = _moe_stage(dest, wts, seg, fp, xs, gt2, w_exp_gate[i].astype(BF16), w_exp_up[i].astype(BF16),
                            w_exp_down[i].astype(BF16), (p if keep_ctx else s_len) // TOKEN_TILE, keep_ctx)
    return _final_norm(x_slab, b, g_final[None], 0)
```

```python
import functools

import jax
import jax.numpy as jnp
from jax import lax
from jax.experimental import pallas as pl
from jax.experimental.pallas import tpu as pltpu

F32 = jnp.float32
BF16 = jnp.bfloat16
U32 = jnp.uint32
I32 = jnp.int32

NORM_EPS = 1e-6
ROPE_BASE = 10000.0
GRID_W = 64
N_MOD = 6

MLA_HEADS = 8
MLA_Q_LORA = 384
MLA_KV_LORA = 256
MLA_NOPE = 64
MLA_ROPE = 32
MLA_V = 64
MLA_SCALE = (MLA_NOPE + MLA_ROPE) ** -0.5
MLA_HEAD_PAD = 128
LOG2_E = 1.4426950408889634

POOL_WINDOWS = (2, 4, 8, 16)
POOL_GROUP_DIM = 128
POOL_WIDTH = 512
POOL_HALO = 8

SWA_Q_HEADS = 8
SWA_KV_HEADS = 2
SWA_HEAD_DIM = 64
SWA_WINDOW = 128
SWA_BLOCK = 128
SWA_SCALE = SWA_HEAD_DIM ** -0.5
SWA_WIDTH = SWA_Q_HEADS * SWA_HEAD_DIM
SWA_KV_WIDTH = SWA_KV_HEADS * SWA_HEAD_DIM

N_EXPERTS = 64
TOP_K = 8
N_GROUPS = 8
TOPK_GROUPS = 4
EXPERTS_PER_GROUP = 8
D_EXPERT = 256
ROUTED_SCALE = 2.5

TOKEN_TILE = 256
SWA_Q_TILE = 256
MLA_Q_TILE = 1024
MLA_KEY_CHUNK = 256
MOE_TILE = 1024
MOE_CHUNK = 256
MOE_BODY_ROWS = (128, 160, 192, 224, 256)
MOE_EXPERTS_PER_STEP = 2
MOE_TILES_PER_STEP = 2
MOE_STAGE_SLOTS = 4
SEG_ALIGN = 8
SLAB_ROWS = 8
PACKED_ROWS = 4
MASK_VALUE = -1e30
HI16 = 0xFFFF0000

VMEM_LIMIT = 56 * 1024 * 1024

_SEG_WIDTHS = (("cq", 384), ("ckv", 256), ("u", 512), ("qs", 512), ("qs_rot", 512), ("ks", 256),
               ("ks_rot", 256), ("vs", 256), ("kr", 128), ("kr_rot", 128), ("gl", 3072))
_SEG = {}
_o = 0
for _n, _w in _SEG_WIDTHS:
    _SEG[_n] = (_o, _o + _w)
    _o += _w
FUSED_IN_WIDTH = _o


def _cparams(sem):
    return pltpu.CompilerParams(dimension_semantics=sem, vmem_limit_bytes=VMEM_LIMIT)


def _dot(a, b):
    return jnp.dot(a, b, preferred_element_type=F32)


def _dot_nt(a, b):
    return lax.dot_general(a, b, (((1,), (1,)), ((), ())), preferred_element_type=F32)


def _sigmoid(x):
    return 1.0 / (1.0 + jnp.exp(-x))


def _rms(x, g):
    return x * lax.rsqrt(jnp.mean(x * x, axis=-1, keepdims=True) + NORM_EPS) * g


def _pack_bf16_pair(v):
    n = v.shape[1] // 2
    bits = pltpu.bitcast(v.astype(BF16).astype(F32), U32)
    return (bits[:, :n] >> 16) | (bits[:, n:] & jnp.uint32(HI16))


def _slab_rows_to_matrix(ref, n_tok, per_tok):
    return jnp.concatenate([ref[pl.ds(c, n_tok, stride=per_tok), :] for c in range(per_tok)], axis=1)


def _matrix_to_slab_rows(ref, val, per_tok):
    n_tok = val.shape[0]
    for c in range(per_tok):
        ref[pl.ds(c, n_tok, stride=per_tok), :] = val[:, c * 128:(c + 1) * 128]


def _unpack_lo(w):
    return pltpu.bitcast(w << 16, F32)


def _unpack_hi(w):
    return pltpu.bitcast(w & jnp.uint32(HI16), F32)


def _mod_kernel(c_ref, w_ref, b_ref, o_ref):
    c = c_ref[...]
    a = (c * _sigmoid(c)).astype(BF16)
    o_ref[0] = _dot(a, w_ref[0].astype(BF16)) + b_ref[0]


def _modulation(cvec, w_mod, b_mod):
    n_layers, d, width = w_mod.shape
    tn = width // 4
    return pl.pallas_call(
        _mod_kernel,
        grid=(n_layers, width // tn),
        in_specs=[pl.BlockSpec((8, d), lambda l, n: (0, 0)),
                  pl.BlockSpec((1, d, tn), lambda l, n: (l, 0, n)),
                  pl.BlockSpec((1, 1, tn), lambda l, n: (l, 0, n))],
        out_specs=pl.BlockSpec((1, 8, tn), lambda l, n: (l, 0, n)),
        out_shape=jax.ShapeDtypeStruct((n_layers, 8, width), F32),
        compiler_params=_cparams(("arbitrary", "arbitrary")),
        name="modulation",
    )(cvec, w_mod, b_mod.reshape(n_layers, 1, width))


def _assemble_kernel(x_ref, ctx_ref, o_ref):
    val = jnp.where(pl.program_id(1) == pl.num_programs(1) - 1, ctx_ref[0], x_ref[0])
    _matrix_to_slab_rows(o_ref, val, SLAB_ROWS)


def _assemble_stream(x, ctx):
    b, s_len, d = x.shape
    tm = TOKEN_TILE
    assert ctx.shape[1] == tm and d == SLAB_ROWS * 128
    nj = s_len // tm + 1
    return pl.pallas_call(
        _assemble_kernel,
        grid=(b, nj),
        in_specs=[pl.BlockSpec((1, tm, d), lambda bi, j: (bi, jnp.minimum(j, nj - 2), 0)),
                  pl.BlockSpec((1, tm, d), lambda bi, j: (bi, 0, 0))],
        out_specs=pl.BlockSpec((tm * SLAB_ROWS, 128), lambda bi, j: (bi * nj + j, 0)),
        out_shape=jax.ShapeDtypeStruct((b * nj * tm * SLAB_ROWS, 128), F32),
        compiler_params=_cparams(("arbitrary", "arbitrary")),
        name="assemble_stream",
    )(x, ctx)


def _in_kernel(x_ref, mod_ref, g_ref, w1_ref, gq_ref, wq_ref, gkv_ref, wkn_ref, wv_ref, tabm_ref, tabs_ref,
               q_ref, k_ref, v_ref, u_ref, qs_ref, ks_ref, vs_ref, gate_ref):
    x = _slab_rows_to_matrix(x_ref, q_ref.shape[1], SLAB_ROWS)
    mod = mod_ref[0, 0]
    h = _rms(x, g_ref[...]) * (1.0 + mod[1:2]) + mod[0:1]
    hb = h.astype(BF16)

    def seg(name):
        a, b = _SEG[name]
        return _dot(hb, w1_ref[:, a:b])

    cos_m = tabm_ref[:, 0:128]
    sin_m = tabm_ref[:, 128:256]
    cos_s = tabs_ref[:, 0:128]
    sin_s = tabs_ref[:, 128:256]

    cqn = _rms(seg("cq"), gq_ref[...]).astype(BF16)
    qa = _dot(cqn, wq_ref[:, 0:1024])
    qb = _dot(cqn, wq_ref[:, 1024:2048])
    q_ref[0] = (qa * jnp.tile(cos_m, (1, MLA_HEADS)) + qb * jnp.tile(sin_m, (1, MLA_HEADS))).astype(BF16)

    ckvn = _rms(seg("ckv"), gkv_ref[...]).astype(BF16)
    kr = seg("kr") * cos_m + seg("kr_rot") * sin_m
    k_ref[0] = (_dot(ckvn, wkn_ref[...]) + jnp.tile(kr, (1, MLA_HEADS))).astype(BF16)
    ones_lane = (lax.broadcasted_iota(I32, (1, MLA_HEADS * MLA_HEAD_PAD), 1) % MLA_HEAD_PAD == MLA_V).astype(F32)
    v_ref[0] = (_dot(ckvn, wv_ref[...]) + ones_lane).astype(BF16)

    u_ref[0] = seg("u")

    qs_ref[0] = (seg("qs") * jnp.tile(cos_s, (1, 4)) + seg("qs_rot") * jnp.tile(sin_s, (1, 4))).astype(BF16)
    ks = seg("ks") * jnp.tile(cos_s, (1, 2)) + seg("ks_rot") * jnp.tile(sin_s, (1, 2))
    for hk in range(SWA_KV_HEADS):
        k_lo = ks[:, hk * 128:(hk + 1) * 128]
        ks_ref[0, :, (2 * hk) * 128:(2 * hk + 1) * 128] = k_lo.astype(BF16)
        ks_ref[0, :, (2 * hk + 1) * 128:(2 * hk + 2) * 128] = pltpu.roll(k_lo, SWA_HEAD_DIM, 1).astype(BF16)
    ones_s = (lax.broadcasted_iota(I32, (1, SWA_KV_HEADS * 128), 1) % 128 == SWA_HEAD_DIM).astype(F32)
    vs_ref[0] = (seg("vs") + ones_s).astype(BF16)

    g0, _ = _SEG["gl"]
    for p in range(6):
        gate_ref[0, :, p * 512:(p + 1) * 512] = _sigmoid(
            _dot(hb, w1_ref[:, g0 + p * 512:g0 + (p + 1) * 512])).astype(BF16)


def _input_stage(x_slab, b, modtab, g_mix, w1, g_q, wq, g_kv, wkn, wv, tabm, tabs):
    d = SLAB_ROWS * 128
    p = x_slab.shape[0] // (b * SLAB_ROWS)
    tm = TOKEN_TILE
    nj = p // tm
    tok = lambda w: pl.BlockSpec((1, tm, w), lambda bi, j: (bi, j, 0))
    full = lambda a: pl.BlockSpec(a.shape, lambda bi, j: (0,) * a.ndim)
    outs = [(1024, BF16), (1024, BF16), (1024, BF16), (512, F32), (512, BF16), (512, BF16), (256, BF16),
            (3072, BF16)]
    return pl.pallas_call(
        _in_kernel,
        grid=(b, nj),
        in_specs=[pl.BlockSpec((tm * SLAB_ROWS, 128), lambda bi, j: (bi * nj + j, 0)),
                  pl.BlockSpec((1, 1, 8, d), lambda bi, j: (bi, j // (nj - 1), 0, 0)),
                  full(g_mix), full(w1), full(g_q), full(wq), full(g_kv), full(wkn), full(wv),
                  pl.BlockSpec((tm, 256), lambda bi, j: (j, 0)),
                  pl.BlockSpec((tm, 256), lambda bi, j: (j, 0))],
        out_specs=[tok(w) for w, _ in outs],
        out_shape=[jax.ShapeDtypeStruct((b, p, w), dt) for w, dt in outs],
        compiler_params=_cparams(("arbitrary", "arbitrary")),
        name="input_stage",
    )(x_slab, modtab, g_mix, w1, g_q, wq, g_kv, wkn, wv, tabm, tabs)


def _mla_kernel(q_ref, k_ref, v_ref, *rest, tk, n_main, tail_rows):
    y_ref, m_ref, acc_ref = rest[-3:]
    hw = MLA_HEAD_PAD
    m_ref[...] = jnp.full(m_ref.shape, MASK_VALUE, F32)
    acc_ref[...] = jnp.zeros(acc_ref.shape, F32)

    def attend(r0, rows):
        for h in range(MLA_HEADS):
            q = q_ref[0, :, h * hw:(h + 1) * hw]
            k = k_ref[0, pl.ds(r0, rows), h * hw:(h + 1) * hw]
            v = v_ref[0, pl.ds(r0, rows), h * hw:(h + 1) * hw]
            s = _dot_nt(q, k)
            m_prev = m_ref[h]
            m_new = jnp.maximum(m_prev, jnp.max(s, axis=1, keepdims=True))
            m_ref[h] = m_new
            pr = jnp.exp2(s - jnp.tile(m_new, (1, rows // hw)))
            acc_ref[h] = jnp.exp2(m_prev - m_new) * acc_ref[h] + _dot(pr.astype(BF16), v)

    if n_main:
        def step(c, carry):
            attend(pl.multiple_of(c * tk, tk), tk)
            return carry

        lax.fori_loop(0, n_main, step, 0)
    if tail_rows:
        attend(n_main * tk, tail_rows)

    first = lax.broadcasted_iota(I32, (q_ref.shape[1], hw), 1) < MLA_V
    for hp in range(MLA_HEADS // 2):
        o = []
        for h in (2 * hp, 2 * hp + 1):
            acc = acc_ref[h]
            o.append(acc / acc[:, MLA_V:MLA_V + 1])
        y_ref[0, :, hp * hw:(hp + 1) * hw] = jnp.where(first, o[0], pltpu.roll(o[1], MLA_V, 1)).astype(BF16)


def _mla_attention(q, k, v, n_ctx):
    b, p, width = q.shape
    s_len = p - n_ctx
    tq = MLA_Q_TILE
    tk = MLA_KEY_CHUNK
    assert s_len % tq == 0 and s_len % tk == 0 and s_len % n_ctx == 0
    out_w = MLA_HEADS * MLA_V
    state = lambda rows: pltpu.VMEM((MLA_HEADS, rows, MLA_HEAD_PAD), F32)
    y_lat = pl.pallas_call(
        functools.partial(_mla_kernel, tk=tk, n_main=p // tk, tail_rows=p % tk),
        grid=(b, s_len // tq),
        in_specs=[pl.BlockSpec((1, tq, width), lambda bi, j: (bi, j, 0)),
                  pl.BlockSpec((1, p, width), lambda bi, j: (bi, 0, 0), pipeline_mode=pl.Buffered(1)),
                  pl.BlockSpec((1, p, width), lambda bi, j: (bi, 0, 0), pipeline_mode=pl.Buffered(1))],
        out_specs=pl.BlockSpec((1, tq, out_w), lambda bi, j: (bi, j, 0)),
        out_shape=jax.ShapeDtypeStruct((b, s_len, out_w), BF16),
        scratch_shapes=[state(tq), state(tq)],
        compiler_params=_cparams(("arbitrary", "arbitrary")),
        name="mla_attention",
    )(q, k, v)
    cblk = s_len // n_ctx
    ctx_rows = lambda w: pl.BlockSpec((1, n_ctx, w), lambda bi: (bi, cblk, 0))
    y_ctx = pl.pallas_call(
        functools.partial(_mla_kernel, tk=tk, n_main=0, tail_rows=n_ctx),
        grid=(b,),
        in_specs=[ctx_rows(width), ctx_rows(width), ctx_rows(width)],
        out_specs=pl.BlockSpec((1, n_ctx, out_w), lambda bi: (bi, 0, 0)),
        out_shape=jax.ShapeDtypeStruct((b, n_ctx, out_w), BF16),
        scratch_shapes=[state(n_ctx), state(n_ctx)],
        compiler_params=_cparams(("arbitrary",)),
        name="mla_attention_ctx",
    )(q, k, v)
    return y_lat, y_ctx


def _swa_kernel(sink_ref, q_ref, k_ref, v_ref, y_ref, *, n_ctx):
    p_len = k_ref.shape[1]
    s_len = p_len - n_ctx
    band = 3 * SWA_BLOCK
    n_lat_tiles = s_len // SWA_BLOCK
    first = lax.broadcasted_iota(I32, (SWA_BLOCK, 128), 1) < SWA_HEAD_DIM
    top = lax.broadcasted_iota(I32, (2 * SWA_BLOCK, 1), 0) < SWA_BLOCK
    n_blocks = q_ref.shape[1] // SWA_BLOCK
    chains = [(t, hk, par) for t in range(n_blocks) for hk in range(SWA_KV_HEADS) for par in range(2)]
    kstart, valid = {}, {}
    for t in range(n_blocks):
        j = pl.program_id(1) * n_blocks + t
        n = jnp.minimum(j, n_lat_tiles - 1)
        ws = jnp.clip((n - 1) * SWA_BLOCK, 0, s_len - band)
        kstart[t] = pl.multiple_of(ws, SWA_BLOCK)
        qpos = n * SWA_BLOCK + lax.broadcasted_iota(I32, (2 * SWA_BLOCK, band), 0) % SWA_BLOCK
        kpos = ws + lax.broadcasted_iota(I32, (2 * SWA_BLOCK, band), 1)
        valid[t] = (jnp.abs(qpos - kpos) <= SWA_WINDOW) & (j < n_lat_tiles)
    scores = {}
    for t, hk, par in chains:
        qrows = slice(t * SWA_BLOCK, (t + 1) * SWA_BLOCK)
        qpair = jnp.concatenate([q_ref[0, qrows, (2 * hk) * 128:(2 * hk + 1) * 128],
                                 q_ref[0, qrows, (2 * hk + 1) * 128:(2 * hk + 2) * 128]], axis=0)
        kcols = slice((2 * hk + par) * 128, (2 * hk + par + 1) * 128)
        s_c = _dot_nt(qpair, k_ref[0, s_len:p_len, kcols])
        s_b = jnp.where(valid[t], _dot_nt(qpair, k_ref[0, pl.ds(kstart[t], band), kcols]), MASK_VALUE)
        scores[t, hk, par] = (s_c, s_b)
    probs = {}
    for t, hk, par in chains:
        s_c, s_b = scores[t, hk, par]
        sink = jnp.where(top, sink_ref[4 * hk + par] * LOG2_E, sink_ref[4 * hk + 2 + par] * LOG2_E)
        m = jnp.maximum(jnp.maximum(jnp.max(s_c, axis=1, keepdims=True), jnp.max(s_b, axis=1, keepdims=True)), sink)
        probs[t, hk, par] = (jnp.exp2(s_c - m).astype(BF16), jnp.exp2(s_b - m).astype(BF16), jnp.exp2(sink - m))
    res = {}
    for t, hk, par in chains:
        p_c, p_b, p_sink = probs[t, hk, par]
        vcols = slice(hk * 128, (hk + 1) * 128)
        o = _dot(p_c, v_ref[0, s_len:p_len, vcols]) + _dot(p_b, v_ref[0, pl.ds(kstart[t], band), vcols])
        o = o / (o[:, SWA_HEAD_DIM:SWA_HEAD_DIM + 1] + p_sink)
        res[t, 4 * hk + par] = o[0:SWA_BLOCK]
        res[t, 4 * hk + 2 + par] = o[SWA_BLOCK:2 * SWA_BLOCK]
    for t in range(n_blocks):
        for pair in range(SWA_Q_HEADS // 2):
            y_ref[0, t * SWA_BLOCK:(t + 1) * SWA_BLOCK, pair * 128:(pair + 1) * 128] = jnp.where(
                first, res[t, 2 * pair], pltpu.roll(res[t, 2 * pair + 1], SWA_HEAD_DIM, 1)).astype(BF16)


def _swa_attention(sink, qs, ks, vs, n_ctx):
    b, p, _ = qs.shape
    tq = SWA_Q_TILE
    assert n_ctx % tq == 0 and (p - n_ctx) % tq == 0
    return pl.pallas_call(
        functools.partial(_swa_kernel, n_ctx=n_ctx),
        grid=(b, p // tq),
        in_specs=[pl.BlockSpec(memory_space=pltpu.SMEM),
                  pl.BlockSpec((1, tq, SWA_WIDTH), lambda bi, j: (bi, j, 0)),
                  pl.BlockSpec((1, p, ks.shape[2]), lambda bi, j: (bi, 0, 0)),
                  pl.BlockSpec((1, p, vs.shape[2]), lambda bi, j: (bi, 0, 0))],
        out_specs=pl.BlockSpec((1, tq, SWA_WIDTH), lambda bi, j: (bi, j, 0)),
        out_shape=jax.ShapeDtypeStruct((b, p, SWA_WIDTH), BF16),
        compiler_params=_cparams(("arbitrary", "arbitrary")),
        name="swa_attention",
    )(sink, qs, ks, vs)


def _pool_kernel(prev_ref, cur_ref, next_ref, w_ref, scale_ref, y_ref, ext_ref, *, n_ctx):
    j = pl.program_id(1)
    tm = cur_ref.shape[1]
    nj = pl.num_programs(1)
    s_len = (nj - 1) * tm
    is_ctx = j == nj - 1
    has_prev = (j >= 1) & (j < nj - 1)
    has_next = j < nj - 2
    ext_ref[0:POOL_HALO, :] = jnp.where(has_prev, prev_ref[0], 0.0)
    ext_ref[POOL_HALO:POOL_HALO + tm, :] = cur_ref[0]
    ext_ref[POOL_HALO + tm:POOL_HALO + tm + POOL_HALO, :] = jnp.where(has_next, next_ref[0], 0.0)
    t = lax.broadcasted_iota(I32, (tm, 1), 0)
    pos = jnp.where(is_ctx, t, j * tm + t)
    seg_len = jnp.where(is_ctx, n_ctx, s_len)
    for g, w in enumerate(POOL_WINDOWS):
        cols = slice(g * POOL_GROUP_DIM, (g + 1) * POOL_GROUP_DIM)
        acc = jnp.zeros((tm, POOL_GROUP_DIM), F32)
        for off in range(-(w // 2), w - w // 2):
            acc = acc + ext_ref[POOL_HALO + off:POOL_HALO + off + tm, cols]
        lo = jnp.maximum(pos - w // 2, 0)
        hi = jnp.minimum(pos + w - w // 2, seg_len)
        cnt = (hi - lo).astype(F32)
        pooled = acc / cnt - cur_ref[0, :, cols]
        y_ref[0, :, cols] = (_dot(pooled.astype(BF16), w_ref[g]) * scale_ref[:, cols]).astype(BF16)


def _pool_stage(u, w_pool, pool_scale, n_ctx):
    b, p, width = u.shape
    tm = TOKEN_TILE
    hb = tm // POOL_HALO
    n_halo_blocks = p // POOL_HALO
    return pl.pallas_call(
        functools.partial(_pool_kernel, n_ctx=n_ctx),
        grid=(b, p // tm),
        in_specs=[pl.BlockSpec((1, POOL_HALO, width), lambda bi, j: (bi, jnp.maximum(j * hb - 1, 0), 0)),
                  pl.BlockSpec((1, tm, width), lambda bi, j: (bi, j, 0)),
                  pl.BlockSpec((1, POOL_HALO, width),
                               lambda bi, j: (bi, jnp.minimum((j + 1) * hb, n_halo_blocks - 1), 0)),
                  pl.BlockSpec(w_pool.shape, lambda bi, j: (0, 0, 0)),
                  pl.BlockSpec(pool_scale.shape, lambda bi, j: (0, 0))],
        out_specs=pl.BlockSpec((1, tm, width), lambda bi, j: (bi, j, 0)),
        out_shape=jax.ShapeDtypeStruct((b, p, width), BF16),
        scratch_shapes=[pltpu.VMEM((tm + 2 * POOL_HALO, width), F32)],
        compiler_params=_cparams(("arbitrary", "arbitrary")),
        name="pool_stage",
    )(u, u, u, w_pool, pool_scale)


def _merge_kernel(yml_ref, ymc_ref, yp_ref, ys_ref, gate_ref, x_ref, mod_ref, g_ref, wbm_ref, wbp_ref, wbs_ref,
                  wout_ref, wsg_ref, wsu_ref, wsd_ref, wrt_ref, xs_ref, fp_ref, lg_ref, *, ctx_tile):
    d = SLAB_ROWS * 128
    mod = mod_ref[0, 0]
    gate = gate_ref[0]
    ym = jnp.where(pl.program_id(1) == ctx_tile, ymc_ref[0], yml_ref[0])
    merged = (gate[:, 0:d].astype(F32) * _dot(ym, wbm_ref[...])
              + gate[:, d:2 * d].astype(F32) * _dot(yp_ref[0], wbp_ref[...])
              + gate[:, 2 * d:3 * d].astype(F32) * _dot(ys_ref[0], wbs_ref[...]))
    x = _slab_rows_to_matrix(x_ref, gate.shape[0], SLAB_ROWS)
    x_mid = x + mod[2:3] * _dot(merged.astype(BF16), wout_ref[...])
    f = _rms(x_mid, g_ref[...]) * (1.0 + mod[4:5]) + mod[3:4]
    fb = f.astype(BF16)
    gsh = _dot(fb, wsg_ref[...])
    shared = _dot((gsh * _sigmoid(gsh) * _dot(fb, wsu_ref[...])).astype(BF16), wsd_ref[...])
    _matrix_to_slab_rows(xs_ref, x_mid + mod[5:6] * shared, SLAB_ROWS)
    _matrix_to_slab_rows(fp_ref, _pack_bf16_pair(f), PACKED_ROWS)
    lg_ref[...] = _dot_nt(wrt_ref[...], fb)


def _merge_stage(ym_lat, ym_ctx, yp, ys, gates, x_slab, modtab, g_ffn, wbm, wbp, wbs, wout, wsg, wsu, wsd, wrt,
                 keep_ctx):
    b, p, _ = yp.shape
    assert ym_ctx.shape[1] == TOKEN_TILE
    d = SLAB_ROWS * 128
    tm = TOKEN_TILE
    nj = p // tm
    nk = nj if keep_ctx else nj - 1
    tok = lambda w: pl.BlockSpec((1, tm, w), lambda bi, j: (bi, j, 0))
    slab_out = lambda rows: pl.BlockSpec((tm * rows, 128), lambda bi, j: (bi * nk + j, 0))
    full = lambda a: pl.BlockSpec(a.shape, lambda bi, j: (0,) * a.ndim)
    return pl.pallas_call(
        functools.partial(_merge_kernel, ctx_tile=nj - 1),
        grid=(b, nk),
        in_specs=[pl.BlockSpec((1, tm, ym_lat.shape[2]), lambda bi, j: (bi, jnp.minimum(j, nj - 2), 0)),
                  pl.BlockSpec((1, tm, ym_ctx.shape[2]), lambda bi, j: (bi, 0, 0)),
                  tok(yp.shape[2]), tok(ys.shape[2]), tok(gates.shape[2]),
                  pl.BlockSpec((tm * SLAB_ROWS, 128), lambda bi, j: (bi * nj + j, 0)),
                  pl.BlockSpec((1, 1, 8, d), lambda bi, j: (bi, j // (nj - 1), 0, 0)),
                  full(g_ffn), full(wbm), full(wbp), full(wbs), full(wout), full(wsg), full(wsu), full(wsd),
                  full(wrt)],
        out_specs=[slab_out(SLAB_ROWS), slab_out(PACKED_ROWS),
                   pl.BlockSpec((N_EXPERTS, tm), lambda bi, j: (0, bi * nk + j))],
        out_shape=[jax.ShapeDtypeStruct((b * nk * tm * SLAB_ROWS, 128), F32),
                   jax.ShapeDtypeStruct((b * nk * tm * PACKED_ROWS, 128), U32),
                   jax.ShapeDtypeStruct((N_EXPERTS, b * nk * tm), F32)],
        compiler_params=_cparams(("arbitrary", "arbitrary")),
        name="merge_stage",
    )(ym_lat, ym_ctx, yp, ys, gates, x_slab, modtab, g_ffn, wbm, wbp, wbs, wout, wsg, wsu, wsd, wrt)


def _route_kernel(lg_ref, bias_ref, dest_ref, wts_ref, seg_ref):
    tm = lg_ref.shape[1]
    ne = N_EXPERTS
    neg_inf = -jnp.inf
    scores = _sigmoid(lg_ref[...])
    sel = scores + bias_ref[...]
    iota_g = lax.broadcasted_iota(I32, (EXPERTS_PER_GROUP, tm), 0)
    gscore = []
    for g in range(N_GROUPS):
        sg = sel[g * EXPERTS_PER_GROUP:(g + 1) * EXPERTS_PER_GROUP]
        m1 = jnp.max(sg, axis=0, keepdims=True)
        i1 = jnp.min(jnp.where(sg == m1, iota_g, EXPERTS_PER_GROUP), axis=0, keepdims=True)
        m2 = jnp.max(jnp.where(iota_g == i1, neg_inf, sg), axis=0, keepdims=True)
        gscore.append(m1 + m2)
    rows = []
    for g in range(N_GROUPS):
        rank = jnp.zeros((1, tm), I32)
        for g2 in range(N_GROUPS):
            if g2 == g:
                continue
            beats = (gscore[g2] >= gscore[g]) if g2 < g else (gscore[g2] > gscore[g])
            rank = rank + beats.astype(I32)
        rows.append(jnp.where(rank < TOPK_GROUPS, sel[g * EXPERTS_PER_GROUP:(g + 1) * EXPERTS_PER_GROUP], neg_inf))
    cur = jnp.concatenate(rows, axis=0)
    iota_e = lax.broadcasted_iota(I32, (ne, tm), 0)
    picks = []
    member = jnp.zeros((ne, tm), F32)
    for _ in range(TOP_K):
        m = jnp.max(cur, axis=0, keepdims=True)
        idx = jnp.min(jnp.where(cur == m, iota_e, ne), axis=0, keepdims=True)
        hit = iota_e == idx
        picks.append(hit)
        member = member + hit.astype(F32)
        cur = jnp.where(hit, neg_inf, cur)
    earlier = (lax.broadcasted_iota(I32, (tm, tm), 0) < lax.broadcasted_iota(I32, (tm, tm), 1)).astype(BF16)
    pos = _dot(member.astype(BF16), earlier)
    cnt_col = jnp.sum(member, axis=1, keepdims=True)
    blocks_col = jnp.floor((cnt_col + (SEG_ALIGN - 1)) * (1.0 / SEG_ALIGN))
    lower = (lax.broadcasted_iota(I32, (ne, ne), 1) < lax.broadcasted_iota(I32, (ne, ne), 0)).astype(BF16)
    off_col = _dot(lower, jnp.broadcast_to(blocks_col, (ne, 128)).astype(BF16))[:, 0:1] * SEG_ALIGN
    base = off_col + pos
    w_rows = [jnp.sum(jnp.where(hit, scores, 0.0), axis=0, keepdims=True) for hit in picks]
    denom = w_rows[0]
    for w in w_rows[1:]:
        denom = denom + w
    for k, hit in enumerate(picks):
        dest_ref[0, k:k + 1, :] = jnp.sum(jnp.where(hit, base, 0.0), axis=0, keepdims=True).astype(I32)
        wts_ref[0, k:k + 1, :] = w_rows[k] / denom * ROUTED_SCALE
    member_pad = jnp.concatenate([member, jnp.zeros((128 - ne, tm), F32)], axis=0).astype(BF16)
    cnt_row = _dot_nt(jnp.ones((8, tm), BF16), member_pad)
    blocks_row = jnp.floor((cnt_row + (SEG_ALIGN - 1)) * (1.0 / SEG_ALIGN))
    before = (lax.broadcasted_iota(I32, (128, 128), 0) < lax.broadcasted_iota(I32, (128, 128), 1)).astype(BF16)
    off_row = _dot(blocks_row.astype(BF16), before) * SEG_ALIGN
    r = lax.broadcasted_iota(I32, (8, 128), 0)
    seg_ref[0] = jnp.where(r == 0, off_row, jnp.where(r == 1, cnt_row, 0.0)).astype(I32)


def _route_stage(lg_t, bias_col):
    ne, t_all = lg_t.shape
    tm = MOE_TILE
    nt = t_all // tm
    return pl.pallas_call(
        _route_kernel,
        grid=(nt,),
        in_specs=[pl.BlockSpec((ne, tm), lambda i: (0, i)),
                  pl.BlockSpec((ne, 1), lambda i: (0, 0))],
        out_specs=[pl.BlockSpec((1, TOP_K, tm), lambda i: (i, 0, 0)),
                   pl.BlockSpec((1, TOP_K, tm), lambda i: (i, 0, 0)),
                   pl.BlockSpec((1, 8, 128), lambda i: (i, 0, 0))],
        out_shape=[jax.ShapeDtypeStruct((nt, TOP_K, tm), I32),
                   jax.ShapeDtypeStruct((nt, TOP_K, tm), F32),
                   jax.ShapeDtypeStruct((nt, 8, 128), I32)],
        compiler_params=_cparams(("arbitrary",)),
        name="route_stage",
    )(lg_t, bias_col)


def _moe_row_stride(tm):
    cap = TOP_K * tm + N_EXPERTS * SEG_ALIGN + MOE_CHUNK
    blocks = cap // 8 + 1
    return 8 * (blocks + 1 - blocks % 2)


def _moe_kernel(dest_ref, wts_ref, seg_ref, fp_hbm, xs_hbm, gt_ref, wg_ref, wu_ref, wd_ref, out_hbm,
                bufs_ref, fp_stage, xs_stage, out_stage, fp_sem, xs_sem, out_sem,
                *, srow, sub, n_sub_per_batch, ctx_sub, n_tiles):
    pr = pl.program_id(0)
    eg = pl.program_id(1)
    tm = dest_ref.shape[2]
    ch = MOE_CHUNK
    group = wg_ref.shape[0]
    lane_blk = 128
    n_blk = tm // lane_blk
    per = bufs_ref.shape[0]
    n_slots = fp_stage.shape[0]
    n_here = jnp.minimum(per, n_tiles - pr * per)

    def table_block(ref, half, blk):
        return ref.at[half, :, pl.ds(pl.multiple_of(blk * lane_blk, lane_blk), lane_blk)]

    def token_rows(half, blk, per_tok):
        first = ((pr * per + half) * tm + blk * lane_blk) * per_tok
        return pl.ds(pl.multiple_of(first, lane_blk * per_tok), lane_blk * per_tok)

    def fp_copy(half, blk, slot):
        return pltpu.make_async_copy(fp_hbm.at[token_rows(half, blk, PACKED_ROWS), :], fp_stage.at[slot],
                                     fp_sem.at[slot])

    def xs_copy(half, blk, slot):
        return pltpu.make_async_copy(xs_hbm.at[token_rows(half, blk, SLAB_ROWS), :], xs_stage.at[slot],
                                     xs_sem.at[slot])

    def out_copy(half, blk, slot):
        return pltpu.make_async_copy(out_stage.at[slot], out_hbm.at[token_rows(half, blk, SLAB_ROWS), :],
                                     out_sem.at[slot])

    def group_rows(half):
        buf_ref = bufs_ref.at[half]
        for blk in range(n_slots - 1):
            fp_copy(half, blk, blk).start()
        buf_ref[...] = jnp.zeros(buf_ref.shape, U32)

        def body(blk, carry):
            slot = blk % n_slots
            fp_copy(half, blk, slot).wait()

            @pl.when(blk + n_slots - 1 < n_blk)
            def _prefetch():
                fp_copy(half, blk + n_slots - 1, (blk + n_slots - 1) % n_slots).start()

            dest = table_block(dest_ref, half, blk)
            rows = fp_stage.at[slot]
            for u in range(lane_blk):
                slab = rows[u * PACKED_ROWS:(u + 1) * PACKED_ROWS, :]
                for k in range(TOP_K):
                    buf_ref[pl.ds(dest[k, u], PACKED_ROWS, stride=srow), :] = slab
            return carry

        lax.fori_loop(0, n_blk, body, 0)

    def experts():
        def one_expert(ge, carry):
            e = eg * group + ge
            counts = [seg_ref[half, 1, e] for half in range(per)]
            offsets = [seg_ref[half, 0, e] for half in range(per)]

            def ffn_rows(halves, rows, chunk_idx):
                words, keep = [], []
                for h in halves:
                    r0 = offsets[h] + chunk_idx * rows
                    words.append([bufs_ref[h, pl.ds(pl.multiple_of(q * srow + r0, 8), rows), :] for q in range(4)])
                    keep.append(lax.broadcasted_iota(I32, (rows, 1), 0) < counts[h] - chunk_idx * rows)
                w = jnp.concatenate([jnp.concatenate(ws, axis=1) for ws in words], axis=0)
                xb = jnp.concatenate([_unpack_lo(w), _unpack_hi(w)], axis=1).astype(BF16)
                g = _dot(xb, wg_ref[ge])
                hmid = g * _sigmoid(g) * _dot(xb, wu_ref[ge])
                packed = _pack_bf16_pair(_dot(hmid.astype(BF16), wd_ref[ge]))
                for n, h in enumerate(halves):
                    r0 = offsets[h] + chunk_idx * rows
                    for q in range(4):
                        bufs_ref[h, pl.ds(pl.multiple_of(q * srow + r0, 8), rows), :] = jnp.where(
                            keep[n], packed[n * rows:(n + 1) * rows, q * 128:(q + 1) * 128], words[n][q])

            most = counts[0]
            for c in counts[1:]:
                most = jnp.maximum(most, c)
            lo = 0
            for rows in MOE_BODY_ROWS:
                pl.when((most > lo) & (most <= rows))(functools.partial(ffn_rows, tuple(range(per)), rows, 0))
                lo = rows

            @pl.when(most > ch)
            def _long_segments():
                for h in range(per):
                    def chunk(c, carry2, h=h):
                        ffn_rows((h,), ch, c)
                        return carry2

                    lax.fori_loop(0, (counts[h] + ch - 1) // ch, chunk, 0)

            return carry

        lax.fori_loop(0, group, one_expert, 0)

    def combine(half):
        buf_ref = bufs_ref.at[half]
        for blk in range(n_slots - 1):
            xs_copy(half, blk, blk).start()

        def body(blk, carry):
            slot = blk % n_slots
            xs_copy(half, blk, slot).wait()

            @pl.when(blk + n_slots - 1 < n_blk)
            def _prefetch():
                xs_copy(half, blk + n_slots - 1, (blk + n_slots - 1) % n_slots).start()

            @pl.when(blk >= n_slots)
            def _slot_free():
                out_copy(half, blk - n_slots, slot).wait()

            sub_blk = ((pr * per + half) * tm + blk * lane_blk) // sub
            bi = sub_blk // n_sub_per_batch
            gate = gt_ref[bi * 2 + (sub_blk - bi * n_sub_per_batch) // ctx_sub]
            dest = table_block(dest_ref, half, blk)
            wts = table_block(wts_ref, half, blk)
            xs = xs_stage.at[slot]
            out = out_stage.at[slot]
            for u in range(lane_blk):
                acc_lo = jnp.zeros((4, 128), F32)
                acc_hi = jnp.zeros((4, 128), F32)
                for k in range(TOP_K):
                    words = buf_ref[pl.ds(dest[k, u], PACKED_ROWS, stride=srow), :]
                    wk = wts[k, u]
                    acc_lo = acc_lo + wk * _unpack_lo(words)
                    acc_hi = acc_hi + wk * _unpack_hi(words)
                out[u * 8:u * 8 + 4, :] = xs[u * 8:u * 8 + 4, :] + gate[0:4] * acc_lo
                out[u * 8 + 4:u * 8 + 8, :] = xs[u * 8 + 4:u * 8 + 8, :] + gate[4:8] * acc_hi
            out_copy(half, blk, slot).start()
            return carry

        lax.fori_loop(0, n_blk, body, 0)
        for blk in range(n_blk - n_slots, n_blk):
            out_copy(half, blk, blk % n_slots).wait()

    def each_tile(fn):
        def body(half, carry):
            fn(half)
            return carry

        lax.fori_loop(0, n_here, body, 0)

    @pl.when(eg == 0)
    def _first_step():
        each_tile(group_rows)
        for half in range(1, per):
            @pl.when(half >= n_here)
            def _clear(half=half):
                bufs_ref[half] = jnp.zeros(bufs_ref.shape[1:], U32)

    experts()
    pl.when(eg == pl.num_programs(1) - 1)(functools.partial(each_tile, combine))


def _moe_stage(dest, wts, seg, fp4, xs8, gt2, wg, wu, wd, n_sub_per_batch, keep_ctx):
    ctx_sub = n_sub_per_batch - 1 if keep_ctx else n_sub_per_batch
    nt, _, tm = dest.shape
    ne = wg.shape[0]
    group = MOE_EXPERTS_PER_STEP
    per = MOE_TILES_PER_STEP
    assert ne % group == 0 and tm % 256 == 0 and TOKEN_TILE % 128 == 0 and per == 2
    n_rows = -(-nt // per)
    pad = ((0, n_rows * per - nt), (0, 0), (0, 0))
    dest, wts, seg = jnp.pad(dest, pad), jnp.pad(wts, pad), jnp.pad(seg, pad)
    srow = _moe_row_stride(tm)
    smem = lambda shape: pl.BlockSpec(shape, lambda i, e: (i, 0, 0), memory_space=pltpu.SMEM)
    hbm = pl.BlockSpec(memory_space=pl.ANY)
    slots = MOE_STAGE_SLOTS
    assert tm // 128 >= slots
    stage = lambda per_tok, dt: pltpu.VMEM((slots, 128 * per_tok, 128), dt)
    return pl.pallas_call(
        functools.partial(_moe_kernel, srow=srow, sub=TOKEN_TILE, n_sub_per_batch=n_sub_per_batch, ctx_sub=ctx_sub,
                          n_tiles=nt),
        grid=(n_rows, ne // group),
        in_specs=[smem((per, TOP_K, tm)), smem((per, TOP_K, tm)), smem((per, 8, 128)),
                  hbm, hbm,
                  pl.BlockSpec(gt2.shape, lambda i, e: (0, 0, 0)),
                  pl.BlockSpec((group,) + wg.shape[1:], lambda i, e: (e, 0, 0)),
                  pl.BlockSpec((group,) + wu.shape[1:], lambda i, e: (e, 0, 0)),
                  pl.BlockSpec((group,) + wd.shape[1:], lambda i, e: (e, 0, 0))],
        out_specs=hbm,
        out_shape=jax.ShapeDtypeStruct(xs8.shape, F32),
        scratch_shapes=[pltpu.VMEM((per, 4 * srow, 128), U32),
                        stage(PACKED_ROWS, U32), stage(SLAB_ROWS, F32), stage(SLAB_ROWS, F32),
                        pltpu.SemaphoreType.DMA((slots,)), pltpu.SemaphoreType.DMA((slots,)),
                        pltpu.SemaphoreType.DMA((slots,))],
        compiler_params=_cparams(("arbitrary", "arbitrary")),
        name="moe_stage",
    )(dest, wts, seg, fp4, xs8, gt2, wg, wu, wd)


def _final_kernel(x_ref, g_ref, o_ref):
    o_ref[0] = _rms(_slab_rows_to_matrix(x_ref, o_ref.shape[1], SLAB_ROWS), g_ref[...])


def _final_norm(x_slab, b, g_final, n_ctx):
    d = SLAB_ROWS * 128
    p = x_slab.shape[0] // (b * SLAB_ROWS)
    tm = TOKEN_TILE
    nj = p // tm
    return pl.pallas_call(
        _final_kernel,
        grid=(b, (p - n_ctx) // tm),
        in_specs=[pl.BlockSpec((tm * SLAB_ROWS, 128), lambda bi, j: (bi * nj + j, 0)),
                  pl.BlockSpec((1, d), lambda bi, j: (0, 0))],
        out_specs=pl.BlockSpec((1, tm, d), lambda bi, j: (bi, j, 0)),
        out_shape=jax.ShapeDtypeStruct((b, p - n_ctx, d), F32),
        compiler_params=_cparams(("arbitrary", "arbitrary")),
        name="final_norm",
    )(x_slab, g_final)


def _rot_cols(w, n_heads, rot_dim):
    kdim = w.shape[0]
    w4 = w.reshape(kdim, n_heads, 4, rot_dim // 4)
    rot = jnp.stack([-w4[:, :, 1], w4[:, :, 0], -w4[:, :, 3], w4[:, :, 2]], axis=2)
    return rot.reshape(kdim, n_heads * rot_dim)


def _rope_pattern(s_len, n_ctx, rot_dim):
    t = jnp.arange(s_len)
    row = (t // GRID_W).astype(F32)
    col = (t % GRID_W).astype(F32)
    n_freq = rot_dim // 4
    inv_freq = ROPE_BASE ** (-jnp.arange(n_freq, dtype=F32) / n_freq)
    ang_r = row[:, None] * inv_freq[None, :]
    ang_c = col[:, None] * inv_freq[None, :]
    cos = jnp.concatenate([jnp.cos(ang_r), jnp.cos(ang_r), jnp.cos(ang_c), jnp.cos(ang_c)], axis=1)
    sin = jnp.concatenate([jnp.sin(ang_r), jnp.sin(ang_r), jnp.sin(ang_c), jnp.sin(ang_c)], axis=1)
    cos = jnp.concatenate([cos, jnp.ones((n_ctx, rot_dim), F32)], axis=0)
    sin = jnp.concatenate([sin, jnp.zeros((n_ctx, rot_dim), F32)], axis=0)
    return cos, sin


def _layer_weights(w_in, w_uq, w_ukv):
    d = w_in.shape[0]
    offs = [0]
    for w in (MLA_Q_LORA, MLA_KV_LORA, MLA_ROPE, POOL_WIDTH, SWA_WIDTH, SWA_KV_WIDTH, SWA_KV_WIDTH, 3 * d):
        offs.append(offs[-1] + w)
    cq, ckv, kr, u, qs, ks, vs, gl = (w_in[:, offs[i]:offs[i + 1]] for i in range(8))
    qs = qs * (SWA_SCALE * LOG2_E)
    zeros = lambda n: jnp.zeros((d, n), F32)
    kr_slot = jnp.concatenate([zeros(MLA_NOPE), kr, zeros(MLA_HEAD_PAD - MLA_NOPE - MLA_ROPE)], axis=1)
    kr_rot_slot = jnp.concatenate([zeros(MLA_NOPE), _rot_cols(kr, 1, MLA_ROPE),
                                   zeros(MLA_HEAD_PAD - MLA_NOPE - MLA_ROPE)], axis=1)

    def kv_slots(w):
        w3 = w.reshape(d, SWA_KV_HEADS, SWA_HEAD_DIM)
        return jnp.concatenate([w3, jnp.zeros_like(w3)], axis=2).reshape(d, SWA_KV_HEADS * 128)

    w1 = jnp.concatenate([cq, ckv, u, qs, _rot_cols(qs, SWA_Q_HEADS, SWA_HEAD_DIM), kv_slots(ks),
                          kv_slots(_rot_cols(ks, SWA_KV_HEADS, SWA_HEAD_DIM)), kv_slots(vs), kr_slot, kr_rot_slot,
                          gl], axis=1).astype(BF16)
    lq = w_uq.shape[0]
    wq3 = (w_uq * (MLA_SCALE * LOG2_E)).reshape(lq, MLA_HEADS, MLA_NOPE + MLA_ROPE)
    nope, rope = wq3[:, :, :MLA_NOPE], wq3[:, :, MLA_NOPE:]
    rope_rot = _rot_cols(rope.reshape(lq, MLA_HEADS * MLA_ROPE), MLA_HEADS, MLA_ROPE).reshape(lq, MLA_HEADS, MLA_ROPE)
    pad = jnp.zeros((lq, MLA_HEADS, MLA_HEAD_PAD - MLA_NOPE - MLA_ROPE), F32)
    wq_a = jnp.concatenate([nope, rope, pad], axis=2).reshape(lq, MLA_HEADS * MLA_HEAD_PAD)
    wq_b = jnp.concatenate([jnp.zeros_like(nope), rope_rot, pad], axis=2).reshape(lq, MLA_HEADS * MLA_HEAD_PAD)
    wq = jnp.concatenate([wq_a, wq_b], axis=1).astype(BF16)
    lkv = w_ukv.shape[0]
    wkv3 = w_ukv.reshape(lkv, MLA_HEADS, MLA_NOPE + MLA_V)
    wkn = jnp.concatenate([wkv3[:, :, :MLA_NOPE], jnp.zeros((lkv, MLA_HEADS, MLA_HEAD_PAD - MLA_NOPE), F32)],
                          axis=2).reshape(lkv, MLA_HEADS * MLA_HEAD_PAD).astype(BF16)
    wv = jnp.concatenate([wkv3[:, :, MLA_NOPE:], jnp.zeros((lkv, MLA_HEADS, MLA_HEAD_PAD - MLA_V), F32)],
                         axis=2).reshape(lkv, MLA_HEADS * MLA_HEAD_PAD).astype(BF16)
    return w1, wq, wkn, wv


def kernel(x, c, ctx, c_ctx, w_mod, b_mod, g_mix, g_ffn, w_in, g_mla_q, g_mla_kv, w_mla_uq, w_mla_ukv, w_pool,
           pool_scale, swa_sink, w_br_mla, w_br_pool, w_br_swa, w_out, w_router, router_bias, w_exp_gate,
           w_exp_up, w_exp_down, w_sh_gate, w_sh_up, w_sh_down, g_final):
    b, s_len, d = x.shape
    n_ctx = ctx.shape[1]
    n_layers = w_mod.shape[0]
    p = n_ctx + s_len
    assert n_ctx == TOKEN_TILE and s_len % TOKEN_TILE == 0 and (b * p) % MOE_TILE == 0 and b + 1 <= 8
    assert d == 1024 and w_in.shape[2] == (MLA_Q_LORA + MLA_KV_LORA + MLA_ROPE + POOL_WIDTH + SWA_WIDTH
                                           + 2 * SWA_KV_WIDTH + 3 * d)

    cm, sm = _rope_pattern(s_len, n_ctx, MLA_ROPE)
    tail = jnp.zeros((p, MLA_HEAD_PAD - MLA_NOPE - MLA_ROPE), F32)
    tabm = jnp.concatenate([jnp.ones((p, MLA_NOPE), F32), cm, tail, jnp.zeros((p, MLA_NOPE), F32), sm, tail], axis=1)
    cs, ss = _rope_pattern(s_len, n_ctx, SWA_HEAD_DIM)
    tabs = jnp.concatenate([cs, cs, ss, ss], axis=1)

    cvec = jnp.concatenate([c, c_ctx[None, :], jnp.zeros((8 - b - 1, d), F32)], axis=0)
    mod_all = _modulation(cvec, w_mod, b_mod).reshape(n_layers, 8, N_MOD, d)

    x_slab = _assemble_stream(x, ctx)
    for i in range(n_layers):
        keep_ctx = i < n_layers - 1
        lat = mod_all[i, :b]
        ctx_rows = jnp.broadcast_to(mod_all[i, b][None], (b, N_MOD, d))
        modtab = jnp.pad(jnp.stack([lat, ctx_rows], axis=1), ((0, 0), (0, 0), (0, 8 - N_MOD), (0, 0)))
        w1, wq, wkn, wv = _layer_weights(w_in[i], w_mla_uq[i], w_mla_ukv[i])
        q, k, v, u, qs, ks, vs, gates = _input_stage(
            x_slab, b, modtab, g_mix[i][None], w1, g_mla_q[i][None], wq, g_mla_kv[i][None], wkn, wv, tabm, tabs)
        y_mla, y_mla_ctx = _mla_attention(q, k, v, n_ctx)
        y_swa = _swa_attention(swa_sink[i], qs, ks, vs, n_ctx)
        y_pool = _pool_stage(u, w_pool[i].astype(BF16), pool_scale[i][None], n_ctx)
        xs, fp, lg_t = _merge_stage(
            y_mla, y_mla_ctx, y_pool, y_swa, gates, x_slab, modtab, g_ffn[i][None], w_br_mla[i].astype(BF16),
            w_br_pool[i].astype(BF16), w_br_swa[i].astype(BF16), w_out[i].astype(BF16),
            w_sh_gate[i].astype(BF16), w_sh_up[i].astype(BF16), w_sh_down[i].astype(BF16),
            w_router[i].T.astype(BF16), keep_ctx)
        dest, wts, seg = _route_stage(lg_t, router_bias[i][:, None])
        gt2 = modtab[:, :, 5, :].reshape(b * 2, 8, d // 8)
        x_slab = _moe_stage(dest, wts, seg, fp, xs, gt2, w_exp_gate[i].astype(BF16), w_exp_up[i].astype(BF16),
                            w_exp_down[i].astype(BF16), (p if keep_ctx else s_len) // TOKEN_TILE, keep_ctx)
    return _final_norm(x_slab, b, g_final[None], 0)
```

```python
import functools

import jax
import jax.numpy as jnp
from jax import lax
from jax.experimental import pallas as pl
from jax.experimental.pallas import tpu as pltpu

F32 = jnp.float32
BF16 = jnp.bfloat16
U32 = jnp.uint32
I32 = jnp.int32

NORM_EPS = 1e-6
ROPE_BASE = 10000.0
GRID_W = 64
N_MOD = 6

MLA_HEADS = 8
MLA_Q_LORA = 384
MLA_KV_LORA = 256
MLA_NOPE = 64
MLA_ROPE = 32
MLA_V = 64
MLA_SCALE = (MLA_NOPE + MLA_ROPE) ** -0.5
MLA_HEAD_PAD = 128
LOG2_E = 1.4426950408889634

POOL_WINDOWS = (2, 4, 8, 16)
POOL_GROUP_DIM = 128
POOL_WIDTH = 512
POOL_HALO = 8

SWA_Q_HEADS = 8
SWA_KV_HEADS = 2
SWA_HEAD_DIM = 64
SWA_WINDOW = 128
SWA_BLOCK = 128
SWA_SCALE = SWA_HEAD_DIM ** -0.5
SWA_WIDTH = SWA_Q_HEADS * SWA_HEAD_DIM
SWA_KV_WIDTH = SWA_KV_HEADS * SWA_HEAD_DIM

N_EXPERTS = 64
TOP_K = 8
N_GROUPS = 8
TOPK_GROUPS = 4
EXPERTS_PER_GROUP = 8
D_EXPERT = 256
ROUTED_SCALE = 2.5

TOKEN_TILE = 256
SWA_Q_TILE = 256
MLA_Q_TILE = 1024
MLA_KEY_CHUNK = 256
MOE_TILE = 1024
MOE_CHUNK = 256
MOE_BODY_ROWS = (128, 160, 192, 224, 256)
MOE_EXPERTS_PER_STEP = 2
MOE_TILES_PER_STEP = 2
MOE_STAGE_SLOTS = 4
SEG_ALIGN = 8
SLAB_ROWS = 8
PACKED_ROWS = 4
MASK_VALUE = -1e30
HI16 = 0xFFFF0000

VMEM_LIMIT = 56 * 1024 * 1024

_SEG_WIDTHS = (("cq", 384), ("ckv", 256), ("u", 512), ("qs", 512), ("qs_rot", 512), ("ks", 256),
               ("ks_rot", 256), ("vs", 256), ("kr", 128), ("kr_rot", 128), ("gl", 3072))
_SEG = {}
_o = 0
for _n, _w in _SEG_WIDTHS:
    _SEG[_n] = (_o, _o + _w)
    _o += _w
FUSED_IN_WIDTH = _o


def _cparams(sem):
    return pltpu.CompilerParams(dimension_semantics=sem, vmem_limit_bytes=VMEM_LIMIT)


def _dot(a, b):
    return jnp.dot(a, b, preferred_element_type=F32)


def _dot_nt(a, b):
    return lax.dot_general(a, b, (((1,), (1,)), ((), ())), preferred_element_type=F32)


def _sigmoid(x):
    return 1.0 / (1.0 + jnp.exp(-x))


def _rms(x, g):
    return x * lax.rsqrt(jnp.mean(x * x, axis=-1, keepdims=True) + NORM_EPS) * g


def _pack_bf16_pair(v):
    n = v.shape[1] // 2
    bits = pltpu.bitcast(v.astype(BF16).astype(F32), U32)
    return (bits[:, :n] >> 16) | (bits[:, n:] & jnp.uint32(HI16))


def _slab_rows_to_matrix(ref, n_tok, per_tok):
    return jnp.concatenate([ref[pl.ds(c, n_tok, stride=per_tok), :] for c in range(per_tok)], axis=1)


def _matrix_to_slab_rows(ref, val, per_tok):
    n_tok = val.shape[0]
    for c in range(per_tok):
        ref[pl.ds(c, n_tok, stride=per_tok), :] = val[:, c * 128:(c + 1) * 128]


def _unpack_lo(w):
    return pltpu.bitcast(w << 16, F32)


def _unpack_hi(w):
    return pltpu.bitcast(w & jnp.uint32(HI16), F32)


def _mod_kernel(c_ref, w_ref, b_ref, o_ref):
    c = c_ref[...]
    a = (c * _sigmoid(c)).astype(BF16)
    o_ref[0] = _dot(a, w_ref[0].astype(BF16)) + b_ref[0]


def _modulation(cvec, w_mod, b_mod):
    n_layers, d, width = w_mod.shape
    tn = width // 4
    return pl.pallas_call(
        _mod_kernel,
        grid=(n_layers, width // tn),
        in_specs=[pl.BlockSpec((8, d), lambda l, n: (0, 0)),
                  pl.BlockSpec((1, d, tn), lambda l, n: (l, 0, n)),
                  pl.BlockSpec((1, 1, tn), lambda l, n: (l, 0, n))],
        out_specs=pl.BlockSpec((1, 8, tn), lambda l, n: (l, 0, n)),
        out_shape=jax.ShapeDtypeStruct((n_layers, 8, width), F32),
        compiler_params=_cparams(("arbitrary", "arbitrary")),
        name="modulation",
    )(cvec, w_mod, b_mod.reshape(n_layers, 1, width))


def _assemble_kernel(x_ref, ctx_ref, o_ref):
    val = jnp.where(pl.program_id(1) == pl.num_programs(1) - 1, ctx_ref[0], x_ref[0])
    _matrix_to_slab_rows(o_ref, val, SLAB_ROWS)


def _assemble_stream(x, ctx):
    b, s_len, d = x.shape
    tm = TOKEN_TILE
    assert ctx.shape[1] == tm and d == SLAB_ROWS * 128
    nj = s_len // tm + 1
    return pl.pallas_call(
        _assemble_kernel,
        grid=(b, nj),
        in_specs=[pl.BlockSpec((1, tm, d), lambda bi, j: (bi, jnp.minimum(j, nj - 2), 0)),
                  pl.BlockSpec((1, tm, d), lambda bi, j: (bi, 0, 0))],
        out_specs=pl.BlockSpec((tm * SLAB_ROWS, 128), lambda bi, j: (bi * nj + j, 0)),
        out_shape=jax.ShapeDtypeStruct((b * nj * tm * SLAB_ROWS, 128), F32),
        compiler_params=_cparams(("arbitrary", "arbitrary")),
        name="assemble_stream",
    )(x, ctx)


def _in_kernel(x_ref, mod_ref, g_ref, w1_ref, gq_ref, wq_ref, gkv_ref, wkn_ref, wv_ref, tabm_ref, tabs_ref,
               q_ref, k_ref, v_ref, u_ref, qs_ref, ks_ref, vs_ref, gate_ref):
    x = _slab_rows_to_matrix(x_ref, q_ref.shape[1], SLAB_ROWS)
    mod = mod_ref[0, 0]
    h = _rms(x, g_ref[...]) * (1.0 + mod[1:2]) + mod[0:1]
    hb = h.astype(BF16)

    def seg(name):
        a, b = _SEG[name]
        return _dot(hb, w1_ref[:, a:b])

    cos_m = tabm_ref[:, 0:128]
    sin_m = tabm_ref[:, 128:256]
    cos_s = tabs_ref[:, 0:128]
    sin_s = tabs_ref[:, 128:256]

    cqn = _rms(seg("cq"), gq_ref[...]).astype(BF16)
    qa = _dot(cqn, wq_ref[:, 0:1024])
    qb = _dot(cqn, wq_ref[:, 1024:2048])
    q_ref[0] = (qa * jnp.tile(cos_m, (1, MLA_HEADS)) + qb * jnp.tile(sin_m, (1, MLA_HEADS))).astype(BF16)

    ckvn = _rms(seg("ckv"), gkv_ref[...]).astype(BF16)
    kr = seg("kr") * cos_m + seg("kr_rot") * sin_m
    k_ref[0] = (_dot(ckvn, wkn_ref[...]) + jnp.tile(kr, (1, MLA_HEADS))).astype(BF16)
    ones_lane = (lax.broadcasted_iota(I32, (1, MLA_HEADS * MLA_HEAD_PAD), 1) % MLA_HEAD_PAD == MLA_V).astype(F32)
    v_ref[0] = (_dot(ckvn, wv_ref[...]) + ones_lane).astype(BF16)

    u_ref[0] = seg("u")

    qs_ref[0] = (seg("qs") * jnp.tile(cos_s, (1, 4)) + seg("qs_rot") * jnp.tile(sin_s, (1, 4))).astype(BF16)
    ks = seg("ks") * jnp.tile(cos_s, (1, 2)) + seg("ks_rot") * jnp.tile(sin_s, (1, 2))
    for hk in range(SWA_KV_HEADS):
        k_lo = ks[:, hk * 128:(hk + 1) * 128]
        ks_ref[0, :, (2 * hk) * 128:(2 * hk + 1) * 128] = k_lo.astype(BF16)
        ks_ref[0, :, (2 * hk + 1) * 128:(2 * hk + 2) * 128] = pltpu.roll(k_lo, SWA_HEAD_DIM, 1).astype(BF16)
    ones_s = (lax.broadcasted_iota(I32, (1, SWA_KV_HEADS * 128), 1) % 128 == SWA_HEAD_DIM).astype(F32)
    vs_ref[0] = (seg("vs") + ones_s).astype(BF16)

    g0, _ = _SEG["gl"]
    for p in range(6):
        gate_ref[0, :, p * 512:(p + 1) * 512] = _sigmoid(
            _dot(hb, w1_ref[:, g0 + p * 512:g0 + (p + 1) * 512])).astype(BF16)


def _input_stage(x_slab, b, modtab, g_mix, w1, g_q, wq, g_kv, wkn, wv, tabm, tabs):
    d = SLAB_ROWS * 128
    p = x_slab.shape[0] // (b * SLAB_ROWS)
    tm = TOKEN_TILE
    nj = p // tm
    tok = lambda w: pl.BlockSpec((1, tm, w), lambda bi, j: (bi, j, 0))
    full = lambda a: pl.BlockSpec(a.shape, lambda bi, j: (0,) * a.ndim)
    outs = [(1024, BF16), (1024, BF16), (1024, BF16), (512, F32), (512, BF16), (512, BF16), (256, BF16),
            (3072, BF16)]
    return pl.pallas_call(
        _in_kernel,
        grid=(b, nj),
        in_specs=[pl.BlockSpec((tm * SLAB_ROWS, 128), lambda bi, j: (bi * nj + j, 0)),
                  pl.BlockSpec((1, 1, 8, d), lambda bi, j: (bi, j // (nj - 1), 0, 0)),
                  full(g_mix), full(w1), full(g_q), full(wq), full(g_kv), full(wkn), full(wv),
                  pl.BlockSpec((tm, 256), lambda bi, j: (j, 0)),
                  pl.BlockSpec((tm, 256), lambda bi, j: (j, 0))],
        out_specs=[tok(w) for w, _ in outs],
        out_shape=[jax.ShapeDtypeStruct((b, p, w), dt) for w, dt in outs],
        compiler_params=_cparams(("arbitrary", "arbitrary")),
        name="input_stage",
    )(x_slab, modtab, g_mix, w1, g_q, wq, g_kv, wkn, wv, tabm, tabs)


def _mla_kernel(q_ref, k_ref, v_ref, *rest, tk, n_main, tail_rows):
    y_ref, m_ref, acc_ref = rest[-3:]
    hw = MLA_HEAD_PAD
    m_ref[...] = jnp.full(m_ref.shape, MASK_VALUE, F32)
    acc_ref[...] = jnp.zeros(acc_ref.shape, F32)

    def attend(r0, rows):
        for h in range(MLA_HEADS):
            q = q_ref[0, :, h * hw:(h + 1) * hw]
            k = k_ref[0, pl.ds(r0, rows), h * hw:(h + 1) * hw]
            v = v_ref[0, pl.ds(r0, rows), h * hw:(h + 1) * hw]
            s = _dot_nt(q, k)
            m_prev = m_ref[h]
            m_new = jnp.maximum(m_prev, jnp.max(s, axis=1, keepdims=True))
            m_ref[h] = m_new
            pr = jnp.exp2(s - jnp.tile(m_new, (1, rows // hw)))
            acc_ref[h] = jnp.exp2(m_prev - m_new) * acc_ref[h] + _dot(pr.astype(BF16), v)

    if n_main:
        def step(c, carry):
            attend(pl.multiple_of(c * tk, tk), tk)
            return carry

        lax.fori_loop(0, n_main, step, 0)
    if tail_rows:
        attend(n_main * tk, tail_rows)

    first = lax.broadcasted_iota(I32, (q_ref.shape[1], hw), 1) < MLA_V
    for hp in range(MLA_HEADS // 2):
        o = []
        for h in (2 * hp, 2 * hp + 1):
            acc = acc_ref[h]
            o.append(acc / acc[:, MLA_V:MLA_V + 1])
        y_ref[0, :, hp * hw:(hp + 1) * hw] = jnp.where(first, o[0], pltpu.roll(o[1], MLA_V, 1)).astype(BF16)


def _mla_attention(q, k, v, n_ctx):
    b, p, width = q.shape
    s_len = p - n_ctx
    tq = MLA_Q_TILE
    tk = MLA_KEY_CHUNK
    assert s_len % tq == 0 and s_len % tk == 0 and s_len % n_ctx == 0
    out_w = MLA_HEADS * MLA_V
    state = lambda rows: pltpu.VMEM((MLA_HEADS, rows, MLA_HEAD_PAD), F32)
    y_lat = pl.pallas_call(
        functools.partial(_mla_kernel, tk=tk, n_main=p // tk, tail_rows=p % tk),
        grid=(b, s_len // tq),
        in_specs=[pl.BlockSpec((1, tq, width), lambda bi, j: (bi, j, 0)),
                  pl.BlockSpec((1, p, width), lambda bi, j: (bi, 0, 0), pipeline_mode=pl.Buffered(1)),
                  pl.BlockSpec((1, p, width), lambda bi, j: (bi, 0, 0), pipeline_mode=pl.Buffered(1))],
        out_specs=pl.BlockSpec((1, tq, out_w), lambda bi, j: (bi, j, 0)),
        out_shape=jax.ShapeDtypeStruct((b, s_len, out_w), BF16),
        scratch_shapes=[state(tq), state(tq)],
        compiler_params=_cparams(("arbitrary", "arbitrary")),
        name="mla_attention",
    )(q, k, v)
    cblk = s_len // n_ctx
    ctx_rows = lambda w: pl.BlockSpec((1, n_ctx, w), lambda bi: (bi, cblk, 0))
    y_ctx = pl.pallas_call(
        functools.partial(_mla_kernel, tk=tk, n_main=0, tail_rows=n_ctx),
        grid=(b,),
        in_specs=[ctx_rows(width), ctx_rows(width), ctx_rows(width)],
        out_specs=pl.BlockSpec((1, n_ctx, out_w), lambda bi: (bi, 0, 0)),
        out_shape=jax.ShapeDtypeStruct((b, n_ctx, out_w), BF16),
        scratch_shapes=[state(n_ctx), state(n_ctx)],
        compiler_params=_cparams(("arbitrary",)),
        name="mla_attention_ctx",
    )(q, k, v)
    return y_lat, y_ctx


def _swa_kernel(sink_ref, q_ref, k_ref, v_ref, y_ref, *, n_ctx):
    p_len = k_ref.shape[1]
    s_len = p_len - n_ctx
    band = 3 * SWA_BLOCK
    n_lat_tiles = s_len // SWA_BLOCK
    first = lax.broadcasted_iota(I32, (SWA_BLOCK, 128), 1) < SWA_HEAD_DIM
    top = lax.broadcasted_iota(I32, (2 * SWA_BLOCK, 1), 0) < SWA_BLOCK
    n_blocks = q_ref.shape[1] // SWA_BLOCK
    chains = [(t, hk, par) for t in range(n_blocks) for hk in range(SWA_KV_HEADS) for par in range(2)]
    kstart, valid = {}, {}
    for t in range(n_blocks):
        j = pl.program_id(1) * n_blocks + t
        n = jnp.minimum(j, n_lat_tiles - 1)
        ws = jnp.clip((n - 1) * SWA_BLOCK, 0, s_len - band)
        kstart[t] = pl.multiple_of(ws, SWA_BLOCK)
        qpos = n * SWA_BLOCK + lax.broadcasted_iota(I32, (2 * SWA_BLOCK, band), 0) % SWA_BLOCK
        kpos = ws + lax.broadcasted_iota(I32, (2 * SWA_BLOCK, band), 1)
        valid[t] = (jnp.abs(qpos - kpos) <= SWA_WINDOW) & (j < n_lat_tiles)
    scores = {}
    for t, hk, par in chains:
        qrows = slice(t * SWA_BLOCK, (t + 1) * SWA_BLOCK)
        qpair = jnp.concatenate([q_ref[0, qrows, (2 * hk) * 128:(2 * hk + 1) * 128],
                                 q_ref[0, qrows, (2 * hk + 1) * 128:(2 * hk + 2) * 128]], axis=0)
        kcols = slice((2 * hk + par) * 128, (2 * hk + par + 1) * 128)
        s_c = _dot_nt(qpair, k_ref[0, s_len:p_len, kcols])
        s_b = jnp.where(valid[t], _dot_nt(qpair, k_ref[0, pl.ds(kstart[t], band), kcols]), MASK_VALUE)
        scores[t, hk, par] = (s_c, s_b)
    probs = {}
    for t, hk, par in chains:
        s_c, s_b = scores[t, hk, par]
        sink = jnp.where(top, sink_ref[4 * hk + par] * LOG2_E, sink_ref[4 * hk + 2 + par] * LOG2_E)
        m = jnp.maximum(jnp.maximum(jnp.max(s_c, axis=1, keepdims=True), jnp.max(s_b, axis=1, keepdims=True)), sink)
        probs[t, hk, par] = (jnp.exp2(s_c - m).astype(BF16), jnp.exp2(s_b - m).astype(BF16), jnp.exp2(sink - m))
    res = {}
    for t, hk, par in chains:
        p_c, p_b, p_sink = probs[t, hk, par]
        vcols = slice(hk * 128, (hk + 1) * 128)
        o = _dot(p_c, v_ref[0, s_len:p_len, vcols]) + _dot(p_b, v_ref[0, pl.ds(kstart[t], band), vcols])
        o = o / (o[:, SWA_HEAD_DIM:SWA_HEAD_DIM + 1] + p_sink)
        res[t, 4 * hk + par] = o[0:SWA_BLOCK]
        res[t, 4 * hk + 2 + par] = o[SWA_BLOCK:2 * SWA_BLOCK]
    for t in range(n_blocks):
        for pair in range(SWA_Q_HEADS // 2):
            y_ref[0, t * SWA_BLOCK:(t + 1) * SWA_BLOCK, pair * 128:(pair + 1) * 128] = jnp.where(
                first, res[t, 2 * pair], pltpu.roll(res[t, 2 * pair + 1], SWA_HEAD_DIM, 1)).astype(BF16)


def _swa_attention(sink, qs, ks, vs, n_ctx):
    b, p, _ = qs.shape
    tq = SWA_Q_TILE
    assert n_ctx % tq == 0 and (p - n_ctx) % tq == 0
    return pl.pallas_call(
        functools.partial(_swa_kernel, n_ctx=n_ctx),
        grid=(b, p // tq),
        in_specs=[pl.BlockSpec(memory_space=pltpu.SMEM),
                  pl.BlockSpec((1, tq, SWA_WIDTH), lambda bi, j: (bi, j, 0)),
                  pl.BlockSpec((1, p, ks.shape[2]), lambda bi, j: (bi, 0, 0)),
                  pl.BlockSpec((1, p, vs.shape[2]), lambda bi, j: (bi, 0, 0))],
        out_specs=pl.BlockSpec((1, tq, SWA_WIDTH), lambda bi, j: (bi, j, 0)),
        out_shape=jax.ShapeDtypeStruct((b, p, SWA_WIDTH), BF16),
        compiler_params=_cparams(("arbitrary", "arbitrary")),
        name="swa_attention",
    )(sink, qs, ks, vs)


def _pool_kernel(prev_ref, cur_ref, next_ref, w_ref, scale_ref, y_ref, ext_ref, *, n_ctx):
    j = pl.program_id(1)
    tm = cur_ref.shape[1]
    nj = pl.num_programs(1)
    s_len = (nj - 1) * tm
    is_ctx = j == nj - 1
    has_prev = (j >= 1) & (j < nj - 1)
    has_next = j < nj - 2
    ext_ref[0:POOL_HALO, :] = jnp.where(has_prev, prev_ref[0], 0.0)
    ext_ref[POOL_HALO:POOL_HALO + tm, :] = cur_ref[0]
    ext_ref[POOL_HALO + tm:POOL_HALO + tm + POOL_HALO, :] = jnp.where(has_next, next_ref[0], 0.0)
    t = lax.broadcasted_iota(I32, (tm, 1), 0)
    pos = jnp.where(is_ctx, t, j * tm + t)
    seg_len = jnp.where(is_ctx, n_ctx, s_len)
    for g, w in enumerate(POOL_WINDOWS):
        cols = slice(g * POOL_GROUP_DIM, (g + 1) * POOL_GROUP_DIM)
        acc = jnp.zeros((tm, POOL_GROUP_DIM), F32)
        for off in range(-(w // 2), w - w // 2):
            acc = acc + ext_ref[POOL_HALO + off:POOL_HALO + off + tm, cols]
        lo = jnp.maximum(pos - w // 2, 0)
        hi = jnp.minimum(pos + w - w // 2, seg_len)
        cnt = (hi - lo).astype(F32)
        pooled = acc / cnt - cur_ref[0, :, cols]
        y_ref[0, :, cols] = (_dot(pooled.astype(BF16), w_ref[g]) * scale_ref[:, cols]).astype(BF16)


def _pool_stage(u, w_pool, pool_scale, n_ctx):
    b, p, width = u.shape
    tm = TOKEN_TILE
    hb = tm // POOL_HALO
    n_halo_blocks = p // POOL_HALO
    return pl.pallas_call(
        functools.partial(_pool_kernel, n_ctx=n_ctx),
        grid=(b, p // tm),
        in_specs=[pl.BlockSpec((1, POOL_HALO, width), lambda bi, j: (bi, jnp.maximum(j * hb - 1, 0), 0)),
                  pl.BlockSpec((1, tm, width), lambda bi, j: (bi, j, 0)),
                  pl.BlockSpec((1, POOL_HALO, width),
                               lambda bi, j: (bi, jnp.minimum((j + 1) * hb, n_halo_blocks - 1), 0)),
                  pl.BlockSpec(w_pool.shape, lambda bi, j: (0, 0, 0)),
                  pl.BlockSpec(pool_scale.shape, lambda bi, j: (0, 0))],
        out_specs=pl.BlockSpec((1, tm, width), lambda bi, j: (bi, j, 0)),
        out_shape=jax.ShapeDtypeStruct((b, p, width), BF16),
        scratch_shapes=[pltpu.VMEM((tm + 2 * POOL_HALO, width), F32)],
        compiler_params=_cparams(("arbitrary", "arbitrary")),
        name="pool_stage",
    )(u, u, u, w_pool, pool_scale)


def _merge_kernel(yml_ref, ymc_ref, yp_ref, ys_ref, gate_ref, x_ref, mod_ref, g_ref, wbm_ref, wbp_ref, wbs_ref,
                  wout_ref, wsg_ref, wsu_ref, wsd_ref, wrt_ref, xs_ref, fp_ref, lg_ref, *, ctx_tile):
    d = SLAB_ROWS * 128
    mod = mod_ref[0, 0]
    gate = gate_ref[0]
    ym = jnp.where(pl.program_id(1) == ctx_tile, ymc_ref[0], yml_ref[0])
    merged = (gate[:, 0:d].astype(F32) * _dot(ym, wbm_ref[...])
              + gate[:, d:2 * d].astype(F32) * _dot(yp_ref[0], wbp_ref[...])
              + gate[:, 2 * d:3 * d].astype(F32) * _dot(ys_ref[0], wbs_ref[...]))
    x = _slab_rows_to_matrix(x_ref, gate.shape[0], SLAB_ROWS)
    x_mid = x + mod[2:3] * _dot(merged.astype(BF16), wout_ref[...])
    f = _rms(x_mid, g_ref[...]) * (1.0 + mod[4:5]) + mod[3:4]
    fb = f.astype(BF16)
    gsh = _dot(fb, wsg_ref[...])
    shared = _dot((gsh * _sigmoid(gsh) * _dot(fb, wsu_ref[...])).astype(BF16), wsd_ref[...])
    _matrix_to_slab_rows(xs_ref, x_mid + mod[5:6] * shared, SLAB_ROWS)
    _matrix_to_slab_rows(fp_ref, _pack_bf16_pair(f), PACKED_ROWS)
    lg_ref[...] = _dot_nt(wrt_ref[...], fb)


def _merge_stage(ym_lat, ym_ctx, yp, ys, gates, x_slab, modtab, g_ffn, wbm, wbp, wbs, wout, wsg, wsu, wsd, wrt,
                 keep_ctx):
    b, p, _ = yp.shape
    assert ym_ctx.shape[1] == TOKEN_TILE
    d = SLAB_ROWS * 128
    tm = TOKEN_TILE
    nj = p // tm
    nk = nj if keep_ctx else nj - 1
    tok = lambda w: pl.BlockSpec((1, tm, w), lambda bi, j: (bi, j, 0))
    slab_out = lambda rows: pl.BlockSpec((tm * rows, 128), lambda bi, j: (bi * nk + j, 0))
    full = lambda a: pl.BlockSpec(a.shape, lambda bi, j: (0,) * a.ndim)
    return pl.pallas_call(
        functools.partial(_merge_kernel, ctx_tile=nj - 1),
        grid=(b, nk),
        in_specs=[pl.BlockSpec((1, tm, ym_lat.shape[2]), lambda bi, j: (bi, jnp.minimum(j, nj - 2), 0)),
                  pl.BlockSpec((1, tm, ym_ctx.shape[2]), lambda bi, j: (bi, 0, 0)),
                  tok(yp.shape[2]), tok(ys.shape[2]), tok(gates.shape[2]),
                  pl.BlockSpec((tm * SLAB_ROWS, 128), lambda bi, j: (bi * nj + j, 0)),
                  pl.BlockSpec((1, 1, 8, d), lambda bi, j: (bi, j // (nj - 1), 0, 0)),
                  full(g_ffn), full(wbm), full(wbp), full(wbs), full(wout), full(wsg), full(wsu), full(wsd),
                  full(wrt)],
        out_specs=[slab_out(SLAB_ROWS), slab_out(PACKED_ROWS),
                   pl.BlockSpec((N_EXPERTS, tm), lambda bi, j: (0, bi * nk + j))],
        out_shape=[jax.ShapeDtypeStruct((b * nk * tm * SLAB_ROWS, 128), F32),
                   jax.ShapeDtypeStruct((b * nk * tm * PACKED_ROWS, 128), U32),
                   jax.ShapeDtypeStruct((N_EXPERTS, b * nk * tm), F32)],
        compiler_params=_cparams(("arbitrary", "arbitrary")),
        name="merge_stage",
    )(ym_lat, ym_ctx, yp, ys, gates, x_slab, modtab, g_ffn, wbm, wbp, wbs, wout, wsg, wsu, wsd, wrt)


def _route_kernel(lg_ref, bias_ref, dest_ref, wts_ref, seg_ref):
    tm = lg_ref.shape[1]
    ne = N_EXPERTS
    neg_inf = -jnp.inf
    scores = _sigmoid(lg_ref[...])
    sel = scores + bias_ref[...]
    iota_g = lax.broadcasted_iota(I32, (EXPERTS_PER_GROUP, tm), 0)
    gscore = []
    for g in range(N_GROUPS):
        sg = sel[g * EXPERTS_PER_GROUP:(g + 1) * EXPERTS_PER_GROUP]
        m1 = jnp.max(sg, axis=0, keepdims=True)
        i1 = jnp.min(jnp.where(sg == m1, iota_g, EXPERTS_PER_GROUP), axis=0, keepdims=True)
        m2 = jnp.max(jnp.where(iota_g == i1, neg_inf, sg), axis=0, keepdims=True)
        gscore.append(m1 + m2)
    rows = []
    for g in range(N_GROUPS):
        rank = jnp.zeros((1, tm), I32)
        for g2 in range(N_GROUPS):
            if g2 == g:
                continue
            beats = (gscore[g2] >= gscore[g]) if g2 < g else (gscore[g2] > gscore[g])
            rank = rank + beats.astype(I32)
        rows.append(jnp.where(rank < TOPK_GROUPS, sel[g * EXPERTS_PER_GROUP:(g + 1) * EXPERTS_PER_GROUP], neg_inf))
    cur = jnp.concatenate(rows, axis=0)
    iota_e = lax.broadcasted_iota(I32, (ne, tm), 0)
    picks = []
    member = jnp.zeros((ne, tm), F32)
    for _ in range(TOP_K):
        m = jnp.max(cur, axis=0, keepdims=True)
        idx = jnp.min(jnp.where(cur == m, iota_e, ne), axis=0, keepdims=True)
        hit = iota_e == idx
        picks.append(hit)
        member = member + hit.astype(F32)
        cur = jnp.where(hit, neg_inf, cur)
    earlier = (lax.broadcasted_iota(I32, (tm, tm), 0) < lax.broadcasted_iota(I32, (tm, tm), 1)).astype(BF16)
    pos = _dot(member.astype(BF16), earlier)
    cnt_col = jnp.sum(member, axis=1, keepdims=True)
    blocks_col = jnp.floor((cnt_col + (SEG_ALIGN - 1)) * (1.0 / SEG_ALIGN))
    lower = (lax.broadcasted_iota(I32, (ne, ne), 1) < lax.broadcasted_iota(I32, (ne, ne), 0)).astype(BF16)
    off_col = _dot(lower, jnp.broadcast_to(blocks_col, (ne, 128)).astype(BF16))[:, 0:1] * SEG_ALIGN
    base = off_col + pos
    w_rows = [jnp.sum(jnp.where(hit, scores, 0.0), axis=0, keepdims=True) for hit in picks]
    denom = w_rows[0]
    for w in w_rows[1:]:
        denom = denom + w
    for k, hit in enumerate(picks):
        dest_ref[0, k:k + 1, :] = jnp.sum(jnp.where(hit, base, 0.0), axis=0, keepdims=True).astype(I32)
        wts_ref[0, k:k + 1, :] = w_rows[k] / denom * ROUTED_SCALE
    member_pad = jnp.concatenate([member, jnp.zeros((128 - ne, tm), F32)], axis=0).astype(BF16)
    cnt_row = _dot_nt(jnp.ones((8, tm), BF16), member_pad)
    blocks_row = jnp.floor((cnt_row + (SEG_ALIGN - 1)) * (1.0 / SEG_ALIGN))
    before = (lax.broadcasted_iota(I32, (128, 128), 0) < lax.broadcasted_iota(I32, (128, 128), 1)).astype(BF16)
    off_row = _dot(blocks_row.astype(BF16), before) * SEG_ALIGN
    r = lax.broadcasted_iota(I32, (8, 128), 0)
    seg_ref[0] = jnp.where(r == 0, off_row, jnp.where(r == 1, cnt_row, 0.0)).astype(I32)


def _route_stage(lg_t, bias_col):
    ne, t_all = lg_t.shape
    tm = MOE_TILE
    nt = t_all // tm
    return pl.pallas_call(
        _route_kernel,
        grid=(nt,),
        in_specs=[pl.BlockSpec((ne, tm), lambda i: (0, i)),
                  pl.BlockSpec((ne, 1), lambda i: (0, 0))],
        out_specs=[pl.BlockSpec((1, TOP_K, tm), lambda i: (i, 0, 0)),
                   pl.BlockSpec((1, TOP_K, tm), lambda i: (i, 0, 0)),
                   pl.BlockSpec((1, 8, 128), lambda i: (i, 0, 0))],
        out_shape=[jax.ShapeDtypeStruct((nt, TOP_K, tm), I32),
                   jax.ShapeDtypeStruct((nt, TOP_K, tm), F32),
                   jax.ShapeDtypeStruct((nt, 8, 128), I32)],
        compiler_params=_cparams(("arbitrary",)),
        name="route_stage",
    )(lg_t, bias_col)


def _moe_row_stride(tm):
    cap = TOP_K * tm + N_EXPERTS * SEG_ALIGN + MOE_CHUNK
    blocks = cap // 8 + 1
    return 8 * (blocks + 1 - blocks % 2)


def _moe_kernel(dest_ref, wts_ref, seg_ref, fp_hbm, xs_hbm, gt_ref, gfin_ref, wg_ref, wu_ref, wd_ref, out_hbm,
                bufs_ref, fp_stage, xs_stage, out_stage, slab_tmp, fp_sem, xs_sem, out_sem,
                *, srow, sub, n_sub_per_batch, ctx_sub, n_tiles, final_norm):
    pr = pl.program_id(0)
    eg = pl.program_id(1)
    tm = dest_ref.shape[2]
    ch = MOE_CHUNK
    group = wg_ref.shape[0]
    lane_blk = 128
    n_blk = tm // lane_blk
    per = bufs_ref.shape[0]
    n_slots = fp_stage.shape[0]
    n_here = jnp.minimum(per, n_tiles - pr * per)

    def table_block(ref, half, blk):
        return ref.at[half, :, pl.ds(pl.multiple_of(blk * lane_blk, lane_blk), lane_blk)]

    def token_rows(half, blk, per_tok):
        first = ((pr * per + half) * tm + blk * lane_blk) * per_tok
        return pl.ds(pl.multiple_of(first, lane_blk * per_tok), lane_blk * per_tok)

    def fp_copy(half, blk, slot):
        return pltpu.make_async_copy(fp_hbm.at[token_rows(half, blk, PACKED_ROWS), :], fp_stage.at[slot],
                                     fp_sem.at[slot])

    def xs_copy(half, blk, slot):
        return pltpu.make_async_copy(xs_hbm.at[token_rows(half, blk, SLAB_ROWS), :], xs_stage.at[slot],
                                     xs_sem.at[slot])

    def out_copy(half, blk, slot):
        rows = token_rows(half, blk, 1 if final_norm else SLAB_ROWS)
        return pltpu.make_async_copy(out_stage.at[slot], out_hbm.at[rows, :], out_sem.at[slot])

    def group_rows(half):
        buf_ref = bufs_ref.at[half]
        for blk in range(n_slots - 1):
            fp_copy(half, blk, blk).start()
        buf_ref[...] = jnp.zeros(buf_ref.shape, U32)

        def body(blk, carry):
            slot = blk % n_slots
            fp_copy(half, blk, slot).wait()

            @pl.when(blk + n_slots - 1 < n_blk)
            def _prefetch():
                fp_copy(half, blk + n_slots - 1, (blk + n_slots - 1) % n_slots).start()

            dest = table_block(dest_ref, half, blk)
            rows = fp_stage.at[slot]
            for u in range(lane_blk):
                slab = rows[u * PACKED_ROWS:(u + 1) * PACKED_ROWS, :]
                for k in range(TOP_K):
                    buf_ref[pl.ds(dest[k, u], PACKED_ROWS, stride=srow), :] = slab
            return carry

        lax.fori_loop(0, n_blk, body, 0)

    def experts():
        def one_expert(ge, carry):
            e = eg * group + ge
            counts = [seg_ref[half, 1, e] for half in range(per)]
            offsets = [seg_ref[half, 0, e] for half in range(per)]

            def ffn_rows(halves, rows, chunk_idx):
                words, keep = [], []
                for h in halves:
                    r0 = offsets[h] + chunk_idx * rows
                    words.append([bufs_ref[h, pl.ds(pl.multiple_of(q * srow + r0, 8), rows), :] for q in range(4)])
                    keep.append(lax.broadcasted_iota(I32, (rows, 1), 0) < counts[h] - chunk_idx * rows)
                xb = []
                for ws in words:
                    w = jnp.concatenate(ws, axis=1)
                    xb.append(jnp.concatenate([_unpack_lo(w), _unpack_hi(w)], axis=1).astype(BF16))
                gates = [(_dot(x, wg_ref[ge]), _dot(x, wu_ref[ge])) for x in xb]
                hmid = [(g * _sigmoid(g) * u).astype(BF16) for g, u in gates]
                packed = [_pack_bf16_pair(_dot(hm, wd_ref[ge])) for hm in hmid]
                for n, h in enumerate(halves):
                    r0 = offsets[h] + chunk_idx * rows
                    for q in range(4):
                        bufs_ref[h, pl.ds(pl.multiple_of(q * srow + r0, 8), rows), :] = jnp.where(
                            keep[n], packed[n][:, q * 128:(q + 1) * 128], words[n][q])

            most = counts[0]
            for c in counts[1:]:
                most = jnp.maximum(most, c)
            lo = 0
            for rows in MOE_BODY_ROWS:
                pl.when((most > lo) & (most <= rows))(functools.partial(ffn_rows, tuple(range(per)), rows, 0))
                lo = rows

            @pl.when(most > ch)
            def _long_segments():
                for h in range(per):
                    def chunk(c, carry2, h=h):
                        ffn_rows((h,), ch, c)
                        return carry2

                    lax.fori_loop(0, (counts[h] + ch - 1) // ch, chunk, 0)

            return carry

        lax.fori_loop(0, group, one_expert, 0)

    def combine(half):
        buf_ref = bufs_ref.at[half]
        for blk in range(n_slots - 1):
            xs_copy(half, blk, blk).start()

        def body(blk, carry):
            slot = blk % n_slots
            xs_copy(half, blk, slot).wait()

            @pl.when(blk + n_slots - 1 < n_blk)
            def _prefetch():
                xs_copy(half, blk + n_slots - 1, (blk + n_slots - 1) % n_slots).start()

            @pl.when(blk >= n_slots)
            def _slot_free():
                out_copy(half, blk - n_slots, slot).wait()

            sub_blk = ((pr * per + half) * tm + blk * lane_blk) // sub
            bi = sub_blk // n_sub_per_batch
            gate = gt_ref[bi * 2 + (sub_blk - bi * n_sub_per_batch) // ctx_sub]
            dest = table_block(dest_ref, half, blk)
            wts = table_block(wts_ref, half, blk)
            xs = xs_stage.at[slot]
            out = slab_tmp if final_norm else out_stage.at[slot]
            for u in range(lane_blk):
                acc_lo = jnp.zeros((4, 128), F32)
                acc_hi = jnp.zeros((4, 128), F32)
                for k in range(TOP_K):
                    words = buf_ref[pl.ds(dest[k, u], PACKED_ROWS, stride=srow), :]
                    wk = wts[k, u]
                    acc_lo = acc_lo + wk * _unpack_lo(words)
                    acc_hi = acc_hi + wk * _unpack_hi(words)
                out[u * 8:u * 8 + 4, :] = xs[u * 8:u * 8 + 4, :] + gate[0:4] * acc_lo
                out[u * 8 + 4:u * 8 + 8, :] = xs[u * 8 + 4:u * 8 + 8, :] + gate[4:8] * acc_hi
            if final_norm:
                out_stage[slot] = _rms(_slab_rows_to_matrix(slab_tmp, lane_blk, SLAB_ROWS), gfin_ref[...])
            out_copy(half, blk, slot).start()
            return carry

        lax.fori_loop(0, n_blk, body, 0)
        for blk in range(n_blk - n_slots, n_blk):
            out_copy(half, blk, blk % n_slots).wait()

    def each_tile(fn):
        def body(half, carry):
            fn(half)
            return carry

        lax.fori_loop(0, n_here, body, 0)

    @pl.when(eg == 0)
    def _first_step():
        each_tile(group_rows)
        for half in range(1, per):
            @pl.when(half >= n_here)
            def _clear(half=half):
                bufs_ref[half] = jnp.zeros(bufs_ref.shape[1:], U32)

    experts()
    pl.when(eg == pl.num_programs(1) - 1)(functools.partial(each_tile, combine))


def _moe_stage(dest, wts, seg, fp4, xs8, gt2, g_final, wg, wu, wd, n_sub_per_batch, keep_ctx, final_norm):
    ctx_sub = n_sub_per_batch - 1 if keep_ctx else n_sub_per_batch
    d = SLAB_ROWS * 128
    out_shape = (xs8.shape[0] // SLAB_ROWS, d) if final_norm else xs8.shape
    out_block = (128, d) if final_norm else (128 * SLAB_ROWS, 128)
    nt, _, tm = dest.shape
    ne = wg.shape[0]
    group = MOE_EXPERTS_PER_STEP
    per = MOE_TILES_PER_STEP
    assert ne % group == 0 and tm % 256 == 0 and TOKEN_TILE % 128 == 0 and per == 2
    n_rows = -(-nt // per)
    pad = ((0, n_rows * per - nt), (0, 0), (0, 0))
    dest, wts, seg = jnp.pad(dest, pad), jnp.pad(wts, pad), jnp.pad(seg, pad)
    srow = _moe_row_stride(tm)
    smem = lambda shape: pl.BlockSpec(shape, lambda i, e: (i, 0, 0), memory_space=pltpu.SMEM)
    hbm = pl.BlockSpec(memory_space=pl.ANY)
    slots = MOE_STAGE_SLOTS
    assert tm // 128 >= slots
    stage = lambda per_tok, dt: pltpu.VMEM((slots, 128 * per_tok, 128), dt)
    return pl.pallas_call(
        functools.partial(_moe_kernel, srow=srow, sub=TOKEN_TILE, n_sub_per_batch=n_sub_per_batch, ctx_sub=ctx_sub,
                          n_tiles=nt, final_norm=final_norm),
        grid=(n_rows, ne // group),
        in_specs=[smem((per, TOP_K, tm)), smem((per, TOP_K, tm)), smem((per, 8, 128)),
                  hbm, hbm,
                  pl.BlockSpec(gt2.shape, lambda i, e: (0, 0, 0)),
                  pl.BlockSpec(g_final.shape, lambda i, e: (0, 0)),
                  pl.BlockSpec((group,) + wg.shape[1:], lambda i, e: (e, 0, 0)),
                  pl.BlockSpec((group,) + wu.shape[1:], lambda i, e: (e, 0, 0)),
                  pl.BlockSpec((group,) + wd.shape[1:], lambda i, e: (e, 0, 0))],
        out_specs=hbm,
        out_shape=jax.ShapeDtypeStruct(out_shape, F32),
        scratch_shapes=[pltpu.VMEM((per, 4 * srow, 128), U32),
                        stage(PACKED_ROWS, U32), stage(SLAB_ROWS, F32), pltpu.VMEM((slots,) + out_block, F32),
                        pltpu.VMEM((128 * SLAB_ROWS, 128), F32),
                        pltpu.SemaphoreType.DMA((slots,)), pltpu.SemaphoreType.DMA((slots,)),
                        pltpu.SemaphoreType.DMA((slots,))],
        compiler_params=_cparams(("arbitrary", "arbitrary")),
        name="moe_stage",
    )(dest, wts, seg, fp4, xs8, gt2, g_final, wg, wu, wd)


def _rot_lanes(x, head_dim):
    quarter = head_dim // 4
    lane = lax.broadcasted_iota(I32, x.shape, 1)
    width = x.shape[1]
    return jnp.where(lane % (2 * quarter) < quarter, -pltpu.roll(x, width - quarter, 1), pltpu.roll(x, quarter, 1))


def _w1_kernel(w_ref, o_ref):
    offs = [0]
    for w in (MLA_Q_LORA, MLA_KV_LORA, MLA_ROPE, POOL_WIDTH, SWA_WIDTH, SWA_KV_WIDTH, SWA_KV_WIDTH):
        offs.append(offs[-1] + w)
    o_cq, o_ckv, o_kr, o_u, o_qs, o_ks, o_vs, o_gl = offs
    lane = lax.broadcasted_iota(I32, (w_ref.shape[0], 128), 1)

    def put(name, val):
        a, b = _SEG[name]
        o_ref[:, a:b] = val.astype(BF16)

    def kv_slots(x):
        low = lane < SWA_HEAD_DIM
        return jnp.concatenate([jnp.where(low, x, 0.0), jnp.where(low, pltpu.roll(x, SWA_HEAD_DIM, 1), 0.0)], axis=1)

    def kr_slot(x):
        return jnp.where((lane >= MLA_NOPE) & (lane < MLA_NOPE + MLA_ROPE), pltpu.roll(x, MLA_NOPE, 1), 0.0)

    put("cq", w_ref[:, o_cq:o_ckv])
    put("ckv", w_ref[:, o_ckv:o_kr])
    put("u", w_ref[:, o_u:o_qs])
    qs = w_ref[:, o_qs:o_ks] * (SWA_SCALE * LOG2_E)
    put("qs", qs)
    put("qs_rot", _rot_lanes(qs, SWA_HEAD_DIM))
    ks = w_ref[:, o_ks:o_vs]
    put("ks", kv_slots(ks))
    put("ks_rot", kv_slots(_rot_lanes(ks, SWA_HEAD_DIM)))
    put("vs", kv_slots(w_ref[:, o_vs:o_gl]))
    kr_tile = w_ref[:, o_kr:o_kr + 128]
    put("kr", kr_slot(kr_tile))
    put("kr_rot", kr_slot(_rot_lanes(kr_tile, MLA_ROPE)))
    put("gl", w_ref[:, o_gl:w_ref.shape[1]])


def _fused_in_weight(w_in):
    d, width = w_in.shape
    tr = 256
    assert d % tr == 0 and SWA_KV_WIDTH == 128 and MLA_KV_LORA + MLA_Q_LORA == 5 * 128
    return pl.pallas_call(
        _w1_kernel,
        grid=(d // tr,),
        in_specs=[pl.BlockSpec((tr, width), lambda i: (i, 0))],
        out_specs=pl.BlockSpec((tr, FUSED_IN_WIDTH), lambda i: (i, 0)),
        out_shape=jax.ShapeDtypeStruct((d, FUSED_IN_WIDTH), BF16),
        compiler_params=_cparams(("arbitrary",)),
        name="fused_in_weight",
    )(w_in)


def _rot_cols(w, n_heads, rot_dim):
    kdim = w.shape[0]
    w4 = w.reshape(kdim, n_heads, 4, rot_dim // 4)
    rot = jnp.stack([-w4[:, :, 1], w4[:, :, 0], -w4[:, :, 3], w4[:, :, 2]], axis=2)
    return rot.reshape(kdim, n_heads * rot_dim)


def _rope_pattern(s_len, n_ctx, rot_dim):
    t = jnp.arange(s_len)
    row = (t // GRID_W).astype(F32)
    col = (t % GRID_W).astype(F32)
    n_freq = rot_dim // 4
    inv_freq = ROPE_BASE ** (-jnp.arange(n_freq, dtype=F32) / n_freq)
    ang_r = row[:, None] * inv_freq[None, :]
    ang_c = col[:, None] * inv_freq[None, :]
    cos = jnp.concatenate([jnp.cos(ang_r), jnp.cos(ang_r), jnp.cos(ang_c), jnp.cos(ang_c)], axis=1)
    sin = jnp.concatenate([jnp.sin(ang_r), jnp.sin(ang_r), jnp.sin(ang_c), jnp.sin(ang_c)], axis=1)
    cos = jnp.concatenate([cos, jnp.ones((n_ctx, rot_dim), F32)], axis=0)
    sin = jnp.concatenate([sin, jnp.zeros((n_ctx, rot_dim), F32)], axis=0)
    return cos, sin


def _layer_weights(w_in, w_uq, w_ukv):
    w1 = _fused_in_weight(w_in)
    lq = w_uq.shape[0]
    wq3 = (w_uq * (MLA_SCALE * LOG2_E)).reshape(lq, MLA_HEADS, MLA_NOPE + MLA_ROPE)
    nope, rope = wq3[:, :, :MLA_NOPE], wq3[:, :, MLA_NOPE:]
    rope_rot = _rot_cols(rope.reshape(lq, MLA_HEADS * MLA_ROPE), MLA_HEADS, MLA_ROPE).reshape(lq, MLA_HEADS, MLA_ROPE)
    pad = jnp.zeros((lq, MLA_HEADS, MLA_HEAD_PAD - MLA_NOPE - MLA_ROPE), F32)
    wq_a = jnp.concatenate([nope, rope, pad], axis=2).reshape(lq, MLA_HEADS * MLA_HEAD_PAD)
    wq_b = jnp.concatenate([jnp.zeros_like(nope), rope_rot, pad], axis=2).reshape(lq, MLA_HEADS * MLA_HEAD_PAD)
    wq = jnp.concatenate([wq_a, wq_b], axis=1).astype(BF16)
    lkv = w_ukv.shape[0]
    wkv3 = w_ukv.reshape(lkv, MLA_HEADS, MLA_NOPE + MLA_V)
    wkn = jnp.concatenate([wkv3[:, :, :MLA_NOPE], jnp.zeros((lkv, MLA_HEADS, MLA_HEAD_PAD - MLA_NOPE), F32)],
                          axis=2).reshape(lkv, MLA_HEADS * MLA_HEAD_PAD).astype(BF16)
    wv = jnp.concatenate([wkv3[:, :, MLA_NOPE:], jnp.zeros((lkv, MLA_HEADS, MLA_HEAD_PAD - MLA_V), F32)],
                         axis=2).reshape(lkv, MLA_HEADS * MLA_HEAD_PAD).astype(BF16)
    return w1, wq, wkn, wv


def kernel(x, c, ctx, c_ctx, w_mod, b_mod, g_mix, g_ffn, w_in, g_mla_q, g_mla_kv, w_mla_uq, w_mla_ukv, w_pool,
           pool_scale, swa_sink, w_br_mla, w_br_pool, w_br_swa, w_out, w_router, router_bias, w_exp_gate,
           w_exp_up, w_exp_down, w_sh_gate, w_sh_up, w_sh_down, g_final):
    b, s_len, d = x.shape
    n_ctx = ctx.shape[1]
    n_layers = w_mod.shape[0]
    p = n_ctx + s_len
    assert n_ctx == TOKEN_TILE and s_len % TOKEN_TILE == 0 and (b * p) % MOE_TILE == 0 and b + 1 <= 8
    assert d == 1024 and w_in.shape[2] == (MLA_Q_LORA + MLA_KV_LORA + MLA_ROPE + POOL_WIDTH + SWA_WIDTH
                                           + 2 * SWA_KV_WIDTH + 3 * d)

    cm, sm = _rope_pattern(s_len, n_ctx, MLA_ROPE)
    tail = jnp.zeros((p, MLA_HEAD_PAD - MLA_NOPE - MLA_ROPE), F32)
    tabm = jnp.concatenate([jnp.ones((p, MLA_NOPE), F32), cm, tail, jnp.zeros((p, MLA_NOPE), F32), sm, tail], axis=1)
    cs, ss = _rope_pattern(s_len, n_ctx, SWA_HEAD_DIM)
    tabs = jnp.concatenate([cs, cs, ss, ss], axis=1)

    cvec = jnp.concatenate([c, c_ctx[None, :], jnp.zeros((8 - b - 1, d), F32)], axis=0)
    mod_all = _modulation(cvec, w_mod, b_mod).reshape(n_layers, 8, N_MOD, d)

    x_slab = _assemble_stream(x, ctx)
    for i in range(n_layers):
        keep_ctx = i < n_layers - 1
        lat = mod_all[i, :b]
        ctx_rows = jnp.broadcast_to(mod_all[i, b][None], (b, N_MOD, d))
        modtab = jnp.pad(jnp.stack([lat, ctx_rows], axis=1), ((0, 0), (0, 0), (0, 8 - N_MOD), (0, 0)))
        w1, wq, wkn, wv = _layer_weights(w_in[i], w_mla_uq[i], w_mla_ukv[i])
        q, k, v, u, qs, ks, vs, gates = _input_stage(
            x_slab, b, modtab, g_mix[i][None], w1, g_mla_q[i][None], wq, g_mla_kv[i][None], wkn, wv, tabm, tabs)
        y_mla, y_mla_ctx = _mla_attention(q, k, v, n_ctx)
        y_swa = _swa_attention(swa_sink[i], qs, ks, vs, n_ctx)
        y_pool = _pool_stage(u, w_pool[i].astype(BF16), pool_scale[i][None], n_ctx)
        xs, fp, lg_t = _merge_stage(
            y_mla, y_mla_ctx, y_pool, y_swa, gates, x_slab, modtab, g_ffn[i][None], w_br_mla[i].astype(BF16),
            w_br_pool[i].astype(BF16), w_br_swa[i].astype(BF16), w_out[i].astype(BF16),
            w_sh_gate[i].astype(BF16), w_sh_up[i].astype(BF16), w_sh_down[i].astype(BF16),
            w_router[i].T.astype(BF16), keep_ctx)
        dest, wts, seg = _route_stage(lg_t, router_bias[i][:, None])
        gt2 = modtab[:, :, 5, :].reshape(b * 2, 8, d // 8)
        x_slab = _moe_stage(dest, wts, seg, fp, xs, gt2, g_final[None], w_exp_gate[i].astype(BF16),
                            w_exp_up[i].astype(BF16), w_exp_down[i].astype(BF16),
                            (p if keep_ctx else s_len) // TOKEN_TILE, keep_ctx, final_norm=not keep_ctx)
    return x_slab.reshape(b, s_len, d)
```

```python
import functools

import jax
import jax.numpy as jnp
from jax import lax
from jax.experimental import pallas as pl
from jax.experimental.pallas import tpu as pltpu

F32 = jnp.float32
BF16 = jnp.bfloat16
U32 = jnp.uint32
I32 = jnp.int32

NORM_EPS = 1e-6
ROPE_BASE = 10000.0
GRID_W = 64
N_MOD = 6

MLA_HEADS = 8
MLA_Q_LORA = 384
MLA_KV_LORA = 256
MLA_NOPE = 64
MLA_ROPE = 32
MLA_V = 64
MLA_SCALE = (MLA_NOPE + MLA_ROPE) ** -0.5
MLA_HEAD_PAD = 128
LOG2_E = 1.4426950408889634

POOL_WINDOWS = (2, 4, 8, 16)
POOL_GROUP_DIM = 128
POOL_WIDTH = 512
POOL_HALO = 8

SWA_Q_HEADS = 8
SWA_KV_HEADS = 2
SWA_HEAD_DIM = 64
SWA_WINDOW = 128
SWA_BLOCK = 128
SWA_SCALE = SWA_HEAD_DIM ** -0.5
SWA_WIDTH = SWA_Q_HEADS * SWA_HEAD_DIM
SWA_KV_WIDTH = SWA_KV_HEADS * SWA_HEAD_DIM

N_EXPERTS = 64
TOP_K = 8
N_GROUPS = 8
TOPK_GROUPS = 4
EXPERTS_PER_GROUP = 8
D_EXPERT = 256
ROUTED_SCALE = 2.5

TOKEN_TILE = 256
SWA_Q_TILE = 256
MLA_Q_TILE = 1024
MLA_KEY_CHUNK = 256
MOE_TILE = 1024
MOE_CHUNK = 256
MOE_BODY_ROWS = (128, 160, 192, 224, 256)
MOE_EXPERTS_PER_STEP = 2
MOE_TILES_PER_STEP = 2
MOE_STAGE_SLOTS = 4
SEG_ALIGN = 8
SLAB_ROWS = 8
PACKED_ROWS = 4
MASK_VALUE = -1e30
HI16 = 0xFFFF0000

VMEM_LIMIT = 56 * 1024 * 1024

_SEG_WIDTHS = (("cq", 384), ("ckv", 256), ("u", 512), ("qs", 512), ("ks", 256), ("vs", 256), ("kr", 128),
               ("gl", 3072))
_SEG = {}
_o = 0
for _n, _w in _SEG_WIDTHS:
    _SEG[_n] = (_o, _o + _w)
    _o += _w
FUSED_IN_WIDTH = _o


def _cparams(sem):
    return pltpu.CompilerParams(dimension_semantics=sem, vmem_limit_bytes=VMEM_LIMIT)


def _dot(a, b):
    return jnp.dot(a, b, preferred_element_type=F32)


def _dot_nt(a, b):
    return lax.dot_general(a, b, (((1,), (1,)), ((), ())), preferred_element_type=F32)


def _sigmoid(x):
    return 1.0 / (1.0 + jnp.exp(-x))


def _rms(x, g):
    return x * lax.rsqrt(jnp.mean(x * x, axis=-1, keepdims=True) + NORM_EPS) * g


def _pack_bf16_pair(v):
    n = v.shape[1] // 2
    bits = pltpu.bitcast(v.astype(BF16).astype(F32), U32)
    return (bits[:, :n] >> 16) | (bits[:, n:] & jnp.uint32(HI16))


def _slab_rows_to_matrix(ref, n_tok, per_tok):
    return jnp.concatenate([ref[pl.ds(c, n_tok, stride=per_tok), :] for c in range(per_tok)], axis=1)


def _matrix_to_slab_rows(ref, val, per_tok):
    n_tok = val.shape[0]
    for c in range(per_tok):
        ref[pl.ds(c, n_tok, stride=per_tok), :] = val[:, c * 128:(c + 1) * 128]


def _unpack_lo(w):
    return pltpu.bitcast(w << 16, F32)


def _unpack_hi(w):
    return pltpu.bitcast(w & jnp.uint32(HI16), F32)


def _mod_kernel(c_ref, w_ref, b_ref, o_ref):
    c = c_ref[...]
    a = (c * _sigmoid(c)).astype(BF16)
    o_ref[0] = _dot(a, w_ref[0].astype(BF16)) + b_ref[0]


def _modulation(cvec, w_mod, b_mod):
    n_layers, d, width = w_mod.shape
    tn = width // 4
    return pl.pallas_call(
        _mod_kernel,
        grid=(n_layers, width // tn),
        in_specs=[pl.BlockSpec((8, d), lambda l, n: (0, 0)),
                  pl.BlockSpec((1, d, tn), lambda l, n: (l, 0, n)),
                  pl.BlockSpec((1, 1, tn), lambda l, n: (l, 0, n))],
        out_specs=pl.BlockSpec((1, 8, tn), lambda l, n: (l, 0, n)),
        out_shape=jax.ShapeDtypeStruct((n_layers, 8, width), F32),
        compiler_params=_cparams(("arbitrary", "arbitrary")),
        name="modulation",
    )(cvec, w_mod, b_mod.reshape(n_layers, 1, width))


def _assemble_kernel(x_ref, ctx_ref, o_ref):
    val = jnp.where(pl.program_id(1) == pl.num_programs(1) - 1, ctx_ref[0], x_ref[0])
    _matrix_to_slab_rows(o_ref, val, SLAB_ROWS)


def _assemble_stream(x, ctx):
    b, s_len, d = x.shape
    tm = TOKEN_TILE
    assert ctx.shape[1] == tm and d == SLAB_ROWS * 128
    nj = s_len // tm + 1
    return pl.pallas_call(
        _assemble_kernel,
        grid=(b, nj),
        in_specs=[pl.BlockSpec((1, tm, d), lambda bi, j: (bi, jnp.minimum(j, nj - 2), 0)),
                  pl.BlockSpec((1, tm, d), lambda bi, j: (bi, 0, 0))],
        out_specs=pl.BlockSpec((tm * SLAB_ROWS, 128), lambda bi, j: (bi * nj + j, 0)),
        out_shape=jax.ShapeDtypeStruct((b * nj * tm * SLAB_ROWS, 128), F32),
        compiler_params=_cparams(("arbitrary", "arbitrary")),
        name="assemble_stream",
    )(x, ctx)


def _in_kernel(x_ref, mod_ref, g_ref, w1_ref, gq_ref, wq_ref, gkv_ref, wkn_ref, wv_ref, tabm_ref, tabs_ref,
               q_ref, k_ref, v_ref, u_ref, qs_ref, ks_ref, vs_ref, gate_ref):
    x = _slab_rows_to_matrix(x_ref, q_ref.shape[1], SLAB_ROWS)
    mod = mod_ref[0, 0]
    h = _rms(x, g_ref[...]) * (1.0 + mod[1:2]) + mod[0:1]
    hb = h.astype(BF16)

    def seg(name):
        a, b = _SEG[name]
        return _dot(hb, w1_ref[:, a:b])

    cos_m = tabm_ref[:, 0:128]
    sin_m = tabm_ref[:, 128:256]
    cos_s = tabs_ref[:, 0:128]
    sin_s = tabs_ref[:, 128:256]

    def rotary(val, cos, sin, head_dim):
        tiles = [val[:, c:c + 128] for c in range(0, val.shape[1], 128)]
        return jnp.concatenate([t * cos + _rot_lanes(t, head_dim) * sin for t in tiles], axis=1)

    cqn = _rms(seg("cq"), gq_ref[...]).astype(BF16)
    q_ref[0] = rotary(_dot(cqn, wq_ref[...]), cos_m, sin_m, MLA_ROPE).astype(BF16)

    ckvn = _rms(seg("ckv"), gkv_ref[...]).astype(BF16)
    kr = rotary(seg("kr"), cos_m, sin_m, MLA_ROPE)
    k_ref[0] = (_dot(ckvn, wkn_ref[...]) + jnp.tile(kr, (1, MLA_HEADS))).astype(BF16)
    ones_lane = (lax.broadcasted_iota(I32, (1, MLA_HEADS * MLA_HEAD_PAD), 1) % MLA_HEAD_PAD == MLA_V).astype(F32)
    v_ref[0] = (_dot(ckvn, wv_ref[...]) + ones_lane).astype(BF16)

    u_ref[0] = seg("u")

    qs_ref[0] = rotary(seg("qs"), cos_s, sin_s, SWA_HEAD_DIM).astype(BF16)
    ks = rotary(seg("ks"), cos_s, sin_s, SWA_HEAD_DIM)
    for hk in range(SWA_KV_HEADS):
        k_lo = ks[:, hk * 128:(hk + 1) * 128]
        ks_ref[0, :, (2 * hk) * 128:(2 * hk + 1) * 128] = k_lo.astype(BF16)
        ks_ref[0, :, (2 * hk + 1) * 128:(2 * hk + 2) * 128] = pltpu.roll(k_lo, SWA_HEAD_DIM, 1).astype(BF16)
    ones_s = (lax.broadcasted_iota(I32, (1, SWA_KV_HEADS * 128), 1) % 128 == SWA_HEAD_DIM).astype(F32)
    vs_ref[0] = (seg("vs") + ones_s).astype(BF16)

    g0, _ = _SEG["gl"]
    for p in range(6):
        gate_ref[0, :, p * 512:(p + 1) * 512] = _sigmoid(
            _dot(hb, w1_ref[:, g0 + p * 512:g0 + (p + 1) * 512])).astype(BF16)


def _input_stage(x_slab, b, modtab, g_mix, w1, g_q, wq, g_kv, wkn, wv, tabm, tabs):
    d = SLAB_ROWS * 128
    p = x_slab.shape[0] // (b * SLAB_ROWS)
    tm = TOKEN_TILE
    nj = p // tm
    tok = lambda w: pl.BlockSpec((1, tm, w), lambda bi, j: (bi, j, 0))
    full = lambda a: pl.BlockSpec(a.shape, lambda bi, j: (0,) * a.ndim)
    outs = [(1024, BF16), (1024, BF16), (1024, BF16), (512, F32), (512, BF16), (512, BF16), (256, BF16),
            (3072, BF16)]
    return pl.pallas_call(
        _in_kernel,
        grid=(b, nj),
        in_specs=[pl.BlockSpec((tm * SLAB_ROWS, 128), lambda bi, j: (bi * nj + j, 0)),
                  pl.BlockSpec((1, 1, 8, d), lambda bi, j: (bi, j // (nj - 1), 0, 0)),
                  full(g_mix), full(w1), full(g_q), full(wq), full(g_kv), full(wkn), full(wv),
                  pl.BlockSpec((tm, 256), lambda bi, j: (j, 0)),
                  pl.BlockSpec((tm, 256), lambda bi, j: (j, 0))],
        out_specs=[tok(w) for w, _ in outs],
        out_shape=[jax.ShapeDtypeStruct((b, p, w), dt) for w, dt in outs],
        compiler_params=_cparams(("arbitrary", "arbitrary")),
        name="input_stage",
    )(x_slab, modtab, g_mix, w1, g_q, wq, g_kv, wkn, wv, tabm, tabs)


def _mla_kernel(q_ref, k_ref, v_ref, *rest, tk, n_main, tail_rows):
    y_ref, m_ref, acc_ref = rest[-3:]
    hw = MLA_HEAD_PAD
    m_ref[...] = jnp.full(m_ref.shape, MASK_VALUE, F32)
    acc_ref[...] = jnp.zeros(acc_ref.shape, F32)

    def attend(r0, rows):
        for h in range(MLA_HEADS):
            q = q_ref[0, :, h * hw:(h + 1) * hw]
            k = k_ref[0, pl.ds(r0, rows), h * hw:(h + 1) * hw]
            v = v_ref[0, pl.ds(r0, rows), h * hw:(h + 1) * hw]
            s = _dot_nt(q, k)
            m_prev = m_ref[h]
            m_new = jnp.maximum(m_prev, jnp.max(s, axis=1, keepdims=True))
            m_ref[h] = m_new
            pr = jnp.exp2(s - jnp.tile(m_new, (1, rows // hw)))
            acc_ref[h] = jnp.exp2(m_prev - m_new) * acc_ref[h] + _dot(pr.astype(BF16), v)

    if n_main:
        def step(c, carry):
            attend(pl.multiple_of(c * tk, tk), tk)
            return carry

        lax.fori_loop(0, n_main, step, 0)
    if tail_rows:
        attend(n_main * tk, tail_rows)

    first = lax.broadcasted_iota(I32, (q_ref.shape[1], hw), 1) < MLA_V
    for hp in range(MLA_HEADS // 2):
        o = []
        for h in (2 * hp, 2 * hp + 1):
            acc = acc_ref[h]
            o.append(acc / acc[:, MLA_V:MLA_V + 1])
        y_ref[0, :, hp * hw:(hp + 1) * hw] = jnp.where(first, o[0], pltpu.roll(o[1], MLA_V, 1)).astype(BF16)


def _mla_attention(q, k, v, n_ctx):
    b, p, width = q.shape
    s_len = p - n_ctx
    tq = MLA_Q_TILE
    tk = MLA_KEY_CHUNK
    assert s_len % tq == 0 and s_len % tk == 0 and s_len % n_ctx == 0
    out_w = MLA_HEADS * MLA_V
    state = lambda rows: pltpu.VMEM((MLA_HEADS, rows, MLA_HEAD_PAD), F32)
    y_lat = pl.pallas_call(
        functools.partial(_mla_kernel, tk=tk, n_main=p // tk, tail_rows=p % tk),
        grid=(b, s_len // tq),
        in_specs=[pl.BlockSpec((1, tq, width), lambda bi, j: (bi, j, 0)),
                  pl.BlockSpec((1, p, width), lambda bi, j: (bi, 0, 0), pipeline_mode=pl.Buffered(1)),
                  pl.BlockSpec((1, p, width), lambda bi, j: (bi, 0, 0), pipeline_mode=pl.Buffered(1))],
        out_specs=pl.BlockSpec((1, tq, out_w), lambda bi, j: (bi, j, 0)),
        out_shape=jax.ShapeDtypeStruct((b, s_len, out_w), BF16),
        scratch_shapes=[state(tq), state(tq)],
        compiler_params=_cparams(("arbitrary", "arbitrary")),
        name="mla_attention",
    )(q, k, v)
    cblk = s_len // n_ctx
    ctx_rows = lambda w: pl.BlockSpec((1, n_ctx, w), lambda bi: (bi, cblk, 0))
    y_ctx = pl.pallas_call(
        functools.partial(_mla_kernel, tk=tk, n_main=0, tail_rows=n_ctx),
        grid=(b,),
        in_specs=[ctx_rows(width), ctx_rows(width), ctx_rows(width)],
        out_specs=pl.BlockSpec((1, n_ctx, out_w), lambda bi: (bi, 0, 0)),
        out_shape=jax.ShapeDtypeStruct((b, n_ctx, out_w), BF16),
        scratch_shapes=[state(n_ctx), state(n_ctx)],
        compiler_params=_cparams(("arbitrary",)),
        name="mla_attention_ctx",
    )(q, k, v)
    return y_lat, y_ctx


def _swa_kernel(sink_ref, q_ref, k_ref, v_ref, y_ref, *, n_ctx):
    p_len = k_ref.shape[1]
    s_len = p_len - n_ctx
    band = 3 * SWA_BLOCK
    n_lat_tiles = s_len // SWA_BLOCK
    first = lax.broadcasted_iota(I32, (SWA_BLOCK, 128), 1) < SWA_HEAD_DIM
    top = lax.broadcasted_iota(I32, (2 * SWA_BLOCK, 1), 0) < SWA_BLOCK
    n_blocks = q_ref.shape[1] // SWA_BLOCK
    chains = [(t, hk, par) for t in range(n_blocks) for hk in range(SWA_KV_HEADS) for par in range(2)]
    kstart, valid = {}, {}
    for t in range(n_blocks):
        j = pl.program_id(1) * n_blocks + t
        n = jnp.minimum(j, n_lat_tiles - 1)
        ws = jnp.clip((n - 1) * SWA_BLOCK, 0, s_len - band)
        kstart[t] = pl.multiple_of(ws, SWA_BLOCK)
        qpos = n * SWA_BLOCK + lax.broadcasted_iota(I32, (2 * SWA_BLOCK, band), 0) % SWA_BLOCK
        kpos = ws + lax.broadcasted_iota(I32, (2 * SWA_BLOCK, band), 1)
        valid[t] = (jnp.abs(qpos - kpos) <= SWA_WINDOW) & (j < n_lat_tiles)
    scores = {}
    for t, hk, par in chains:
        qrows = slice(t * SWA_BLOCK, (t + 1) * SWA_BLOCK)
        qpair = jnp.concatenate([q_ref[0, qrows, (2 * hk) * 128:(2 * hk + 1) * 128],
                                 q_ref[0, qrows, (2 * hk + 1) * 128:(2 * hk + 2) * 128]], axis=0)
        kcols = slice((2 * hk + par) * 128, (2 * hk + par + 1) * 128)
        s_c = _dot_nt(qpair, k_ref[0, s_len:p_len, kcols])
        s_b = jnp.where(valid[t], _dot_nt(qpair, k_ref[0, pl.ds(kstart[t], band), kcols]), MASK_VALUE)
        scores[t, hk, par] = (s_c, s_b)
    probs = {}
    for t, hk, par in chains:
        s_c, s_b = scores[t, hk, par]
        sink = jnp.where(top, sink_ref[4 * hk + par] * LOG2_E, sink_ref[4 * hk + 2 + par] * LOG2_E)
        m = jnp.maximum(jnp.maximum(jnp.max(s_c, axis=1, keepdims=True), jnp.max(s_b, axis=1, keepdims=True)), sink)
        probs[t, hk, par] = (jnp.exp2(s_c - m).astype(BF16), jnp.exp2(s_b - m).astype(BF16), jnp.exp2(sink - m))
    res = {}
    for t, hk, par in chains:
        p_c, p_b, p_sink = probs[t, hk, par]
        vcols = slice(hk * 128, (hk + 1) * 128)
        o = _dot(p_c, v_ref[0, s_len:p_len, vcols]) + _dot(p_b, v_ref[0, pl.ds(kstart[t], band), vcols])
        o = o / (o[:, SWA_HEAD_DIM:SWA_HEAD_DIM + 1] + p_sink)
        res[t, 4 * hk + par] = o[0:SWA_BLOCK]
        res[t, 4 * hk + 2 + par] = o[SWA_BLOCK:2 * SWA_BLOCK]
    for t in range(n_blocks):
        for pair in range(SWA_Q_HEADS // 2):
            y_ref[0, t * SWA_BLOCK:(t + 1) * SWA_BLOCK, pair * 128:(pair + 1) * 128] = jnp.where(
                first, res[t, 2 * pair], pltpu.roll(res[t, 2 * pair + 1], SWA_HEAD_DIM, 1)).astype(BF16)


def _swa_attention(sink, qs, ks, vs, n_ctx):
    b, p, _ = qs.shape
    tq = SWA_Q_TILE
    assert n_ctx % tq == 0 and (p - n_ctx) % tq == 0
    return pl.pallas_call(
        functools.partial(_swa_kernel, n_ctx=n_ctx),
        grid=(b, p // tq),
        in_specs=[pl.BlockSpec(memory_space=pltpu.SMEM),
                  pl.BlockSpec((1, tq, SWA_WIDTH), lambda bi, j: (bi, j, 0)),
                  pl.BlockSpec((1, p, ks.shape[2]), lambda bi, j: (bi, 0, 0)),
                  pl.BlockSpec((1, p, vs.shape[2]), lambda bi, j: (bi, 0, 0))],
        out_specs=pl.BlockSpec((1, tq, SWA_WIDTH), lambda bi, j: (bi, j, 0)),
        out_shape=jax.ShapeDtypeStruct((b, p, SWA_WIDTH), BF16),
        compiler_params=_cparams(("arbitrary", "arbitrary")),
        name="swa_attention",
    )(sink, qs, ks, vs)


def _pool_kernel(prev_ref, cur_ref, next_ref, w_ref, scale_ref, y_ref, ext_ref, *, n_ctx):
    j = pl.program_id(1)
    tm = cur_ref.shape[1]
    nj = pl.num_programs(1)
    s_len = (nj - 1) * tm
    is_ctx = j == nj - 1
    has_prev = (j >= 1) & (j < nj - 1)
    has_next = j < nj - 2
    ext_ref[0:POOL_HALO, :] = jnp.where(has_prev, prev_ref[0], 0.0)
    ext_ref[POOL_HALO:POOL_HALO + tm, :] = cur_ref[0]
    ext_ref[POOL_HALO + tm:POOL_HALO + tm + POOL_HALO, :] = jnp.where(has_next, next_ref[0], 0.0)
    t = lax.broadcasted_iota(I32, (tm, 1), 0)
    pos = jnp.where(is_ctx, t, j * tm + t)
    seg_len = jnp.where(is_ctx, n_ctx, s_len)
    for g, w in enumerate(POOL_WINDOWS):
        cols = slice(g * POOL_GROUP_DIM, (g + 1) * POOL_GROUP_DIM)
        acc = jnp.zeros((tm, POOL_GROUP_DIM), F32)
        for off in range(-(w // 2), w - w // 2):
            acc = acc + ext_ref[POOL_HALO + off:POOL_HALO + off + tm, cols]
        lo = jnp.maximum(pos - w // 2, 0)
        hi = jnp.minimum(pos + w - w // 2, seg_len)
        cnt = (hi - lo).astype(F32)
        pooled = acc / cnt - cur_ref[0, :, cols]
        y_ref[0, :, cols] = (_dot(pooled.astype(BF16), w_ref[g]) * scale_ref[:, cols]).astype(BF16)


def _pool_stage(u, w_pool, pool_scale, n_ctx):
    b, p, width = u.shape
    tm = TOKEN_TILE
    hb = tm // POOL_HALO
    n_halo_blocks = p // POOL_HALO
    return pl.pallas_call(
        functools.partial(_pool_kernel, n_ctx=n_ctx),
        grid=(b, p // tm),
        in_specs=[pl.BlockSpec((1, POOL_HALO, width), lambda bi, j: (bi, jnp.maximum(j * hb - 1, 0), 0)),
                  pl.BlockSpec((1, tm, width), lambda bi, j: (bi, j, 0)),
                  pl.BlockSpec((1, POOL_HALO, width),
                               lambda bi, j: (bi, jnp.minimum((j + 1) * hb, n_halo_blocks - 1), 0)),
                  pl.BlockSpec(w_pool.shape, lambda bi, j: (0, 0, 0)),
                  pl.BlockSpec(pool_scale.shape, lambda bi, j: (0, 0))],
        out_specs=pl.BlockSpec((1, tm, width), lambda bi, j: (bi, j, 0)),
        out_shape=jax.ShapeDtypeStruct((b, p, width), BF16),
        scratch_shapes=[pltpu.VMEM((tm + 2 * POOL_HALO, width), F32)],
        compiler_params=_cparams(("arbitrary", "arbitrary")),
        name="pool_stage",
    )(u, u, u, w_pool, pool_scale)


def _merge_kernel(yml_ref, ymc_ref, yp_ref, ys_ref, gate_ref, x_ref, mod_ref, g_ref, wbm_ref, wbp_ref, wbs_ref,
                  wout_ref, wsg_ref, wsu_ref, wsd_ref, wrt_ref, xs_ref, fp_ref, lg_ref, *, ctx_tile):
    d = SLAB_ROWS * 128
    mod = mod_ref[0, 0]
    gate = gate_ref[0]
    ym = jnp.where(pl.program_id(1) == ctx_tile, ymc_ref[0], yml_ref[0])
    merged = (gate[:, 0:d].astype(F32) * _dot(ym, wbm_ref[...])
              + gate[:, d:2 * d].astype(F32) * _dot(yp_ref[0], wbp_ref[...])
              + gate[:, 2 * d:3 * d].astype(F32) * _dot(ys_ref[0], wbs_ref[...]))
    x = _slab_rows_to_matrix(x_ref, gate.shape[0], SLAB_ROWS)
    x_mid = x + mod[2:3] * _dot(merged.astype(BF16), wout_ref[...])
    f = _rms(x_mid, g_ref[...]) * (1.0 + mod[4:5]) + mod[3:4]
    fb = f.astype(BF16)
    gsh = _dot(fb, wsg_ref[...])
    shared = _dot((gsh * _sigmoid(gsh) * _dot(fb, wsu_ref[...])).astype(BF16), wsd_ref[...])
    _matrix_to_slab_rows(xs_ref, x_mid + mod[5:6] * shared, SLAB_ROWS)
    _matrix_to_slab_rows(fp_ref, _pack_bf16_pair(f), PACKED_ROWS)
    lg_ref[...] = _dot_nt(wrt_ref[...], fb)


def _merge_stage(ym_lat, ym_ctx, yp, ys, gates, x_slab, modtab, g_ffn, wbm, wbp, wbs, wout, wsg, wsu, wsd, wrt,
                 keep_ctx):
    b, p, _ = yp.shape
    assert ym_ctx.shape[1] == TOKEN_TILE
    d = SLAB_ROWS * 128
    tm = TOKEN_TILE
    nj = p // tm
    nk = nj if keep_ctx else nj - 1
    tok = lambda w: pl.BlockSpec((1, tm, w), lambda bi, j: (bi, j, 0))
    slab_out = lambda rows: pl.BlockSpec((tm * rows, 128), lambda bi, j: (bi * nk + j, 0))
    full = lambda a: pl.BlockSpec(a.shape, lambda bi, j: (0,) * a.ndim)
    return pl.pallas_call(
        functools.partial(_merge_kernel, ctx_tile=nj - 1),
        grid=(b, nk),
        in_specs=[pl.BlockSpec((1, tm, ym_lat.shape[2]), lambda bi, j: (bi, jnp.minimum(j, nj - 2), 0)),
                  pl.BlockSpec((1, tm, ym_ctx.shape[2]), lambda bi, j: (bi, 0, 0)),
                  tok(yp.shape[2]), tok(ys.shape[2]), tok(gates.shape[2]),
                  pl.BlockSpec((tm * SLAB_ROWS, 128), lambda bi, j: (bi * nj + j, 0)),
                  pl.BlockSpec((1, 1, 8, d), lambda bi, j: (bi, j // (nj - 1), 0, 0)),
                  full(g_ffn), full(wbm), full(wbp), full(wbs), full(wout), full(wsg), full(wsu), full(wsd),
                  full(wrt)],
        out_specs=[slab_out(SLAB_ROWS), slab_out(PACKED_ROWS),
                   pl.BlockSpec((N_EXPERTS, tm), lambda bi, j: (0, bi * nk + j))],
        out_shape=[jax.ShapeDtypeStruct((b * nk * tm * SLAB_ROWS, 128), F32),
                   jax.ShapeDtypeStruct((b * nk * tm * PACKED_ROWS, 128), U32),
                   jax.ShapeDtypeStruct((N_EXPERTS, b * nk * tm), F32)],
        compiler_params=_cparams(("arbitrary", "arbitrary")),
        name="merge_stage",
    )(ym_lat, ym_ctx, yp, ys, gates, x_slab, modtab, g_ffn, wbm, wbp, wbs, wout, wsg, wsu, wsd, wrt)


def _route_kernel(lg_ref, bias_ref, dest_ref, wts_ref, seg_ref):
    tm = lg_ref.shape[1]
    ne = N_EXPERTS
    neg_inf = -jnp.inf
    scores = _sigmoid(lg_ref[...])
    sel = scores + bias_ref[...]
    iota_g = lax.broadcasted_iota(I32, (EXPERTS_PER_GROUP, tm), 0)
    gscore = []
    for g in range(N_GROUPS):
        sg = sel[g * EXPERTS_PER_GROUP:(g + 1) * EXPERTS_PER_GROUP]
        m1 = jnp.max(sg, axis=0, keepdims=True)
        i1 = jnp.min(jnp.where(sg == m1, iota_g, EXPERTS_PER_GROUP), axis=0, keepdims=True)
        m2 = jnp.max(jnp.where(iota_g == i1, neg_inf, sg), axis=0, keepdims=True)
        gscore.append(m1 + m2)
    rows = []
    for g in range(N_GROUPS):
        rank = jnp.zeros((1, tm), I32)
        for g2 in range(N_GROUPS):
            if g2 == g:
                continue
            beats = (gscore[g2] >= gscore[g]) if g2 < g else (gscore[g2] > gscore[g])
            rank = rank + beats.astype(I32)
        rows.append(jnp.where(rank < TOPK_GROUPS, sel[g * EXPERTS_PER_GROUP:(g + 1) * EXPERTS_PER_GROUP], neg_inf))
    cur = jnp.concatenate(rows, axis=0)
    iota_e = lax.broadcasted_iota(I32, (ne, tm), 0)
    picks = []
    member = jnp.zeros((ne, tm), F32)
    for _ in range(TOP_K):
        m = jnp.max(cur, axis=0, keepdims=True)
        idx = jnp.min(jnp.where(cur == m, iota_e, ne), axis=0, keepdims=True)
        hit = iota_e == idx
        picks.append(hit)
        member = member + hit.astype(F32)
        cur = jnp.where(hit, neg_inf, cur)
    earlier = (lax.broadcasted_iota(I32, (tm, tm), 0) < lax.broadcasted_iota(I32, (tm, tm), 1)).astype(BF16)
    pos = _dot(member.astype(BF16), earlier)
    cnt_col = jnp.sum(member, axis=1, keepdims=True)
    blocks_col = jnp.floor((cnt_col + (SEG_ALIGN - 1)) * (1.0 / SEG_ALIGN))
    lower = (lax.broadcasted_iota(I32, (ne, ne), 1) < lax.broadcasted_iota(I32, (ne, ne), 0)).astype(BF16)
    off_col = _dot(lower, jnp.broadcast_to(blocks_col, (ne, 128)).astype(BF16))[:, 0:1] * SEG_ALIGN
    base = off_col + pos
    w_rows = [jnp.sum(jnp.where(hit, scores, 0.0), axis=0, keepdims=True) for hit in picks]
    denom = w_rows[0]
    for w in w_rows[1:]:
        denom = denom + w
    for k, hit in enumerate(picks):
        dest_ref[0, k:k + 1, :] = jnp.sum(jnp.where(hit, base, 0.0), axis=0, keepdims=True).astype(I32)
        wts_ref[0, k:k + 1, :] = w_rows[k] / denom * ROUTED_SCALE
    member_pad = jnp.concatenate([member, jnp.zeros((128 - ne, tm), F32)], axis=0).astype(BF16)
    cnt_row = _dot_nt(jnp.ones((8, tm), BF16), member_pad)
    blocks_row = jnp.floor((cnt_row + (SEG_ALIGN - 1)) * (1.0 / SEG_ALIGN))
    before = (lax.broadcasted_iota(I32, (128, 128), 0) < lax.broadcasted_iota(I32, (128, 128), 1)).astype(BF16)
    off_row = _dot(blocks_row.astype(BF16), before) * SEG_ALIGN
    r = lax.broadcasted_iota(I32, (8, 128), 0)
    seg_ref[0] = jnp.where(r == 0, off_row, jnp.where(r == 1, cnt_row, 0.0)).astype(I32)


def _route_stage(lg_t, bias_col):
    ne, t_all = lg_t.shape
    tm = MOE_TILE
    nt = t_all // tm
    return pl.pallas_call(
        _route_kernel,
        grid=(nt,),
        in_specs=[pl.BlockSpec((ne, tm), lambda i: (0, i)),
                  pl.BlockSpec((ne, 1), lambda i: (0, 0))],
        out_specs=[pl.BlockSpec((1, TOP_K, tm), lambda i: (i, 0, 0)),
                   pl.BlockSpec((1, TOP_K, tm), lambda i: (i, 0, 0)),
                   pl.BlockSpec((1, 8, 128), lambda i: (i, 0, 0))],
        out_shape=[jax.ShapeDtypeStruct((nt, TOP_K, tm), I32),
                   jax.ShapeDtypeStruct((nt, TOP_K, tm), F32),
                   jax.ShapeDtypeStruct((nt, 8, 128), I32)],
        compiler_params=_cparams(("arbitrary",)),
        name="route_stage",
    )(lg_t, bias_col)


def _moe_row_stride(tm):
    cap = TOP_K * tm + N_EXPERTS * SEG_ALIGN + MOE_CHUNK
    blocks = cap // 8 + 1
    return 8 * (blocks + 1 - blocks % 2)


def _moe_kernel(dest_ref, wts_ref, seg_ref, fp_hbm, xs_hbm, gt_ref, gfin_ref, wg_ref, wu_ref, wd_ref, out_hbm,
                bufs_ref, fp_stage, xs_stage, out_stage, slab_tmp, fp_sem, xs_sem, out_sem,
                *, srow, sub, n_sub_per_batch, ctx_sub, n_tiles, final_norm):
    pr = pl.program_id(0)
    eg = pl.program_id(1)
    tm = dest_ref.shape[2]
    ch = MOE_CHUNK
    group = wg_ref.shape[0]
    lane_blk = 128
    n_blk = tm // lane_blk
    per = bufs_ref.shape[0]
    n_slots = fp_stage.shape[0]
    n_here = jnp.minimum(per, n_tiles - pr * per)

    def table_block(ref, half, blk):
        return ref.at[half, :, pl.ds(pl.multiple_of(blk * lane_blk, lane_blk), lane_blk)]

    def token_rows(half, blk, per_tok):
        first = ((pr * per + half) * tm + blk * lane_blk) * per_tok
        return pl.ds(pl.multiple_of(first, lane_blk * per_tok), lane_blk * per_tok)

    def fp_copy(half, blk, slot):
        return pltpu.make_async_copy(fp_hbm.at[token_rows(half, blk, PACKED_ROWS), :], fp_stage.at[slot],
                                     fp_sem.at[slot])

    def xs_copy(half, blk, slot):
        return pltpu.make_async_copy(xs_hbm.at[token_rows(half, blk, SLAB_ROWS), :], xs_stage.at[slot],
                                     xs_sem.at[slot])

    def out_copy(half, blk, slot):
        rows = token_rows(half, blk, 1 if final_norm else SLAB_ROWS)
        return pltpu.make_async_copy(out_stage.at[slot], out_hbm.at[rows, :], out_sem.at[slot])

    def group_rows(half):
        buf_ref = bufs_ref.at[half]
        for blk in range(n_slots - 1):
            fp_copy(half, blk, blk).start()
        buf_ref[...] = jnp.zeros(buf_ref.shape, U32)

        def body(blk, carry):
            slot = blk % n_slots
            fp_copy(half, blk, slot).wait()

            @pl.when(blk + n_slots - 1 < n_blk)
            def _prefetch():
                fp_copy(half, blk + n_slots - 1, (blk + n_slots - 1) % n_slots).start()

            dest = table_block(dest_ref, half, blk)
            rows = fp_stage.at[slot]
            for u in range(lane_blk):
                slab = rows[u * PACKED_ROWS:(u + 1) * PACKED_ROWS, :]
                for k in range(TOP_K):
                    buf_ref[pl.ds(dest[k, u], PACKED_ROWS, stride=srow), :] = slab
            return carry

        lax.fori_loop(0, n_blk, body, 0)

    def experts():
        def one_expert(ge, carry):
            e = eg * group + ge
            counts = [seg_ref[half, 1, e] for half in range(per)]
            offsets = [seg_ref[half, 0, e] for half in range(per)]

            def ffn_rows(halves, rows, chunk_idx):
                words, keep = [], []
                for h in halves:
                    r0 = offsets[h] + chunk_idx * rows
                    words.append([bufs_ref[h, pl.ds(pl.multiple_of(q * srow + r0, 8), rows), :] for q in range(4)])
                    keep.append(lax.broadcasted_iota(I32, (rows, 1), 0) < counts[h] - chunk_idx * rows)
                xb = []
                for ws in words:
                    w = jnp.concatenate(ws, axis=1)
                    xb.append(jnp.concatenate([_unpack_lo(w), _unpack_hi(w)], axis=1).astype(BF16))
                gates = [(_dot(x, wg_ref[ge]), _dot(x, wu_ref[ge])) for x in xb]
                hmid = [(g * _sigmoid(g) * u).astype(BF16) for g, u in gates]
                packed = [_pack_bf16_pair(_dot(hm, wd_ref[ge])) for hm in hmid]
                for n, h in enumerate(halves):
                    r0 = offsets[h] + chunk_idx * rows
                    for q in range(4):
                        bufs_ref[h, pl.ds(pl.multiple_of(q * srow + r0, 8), rows), :] = jnp.where(
                            keep[n], packed[n][:, q * 128:(q + 1) * 128], words[n][q])

            most = counts[0]
            for c in counts[1:]:
                most = jnp.maximum(most, c)
            lo = 0
            for rows in MOE_BODY_ROWS:
                pl.when((most > lo) & (most <= rows))(functools.partial(ffn_rows, tuple(range(per)), rows, 0))
                lo = rows

            @pl.when(most > ch)
            def _long_segments():
                for h in range(per):
                    def chunk(c, carry2, h=h):
                        ffn_rows((h,), ch, c)
                        return carry2

                    lax.fori_loop(0, (counts[h] + ch - 1) // ch, chunk, 0)

            return carry

        lax.fori_loop(0, group, one_expert, 0)

    def combine(half):
        buf_ref = bufs_ref.at[half]
        for blk in range(n_slots - 1):
            xs_copy(half, blk, blk).start()

        def body(blk, carry):
            slot = blk % n_slots
            xs_copy(half, blk, slot).wait()

            @pl.when(blk + n_slots - 1 < n_blk)
            def _prefetch():
                xs_copy(half, blk + n_slots - 1, (blk + n_slots - 1) % n_slots).start()

            @pl.when(blk >= n_slots)
            def _slot_free():
                out_copy(half, blk - n_slots, slot).wait()

            sub_blk = ((pr * per + half) * tm + blk * lane_blk) // sub
            bi = sub_blk // n_sub_per_batch
            gate = gt_ref[bi * 2 + (sub_blk - bi * n_sub_per_batch) // ctx_sub]
            dest = table_block(dest_ref, half, blk)
            wts = table_block(wts_ref, half, blk)
            xs = xs_stage.at[slot]
            out = slab_tmp if final_norm else out_stage.at[slot]
            for u in range(lane_blk):
                acc_lo = jnp.zeros((4, 128), F32)
                acc_hi = jnp.zeros((4, 128), F32)
                for k in range(TOP_K):
                    words = buf_ref[pl.ds(dest[k, u], PACKED_ROWS, stride=srow), :]
                    wk = wts[k, u]
                    acc_lo = acc_lo + wk * _unpack_lo(words)
                    acc_hi = acc_hi + wk * _unpack_hi(words)
                out[u * 8:u * 8 + 4, :] = xs[u * 8:u * 8 + 4, :] + gate[0:4] * acc_lo
                out[u * 8 + 4:u * 8 + 8, :] = xs[u * 8 + 4:u * 8 + 8, :] + gate[4:8] * acc_hi
            if final_norm:
                out_stage[slot] = _rms(_slab_rows_to_matrix(slab_tmp, lane_blk, SLAB_ROWS), gfin_ref[...])
            out_copy(half, blk, slot).start()
            return carry

        lax.fori_loop(0, n_blk, body, 0)
        for blk in range(n_blk - n_slots, n_blk):
            out_copy(half, blk, blk % n_slots).wait()

    def each_tile(fn):
        def body(half, carry):
            fn(half)
            return carry

        lax.fori_loop(0, n_here, body, 0)

    @pl.when(eg == 0)
    def _first_step():
        each_tile(group_rows)
        for half in range(1, per):
            @pl.when(half >= n_here)
            def _clear(half=half):
                bufs_ref[half] = jnp.zeros(bufs_ref.shape[1:], U32)

    experts()
    pl.when(eg == pl.num_programs(1) - 1)(functools.partial(each_tile, combine))


def _moe_stage(dest, wts, seg, fp4, xs8, gt2, g_final, wg, wu, wd, layer, n_sub_per_batch, keep_ctx, final_norm):
    ctx_sub = n_sub_per_batch - 1 if keep_ctx else n_sub_per_batch
    d = SLAB_ROWS * 128
    out_shape = (xs8.shape[0] // SLAB_ROWS, d) if final_norm else xs8.shape
    out_block = (128, d) if final_norm else (128 * SLAB_ROWS, 128)
    nt, _, tm = dest.shape
    ne = wg.shape[1]
    group = MOE_EXPERTS_PER_STEP
    per = MOE_TILES_PER_STEP
    assert ne % group == 0 and tm % 256 == 0 and TOKEN_TILE % 128 == 0 and per == 2
    n_rows = -(-nt // per)
    pad = ((0, n_rows * per - nt), (0, 0), (0, 0))
    dest, wts, seg = jnp.pad(dest, pad), jnp.pad(wts, pad), jnp.pad(seg, pad)
    srow = _moe_row_stride(tm)
    smem = lambda shape: pl.BlockSpec(shape, lambda i, e: (i, 0, 0), memory_space=pltpu.SMEM)
    hbm = pl.BlockSpec(memory_space=pl.ANY)
    slots = MOE_STAGE_SLOTS
    assert tm // 128 >= slots
    stage = lambda per_tok, dt: pltpu.VMEM((slots, 128 * per_tok, 128), dt)
    return pl.pallas_call(
        functools.partial(_moe_kernel, srow=srow, sub=TOKEN_TILE, n_sub_per_batch=n_sub_per_batch, ctx_sub=ctx_sub,
                          n_tiles=nt, final_norm=final_norm),
        grid=(n_rows, ne // group),
        in_specs=[smem((per, TOP_K, tm)), smem((per, TOP_K, tm)), smem((per, 8, 128)),
                  hbm, hbm,
                  pl.BlockSpec(gt2.shape, lambda i, e: (0, 0, 0)),
                  pl.BlockSpec(g_final.shape, lambda i, e: (0, 0)),
                  pl.BlockSpec((None, group) + wg.shape[2:], lambda i, e: (layer, e, 0, 0)),
                  pl.BlockSpec((None, group) + wu.shape[2:], lambda i, e: (layer, e, 0, 0)),
                  pl.BlockSpec((None, group) + wd.shape[2:], lambda i, e: (layer, e, 0, 0))],
        out_specs=hbm,
        out_shape=jax.ShapeDtypeStruct(out_shape, F32),
        scratch_shapes=[pltpu.VMEM((per, 4 * srow, 128), U32),
                        stage(PACKED_ROWS, U32), stage(SLAB_ROWS, F32), pltpu.VMEM((slots,) + out_block, F32),
                        pltpu.VMEM((128 * SLAB_ROWS, 128), F32),
                        pltpu.SemaphoreType.DMA((slots,)), pltpu.SemaphoreType.DMA((slots,)),
                        pltpu.SemaphoreType.DMA((slots,))],
        compiler_params=_cparams(("arbitrary", "arbitrary")),
        name="moe_stage",
    )(dest, wts, seg, fp4, xs8, gt2, g_final, wg, wu, wd)


def _rot_lanes(x, head_dim):
    quarter = head_dim // 4
    lane = lax.broadcasted_iota(I32, x.shape, 1)
    width = x.shape[1]
    return jnp.where(lane % (2 * quarter) < quarter, -pltpu.roll(x, width - quarter, 1), pltpu.roll(x, quarter, 1))


def _w1_kernel(w_ref, o_ref):
    offs = [0]
    for w in (MLA_Q_LORA, MLA_KV_LORA, MLA_ROPE, POOL_WIDTH, SWA_WIDTH, SWA_KV_WIDTH, SWA_KV_WIDTH):
        offs.append(offs[-1] + w)
    o_cq, o_ckv, o_kr, o_u, o_qs, o_ks, o_vs, o_gl = offs
    lane = lax.broadcasted_iota(I32, (w_ref.shape[0], 128), 1)

    def put(name, val):
        a, b = _SEG[name]
        o_ref[:, a:b] = val.astype(BF16)

    def kv_slots(x):
        low = lane < SWA_HEAD_DIM
        return jnp.concatenate([jnp.where(low, x, 0.0), jnp.where(low, pltpu.roll(x, SWA_HEAD_DIM, 1), 0.0)], axis=1)

    def kr_slot(x):
        return jnp.where((lane >= MLA_NOPE) & (lane < MLA_NOPE + MLA_ROPE), pltpu.roll(x, MLA_NOPE, 1), 0.0)

    put("cq", w_ref[:, o_cq:o_ckv])
    put("ckv", w_ref[:, o_ckv:o_kr])
    put("u", w_ref[:, o_u:o_qs])
    put("qs", w_ref[:, o_qs:o_ks] * (SWA_SCALE * LOG2_E))
    put("ks", kv_slots(w_ref[:, o_ks:o_vs]))
    put("vs", kv_slots(w_ref[:, o_vs:o_gl]))
    put("kr", kr_slot(w_ref[:, o_kr:o_kr + 128]))
    put("gl", w_ref[:, o_gl:w_ref.shape[1]])


def _fused_in_weight(w_in, layer):
    _, d, width = w_in.shape
    tr = 256
    assert d % tr == 0 and SWA_KV_WIDTH == 128 and MLA_KV_LORA + MLA_Q_LORA == 5 * 128
    return pl.pallas_call(
        _w1_kernel,
        grid=(d // tr,),
        in_specs=[pl.BlockSpec((None, tr, width), lambda i: (layer, i, 0))],
        out_specs=pl.BlockSpec((tr, FUSED_IN_WIDTH), lambda i: (i, 0)),
        out_shape=jax.ShapeDtypeStruct((d, FUSED_IN_WIDTH), BF16),
        compiler_params=_cparams(("arbitrary",)),
        name="fused_in_weight",
    )(w_in)


def _rope_pattern(s_len, n_ctx, rot_dim):
    t = jnp.arange(s_len)
    row = (t // GRID_W).astype(F32)
    col = (t % GRID_W).astype(F32)
    n_freq = rot_dim // 4
    inv_freq = ROPE_BASE ** (-jnp.arange(n_freq, dtype=F32) / n_freq)
    ang_r = row[:, None] * inv_freq[None, :]
    ang_c = col[:, None] * inv_freq[None, :]
    cos = jnp.concatenate([jnp.cos(ang_r), jnp.cos(ang_r), jnp.cos(ang_c), jnp.cos(ang_c)], axis=1)
    sin = jnp.concatenate([jnp.sin(ang_r), jnp.sin(ang_r), jnp.sin(ang_c), jnp.sin(ang_c)], axis=1)
    cos = jnp.concatenate([cos, jnp.ones((n_ctx, rot_dim), F32)], axis=0)
    sin = jnp.concatenate([sin, jnp.zeros((n_ctx, rot_dim), F32)], axis=0)
    return cos, sin


def _layer_weights(w_in, layer, w_uq, w_ukv):
    w1 = _fused_in_weight(w_in, layer)
    lq = w_uq.shape[0]
    wq3 = (w_uq * (MLA_SCALE * LOG2_E)).reshape(lq, MLA_HEADS, MLA_NOPE + MLA_ROPE)
    pad = jnp.zeros((lq, MLA_HEADS, MLA_HEAD_PAD - MLA_NOPE - MLA_ROPE), F32)
    wq = jnp.concatenate([wq3, pad], axis=2).reshape(lq, MLA_HEADS * MLA_HEAD_PAD).astype(BF16)
    lkv = w_ukv.shape[0]
    wkv3 = w_ukv.reshape(lkv, MLA_HEADS, MLA_NOPE + MLA_V)
    wkn = jnp.concatenate([wkv3[:, :, :MLA_NOPE], jnp.zeros((lkv, MLA_HEADS, MLA_HEAD_PAD - MLA_NOPE), F32)],
                          axis=2).reshape(lkv, MLA_HEADS * MLA_HEAD_PAD).astype(BF16)
    wv = jnp.concatenate([wkv3[:, :, MLA_NOPE:], jnp.zeros((lkv, MLA_HEADS, MLA_HEAD_PAD - MLA_V), F32)],
                         axis=2).reshape(lkv, MLA_HEADS * MLA_HEAD_PAD).astype(BF16)
    return w1, wq, wkn, wv


def kernel(x, c, ctx, c_ctx, w_mod, b_mod, g_mix, g_ffn, w_in, g_mla_q, g_mla_kv, w_mla_uq, w_mla_ukv, w_pool,
           pool_scale, swa_sink, w_br_mla, w_br_pool, w_br_swa, w_out, w_router, router_bias, w_exp_gate,
           w_exp_up, w_exp_down, w_sh_gate, w_sh_up, w_sh_down, g_final):
    b, s_len, d = x.shape
    n_ctx = ctx.shape[1]
    n_layers = w_mod.shape[0]
    p = n_ctx + s_len
    assert n_ctx == TOKEN_TILE and s_len % TOKEN_TILE == 0 and (b * p) % MOE_TILE == 0 and b + 1 <= 8
    assert d == 1024 and w_in.shape[2] == (MLA_Q_LORA + MLA_KV_LORA + MLA_ROPE + POOL_WIDTH + SWA_WIDTH
                                           + 2 * SWA_KV_WIDTH + 3 * d)

    cm, sm = _rope_pattern(s_len, n_ctx, MLA_ROPE)
    tail = jnp.zeros((p, MLA_HEAD_PAD - MLA_NOPE - MLA_ROPE), F32)
    tabm = jnp.concatenate([jnp.ones((p, MLA_NOPE), F32), cm, tail, jnp.zeros((p, MLA_NOPE), F32), sm, tail], axis=1)
    cs, ss = _rope_pattern(s_len, n_ctx, SWA_HEAD_DIM)
    tabs = jnp.concatenate([cs, cs, ss, ss], axis=1)

    cvec = jnp.concatenate([c, c_ctx[None, :], jnp.zeros((8 - b - 1, d), F32)], axis=0)
    mod_all = _modulation(cvec, w_mod, b_mod).reshape(n_layers, 8, N_MOD, d)

    x_slab = _assemble_stream(x, ctx)
    wg_all, wu_all, wd_all = w_exp_gate.astype(BF16), w_exp_up.astype(BF16), w_exp_down.astype(BF16)
    for i in range(n_layers):
        keep_ctx = i < n_layers - 1
        lat = mod_all[i, :b]
        ctx_rows = jnp.broadcast_to(mod_all[i, b][None], (b, N_MOD, d))
        modtab = jnp.pad(jnp.stack([lat, ctx_rows], axis=1), ((0, 0), (0, 0), (0, 8 - N_MOD), (0, 0)))
        w1, wq, wkn, wv = _layer_weights(w_in, i, w_mla_uq[i], w_mla_ukv[i])
        q, k, v, u, qs, ks, vs, gates = _input_stage(
            x_slab, b, modtab, g_mix[i][None], w1, g_mla_q[i][None], wq, g_mla_kv[i][None], wkn, wv, tabm, tabs)
        y_mla, y_mla_ctx = _mla_attention(q, k, v, n_ctx)
        y_swa = _swa_attention(swa_sink[i], qs, ks, vs, n_ctx)
        y_pool = _pool_stage(u, w_pool[i].astype(BF16), pool_scale[i][None], n_ctx)
        xs, fp, lg_t = _merge_stage(
            y_mla, y_mla_ctx, y_pool, y_swa, gates, x_slab, modtab, g_ffn[i][None], w_br_mla[i].astype(BF16),
            w_br_pool[i].astype(BF16), w_br_swa[i].astype(BF16), w_out[i].astype(BF16),
            w_sh_gate[i].astype(BF16), w_sh_up[i].astype(BF16), w_sh_down[i].astype(BF16),
            w_router[i].T.astype(BF16), keep_ctx)
        dest, wts, seg = _route_stage(lg_t, router_bias[i][:, None])
        gt2 = modtab[:, :, 5, :].reshape(b * 2, 8, d // 8)
        x_slab = _moe_stage(dest, wts, seg, fp, xs, gt2, g_final[None], wg_all, wu_all, wd_all, i,
                            (p if keep_ctx else s_len) // TOKEN_TILE, keep_ctx, final_norm=not keep_ctx)
    return x_slab.reshape(b, s_len, d)
```

```python
import functools

import jax
import jax.numpy as jnp
from jax import lax
from jax.experimental import pallas as pl
from jax.experimental.pallas import tpu as pltpu

F32 = jnp.float32
BF16 = jnp.bfloat16
U32 = jnp.uint32
I32 = jnp.int32

NORM_EPS = 1e-6
ROPE_BASE = 10000.0
GRID_W = 64
N_MOD = 6

MLA_HEADS = 8
MLA_Q_LORA = 384
MLA_KV_LORA = 256
MLA_NOPE = 64
MLA_ROPE = 32
MLA_V = 64
MLA_SCALE = (MLA_NOPE + MLA_ROPE) ** -0.5
MLA_HEAD_PAD = 128
LOG2_E = 1.4426950408889634

POOL_WINDOWS = (2, 4, 8, 16)
POOL_GROUP_DIM = 128
POOL_WIDTH = 512
POOL_HALO = 8

SWA_Q_HEADS = 8
SWA_KV_HEADS = 2
SWA_HEAD_DIM = 64
SWA_WINDOW = 128
SWA_BLOCK = 128
SWA_SCALE = SWA_HEAD_DIM ** -0.5
SWA_WIDTH = SWA_Q_HEADS * SWA_HEAD_DIM
SWA_KV_WIDTH = SWA_KV_HEADS * SWA_HEAD_DIM

N_EXPERTS = 64
TOP_K = 8
N_GROUPS = 8
TOPK_GROUPS = 4
EXPERTS_PER_GROUP = 8
D_EXPERT = 256
ROUTED_SCALE = 2.5

TOKEN_TILE = 256
SWA_Q_TILE = 256
MLA_Q_TILE = 1024
MLA_KEY_CHUNK = 256
MOE_TILE = 1024
MOE_CHUNK = 256
MOE_BODY_ROWS = (128, 160, 192, 224, 256)
MOE_EXPERTS_PER_STEP = 2
MOE_TILES_PER_STEP = 2
MOE_STAGE_SLOTS = 4
SEG_ALIGN = 8
SLAB_ROWS = 8
PACKED_ROWS = 4
MASK_VALUE = -1e30
HI16 = 0xFFFF0000

VMEM_LIMIT = 56 * 1024 * 1024

_SEG_WIDTHS = (("cq", 384), ("ckv", 256), ("u", 512), ("qs", 512), ("ks", 256), ("vs", 256), ("kr", 128),
               ("gl", 3072))
_SEG = {}
_o = 0
for _n, _w in _SEG_WIDTHS:
    _SEG[_n] = (_o, _o + _w)
    _o += _w
FUSED_IN_WIDTH = _o


def _cparams(sem):
    return pltpu.CompilerParams(dimension_semantics=sem, vmem_limit_bytes=VMEM_LIMIT)


def _dot(a, b):
    return jnp.dot(a, b, preferred_element_type=F32)


def _dot_nt(a, b):
    return lax.dot_general(a, b, (((1,), (1,)), ((), ())), preferred_element_type=F32)


def _sigmoid(x):
    return 1.0 / (1.0 + jnp.exp(-x))


def _rms(x, g):
    return x * lax.rsqrt(jnp.mean(x * x, axis=-1, keepdims=True) + NORM_EPS) * g


def _pack_bf16_pair(v):
    n = v.shape[1] // 2
    bits = pltpu.bitcast(v.astype(BF16).astype(F32), U32)
    return (bits[:, :n] >> 16) | (bits[:, n:] & jnp.uint32(HI16))


def _slab_rows_to_matrix(ref, n_tok, per_tok):
    return jnp.concatenate([ref[pl.ds(c, n_tok, stride=per_tok), :] for c in range(per_tok)], axis=1)


def _matrix_to_slab_rows(ref, val, per_tok):
    n_tok = val.shape[0]
    for c in range(per_tok):
        ref[pl.ds(c, n_tok, stride=per_tok), :] = val[:, c * 128:(c + 1) * 128]


def _unpack_lo(w):
    return pltpu.bitcast(w << 16, F32)


def _unpack_hi(w):
    return pltpu.bitcast(w & jnp.uint32(HI16), F32)


def _mod_kernel(c_ref, w_ref, b_ref, o_ref):
    c = c_ref[...]
    a = (c * _sigmoid(c)).astype(BF16)
    o_ref[0] = _dot(a, w_ref[0].astype(BF16)) + b_ref[0]


def _modulation(cvec, w_mod, b_mod):
    n_layers, d, width = w_mod.shape
    tn = width // 4
    return pl.pallas_call(
        _mod_kernel,
        grid=(n_layers, width // tn),
        in_specs=[pl.BlockSpec((8, d), lambda l, n: (0, 0)),
                  pl.BlockSpec((1, d, tn), lambda l, n: (l, 0, n)),
                  pl.BlockSpec((1, 1, tn), lambda l, n: (l, 0, n))],
        out_specs=pl.BlockSpec((1, 8, tn), lambda l, n: (l, 0, n)),
        out_shape=jax.ShapeDtypeStruct((n_layers, 8, width), F32),
        compiler_params=_cparams(("arbitrary", "arbitrary")),
        name="modulation",
    )(cvec, w_mod, b_mod.reshape(n_layers, 1, width))


def _stream_specs(stream, tm, nj):
    if len(stream) == 1:
        return [pl.BlockSpec((tm * SLAB_ROWS, 128), lambda bi, j: (bi * nj + j, 0))]
    d = stream[0].shape[2]
    return [pl.BlockSpec((1, tm, d), lambda bi, j: (bi, jnp.minimum(j, nj - 2), 0)),
            pl.BlockSpec((1, tm, d), lambda bi, j: (bi, 0, 0))]


def _stream_tile(refs, n_tok, ctx_tile):
    if len(refs) == 1:
        return _slab_rows_to_matrix(refs[0], n_tok, SLAB_ROWS)
    return jnp.where(pl.program_id(1) == ctx_tile, refs[1][0], refs[0][0])


def _stream_tokens(stream, b):
    return stream[0].shape[0] // (b * SLAB_ROWS) if len(stream) == 1 else stream[0].shape[1] + stream[1].shape[1]


def _in_kernel(*refs, n_stream, ctx_tile):
    (mod_ref, g_ref, w1_ref, gq_ref, wq_ref, gkv_ref, wkn_ref, wv_ref, tabm_ref, tabs_ref,
     q_ref, k_ref, v_ref, u_ref, qs_ref, ks_ref, vs_ref, gate_ref) = refs[n_stream:]
    x = _stream_tile(refs[:n_stream], q_ref.shape[1], ctx_tile)
    mod = mod_ref[0, 0]
    h = _rms(x, g_ref[...]) * (1.0 + mod[1:2]) + mod[0:1]
    hb = h.astype(BF16)

    def seg(name):
        a, b = _SEG[name]
        return _dot(hb, w1_ref[:, a:b])

    cos_m = tabm_ref[:, 0:128]
    sin_m = tabm_ref[:, 128:256]
    cos_s = tabs_ref[:, 0:128]
    sin_s = tabs_ref[:, 128:256]

    def rotary(val, cos, sin, head_dim):
        tiles = [val[:, c:c + 128] for c in range(0, val.shape[1], 128)]
        return jnp.concatenate([t * cos + _rot_lanes(t, head_dim) * sin for t in tiles], axis=1)

    cqn = _rms(seg("cq"), gq_ref[...]).astype(BF16)
    q_ref[0] = rotary(_dot(cqn, wq_ref[...]), cos_m, sin_m, MLA_ROPE).astype(BF16)

    ckvn = _rms(seg("ckv"), gkv_ref[...]).astype(BF16)
    kr = rotary(seg("kr"), cos_m, sin_m, MLA_ROPE)
    k_ref[0] = (_dot(ckvn, wkn_ref[...]) + jnp.tile(kr, (1, MLA_HEADS))).astype(BF16)
    ones_lane = (lax.broadcasted_iota(I32, (1, MLA_HEADS * MLA_HEAD_PAD), 1) % MLA_HEAD_PAD == MLA_V).astype(F32)
    v_ref[0] = (_dot(ckvn, wv_ref[...]) + ones_lane).astype(BF16)

    u_ref[0] = seg("u")

    qs_ref[0] = rotary(seg("qs"), cos_s, sin_s, SWA_HEAD_DIM).astype(BF16)
    ks = rotary(seg("ks"), cos_s, sin_s, SWA_HEAD_DIM)
    for hk in range(SWA_KV_HEADS):
        k_lo = ks[:, hk * 128:(hk + 1) * 128]
        ks_ref[0, :, (2 * hk) * 128:(2 * hk + 1) * 128] = k_lo.astype(BF16)
        ks_ref[0, :, (2 * hk + 1) * 128:(2 * hk + 2) * 128] = pltpu.roll(k_lo, SWA_HEAD_DIM, 1).astype(BF16)
    ones_s = (lax.broadcasted_iota(I32, (1, SWA_KV_HEADS * 128), 1) % 128 == SWA_HEAD_DIM).astype(F32)
    vs_ref[0] = (seg("vs") + ones_s).astype(BF16)

    g0, _ = _SEG["gl"]
    for p in range(6):
        gate_ref[0, :, p * 512:(p + 1) * 512] = _sigmoid(
            _dot(hb, w1_ref[:, g0 + p * 512:g0 + (p + 1) * 512])).astype(BF16)


def _input_stage(stream, b, modtab, g_mix, w1, g_q, wq, g_kv, wkn, wv, tabm, tabs):
    d = SLAB_ROWS * 128
    p = _stream_tokens(stream, b)
    tm = TOKEN_TILE
    nj = p // tm
    tok = lambda w: pl.BlockSpec((1, tm, w), lambda bi, j: (bi, j, 0))
    full = lambda a: pl.BlockSpec(a.shape, lambda bi, j: (0,) * a.ndim)
    outs = [(1024, BF16), (1024, BF16), (1024, BF16), (512, F32), (512, BF16), (512, BF16), (256, BF16),
            (3072, BF16)]
    return pl.pallas_call(
        functools.partial(_in_kernel, n_stream=len(stream), ctx_tile=nj - 1),
        grid=(b, nj),
        in_specs=_stream_specs(stream, tm, nj) + [
            pl.BlockSpec((1, 1, 8, d), lambda bi, j: (bi, j // (nj - 1), 0, 0)),
            full(g_mix), full(w1), full(g_q), full(wq), full(g_kv), full(wkn), full(wv),
            pl.BlockSpec((tm, 256), lambda bi, j: (j, 0)),
            pl.BlockSpec((tm, 256), lambda bi, j: (j, 0))],
        out_specs=[tok(w) for w, _ in outs],
        out_shape=[jax.ShapeDtypeStruct((b, p, w), dt) for w, dt in outs],
        compiler_params=_cparams(("arbitrary", "arbitrary")),
        name="input_stage",
    )(*stream, modtab, g_mix, w1, g_q, wq, g_kv, wkn, wv, tabm, tabs)


def _mla_kernel(q_ref, k_ref, v_ref, *rest, tk, n_main, tail_rows):
    y_ref, m_ref, acc_ref = rest[-3:]
    hw = MLA_HEAD_PAD
    m_ref[...] = jnp.full(m_ref.shape, MASK_VALUE, F32)
    acc_ref[...] = jnp.zeros(acc_ref.shape, F32)

    def attend(r0, rows):
        for h in range(MLA_HEADS):
            q = q_ref[0, :, h * hw:(h + 1) * hw]
            k = k_ref[0, pl.ds(r0, rows), h * hw:(h + 1) * hw]
            v = v_ref[0, pl.ds(r0, rows), h * hw:(h + 1) * hw]
            s = _dot_nt(q, k)
            m_prev = m_ref[h]
            m_new = jnp.maximum(m_prev, jnp.max(s, axis=1, keepdims=True))
            m_ref[h] = m_new
            pr = jnp.exp2(s - jnp.tile(m_new, (1, rows // hw)))
            acc_ref[h] = jnp.exp2(m_prev - m_new) * acc_ref[h] + _dot(pr.astype(BF16), v)

    if n_main:
        def step(c, carry):
            attend(pl.multiple_of(c * tk, tk), tk)
            return carry

        lax.fori_loop(0, n_main, step, 0)
    if tail_rows:
        attend(n_main * tk, tail_rows)

    first = lax.broadcasted_iota(I32, (q_ref.shape[1], hw), 1) < MLA_V
    for hp in range(MLA_HEADS // 2):
        o = []
        for h in (2 * hp, 2 * hp + 1):
            acc = acc_ref[h]
            o.append(acc / acc[:, MLA_V:MLA_V + 1])
        y_ref[0, :, hp * hw:(hp + 1) * hw] = jnp.where(first, o[0], pltpu.roll(o[1], MLA_V, 1)).astype(BF16)


def _mla_attention(q, k, v, n_ctx):
    b, p, width = q.shape
    s_len = p - n_ctx
    tq = MLA_Q_TILE
    tk = MLA_KEY_CHUNK
    assert s_len % tq == 0 and s_len % tk == 0 and s_len % n_ctx == 0
    out_w = MLA_HEADS * MLA_V
    state = lambda rows: pltpu.VMEM((MLA_HEADS, rows, MLA_HEAD_PAD), F32)
    y_lat = pl.pallas_call(
        functools.partial(_mla_kernel, tk=tk, n_main=p // tk, tail_rows=p % tk),
        grid=(b, s_len // tq),
        in_specs=[pl.BlockSpec((1, tq, width), lambda bi, j: (bi, j, 0)),
                  pl.BlockSpec((1, p, width), lambda bi, j: (bi, 0, 0), pipeline_mode=pl.Buffered(1)),
                  pl.BlockSpec((1, p, width), lambda bi, j: (bi, 0, 0), pipeline_mode=pl.Buffered(1))],
        out_specs=pl.BlockSpec((1, tq, out_w), lambda bi, j: (bi, j, 0)),
        out_shape=jax.ShapeDtypeStruct((b, s_len, out_w), BF16),
        scratch_shapes=[state(tq), state(tq)],
        compiler_params=_cparams(("arbitrary", "arbitrary")),
        name="mla_attention",
    )(q, k, v)
    cblk = s_len // n_ctx
    ctx_rows = lambda w: pl.BlockSpec((1, n_ctx, w), lambda bi: (bi, cblk, 0))
    y_ctx = pl.pallas_call(
        functools.partial(_mla_kernel, tk=tk, n_main=0, tail_rows=n_ctx),
        grid=(b,),
        in_specs=[ctx_rows(width), ctx_rows(width), ctx_rows(width)],
        out_specs=pl.BlockSpec((1, n_ctx, out_w), lambda bi: (bi, 0, 0)),
        out_shape=jax.ShapeDtypeStruct((b, n_ctx, out_w), BF16),
        scratch_shapes=[state(n_ctx), state(n_ctx)],
        compiler_params=_cparams(("arbitrary",)),
        name="mla_attention_ctx",
    )(q, k, v)
    return y_lat, y_ctx


def _swa_kernel(sink_ref, q_ref, k_ref, v_ref, y_ref, *, n_ctx):
    p_len = k_ref.shape[1]
    s_len = p_len - n_ctx
    band = 3 * SWA_BLOCK
    n_lat_tiles = s_len // SWA_BLOCK
    first = lax.broadcasted_iota(I32, (SWA_BLOCK, 128), 1) < SWA_HEAD_DIM
    top = lax.broadcasted_iota(I32, (2 * SWA_BLOCK, 1), 0) < SWA_BLOCK
    n_blocks = q_ref.shape[1] // SWA_BLOCK
    chains = [(t, hk, par) for t in range(n_blocks) for hk in range(SWA_KV_HEADS) for par in range(2)]
    kstart, valid = {}, {}
    for t in range(n_blocks):
        j = pl.program_id(1) * n_blocks + t
        n = jnp.minimum(j, n_lat_tiles - 1)
        ws = jnp.clip((n - 1) * SWA_BLOCK, 0, s_len - band)
        kstart[t] = pl.multiple_of(ws, SWA_BLOCK)
        qpos = n * SWA_BLOCK + lax.broadcasted_iota(I32, (2 * SWA_BLOCK, band), 0) % SWA_BLOCK
        kpos = ws + lax.broadcasted_iota(I32, (2 * SWA_BLOCK, band), 1)
        valid[t] = (jnp.abs(qpos - kpos) <= SWA_WINDOW) & (j < n_lat_tiles)
    scores = {}
    for t, hk, par in chains:
        qrows = slice(t * SWA_BLOCK, (t + 1) * SWA_BLOCK)
        qpair = jnp.concatenate([q_ref[0, qrows, (2 * hk) * 128:(2 * hk + 1) * 128],
                                 q_ref[0, qrows, (2 * hk + 1) * 128:(2 * hk + 2) * 128]], axis=0)
        kcols = slice((2 * hk + par) * 128, (2 * hk + par + 1) * 128)
        s_c = _dot_nt(qpair, k_ref[0, s_len:p_len, kcols])
        s_b = jnp.where(valid[t], _dot_nt(qpair, k_ref[0, pl.ds(kstart[t], band), kcols]), MASK_VALUE)
        scores[t, hk, par] = (s_c, s_b)
    probs = {}
    for t, hk, par in chains:
        s_c, s_b = scores[t, hk, par]
        sink = jnp.where(top, sink_ref[4 * hk + par] * LOG2_E, sink_ref[4 * hk + 2 + par] * LOG2_E)
        m = jnp.maximum(jnp.maximum(jnp.max(s_c, axis=1, keepdims=True), jnp.max(s_b, axis=1, keepdims=True)), sink)
        probs[t, hk, par] = (jnp.exp2(s_c - m).astype(BF16), jnp.exp2(s_b - m).astype(BF16), jnp.exp2(sink - m))
    res = {}
    for t, hk, par in chains:
        p_c, p_b, p_sink = probs[t, hk, par]
        vcols = slice(hk * 128, (hk + 1) * 128)
        o = _dot(p_c, v_ref[0, s_len:p_len, vcols]) + _dot(p_b, v_ref[0, pl.ds(kstart[t], band), vcols])
        o = o / (o[:, SWA_HEAD_DIM:SWA_HEAD_DIM + 1] + p_sink)
        res[t, 4 * hk + par] = o[0:SWA_BLOCK]
        res[t, 4 * hk + 2 + par] = o[SWA_BLOCK:2 * SWA_BLOCK]
    for t in range(n_blocks):
        for pair in range(SWA_Q_HEADS // 2):
            y_ref[0, t * SWA_BLOCK:(t + 1) * SWA_BLOCK, pair * 128:(pair + 1) * 128] = jnp.where(
                first, res[t, 2 * pair], pltpu.roll(res[t, 2 * pair + 1], SWA_HEAD_DIM, 1)).astype(BF16)


def _swa_attention(sink, qs, ks, vs, n_ctx):
    b, p, _ = qs.shape
    tq = SWA_Q_TILE
    assert n_ctx % tq == 0 and (p - n_ctx) % tq == 0
    return pl.pallas_call(
        functools.partial(_swa_kernel, n_ctx=n_ctx),
        grid=(b, p // tq),
        in_specs=[pl.BlockSpec(memory_space=pltpu.SMEM),
                  pl.BlockSpec((1, tq, SWA_WIDTH), lambda bi, j: (bi, j, 0)),
                  pl.BlockSpec((1, p, ks.shape[2]), lambda bi, j: (bi, 0, 0)),
                  pl.BlockSpec((1, p, vs.shape[2]), lambda bi, j: (bi, 0, 0))],
        out_specs=pl.BlockSpec((1, tq, SWA_WIDTH), lambda bi, j: (bi, j, 0)),
        out_shape=jax.ShapeDtypeStruct((b, p, SWA_WIDTH), BF16),
        compiler_params=_cparams(("arbitrary", "arbitrary")),
        name="swa_attention",
    )(sink, qs, ks, vs)


def _pool_kernel(prev_ref, cur_ref, next_ref, w_ref, scale_ref, y_ref, ext_ref, *, n_ctx):
    j = pl.program_id(1)
    tm = cur_ref.shape[1]
    nj = pl.num_programs(1)
    s_len = (nj - 1) * tm
    is_ctx = j == nj - 1
    has_prev = (j >= 1) & (j < nj - 1)
    has_next = j < nj - 2
    ext_ref[0:POOL_HALO, :] = jnp.where(has_prev, prev_ref[0], 0.0)
    ext_ref[POOL_HALO:POOL_HALO + tm, :] = cur_ref[0]
    ext_ref[POOL_HALO + tm:POOL_HALO + tm + POOL_HALO, :] = jnp.where(has_next, next_ref[0], 0.0)
    t = lax.broadcasted_iota(I32, (tm, 1), 0)
    pos = jnp.where(is_ctx, t, j * tm + t)
    seg_len = jnp.where(is_ctx, n_ctx, s_len)
    for g, w in enumerate(POOL_WINDOWS):
        cols = slice(g * POOL_GROUP_DIM, (g + 1) * POOL_GROUP_DIM)
        acc = jnp.zeros((tm, POOL_GROUP_DIM), F32)
        for off in range(-(w // 2), w - w // 2):
            acc = acc + ext_ref[POOL_HALO + off:POOL_HALO + off + tm, cols]
        lo = jnp.maximum(pos - w // 2, 0)
        hi = jnp.minimum(pos + w - w // 2, seg_len)
        cnt = (hi - lo).astype(F32)
        pooled = acc / cnt - cur_ref[0, :, cols]
        y_ref[0, :, cols] = (_dot(pooled.astype(BF16), w_ref[g]) * scale_ref[:, cols]).astype(BF16)


def _pool_stage(u, w_pool, pool_scale, n_ctx):
    b, p, width = u.shape
    tm = TOKEN_TILE
    hb = tm // POOL_HALO
    n_halo_blocks = p // POOL_HALO
    return pl.pallas_call(
        functools.partial(_pool_kernel, n_ctx=n_ctx),
        grid=(b, p // tm),
        in_specs=[pl.BlockSpec((1, POOL_HALO, width), lambda bi, j: (bi, jnp.maximum(j * hb - 1, 0), 0)),
                  pl.BlockSpec((1, tm, width), lambda bi, j: (bi, j, 0)),
                  pl.BlockSpec((1, POOL_HALO, width),
                               lambda bi, j: (bi, jnp.minimum((j + 1) * hb, n_halo_blocks - 1), 0)),
                  pl.BlockSpec(w_pool.shape, lambda bi, j: (0, 0, 0)),
                  pl.BlockSpec(pool_scale.shape, lambda bi, j: (0, 0))],
        out_specs=pl.BlockSpec((1, tm, width), lambda bi, j: (bi, j, 0)),
        out_shape=jax.ShapeDtypeStruct((b, p, width), BF16),
        scratch_shapes=[pltpu.VMEM((tm + 2 * POOL_HALO, width), F32)],
        compiler_params=_cparams(("arbitrary", "arbitrary")),
        name="pool_stage",
    )(u, u, u, w_pool, pool_scale)


def _merge_kernel(*refs, n_stream, ctx_tile):
    (yml_ref, ymc_ref, yp_ref, ys_ref, gate_ref, mod_ref, g_ref, wbm_ref, wbp_ref, wbs_ref,
     wout_ref, wsg_ref, wsu_ref, wsd_ref, wrt_ref, xs_ref, fp_ref, lg_ref) = refs[n_stream:]
    d = SLAB_ROWS * 128
    mod = mod_ref[0, 0]
    gate = gate_ref[0]
    ym = jnp.where(pl.program_id(1) == ctx_tile, ymc_ref[0], yml_ref[0])
    merged = (gate[:, 0:d].astype(F32) * _dot(ym, wbm_ref[...])
              + gate[:, d:2 * d].astype(F32) * _dot(yp_ref[0], wbp_ref[...])
              + gate[:, 2 * d:3 * d].astype(F32) * _dot(ys_ref[0], wbs_ref[...]))
    x = _stream_tile(refs[:n_stream], gate.shape[0], ctx_tile)
    x_mid = x + mod[2:3] * _dot(merged.astype(BF16), wout_ref[...])
    f = _rms(x_mid, g_ref[...]) * (1.0 + mod[4:5]) + mod[3:4]
    fb = f.astype(BF16)
    gsh = _dot(fb, wsg_ref[...])
    shared = _dot((gsh * _sigmoid(gsh) * _dot(fb, wsu_ref[...])).astype(BF16), wsd_ref[...])
    _matrix_to_slab_rows(xs_ref, x_mid + mod[5:6] * shared, SLAB_ROWS)
    _matrix_to_slab_rows(fp_ref, _pack_bf16_pair(f), PACKED_ROWS)
    lg_ref[...] = _dot_nt(wrt_ref[...], fb)


def _merge_stage(stream, ym_lat, ym_ctx, yp, ys, gates, modtab, g_ffn, wbm, wbp, wbs, wout, wsg, wsu, wsd, wrt,
                 keep_ctx):
    b, p, _ = yp.shape
    assert ym_ctx.shape[1] == TOKEN_TILE
    d = SLAB_ROWS * 128
    tm = TOKEN_TILE
    nj = p // tm
    nk = nj if keep_ctx else nj - 1
    tok = lambda w: pl.BlockSpec((1, tm, w), lambda bi, j: (bi, j, 0))
    slab_out = lambda rows: pl.BlockSpec((tm * rows, 128), lambda bi, j: (bi * nk + j, 0))
    full = lambda a: pl.BlockSpec(a.shape, lambda bi, j: (0,) * a.ndim)
    return pl.pallas_call(
        functools.partial(_merge_kernel, n_stream=len(stream), ctx_tile=nj - 1),
        grid=(b, nk),
        in_specs=_stream_specs(stream, tm, nj) + [
            pl.BlockSpec((1, tm, ym_lat.shape[2]), lambda bi, j: (bi, jnp.minimum(j, nj - 2), 0)),
            pl.BlockSpec((1, tm, ym_ctx.shape[2]), lambda bi, j: (bi, 0, 0)),
            tok(yp.shape[2]), tok(ys.shape[2]), tok(gates.shape[2]),
            pl.BlockSpec((1, 1, 8, d), lambda bi, j: (bi, j // (nj - 1), 0, 0)),
            full(g_ffn), full(wbm), full(wbp), full(wbs), full(wout), full(wsg), full(wsu), full(wsd),
            full(wrt)],
        out_specs=[slab_out(SLAB_ROWS), slab_out(PACKED_ROWS),
                   pl.BlockSpec((N_EXPERTS, tm), lambda bi, j: (0, bi * nk + j))],
        out_shape=[jax.ShapeDtypeStruct((b * nk * tm * SLAB_ROWS, 128), F32),
                   jax.ShapeDtypeStruct((b * nk * tm * PACKED_ROWS, 128), U32),
                   jax.ShapeDtypeStruct((N_EXPERTS, b * nk * tm), F32)],
        compiler_params=_cparams(("arbitrary", "arbitrary")),
        name="merge_stage",
    )(*stream, ym_lat, ym_ctx, yp, ys, gates, modtab, g_ffn, wbm, wbp, wbs, wout, wsg, wsu, wsd, wrt)


def _route_kernel(lg_ref, bias_ref, dest_ref, wts_ref, seg_ref):
    tm = lg_ref.shape[1]
    ne = N_EXPERTS
    neg_inf = -jnp.inf
    scores = _sigmoid(lg_ref[...])
    sel = scores + bias_ref[...]
    iota_g = lax.broadcasted_iota(I32, (EXPERTS_PER_GROUP, tm), 0)
    gscore = []
    for g in range(N_GROUPS):
        sg = sel[g * EXPERTS_PER_GROUP:(g + 1) * EXPERTS_PER_GROUP]
        m1 = jnp.max(sg, axis=0, keepdims=True)
        i1 = jnp.min(jnp.where(sg == m1, iota_g, EXPERTS_PER_GROUP), axis=0, keepdims=True)
        m2 = jnp.max(jnp.where(iota_g == i1, neg_inf, sg), axis=0, keepdims=True)
        gscore.append(m1 + m2)
    rows = []
    for g in range(N_GROUPS):
        rank = jnp.zeros((1, tm), I32)
        for g2 in range(N_GROUPS):
            if g2 == g:
                continue
            beats = (gscore[g2] >= gscore[g]) if g2 < g else (gscore[g2] > gscore[g])
            rank = rank + beats.astype(I32)
        rows.append(jnp.where(rank < TOPK_GROUPS, sel[g * EXPERTS_PER_GROUP:(g + 1) * EXPERTS_PER_GROUP], neg_inf))
    cur = jnp.concatenate(rows, axis=0)
    iota_e = lax.broadcasted_iota(I32, (ne, tm), 0)
    picks = []
    member = jnp.zeros((ne, tm), F32)
    for _ in range(TOP_K):
        m = jnp.max(cur, axis=0, keepdims=True)
        idx = jnp.min(jnp.where(cur == m, iota_e, ne), axis=0, keepdims=True)
        hit = iota_e == idx
        picks.append(hit)
        member = member + hit.astype(F32)
        cur = jnp.where(hit, neg_inf, cur)
    earlier = (lax.broadcasted_iota(I32, (tm, tm), 0) < lax.broadcasted_iota(I32, (tm, tm), 1)).astype(BF16)
    pos = _dot(member.astype(BF16), earlier)
    cnt_col = jnp.sum(member, axis=1, keepdims=True)
    blocks_col = jnp.floor((cnt_col + (SEG_ALIGN - 1)) * (1.0 / SEG_ALIGN))
    lower = (lax.broadcasted_iota(I32, (ne, ne), 1) < lax.broadcasted_iota(I32, (ne, ne), 0)).astype(BF16)
    off_col = _dot(lower, jnp.broadcast_to(blocks_col, (ne, 128)).astype(BF16))[:, 0:1] * SEG_ALIGN
    base = off_col + pos
    w_rows = [jnp.sum(jnp.where(hit, scores, 0.0), axis=0, keepdims=True) for hit in picks]
    denom = w_rows[0]
    for w in w_rows[1:]:
        denom = denom + w
    for k, hit in enumerate(picks):
        dest_ref[0, k:k + 1, :] = jnp.sum(jnp.where(hit, base, 0.0), axis=0, keepdims=True).astype(I32)
        wts_ref[0, k:k + 1, :] = w_rows[k] / denom * ROUTED_SCALE
    member_pad = jnp.concatenate([member, jnp.zeros((128 - ne, tm), F32)], axis=0).astype(BF16)
    cnt_row = _dot_nt(jnp.ones((8, tm), BF16), member_pad)
    blocks_row = jnp.floor((cnt_row + (SEG_ALIGN - 1)) * (1.0 / SEG_ALIGN))
    before = (lax.broadcasted_iota(I32, (128, 128), 0) < lax.broadcasted_iota(I32, (128, 128), 1)).astype(BF16)
    off_row = _dot(blocks_row.astype(BF16), before) * SEG_ALIGN
    r = lax.broadcasted_iota(I32, (8, 128), 0)
    seg_ref[0] = jnp.where(r == 0, off_row, jnp.where(r == 1, cnt_row, 0.0)).astype(I32)


def _route_stage(lg_t, bias_col):
    ne, t_all = lg_t.shape
    tm = MOE_TILE
    nt = t_all // tm
    return pl.pallas_call(
        _route_kernel,
        grid=(nt,),
        in_specs=[pl.BlockSpec((ne, tm), lambda i: (0, i)),
                  pl.BlockSpec((ne, 1), lambda i: (0, 0))],
        out_specs=[pl.BlockSpec((1, TOP_K, tm), lambda i: (i, 0, 0)),
                   pl.BlockSpec((1, TOP_K, tm), lambda i: (i, 0, 0)),
                   pl.BlockSpec((1, 8, 128), lambda i: (i, 0, 0))],
        out_shape=[jax.ShapeDtypeStruct((nt, TOP_K, tm), I32),
                   jax.ShapeDtypeStruct((nt, TOP_K, tm), F32),
                   jax.ShapeDtypeStruct((nt, 8, 128), I32)],
        compiler_params=_cparams(("arbitrary",)),
        name="route_stage",
    )(lg_t, bias_col)


def _moe_row_stride(tm):
    cap = TOP_K * tm + N_EXPERTS * SEG_ALIGN + MOE_CHUNK
    blocks = cap // 8 + 1
    return 8 * (blocks + 1 - blocks % 2)


def _moe_kernel(dest_ref, wts_ref, seg_ref, fp_hbm, xs_hbm, gt_ref, gfin_ref, wg_ref, wu_ref, wd_ref, out_hbm,
                bufs_ref, fp_stage, xs_stage, out_stage, slab_tmp, fp_sem, xs_sem, out_sem,
                *, srow, sub, n_sub_per_batch, ctx_sub, n_tiles, final_norm):
    pr = pl.program_id(0)
    eg = pl.program_id(1)
    tm = dest_ref.shape[2]
    ch = MOE_CHUNK
    group = wg_ref.shape[0]
    lane_blk = 128
    n_blk = tm // lane_blk
    per = bufs_ref.shape[0]
    n_slots = fp_stage.shape[0]
    n_here = jnp.minimum(per, n_tiles - pr * per)

    def table_block(ref, half, blk):
        return ref.at[half, :, pl.ds(pl.multiple_of(blk * lane_blk, lane_blk), lane_blk)]

    def token_rows(half, blk, per_tok):
        first = ((pr * per + half) * tm + blk * lane_blk) * per_tok
        return pl.ds(pl.multiple_of(first, lane_blk * per_tok), lane_blk * per_tok)

    def fp_copy(half, blk, slot):
        return pltpu.make_async_copy(fp_hbm.at[token_rows(half, blk, PACKED_ROWS), :], fp_stage.at[slot],
                                     fp_sem.at[slot])

    def xs_copy(half, blk, slot):
        return pltpu.make_async_copy(xs_hbm.at[token_rows(half, blk, SLAB_ROWS), :], xs_stage.at[slot],
                                     xs_sem.at[slot])

    def out_copy(half, blk, slot):
        rows = token_rows(half, blk, 1 if final_norm else SLAB_ROWS)
        return pltpu.make_async_copy(out_stage.at[slot], out_hbm.at[rows, :], out_sem.at[slot])

    def group_rows(half):
        buf_ref = bufs_ref.at[half]
        for blk in range(n_slots - 1):
            fp_copy(half, blk, blk).start()
        buf_ref[...] = jnp.zeros(buf_ref.shape, U32)

        def body(blk, carry):
            slot = blk % n_slots
            fp_copy(half, blk, slot).wait()

            @pl.when(blk + n_slots - 1 < n_blk)
            def _prefetch():
                fp_copy(half, blk + n_slots - 1, (blk + n_slots - 1) % n_slots).start()

            dest = table_block(dest_ref, half, blk)
            rows = fp_stage.at[slot]
            for u in range(lane_blk):
                slab = rows[u * PACKED_ROWS:(u + 1) * PACKED_ROWS, :]
                for k in range(TOP_K):
                    buf_ref[pl.ds(dest[k, u], PACKED_ROWS, stride=srow), :] = slab
            return carry

        lax.fori_loop(0, n_blk, body, 0)

    def experts():
        def one_expert(ge, carry):
            e = eg * group + ge
            counts = [seg_ref[half, 1, e] for half in range(per)]
            offsets = [seg_ref[half, 0, e] for half in range(per)]

            def ffn_rows(halves, rows, chunk_idx):
                words, keep = [], []
                for h in halves:
                    r0 = offsets[h] + chunk_idx * rows
                    words.append([bufs_ref[h, pl.ds(pl.multiple_of(q * srow + r0, 8), rows), :] for q in range(4)])
                    keep.append(lax.broadcasted_iota(I32, (rows, 1), 0) < counts[h] - chunk_idx * rows)
                xb = []
                for ws in words:
                    w = jnp.concatenate(ws, axis=1)
                    xb.append(jnp.concatenate([_unpack_lo(w), _unpack_hi(w)], axis=1).astype(BF16))
                gates = [(_dot(x, wg_ref[ge]), _dot(x, wu_ref[ge])) for x in xb]
                hmid = [(g * _sigmoid(g) * u).astype(BF16) for g, u in gates]
                packed = [_pack_bf16_pair(_dot(hm, wd_ref[ge])) for hm in hmid]
                for n, h in enumerate(halves):
                    r0 = offsets[h] + chunk_idx * rows
                    for q in range(4):
                        bufs_ref[h, pl.ds(pl.multiple_of(q * srow + r0, 8), rows), :] = jnp.where(
                            keep[n], packed[n][:, q * 128:(q + 1) * 128], words[n][q])

            most = counts[0]
            for c in counts[1:]:
                most = jnp.maximum(most, c)
            lo = 0
            for rows in MOE_BODY_ROWS:
                pl.when((most > lo) & (most <= rows))(functools.partial(ffn_rows, tuple(range(per)), rows, 0))
                lo = rows

            @pl.when(most > ch)
            def _long_segments():
                for h in range(per):
                    def chunk(c, carry2, h=h):
                        ffn_rows((h,), ch, c)
                        return carry2

                    lax.fori_loop(0, (counts[h] + ch - 1) // ch, chunk, 0)

            return carry

        lax.fori_loop(0, group, one_expert, 0)

    def combine(half):
        buf_ref = bufs_ref.at[half]
        for blk in range(n_slots - 1):
            xs_copy(half, blk, blk).start()

        def body(blk, carry):
            slot = blk % n_slots
            xs_copy(half, blk, slot).wait()

            @pl.when(blk + n_slots - 1 < n_blk)
            def _prefetch():
                xs_copy(half, blk + n_slots - 1, (blk + n_slots - 1) % n_slots).start()

            @pl.when(blk >= n_slots)
            def _slot_free():
                out_copy(half, blk - n_slots, slot).wait()

            sub_blk = ((pr * per + half) * tm + blk * lane_blk) // sub
            bi = sub_blk // n_sub_per_batch
            gate = gt_ref[bi * 2 + (sub_blk - bi * n_sub_per_batch) // ctx_sub]
            dest = table_block(dest_ref, half, blk)
            wts = table_block(wts_ref, half, blk)
            xs = xs_stage.at[slot]
            out = slab_tmp if final_norm else out_stage.at[slot]
            for u in range(lane_blk):
                acc_lo = jnp.zeros((4, 128), F32)
                acc_hi = jnp.zeros((4, 128), F32)
                for k in range(TOP_K):
                    words = buf_ref[pl.ds(dest[k, u], PACKED_ROWS, stride=srow), :]
                    wk = wts[k, u]
                    acc_lo = acc_lo + wk * _unpack_lo(words)
                    acc_hi = acc_hi + wk * _unpack_hi(words)
                out[u * 8:u * 8 + 4, :] = xs[u * 8:u * 8 + 4, :] + gate[0:4] * acc_lo
                out[u * 8 + 4:u * 8 + 8, :] = xs[u * 8 + 4:u * 8 + 8, :] + gate[4:8] * acc_hi
            if final_norm:
                out_stage[slot] = _rms(_slab_rows_to_matrix(slab_tmp, lane_blk, SLAB_ROWS), gfin_ref[...])
            out_copy(half, blk, slot).start()
            return carry

        lax.fori_loop(0, n_blk, body, 0)
        for blk in range(n_blk - n_slots, n_blk):
            out_copy(half, blk, blk % n_slots).wait()

    def each_tile(fn):
        def body(half, carry):
            fn(half)
            return carry

        lax.fori_loop(0, n_here, body, 0)

    @pl.when(eg == 0)
    def _first_step():
        each_tile(group_rows)
        for half in range(1, per):
            @pl.when(half >= n_here)
            def _clear(half=half):
                bufs_ref[half] = jnp.zeros(bufs_ref.shape[1:], U32)

    experts()
    pl.when(eg == pl.num_programs(1) - 1)(functools.partial(each_tile, combine))


def _moe_stage(dest, wts, seg, fp4, xs8, gt2, g_final, wg, wu, wd, layer, n_sub_per_batch, keep_ctx, final_norm):
    ctx_sub = n_sub_per_batch - 1 if keep_ctx else n_sub_per_batch
    d = SLAB_ROWS * 128
    out_shape = (xs8.shape[0] // SLAB_ROWS, d) if final_norm else xs8.shape
    out_block = (128, d) if final_norm else (128 * SLAB_ROWS, 128)
    nt, _, tm = dest.shape
    ne = wg.shape[1]
    group = MOE_EXPERTS_PER_STEP
    per = MOE_TILES_PER_STEP
    assert ne % group == 0 and tm % 256 == 0 and TOKEN_TILE % 128 == 0 and per == 2
    n_rows = -(-nt // per)
    pad = ((0, n_rows * per - nt), (0, 0), (0, 0))
    dest, wts, seg = jnp.pad(dest, pad), jnp.pad(wts, pad), jnp.pad(seg, pad)
    srow = _moe_row_stride(tm)
    smem = lambda shape: pl.BlockSpec(shape, lambda i, e: (i, 0, 0), memory_space=pltpu.SMEM)
    hbm = pl.BlockSpec(memory_space=pl.ANY)
    slots = MOE_STAGE_SLOTS
    assert tm // 128 >= slots
    stage = lambda per_tok, dt: pltpu.VMEM((slots, 128 * per_tok, 128), dt)
    return pl.pallas_call(
        functools.partial(_moe_kernel, srow=srow, sub=TOKEN_TILE, n_sub_per_batch=n_sub_per_batch, ctx_sub=ctx_sub,
                          n_tiles=nt, final_norm=final_norm),
        grid=(n_rows, ne // group),
        in_specs=[smem((per, TOP_K, tm)), smem((per, TOP_K, tm)), smem((per, 8, 128)),
                  hbm, hbm,
                  pl.BlockSpec(gt2.shape, lambda i, e: (0, 0, 0)),
                  pl.BlockSpec(g_final.shape, lambda i, e: (0, 0)),
                  pl.BlockSpec((None, group) + wg.shape[2:], lambda i, e: (layer, e, 0, 0)),
                  pl.BlockSpec((None, group) + wu.shape[2:], lambda i, e: (layer, e, 0, 0)),
                  pl.BlockSpec((None, group) + wd.shape[2:], lambda i, e: (layer, e, 0, 0))],
        out_specs=hbm,
        out_shape=jax.ShapeDtypeStruct(out_shape, F32),
        scratch_shapes=[pltpu.VMEM((per, 4 * srow, 128), U32),
                        stage(PACKED_ROWS, U32), stage(SLAB_ROWS, F32), pltpu.VMEM((slots,) + out_block, F32),
                        pltpu.VMEM((128 * SLAB_ROWS, 128), F32),
                        pltpu.SemaphoreType.DMA((slots,)), pltpu.SemaphoreType.DMA((slots,)),
                        pltpu.SemaphoreType.DMA((slots,))],
        compiler_params=_cparams(("arbitrary", "arbitrary")),
        name="moe_stage",
    )(dest, wts, seg, fp4, xs8, gt2, g_final, wg, wu, wd)


def _rot_lanes(x, head_dim):
    quarter = head_dim // 4
    lane = lax.broadcasted_iota(I32, x.shape, 1)
    width = x.shape[1]
    return jnp.where(lane % (2 * quarter) < quarter, -pltpu.roll(x, width - quarter, 1), pltpu.roll(x, quarter, 1))


def _w1_kernel(w_ref, o_ref):
    offs = [0]
    for w in (MLA_Q_LORA, MLA_KV_LORA, MLA_ROPE, POOL_WIDTH, SWA_WIDTH, SWA_KV_WIDTH, SWA_KV_WIDTH):
        offs.append(offs[-1] + w)
    o_cq, o_ckv, o_kr, o_u, o_qs, o_ks, o_vs, o_gl = offs
    lane = lax.broadcasted_iota(I32, (w_ref.shape[0], 128), 1)

    def put(name, val):
        a, b = _SEG[name]
        o_ref[:, a:b] = val.astype(BF16)

    def kv_slots(x):
        low = lane < SWA_HEAD_DIM
        return jnp.concatenate([jnp.where(low, x, 0.0), jnp.where(low, pltpu.roll(x, SWA_HEAD_DIM, 1), 0.0)], axis=1)

    def kr_slot(x):
        return jnp.where((lane >= MLA_NOPE) & (lane < MLA_NOPE + MLA_ROPE), pltpu.roll(x, MLA_NOPE, 1), 0.0)

    put("cq", w_ref[:, o_cq:o_ckv])
    put("ckv", w_ref[:, o_ckv:o_kr])
    put("u", w_ref[:, o_u:o_qs])
    put("qs", w_ref[:, o_qs:o_ks] * (SWA_SCALE * LOG2_E))
    put("ks", kv_slots(w_ref[:, o_ks:o_vs]))
    put("vs", kv_slots(w_ref[:, o_vs:o_gl]))
    put("kr", kr_slot(w_ref[:, o_kr:o_kr + 128]))
    put("gl", w_ref[:, o_gl:w_ref.shape[1]])


def _fused_in_weight(w_in, layer):
    _, d, width = w_in.shape
    tr = 256
    assert d % tr == 0 and SWA_KV_WIDTH == 128 and MLA_KV_LORA + MLA_Q_LORA == 5 * 128
    return pl.pallas_call(
        _w1_kernel,
        grid=(d // tr,),
        in_specs=[pl.BlockSpec((None, tr, width), lambda i: (layer, i, 0))],
        out_specs=pl.BlockSpec((tr, FUSED_IN_WIDTH), lambda i: (i, 0)),
        out_shape=jax.ShapeDtypeStruct((d, FUSED_IN_WIDTH), BF16),
        compiler_params=_cparams(("arbitrary",)),
        name="fused_in_weight",
    )(w_in)


def _rope_pattern(s_len, n_ctx, rot_dim):
    t = jnp.arange(s_len)
    row = (t // GRID_W).astype(F32)
    col = (t % GRID_W).astype(F32)
    n_freq = rot_dim // 4
    inv_freq = ROPE_BASE ** (-jnp.arange(n_freq, dtype=F32) / n_freq)
    ang_r = row[:, None] * inv_freq[None, :]
    ang_c = col[:, None] * inv_freq[None, :]
    cos = jnp.concatenate([jnp.cos(ang_r), jnp.cos(ang_r), jnp.cos(ang_c), jnp.cos(ang_c)], axis=1)
    sin = jnp.concatenate([jnp.sin(ang_r), jnp.sin(ang_r), jnp.sin(ang_c), jnp.sin(ang_c)], axis=1)
    cos = jnp.concatenate([cos, jnp.ones((n_ctx, rot_dim), F32)], axis=0)
    sin = jnp.concatenate([sin, jnp.zeros((n_ctx, rot_dim), F32)], axis=0)
    return cos, sin


def _layer_weights(w_in, layer, w_uq, w_ukv):
    w1 = _fused_in_weight(w_in, layer)
    lq = w_uq.shape[0]
    wq3 = (w_uq * (MLA_SCALE * LOG2_E)).reshape(lq, MLA_HEADS, MLA_NOPE + MLA_ROPE)
    pad = jnp.zeros((lq, MLA_HEADS, MLA_HEAD_PAD - MLA_NOPE - MLA_ROPE), F32)
    wq = jnp.concatenate([wq3, pad], axis=2).reshape(lq, MLA_HEADS * MLA_HEAD_PAD).astype(BF16)
    lkv = w_ukv.shape[0]
    wkv3 = w_ukv.reshape(lkv, MLA_HEADS, MLA_NOPE + MLA_V)
    wkn = jnp.concatenate([wkv3[:, :, :MLA_NOPE], jnp.zeros((lkv, MLA_HEADS, MLA_HEAD_PAD - MLA_NOPE), F32)],
                          axis=2).reshape(lkv, MLA_HEADS * MLA_HEAD_PAD).astype(BF16)
    wv = jnp.concatenate([wkv3[:, :, MLA_NOPE:], jnp.zeros((lkv, MLA_HEADS, MLA_HEAD_PAD - MLA_V), F32)],
                         axis=2).reshape(lkv, MLA_HEADS * MLA_HEAD_PAD).astype(BF16)
    return w1, wq, wkn, wv


def kernel(x, c, ctx, c_ctx, w_mod, b_mod, g_mix, g_ffn, w_in, g_mla_q, g_mla_kv, w_mla_uq, w_mla_ukv, w_pool,
           pool_scale, swa_sink, w_br_mla, w_br_pool, w_br_swa, w_out, w_router, router_bias, w_exp_gate,
           w_exp_up, w_exp_down, w_sh_gate, w_sh_up, w_sh_down, g_final):
    b, s_len, d = x.shape
    n_ctx = ctx.shape[1]
    n_layers = w_mod.shape[0]
    p = n_ctx + s_len
    assert n_ctx == TOKEN_TILE and s_len % TOKEN_TILE == 0 and (b * p) % MOE_TILE == 0 and b + 1 <= 8
    assert d == 1024 and w_in.shape[2] == (MLA_Q_LORA + MLA_KV_LORA + MLA_ROPE + POOL_WIDTH + SWA_WIDTH
                                           + 2 * SWA_KV_WIDTH + 3 * d)

    cm, sm = _rope_pattern(s_len, n_ctx, MLA_ROPE)
    tail = jnp.zeros((p, MLA_HEAD_PAD - MLA_NOPE - MLA_ROPE), F32)
    tabm = jnp.concatenate([jnp.ones((p, MLA_NOPE), F32), cm, tail, jnp.zeros((p, MLA_NOPE), F32), sm, tail], axis=1)
    cs, ss = _rope_pattern(s_len, n_ctx, SWA_HEAD_DIM)
    tabs = jnp.concatenate([cs, cs, ss, ss], axis=1)

    cvec = jnp.concatenate([c, c_ctx[None, :], jnp.zeros((8 - b - 1, d), F32)], axis=0)
    mod_all = _modulation(cvec, w_mod, b_mod).reshape(n_layers, 8, N_MOD, d)

    stream = (x, ctx)
    wg_all, wu_all, wd_all = w_exp_gate.astype(BF16), w_exp_up.astype(BF16), w_exp_down.astype(BF16)
    for i in range(n_layers):
        keep_ctx = i < n_layers - 1
        lat = mod_all[i, :b]
        ctx_rows = jnp.broadcast_to(mod_all[i, b][None], (b, N_MOD, d))
        modtab = jnp.pad(jnp.stack([lat, ctx_rows], axis=1), ((0, 0), (0, 0), (0, 8 - N_MOD), (0, 0)))
        w1, wq, wkn, wv = _layer_weights(w_in, i, w_mla_uq[i], w_mla_ukv[i])
        q, k, v, u, qs, ks, vs, gates = _input_stage(
            stream, b, modtab, g_mix[i][None], w1, g_mla_q[i][None], wq, g_mla_kv[i][None], wkn, wv, tabm, tabs)
        y_mla, y_mla_ctx = _mla_attention(q, k, v, n_ctx)
        y_swa = _swa_attention(swa_sink[i], qs, ks, vs, n_ctx)
        y_pool = _pool_stage(u, w_pool[i].astype(BF16), pool_scale[i][None], n_ctx)
        xs, fp, lg_t = _merge_stage(
            stream, y_mla, y_mla_ctx, y_pool, y_swa, gates, modtab, g_ffn[i][None], w_br_mla[i].astype(BF16),
            w_br_pool[i].astype(BF16), w_br_swa[i].astype(BF16), w_out[i].astype(BF16),
            w_sh_gate[i].astype(BF16), w_sh_up[i].astype(BF16), w_sh_down[i].astype(BF16),
            w_router[i].T.astype(BF16), keep_ctx)
        dest, wts, seg = _route_stage(lg_t, router_bias[i][:, None])
        gt2 = modtab[:, :, 5, :].reshape(b * 2, 8, d // 8)
        stream = (_moe_stage(dest, wts, seg, fp, xs, gt2, g_final[None], wg_all, wu_all, wd_all, i,
                             (p if keep_ctx else s_len) // TOKEN_TILE, keep_ctx, final_norm=not keep_ctx),)
    return stream[0].reshape(b, s_len, d)
```

```python
import functools

import jax
import jax.numpy as jnp
from jax import lax
from jax.experimental import pallas as pl
from jax.experimental.pallas import tpu as pltpu

F32 = jnp.float32
BF16 = jnp.bfloat16
U32 = jnp.uint32
I32 = jnp.int32

NORM_EPS = 1e-6
ROPE_BASE = 10000.0
GRID_W = 64
N_MOD = 6

MLA_HEADS = 8
MLA_Q_LORA = 384
MLA_KV_LORA = 256
MLA_NOPE = 64
MLA_ROPE = 32
MLA_V = 64
MLA_SCALE = (MLA_NOPE + MLA_ROPE) ** -0.5
MLA_HEAD_PAD = 128
LOG2_E = 1.4426950408889634

POOL_WINDOWS = (2, 4, 8, 16)
POOL_GROUP_DIM = 128
POOL_WIDTH = 512
POOL_HALO = 8

SWA_Q_HEADS = 8
SWA_KV_HEADS = 2
SWA_HEAD_DIM = 64
SWA_WINDOW = 128
SWA_BLOCK = 128
SWA_SCALE = SWA_HEAD_DIM ** -0.5
SWA_WIDTH = SWA_Q_HEADS * SWA_HEAD_DIM
SWA_KV_WIDTH = SWA_KV_HEADS * SWA_HEAD_DIM

N_EXPERTS = 64
TOP_K = 8
N_GROUPS = 8
TOPK_GROUPS = 4
EXPERTS_PER_GROUP = 8
D_EXPERT = 256
ROUTED_SCALE = 2.5

TOKEN_TILE = 256
SWA_Q_TILE = 256
MLA_Q_TILE = 1024
MLA_KEY_CHUNK = 256
MOE_TILE = 1024
MOE_CHUNK = 256
MOE_BODY_ROWS = (128, 160, 192, 224, 256)
MOE_EXPERTS_PER_STEP = 2
MOE_TILES_PER_STEP = 2
MOE_STAGE_SLOTS = 4
SEG_ALIGN = 8
SLAB_ROWS = 8
PACKED_ROWS = 4
MASK_VALUE = -1e30
HI16 = 0xFFFF0000

VMEM_LIMIT = 56 * 1024 * 1024

_SEG_WIDTHS = (("cq", 384), ("ckv", 256), ("u", 512), ("qs", 512), ("ks", 256), ("vs", 256), ("kr", 128),
               ("gl", 3072))
_SEG = {}
_o = 0
for _n, _w in _SEG_WIDTHS:
    _SEG[_n] = (_o, _o + _w)
    _o += _w
FUSED_IN_WIDTH = _o


def _cparams(sem):
    return pltpu.CompilerParams(dimension_semantics=sem, vmem_limit_bytes=VMEM_LIMIT)


def _dot(a, b):
    return jnp.dot(a, b, preferred_element_type=F32)


def _dot_nt(a, b):
    return lax.dot_general(a, b, (((1,), (1,)), ((), ())), preferred_element_type=F32)


def _sigmoid(x):
    return 1.0 / (1.0 + jnp.exp(-x))


def _rms(x, g):
    return x * lax.rsqrt(jnp.mean(x * x, axis=-1, keepdims=True) + NORM_EPS) * g


def _pack_bf16_pair(v):
    n = v.shape[1] // 2
    bits = pltpu.bitcast(v.astype(BF16).astype(F32), U32)
    return (bits[:, :n] >> 16) | (bits[:, n:] & jnp.uint32(HI16))


def _slab_rows_to_matrix(ref, n_tok, per_tok):
    return jnp.concatenate([ref[pl.ds(c, n_tok, stride=per_tok), :] for c in range(per_tok)], axis=1)


def _matrix_to_slab_rows(ref, val, per_tok):
    n_tok = val.shape[0]
    for c in range(per_tok):
        ref[pl.ds(c, n_tok, stride=per_tok), :] = val[:, c * 128:(c + 1) * 128]


def _unpack_lo(w):
    return pltpu.bitcast(w << 16, F32)


def _unpack_hi(w):
    return pltpu.bitcast(w & jnp.uint32(HI16), F32)


def _mod_kernel(c_ref, w_ref, b_ref, o_ref):
    c = c_ref[...]
    a = (c * _sigmoid(c)).astype(BF16)
    o_ref[0] = _dot(a, w_ref[0].astype(BF16)) + b_ref[0]


def _modulation(cvec, w_mod, b_mod):
    n_layers, d, width = w_mod.shape
    tn = width // 4
    return pl.pallas_call(
        _mod_kernel,
        grid=(n_layers, width // tn),
        in_specs=[pl.BlockSpec((8, d), lambda l, n: (0, 0)),
                  pl.BlockSpec((1, d, tn), lambda l, n: (l, 0, n)),
                  pl.BlockSpec((1, 1, tn), lambda l, n: (l, 0, n))],
        out_specs=pl.BlockSpec((1, 8, tn), lambda l, n: (l, 0, n)),
        out_shape=jax.ShapeDtypeStruct((n_layers, 8, width), F32),
        compiler_params=_cparams(("arbitrary", "arbitrary")),
        name="modulation",
    )(cvec, w_mod, b_mod.reshape(n_layers, 1, width))


def _stream_specs(stream, tm, nj):
    if len(stream) == 1:
        return [pl.BlockSpec((tm * SLAB_ROWS, 128), lambda bi, j: (bi * nj + j, 0))]
    d = stream[0].shape[2]
    return [pl.BlockSpec((1, tm, d), lambda bi, j: (bi, jnp.minimum(j, nj - 2), 0)),
            pl.BlockSpec((1, tm, d), lambda bi, j: (bi, 0, 0))]


def _stream_tile(refs, n_tok, ctx_tile):
    if len(refs) == 1:
        return _slab_rows_to_matrix(refs[0], n_tok, SLAB_ROWS)
    return jnp.where(pl.program_id(1) == ctx_tile, refs[1][0], refs[0][0])


def _stream_tokens(stream, b):
    return stream[0].shape[0] // (b * SLAB_ROWS) if len(stream) == 1 else stream[0].shape[1] + stream[1].shape[1]


def _in_kernel(*refs, n_stream, ctx_tile):
    (mod_ref, g_ref, w1_ref, gq_ref, wq_ref, gkv_ref, wkn_ref, wv_ref, tabm_ref, tabs_ref,
     q_ref, k_ref, v_ref, u_ref, qs_ref, ks_ref, vs_ref, gate_ref) = refs[n_stream:]
    x = _stream_tile(refs[:n_stream], q_ref.shape[1], ctx_tile)
    mod = mod_ref[0, 0]
    h = _rms(x, g_ref[...]) * (1.0 + mod[1:2]) + mod[0:1]
    hb = h.astype(BF16)

    def seg(name):
        a, b = _SEG[name]
        return _dot(hb, w1_ref[:, a:b])

    cos_m = tabm_ref[:, 0:128]
    sin_m = tabm_ref[:, 128:256]
    cos_s = tabs_ref[:, 0:128]
    sin_s = tabs_ref[:, 128:256]

    def rotary(val, cos, sin, head_dim):
        tiles = [val[:, c:c + 128] for c in range(0, val.shape[1], 128)]
        return jnp.concatenate([t * cos + _rot_lanes(t, head_dim) * sin for t in tiles], axis=1)

    cqn = _rms(seg("cq"), gq_ref[...]).astype(BF16)
    q_ref[0] = rotary(_dot(cqn, wq_ref[...]), cos_m, sin_m, MLA_ROPE).astype(BF16)

    ckvn = _rms(seg("ckv"), gkv_ref[...]).astype(BF16)
    kr = rotary(seg("kr"), cos_m, sin_m, MLA_ROPE)
    k_ref[0] = (_dot(ckvn, wkn_ref[...]) + jnp.tile(kr, (1, MLA_HEADS))).astype(BF16)
    ones_lane = (lax.broadcasted_iota(I32, (1, MLA_HEADS * MLA_HEAD_PAD), 1) % MLA_HEAD_PAD == MLA_V).astype(F32)
    v_ref[0] = (_dot(ckvn, wv_ref[...]) + ones_lane).astype(BF16)

    u_ref[0] = seg("u")

    qs_ref[0] = rotary(seg("qs"), cos_s, sin_s, SWA_HEAD_DIM).astype(BF16)
    ks = rotary(seg("ks"), cos_s, sin_s, SWA_HEAD_DIM)
    for hk in range(SWA_KV_HEADS):
        k_lo = ks[:, hk * 128:(hk + 1) * 128]
        ks_ref[0, :, (2 * hk) * 128:(2 * hk + 1) * 128] = k_lo.astype(BF16)
        ks_ref[0, :, (2 * hk + 1) * 128:(2 * hk + 2) * 128] = pltpu.roll(k_lo, SWA_HEAD_DIM, 1).astype(BF16)
    ones_s = (lax.broadcasted_iota(I32, (1, SWA_KV_HEADS * 128), 1) % 128 == SWA_HEAD_DIM).astype(F32)
    vs_ref[0] = (seg("vs") + ones_s).astype(BF16)

    g0, _ = _SEG["gl"]
    for p in range(6):
        gate_ref[0, :, p * 512:(p + 1) * 512] = _sigmoid(
            _dot(hb, w1_ref[:, g0 + p * 512:g0 + (p + 1) * 512])).astype(BF16)


def _input_stage(stream, b, modtab, g_mix, w1, g_q, wq, g_kv, wkn, wv, tabm, tabs):
    d = SLAB_ROWS * 128
    p = _stream_tokens(stream, b)
    tm = TOKEN_TILE
    nj = p // tm
    tok = lambda w: pl.BlockSpec((1, tm, w), lambda bi, j: (bi, j, 0))
    full = lambda a: pl.BlockSpec(a.shape, lambda bi, j: (0,) * a.ndim)
    outs = [(1024, BF16), (1024, BF16), (1024, BF16), (512, F32), (512, BF16), (512, BF16), (256, BF16),
            (3072, BF16)]
    return pl.pallas_call(
        functools.partial(_in_kernel, n_stream=len(stream), ctx_tile=nj - 1),
        grid=(b, nj),
        in_specs=_stream_specs(stream, tm, nj) + [
            pl.BlockSpec((1, 1, 8, d), lambda bi, j: (bi, j // (nj - 1), 0, 0)),
            full(g_mix), full(w1), full(g_q), full(wq), full(g_kv), full(wkn), full(wv),
            pl.BlockSpec((tm, 256), lambda bi, j: (j, 0)),
            pl.BlockSpec((tm, 256), lambda bi, j: (j, 0))],
        out_specs=[tok(w) for w, _ in outs],
        out_shape=[jax.ShapeDtypeStruct((b, p, w), dt) for w, dt in outs],
        compiler_params=_cparams(("arbitrary", "arbitrary")),
        name="input_stage",
    )(*stream, modtab, g_mix, w1, g_q, wq, g_kv, wkn, wv, tabm, tabs)


def _mla_kernel(q_ref, k_ref, v_ref, *rest, tk, n_main, tail_rows):
    y_ref, m_ref, acc_ref = rest[-3:]
    hw = MLA_HEAD_PAD
    m_ref[...] = jnp.full(m_ref.shape, MASK_VALUE, F32)
    acc_ref[...] = jnp.zeros(acc_ref.shape, F32)

    def attend(r0, rows):
        for h in range(MLA_HEADS):
            q = q_ref[0, :, h * hw:(h + 1) * hw]
            k = k_ref[0, pl.ds(r0, rows), h * hw:(h + 1) * hw]
            v = v_ref[0, pl.ds(r0, rows), h * hw:(h + 1) * hw]
            s = _dot_nt(q, k)
            m_prev = m_ref[h]
            m_new = jnp.maximum(m_prev, jnp.max(s, axis=1, keepdims=True))
            m_ref[h] = m_new
            pr = jnp.exp2(s - jnp.tile(m_new, (1, rows // hw)))
            acc_ref[h] = jnp.exp2(m_prev - m_new) * acc_ref[h] + _dot(pr.astype(BF16), v)

    if n_main:
        def step(c, carry):
            attend(pl.multiple_of(c * tk, tk), tk)
            return carry

        lax.fori_loop(0, n_main, step, 0)
    if tail_rows:
        attend(n_main * tk, tail_rows)

    first = lax.broadcasted_iota(I32, (q_ref.shape[1], hw), 1) < MLA_V
    for hp in range(MLA_HEADS // 2):
        o = []
        for h in (2 * hp, 2 * hp + 1):
            acc = acc_ref[h]
            o.append(acc / acc[:, MLA_V:MLA_V + 1])
        y_ref[0, :, hp * hw:(hp + 1) * hw] = jnp.where(first, o[0], pltpu.roll(o[1], MLA_V, 1)).astype(BF16)


def _mla_attention(q, k, v, n_ctx):
    b, p, width = q.shape
    s_len = p - n_ctx
    tq = MLA_Q_TILE
    tk = MLA_KEY_CHUNK
    assert s_len % tq == 0 and s_len % tk == 0 and s_len % n_ctx == 0
    out_w = MLA_HEADS * MLA_V
    state = lambda rows: pltpu.VMEM((MLA_HEADS, rows, MLA_HEAD_PAD), F32)
    y_lat = pl.pallas_call(
        functools.partial(_mla_kernel, tk=tk, n_main=p // tk, tail_rows=p % tk),
        grid=(b, s_len // tq),
        in_specs=[pl.BlockSpec((1, tq, width), lambda bi, j: (bi, j, 0)),
                  pl.BlockSpec((1, p, width), lambda bi, j: (bi, 0, 0), pipeline_mode=pl.Buffered(1)),
                  pl.BlockSpec((1, p, width), lambda bi, j: (bi, 0, 0), pipeline_mode=pl.Buffered(1))],
        out_specs=pl.BlockSpec((1, tq, out_w), lambda bi, j: (bi, j, 0)),
        out_shape=jax.ShapeDtypeStruct((b, s_len, out_w), BF16),
        scratch_shapes=[state(tq), state(tq)],
        compiler_params=_cparams(("arbitrary", "arbitrary")),
        name="mla_attention",
    )(q, k, v)
    cblk = s_len // n_ctx
    ctx_rows = lambda w: pl.BlockSpec((1, n_ctx, w), lambda bi: (bi, cblk, 0))
    y_ctx = pl.pallas_call(
        functools.partial(_mla_kernel, tk=tk, n_main=0, tail_rows=n_ctx),
        grid=(b,),
        in_specs=[ctx_rows(width), ctx_rows(width), ctx_rows(width)],
        out_specs=pl.BlockSpec((1, n_ctx, out_w), lambda bi: (bi, 0, 0)),
        out_shape=jax.ShapeDtypeStruct((b, n_ctx, out_w), BF16),
        scratch_shapes=[state(n_ctx), state(n_ctx)],
        compiler_params=_cparams(("arbitrary",)),
        name="mla_attention_ctx",
    )(q, k, v)
    return y_lat, y_ctx


def _swa_kernel(sink_ref, q_ref, k_ref, v_ref, y_ref, *, n_ctx):
    p_len = k_ref.shape[1]
    s_len = p_len - n_ctx
    band = 3 * SWA_BLOCK
    n_lat_tiles = s_len // SWA_BLOCK
    first = lax.broadcasted_iota(I32, (SWA_BLOCK, 128), 1) < SWA_HEAD_DIM
    top = lax.broadcasted_iota(I32, (2 * SWA_BLOCK, 1), 0) < SWA_BLOCK
    n_blocks = q_ref.shape[1] // SWA_BLOCK
    chains = [(t, hk, par) for t in range(n_blocks) for hk in range(SWA_KV_HEADS) for par in range(2)]
    kstart, valid = {}, {}
    for t in range(n_blocks):
        j = pl.program_id(1) * n_blocks + t
        n = jnp.minimum(j, n_lat_tiles - 1)
        ws = jnp.clip((n - 1) * SWA_BLOCK, 0, s_len - band)
        kstart[t] = pl.multiple_of(ws, SWA_BLOCK)
        qpos = n * SWA_BLOCK + lax.broadcasted_iota(I32, (2 * SWA_BLOCK, band), 0) % SWA_BLOCK
        kpos = ws + lax.broadcasted_iota(I32, (2 * SWA_BLOCK, band), 1)
        valid[t] = (jnp.abs(qpos - kpos) <= SWA_WINDOW) & (j < n_lat_tiles)
    scores = {}
    for t, hk, par in chains:
        qrows = slice(t * SWA_BLOCK, (t + 1) * SWA_BLOCK)
        qpair = jnp.concatenate([q_ref[0, qrows, (2 * hk) * 128:(2 * hk + 1) * 128],
                                 q_ref[0, qrows, (2 * hk + 1) * 128:(2 * hk + 2) * 128]], axis=0)
        kcols = slice((2 * hk + par) * 128, (2 * hk + par + 1) * 128)
        s_c = _dot_nt(qpair, k_ref[0, s_len:p_len, kcols])
        s_b = jnp.where(valid[t], _dot_nt(qpair, k_ref[0, pl.ds(kstart[t], band), kcols]), MASK_VALUE)
        scores[t, hk, par] = (s_c, s_b)
    probs = {}
    for t, hk, par in chains:
        s_c, s_b = scores[t, hk, par]
        sink = jnp.where(top, sink_ref[4 * hk + par] * LOG2_E, sink_ref[4 * hk + 2 + par] * LOG2_E)
        m = jnp.maximum(jnp.maximum(jnp.max(s_c, axis=1, keepdims=True), jnp.max(s_b, axis=1, keepdims=True)), sink)
        probs[t, hk, par] = (jnp.exp2(s_c - m).astype(BF16), jnp.exp2(s_b - m).astype(BF16), jnp.exp2(sink - m))
    res = {}
    for t, hk, par in chains:
        p_c, p_b, p_sink = probs[t, hk, par]
        vcols = slice(hk * 128, (hk + 1) * 128)
        o = _dot(p_c, v_ref[0, s_len:p_len, vcols]) + _dot(p_b, v_ref[0, pl.ds(kstart[t], band), vcols])
        o = o / (o[:, SWA_HEAD_DIM:SWA_HEAD_DIM + 1] + p_sink)
        res[t, 4 * hk + par] = o[0:SWA_BLOCK]
        res[t, 4 * hk + 2 + par] = o[SWA_BLOCK:2 * SWA_BLOCK]
    for t in range(n_blocks):
        for pair in range(SWA_Q_HEADS // 2):
            y_ref[0, t * SWA_BLOCK:(t + 1) * SWA_BLOCK, pair * 128:(pair + 1) * 128] = jnp.where(
                first, res[t, 2 * pair], pltpu.roll(res[t, 2 * pair + 1], SWA_HEAD_DIM, 1)).astype(BF16)


def _swa_attention(sink, qs, ks, vs, n_ctx):
    b, p, _ = qs.shape
    tq = SWA_Q_TILE
    assert n_ctx % tq == 0 and (p - n_ctx) % tq == 0
    return pl.pallas_call(
        functools.partial(_swa_kernel, n_ctx=n_ctx),
        grid=(b, p // tq),
        in_specs=[pl.BlockSpec(memory_space=pltpu.SMEM),
                  pl.BlockSpec((1, tq, SWA_WIDTH), lambda bi, j: (bi, j, 0)),
                  pl.BlockSpec((1, p, ks.shape[2]), lambda bi, j: (bi, 0, 0)),
                  pl.BlockSpec((1, p, vs.shape[2]), lambda bi, j: (bi, 0, 0))],
        out_specs=pl.BlockSpec((1, tq, SWA_WIDTH), lambda bi, j: (bi, j, 0)),
        out_shape=jax.ShapeDtypeStruct((b, p, SWA_WIDTH), BF16),
        compiler_params=_cparams(("arbitrary", "arbitrary")),
        name="swa_attention",
    )(sink, qs, ks, vs)


def _pool_kernel(prev_ref, cur_ref, next_ref, w_ref, scale_ref, y_ref, ext_ref, *, n_ctx):
    j = pl.program_id(1)
    tm = cur_ref.shape[1]
    nj = pl.num_programs(1)
    s_len = (nj - 1) * tm
    is_ctx = j == nj - 1
    has_prev = (j >= 1) & (j < nj - 1)
    has_next = j < nj - 2
    ext_ref[0:POOL_HALO, :] = jnp.where(has_prev, prev_ref[0], 0.0)
    ext_ref[POOL_HALO:POOL_HALO + tm, :] = cur_ref[0]
    ext_ref[POOL_HALO + tm:POOL_HALO + tm + POOL_HALO, :] = jnp.where(has_next, next_ref[0], 0.0)
    t = lax.broadcasted_iota(I32, (tm, 1), 0)
    pos = jnp.where(is_ctx, t, j * tm + t)
    seg_len = jnp.where(is_ctx, n_ctx, s_len)
    for g, w in enumerate(POOL_WINDOWS):
        cols = slice(g * POOL_GROUP_DIM, (g + 1) * POOL_GROUP_DIM)
        acc = jnp.zeros((tm, POOL_GROUP_DIM), F32)
        for off in range(-(w // 2), w - w // 2):
            acc = acc + ext_ref[POOL_HALO + off:POOL_HALO + off + tm, cols]
        lo = jnp.maximum(pos - w // 2, 0)
        hi = jnp.minimum(pos + w - w // 2, seg_len)
        cnt = (hi - lo).astype(F32)
        pooled = acc / cnt - cur_ref[0, :, cols]
        y_ref[0, :, cols] = (_dot(pooled.astype(BF16), w_ref[g]) * scale_ref[:, cols]).astype(BF16)


def _pool_stage(u, w_pool, pool_scale, n_ctx):
    b, p, width = u.shape
    tm = TOKEN_TILE
    hb = tm // POOL_HALO
    n_halo_blocks = p // POOL_HALO
    return pl.pallas_call(
        functools.partial(_pool_kernel, n_ctx=n_ctx),
        grid=(b, p // tm),
        in_specs=[pl.BlockSpec((1, POOL_HALO, width), lambda bi, j: (bi, jnp.maximum(j * hb - 1, 0), 0)),
                  pl.BlockSpec((1, tm, width), lambda bi, j: (bi, j, 0)),
                  pl.BlockSpec((1, POOL_HALO, width),
                               lambda bi, j: (bi, jnp.minimum((j + 1) * hb, n_halo_blocks - 1), 0)),
                  pl.BlockSpec(w_pool.shape, lambda bi, j: (0, 0, 0)),
                  pl.BlockSpec(pool_scale.shape, lambda bi, j: (0, 0))],
        out_specs=pl.BlockSpec((1, tm, width), lambda bi, j: (bi, j, 0)),
        out_shape=jax.ShapeDtypeStruct((b, p, width), BF16),
        scratch_shapes=[pltpu.VMEM((tm + 2 * POOL_HALO, width), F32)],
        compiler_params=_cparams(("arbitrary", "arbitrary")),
        name="pool_stage",
    )(u, u, u, w_pool, pool_scale)


def _merge_kernel(*refs, n_stream, ctx_tile):
    (yml_ref, ymc_ref, yp_ref, ys_ref, gate_ref, mod_ref, g_ref, wbm_ref, wbp_ref, wbs_ref,
     wout_ref, wsg_ref, wsu_ref, wsd_ref, wrt_ref, xs_ref, fp_ref, lg_ref) = refs[n_stream:]
    d = SLAB_ROWS * 128
    mod = mod_ref[0, 0]
    gate = gate_ref[0]
    ym = jnp.where(pl.program_id(1) == ctx_tile, ymc_ref[0], yml_ref[0])
    merged = (gate[:, 0:d].astype(F32) * _dot(ym, wbm_ref[...])
              + gate[:, d:2 * d].astype(F32) * _dot(yp_ref[0], wbp_ref[...])
              + gate[:, 2 * d:3 * d].astype(F32) * _dot(ys_ref[0], wbs_ref[...]))
    x = _stream_tile(refs[:n_stream], gate.shape[0], ctx_tile)
    x_mid = x + mod[2:3] * _dot(merged.astype(BF16), wout_ref[...])
    f = _rms(x_mid, g_ref[...]) * (1.0 + mod[4:5]) + mod[3:4]
    fb = f.astype(BF16)
    gsh = _dot(fb, wsg_ref[...])
    shared = _dot((gsh * _sigmoid(gsh) * _dot(fb, wsu_ref[...])).astype(BF16), wsd_ref[...])
    _matrix_to_slab_rows(xs_ref, x_mid + mod[5:6] * shared, SLAB_ROWS)
    _matrix_to_slab_rows(fp_ref, _pack_bf16_pair(f), PACKED_ROWS)
    lg_ref[...] = _dot_nt(wrt_ref[...], fb)


def _merge_stage(stream, ym_lat, ym_ctx, yp, ys, gates, modtab, g_ffn, wbm, wbp, wbs, wout, wsg, wsu, wsd, wrt,
                 keep_ctx):
    b, p, _ = yp.shape
    assert ym_ctx.shape[1] == TOKEN_TILE
    d = SLAB_ROWS * 128
    tm = TOKEN_TILE
    nj = p // tm
    nk = nj if keep_ctx else nj - 1
    tok = lambda w: pl.BlockSpec((1, tm, w), lambda bi, j: (bi, j, 0))
    slab_out = lambda rows: pl.BlockSpec((tm * rows, 128), lambda bi, j: (bi * nk + j, 0))
    full = lambda a: pl.BlockSpec(a.shape, lambda bi, j: (0,) * a.ndim)
    return pl.pallas_call(
        functools.partial(_merge_kernel, n_stream=len(stream), ctx_tile=nj - 1),
        grid=(b, nk),
        in_specs=_stream_specs(stream, tm, nj) + [
            pl.BlockSpec((1, tm, ym_lat.shape[2]), lambda bi, j: (bi, jnp.minimum(j, nj - 2), 0)),
            pl.BlockSpec((1, tm, ym_ctx.shape[2]), lambda bi, j: (bi, 0, 0)),
            tok(yp.shape[2]), tok(ys.shape[2]), tok(gates.shape[2]),
            pl.BlockSpec((1, 1, 8, d), lambda bi, j: (bi, j // (nj - 1), 0, 0)),
            full(g_ffn), full(wbm), full(wbp), full(wbs), full(wout), full(wsg), full(wsu), full(wsd),
            full(wrt)],
        out_specs=[slab_out(SLAB_ROWS), slab_out(PACKED_ROWS),
                   pl.BlockSpec((N_EXPERTS, tm), lambda bi, j: (0, bi * nk + j))],
        out_shape=[jax.ShapeDtypeStruct((b * nk * tm * SLAB_ROWS, 128), F32),
                   jax.ShapeDtypeStruct((b * nk * tm * PACKED_ROWS, 128), U32),
                   jax.ShapeDtypeStruct((N_EXPERTS, b * nk * tm), F32)],
        compiler_params=_cparams(("arbitrary", "arbitrary")),
        name="merge_stage",
    )(*stream, ym_lat, ym_ctx, yp, ys, gates, modtab, g_ffn, wbm, wbp, wbs, wout, wsg, wsu, wsd, wrt)


def _route_kernel(lg_ref, bias_ref, dest_ref, wts_ref, seg_ref):
    tm = lg_ref.shape[1]
    ne = N_EXPERTS
    neg_inf = -jnp.inf
    scores = _sigmoid(lg_ref[...])
    sel = scores + bias_ref[...]
    iota_g = lax.broadcasted_iota(I32, (EXPERTS_PER_GROUP, tm), 0)
    gscore = []
    for g in range(N_GROUPS):
        sg = sel[g * EXPERTS_PER_GROUP:(g + 1) * EXPERTS_PER_GROUP]
        m1 = jnp.max(sg, axis=0, keepdims=True)
        i1 = jnp.min(jnp.where(sg == m1, iota_g, EXPERTS_PER_GROUP), axis=0, keepdims=True)
        m2 = jnp.max(jnp.where(iota_g == i1, neg_inf, sg), axis=0, keepdims=True)
        gscore.append(m1 + m2)
    rows = []
    for g in range(N_GROUPS):
        rank = jnp.zeros((1, tm), I32)
        for g2 in range(N_GROUPS):
            if g2 == g:
                continue
            beats = (gscore[g2] >= gscore[g]) if g2 < g else (gscore[g2] > gscore[g])
            rank = rank + beats.astype(I32)
        rows.append(jnp.where(rank < TOPK_GROUPS, sel[g * EXPERTS_PER_GROUP:(g + 1) * EXPERTS_PER_GROUP], neg_inf))
    cur = jnp.concatenate(rows, axis=0)
    iota_e = lax.broadcasted_iota(I32, (ne, tm), 0)
    picks = []
    member = jnp.zeros((ne, tm), F32)
    for _ in range(TOP_K):
        m = jnp.max(cur, axis=0, keepdims=True)
        idx = jnp.min(jnp.where(cur == m, iota_e, ne), axis=0, keepdims=True)
        hit = iota_e == idx
        picks.append(hit)
        member = member + hit.astype(F32)
        cur = jnp.where(hit, neg_inf, cur)
    earlier = (lax.broadcasted_iota(I32, (tm, tm), 0) < lax.broadcasted_iota(I32, (tm, tm), 1)).astype(BF16)
    pos = _dot(member.astype(BF16), earlier)
    cnt_col = jnp.sum(member, axis=1, keepdims=True)
    blocks_col = jnp.floor((cnt_col + (SEG_ALIGN - 1)) * (1.0 / SEG_ALIGN))
    lower = (lax.broadcasted_iota(I32, (ne, ne), 1) < lax.broadcasted_iota(I32, (ne, ne), 0)).astype(BF16)
    off_col = _dot(lower, jnp.broadcast_to(blocks_col, (ne, 128)).astype(BF16))[:, 0:1] * SEG_ALIGN
    base = off_col + pos
    w_rows = [jnp.sum(jnp.where(hit, scores, 0.0), axis=0, keepdims=True) for hit in picks]
    denom = w_rows[0]
    for w in w_rows[1:]:
        denom = denom + w
    for k, hit in enumerate(picks):
        dest_ref[0, k:k + 1, :] = jnp.sum(jnp.where(hit, base, 0.0), axis=0, keepdims=True).astype(I32)
        wts_ref[0, k:k + 1, :] = w_rows[k] / denom * ROUTED_SCALE
    member_pad = jnp.concatenate([member, jnp.zeros((128 - ne, tm), F32)], axis=0).astype(BF16)
    cnt_row = _dot_nt(jnp.ones((8, tm), BF16), member_pad)
    blocks_row = jnp.floor((cnt_row + (SEG_ALIGN - 1)) * (1.0 / SEG_ALIGN))
    before = (lax.broadcasted_iota(I32, (128, 128), 0) < lax.broadcasted_iota(I32, (128, 128), 1)).astype(BF16)
    off_row = _dot(blocks_row.astype(BF16), before) * SEG_ALIGN
    r = lax.broadcasted_iota(I32, (8, 128), 0)
    seg_ref[0] = jnp.where(r == 0, off_row, jnp.where(r == 1, cnt_row, 0.0)).astype(I32)


def _route_stage(lg_t, bias_col):
    ne, t_all = lg_t.shape
    tm = MOE_TILE
    nt = t_all // tm
    return pl.pallas_call(
        _route_kernel,
        grid=(nt,),
        in_specs=[pl.BlockSpec((ne, tm), lambda i: (0, i)),
                  pl.BlockSpec((ne, 1), lambda i: (0, 0))],
        out_specs=[pl.BlockSpec((1, TOP_K, tm), lambda i: (i, 0, 0)),
                   pl.BlockSpec((1, TOP_K, tm), lambda i: (i, 0, 0)),
                   pl.BlockSpec((1, 8, 128), lambda i: (i, 0, 0))],
        out_shape=[jax.ShapeDtypeStruct((nt, TOP_K, tm), I32),
                   jax.ShapeDtypeStruct((nt, TOP_K, tm), F32),
                   jax.ShapeDtypeStruct((nt, 8, 128), I32)],
        compiler_params=_cparams(("arbitrary",)),
        name="route_stage",
    )(lg_t, bias_col)


def _moe_row_stride(tm):
    cap = TOP_K * tm + N_EXPERTS * SEG_ALIGN + MOE_CHUNK
    blocks = cap // 8 + 1
    return 8 * (blocks + 1 - blocks % 2)


def _moe_kernel(dest_ref, wts_ref, seg_ref, fp_hbm, xs_hbm, gt_ref, gfin_ref, wg_ref, wu_ref, wd_ref, out_hbm,
                bufs_ref, fp_stage, xs_stage, out_stage, slab_tmp, fp_sem, xs_sem, out_sem,
                *, srow, sub, n_sub_per_batch, ctx_sub, n_tiles, final_norm):
    pr = pl.program_id(0)
    eg = pl.program_id(1)
    tm = dest_ref.shape[2]
    ch = MOE_CHUNK
    group = wg_ref.shape[0]
    lane_blk = 128
    n_blk = tm // lane_blk
    per = bufs_ref.shape[0]
    n_slots = fp_stage.shape[0]
    n_here = jnp.minimum(per, n_tiles - pr * per)

    def table_block(ref, half, blk):
        return ref.at[half, :, pl.ds(pl.multiple_of(blk * lane_blk, lane_blk), lane_blk)]

    def token_rows(half, blk, per_tok):
        first = ((pr * per + half) * tm + blk * lane_blk) * per_tok
        return pl.ds(pl.multiple_of(first, lane_blk * per_tok), lane_blk * per_tok)

    def fp_copy(half, blk, slot):
        return pltpu.make_async_copy(fp_hbm.at[token_rows(half, blk, PACKED_ROWS), :], fp_stage.at[slot],
                                     fp_sem.at[slot])

    def xs_copy(half, blk, slot):
        return pltpu.make_async_copy(xs_hbm.at[token_rows(half, blk, SLAB_ROWS), :], xs_stage.at[slot],
                                     xs_sem.at[slot])

    def out_copy(half, blk, slot):
        if not final_norm:
            dst = out_hbm.at[token_rows(half, blk, SLAB_ROWS), :]
        else:
            first = (pr * per + half) * tm + blk * lane_blk
            row_len = out_hbm.shape[1]
            bi = first // row_len
            dst = out_hbm.at[bi, pl.ds(pl.multiple_of(first - bi * row_len, lane_blk), lane_blk), :]
        return pltpu.make_async_copy(out_stage.at[slot], dst, out_sem.at[slot])

    def group_rows(half):
        buf_ref = bufs_ref.at[half]
        for blk in range(n_slots - 1):
            fp_copy(half, blk, blk).start()
        buf_ref[...] = jnp.zeros(buf_ref.shape, U32)

        def body(blk, carry):
            slot = blk % n_slots
            fp_copy(half, blk, slot).wait()

            @pl.when(blk + n_slots - 1 < n_blk)
            def _prefetch():
                fp_copy(half, blk + n_slots - 1, (blk + n_slots - 1) % n_slots).start()

            dest = table_block(dest_ref, half, blk)
            rows = fp_stage.at[slot]
            for u in range(lane_blk):
                slab = rows[u * PACKED_ROWS:(u + 1) * PACKED_ROWS, :]
                for k in range(TOP_K):
                    buf_ref[pl.ds(dest[k, u], PACKED_ROWS, stride=srow), :] = slab
            return carry

        lax.fori_loop(0, n_blk, body, 0)

    def experts():
        def one_expert(ge, carry):
            e = eg * group + ge
            counts = [seg_ref[half, 1, e] for half in range(per)]
            offsets = [seg_ref[half, 0, e] for half in range(per)]

            def ffn_rows(halves, rows, chunk_idx):
                words, keep = [], []
                for h in halves:
                    r0 = offsets[h] + chunk_idx * rows
                    words.append([bufs_ref[h, pl.ds(pl.multiple_of(q * srow + r0, 8), rows), :] for q in range(4)])
                    keep.append(lax.broadcasted_iota(I32, (rows, 1), 0) < counts[h] - chunk_idx * rows)
                xb = []
                for ws in words:
                    w = jnp.concatenate(ws, axis=1)
                    xb.append(jnp.concatenate([_unpack_lo(w), _unpack_hi(w)], axis=1).astype(BF16))
                gates = [(_dot(x, wg_ref[ge]), _dot(x, wu_ref[ge])) for x in xb]
                hmid = [(g * _sigmoid(g) * u).astype(BF16) for g, u in gates]
                packed = [_pack_bf16_pair(_dot(hm, wd_ref[ge])) for hm in hmid]
                for n, h in enumerate(halves):
                    r0 = offsets[h] + chunk_idx * rows
                    for q in range(4):
                        bufs_ref[h, pl.ds(pl.multiple_of(q * srow + r0, 8), rows), :] = jnp.where(
                            keep[n], packed[n][:, q * 128:(q + 1) * 128], words[n][q])

            most = counts[0]
            for c in counts[1:]:
                most = jnp.maximum(most, c)
            lo = 0
            for rows in MOE_BODY_ROWS:
                pl.when((most > lo) & (most <= rows))(functools.partial(ffn_rows, tuple(range(per)), rows, 0))
                lo = rows

            @pl.when(most > ch)
            def _long_segments():
                for h in range(per):
                    def chunk(c, carry2, h=h):
                        ffn_rows((h,), ch, c)
                        return carry2

                    lax.fori_loop(0, (counts[h] + ch - 1) // ch, chunk, 0)

            return carry

        lax.fori_loop(0, group, one_expert, 0)

    def combine(half):
        buf_ref = bufs_ref.at[half]
        for blk in range(n_slots - 1):
            xs_copy(half, blk, blk).start()

        def body(blk, carry):
            slot = blk % n_slots
            xs_copy(half, blk, slot).wait()

            @pl.when(blk + n_slots - 1 < n_blk)
            def _prefetch():
                xs_copy(half, blk + n_slots - 1, (blk + n_slots - 1) % n_slots).start()

            @pl.when(blk >= n_slots)
            def _slot_free():
                out_copy(half, blk - n_slots, slot).wait()

            sub_blk = ((pr * per + half) * tm + blk * lane_blk) // sub
            bi = sub_blk // n_sub_per_batch
            gate = gt_ref[bi * 2 + (sub_blk - bi * n_sub_per_batch) // ctx_sub]
            dest = table_block(dest_ref, half, blk)
            wts = table_block(wts_ref, half, blk)
            xs = xs_stage.at[slot]
            out = slab_tmp if final_norm else out_stage.at[slot]
            for u in range(lane_blk):
                acc_lo = jnp.zeros((4, 128), F32)
                acc_hi = jnp.zeros((4, 128), F32)
                for k in range(TOP_K):
                    words = buf_ref[pl.ds(dest[k, u], PACKED_ROWS, stride=srow), :]
                    wk = wts[k, u]
                    acc_lo = acc_lo + wk * _unpack_lo(words)
                    acc_hi = acc_hi + wk * _unpack_hi(words)
                out[u * 8:u * 8 + 4, :] = xs[u * 8:u * 8 + 4, :] + gate[0:4] * acc_lo
                out[u * 8 + 4:u * 8 + 8, :] = xs[u * 8 + 4:u * 8 + 8, :] + gate[4:8] * acc_hi
            if final_norm:
                out_stage[slot] = _rms(_slab_rows_to_matrix(slab_tmp, lane_blk, SLAB_ROWS), gfin_ref[...])
            out_copy(half, blk, slot).start()
            return carry

        lax.fori_loop(0, n_blk, body, 0)
        for blk in range(n_blk - n_slots, n_blk):
            out_copy(half, blk, blk % n_slots).wait()

    def each_tile(fn):
        def body(half, carry):
            fn(half)
            return carry

        lax.fori_loop(0, n_here, body, 0)

    @pl.when(eg == 0)
    def _first_step():
        each_tile(group_rows)
        for half in range(1, per):
            @pl.when(half >= n_here)
            def _clear(half=half):
                bufs_ref[half] = jnp.zeros(bufs_ref.shape[1:], U32)

    experts()
    pl.when(eg == pl.num_programs(1) - 1)(functools.partial(each_tile, combine))


def _moe_stage(dest, wts, seg, fp4, xs8, gt2, g_final, wg, wu, wd, layer, n_sub_per_batch, keep_ctx, final_norm):
    ctx_sub = n_sub_per_batch - 1 if keep_ctx else n_sub_per_batch
    d = SLAB_ROWS * 128
    tokens_per_row = n_sub_per_batch * TOKEN_TILE
    out_shape = (xs8.shape[0] // (SLAB_ROWS * tokens_per_row), tokens_per_row, d) if final_norm else xs8.shape
    out_block = (128, d) if final_norm else (128 * SLAB_ROWS, 128)
    nt, _, tm = dest.shape
    ne = wg.shape[1]
    group = MOE_EXPERTS_PER_STEP
    per = MOE_TILES_PER_STEP
    assert ne % group == 0 and tm % 256 == 0 and TOKEN_TILE % 128 == 0 and per == 2
    n_rows = -(-nt // per)
    pad = ((0, n_rows * per - nt), (0, 0), (0, 0))
    dest, wts, seg = jnp.pad(dest, pad), jnp.pad(wts, pad), jnp.pad(seg, pad)
    srow = _moe_row_stride(tm)
    smem = lambda shape: pl.BlockSpec(shape, lambda i, e: (i, 0, 0), memory_space=pltpu.SMEM)
    hbm = pl.BlockSpec(memory_space=pl.ANY)
    slots = MOE_STAGE_SLOTS
    assert tm // 128 >= slots
    stage = lambda per_tok, dt: pltpu.VMEM((slots, 128 * per_tok, 128), dt)
    return pl.pallas_call(
        functools.partial(_moe_kernel, srow=srow, sub=TOKEN_TILE, n_sub_per_batch=n_sub_per_batch, ctx_sub=ctx_sub,
                          n_tiles=nt, final_norm=final_norm),
        grid=(n_rows, ne // group),
        in_specs=[smem((per, TOP_K, tm)), smem((per, TOP_K, tm)), smem((per, 8, 128)),
                  hbm, hbm,
                  pl.BlockSpec(gt2.shape, lambda i, e: (0, 0, 0)),
                  pl.BlockSpec(g_final.shape, lambda i, e: (0, 0)),
                  pl.BlockSpec((None, group) + wg.shape[2:], lambda i, e: (layer, e, 0, 0)),
                  pl.BlockSpec((None, group) + wu.shape[2:], lambda i, e: (layer, e, 0, 0)),
                  pl.BlockSpec((None, group) + wd.shape[2:], lambda i, e: (layer, e, 0, 0))],
        out_specs=hbm,
        out_shape=jax.ShapeDtypeStruct(out_shape, F32),
        scratch_shapes=[pltpu.VMEM((per, 4 * srow, 128), U32),
                        stage(PACKED_ROWS, U32), stage(SLAB_ROWS, F32), pltpu.VMEM((slots,) + out_block, F32),
                        pltpu.VMEM((128 * SLAB_ROWS, 128), F32),
                        pltpu.SemaphoreType.DMA((slots,)), pltpu.SemaphoreType.DMA((slots,)),
                        pltpu.SemaphoreType.DMA((slots,))],
        compiler_params=_cparams(("arbitrary", "arbitrary")),
        name="moe_stage",
    )(dest, wts, seg, fp4, xs8, gt2, g_final, wg, wu, wd)


def _rot_lanes(x, head_dim):
    quarter = head_dim // 4
    lane = lax.broadcasted_iota(I32, x.shape, 1)
    width = x.shape[1]
    return jnp.where(lane % (2 * quarter) < quarter, -pltpu.roll(x, width - quarter, 1), pltpu.roll(x, quarter, 1))


def _w1_kernel(w_ref, o_ref):
    offs = [0]
    for w in (MLA_Q_LORA, MLA_KV_LORA, MLA_ROPE, POOL_WIDTH, SWA_WIDTH, SWA_KV_WIDTH, SWA_KV_WIDTH):
        offs.append(offs[-1] + w)
    o_cq, o_ckv, o_kr, o_u, o_qs, o_ks, o_vs, o_gl = offs
    lane = lax.broadcasted_iota(I32, (w_ref.shape[0], 128), 1)

    def put(name, val):
        a, b = _SEG[name]
        o_ref[:, a:b] = val.astype(BF16)

    def kv_slots(x):
        low = lane < SWA_HEAD_DIM
        return jnp.concatenate([jnp.where(low, x, 0.0), jnp.where(low, pltpu.roll(x, SWA_HEAD_DIM, 1), 0.0)], axis=1)

    def kr_slot(x):
        return jnp.where((lane >= MLA_NOPE) & (lane < MLA_NOPE + MLA_ROPE), pltpu.roll(x, MLA_NOPE, 1), 0.0)

    put("cq", w_ref[:, o_cq:o_ckv])
    put("ckv", w_ref[:, o_ckv:o_kr])
    put("u", w_ref[:, o_u:o_qs])
    put("qs", w_ref[:, o_qs:o_ks] * (SWA_SCALE * LOG2_E))
    put("ks", kv_slots(w_ref[:, o_ks:o_vs]))
    put("vs", kv_slots(w_ref[:, o_vs:o_gl]))
    put("kr", kr_slot(w_ref[:, o_kr:o_kr + 128]))
    put("gl", w_ref[:, o_gl:w_ref.shape[1]])


def _fused_in_weight(w_in, layer):
    _, d, width = w_in.shape
    tr = 256
    assert d % tr == 0 and SWA_KV_WIDTH == 128 and MLA_KV_LORA + MLA_Q_LORA == 5 * 128
    return pl.pallas_call(
        _w1_kernel,
        grid=(d // tr,),
        in_specs=[pl.BlockSpec((None, tr, width), lambda i: (layer, i, 0))],
        out_specs=pl.BlockSpec((tr, FUSED_IN_WIDTH), lambda i: (i, 0)),
        out_shape=jax.ShapeDtypeStruct((d, FUSED_IN_WIDTH), BF16),
        compiler_params=_cparams(("arbitrary",)),
        name="fused_in_weight",
    )(w_in)


def _rope_pattern(s_len, n_ctx, rot_dim):
    t = jnp.arange(s_len)
    row = (t // GRID_W).astype(F32)
    col = (t % GRID_W).astype(F32)
    n_freq = rot_dim // 4
    inv_freq = ROPE_BASE ** (-jnp.arange(n_freq, dtype=F32) / n_freq)
    ang_r = row[:, None] * inv_freq[None, :]
    ang_c = col[:, None] * inv_freq[None, :]
    cos = jnp.concatenate([jnp.cos(ang_r), jnp.cos(ang_r), jnp.cos(ang_c), jnp.cos(ang_c)], axis=1)
    sin = jnp.concatenate([jnp.sin(ang_r), jnp.sin(ang_r), jnp.sin(ang_c), jnp.sin(ang_c)], axis=1)
    cos = jnp.concatenate([cos, jnp.ones((n_ctx, rot_dim), F32)], axis=0)
    sin = jnp.concatenate([sin, jnp.zeros((n_ctx, rot_dim), F32)], axis=0)
    return cos, sin


def _layer_weights(w_in, layer, w_uq, w_ukv):
    w1 = _fused_in_weight(w_in, layer)
    lq = w_uq.shape[0]
    wq3 = (w_uq * (MLA_SCALE * LOG2_E)).reshape(lq, MLA_HEADS, MLA_NOPE + MLA_ROPE)
    pad = jnp.zeros((lq, MLA_HEADS, MLA_HEAD_PAD - MLA_NOPE - MLA_ROPE), F32)
    wq = jnp.concatenate([wq3, pad], axis=2).reshape(lq, MLA_HEADS * MLA_HEAD_PAD).astype(BF16)
    lkv = w_ukv.shape[0]
    wkv3 = w_ukv.reshape(lkv, MLA_HEADS, MLA_NOPE + MLA_V)
    wkn = jnp.concatenate([wkv3[:, :, :MLA_NOPE], jnp.zeros((lkv, MLA_HEADS, MLA_HEAD_PAD - MLA_NOPE), F32)],
                          axis=2).reshape(lkv, MLA_HEADS * MLA_HEAD_PAD).astype(BF16)
    wv = jnp.concatenate([wkv3[:, :, MLA_NOPE:], jnp.zeros((lkv, MLA_HEADS, MLA_HEAD_PAD - MLA_V), F32)],
                         axis=2).reshape(lkv, MLA_HEADS * MLA_HEAD_PAD).astype(BF16)
    return w1, wq, wkn, wv


def kernel(x, c, ctx, c_ctx, w_mod, b_mod, g_mix, g_ffn, w_in, g_mla_q, g_mla_kv, w_mla_uq, w_mla_ukv, w_pool,
           pool_scale, swa_sink, w_br_mla, w_br_pool, w_br_swa, w_out, w_router, router_bias, w_exp_gate,
           w_exp_up, w_exp_down, w_sh_gate, w_sh_up, w_sh_down, g_final):
    b, s_len, d = x.shape
    n_ctx = ctx.shape[1]
    n_layers = w_mod.shape[0]
    p = n_ctx + s_len
    assert n_ctx == TOKEN_TILE and s_len % TOKEN_TILE == 0 and (b * p) % MOE_TILE == 0 and b + 1 <= 8
    assert d == 1024 and w_in.shape[2] == (MLA_Q_LORA + MLA_KV_LORA + MLA_ROPE + POOL_WIDTH + SWA_WIDTH
                                           + 2 * SWA_KV_WIDTH + 3 * d)

    cm, sm = _rope_pattern(s_len, n_ctx, MLA_ROPE)
    tail = jnp.zeros((p, MLA_HEAD_PAD - MLA_NOPE - MLA_ROPE), F32)
    tabm = jnp.concatenate([jnp.ones((p, MLA_NOPE), F32), cm, tail, jnp.zeros((p, MLA_NOPE), F32), sm, tail], axis=1)
    cs, ss = _rope_pattern(s_len, n_ctx, SWA_HEAD_DIM)
    tabs = jnp.concatenate([cs, cs, ss, ss], axis=1)

    cvec = jnp.concatenate([c, c_ctx[None, :], jnp.zeros((8 - b - 1, d), F32)], axis=0)
    mod_all = _modulation(cvec, w_mod, b_mod).reshape(n_layers, 8, N_MOD, d)

    stream = (x, ctx)
    wg_all, wu_all, wd_all = w_exp_gate.astype(BF16), w_exp_up.astype(BF16), w_exp_down.astype(BF16)
    for i in range(n_layers):
        keep_ctx = i < n_layers - 1
        lat = mod_all[i, :b]
        ctx_rows = jnp.broadcast_to(mod_all[i, b][None], (b, N_MOD, d))
        modtab = jnp.pad(jnp.stack([lat, ctx_rows], axis=1), ((0, 0), (0, 0), (0, 8 - N_MOD), (0, 0)))
        w1, wq, wkn, wv = _layer_weights(w_in, i, w_mla_uq[i], w_mla_ukv[i])
        q, k, v, u, qs, ks, vs, gates = _input_stage(
            stream, b, modtab, g_mix[i][None], w1, g_mla_q[i][None], wq, g_mla_kv[i][None], wkn, wv, tabm, tabs)
        y_mla, y_mla_ctx = _mla_attention(q, k, v, n_ctx)
        y_swa = _swa_attention(swa_sink[i], qs, ks, vs, n_ctx)
        y_pool = _pool_stage(u, w_pool[i].astype(BF16), pool_scale[i][None], n_ctx)
        xs, fp, lg_t = _merge_stage(
            stream, y_mla, y_mla_ctx, y_pool, y_swa, gates, modtab, g_ffn[i][None], w_br_mla[i].astype(BF16),
            w_br_pool[i].astype(BF16), w_br_swa[i].astype(BF16), w_out[i].astype(BF16),
            w_sh_gate[i].astype(BF16), w_sh_up[i].astype(BF16), w_sh_down[i].astype(BF16),
            w_router[i].T.astype(BF16), keep_ctx)
        dest, wts, seg = _route_stage(lg_t, router_bias[i][:, None])
        gt2 = modtab[:, :, 5, :].reshape(b * 2, 8, d // 8)
        stream = (_moe_stage(dest, wts, seg, fp, xs, gt2, g_final[None], wg_all, wu_all, wd_all, i,
                             (p if keep_ctx else s_len) // TOKEN_TILE, keep_ctx, final_norm=not keep_ctx),)
    return stream[0]
```

```python
import functools

import jax
import jax.numpy as jnp
from jax import lax
from jax.experimental import pallas as pl
from jax.experimental.pallas import tpu as pltpu

F32 = jnp.float32
BF16 = jnp.bfloat16
U32 = jnp.uint32
I32 = jnp.int32

NORM_EPS = 1e-6
ROPE_BASE = 10000.0
GRID_W = 64
N_MOD = 6

MLA_HEADS = 8
MLA_Q_LORA = 384
MLA_KV_LORA = 256
MLA_NOPE = 64
MLA_ROPE = 32
MLA_V = 64
MLA_SCALE = (MLA_NOPE + MLA_ROPE) ** -0.5
MLA_HEAD_PAD = 128
LOG2_E = 1.4426950408889634

POOL_WINDOWS = (2, 4, 8, 16)
POOL_GROUP_DIM = 128
POOL_WIDTH = 512
POOL_HALO = 8

SWA_Q_HEADS = 8
SWA_KV_HEADS = 2
SWA_HEAD_DIM = 64
SWA_WINDOW = 128
SWA_BLOCK = 128
SWA_SCALE = SWA_HEAD_DIM ** -0.5
SWA_WIDTH = SWA_Q_HEADS * SWA_HEAD_DIM
SWA_KV_WIDTH = SWA_KV_HEADS * SWA_HEAD_DIM

N_EXPERTS = 64
TOP_K = 8
N_GROUPS = 8
TOPK_GROUPS = 4
EXPERTS_PER_GROUP = 8
D_EXPERT = 256
ROUTED_SCALE = 2.5

TOKEN_TILE = 256
SWA_Q_TILE = 256
MLA_Q_TILE = 1024
MLA_KEY_CHUNK = 256
MOE_TILE = 1024
MOE_CHUNK = 256
MOE_BODY_ROWS = (128, 144, 160, 176, 192, 224, 256)
MOE_EXPERTS_PER_STEP = 2
MOE_TILES_PER_STEP = 2
MOE_STAGE_SLOTS = 4
SEG_ALIGN = 8
SLAB_ROWS = 8
PACKED_ROWS = 4
MASK_VALUE = -1e30
HI16 = 0xFFFF0000

VMEM_LIMIT = 56 * 1024 * 1024

_SEG_WIDTHS = (("cq", 384), ("ckv", 256), ("u", 512), ("qs", 512), ("ks", 256), ("vs", 256), ("kr", 128),
               ("gl", 3072))
_SEG = {}
_o = 0
for _n, _w in _SEG_WIDTHS:
    _SEG[_n] = (_o, _o + _w)
    _o += _w
FUSED_IN_WIDTH = _o


def _cparams(sem):
    return pltpu.CompilerParams(dimension_semantics=sem, vmem_limit_bytes=VMEM_LIMIT)


def _dot(a, b):
    return jnp.dot(a, b, preferred_element_type=F32)


def _dot_nt(a, b):
    return lax.dot_general(a, b, (((1,), (1,)), ((), ())), preferred_element_type=F32)


def _sigmoid(x):
    return 1.0 / (1.0 + jnp.exp(-x))


def _rms(x, g):
    return x * lax.rsqrt(jnp.mean(x * x, axis=-1, keepdims=True) + NORM_EPS) * g


def _pack_bf16_pair(v):
    n = v.shape[1] // 2
    bits = pltpu.bitcast(v.astype(BF16).astype(F32), U32)
    return (bits[:, :n] >> 16) | (bits[:, n:] & jnp.uint32(HI16))


def _slab_rows_to_matrix(ref, n_tok, per_tok):
    return jnp.concatenate([ref[pl.ds(c, n_tok, stride=per_tok), :] for c in range(per_tok)], axis=1)


def _matrix_to_slab_rows(ref, val, per_tok):
    n_tok = val.shape[0]
    for c in range(per_tok):
        ref[pl.ds(c, n_tok, stride=per_tok), :] = val[:, c * 128:(c + 1) * 128]


def _unpack_lo(w):
    return pltpu.bitcast(w << 16, F32)


def _unpack_hi(w):
    return pltpu.bitcast(w & jnp.uint32(HI16), F32)


def _mod_kernel(c_ref, w_ref, b_ref, o_ref):
    c = c_ref[...]
    a = (c * _sigmoid(c)).astype(BF16)
    o_ref[0] = _dot(a, w_ref[0].astype(BF16)) + b_ref[0]


def _modulation(cvec, w_mod, b_mod):
    n_layers, d, width = w_mod.shape
    tn = width // 4
    return pl.pallas_call(
        _mod_kernel,
        grid=(n_layers, width // tn),
        in_specs=[pl.BlockSpec((8, d), lambda l, n: (0, 0)),
                  pl.BlockSpec((1, d, tn), lambda l, n: (l, 0, n)),
                  pl.BlockSpec((1, 1, tn), lambda l, n: (l, 0, n))],
        out_specs=pl.BlockSpec((1, 8, tn), lambda l, n: (l, 0, n)),
        out_shape=jax.ShapeDtypeStruct((n_layers, 8, width), F32),
        compiler_params=_cparams(("arbitrary", "arbitrary")),
        name="modulation",
    )(cvec, w_mod, b_mod.reshape(n_layers, 1, width))


def _stream_specs(stream, tm, nj):
    if len(stream) == 1:
        return [pl.BlockSpec((tm * SLAB_ROWS, 128), lambda bi, j: (bi * nj + j, 0))]
    d = stream[0].shape[2]
    return [pl.BlockSpec((1, tm, d), lambda bi, j: (bi, jnp.minimum(j, nj - 2), 0)),
            pl.BlockSpec((1, tm, d), lambda bi, j: (bi, 0, 0))]


def _stream_tile(refs, n_tok, ctx_tile):
    if len(refs) == 1:
        return _slab_rows_to_matrix(refs[0], n_tok, SLAB_ROWS)
    return jnp.where(pl.program_id(1) == ctx_tile, refs[1][0], refs[0][0])


def _stream_tokens(stream, b):
    return stream[0].shape[0] // (b * SLAB_ROWS) if len(stream) == 1 else stream[0].shape[1] + stream[1].shape[1]


def _in_kernel(*refs, n_stream, ctx_tile):
    (mod_ref, g_ref, w1_ref, gq_ref, wq_ref, gkv_ref, wkn_ref, wv_ref, tabm_ref, tabs_ref,
     q_ref, k_ref, v_ref, u_ref, qs_ref, ks_ref, vs_ref, gate_ref) = refs[n_stream:]
    x = _stream_tile(refs[:n_stream], q_ref.shape[1], ctx_tile)
    mod = mod_ref[0, 0]
    h = _rms(x, g_ref[...]) * (1.0 + mod[1:2]) + mod[0:1]
    hb = h.astype(BF16)

    def seg(name):
        a, b = _SEG[name]
        return _dot(hb, w1_ref[:, a:b])

    cos_m = tabm_ref[:, 0:128]
    sin_m = tabm_ref[:, 128:256]
    cos_s = tabs_ref[:, 0:128]
    sin_s = tabs_ref[:, 128:256]

    def rotary(val, cos, sin, head_dim):
        tiles = [val[:, c:c + 128] for c in range(0, val.shape[1], 128)]
        return jnp.concatenate([t * cos + _rot_lanes(t, head_dim) * sin for t in tiles], axis=1)

    cqn = _rms(seg("cq"), gq_ref[...]).astype(BF16)
    q_ref[0] = rotary(_dot(cqn, wq_ref[...]), cos_m, sin_m, MLA_ROPE).astype(BF16)

    ckvn = _rms(seg("ckv"), gkv_ref[...]).astype(BF16)
    kr = rotary(seg("kr"), cos_m, sin_m, MLA_ROPE)
    k_ref[0] = (_dot(ckvn, wkn_ref[...]) + jnp.tile(kr, (1, MLA_HEADS))).astype(BF16)
    ones_lane = (lax.broadcasted_iota(I32, (1, MLA_HEADS * MLA_HEAD_PAD), 1) % MLA_HEAD_PAD == MLA_V).astype(F32)
    v_ref[0] = (_dot(ckvn, wv_ref[...]) + ones_lane).astype(BF16)

    u_ref[0] = seg("u")

    qs_ref[0] = rotary(seg("qs"), cos_s, sin_s, SWA_HEAD_DIM).astype(BF16)
    ks = rotary(seg("ks"), cos_s, sin_s, SWA_HEAD_DIM)
    for hk in range(SWA_KV_HEADS):
        k_lo = ks[:, hk * 128:(hk + 1) * 128]
        ks_ref[0, :, (2 * hk) * 128:(2 * hk + 1) * 128] = k_lo.astype(BF16)
        ks_ref[0, :, (2 * hk + 1) * 128:(2 * hk + 2) * 128] = pltpu.roll(k_lo, SWA_HEAD_DIM, 1).astype(BF16)
    ones_s = (lax.broadcasted_iota(I32, (1, SWA_KV_HEADS * 128), 1) % 128 == SWA_HEAD_DIM).astype(F32)
    vs_ref[0] = (seg("vs") + ones_s).astype(BF16)

    g0, _ = _SEG["gl"]
    for p in range(6):
        gate_ref[0, :, p * 512:(p + 1) * 512] = _sigmoid(
            _dot(hb, w1_ref[:, g0 + p * 512:g0 + (p + 1) * 512])).astype(BF16)


def _input_stage(stream, b, modtab, g_mix, w1, g_q, wq, g_kv, wkn, wv, tabm, tabs):
    d = SLAB_ROWS * 128
    p = _stream_tokens(stream, b)
    tm = TOKEN_TILE
    nj = p // tm
    tok = lambda w: pl.BlockSpec((1, tm, w), lambda bi, j: (bi, j, 0))
    full = lambda a: pl.BlockSpec(a.shape, lambda bi, j: (0,) * a.ndim)
    outs = [(1024, BF16), (1024, BF16), (1024, BF16), (512, F32), (512, BF16), (512, BF16), (256, BF16),
            (3072, BF16)]
    return pl.pallas_call(
        functools.partial(_in_kernel, n_stream=len(stream), ctx_tile=nj - 1),
        grid=(b, nj),
        in_specs=_stream_specs(stream, tm, nj) + [
            pl.BlockSpec((1, 1, 8, d), lambda bi, j: (bi, j // (nj - 1), 0, 0)),
            full(g_mix), full(w1), full(g_q), full(wq), full(g_kv), full(wkn), full(wv),
            pl.BlockSpec((tm, 256), lambda bi, j: (j, 0)),
            pl.BlockSpec((tm, 256), lambda bi, j: (j, 0))],
        out_specs=[tok(w) for w, _ in outs],
        out_shape=[jax.ShapeDtypeStruct((b, p, w), dt) for w, dt in outs],
        compiler_params=_cparams(("arbitrary", "arbitrary")),
        name="input_stage",
    )(*stream, modtab, g_mix, w1, g_q, wq, g_kv, wkn, wv, tabm, tabs)


def _mla_kernel(q_ref, k_ref, v_ref, *rest, tk, n_main, tail_rows):
    y_ref, m_ref, acc_ref = rest[-3:]
    hw = MLA_HEAD_PAD
    m_ref[...] = jnp.full(m_ref.shape, MASK_VALUE, F32)
    acc_ref[...] = jnp.zeros(acc_ref.shape, F32)

    def attend(r0, rows):
        for h in range(MLA_HEADS):
            q = q_ref[0, :, h * hw:(h + 1) * hw]
            k = k_ref[0, pl.ds(r0, rows), h * hw:(h + 1) * hw]
            v = v_ref[0, pl.ds(r0, rows), h * hw:(h + 1) * hw]
            s = _dot_nt(q, k)
            m_prev = m_ref[h]
            m_new = jnp.maximum(m_prev, jnp.max(s, axis=1, keepdims=True))
            m_ref[h] = m_new
            pr = jnp.exp2(s - jnp.tile(m_new, (1, rows // hw)))
            acc_ref[h] = jnp.exp2(m_prev - m_new) * acc_ref[h] + _dot(pr.astype(BF16), v)

    if n_main:
        def step(c, carry):
            attend(pl.multiple_of(c * tk, tk), tk)
            return carry

        lax.fori_loop(0, n_main, step, 0)
    if tail_rows:
        attend(n_main * tk, tail_rows)

    first = lax.broadcasted_iota(I32, (q_ref.shape[1], hw), 1) < MLA_V
    for hp in range(MLA_HEADS // 2):
        o = []
        for h in (2 * hp, 2 * hp + 1):
            acc = acc_ref[h]
            o.append(acc / acc[:, MLA_V:MLA_V + 1])
        y_ref[0, :, hp * hw:(hp + 1) * hw] = jnp.where(first, o[0], pltpu.roll(o[1], MLA_V, 1)).astype(BF16)


def _mla_attention(q, k, v, n_ctx):
    b, p, width = q.shape
    s_len = p - n_ctx
    tq = MLA_Q_TILE
    tk = MLA_KEY_CHUNK
    assert s_len % tq == 0 and s_len % tk == 0 and s_len % n_ctx == 0
    out_w = MLA_HEADS * MLA_V
    state = lambda rows: pltpu.VMEM((MLA_HEADS, rows, MLA_HEAD_PAD), F32)
    y_lat = pl.pallas_call(
        functools.partial(_mla_kernel, tk=tk, n_main=p // tk, tail_rows=p % tk),
        grid=(b, s_len // tq),
        in_specs=[pl.BlockSpec((1, tq, width), lambda bi, j: (bi, j, 0)),
                  pl.BlockSpec((1, p, width), lambda bi, j: (bi, 0, 0), pipeline_mode=pl.Buffered(1)),
                  pl.BlockSpec((1, p, width), lambda bi, j: (bi, 0, 0), pipeline_mode=pl.Buffered(1))],
        out_specs=pl.BlockSpec((1, tq, out_w), lambda bi, j: (bi, j, 0)),
        out_shape=jax.ShapeDtypeStruct((b, s_len, out_w), BF16),
        scratch_shapes=[state(tq), state(tq)],
        compiler_params=_cparams(("arbitrary", "arbitrary")),
        name="mla_attention",
    )(q, k, v)
    cblk = s_len // n_ctx
    ctx_rows = lambda w: pl.BlockSpec((1, n_ctx, w), lambda bi: (bi, cblk, 0))
    y_ctx = pl.pallas_call(
        functools.partial(_mla_kernel, tk=tk, n_main=0, tail_rows=n_ctx),
        grid=(b,),
        in_specs=[ctx_rows(width), ctx_rows(width), ctx_rows(width)],
        out_specs=pl.BlockSpec((1, n_ctx, out_w), lambda bi: (bi, 0, 0)),
        out_shape=jax.ShapeDtypeStruct((b, n_ctx, out_w), BF16),
        scratch_shapes=[state(n_ctx), state(n_ctx)],
        compiler_params=_cparams(("arbitrary",)),
        name="mla_attention_ctx",
    )(q, k, v)
    return y_lat, y_ctx


def _swa_kernel(sink_ref, q_ref, k_ref, v_ref, y_ref, *, n_ctx):
    p_len = k_ref.shape[1]
    s_len = p_len - n_ctx
    band = 3 * SWA_BLOCK
    n_lat_tiles = s_len // SWA_BLOCK
    first = lax.broadcasted_iota(I32, (SWA_BLOCK, 128), 1) < SWA_HEAD_DIM
    top = lax.broadcasted_iota(I32, (2 * SWA_BLOCK, 1), 0) < SWA_BLOCK
    n_blocks = q_ref.shape[1] // SWA_BLOCK
    chains = [(t, hk, par) for t in range(n_blocks) for hk in range(SWA_KV_HEADS) for par in range(2)]
    kstart, valid = {}, {}
    for t in range(n_blocks):
        j = pl.program_id(1) * n_blocks + t
        n = jnp.minimum(j, n_lat_tiles - 1)
        ws = jnp.clip((n - 1) * SWA_BLOCK, 0, s_len - band)
        kstart[t] = pl.multiple_of(ws, SWA_BLOCK)
        qpos = n * SWA_BLOCK + lax.broadcasted_iota(I32, (2 * SWA_BLOCK, band), 0) % SWA_BLOCK
        kpos = ws + lax.broadcasted_iota(I32, (2 * SWA_BLOCK, band), 1)
        valid[t] = (jnp.abs(qpos - kpos) <= SWA_WINDOW) & (j < n_lat_tiles)
    scores = {}
    for t, hk, par in chains:
        qrows = slice(t * SWA_BLOCK, (t + 1) * SWA_BLOCK)
        qpair = jnp.concatenate([q_ref[0, qrows, (2 * hk) * 128:(2 * hk + 1) * 128],
                                 q_ref[0, qrows, (2 * hk + 1) * 128:(2 * hk + 2) * 128]], axis=0)
        kcols = slice((2 * hk + par) * 128, (2 * hk + par + 1) * 128)
        s_c = _dot_nt(qpair, k_ref[0, s_len:p_len, kcols])
        s_b = jnp.where(valid[t], _dot_nt(qpair, k_ref[0, pl.ds(kstart[t], band), kcols]), MASK_VALUE)
        scores[t, hk, par] = (s_c, s_b)
    probs = {}
    for t, hk, par in chains:
        s_c, s_b = scores[t, hk, par]
        sink = jnp.where(top, sink_ref[4 * hk + par] * LOG2_E, sink_ref[4 * hk + 2 + par] * LOG2_E)
        m = jnp.maximum(jnp.maximum(jnp.max(s_c, axis=1, keepdims=True), jnp.max(s_b, axis=1, keepdims=True)), sink)
        probs[t, hk, par] = (jnp.exp2(s_c - m).astype(BF16), jnp.exp2(s_b - m).astype(BF16), jnp.exp2(sink - m))
    res = {}
    for t, hk, par in chains:
        p_c, p_b, p_sink = probs[t, hk, par]
        vcols = slice(hk * 128, (hk + 1) * 128)
        o = _dot(p_c, v_ref[0, s_len:p_len, vcols]) + _dot(p_b, v_ref[0, pl.ds(kstart[t], band), vcols])
        o = o / (o[:, SWA_HEAD_DIM:SWA_HEAD_DIM + 1] + p_sink)
        res[t, 4 * hk + par] = o[0:SWA_BLOCK]
        res[t, 4 * hk + 2 + par] = o[SWA_BLOCK:2 * SWA_BLOCK]
    for t in range(n_blocks):
        for pair in range(SWA_Q_HEADS // 2):
            y_ref[0, t * SWA_BLOCK:(t + 1) * SWA_BLOCK, pair * 128:(pair + 1) * 128] = jnp.where(
                first, res[t, 2 * pair], pltpu.roll(res[t, 2 * pair + 1], SWA_HEAD_DIM, 1)).astype(BF16)


def _swa_attention(sink, qs, ks, vs, n_ctx):
    b, p, _ = qs.shape
    tq = SWA_Q_TILE
    assert n_ctx % tq == 0 and (p - n_ctx) % tq == 0
    return pl.pallas_call(
        functools.partial(_swa_kernel, n_ctx=n_ctx),
        grid=(b, p // tq),
        in_specs=[pl.BlockSpec(memory_space=pltpu.SMEM),
                  pl.BlockSpec((1, tq, SWA_WIDTH), lambda bi, j: (bi, j, 0)),
                  pl.BlockSpec((1, p, ks.shape[2]), lambda bi, j: (bi, 0, 0)),
                  pl.BlockSpec((1, p, vs.shape[2]), lambda bi, j: (bi, 0, 0))],
        out_specs=pl.BlockSpec((1, tq, SWA_WIDTH), lambda bi, j: (bi, j, 0)),
        out_shape=jax.ShapeDtypeStruct((b, p, SWA_WIDTH), BF16),
        compiler_params=_cparams(("arbitrary", "arbitrary")),
        name="swa_attention",
    )(sink, qs, ks, vs)


def _pool_kernel(prev_ref, cur_ref, next_ref, w_ref, scale_ref, y_ref, ext_ref, *, n_ctx):
    j = pl.program_id(1)
    tm = cur_ref.shape[1]
    nj = pl.num_programs(1)
    s_len = (nj - 1) * tm
    is_ctx = j == nj - 1
    has_prev = (j >= 1) & (j < nj - 1)
    has_next = j < nj - 2
    ext_ref[0:POOL_HALO, :] = jnp.where(has_prev, prev_ref[0], 0.0)
    ext_ref[POOL_HALO:POOL_HALO + tm, :] = cur_ref[0]
    ext_ref[POOL_HALO + tm:POOL_HALO + tm + POOL_HALO, :] = jnp.where(has_next, next_ref[0], 0.0)
    t = lax.broadcasted_iota(I32, (tm, 1), 0)
    pos = jnp.where(is_ctx, t, j * tm + t)
    seg_len = jnp.where(is_ctx, n_ctx, s_len)
    for g, w in enumerate(POOL_WINDOWS):
        cols = slice(g * POOL_GROUP_DIM, (g + 1) * POOL_GROUP_DIM)
        acc = jnp.zeros((tm, POOL_GROUP_DIM), F32)
        for off in range(-(w // 2), w - w // 2):
            acc = acc + ext_ref[POOL_HALO + off:POOL_HALO + off + tm, cols]
        lo = jnp.maximum(pos - w // 2, 0)
        hi = jnp.minimum(pos + w - w // 2, seg_len)
        cnt = (hi - lo).astype(F32)
        pooled = acc / cnt - cur_ref[0, :, cols]
        y_ref[0, :, cols] = (_dot(pooled.astype(BF16), w_ref[g]) * scale_ref[:, cols]).astype(BF16)


def _pool_stage(u, w_pool, pool_scale, n_ctx):
    b, p, width = u.shape
    tm = TOKEN_TILE
    hb = tm // POOL_HALO
    n_halo_blocks = p // POOL_HALO
    return pl.pallas_call(
        functools.partial(_pool_kernel, n_ctx=n_ctx),
        grid=(b, p // tm),
        in_specs=[pl.BlockSpec((1, POOL_HALO, width), lambda bi, j: (bi, jnp.maximum(j * hb - 1, 0), 0)),
                  pl.BlockSpec((1, tm, width), lambda bi, j: (bi, j, 0)),
                  pl.BlockSpec((1, POOL_HALO, width),
                               lambda bi, j: (bi, jnp.minimum((j + 1) * hb, n_halo_blocks - 1), 0)),
                  pl.BlockSpec(w_pool.shape, lambda bi, j: (0, 0, 0)),
                  pl.BlockSpec(pool_scale.shape, lambda bi, j: (0, 0))],
        out_specs=pl.BlockSpec((1, tm, width), lambda bi, j: (bi, j, 0)),
        out_shape=jax.ShapeDtypeStruct((b, p, width), BF16),
        scratch_shapes=[pltpu.VMEM((tm + 2 * POOL_HALO, width), F32)],
        compiler_params=_cparams(("arbitrary", "arbitrary")),
        name="pool_stage",
    )(u, u, u, w_pool, pool_scale)


def _merge_kernel(*refs, n_stream, ctx_tile):
    (yml_ref, ymc_ref, yp_ref, ys_ref, gate_ref, mod_ref, g_ref, wbm_ref, wbp_ref, wbs_ref,
     wout_ref, wsg_ref, wsu_ref, wsd_ref, wrt_ref, xs_ref, fp_ref, lg_ref) = refs[n_stream:]
    d = SLAB_ROWS * 128
    mod = mod_ref[0, 0]
    gate = gate_ref[0]
    ym = jnp.where(pl.program_id(1) == ctx_tile, ymc_ref[0], yml_ref[0])
    merged = (gate[:, 0:d].astype(F32) * _dot(ym, wbm_ref[...])
              + gate[:, d:2 * d].astype(F32) * _dot(yp_ref[0], wbp_ref[...])
              + gate[:, 2 * d:3 * d].astype(F32) * _dot(ys_ref[0], wbs_ref[...]))
    x = _stream_tile(refs[:n_stream], gate.shape[0], ctx_tile)
    x_mid = x + mod[2:3] * _dot(merged.astype(BF16), wout_ref[...])
    f = _rms(x_mid, g_ref[...]) * (1.0 + mod[4:5]) + mod[3:4]
    fb = f.astype(BF16)
    gsh = _dot(fb, wsg_ref[...])
    shared = _dot((gsh * _sigmoid(gsh) * _dot(fb, wsu_ref[...])).astype(BF16), wsd_ref[...])
    _matrix_to_slab_rows(xs_ref, x_mid + mod[5:6] * shared, SLAB_ROWS)
    _matrix_to_slab_rows(fp_ref, _pack_bf16_pair(f), PACKED_ROWS)
    lg_ref[...] = _dot_nt(wrt_ref[...], fb)


def _merge_stage(stream, ym_lat, ym_ctx, yp, ys, gates, modtab, g_ffn, wbm, wbp, wbs, wout, wsg, wsu, wsd, wrt,
                 keep_ctx):
    b, p, _ = yp.shape
    assert ym_ctx.shape[1] == TOKEN_TILE
    d = SLAB_ROWS * 128
    tm = TOKEN_TILE
    nj = p // tm
    nk = nj if keep_ctx else nj - 1
    tok = lambda w: pl.BlockSpec((1, tm, w), lambda bi, j: (bi, j, 0))
    slab_out = lambda rows: pl.BlockSpec((tm * rows, 128), lambda bi, j: (bi * nk + j, 0))
    full = lambda a: pl.BlockSpec(a.shape, lambda bi, j: (0,) * a.ndim)
    return pl.pallas_call(
        functools.partial(_merge_kernel, n_stream=len(stream), ctx_tile=nj - 1),
        grid=(b, nk),
        in_specs=_stream_specs(stream, tm, nj) + [
            pl.BlockSpec((1, tm, ym_lat.shape[2]), lambda bi, j: (bi, jnp.minimum(j, nj - 2), 0)),
            pl.BlockSpec((1, tm, ym_ctx.shape[2]), lambda bi, j: (bi, 0, 0)),
            tok(yp.shape[2]), tok(ys.shape[2]), tok(gates.shape[2]),
            pl.BlockSpec((1, 1, 8, d), lambda bi, j: (bi, j // (nj - 1), 0, 0)),
            full(g_ffn), full(wbm), full(wbp), full(wbs), full(wout), full(wsg), full(wsu), full(wsd),
            full(wrt)],
        out_specs=[slab_out(SLAB_ROWS), slab_out(PACKED_ROWS),
                   pl.BlockSpec((N_EXPERTS, tm), lambda bi, j: (0, bi * nk + j))],
        out_shape=[jax.ShapeDtypeStruct((b * nk * tm * SLAB_ROWS, 128), F32),
                   jax.ShapeDtypeStruct((b * nk * tm * PACKED_ROWS, 128), U32),
                   jax.ShapeDtypeStruct((N_EXPERTS, b * nk * tm), F32)],
        compiler_params=_cparams(("arbitrary", "arbitrary")),
        name="merge_stage",
    )(*stream, ym_lat, ym_ctx, yp, ys, gates, modtab, g_ffn, wbm, wbp, wbs, wout, wsg, wsu, wsd, wrt)


def _route_kernel(lg_ref, bias_ref, dest_ref, wts_ref, seg_ref):
    tm = lg_ref.shape[1]
    ne = N_EXPERTS
    neg_inf = -jnp.inf
    scores = _sigmoid(lg_ref[...])
    sel = scores + bias_ref[...]
    iota_g = lax.broadcasted_iota(I32, (EXPERTS_PER_GROUP, tm), 0)
    gscore = []
    for g in range(N_GROUPS):
        sg = sel[g * EXPERTS_PER_GROUP:(g + 1) * EXPERTS_PER_GROUP]
        m1 = jnp.max(sg, axis=0, keepdims=True)
        i1 = jnp.min(jnp.where(sg == m1, iota_g, EXPERTS_PER_GROUP), axis=0, keepdims=True)
        m2 = jnp.max(jnp.where(iota_g == i1, neg_inf, sg), axis=0, keepdims=True)
        gscore.append(m1 + m2)
    rows = []
    for g in range(N_GROUPS):
        rank = jnp.zeros((1, tm), I32)
        for g2 in range(N_GROUPS):
            if g2 == g:
                continue
            beats = (gscore[g2] >= gscore[g]) if g2 < g else (gscore[g2] > gscore[g])
            rank = rank + beats.astype(I32)
        rows.append(jnp.where(rank < TOPK_GROUPS, sel[g * EXPERTS_PER_GROUP:(g + 1) * EXPERTS_PER_GROUP], neg_inf))
    cur = jnp.concatenate(rows, axis=0)
    iota_e = lax.broadcasted_iota(I32, (ne, tm), 0)
    picks = []
    member = jnp.zeros((ne, tm), F32)
    for _ in range(TOP_K):
        m = jnp.max(cur, axis=0, keepdims=True)
        idx = jnp.min(jnp.where(cur == m, iota_e, ne), axis=0, keepdims=True)
        hit = iota_e == idx
        picks.append(hit)
        member = member + hit.astype(F32)
        cur = jnp.where(hit, neg_inf, cur)
    earlier = (lax.broadcasted_iota(I32, (tm, tm), 0) < lax.broadcasted_iota(I32, (tm, tm), 1)).astype(BF16)
    pos = _dot(member.astype(BF16), earlier)
    cnt_col = jnp.sum(member, axis=1, keepdims=True)
    blocks_col = jnp.floor((cnt_col + (SEG_ALIGN - 1)) * (1.0 / SEG_ALIGN))
    lower = (lax.broadcasted_iota(I32, (ne, ne), 1) < lax.broadcasted_iota(I32, (ne, ne), 0)).astype(BF16)
    off_col = _dot(lower, jnp.broadcast_to(blocks_col, (ne, 128)).astype(BF16))[:, 0:1] * SEG_ALIGN
    base = off_col + pos
    w_rows = [jnp.sum(jnp.where(hit, scores, 0.0), axis=0, keepdims=True) for hit in picks]
    denom = w_rows[0]
    for w in w_rows[1:]:
        denom = denom + w
    for k, hit in enumerate(picks):
        dest_ref[0, k:k + 1, :] = jnp.sum(jnp.where(hit, base, 0.0), axis=0, keepdims=True).astype(I32)
        wts_ref[0, k:k + 1, :] = w_rows[k] / denom * ROUTED_SCALE
    member_pad = jnp.concatenate([member, jnp.zeros((128 - ne, tm), F32)], axis=0).astype(BF16)
    cnt_row = _dot_nt(jnp.ones((8, tm), BF16), member_pad)
    blocks_row = jnp.floor((cnt_row + (SEG_ALIGN - 1)) * (1.0 / SEG_ALIGN))
    before = (lax.broadcasted_iota(I32, (128, 128), 0) < lax.broadcasted_iota(I32, (128, 128), 1)).astype(BF16)
    off_row = _dot(blocks_row.astype(BF16), before) * SEG_ALIGN
    r = lax.broadcasted_iota(I32, (8, 128), 0)
    seg_ref[0] = jnp.where(r == 0, off_row, jnp.where(r == 1, cnt_row, 0.0)).astype(I32)


def _route_stage(lg_t, bias_col):
    ne, t_all = lg_t.shape
    tm = MOE_TILE
    nt = t_all // tm
    return pl.pallas_call(
        _route_kernel,
        grid=(nt,),
        in_specs=[pl.BlockSpec((ne, tm), lambda i: (0, i)),
                  pl.BlockSpec((ne, 1), lambda i: (0, 0))],
        out_specs=[pl.BlockSpec((1, TOP_K, tm), lambda i: (i, 0, 0)),
                   pl.BlockSpec((1, TOP_K, tm), lambda i: (i, 0, 0)),
                   pl.BlockSpec((1, 8, 128), lambda i: (i, 0, 0))],
        out_shape=[jax.ShapeDtypeStruct((nt, TOP_K, tm), I32),
                   jax.ShapeDtypeStruct((nt, TOP_K, tm), F32),
                   jax.ShapeDtypeStruct((nt, 8, 128), I32)],
        compiler_params=_cparams(("arbitrary",)),
        name="route_stage",
    )(lg_t, bias_col)


def _moe_row_stride(tm):
    cap = TOP_K * tm + N_EXPERTS * SEG_ALIGN + MOE_CHUNK
    blocks = cap // 8 + 1
    return 8 * (blocks + 1 - blocks % 2)


def _moe_kernel(dest_ref, wts_ref, seg_ref, fp_hbm, xs_hbm, gt_ref, gfin_ref, wg_ref, wu_ref, wd_ref, out_hbm,
                bufs_ref, fp_stage, xs_stage, out_stage, slab_tmp, fp_sem, xs_sem, out_sem,
                *, srow, sub, n_sub_per_batch, ctx_sub, n_tiles, final_norm):
    pr = pl.program_id(0)
    eg = pl.program_id(1)
    tm = dest_ref.shape[2]
    ch = MOE_CHUNK
    group = wg_ref.shape[0]
    lane_blk = 128
    n_blk = tm // lane_blk
    per = bufs_ref.shape[0]
    n_slots = fp_stage.shape[0]
    n_here = jnp.minimum(per, n_tiles - pr * per)

    def table_block(ref, half, blk):
        return ref.at[half, :, pl.ds(pl.multiple_of(blk * lane_blk, lane_blk), lane_blk)]

    def table_rows(ref, half, blk):
        cols = pl.ds(pl.multiple_of(blk * lane_blk, lane_blk), lane_blk)
        return [ref.at[half, k, cols] for k in range(TOP_K)]

    def token_rows(half, blk, per_tok):
        first = ((pr * per + half) * tm + blk * lane_blk) * per_tok
        return pl.ds(pl.multiple_of(first, lane_blk * per_tok), lane_blk * per_tok)

    def fp_copy(half, blk, slot):
        return pltpu.make_async_copy(fp_hbm.at[token_rows(half, blk, PACKED_ROWS), :], fp_stage.at[slot],
                                     fp_sem.at[slot])

    def xs_copy(half, blk, slot):
        return pltpu.make_async_copy(xs_hbm.at[token_rows(half, blk, SLAB_ROWS), :], xs_stage.at[slot],
                                     xs_sem.at[slot])

    def out_copy(half, blk, slot):
        if not final_norm:
            dst = out_hbm.at[token_rows(half, blk, SLAB_ROWS), :]
        else:
            first = (pr * per + half) * tm + blk * lane_blk
            row_len = out_hbm.shape[1]
            bi = first // row_len
            dst = out_hbm.at[bi, pl.ds(pl.multiple_of(first - bi * row_len, lane_blk), lane_blk), :]
        return pltpu.make_async_copy(out_stage.at[slot], dst, out_sem.at[slot])

    def group_rows(half):
        buf_ref = bufs_ref.at[half]
        for blk in range(n_slots - 1):
            fp_copy(half, blk, blk).start()
        buf_ref[...] = jnp.zeros(buf_ref.shape, U32)

        def body(blk, carry):
            slot = blk % n_slots
            fp_copy(half, blk, slot).wait()

            @pl.when(blk + n_slots - 1 < n_blk)
            def _prefetch():
                fp_copy(half, blk + n_slots - 1, (blk + n_slots - 1) % n_slots).start()

            dest = table_rows(dest_ref, half, blk)
            rows = fp_stage.at[slot]
            for u in range(lane_blk):
                slab = rows[u * PACKED_ROWS:(u + 1) * PACKED_ROWS, :]
                for k in range(TOP_K):
                    buf_ref[pl.ds(dest[k][u], PACKED_ROWS, stride=srow), :] = slab
            return carry

        lax.fori_loop(0, n_blk, body, 0)

    def experts():
        def one_expert(ge, carry):
            e = eg * group + ge
            counts = [seg_ref[half, 1, e] for half in range(per)]
            offsets = [seg_ref[half, 0, e] for half in range(per)]

            def ffn_rows(halves, rows, chunk_idx):
                words, keep = [], []
                for h in halves:
                    r0 = offsets[h] + chunk_idx * rows
                    words.append([bufs_ref[h, pl.ds(pl.multiple_of(q * srow + r0, 8), rows), :] for q in range(4)])
                    keep.append(lax.broadcasted_iota(I32, (rows, 1), 0) < counts[h] - chunk_idx * rows)
                xb = []
                for ws in words:
                    w = jnp.concatenate(ws, axis=1)
                    xb.append(jnp.concatenate([_unpack_lo(w), _unpack_hi(w)], axis=1).astype(BF16))
                gates = [(_dot(x, wg_ref[ge]), _dot(x, wu_ref[ge])) for x in xb]
                hmid = [(g * _sigmoid(g) * u).astype(BF16) for g, u in gates]
                packed = [_pack_bf16_pair(_dot(hm, wd_ref[ge])) for hm in hmid]
                for n, h in enumerate(halves):
                    r0 = offsets[h] + chunk_idx * rows
                    for q in range(4):
                        bufs_ref[h, pl.ds(pl.multiple_of(q * srow + r0, 8), rows), :] = jnp.where(
                            keep[n], packed[n][:, q * 128:(q + 1) * 128], words[n][q])

            most = counts[0]
            for c in counts[1:]:
                most = jnp.maximum(most, c)
            lo = 0
            for rows in MOE_BODY_ROWS:
                pl.when((most > lo) & (most <= rows))(functools.partial(ffn_rows, tuple(range(per)), rows, 0))
                lo = rows

            @pl.when(most > ch)
            def _long_segments():
                for h in range(per):
                    def chunk(c, carry2, h=h):
                        ffn_rows((h,), ch, c)
                        return carry2

                    lax.fori_loop(0, (counts[h] + ch - 1) // ch, chunk, 0)

            return carry

        lax.fori_loop(0, group, one_expert, 0)

    def combine(half):
        buf_ref = bufs_ref.at[half]
        for blk in range(n_slots - 1):
            xs_copy(half, blk, blk).start()

        def body(blk, carry):
            slot = blk % n_slots
            xs_copy(half, blk, slot).wait()

            @pl.when(blk + n_slots - 1 < n_blk)
            def _prefetch():
                xs_copy(half, blk + n_slots - 1, (blk + n_slots - 1) % n_slots).start()

            @pl.when(blk >= n_slots)
            def _slot_free():
                out_copy(half, blk - n_slots, slot).wait()

            sub_blk = ((pr * per + half) * tm + blk * lane_blk) // sub
            bi = sub_blk // n_sub_per_batch
            gate = gt_ref[bi * 2 + (sub_blk - bi * n_sub_per_batch) // ctx_sub]
            dest = table_block(dest_ref, half, blk)
            wts = table_block(wts_ref, half, blk)
            xs = xs_stage.at[slot]
            out = slab_tmp if final_norm else out_stage.at[slot]
            for u in range(lane_blk):
                acc_lo = jnp.zeros((4, 128), F32)
                acc_hi = jnp.zeros((4, 128), F32)
                for k in range(TOP_K):
                    words = buf_ref[pl.ds(dest[k, u], PACKED_ROWS, stride=srow), :]
                    wk = wts[k, u]
                    acc_lo = acc_lo + wk * _unpack_lo(words)
                    acc_hi = acc_hi + wk * _unpack_hi(words)
                out[u * 8:u * 8 + 4, :] = xs[u * 8:u * 8 + 4, :] + gate[0:4] * acc_lo
                out[u * 8 + 4:u * 8 + 8, :] = xs[u * 8 + 4:u * 8 + 8, :] + gate[4:8] * acc_hi
            if final_norm:
                out_stage[slot] = _rms(_slab_rows_to_matrix(slab_tmp, lane_blk, SLAB_ROWS), gfin_ref[...])
            out_copy(half, blk, slot).start()
            return carry

        lax.fori_loop(0, n_blk, body, 0)
        for blk in range(n_blk - n_slots, n_blk):
            out_copy(half, blk, blk % n_slots).wait()

    def each_tile(fn):
        def body(half, carry):
            fn(half)
            return carry

        lax.fori_loop(0, n_here, body, 0)

    @pl.when(eg == 0)
    def _first_step():
        each_tile(group_rows)
        for half in range(1, per):
            @pl.when(half >= n_here)
            def _clear(half=half):
                bufs_ref[half] = jnp.zeros(bufs_ref.shape[1:], U32)

    experts()
    pl.when(eg == pl.num_programs(1) - 1)(functools.partial(each_tile, combine))


def _moe_stage(dest, wts, seg, fp4, xs8, gt2, g_final, wg, wu, wd, layer, n_sub_per_batch, keep_ctx, final_norm):
    ctx_sub = n_sub_per_batch - 1 if keep_ctx else n_sub_per_batch
    d = SLAB_ROWS * 128
    tokens_per_row = n_sub_per_batch * TOKEN_TILE
    out_shape = (xs8.shape[0] // (SLAB_ROWS * tokens_per_row), tokens_per_row, d) if final_norm else xs8.shape
    out_block = (128, d) if final_norm else (128 * SLAB_ROWS, 128)
    nt, _, tm = dest.shape
    ne = wg.shape[1]
    group = MOE_EXPERTS_PER_STEP
    per = MOE_TILES_PER_STEP
    assert ne % group == 0 and tm % 256 == 0 and TOKEN_TILE % 128 == 0 and per == 2
    n_rows = -(-nt // per)
    pad = ((0, n_rows * per - nt), (0, 0), (0, 0))
    dest, wts, seg = jnp.pad(dest, pad), jnp.pad(wts, pad), jnp.pad(seg, pad)
    srow = _moe_row_stride(tm)
    smem = lambda shape: pl.BlockSpec(shape, lambda i, e: (i, 0, 0), memory_space=pltpu.SMEM)
    hbm = pl.BlockSpec(memory_space=pl.ANY)
    slots = MOE_STAGE_SLOTS
    assert tm // 128 >= slots
    stage = lambda per_tok, dt: pltpu.VMEM((slots, 128 * per_tok, 128), dt)
    return pl.pallas_call(
        functools.partial(_moe_kernel, srow=srow, sub=TOKEN_TILE, n_sub_per_batch=n_sub_per_batch, ctx_sub=ctx_sub,
                          n_tiles=nt, final_norm=final_norm),
        grid=(n_rows, ne // group),
        in_specs=[smem((per, TOP_K, tm)), smem((per, TOP_K, tm)), smem((per, 8, 128)),
                  hbm, hbm,
                  pl.BlockSpec(gt2.shape, lambda i, e: (0, 0, 0)),
                  pl.BlockSpec(g_final.shape, lambda i, e: (0, 0)),
                  pl.BlockSpec((None, group) + wg.shape[2:], lambda i, e: (layer, e, 0, 0)),
                  pl.BlockSpec((None, group) + wu.shape[2:], lambda i, e: (layer, e, 0, 0)),
                  pl.BlockSpec((None, group) + wd.shape[2:], lambda i, e: (layer, e, 0, 0))],
        out_specs=hbm,
        out_shape=jax.ShapeDtypeStruct(out_shape, F32),
        scratch_shapes=[pltpu.VMEM((per, 4 * srow, 128), U32),
                        stage(PACKED_ROWS, U32), stage(SLAB_ROWS, F32), pltpu.VMEM((slots,) + out_block, F32),
                        pltpu.VMEM((128 * SLAB_ROWS, 128), F32),
                        pltpu.SemaphoreType.DMA((slots,)), pltpu.SemaphoreType.DMA((slots,)),
                        pltpu.SemaphoreType.DMA((slots,))],
        compiler_params=_cparams(("arbitrary", "arbitrary")),
        name="moe_stage",
    )(dest, wts, seg, fp4, xs8, gt2, g_final, wg, wu, wd)


def _rot_lanes(x, head_dim):
    quarter = head_dim // 4
    lane = lax.broadcasted_iota(I32, x.shape, 1)
    width = x.shape[1]
    return jnp.where(lane % (2 * quarter) < quarter, -pltpu.roll(x, width - quarter, 1), pltpu.roll(x, quarter, 1))


def _w1_kernel(w_ref, o_ref):
    offs = [0]
    for w in (MLA_Q_LORA, MLA_KV_LORA, MLA_ROPE, POOL_WIDTH, SWA_WIDTH, SWA_KV_WIDTH, SWA_KV_WIDTH):
        offs.append(offs[-1] + w)
    o_cq, o_ckv, o_kr, o_u, o_qs, o_ks, o_vs, o_gl = offs
    lane = lax.broadcasted_iota(I32, (w_ref.shape[0], 128), 1)

    def put(name, val):
        a, b = _SEG[name]
        o_ref[:, a:b] = val.astype(BF16)

    def kv_slots(x):
        low = lane < SWA_HEAD_DIM
        return jnp.concatenate([jnp.where(low, x, 0.0), jnp.where(low, pltpu.roll(x, SWA_HEAD_DIM, 1), 0.0)], axis=1)

    def kr_slot(x):
        return jnp.where((lane >= MLA_NOPE) & (lane < MLA_NOPE + MLA_ROPE), pltpu.roll(x, MLA_NOPE, 1), 0.0)

    put("cq", w_ref[:, o_cq:o_ckv])
    put("ckv", w_ref[:, o_ckv:o_kr])
    put("u", w_ref[:, o_u:o_qs])
    put("qs", w_ref[:, o_qs:o_ks] * (SWA_SCALE * LOG2_E))
    put("ks", kv_slots(w_ref[:, o_ks:o_vs]))
    put("vs", kv_slots(w_ref[:, o_vs:o_gl]))
    put("kr", kr_slot(w_ref[:, o_kr:o_kr + 128]))
    put("gl", w_ref[:, o_gl:w_ref.shape[1]])


def _fused_in_weight(w_in, layer):
    _, d, width = w_in.shape
    tr = 256
    assert d % tr == 0 and SWA_KV_WIDTH == 128 and MLA_KV_LORA + MLA_Q_LORA == 5 * 128
    return pl.pallas_call(
        _w1_kernel,
        grid=(d // tr,),
        in_specs=[pl.BlockSpec((None, tr, width), lambda i: (layer, i, 0))],
        out_specs=pl.BlockSpec((tr, FUSED_IN_WIDTH), lambda i: (i, 0)),
        out_shape=jax.ShapeDtypeStruct((d, FUSED_IN_WIDTH), BF16),
        compiler_params=_cparams(("arbitrary",)),
        name="fused_in_weight",
    )(w_in)


def _rope_pattern(s_len, n_ctx, rot_dim):
    t = jnp.arange(s_len)
    row = (t // GRID_W).astype(F32)
    col = (t % GRID_W).astype(F32)
    n_freq = rot_dim // 4
    inv_freq = ROPE_BASE ** (-jnp.arange(n_freq, dtype=F32) / n_freq)
    ang_r = row[:, None] * inv_freq[None, :]
    ang_c = col[:, None] * inv_freq[None, :]
    cos = jnp.concatenate([jnp.cos(ang_r), jnp.cos(ang_r), jnp.cos(ang_c), jnp.cos(ang_c)], axis=1)
    sin = jnp.concatenate([jnp.sin(ang_r), jnp.sin(ang_r), jnp.sin(ang_c), jnp.sin(ang_c)], axis=1)
    cos = jnp.concatenate([cos, jnp.ones((n_ctx, rot_dim), F32)], axis=0)
    sin = jnp.concatenate([sin, jnp.zeros((n_ctx, rot_dim), F32)], axis=0)
    return cos, sin


def _layer_weights(w_in, layer, w_uq, w_ukv):
    w1 = _fused_in_weight(w_in, layer)
    lq = w_uq.shape[0]
    wq3 = (w_uq * (MLA_SCALE * LOG2_E)).reshape(lq, MLA_HEADS, MLA_NOPE + MLA_ROPE)
    pad = jnp.zeros((lq, MLA_HEADS, MLA_HEAD_PAD - MLA_NOPE - MLA_ROPE), F32)
    wq = jnp.concatenate([wq3, pad], axis=2).reshape(lq, MLA_HEADS * MLA_HEAD_PAD).astype(BF16)
    lkv = w_ukv.shape[0]
    wkv3 = w_ukv.reshape(lkv, MLA_HEADS, MLA_NOPE + MLA_V)
    wkn = jnp.concatenate([wkv3[:, :, :MLA_NOPE], jnp.zeros((lkv, MLA_HEADS, MLA_HEAD_PAD - MLA_NOPE), F32)],
                          axis=2).reshape(lkv, MLA_HEADS * MLA_HEAD_PAD).astype(BF16)
    wv = jnp.concatenate([wkv3[:, :, MLA_NOPE:], jnp.zeros((lkv, MLA_HEADS, MLA_HEAD_PAD - MLA_V), F32)],
                         axis=2).reshape(lkv, MLA_HEADS * MLA_HEAD_PAD).astype(BF16)
    return w1, wq, wkn, wv


def kernel(x, c, ctx, c_ctx, w_mod, b_mod, g_mix, g_ffn, w_in, g_mla_q, g_mla_kv, w_mla_uq, w_mla_ukv, w_pool,
           pool_scale, swa_sink, w_br_mla, w_br_pool, w_br_swa, w_out, w_router, router_bias, w_exp_gate,
           w_exp_up, w_exp_down, w_sh_gate, w_sh_up, w_sh_down, g_final):
    b, s_len, d = x.shape
    n_ctx = ctx.shape[1]
    n_layers = w_mod.shape[0]
    p = n_ctx + s_len
    assert n_ctx == TOKEN_TILE and s_len % TOKEN_TILE == 0 and (b * p) % MOE_TILE == 0 and b + 1 <= 8
    assert d == 1024 and w_in.shape[2] == (MLA_Q_LORA + MLA_KV_LORA + MLA_ROPE + POOL_WIDTH + SWA_WIDTH
                                           + 2 * SWA_KV_WIDTH + 3 * d)

    cm, sm = _rope_pattern(s_len, n_ctx, MLA_ROPE)
    tail = jnp.zeros((p, MLA_HEAD_PAD - MLA_NOPE - MLA_ROPE), F32)
    tabm = jnp.concatenate([jnp.ones((p, MLA_NOPE), F32), cm, tail, jnp.zeros((p, MLA_NOPE), F32), sm, tail], axis=1)
    cs, ss = _rope_pattern(s_len, n_ctx, SWA_HEAD_DIM)
    tabs = jnp.concatenate([cs, cs, ss, ss], axis=1)

    cvec = jnp.concatenate([c, c_ctx[None, :], jnp.zeros((8 - b - 1, d), F32)], axis=0)
    mod_all = _modulation(cvec, w_mod, b_mod).reshape(n_layers, 8, N_MOD, d)

    stream = (x, ctx)
    wg_all, wu_all, wd_all = w_exp_gate.astype(BF16), w_exp_up.astype(BF16), w_exp_down.astype(BF16)
    for i in range(n_layers):
        keep_ctx = i < n_layers - 1
        lat = mod_all[i, :b]
        ctx_rows = jnp.broadcast_to(mod_all[i, b][None], (b, N_MOD, d))
        modtab = jnp.pad(jnp.stack([lat, ctx_rows], axis=1), ((0, 0), (0, 0), (0, 8 - N_MOD), (0, 0)))
        w1, wq, wkn, wv = _layer_weights(w_in, i, w_mla_uq[i], w_mla_ukv[i])
        q, k, v, u, qs, ks, vs, gates = _input_stage(
            stream, b, modtab, g_mix[i][None], w1, g_mla_q[i][None], wq, g_mla_kv[i][None], wkn, wv, tabm, tabs)
        y_mla, y_mla_ctx = _mla_attention(q, k, v, n_ctx)
        y_swa = _swa_attention(swa_sink[i], qs, ks, vs, n_ctx)
        y_pool = _pool_stage(u, w_pool[i].astype(BF16), pool_scale[i][None], n_ctx)
        xs, fp, lg_t = _merge_stage(
            stream, y_mla, y_mla_ctx, y_pool, y_swa, gates, modtab, g_ffn[i][None], w_br_mla[i].astype(BF16),
            w_br_pool[i].astype(BF16), w_br_swa[i].astype(BF16), w_out[i].astype(BF16),
            w_sh_gate[i].astype(BF16), w_sh_up[i].astype(BF16), w_sh_down[i].astype(BF16),
            w_router[i].T.astype(BF16), keep_ctx)
        dest, wts, seg = _route_stage(lg_t, router_bias[i][:, None])
        gt2 = modtab[:, :, 5, :].reshape(b * 2, 8, d // 8)
        stream = (_moe_stage(dest, wts, seg, fp, xs, gt2, g_final[None], wg_all, wu_all, wd_all, i,
                             (p if keep_ctx else s_len) // TOKEN_TILE, keep_ctx, final_norm=not keep_ctx),)
    return stream[0]
```

```python
import functools

import jax
import jax.numpy as jnp
from jax import lax
from jax.experimental import pallas as pl
from jax.experimental.pallas import tpu as pltpu

F32 = jnp.float32
BF16 = jnp.bfloat16
U32 = jnp.uint32
I32 = jnp.int32

NORM_EPS = 1e-6
ROPE_BASE = 10000.0
GRID_W = 64
N_MOD = 6

MLA_HEADS = 8
MLA_Q_LORA = 384
MLA_KV_LORA = 256
MLA_NOPE = 64
MLA_ROPE = 32
MLA_V = 64
MLA_SCALE = (MLA_NOPE + MLA_ROPE) ** -0.5
MLA_HEAD_PAD = 128
LOG2_E = 1.4426950408889634

POOL_WINDOWS = (2, 4, 8, 16)
POOL_GROUP_DIM = 128
POOL_WIDTH = 512
POOL_HALO = 8

SWA_Q_HEADS = 8
SWA_KV_HEADS = 2
SWA_HEAD_DIM = 64
SWA_WINDOW = 128
SWA_BLOCK = 128
SWA_SCALE = SWA_HEAD_DIM ** -0.5
SWA_WIDTH = SWA_Q_HEADS * SWA_HEAD_DIM
SWA_KV_WIDTH = SWA_KV_HEADS * SWA_HEAD_DIM

N_EXPERTS = 64
TOP_K = 8
N_GROUPS = 8
TOPK_GROUPS = 4
EXPERTS_PER_GROUP = 8
D_EXPERT = 256
ROUTED_SCALE = 2.5

TOKEN_TILE = 256
SWA_Q_TILE = 256
MLA_Q_TILE = 1024
MLA_KEY_CHUNK = 256
MOE_TILE = 1024
MOE_CHUNK = 256
MOE_BODY_ROWS = (128, 144, 160, 176, 192, 224, 256)
MOE_EXPERTS_PER_STEP = 2
MOE_TILES_PER_STEP = 2
MOE_STAGE_SLOTS = 4
SEG_ALIGN = 8
SLAB_ROWS = 8
PACKED_ROWS = 4
MASK_VALUE = -1e30
HI16 = 0xFFFF0000

VMEM_LIMIT = 56 * 1024 * 1024

_SEG_WIDTHS = (("cq", 384), ("ckv", 256), ("u", 512), ("qs", 512), ("ks", 256), ("vs", 256), ("kr", 128),
               ("gl", 3072))
_SEG = {}
_o = 0
for _n, _w in _SEG_WIDTHS:
    _SEG[_n] = (_o, _o + _w)
    _o += _w
FUSED_IN_WIDTH = _o


def _cparams(sem):
    return pltpu.CompilerParams(dimension_semantics=sem, vmem_limit_bytes=VMEM_LIMIT)


def _dot(a, b):
    return jnp.dot(a, b, preferred_element_type=F32)


def _dot_nt(a, b):
    return lax.dot_general(a, b, (((1,), (1,)), ((), ())), preferred_element_type=F32)


def _sigmoid(x):
    return 1.0 / (1.0 + jnp.exp(-x))


def _rms(x, g):
    return x * lax.rsqrt(jnp.mean(x * x, axis=-1, keepdims=True) + NORM_EPS) * g


def _pack_bf16_pair(v):
    n = v.shape[1] // 2
    bits = pltpu.bitcast(v.astype(BF16).astype(F32), U32)
    return (bits[:, :n] >> 16) | (bits[:, n:] & jnp.uint32(HI16))


def _slab_rows_to_matrix(ref, n_tok, per_tok):
    return jnp.concatenate([ref[pl.ds(c, n_tok, stride=per_tok), :] for c in range(per_tok)], axis=1)


def _matrix_to_slab_rows(ref, val, per_tok):
    n_tok = val.shape[0]
    for c in range(per_tok):
        ref[pl.ds(c, n_tok, stride=per_tok), :] = val[:, c * 128:(c + 1) * 128]


def _unpack_lo(w):
    return pltpu.bitcast(w << 16, F32)


def _unpack_hi(w):
    return pltpu.bitcast(w & jnp.uint32(HI16), F32)


def _mod_kernel(c_ref, w_ref, b_ref, o_ref):
    c = c_ref[...]
    a = (c * _sigmoid(c)).astype(BF16)
    o_ref[0] = _dot(a, w_ref[0].astype(BF16)) + b_ref[0]


def _modulation(cvec, w_mod, b_mod):
    n_layers, d, width = w_mod.shape
    tn = width // 4
    return pl.pallas_call(
        _mod_kernel,
        grid=(n_layers, width // tn),
        in_specs=[pl.BlockSpec((8, d), lambda l, n: (0, 0)),
                  pl.BlockSpec((1, d, tn), lambda l, n: (l, 0, n)),
                  pl.BlockSpec((1, 1, tn), lambda l, n: (l, 0, n))],
        out_specs=pl.BlockSpec((1, 8, tn), lambda l, n: (l, 0, n)),
        out_shape=jax.ShapeDtypeStruct((n_layers, 8, width), F32),
        compiler_params=_cparams(("arbitrary", "arbitrary")),
        name="modulation",
    )(cvec, w_mod, b_mod.reshape(n_layers, 1, width))


def _stream_specs(stream, tm, nj):
    if len(stream) == 1:
        return [pl.BlockSpec((tm * SLAB_ROWS, 128), lambda bi, j: (bi * nj + j, 0))]
    d = stream[0].shape[2]
    return [pl.BlockSpec((1, tm, d), lambda bi, j: (bi, jnp.minimum(j, nj - 2), 0)),
            pl.BlockSpec((1, tm, d), lambda bi, j: (bi, 0, 0))]


def _stream_tile(refs, n_tok, is_ctx):
    if len(refs) == 1:
        return _slab_rows_to_matrix(refs[0], n_tok, SLAB_ROWS)
    return jnp.where(is_ctx, refs[1][0], refs[0][0])


def _stream_tokens(stream, b):
    return stream[0].shape[0] // (b * SLAB_ROWS) if len(stream) == 1 else stream[0].shape[1] + stream[1].shape[1]


def _in_kernel(*refs, n_stream, ctx_tile):
    (mod_ref, g_ref, w1_ref, gq_ref, wq_ref, gkv_ref, wkn_ref, wv_ref, tabm_ref, tabs_ref,
     q_ref, k_ref, v_ref, u_ref, qs_ref, ks_ref, vs_ref, gate_ref) = refs[n_stream:]
    x = _stream_tile(refs[:n_stream], q_ref.shape[1], pl.program_id(1) == ctx_tile)
    mod = mod_ref[0, 0]
    h = _rms(x, g_ref[...]) * (1.0 + mod[1:2]) + mod[0:1]
    hb = h.astype(BF16)

    def seg(name):
        a, b = _SEG[name]
        return _dot(hb, w1_ref[:, a:b])

    cos_m = tabm_ref[:, 0:128]
    sin_m = tabm_ref[:, 128:256]
    cos_s = tabs_ref[:, 0:128]
    sin_s = tabs_ref[:, 128:256]

    def rotary(val, cos, sin, head_dim):
        tiles = [val[:, c:c + 128] for c in range(0, val.shape[1], 128)]
        return jnp.concatenate([t * cos + _rot_lanes(t, head_dim) * sin for t in tiles], axis=1)

    cqn = _rms(seg("cq"), gq_ref[...]).astype(BF16)
    q_ref[0] = rotary(_dot(cqn, wq_ref[...]), cos_m, sin_m, MLA_ROPE).astype(BF16)

    ckvn = _rms(seg("ckv"), gkv_ref[...]).astype(BF16)
    kr = rotary(seg("kr"), cos_m, sin_m, MLA_ROPE)
    k_ref[0] = (_dot(ckvn, wkn_ref[...]) + jnp.tile(kr, (1, MLA_HEADS))).astype(BF16)
    ones_lane = (lax.broadcasted_iota(I32, (1, MLA_HEADS * MLA_HEAD_PAD), 1) % MLA_HEAD_PAD == MLA_V).astype(F32)
    v_ref[0] = (_dot(ckvn, wv_ref[...]) + ones_lane).astype(BF16)

    u_ref[0] = seg("u")

    qs_ref[0] = rotary(seg("qs"), cos_s, sin_s, SWA_HEAD_DIM).astype(BF16)
    ks = rotary(seg("ks"), cos_s, sin_s, SWA_HEAD_DIM)
    for hk in range(SWA_KV_HEADS):
        k_lo = ks[:, hk * 128:(hk + 1) * 128]
        ks_ref[0, :, (2 * hk) * 128:(2 * hk + 1) * 128] = k_lo.astype(BF16)
        ks_ref[0, :, (2 * hk + 1) * 128:(2 * hk + 2) * 128] = pltpu.roll(k_lo, SWA_HEAD_DIM, 1).astype(BF16)
    ones_s = (lax.broadcasted_iota(I32, (1, SWA_KV_HEADS * 128), 1) % 128 == SWA_HEAD_DIM).astype(F32)
    vs_ref[0] = (seg("vs") + ones_s).astype(BF16)

    g0, _ = _SEG["gl"]
    for p in range(6):
        gate_ref[0, :, p * 512:(p + 1) * 512] = _sigmoid(
            _dot(hb, w1_ref[:, g0 + p * 512:g0 + (p + 1) * 512])).astype(BF16)


def _input_stage(stream, b, modtab, g_mix, w1, g_q, wq, g_kv, wkn, wv, tabm, tabs):
    d = SLAB_ROWS * 128
    p = _stream_tokens(stream, b)
    tm = TOKEN_TILE
    nj = p // tm
    tok = lambda w: pl.BlockSpec((1, tm, w), lambda bi, j: (bi, j, 0))
    full = lambda a: pl.BlockSpec(a.shape, lambda bi, j: (0,) * a.ndim)
    outs = [(1024, BF16), (1024, BF16), (1024, BF16), (512, F32), (512, BF16), (512, BF16), (256, BF16),
            (3072, BF16)]
    return pl.pallas_call(
        functools.partial(_in_kernel, n_stream=len(stream), ctx_tile=nj - 1),
        grid=(b, nj),
        in_specs=_stream_specs(stream, tm, nj) + [
            pl.BlockSpec((1, 1, 8, d), lambda bi, j: (bi, j // (nj - 1), 0, 0)),
            full(g_mix), full(w1), full(g_q), full(wq), full(g_kv), full(wkn), full(wv),
            pl.BlockSpec((tm, 256), lambda bi, j: (j, 0)),
            pl.BlockSpec((tm, 256), lambda bi, j: (j, 0))],
        out_specs=[tok(w) for w, _ in outs],
        out_shape=[jax.ShapeDtypeStruct((b, p, w), dt) for w, dt in outs],
        compiler_params=_cparams(("arbitrary", "arbitrary")),
        name="input_stage",
    )(*stream, modtab, g_mix, w1, g_q, wq, g_kv, wkn, wv, tabm, tabs)


def _mla_kernel(q_ref, k_ref, v_ref, *rest, tk, n_main, tail_rows):
    y_ref, m_ref, acc_ref = rest[-3:]
    hw = MLA_HEAD_PAD
    m_ref[...] = jnp.full(m_ref.shape, MASK_VALUE, F32)
    acc_ref[...] = jnp.zeros(acc_ref.shape, F32)

    def attend(r0, rows):
        for h in range(MLA_HEADS):
            q = q_ref[0, :, h * hw:(h + 1) * hw]
            k = k_ref[0, pl.ds(r0, rows), h * hw:(h + 1) * hw]
            v = v_ref[0, pl.ds(r0, rows), h * hw:(h + 1) * hw]
            s = _dot_nt(q, k)
            m_prev = m_ref[h]
            m_new = jnp.maximum(m_prev, jnp.max(s, axis=1, keepdims=True))
            m_ref[h] = m_new
            pr = jnp.exp2(s - jnp.tile(m_new, (1, rows // hw)))
            acc_ref[h] = jnp.exp2(m_prev - m_new) * acc_ref[h] + _dot(pr.astype(BF16), v)

    if n_main:
        def step(c, carry):
            attend(pl.multiple_of(c * tk, tk), tk)
            return carry

        lax.fori_loop(0, n_main, step, 0)
    if tail_rows:
        attend(n_main * tk, tail_rows)

    first = lax.broadcasted_iota(I32, (q_ref.shape[1], hw), 1) < MLA_V
    for hp in range(MLA_HEADS // 2):
        o = []
        for h in (2 * hp, 2 * hp + 1):
            acc = acc_ref[h]
            o.append(acc / acc[:, MLA_V:MLA_V + 1])
        y_ref[0, :, hp * hw:(hp + 1) * hw] = jnp.where(first, o[0], pltpu.roll(o[1], MLA_V, 1)).astype(BF16)


def _mla_attention(q, k, v, n_ctx):
    b, p, width = q.shape
    s_len = p - n_ctx
    tq = MLA_Q_TILE
    tk = MLA_KEY_CHUNK
    assert s_len % tq == 0 and s_len % tk == 0 and s_len % n_ctx == 0
    out_w = MLA_HEADS * MLA_V
    state = lambda rows: pltpu.VMEM((MLA_HEADS, rows, MLA_HEAD_PAD), F32)
    y_lat = pl.pallas_call(
        functools.partial(_mla_kernel, tk=tk, n_main=p // tk, tail_rows=p % tk),
        grid=(b, s_len // tq),
        in_specs=[pl.BlockSpec((1, tq, width), lambda bi, j: (bi, j, 0)),
                  pl.BlockSpec((1, p, width), lambda bi, j: (bi, 0, 0), pipeline_mode=pl.Buffered(1)),
                  pl.BlockSpec((1, p, width), lambda bi, j: (bi, 0, 0), pipeline_mode=pl.Buffered(1))],
        out_specs=pl.BlockSpec((1, tq, out_w), lambda bi, j: (bi, j, 0)),
        out_shape=jax.ShapeDtypeStruct((b, s_len, out_w), BF16),
        scratch_shapes=[state(tq), state(tq)],
        compiler_params=_cparams(("arbitrary", "arbitrary")),
        name="mla_attention",
    )(q, k, v)
    cblk = s_len // n_ctx
    ctx_rows = lambda w: pl.BlockSpec((1, n_ctx, w), lambda bi: (bi, cblk, 0))
    y_ctx = pl.pallas_call(
        functools.partial(_mla_kernel, tk=tk, n_main=0, tail_rows=n_ctx),
        grid=(b,),
        in_specs=[ctx_rows(width), ctx_rows(width), ctx_rows(width)],
        out_specs=pl.BlockSpec((1, n_ctx, out_w), lambda bi: (bi, 0, 0)),
        out_shape=jax.ShapeDtypeStruct((b, n_ctx, out_w), BF16),
        scratch_shapes=[state(n_ctx), state(n_ctx)],
        compiler_params=_cparams(("arbitrary",)),
        name="mla_attention_ctx",
    )(q, k, v)
    return y_lat, y_ctx


def _swa_kernel(sink_ref, q_ref, k_ref, v_ref, y_ref, *, n_ctx):
    p_len = k_ref.shape[1]
    s_len = p_len - n_ctx
    band = 3 * SWA_BLOCK
    n_lat_tiles = s_len // SWA_BLOCK
    first = lax.broadcasted_iota(I32, (SWA_BLOCK, 128), 1) < SWA_HEAD_DIM
    top = lax.broadcasted_iota(I32, (2 * SWA_BLOCK, 1), 0) < SWA_BLOCK
    n_blocks = q_ref.shape[1] // SWA_BLOCK
    chains = [(t, hk, par) for t in range(n_blocks) for hk in range(SWA_KV_HEADS) for par in range(2)]
    kstart, valid = {}, {}
    for t in range(n_blocks):
        j = pl.program_id(1) * n_blocks + t
        n = jnp.minimum(j, n_lat_tiles - 1)
        ws = jnp.clip((n - 1) * SWA_BLOCK, 0, s_len - band)
        kstart[t] = pl.multiple_of(ws, SWA_BLOCK)
        qpos = n * SWA_BLOCK + lax.broadcasted_iota(I32, (2 * SWA_BLOCK, band), 0) % SWA_BLOCK
        kpos = ws + lax.broadcasted_iota(I32, (2 * SWA_BLOCK, band), 1)
        valid[t] = (jnp.abs(qpos - kpos) <= SWA_WINDOW) & (j < n_lat_tiles)
    scores = {}
    for t, hk, par in chains:
        qrows = slice(t * SWA_BLOCK, (t + 1) * SWA_BLOCK)
        qpair = jnp.concatenate([q_ref[0, qrows, (2 * hk) * 128:(2 * hk + 1) * 128],
                                 q_ref[0, qrows, (2 * hk + 1) * 128:(2 * hk + 2) * 128]], axis=0)
        kcols = slice((2 * hk + par) * 128, (2 * hk + par + 1) * 128)
        s_c = _dot_nt(qpair, k_ref[0, s_len:p_len, kcols])
        s_b = jnp.where(valid[t], _dot_nt(qpair, k_ref[0, pl.ds(kstart[t], band), kcols]), MASK_VALUE)
        scores[t, hk, par] = (s_c, s_b)
    probs = {}
    for t, hk, par in chains:
        s_c, s_b = scores[t, hk, par]
        sink = jnp.where(top, sink_ref[4 * hk + par] * LOG2_E, sink_ref[4 * hk + 2 + par] * LOG2_E)
        m = jnp.maximum(jnp.maximum(jnp.max(s_c, axis=1, keepdims=True), jnp.max(s_b, axis=1, keepdims=True)), sink)
        probs[t, hk, par] = (jnp.exp2(s_c - m).astype(BF16), jnp.exp2(s_b - m).astype(BF16), jnp.exp2(sink - m))
    res = {}
    for t, hk, par in chains:
        p_c, p_b, p_sink = probs[t, hk, par]
        vcols = slice(hk * 128, (hk + 1) * 128)
        o = _dot(p_c, v_ref[0, s_len:p_len, vcols]) + _dot(p_b, v_ref[0, pl.ds(kstart[t], band), vcols])
        o = o / (o[:, SWA_HEAD_DIM:SWA_HEAD_DIM + 1] + p_sink)
        res[t, 4 * hk + par] = o[0:SWA_BLOCK]
        res[t, 4 * hk + 2 + par] = o[SWA_BLOCK:2 * SWA_BLOCK]
    for t in range(n_blocks):
        for pair in range(SWA_Q_HEADS // 2):
            y_ref[0, t * SWA_BLOCK:(t + 1) * SWA_BLOCK, pair * 128:(pair + 1) * 128] = jnp.where(
                first, res[t, 2 * pair], pltpu.roll(res[t, 2 * pair + 1], SWA_HEAD_DIM, 1)).astype(BF16)


def _swa_attention(sink, qs, ks, vs, n_ctx):
    b, p, _ = qs.shape
    tq = SWA_Q_TILE
    assert n_ctx % tq == 0 and (p - n_ctx) % tq == 0
    return pl.pallas_call(
        functools.partial(_swa_kernel, n_ctx=n_ctx),
        grid=(b, p // tq),
        in_specs=[pl.BlockSpec(memory_space=pltpu.SMEM),
                  pl.BlockSpec((1, tq, SWA_WIDTH), lambda bi, j: (bi, j, 0)),
                  pl.BlockSpec((1, p, ks.shape[2]), lambda bi, j: (bi, 0, 0)),
                  pl.BlockSpec((1, p, vs.shape[2]), lambda bi, j: (bi, 0, 0))],
        out_specs=pl.BlockSpec((1, tq, SWA_WIDTH), lambda bi, j: (bi, j, 0)),
        out_shape=jax.ShapeDtypeStruct((b, p, SWA_WIDTH), BF16),
        compiler_params=_cparams(("arbitrary", "arbitrary")),
        name="swa_attention",
    )(sink, qs, ks, vs)


def _pool_kernel(prev_ref, cur_ref, next_ref, w_ref, scale_ref, y_ref, ext_ref, *, n_ctx):
    j = pl.program_id(1)
    tm = cur_ref.shape[1]
    nj = pl.num_programs(1)
    s_len = (nj - 1) * tm
    is_ctx = j == nj - 1
    has_prev = (j >= 1) & (j < nj - 1)
    has_next = j < nj - 2
    ext_ref[0:POOL_HALO, :] = jnp.where(has_prev, prev_ref[0], 0.0)
    ext_ref[POOL_HALO:POOL_HALO + tm, :] = cur_ref[0]
    ext_ref[POOL_HALO + tm:POOL_HALO + tm + POOL_HALO, :] = jnp.where(has_next, next_ref[0], 0.0)
    t = lax.broadcasted_iota(I32, (tm, 1), 0)
    pos = jnp.where(is_ctx, t, j * tm + t)
    seg_len = jnp.where(is_ctx, n_ctx, s_len)
    for g, w in enumerate(POOL_WINDOWS):
        cols = slice(g * POOL_GROUP_DIM, (g + 1) * POOL_GROUP_DIM)
        acc = jnp.zeros((tm, POOL_GROUP_DIM), F32)
        for off in range(-(w // 2), w - w // 2):
            acc = acc + ext_ref[POOL_HALO + off:POOL_HALO + off + tm, cols]
        lo = jnp.maximum(pos - w // 2, 0)
        hi = jnp.minimum(pos + w - w // 2, seg_len)
        cnt = (hi - lo).astype(F32)
        pooled = acc / cnt - cur_ref[0, :, cols]
        y_ref[0, :, cols] = (_dot(pooled.astype(BF16), w_ref[g]) * scale_ref[:, cols]).astype(BF16)


def _pool_stage(u, w_pool, pool_scale, n_ctx):
    b, p, width = u.shape
    tm = TOKEN_TILE
    hb = tm // POOL_HALO
    n_halo_blocks = p // POOL_HALO
    return pl.pallas_call(
        functools.partial(_pool_kernel, n_ctx=n_ctx),
        grid=(b, p // tm),
        in_specs=[pl.BlockSpec((1, POOL_HALO, width), lambda bi, j: (bi, jnp.maximum(j * hb - 1, 0), 0)),
                  pl.BlockSpec((1, tm, width), lambda bi, j: (bi, j, 0)),
                  pl.BlockSpec((1, POOL_HALO, width),
                               lambda bi, j: (bi, jnp.minimum((j + 1) * hb, n_halo_blocks - 1), 0)),
                  pl.BlockSpec(w_pool.shape, lambda bi, j: (0, 0, 0)),
                  pl.BlockSpec(pool_scale.shape, lambda bi, j: (0, 0))],
        out_specs=pl.BlockSpec((1, tm, width), lambda bi, j: (bi, j, 0)),
        out_shape=jax.ShapeDtypeStruct((b, p, width), BF16),
        scratch_shapes=[pltpu.VMEM((tm + 2 * POOL_HALO, width), F32)],
        compiler_params=_cparams(("arbitrary", "arbitrary")),
        name="pool_stage",
    )(u, u, u, w_pool, pool_scale)


MERGE_TILES = 2


def _merge_kernel(*refs, n_stream, ctx_tile, tiles_per_row):
    n_in = n_stream + 6
    tiles = [refs[t * n_in:(t + 1) * n_in] for t in range(MERGE_TILES)]
    (g_ref, wbm_ref, wbp_ref, wbs_ref, wout_ref, wsg_ref, wsu_ref, wsd_ref, wrt_ref,
     xs_ref, fp_ref, lg_ref) = refs[MERGE_TILES * n_in:]
    d = SLAB_ROWS * 128
    tm = tiles[0][n_stream + 2].shape[1]
    both = range(MERGE_TILES)
    is_ctx = [(pl.program_id(0) * MERGE_TILES + t) % tiles_per_row == ctx_tile for t in both]
    mod = [tiles[t][n_stream + 5][0, 0] for t in both]
    gate = [tiles[t][n_stream + 4][0] for t in both]
    branches = []
    for t in both:
        yml_ref, ymc_ref, yp_ref, ys_ref = tiles[t][n_stream:n_stream + 4]
        ym = jnp.where(is_ctx[t], ymc_ref[0], yml_ref[0])
        branches.append((_dot(ym, wbm_ref[...]), _dot(yp_ref[0], wbp_ref[...]), _dot(ys_ref[0], wbs_ref[...])))
    merged = [(gate[t][:, 0:d].astype(F32) * branches[t][0] + gate[t][:, d:2 * d].astype(F32) * branches[t][1]
               + gate[t][:, 2 * d:3 * d].astype(F32) * branches[t][2]).astype(BF16) for t in both]
    proj = [_dot(merged[t], wout_ref[...]) for t in both]
    x_mid = [_stream_tile(tiles[t][:n_stream], tm, is_ctx[t]) + mod[t][2:3] * proj[t] for t in both]
    f = [_rms(x_mid[t], g_ref[...]) * (1.0 + mod[t][4:5]) + mod[t][3:4] for t in both]
    fb = [f[t].astype(BF16) for t in both]
    up = [(_dot(fb[t], wsg_ref[...]), _dot(fb[t], wsu_ref[...])) for t in both]
    for t in both:
        lg_ref[:, t * tm:(t + 1) * tm] = _dot_nt(wrt_ref[...], fb[t])
    hmid = [(up[t][0] * _sigmoid(up[t][0]) * up[t][1]).astype(BF16) for t in both]
    shared = [_dot(hmid[t], wsd_ref[...]) for t in both]
    for t in both:
        _matrix_to_slab_rows(xs_ref.at[t * tm * SLAB_ROWS:(t + 1) * tm * SLAB_ROWS, :],
                             x_mid[t] + mod[t][5:6] * shared[t], SLAB_ROWS)
        _matrix_to_slab_rows(fp_ref.at[t * tm * PACKED_ROWS:(t + 1) * tm * PACKED_ROWS, :],
                             _pack_bf16_pair(f[t]), PACKED_ROWS)


def _merge_stage(stream, ym_lat, ym_ctx, yp, ys, gates, modtab, g_ffn, wbm, wbp, wbs, wout, wsg, wsu, wsd, wrt,
                 keep_ctx):
    b, p, _ = yp.shape
    assert ym_ctx.shape[1] == TOKEN_TILE
    d = SLAB_ROWS * 128
    tm = TOKEN_TILE
    nj = p // tm
    nk = nj if keep_ctx else nj - 1
    per = MERGE_TILES
    assert (b * nk) % per == 0
    full = lambda a: pl.BlockSpec(a.shape, lambda s: (0,) * a.ndim)

    def tile_specs(t):
        row = lambda s: (s * per + t) // nk
        col = lambda s: (s * per + t) % nk
        tok = lambda w: pl.BlockSpec((1, tm, w), lambda s: (row(s), col(s), 0))
        lat = lambda w: pl.BlockSpec((1, tm, w), lambda s: (row(s), jnp.minimum(col(s), nj - 2), 0))
        ctx = lambda w: pl.BlockSpec((1, tm, w), lambda s: (row(s), 0, 0))
        if len(stream) == 1:
            x_specs = [pl.BlockSpec((tm * SLAB_ROWS, 128), lambda s: (row(s) * nj + col(s), 0))]
        else:
            x_specs = [lat(d), ctx(d)]
        return x_specs + [lat(ym_lat.shape[2]), ctx(ym_ctx.shape[2]), tok(yp.shape[2]), tok(ys.shape[2]),
                          tok(gates.shape[2]),
                          pl.BlockSpec((1, 1, 8, d), lambda s: (row(s), col(s) // (nj - 1), 0, 0))]

    tile_args = list(stream) + [ym_lat, ym_ctx, yp, ys, gates, modtab]
    return pl.pallas_call(
        functools.partial(_merge_kernel, n_stream=len(stream), ctx_tile=nj - 1, tiles_per_row=nk),
        grid=(b * nk // per,),
        in_specs=[spec for t in range(per) for spec in tile_specs(t)] + [
            full(g_ffn), full(wbm), full(wbp), full(wbs), full(wout), full(wsg), full(wsu), full(wsd), full(wrt)],
        out_specs=[pl.BlockSpec((per * tm * SLAB_ROWS, 128), lambda s: (s, 0)),
                   pl.BlockSpec((per * tm * PACKED_ROWS, 128), lambda s: (s, 0)),
                   pl.BlockSpec((N_EXPERTS, per * tm), lambda s: (0, s))],
        out_shape=[jax.ShapeDtypeStruct((b * nk * tm * SLAB_ROWS, 128), F32),
                   jax.ShapeDtypeStruct((b * nk * tm * PACKED_ROWS, 128), U32),
                   jax.ShapeDtypeStruct((N_EXPERTS, b * nk * tm), F32)],
        compiler_params=_cparams(("arbitrary",)),
        name="merge_stage",
    )(*(tile_args * per), g_ffn, wbm, wbp, wbs, wout, wsg, wsu, wsd, wrt)


def _route_kernel(lg_ref, bias_ref, dest_ref, wts_ref, seg_ref):
    tm = lg_ref.shape[1]
    ne = N_EXPERTS
    neg_inf = -jnp.inf
    scores = _sigmoid(lg_ref[...])
    sel = scores + bias_ref[...]
    iota_g = lax.broadcasted_iota(I32, (EXPERTS_PER_GROUP, tm), 0)
    gscore = []
    for g in range(N_GROUPS):
        sg = sel[g * EXPERTS_PER_GROUP:(g + 1) * EXPERTS_PER_GROUP]
        m1 = jnp.max(sg, axis=0, keepdims=True)
        i1 = jnp.min(jnp.where(sg == m1, iota_g, EXPERTS_PER_GROUP), axis=0, keepdims=True)
        m2 = jnp.max(jnp.where(iota_g == i1, neg_inf, sg), axis=0, keepdims=True)
        gscore.append(m1 + m2)
    rows = []
    for g in range(N_GROUPS):
        rank = jnp.zeros((1, tm), I32)
        for g2 in range(N_GROUPS):
            if g2 == g:
                continue
            beats = (gscore[g2] >= gscore[g]) if g2 < g else (gscore[g2] > gscore[g])
            rank = rank + beats.astype(I32)
        rows.append(jnp.where(rank < TOPK_GROUPS, sel[g * EXPERTS_PER_GROUP:(g + 1) * EXPERTS_PER_GROUP], neg_inf))
    cur = jnp.concatenate(rows, axis=0)
    iota_e = lax.broadcasted_iota(I32, (ne, tm), 0)
    picks = []
    member = jnp.zeros((ne, tm), F32)
    for _ in range(TOP_K):
        m = jnp.max(cur, axis=0, keepdims=True)
        idx = jnp.min(jnp.where(cur == m, iota_e, ne), axis=0, keepdims=True)
        hit = iota_e == idx
        picks.append(hit)
        member = member + hit.astype(F32)
        cur = jnp.where(hit, neg_inf, cur)
    earlier = (lax.broadcasted_iota(I32, (tm, tm), 0) < lax.broadcasted_iota(I32, (tm, tm), 1)).astype(BF16)
    pos = _dot(member.astype(BF16), earlier)
    cnt_col = jnp.sum(member, axis=1, keepdims=True)
    blocks_col = jnp.floor((cnt_col + (SEG_ALIGN - 1)) * (1.0 / SEG_ALIGN))
    lower = (lax.broadcasted_iota(I32, (ne, ne), 1) < lax.broadcasted_iota(I32, (ne, ne), 0)).astype(BF16)
    off_col = _dot(lower, jnp.broadcast_to(blocks_col, (ne, 128)).astype(BF16))[:, 0:1] * SEG_ALIGN
    base = off_col + pos
    w_rows = [jnp.sum(jnp.where(hit, scores, 0.0), axis=0, keepdims=True) for hit in picks]
    denom = w_rows[0]
    for w in w_rows[1:]:
        denom = denom + w
    for k, hit in enumerate(picks):
        dest_ref[0, k:k + 1, :] = jnp.sum(jnp.where(hit, base, 0.0), axis=0, keepdims=True).astype(I32)
        wts_ref[0, k:k + 1, :] = w_rows[k] / denom * ROUTED_SCALE
    member_pad = jnp.concatenate([member, jnp.zeros((128 - ne, tm), F32)], axis=0).astype(BF16)
    cnt_row = _dot_nt(jnp.ones((8, tm), BF16), member_pad)
    blocks_row = jnp.floor((cnt_row + (SEG_ALIGN - 1)) * (1.0 / SEG_ALIGN))
    before = (lax.broadcasted_iota(I32, (128, 128), 0) < lax.broadcasted_iota(I32, (128, 128), 1)).astype(BF16)
    off_row = _dot(blocks_row.astype(BF16), before) * SEG_ALIGN
    r = lax.broadcasted_iota(I32, (8, 128), 0)
    seg_ref[0] = jnp.where(r == 0, off_row, jnp.where(r == 1, cnt_row, 0.0)).astype(I32)


def _route_stage(lg_t, bias_col):
    ne, t_all = lg_t.shape
    tm = MOE_TILE
    nt = t_all // tm
    return pl.pallas_call(
        _route_kernel,
        grid=(nt,),
        in_specs=[pl.BlockSpec((ne, tm), lambda i: (0, i)),
                  pl.BlockSpec((ne, 1), lambda i: (0, 0))],
        out_specs=[pl.BlockSpec((1, TOP_K, tm), lambda i: (i, 0, 0)),
                   pl.BlockSpec((1, TOP_K, tm), lambda i: (i, 0, 0)),
                   pl.BlockSpec((1, 8, 128), lambda i: (i, 0, 0))],
        out_shape=[jax.ShapeDtypeStruct((nt, TOP_K, tm), I32),
                   jax.ShapeDtypeStruct((nt, TOP_K, tm), F32),
                   jax.ShapeDtypeStruct((nt, 8, 128), I32)],
        compiler_params=_cparams(("arbitrary",)),
        name="route_stage",
    )(lg_t, bias_col)


def _moe_row_stride(tm):
    cap = TOP_K * tm + N_EXPERTS * SEG_ALIGN + MOE_CHUNK
    blocks = cap // 8 + 1
    return 8 * (blocks + 1 - blocks % 2)


def _moe_kernel(dest_ref, wts_ref, seg_ref, fp_hbm, xs_hbm, gt_ref, gfin_ref, wg_ref, wu_ref, wd_ref, out_hbm,
                bufs_ref, fp_stage, xs_stage, out_stage, slab_tmp, fp_sem, xs_sem, out_sem,
                *, srow, sub, n_sub_per_batch, ctx_sub, n_tiles, final_norm):
    pr = pl.program_id(0)
    eg = pl.program_id(1)
    tm = dest_ref.shape[2]
    ch = MOE_CHUNK
    group = wg_ref.shape[0]
    lane_blk = 128
    n_blk = tm // lane_blk
    per = bufs_ref.shape[0]
    n_slots = fp_stage.shape[0]
    n_here = jnp.minimum(per, n_tiles - pr * per)

    def table_block(ref, half, blk):
        return ref.at[half, :, pl.ds(pl.multiple_of(blk * lane_blk, lane_blk), lane_blk)]

    def table_rows(ref, half, blk):
        cols = pl.ds(pl.multiple_of(blk * lane_blk, lane_blk), lane_blk)
        return [ref.at[half, k, cols] for k in range(TOP_K)]

    def token_rows(half, blk, per_tok):
        first = ((pr * per + half) * tm + blk * lane_blk) * per_tok
        return pl.ds(pl.multiple_of(first, lane_blk * per_tok), lane_blk * per_tok)

    def fp_copy(half, blk, slot):
        return pltpu.make_async_copy(fp_hbm.at[token_rows(half, blk, PACKED_ROWS), :], fp_stage.at[slot],
                                     fp_sem.at[slot])

    def xs_copy(half, blk, slot):
        return pltpu.make_async_copy(xs_hbm.at[token_rows(half, blk, SLAB_ROWS), :], xs_stage.at[slot],
                                     xs_sem.at[slot])

    def out_copy(half, blk, slot):
        if not final_norm:
            dst = out_hbm.at[token_rows(half, blk, SLAB_ROWS), :]
        else:
            first = (pr * per + half) * tm + blk * lane_blk
            row_len = out_hbm.shape[1]
            bi = first // row_len
            dst = out_hbm.at[bi, pl.ds(pl.multiple_of(first - bi * row_len, lane_blk), lane_blk), :]
        return pltpu.make_async_copy(out_stage.at[slot], dst, out_sem.at[slot])

    def group_rows(half):
        buf_ref = bufs_ref.at[half]
        for blk in range(n_slots - 1):
            fp_copy(half, blk, blk).start()
        buf_ref[...] = jnp.zeros(buf_ref.shape, U32)

        def body(blk, carry):
            slot = blk % n_slots
            fp_copy(half, blk, slot).wait()

            @pl.when(blk + n_slots - 1 < n_blk)
            def _prefetch():
                fp_copy(half, blk + n_slots - 1, (blk + n_slots - 1) % n_slots).start()

            dest = table_rows(dest_ref, half, blk)
            rows = fp_stage.at[slot]
            for u in range(lane_blk):
                slab = rows[u * PACKED_ROWS:(u + 1) * PACKED_ROWS, :]
                for k in range(TOP_K):
                    buf_ref[pl.ds(dest[k][u], PACKED_ROWS, stride=srow), :] = slab
            return carry

        lax.fori_loop(0, n_blk, body, 0)

    def experts():
        def one_expert(ge, carry):
            e = eg * group + ge
            counts = [seg_ref[half, 1, e] for half in range(per)]
            offsets = [seg_ref[half, 0, e] for half in range(per)]

            def ffn_rows(halves, rows, chunk_idx):
                words, keep = [], []
                for h in halves:
                    r0 = offsets[h] + chunk_idx * rows
                    words.append([bufs_ref[h, pl.ds(pl.multiple_of(q * srow + r0, 8), rows), :] for q in range(4)])
                    keep.append(lax.broadcasted_iota(I32, (rows, 1), 0) < counts[h] - chunk_idx * rows)
                xb = []
                for ws in words:
                    w = jnp.concatenate(ws, axis=1)
                    xb.append(jnp.concatenate([_unpack_lo(w), _unpack_hi(w)], axis=1).astype(BF16))
                gates = [(_dot(x, wg_ref[ge]), _dot(x, wu_ref[ge])) for x in xb]
                hmid = [(g * _sigmoid(g) * u).astype(BF16) for g, u in gates]
                packed = [_pack_bf16_pair(_dot(hm, wd_ref[ge])) for hm in hmid]
                for n, h in enumerate(halves):
                    r0 = offsets[h] + chunk_idx * rows
                    for q in range(4):
                        bufs_ref[h, pl.ds(pl.multiple_of(q * srow + r0, 8), rows), :] = jnp.where(
                            keep[n], packed[n][:, q * 128:(q + 1) * 128], words[n][q])

            most = counts[0]
            for c in counts[1:]:
                most = jnp.maximum(most, c)
            lo = 0
            for rows in MOE_BODY_ROWS:
                pl.when((most > lo) & (most <= rows))(functools.partial(ffn_rows, tuple(range(per)), rows, 0))
                lo = rows

            @pl.when(most > ch)
            def _long_segments():
                for h in range(per):
                    def chunk(c, carry2, h=h):
                        ffn_rows((h,), ch, c)
                        return carry2

                    lax.fori_loop(0, (counts[h] + ch - 1) // ch, chunk, 0)

            return carry

        lax.fori_loop(0, group, one_expert, 0)

    def combine(half):
        buf_ref = bufs_ref.at[half]
        for blk in range(n_slots - 1):
            xs_copy(half, blk, blk).start()

        def body(blk, carry):
            slot = blk % n_slots
            xs_copy(half, blk, slot).wait()

            @pl.when(blk + n_slots - 1 < n_blk)
            def _prefetch():
                xs_copy(half, blk + n_slots - 1, (blk + n_slots - 1) % n_slots).start()

            @pl.when(blk >= n_slots)
            def _slot_free():
                out_copy(half, blk - n_slots, slot).wait()

            sub_blk = ((pr * per + half) * tm + blk * lane_blk) // sub
            bi = sub_blk // n_sub_per_batch
            gate = gt_ref[bi * 2 + (sub_blk - bi * n_sub_per_batch) // ctx_sub]
            dest = table_block(dest_ref, half, blk)
            wts = table_block(wts_ref, half, blk)
            xs = xs_stage.at[slot]
            out = slab_tmp if final_norm else out_stage.at[slot]
            for u in range(lane_blk):
                acc_lo = jnp.zeros((4, 128), F32)
                acc_hi = jnp.zeros((4, 128), F32)
                for k in range(TOP_K):
                    words = buf_ref[pl.ds(dest[k, u], PACKED_ROWS, stride=srow), :]
                    wk = wts[k, u]
                    acc_lo = acc_lo + wk * _unpack_lo(words)
                    acc_hi = acc_hi + wk * _unpack_hi(words)
                out[u * 8:u * 8 + 4, :] = xs[u * 8:u * 8 + 4, :] + gate[0:4] * acc_lo
                out[u * 8 + 4:u * 8 + 8, :] = xs[u * 8 + 4:u * 8 + 8, :] + gate[4:8] * acc_hi
            if final_norm:
                out_stage[slot] = _rms(_slab_rows_to_matrix(slab_tmp, lane_blk, SLAB_ROWS), gfin_ref[...])
            out_copy(half, blk, slot).start()
            return carry

        lax.fori_loop(0, n_blk, body, 0)
        for blk in range(n_blk - n_slots, n_blk):
            out_copy(half, blk, blk % n_slots).wait()

    def each_tile(fn):
        def body(half, carry):
            fn(half)
            return carry

        lax.fori_loop(0, n_here, body, 0)

    @pl.when(eg == 0)
    def _first_step():
        each_tile(group_rows)
        for half in range(1, per):
            @pl.when(half >= n_here)
            def _clear(half=half):
                bufs_ref[half] = jnp.zeros(bufs_ref.shape[1:], U32)

    experts()
    pl.when(eg == pl.num_programs(1) - 1)(functools.partial(each_tile, combine))


def _moe_stage(dest, wts, seg, fp4, xs8, gt2, g_final, wg, wu, wd, layer, n_sub_per_batch, keep_ctx, final_norm):
    ctx_sub = n_sub_per_batch - 1 if keep_ctx else n_sub_per_batch
    d = SLAB_ROWS * 128
    tokens_per_row = n_sub_per_batch * TOKEN_TILE
    out_shape = (xs8.shape[0] // (SLAB_ROWS * tokens_per_row), tokens_per_row, d) if final_norm else xs8.shape
    out_block = (128, d) if final_norm else (128 * SLAB_ROWS, 128)
    nt, _, tm = dest.shape
    ne = wg.shape[1]
    group = MOE_EXPERTS_PER_STEP
    per = MOE_TILES_PER_STEP
    assert ne % group == 0 and tm % 256 == 0 and TOKEN_TILE % 128 == 0 and per == 2
    n_rows = -(-nt // per)
    pad = ((0, n_rows * per - nt), (0, 0), (0, 0))
    dest, wts, seg = jnp.pad(dest, pad), jnp.pad(wts, pad), jnp.pad(seg, pad)
    srow = _moe_row_stride(tm)
    smem = lambda shape: pl.BlockSpec(shape, lambda i, e: (i, 0, 0), memory_space=pltpu.SMEM)
    hbm = pl.BlockSpec(memory_space=pl.ANY)
    slots = MOE_STAGE_SLOTS
    assert tm // 128 >= slots
    stage = lambda per_tok, dt: pltpu.VMEM((slots, 128 * per_tok, 128), dt)
    return pl.pallas_call(
        functools.partial(_moe_kernel, srow=srow, sub=TOKEN_TILE, n_sub_per_batch=n_sub_per_batch, ctx_sub=ctx_sub,
                          n_tiles=nt, final_norm=final_norm),
        grid=(n_rows, ne // group),
        in_specs=[smem((per, TOP_K, tm)), smem((per, TOP_K, tm)), smem((per, 8, 128)),
                  hbm, hbm,
                  pl.BlockSpec(gt2.shape, lambda i, e: (0, 0, 0)),
                  pl.BlockSpec(g_final.shape, lambda i, e: (0, 0)),
                  pl.BlockSpec((None, group) + wg.shape[2:], lambda i, e: (layer, e, 0, 0)),
                  pl.BlockSpec((None, group) + wu.shape[2:], lambda i, e: (layer, e, 0, 0)),
                  pl.BlockSpec((None, group) + wd.shape[2:], lambda i, e: (layer, e, 0, 0))],
        out_specs=hbm,
        out_shape=jax.ShapeDtypeStruct(out_shape, F32),
        scratch_shapes=[pltpu.VMEM((per, 4 * srow, 128), U32),
                        stage(PACKED_ROWS, U32), stage(SLAB_ROWS, F32), pltpu.VMEM((slots,) + out_block, F32),
                        pltpu.VMEM((128 * SLAB_ROWS, 128), F32),
                        pltpu.SemaphoreType.DMA((slots,)), pltpu.SemaphoreType.DMA((slots,)),
                        pltpu.SemaphoreType.DMA((slots,))],
        compiler_params=_cparams(("arbitrary", "arbitrary")),
        name="moe_stage",
    )(dest, wts, seg, fp4, xs8, gt2, g_final, wg, wu, wd)


def _rot_lanes(x, head_dim):
    quarter = head_dim // 4
    lane = lax.broadcasted_iota(I32, x.shape, 1)
    width = x.shape[1]
    return jnp.where(lane % (2 * quarter) < quarter, -pltpu.roll(x, width - quarter, 1), pltpu.roll(x, quarter, 1))


def _w1_kernel(w_ref, o_ref):
    offs = [0]
    for w in (MLA_Q_LORA, MLA_KV_LORA, MLA_ROPE, POOL_WIDTH, SWA_WIDTH, SWA_KV_WIDTH, SWA_KV_WIDTH):
        offs.append(offs[-1] + w)
    o_cq, o_ckv, o_kr, o_u, o_qs, o_ks, o_vs, o_gl = offs
    lane = lax.broadcasted_iota(I32, (w_ref.shape[0], 128), 1)

    def put(name, val):
        a, b = _SEG[name]
        o_ref[:, a:b] = val.astype(BF16)

    def kv_slots(x):
        low = lane < SWA_HEAD_DIM
        return jnp.concatenate([jnp.where(low, x, 0.0), jnp.where(low, pltpu.roll(x, SWA_HEAD_DIM, 1), 0.0)], axis=1)

    def kr_slot(x):
        return jnp.where((lane >= MLA_NOPE) & (lane < MLA_NOPE + MLA_ROPE), pltpu.roll(x, MLA_NOPE, 1), 0.0)

    put("cq", w_ref[:, o_cq:o_ckv])
    put("ckv", w_ref[:, o_ckv:o_kr])
    put("u", w_ref[:, o_u:o_qs])
    put("qs", w_ref[:, o_qs:o_ks] * (SWA_SCALE * LOG2_E))
    put("ks", kv_slots(w_ref[:, o_ks:o_vs]))
    put("vs", kv_slots(w_ref[:, o_vs:o_gl]))
    put("kr", kr_slot(w_ref[:, o_kr:o_kr + 128]))
    put("gl", w_ref[:, o_gl:w_ref.shape[1]])


def _fused_in_weight(w_in, layer):
    _, d, width = w_in.shape
    tr = 256
    assert d % tr == 0 and SWA_KV_WIDTH == 128 and MLA_KV_LORA + MLA_Q_LORA == 5 * 128
    return pl.pallas_call(
        _w1_kernel,
        grid=(d // tr,),
        in_specs=[pl.BlockSpec((None, tr, width), lambda i: (layer, i, 0))],
        out_specs=pl.BlockSpec((tr, FUSED_IN_WIDTH), lambda i: (i, 0)),
        out_shape=jax.ShapeDtypeStruct((d, FUSED_IN_WIDTH), BF16),
        compiler_params=_cparams(("arbitrary",)),
        name="fused_in_weight",
    )(w_in)


def _rope_pattern(s_len, n_ctx, rot_dim):
    t = jnp.arange(s_len)
    row = (t // GRID_W).astype(F32)
    col = (t % GRID_W).astype(F32)
    n_freq = rot_dim // 4
    inv_freq = ROPE_BASE ** (-jnp.arange(n_freq, dtype=F32) / n_freq)
    ang_r = row[:, None] * inv_freq[None, :]
    ang_c = col[:, None] * inv_freq[None, :]
    cos = jnp.concatenate([jnp.cos(ang_r), jnp.cos(ang_r), jnp.cos(ang_c), jnp.cos(ang_c)], axis=1)
    sin = jnp.concatenate([jnp.sin(ang_r), jnp.sin(ang_r), jnp.sin(ang_c), jnp.sin(ang_c)], axis=1)
    cos = jnp.concatenate([cos, jnp.ones((n_ctx, rot_dim), F32)], axis=0)
    sin = jnp.concatenate([sin, jnp.zeros((n_ctx, rot_dim), F32)], axis=0)
    return cos, sin


def _layer_weights(w_in, layer, w_uq, w_ukv):
    w1 = _fused_in_weight(w_in, layer)
    lq = w_uq.shape[0]
    wq3 = (w_uq * (MLA_SCALE * LOG2_E)).reshape(lq, MLA_HEADS, MLA_NOPE + MLA_ROPE)
    pad = jnp.zeros((lq, MLA_HEADS, MLA_HEAD_PAD - MLA_NOPE - MLA_ROPE), F32)
    wq = jnp.concatenate([wq3, pad], axis=2).reshape(lq, MLA_HEADS * MLA_HEAD_PAD).astype(BF16)
    lkv = w_ukv.shape[0]
    wkv3 = w_ukv.reshape(lkv, MLA_HEADS, MLA_NOPE + MLA_V)
    wkn = jnp.concatenate([wkv3[:, :, :MLA_NOPE], jnp.zeros((lkv, MLA_HEADS, MLA_HEAD_PAD - MLA_NOPE), F32)],
                          axis=2).reshape(lkv, MLA_HEADS * MLA_HEAD_PAD).astype(BF16)
    wv = jnp.concatenate([wkv3[:, :, MLA_NOPE:], jnp.zeros((lkv, MLA_HEADS, MLA_HEAD_PAD - MLA_V), F32)],
                         axis=2).reshape(lkv, MLA_HEADS * MLA_HEAD_PAD).astype(BF16)
    return w1, wq, wkn, wv


def kernel(x, c, ctx, c_ctx, w_mod, b_mod, g_mix, g_ffn, w_in, g_mla_q, g_mla_kv, w_mla_uq, w_mla_ukv, w_pool,
           pool_scale, swa_sink, w_br_mla, w_br_pool, w_br_swa, w_out, w_router, router_bias, w_exp_gate,
           w_exp_up, w_exp_down, w_sh_gate, w_sh_up, w_sh_down, g_final):
    b, s_len, d = x.shape
    n_ctx = ctx.shape[1]
    n_layers = w_mod.shape[0]
    p = n_ctx + s_len
    assert n_ctx == TOKEN_TILE and s_len % TOKEN_TILE == 0 and (b * p) % MOE_TILE == 0 and b + 1 <= 8
    assert d == 1024 and w_in.shape[2] == (MLA_Q_LORA + MLA_KV_LORA + MLA_ROPE + POOL_WIDTH + SWA_WIDTH
                                           + 2 * SWA_KV_WIDTH + 3 * d)

    cm, sm = _rope_pattern(s_len, n_ctx, MLA_ROPE)
    tail = jnp.zeros((p, MLA_HEAD_PAD - MLA_NOPE - MLA_ROPE), F32)
    tabm = jnp.concatenate([jnp.ones((p, MLA_NOPE), F32), cm, tail, jnp.zeros((p, MLA_NOPE), F32), sm, tail], axis=1)
    cs, ss = _rope_pattern(s_len, n_ctx, SWA_HEAD_DIM)
    tabs = jnp.concatenate([cs, cs, ss, ss], axis=1)

    cvec = jnp.concatenate([c, c_ctx[None, :], jnp.zeros((8 - b - 1, d), F32)], axis=0)
    mod_all = _modulation(cvec, w_mod, b_mod).reshape(n_layers, 8, N_MOD, d)

    stream = (x, ctx)
    wg_all, wu_all, wd_all = w_exp_gate.astype(BF16), w_exp_up.astype(BF16), w_exp_down.astype(BF16)
    for i in range(n_layers):
        keep_ctx = i < n_layers - 1
        lat = mod_all[i, :b]
        ctx_rows = jnp.broadcast_to(mod_all[i, b][None], (b, N_MOD, d))
        modtab = jnp.pad(jnp.stack([lat, ctx_rows], axis=1), ((0, 0), (0, 0), (0, 8 - N_MOD), (0, 0)))
        w1, wq, wkn, wv = _layer_weights(w_in, i, w_mla_uq[i], w_mla_ukv[i])
        q, k, v, u, qs, ks, vs, gates = _input_stage(
            stream, b, modtab, g_mix[i][None], w1, g_mla_q[i][None], wq, g_mla_kv[i][None], wkn, wv, tabm, tabs)
        y_mla, y_mla_ctx = _mla_attention(q, k, v, n_ctx)
        y_swa = _swa_attention(swa_sink[i], qs, ks, vs, n_ctx)
        y_pool = _pool_stage(u, w_pool[i].astype(BF16), pool_scale[i][None], n_ctx)
        xs, fp, lg_t = _merge_stage(
            stream, y_mla, y_mla_ctx, y_pool, y_swa, gates, modtab, g_ffn[i][None], w_br_mla[i].astype(BF16),
            w_br_pool[i].astype(BF16), w_br_swa[i].astype(BF16), w_out[i].astype(BF16),
            w_sh_gate[i].astype(BF16), w_sh_up[i].astype(BF16), w_sh_down[i].astype(BF16),
            w_router[i].T.astype(BF16), keep_ctx)
        dest, wts, seg = _route_stage(lg_t, router_bias[i][:, None])
        gt2 = modtab[:, :, 5, :].reshape(b * 2, 8, d // 8)
        stream = (_moe_stage(dest, wts, seg, fp, xs, gt2, g_final[None], wg_all, wu_all, wd_all, i,
                             (p if keep_ctx else s_len) // TOKEN_TILE, keep_ctx, final_norm=not keep_ctx),)
    return stream[0]
```

```python
import functools

import jax
import jax.numpy as jnp
from jax import lax
from jax.experimental import pallas as pl
from jax.experimental.pallas import tpu as pltpu

F32 = jnp.float32
BF16 = jnp.bfloat16
U32 = jnp.uint32
I32 = jnp.int32

NORM_EPS = 1e-6
ROPE_BASE = 10000.0
GRID_W = 64
N_MOD = 6

MLA_HEADS = 8
MLA_Q_LORA = 384
MLA_KV_LORA = 256
MLA_NOPE = 64
MLA_ROPE = 32
MLA_V = 64
MLA_SCALE = (MLA_NOPE + MLA_ROPE) ** -0.5
MLA_HEAD_PAD = 128
LOG2_E = 1.4426950408889634

POOL_WINDOWS = (2, 4, 8, 16)
POOL_GROUP_DIM = 128
POOL_WIDTH = 512
POOL_HALO = 8

SWA_Q_HEADS = 8
SWA_KV_HEADS = 2
SWA_HEAD_DIM = 64
SWA_WINDOW = 128
SWA_BLOCK = 128
SWA_SCALE = SWA_HEAD_DIM ** -0.5
SWA_WIDTH = SWA_Q_HEADS * SWA_HEAD_DIM
SWA_KV_WIDTH = SWA_KV_HEADS * SWA_HEAD_DIM

N_EXPERTS = 64
TOP_K = 8
N_GROUPS = 8
TOPK_GROUPS = 4
EXPERTS_PER_GROUP = 8
D_EXPERT = 256
ROUTED_SCALE = 2.5

TOKEN_TILE = 256
SWA_Q_TILE = 256
MLA_Q_TILE = 1024
MLA_KEY_CHUNK = 256
MOE_TILE = 1024
MOE_CHUNK = 256
MOE_BODY_ROWS = (128, 144, 160, 176, 192, 224, 256)
MOE_EXPERTS_PER_STEP = 2
MOE_TILES_PER_STEP = 2
MOE_STAGE_SLOTS = 4
SEG_ALIGN = 8
SLAB_ROWS = 8
PACKED_ROWS = 4
MASK_VALUE = -1e30
HI16 = 0xFFFF0000

VMEM_LIMIT = 56 * 1024 * 1024

_SEG_WIDTHS = (("cq", 384), ("ckv", 256), ("u", 512), ("qs", 512), ("ks", 256), ("vs", 256), ("kr", 128),
               ("gl", 3072))
_SEG = {}
_o = 0
for _n, _w in _SEG_WIDTHS:
    _SEG[_n] = (_o, _o + _w)
    _o += _w
FUSED_IN_WIDTH = _o


def _cparams(sem):
    return pltpu.CompilerParams(dimension_semantics=sem, vmem_limit_bytes=VMEM_LIMIT)


def _dot(a, b):
    return jnp.dot(a, b, preferred_element_type=F32)


def _dot_nt(a, b):
    return lax.dot_general(a, b, (((1,), (1,)), ((), ())), preferred_element_type=F32)


def _sigmoid(x):
    return 1.0 / (1.0 + jnp.exp(-x))


def _rms(x, g):
    return x * lax.rsqrt(jnp.mean(x * x, axis=-1, keepdims=True) + NORM_EPS) * g


def _pack_bf16_pair(v):
    n = v.shape[1] // 2
    bits = pltpu.bitcast(v.astype(BF16).astype(F32), U32)
    return (bits[:, :n] >> 16) | (bits[:, n:] & jnp.uint32(HI16))


def _slab_rows_to_matrix(ref, n_tok, per_tok):
    return jnp.concatenate([ref[pl.ds(c, n_tok, stride=per_tok), :] for c in range(per_tok)], axis=1)


def _matrix_to_slab_rows(ref, val, per_tok):
    n_tok = val.shape[0]
    for c in range(per_tok):
        ref[pl.ds(c, n_tok, stride=per_tok), :] = val[:, c * 128:(c + 1) * 128]


def _unpack_lo(w):
    return pltpu.bitcast(w << 16, F32)


def _unpack_hi(w):
    return pltpu.bitcast(w & jnp.uint32(HI16), F32)


def _mod_kernel(c_ref, w_ref, b_ref, o_ref):
    c = c_ref[...]
    a = (c * _sigmoid(c)).astype(BF16)
    o_ref[0] = _dot(a, w_ref[0].astype(BF16)) + b_ref[0]


def _modulation(cvec, w_mod, b_mod):
    n_layers, d, width = w_mod.shape
    tn = width // 4
    return pl.pallas_call(
        _mod_kernel,
        grid=(n_layers, width // tn),
        in_specs=[pl.BlockSpec((8, d), lambda l, n: (0, 0)),
                  pl.BlockSpec((1, d, tn), lambda l, n: (l, 0, n)),
                  pl.BlockSpec((1, 1, tn), lambda l, n: (l, 0, n))],
        out_specs=pl.BlockSpec((1, 8, tn), lambda l, n: (l, 0, n)),
        out_shape=jax.ShapeDtypeStruct((n_layers, 8, width), F32),
        compiler_params=_cparams(("arbitrary", "arbitrary")),
        name="modulation",
    )(cvec, w_mod, b_mod.reshape(n_layers, 1, width))


def _stream_tile(refs, n_tok, is_ctx):
    if len(refs) == 1:
        return _slab_rows_to_matrix(refs[0], n_tok, SLAB_ROWS)
    return jnp.where(is_ctx, refs[1][0], refs[0][0])


def _stream_tokens(stream, b):
    return stream[0].shape[0] // (b * SLAB_ROWS) if len(stream) == 1 else stream[0].shape[1] + stream[1].shape[1]


INPUT_TILES = 2


def _in_kernel(*refs, n_stream, ctx_tile, tiles_per_row):
    n_in = n_stream + 3
    tiles = [refs[t * n_in:(t + 1) * n_in] for t in range(INPUT_TILES)]
    (g_ref, w1_ref, gq_ref, wq_ref, gkv_ref, wkn_ref, wv_ref,
     q_ref, k_ref, v_ref, u_ref, qs_ref, ks_ref, vs_ref, gate_ref) = refs[INPUT_TILES * n_in:]
    tm = tiles[0][n_stream + 1].shape[0]
    both = range(INPUT_TILES)
    rows = [slice(t * tm, (t + 1) * tm) for t in both]
    hb, tabs = [], []
    for t in both:
        is_ctx = (pl.program_id(0) * INPUT_TILES + t) % tiles_per_row == ctx_tile
        mod = tiles[t][n_stream][0, 0]
        x = _stream_tile(tiles[t][:n_stream], tm, is_ctx)
        hb.append((_rms(x, g_ref[...]) * (1.0 + mod[1:2]) + mod[0:1]).astype(BF16))
        tabm_ref, tabs_ref = tiles[t][n_stream + 1:n_stream + 3]
        tabs.append((tabm_ref[:, 0:128], tabm_ref[:, 128:256], tabs_ref[:, 0:128], tabs_ref[:, 128:256]))

    def seg(name):
        a, b = _SEG[name]
        return [_dot(hb[t], w1_ref[:, a:b]) for t in both]

    def rotary(val, cos, sin, head_dim):
        pieces = [val[:, c:c + 128] for c in range(0, val.shape[1], 128)]
        return jnp.concatenate([p * cos + _rot_lanes(p, head_dim) * sin for p in pieces], axis=1)

    cq, ckv, kr, u, qs, ks, vs = (seg(n) for n in ("cq", "ckv", "kr", "u", "qs", "ks", "vs"))
    cqn = [_rms(cq[t], gq_ref[...]).astype(BF16) for t in both]
    ckvn = [_rms(ckv[t], gkv_ref[...]).astype(BF16) for t in both]
    q_up = [_dot(cqn[t], wq_ref[...]) for t in both]
    k_up = [_dot(ckvn[t], wkn_ref[...]) for t in both]
    v_up = [_dot(ckvn[t], wv_ref[...]) for t in both]
    ones_m = (lax.broadcasted_iota(I32, (1, MLA_HEADS * MLA_HEAD_PAD), 1) % MLA_HEAD_PAD == MLA_V).astype(F32)
    ones_s = (lax.broadcasted_iota(I32, (1, SWA_KV_HEADS * 128), 1) % 128 == SWA_HEAD_DIM).astype(F32)
    for t in both:
        cos_m, sin_m, cos_s, sin_s = tabs[t]
        q_ref[rows[t], :] = rotary(q_up[t], cos_m, sin_m, MLA_ROPE).astype(BF16)
        k_ref[rows[t], :] = (k_up[t] + jnp.tile(rotary(kr[t], cos_m, sin_m, MLA_ROPE), (1, MLA_HEADS))).astype(BF16)
        v_ref[rows[t], :] = (v_up[t] + ones_m).astype(BF16)
        u_ref[rows[t], :] = u[t]
        qs_ref[rows[t], :] = rotary(qs[t], cos_s, sin_s, SWA_HEAD_DIM).astype(BF16)
        ks_rot = rotary(ks[t], cos_s, sin_s, SWA_HEAD_DIM)
        for hk in range(SWA_KV_HEADS):
            k_lo = ks_rot[:, hk * 128:(hk + 1) * 128]
            ks_ref[rows[t], (2 * hk) * 128:(2 * hk + 1) * 128] = k_lo.astype(BF16)
            ks_ref[rows[t], (2 * hk + 1) * 128:(2 * hk + 2) * 128] = pltpu.roll(k_lo, SWA_HEAD_DIM, 1).astype(BF16)
        vs_ref[rows[t], :] = (vs[t] + ones_s).astype(BF16)

    g0, _ = _SEG["gl"]
    for p in range(6):
        for t in both:
            gate_ref[rows[t], p * 512:(p + 1) * 512] = _sigmoid(
                _dot(hb[t], w1_ref[:, g0 + p * 512:g0 + (p + 1) * 512])).astype(BF16)


def _input_stage(stream, b, modtab, g_mix, w1, g_q, wq, g_kv, wkn, wv, tabm, tabs):
    d = SLAB_ROWS * 128
    p = _stream_tokens(stream, b)
    tm = TOKEN_TILE
    nj = p // tm
    per = INPUT_TILES
    assert (b * nj) % per == 0
    full = lambda a: pl.BlockSpec(a.shape, lambda s: (0,) * a.ndim)
    outs = [(1024, BF16), (1024, BF16), (1024, BF16), (512, F32), (512, BF16), (512, BF16), (256, BF16),
            (3072, BF16)]

    def tile_specs(t):
        row = lambda s: (s * per + t) // nj
        col = lambda s: (s * per + t) % nj
        if len(stream) == 1:
            x_specs = [pl.BlockSpec((tm * SLAB_ROWS, 128), lambda s: (s * per + t, 0))]
        else:
            x_specs = [pl.BlockSpec((1, tm, d), lambda s: (row(s), jnp.minimum(col(s), nj - 2), 0)),
                       pl.BlockSpec((1, tm, d), lambda s: (row(s), 0, 0))]
        return x_specs + [pl.BlockSpec((1, 1, 8, d), lambda s: (row(s), col(s) // (nj - 1), 0, 0)),
                          pl.BlockSpec((tm, 256), lambda s: (col(s), 0)),
                          pl.BlockSpec((tm, 256), lambda s: (col(s), 0))]

    tile_args = list(stream) + [modtab, tabm, tabs]
    res = pl.pallas_call(
        functools.partial(_in_kernel, n_stream=len(stream), ctx_tile=nj - 1, tiles_per_row=nj),
        grid=(b * nj // per,),
        in_specs=[spec for t in range(per) for spec in tile_specs(t)] + [
            full(g_mix), full(w1), full(g_q), full(wq), full(g_kv), full(wkn), full(wv)],
        out_specs=[pl.BlockSpec((per * tm, w), lambda s: (s, 0)) for w, _ in outs],
        out_shape=[jax.ShapeDtypeStruct((b * p, w), dt) for w, dt in outs],
        compiler_params=_cparams(("arbitrary",)),
        name="input_stage",
    )(*(tile_args * per), g_mix, w1, g_q, wq, g_kv, wkn, wv)
    return [r.reshape(b, p, r.shape[1]) for r in res]


def _mla_kernel(q_ref, k_ref, v_ref, *rest, tk, n_main, tail_rows):
    y_ref, m_ref, acc_ref = rest[-3:]
    hw = MLA_HEAD_PAD
    m_ref[...] = jnp.full(m_ref.shape, MASK_VALUE, F32)
    acc_ref[...] = jnp.zeros(acc_ref.shape, F32)

    def attend(r0, rows):
        for h in range(MLA_HEADS):
            q = q_ref[0, :, h * hw:(h + 1) * hw]
            k = k_ref[0, pl.ds(r0, rows), h * hw:(h + 1) * hw]
            v = v_ref[0, pl.ds(r0, rows), h * hw:(h + 1) * hw]
            s = _dot_nt(q, k)
            m_prev = m_ref[h]
            m_new = jnp.maximum(m_prev, jnp.max(s, axis=1, keepdims=True))
            m_ref[h] = m_new
            pr = jnp.exp2(s - jnp.tile(m_new, (1, rows // hw)))
            acc_ref[h] = jnp.exp2(m_prev - m_new) * acc_ref[h] + _dot(pr.astype(BF16), v)

    if n_main:
        def step(c, carry):
            attend(pl.multiple_of(c * tk, tk), tk)
            return carry

        lax.fori_loop(0, n_main, step, 0)
    if tail_rows:
        attend(n_main * tk, tail_rows)

    first = lax.broadcasted_iota(I32, (q_ref.shape[1], hw), 1) < MLA_V
    for hp in range(MLA_HEADS // 2):
        o = []
        for h in (2 * hp, 2 * hp + 1):
            acc = acc_ref[h]
            o.append(acc / acc[:, MLA_V:MLA_V + 1])
        y_ref[0, :, hp * hw:(hp + 1) * hw] = jnp.where(first, o[0], pltpu.roll(o[1], MLA_V, 1)).astype(BF16)


def _mla_attention(q, k, v, n_ctx):
    b, p, width = q.shape
    s_len = p - n_ctx
    tq = MLA_Q_TILE
    tk = MLA_KEY_CHUNK
    assert s_len % tq == 0 and s_len % tk == 0 and s_len % n_ctx == 0
    out_w = MLA_HEADS * MLA_V
    state = lambda rows: pltpu.VMEM((MLA_HEADS, rows, MLA_HEAD_PAD), F32)
    y_lat = pl.pallas_call(
        functools.partial(_mla_kernel, tk=tk, n_main=p // tk, tail_rows=p % tk),
        grid=(b, s_len // tq),
        in_specs=[pl.BlockSpec((1, tq, width), lambda bi, j: (bi, j, 0)),
                  pl.BlockSpec((1, p, width), lambda bi, j: (bi, 0, 0), pipeline_mode=pl.Buffered(1)),
                  pl.BlockSpec((1, p, width), lambda bi, j: (bi, 0, 0), pipeline_mode=pl.Buffered(1))],
        out_specs=pl.BlockSpec((1, tq, out_w), lambda bi, j: (bi, j, 0)),
        out_shape=jax.ShapeDtypeStruct((b, s_len, out_w), BF16),
        scratch_shapes=[state(tq), state(tq)],
        compiler_params=_cparams(("arbitrary", "arbitrary")),
        name="mla_attention",
    )(q, k, v)
    cblk = s_len // n_ctx
    ctx_rows = lambda w: pl.BlockSpec((1, n_ctx, w), lambda bi: (bi, cblk, 0))
    y_ctx = pl.pallas_call(
        functools.partial(_mla_kernel, tk=tk, n_main=0, tail_rows=n_ctx),
        grid=(b,),
        in_specs=[ctx_rows(width), ctx_rows(width), ctx_rows(width)],
        out_specs=pl.BlockSpec((1, n_ctx, out_w), lambda bi: (bi, 0, 0)),
        out_shape=jax.ShapeDtypeStruct((b, n_ctx, out_w), BF16),
        scratch_shapes=[state(n_ctx), state(n_ctx)],
        compiler_params=_cparams(("arbitrary",)),
        name="mla_attention_ctx",
    )(q, k, v)
    return y_lat, y_ctx


def _swa_kernel(sink_ref, q_ref, k_ref, v_ref, y_ref, *, n_ctx):
    p_len = k_ref.shape[1]
    s_len = p_len - n_ctx
    band = 3 * SWA_BLOCK
    n_lat_tiles = s_len // SWA_BLOCK
    first = lax.broadcasted_iota(I32, (SWA_BLOCK, 128), 1) < SWA_HEAD_DIM
    top = lax.broadcasted_iota(I32, (2 * SWA_BLOCK, 1), 0) < SWA_BLOCK
    n_blocks = q_ref.shape[1] // SWA_BLOCK
    chains = [(t, hk, par) for t in range(n_blocks) for hk in range(SWA_KV_HEADS) for par in range(2)]
    kstart, valid = {}, {}
    for t in range(n_blocks):
        j = pl.program_id(1) * n_blocks + t
        n = jnp.minimum(j, n_lat_tiles - 1)
        ws = jnp.clip((n - 1) * SWA_BLOCK, 0, s_len - band)
        kstart[t] = pl.multiple_of(ws, SWA_BLOCK)
        qpos = n * SWA_BLOCK + lax.broadcasted_iota(I32, (2 * SWA_BLOCK, band), 0) % SWA_BLOCK
        kpos = ws + lax.broadcasted_iota(I32, (2 * SWA_BLOCK, band), 1)
        valid[t] = (jnp.abs(qpos - kpos) <= SWA_WINDOW) & (j < n_lat_tiles)
    scores = {}
    for t, hk, par in chains:
        qrows = slice(t * SWA_BLOCK, (t + 1) * SWA_BLOCK)
        qpair = jnp.concatenate([q_ref[0, qrows, (2 * hk) * 128:(2 * hk + 1) * 128],
                                 q_ref[0, qrows, (2 * hk + 1) * 128:(2 * hk + 2) * 128]], axis=0)
        kcols = slice((2 * hk + par) * 128, (2 * hk + par + 1) * 128)
        s_c = _dot_nt(qpair, k_ref[0, s_len:p_len, kcols])
        s_b = jnp.where(valid[t], _dot_nt(qpair, k_ref[0, pl.ds(kstart[t], band), kcols]), MASK_VALUE)
        scores[t, hk, par] = (s_c, s_b)
    probs = {}
    for t, hk, par in chains:
        s_c, s_b = scores[t, hk, par]
        sink = jnp.where(top, sink_ref[4 * hk + par] * LOG2_E, sink_ref[4 * hk + 2 + par] * LOG2_E)
        m = jnp.maximum(jnp.maximum(jnp.max(s_c, axis=1, keepdims=True), jnp.max(s_b, axis=1, keepdims=True)), sink)
        probs[t, hk, par] = (jnp.exp2(s_c - m).astype(BF16), jnp.exp2(s_b - m).astype(BF16), jnp.exp2(sink - m))
    res = {}
    for t, hk, par in chains:
        p_c, p_b, p_sink = probs[t, hk, par]
        vcols = slice(hk * 128, (hk + 1) * 128)
        o = _dot(p_c, v_ref[0, s_len:p_len, vcols]) + _dot(p_b, v_ref[0, pl.ds(kstart[t], band), vcols])
        o = o / (o[:, SWA_HEAD_DIM:SWA_HEAD_DIM + 1] + p_sink)
        res[t, 4 * hk + par] = o[0:SWA_BLOCK]
        res[t, 4 * hk + 2 + par] = o[SWA_BLOCK:2 * SWA_BLOCK]
    for t in range(n_blocks):
        for pair in range(SWA_Q_HEADS // 2):
            y_ref[0, t * SWA_BLOCK:(t + 1) * SWA_BLOCK, pair * 128:(pair + 1) * 128] = jnp.where(
                first, res[t, 2 * pair], pltpu.roll(res[t, 2 * pair + 1], SWA_HEAD_DIM, 1)).astype(BF16)


def _swa_attention(sink, qs, ks, vs, n_ctx):
    b, p, _ = qs.shape
    tq = SWA_Q_TILE
    assert n_ctx % tq == 0 and (p - n_ctx) % tq == 0
    return pl.pallas_call(
        functools.partial(_swa_kernel, n_ctx=n_ctx),
        grid=(b, p // tq),
        in_specs=[pl.BlockSpec(memory_space=pltpu.SMEM),
                  pl.BlockSpec((1, tq, SWA_WIDTH), lambda bi, j: (bi, j, 0)),
                  pl.BlockSpec((1, p, ks.shape[2]), lambda bi, j: (bi, 0, 0)),
                  pl.BlockSpec((1, p, vs.shape[2]), lambda bi, j: (bi, 0, 0))],
        out_specs=pl.BlockSpec((1, tq, SWA_WIDTH), lambda bi, j: (bi, j, 0)),
        out_shape=jax.ShapeDtypeStruct((b, p, SWA_WIDTH), BF16),
        compiler_params=_cparams(("arbitrary", "arbitrary")),
        name="swa_attention",
    )(sink, qs, ks, vs)


def _pool_kernel(prev_ref, cur_ref, next_ref, w_ref, scale_ref, y_ref, ext_ref, *, n_ctx):
    j = pl.program_id(1)
    tm = cur_ref.shape[1]
    nj = pl.num_programs(1)
    s_len = (nj - 1) * tm
    is_ctx = j == nj - 1
    has_prev = (j >= 1) & (j < nj - 1)
    has_next = j < nj - 2
    ext_ref[0:POOL_HALO, :] = jnp.where(has_prev, prev_ref[0], 0.0)
    ext_ref[POOL_HALO:POOL_HALO + tm, :] = cur_ref[0]
    ext_ref[POOL_HALO + tm:POOL_HALO + tm + POOL_HALO, :] = jnp.where(has_next, next_ref[0], 0.0)
    t = lax.broadcasted_iota(I32, (tm, 1), 0)
    pos = jnp.where(is_ctx, t, j * tm + t)
    seg_len = jnp.where(is_ctx, n_ctx, s_len)
    for g, w in enumerate(POOL_WINDOWS):
        cols = slice(g * POOL_GROUP_DIM, (g + 1) * POOL_GROUP_DIM)
        acc = jnp.zeros((tm, POOL_GROUP_DIM), F32)
        for off in range(-(w // 2), w - w // 2):
            acc = acc + ext_ref[POOL_HALO + off:POOL_HALO + off + tm, cols]
        lo = jnp.maximum(pos - w // 2, 0)
        hi = jnp.minimum(pos + w - w // 2, seg_len)
        cnt = (hi - lo).astype(F32)
        pooled = acc / cnt - cur_ref[0, :, cols]
        y_ref[0, :, cols] = (_dot(pooled.astype(BF16), w_ref[g]) * scale_ref[:, cols]).astype(BF16)


def _pool_stage(u, w_pool, pool_scale, n_ctx):
    b, p, width = u.shape
    tm = TOKEN_TILE
    hb = tm // POOL_HALO
    n_halo_blocks = p // POOL_HALO
    return pl.pallas_call(
        functools.partial(_pool_kernel, n_ctx=n_ctx),
        grid=(b, p // tm),
        in_specs=[pl.BlockSpec((1, POOL_HALO, width), lambda bi, j: (bi, jnp.maximum(j * hb - 1, 0), 0)),
                  pl.BlockSpec((1, tm, width), lambda bi, j: (bi, j, 0)),
                  pl.BlockSpec((1, POOL_HALO, width),
                               lambda bi, j: (bi, jnp.minimum((j + 1) * hb, n_halo_blocks - 1), 0)),
                  pl.BlockSpec(w_pool.shape, lambda bi, j: (0, 0, 0)),
                  pl.BlockSpec(pool_scale.shape, lambda bi, j: (0, 0))],
        out_specs=pl.BlockSpec((1, tm, width), lambda bi, j: (bi, j, 0)),
        out_shape=jax.ShapeDtypeStruct((b, p, width), BF16),
        scratch_shapes=[pltpu.VMEM((tm + 2 * POOL_HALO, width), F32)],
        compiler_params=_cparams(("arbitrary", "arbitrary")),
        name="pool_stage",
    )(u, u, u, w_pool, pool_scale)


MERGE_TILES = 2


def _merge_kernel(*refs, n_stream, ctx_tile, tiles_per_row):
    n_in = n_stream + 6
    tiles = [refs[t * n_in:(t + 1) * n_in] for t in range(MERGE_TILES)]
    (g_ref, wbm_ref, wbp_ref, wbs_ref, wout_ref, wsg_ref, wsu_ref, wsd_ref, wrt_ref,
     xs_ref, fp_ref, lg_ref) = refs[MERGE_TILES * n_in:]
    d = SLAB_ROWS * 128
    tm = tiles[0][n_stream + 2].shape[1]
    both = range(MERGE_TILES)
    is_ctx = [(pl.program_id(0) * MERGE_TILES + t) % tiles_per_row == ctx_tile for t in both]
    mod = [tiles[t][n_stream + 5][0, 0] for t in both]
    gate = [tiles[t][n_stream + 4][0] for t in both]
    branches = []
    for t in both:
        yml_ref, ymc_ref, yp_ref, ys_ref = tiles[t][n_stream:n_stream + 4]
        ym = jnp.where(is_ctx[t], ymc_ref[0], yml_ref[0])
        branches.append((_dot(ym, wbm_ref[...]), _dot(yp_ref[0], wbp_ref[...]), _dot(ys_ref[0], wbs_ref[...])))
    merged = [(gate[t][:, 0:d].astype(F32) * branches[t][0] + gate[t][:, d:2 * d].astype(F32) * branches[t][1]
               + gate[t][:, 2 * d:3 * d].astype(F32) * branches[t][2]).astype(BF16) for t in both]
    proj = [_dot(merged[t], wout_ref[...]) for t in both]
    x_mid = [_stream_tile(tiles[t][:n_stream], tm, is_ctx[t]) + mod[t][2:3] * proj[t] for t in both]
    f = [_rms(x_mid[t], g_ref[...]) * (1.0 + mod[t][4:5]) + mod[t][3:4] for t in both]
    fb = [f[t].astype(BF16) for t in both]
    up = [(_dot(fb[t], wsg_ref[...]), _dot(fb[t], wsu_ref[...])) for t in both]
    for t in both:
        lg_ref[:, t * tm:(t + 1) * tm] = _dot_nt(wrt_ref[...], fb[t])
    hmid = [(up[t][0] * _sigmoid(up[t][0]) * up[t][1]).astype(BF16) for t in both]
    shared = [_dot(hmid[t], wsd_ref[...]) for t in both]
    for t in both:
        _matrix_to_slab_rows(xs_ref.at[t * tm * SLAB_ROWS:(t + 1) * tm * SLAB_ROWS, :],
                             x_mid[t] + mod[t][5:6] * shared[t], SLAB_ROWS)
        _matrix_to_slab_rows(fp_ref.at[t * tm * PACKED_ROWS:(t + 1) * tm * PACKED_ROWS, :],
                             _pack_bf16_pair(f[t]), PACKED_ROWS)


def _merge_stage(stream, ym_lat, ym_ctx, yp, ys, gates, modtab, g_ffn, wbm, wbp, wbs, wout, wsg, wsu, wsd, wrt,
                 keep_ctx):
    b, p, _ = yp.shape
    assert ym_ctx.shape[1] == TOKEN_TILE
    d = SLAB_ROWS * 128
    tm = TOKEN_TILE
    nj = p // tm
    nk = nj if keep_ctx else nj - 1
    per = MERGE_TILES
    assert (b * nk) % per == 0
    full = lambda a: pl.BlockSpec(a.shape, lambda s: (0,) * a.ndim)

    def tile_specs(t):
        row = lambda s: (s * per + t) // nk
        col = lambda s: (s * per + t) % nk
        tok = lambda w: pl.BlockSpec((1, tm, w), lambda s: (row(s), col(s), 0))
        lat = lambda w: pl.BlockSpec((1, tm, w), lambda s: (row(s), jnp.minimum(col(s), nj - 2), 0))
        ctx = lambda w: pl.BlockSpec((1, tm, w), lambda s: (row(s), 0, 0))
        if len(stream) == 1:
            x_specs = [pl.BlockSpec((tm * SLAB_ROWS, 128), lambda s: (row(s) * nj + col(s), 0))]
        else:
            x_specs = [lat(d), ctx(d)]
        return x_specs + [lat(ym_lat.shape[2]), ctx(ym_ctx.shape[2]), tok(yp.shape[2]), tok(ys.shape[2]),
                          tok(gates.shape[2]),
                          pl.BlockSpec((1, 1, 8, d), lambda s: (row(s), col(s) // (nj - 1), 0, 0))]

    tile_args = list(stream) + [ym_lat, ym_ctx, yp, ys, gates, modtab]
    return pl.pallas_call(
        functools.partial(_merge_kernel, n_stream=len(stream), ctx_tile=nj - 1, tiles_per_row=nk),
        grid=(b * nk // per,),
        in_specs=[spec for t in range(per) for spec in tile_specs(t)] + [
            full(g_ffn), full(wbm), full(wbp), full(wbs), full(wout), full(wsg), full(wsu), full(wsd), full(wrt)],
        out_specs=[pl.BlockSpec((per * tm * SLAB_ROWS, 128), lambda s: (s, 0)),
                   pl.BlockSpec((per * tm * PACKED_ROWS, 128), lambda s: (s, 0)),
                   pl.BlockSpec((N_EXPERTS, per * tm), lambda s: (0, s))],
        out_shape=[jax.ShapeDtypeStruct((b * nk * tm * SLAB_ROWS, 128), F32),
                   jax.ShapeDtypeStruct((b * nk * tm * PACKED_ROWS, 128), U32),
                   jax.ShapeDtypeStruct((N_EXPERTS, b * nk * tm), F32)],
        compiler_params=_cparams(("arbitrary",)),
        name="merge_stage",
    )(*(tile_args * per), g_ffn, wbm, wbp, wbs, wout, wsg, wsu, wsd, wrt)


def _route_kernel(lg_ref, bias_ref, dest_ref, wts_ref, seg_ref):
    tm = lg_ref.shape[1]
    ne = N_EXPERTS
    neg_inf = -jnp.inf
    scores = _sigmoid(lg_ref[...])
    sel = scores + bias_ref[...]
    iota_g = lax.broadcasted_iota(I32, (EXPERTS_PER_GROUP, tm), 0)
    gscore = []
    for g in range(N_GROUPS):
        sg = sel[g * EXPERTS_PER_GROUP:(g + 1) * EXPERTS_PER_GROUP]
        m1 = jnp.max(sg, axis=0, keepdims=True)
        i1 = jnp.min(jnp.where(sg == m1, iota_g, EXPERTS_PER_GROUP), axis=0, keepdims=True)
        m2 = jnp.max(jnp.where(iota_g == i1, neg_inf, sg), axis=0, keepdims=True)
        gscore.append(m1 + m2)
    rows = []
    for g in range(N_GROUPS):
        rank = jnp.zeros((1, tm), I32)
        for g2 in range(N_GROUPS):
            if g2 == g:
                continue
            beats = (gscore[g2] >= gscore[g]) if g2 < g else (gscore[g2] > gscore[g])
            rank = rank + beats.astype(I32)
        rows.append(jnp.where(rank < TOPK_GROUPS, sel[g * EXPERTS_PER_GROUP:(g + 1) * EXPERTS_PER_GROUP], neg_inf))
    cur = jnp.concatenate(rows, axis=0)
    iota_e = lax.broadcasted_iota(I32, (ne, tm), 0)
    picks = []
    member = jnp.zeros((ne, tm), F32)
    for _ in range(TOP_K):
        m = jnp.max(cur, axis=0, keepdims=True)
        idx = jnp.min(jnp.where(cur == m, iota_e, ne), axis=0, keepdims=True)
        hit = iota_e == idx
        picks.append(hit)
        member = member + hit.astype(F32)
        cur = jnp.where(hit, neg_inf, cur)
    earlier = (lax.broadcasted_iota(I32, (tm, tm), 0) < lax.broadcasted_iota(I32, (tm, tm), 1)).astype(BF16)
    pos = _dot(member.astype(BF16), earlier)
    cnt_col = jnp.sum(member, axis=1, keepdims=True)
    blocks_col = jnp.floor((cnt_col + (SEG_ALIGN - 1)) * (1.0 / SEG_ALIGN))
    lower = (lax.broadcasted_iota(I32, (ne, ne), 1) < lax.broadcasted_iota(I32, (ne, ne), 0)).astype(BF16)
    off_col = _dot(lower, jnp.broadcast_to(blocks_col, (ne, 128)).astype(BF16))[:, 0:1] * SEG_ALIGN
    base = off_col + pos
    w_rows = [jnp.sum(jnp.where(hit, scores, 0.0), axis=0, keepdims=True) for hit in picks]
    denom = w_rows[0]
    for w in w_rows[1:]:
        denom = denom + w
    for k, hit in enumerate(picks):
        dest_ref[0, k:k + 1, :] = jnp.sum(jnp.where(hit, base, 0.0), axis=0, keepdims=True).astype(I32)
        wts_ref[0, k:k + 1, :] = w_rows[k] / denom * ROUTED_SCALE
    member_pad = jnp.concatenate([member, jnp.zeros((128 - ne, tm), F32)], axis=0).astype(BF16)
    cnt_row = _dot_nt(jnp.ones((8, tm), BF16), member_pad)
    blocks_row = jnp.floor((cnt_row + (SEG_ALIGN - 1)) * (1.0 / SEG_ALIGN))
    before = (lax.broadcasted_iota(I32, (128, 128), 0) < lax.broadcasted_iota(I32, (128, 128), 1)).astype(BF16)
    off_row = _dot(blocks_row.astype(BF16), before) * SEG_ALIGN
    r = lax.broadcasted_iota(I32, (8, 128), 0)
    seg_ref[0] = jnp.where(r == 0, off_row, jnp.where(r == 1, cnt_row, 0.0)).astype(I32)


def _route_stage(lg_t, bias_col):
    ne, t_all = lg_t.shape
    tm = MOE_TILE
    nt = t_all // tm
    return pl.pallas_call(
        _route_kernel,
        grid=(nt,),
        in_specs=[pl.BlockSpec((ne, tm), lambda i: (0, i)),
                  pl.BlockSpec((ne, 1), lambda i: (0, 0))],
        out_specs=[pl.BlockSpec((1, TOP_K, tm), lambda i: (i, 0, 0)),
                   pl.BlockSpec((1, TOP_K, tm), lambda i: (i, 0, 0)),
                   pl.BlockSpec((1, 8, 128), lambda i: (i, 0, 0))],
        out_shape=[jax.ShapeDtypeStruct((nt, TOP_K, tm), I32),
                   jax.ShapeDtypeStruct((nt, TOP_K, tm), F32),
                   jax.ShapeDtypeStruct((nt, 8, 128), I32)],
        compiler_params=_cparams(("arbitrary",)),
        name="route_stage",
    )(lg_t, bias_col)


def _moe_row_stride(tm):
    cap = TOP_K * tm + N_EXPERTS * SEG_ALIGN + MOE_CHUNK
    blocks = cap // 8 + 1
    return 8 * (blocks + 1 - blocks % 2)


def _moe_kernel(dest_ref, wts_ref, seg_ref, fp_hbm, xs_hbm, gt_ref, gfin_ref, wg_ref, wu_ref, wd_ref, out_hbm,
                bufs_ref, fp_stage, xs_stage, out_stage, slab_tmp, fp_sem, xs_sem, out_sem,
                *, srow, sub, n_sub_per_batch, ctx_sub, n_tiles, final_norm):
    pr = pl.program_id(0)
    eg = pl.program_id(1)
    tm = dest_ref.shape[2]
    ch = MOE_CHUNK
    group = wg_ref.shape[0]
    lane_blk = 128
    n_blk = tm // lane_blk
    per = bufs_ref.shape[0]
    n_slots = fp_stage.shape[0]
    n_here = jnp.minimum(per, n_tiles - pr * per)

    def table_block(ref, half, blk):
        return ref.at[half, :, pl.ds(pl.multiple_of(blk * lane_blk, lane_blk), lane_blk)]

    def table_rows(ref, half, blk):
        cols = pl.ds(pl.multiple_of(blk * lane_blk, lane_blk), lane_blk)
        return [ref.at[half, k, cols] for k in range(TOP_K)]

    def token_rows(half, blk, per_tok):
        first = ((pr * per + half) * tm + blk * lane_blk) * per_tok
        return pl.ds(pl.multiple_of(first, lane_blk * per_tok), lane_blk * per_tok)

    def fp_copy(half, blk, slot):
        return pltpu.make_async_copy(fp_hbm.at[token_rows(half, blk, PACKED_ROWS), :], fp_stage.at[slot],
                                     fp_sem.at[slot])

    def xs_copy(half, blk, slot):
        return pltpu.make_async_copy(xs_hbm.at[token_rows(half, blk, SLAB_ROWS), :], xs_stage.at[slot],
                                     xs_sem.at[slot])

    def out_copy(half, blk, slot):
        if not final_norm:
            dst = out_hbm.at[token_rows(half, blk, SLAB_ROWS), :]
        else:
            first = (pr * per + half) * tm + blk * lane_blk
            row_len = out_hbm.shape[1]
            bi = first // row_len
            dst = out_hbm.at[bi, pl.ds(pl.multiple_of(first - bi * row_len, lane_blk), lane_blk), :]
        return pltpu.make_async_copy(out_stage.at[slot], dst, out_sem.at[slot])

    def group_rows(half):
        buf_ref = bufs_ref.at[half]
        for blk in range(n_slots - 1):
            fp_copy(half, blk, blk).start()
        buf_ref[...] = jnp.zeros(buf_ref.shape, U32)

        def body(blk, carry):
            slot = blk % n_slots
            fp_copy(half, blk, slot).wait()

            @pl.when(blk + n_slots - 1 < n_blk)
            def _prefetch():
                fp_copy(half, blk + n_slots - 1, (blk + n_slots - 1) % n_slots).start()

            dest = table_rows(dest_ref, half, blk)
            rows = fp_stage.at[slot]
            for u in range(lane_blk):
                slab = rows[u * PACKED_ROWS:(u + 1) * PACKED_ROWS, :]
                for k in range(TOP_K):
                    buf_ref[pl.ds(dest[k][u], PACKED_ROWS, stride=srow), :] = slab
            return carry

        lax.fori_loop(0, n_blk, body, 0)

    def experts():
        def one_expert(ge, carry):
            e = eg * group + ge
            counts = [seg_ref[half, 1, e] for half in range(per)]
            offsets = [seg_ref[half, 0, e] for half in range(per)]

            def ffn_rows(halves, rows, chunk_idx):
                words, keep = [], []
                for h in halves:
                    r0 = offsets[h] + chunk_idx * rows
                    words.append([bufs_ref[h, pl.ds(pl.multiple_of(q * srow + r0, 8), rows), :] for q in range(4)])
                    keep.append(lax.broadcasted_iota(I32, (rows, 1), 0) < counts[h] - chunk_idx * rows)
                xb = []
                for ws in words:
                    w = jnp.concatenate(ws, axis=1)
                    xb.append(jnp.concatenate([_unpack_lo(w), _unpack_hi(w)], axis=1).astype(BF16))
                gates = [(_dot(x, wg_ref[ge]), _dot(x, wu_ref[ge])) for x in xb]
                hmid = [(g * _sigmoid(g) * u).astype(BF16) for g, u in gates]
                packed = [_pack_bf16_pair(_dot(hm, wd_ref[ge])) for hm in hmid]
                for n, h in enumerate(halves):
                    r0 = offsets[h] + chunk_idx * rows
                    for q in range(4):
                        bufs_ref[h, pl.ds(pl.multiple_of(q * srow + r0, 8), rows), :] = jnp.where(
                            keep[n], packed[n][:, q * 128:(q + 1) * 128], words[n][q])

            most = counts[0]
            for c in counts[1:]:
                most = jnp.maximum(most, c)
            lo = 0
            for rows in MOE_BODY_ROWS:
                pl.when((most > lo) & (most <= rows))(functools.partial(ffn_rows, tuple(range(per)), rows, 0))
                lo = rows

            @pl.when(most > ch)
            def _long_segments():
                for h in range(per):
                    def chunk(c, carry2, h=h):
                        ffn_rows((h,), ch, c)
                        return carry2

                    lax.fori_loop(0, (counts[h] + ch - 1) // ch, chunk, 0)

            return carry

        lax.fori_loop(0, group, one_expert, 0)

    def combine(half):
        buf_ref = bufs_ref.at[half]
        for blk in range(n_slots - 1):
            xs_copy(half, blk, blk).start()

        def body(blk, carry):
            slot = blk % n_slots
            xs_copy(half, blk, slot).wait()

            @pl.when(blk + n_slots - 1 < n_blk)
            def _prefetch():
                xs_copy(half, blk + n_slots - 1, (blk + n_slots - 1) % n_slots).start()

            @pl.when(blk >= n_slots)
            def _slot_free():
                out_copy(half, blk - n_slots, slot).wait()

            sub_blk = ((pr * per + half) * tm + blk * lane_blk) // sub
            bi = sub_blk // n_sub_per_batch
            gate = gt_ref[bi * 2 + (sub_blk - bi * n_sub_per_batch) // ctx_sub]
            dest = table_block(dest_ref, half, blk)
            wts = table_block(wts_ref, half, blk)
            xs = xs_stage.at[slot]
            out = slab_tmp if final_norm else out_stage.at[slot]
            for u in range(lane_blk):
                acc_lo = jnp.zeros((4, 128), F32)
                acc_hi = jnp.zeros((4, 128), F32)
                for k in range(TOP_K):
                    words = buf_ref[pl.ds(dest[k, u], PACKED_ROWS, stride=srow), :]
                    wk = wts[k, u]
                    acc_lo = acc_lo + wk * _unpack_lo(words)
                    acc_hi = acc_hi + wk * _unpack_hi(words)
                out[u * 8:u * 8 + 4, :] = xs[u * 8:u * 8 + 4, :] + gate[0:4] * acc_lo
                out[u * 8 + 4:u * 8 + 8, :] = xs[u * 8 + 4:u * 8 + 8, :] + gate[4:8] * acc_hi
            if final_norm:
                out_stage[slot] = _rms(_slab_rows_to_matrix(slab_tmp, lane_blk, SLAB_ROWS), gfin_ref[...])
            out_copy(half, blk, slot).start()
            return carry

        lax.fori_loop(0, n_blk, body, 0)
        for blk in range(n_blk - n_slots, n_blk):
            out_copy(half, blk, blk % n_slots).wait()

    def each_tile(fn):
        def body(half, carry):
            fn(half)
            return carry

        lax.fori_loop(0, n_here, body, 0)

    @pl.when(eg == 0)
    def _first_step():
        each_tile(group_rows)
        for half in range(1, per):
            @pl.when(half >= n_here)
            def _clear(half=half):
                bufs_ref[half] = jnp.zeros(bufs_ref.shape[1:], U32)

    experts()
    pl.when(eg == pl.num_programs(1) - 1)(functools.partial(each_tile, combine))


def _moe_stage(dest, wts, seg, fp4, xs8, gt2, g_final, wg, wu, wd, layer, n_sub_per_batch, keep_ctx, final_norm):
    ctx_sub = n_sub_per_batch - 1 if keep_ctx else n_sub_per_batch
    d = SLAB_ROWS * 128
    tokens_per_row = n_sub_per_batch * TOKEN_TILE
    out_shape = (xs8.shape[0] // (SLAB_ROWS * tokens_per_row), tokens_per_row, d) if final_norm else xs8.shape
    out_block = (128, d) if final_norm else (128 * SLAB_ROWS, 128)
    nt, _, tm = dest.shape
    ne = wg.shape[1]
    group = MOE_EXPERTS_PER_STEP
    per = MOE_TILES_PER_STEP
    assert ne % group == 0 and tm % 256 == 0 and TOKEN_TILE % 128 == 0 and per == 2
    n_rows = -(-nt // per)
    pad = ((0, n_rows * per - nt), (0, 0), (0, 0))
    dest, wts, seg = jnp.pad(dest, pad), jnp.pad(wts, pad), jnp.pad(seg, pad)
    srow = _moe_row_stride(tm)
    smem = lambda shape: pl.BlockSpec(shape, lambda i, e: (i, 0, 0), memory_space=pltpu.SMEM)
    hbm = pl.BlockSpec(memory_space=pl.ANY)
    slots = MOE_STAGE_SLOTS
    assert tm // 128 >= slots
    stage = lambda per_tok, dt: pltpu.VMEM((slots, 128 * per_tok, 128), dt)
    return pl.pallas_call(
        functools.partial(_moe_kernel, srow=srow, sub=TOKEN_TILE, n_sub_per_batch=n_sub_per_batch, ctx_sub=ctx_sub,
                          n_tiles=nt, final_norm=final_norm),
        grid=(n_rows, ne // group),
        in_specs=[smem((per, TOP_K, tm)), smem((per, TOP_K, tm)), smem((per, 8, 128)),
                  hbm, hbm,
                  pl.BlockSpec(gt2.shape, lambda i, e: (0, 0, 0)),
                  pl.BlockSpec(g_final.shape, lambda i, e: (0, 0)),
                  pl.BlockSpec((None, group) + wg.shape[2:], lambda i, e: (layer, e, 0, 0)),
                  pl.BlockSpec((None, group) + wu.shape[2:], lambda i, e: (layer, e, 0, 0)),
                  pl.BlockSpec((None, group) + wd.shape[2:], lambda i, e: (layer, e, 0, 0))],
        out_specs=hbm,
        out_shape=jax.ShapeDtypeStruct(out_shape, F32),
        scratch_shapes=[pltpu.VMEM((per, 4 * srow, 128), U32),
                        stage(PACKED_ROWS, U32), stage(SLAB_ROWS, F32), pltpu.VMEM((slots,) + out_block, F32),
                        pltpu.VMEM((128 * SLAB_ROWS, 128), F32),
                        pltpu.SemaphoreType.DMA((slots,)), pltpu.SemaphoreType.DMA((slots,)),
                        pltpu.SemaphoreType.DMA((slots,))],
        compiler_params=_cparams(("arbitrary", "arbitrary")),
        name="moe_stage",
    )(dest, wts, seg, fp4, xs8, gt2, g_final, wg, wu, wd)


def _rot_lanes(x, head_dim):
    quarter = head_dim // 4
    lane = lax.broadcasted_iota(I32, x.shape, 1)
    width = x.shape[1]
    return jnp.where(lane % (2 * quarter) < quarter, -pltpu.roll(x, width - quarter, 1), pltpu.roll(x, quarter, 1))


def _w1_kernel(w_ref, o_ref):
    offs = [0]
    for w in (MLA_Q_LORA, MLA_KV_LORA, MLA_ROPE, POOL_WIDTH, SWA_WIDTH, SWA_KV_WIDTH, SWA_KV_WIDTH):
        offs.append(offs[-1] + w)
    o_cq, o_ckv, o_kr, o_u, o_qs, o_ks, o_vs, o_gl = offs
    lane = lax.broadcasted_iota(I32, (w_ref.shape[0], 128), 1)

    def put(name, val):
        a, b = _SEG[name]
        o_ref[:, a:b] = val.astype(BF16)

    def kv_slots(x):
        low = lane < SWA_HEAD_DIM
        return jnp.concatenate([jnp.where(low, x, 0.0), jnp.where(low, pltpu.roll(x, SWA_HEAD_DIM, 1), 0.0)], axis=1)

    def kr_slot(x):
        return jnp.where((lane >= MLA_NOPE) & (lane < MLA_NOPE + MLA_ROPE), pltpu.roll(x, MLA_NOPE, 1), 0.0)

    put("cq", w_ref[:, o_cq:o_ckv])
    put("ckv", w_ref[:, o_ckv:o_kr])
    put("u", w_ref[:, o_u:o_qs])
    put("qs", w_ref[:, o_qs:o_ks] * (SWA_SCALE * LOG2_E))
    put("ks", kv_slots(w_ref[:, o_ks:o_vs]))
    put("vs", kv_slots(w_ref[:, o_vs:o_gl]))
    put("kr", kr_slot(w_ref[:, o_kr:o_kr + 128]))
    put("gl", w_ref[:, o_gl:w_ref.shape[1]])


def _fused_in_weight(w_in, layer):
    _, d, width = w_in.shape
    tr = 256
    assert d % tr == 0 and SWA_KV_WIDTH == 128 and MLA_KV_LORA + MLA_Q_LORA == 5 * 128
    return pl.pallas_call(
        _w1_kernel,
        grid=(d // tr,),
        in_specs=[pl.BlockSpec((None, tr, width), lambda i: (layer, i, 0))],
        out_specs=pl.BlockSpec((tr, FUSED_IN_WIDTH), lambda i: (i, 0)),
        out_shape=jax.ShapeDtypeStruct((d, FUSED_IN_WIDTH), BF16),
        compiler_params=_cparams(("arbitrary",)),
        name="fused_in_weight",
    )(w_in)


def _rope_pattern(s_len, n_ctx, rot_dim):
    t = jnp.arange(s_len)
    row = (t // GRID_W).astype(F32)
    col = (t % GRID_W).astype(F32)
    n_freq = rot_dim // 4
    inv_freq = ROPE_BASE ** (-jnp.arange(n_freq, dtype=F32) / n_freq)
    ang_r = row[:, None] * inv_freq[None, :]
    ang_c = col[:, None] * inv_freq[None, :]
    cos = jnp.concatenate([jnp.cos(ang_r), jnp.cos(ang_r), jnp.cos(ang_c), jnp.cos(ang_c)], axis=1)
    sin = jnp.concatenate([jnp.sin(ang_r), jnp.sin(ang_r), jnp.sin(ang_c), jnp.sin(ang_c)], axis=1)
    cos = jnp.concatenate([cos, jnp.ones((n_ctx, rot_dim), F32)], axis=0)
    sin = jnp.concatenate([sin, jnp.zeros((n_ctx, rot_dim), F32)], axis=0)
    return cos, sin


def _layer_weights(w_in, layer, w_uq, w_ukv):
    w1 = _fused_in_weight(w_in, layer)
    lq = w_uq.shape[0]
    wq3 = (w_uq * (MLA_SCALE * LOG2_E)).reshape(lq, MLA_HEADS, MLA_NOPE + MLA_ROPE)
    pad = jnp.zeros((lq, MLA_HEADS, MLA_HEAD_PAD - MLA_NOPE - MLA_ROPE), F32)
    wq = jnp.concatenate([wq3, pad], axis=2).reshape(lq, MLA_HEADS * MLA_HEAD_PAD).astype(BF16)
    lkv = w_ukv.shape[0]
    wkv3 = w_ukv.reshape(lkv, MLA_HEADS, MLA_NOPE + MLA_V)
    wkn = jnp.concatenate([wkv3[:, :, :MLA_NOPE], jnp.zeros((lkv, MLA_HEADS, MLA_HEAD_PAD - MLA_NOPE), F32)],
                          axis=2).reshape(lkv, MLA_HEADS * MLA_HEAD_PAD).astype(BF16)
    wv = jnp.concatenate([wkv3[:, :, MLA_NOPE:], jnp.zeros((lkv, MLA_HEADS, MLA_HEAD_PAD - MLA_V), F32)],
                         axis=2).reshape(lkv, MLA_HEADS * MLA_HEAD_PAD).astype(BF16)
    return w1, wq, wkn, wv


def kernel(x, c, ctx, c_ctx, w_mod, b_mod, g_mix, g_ffn, w_in, g_mla_q, g_mla_kv, w_mla_uq, w_mla_ukv, w_pool,
           pool_scale, swa_sink, w_br_mla, w_br_pool, w_br_swa, w_out, w_router, router_bias, w_exp_gate,
           w_exp_up, w_exp_down, w_sh_gate, w_sh_up, w_sh_down, g_final):
    b, s_len, d = x.shape
    n_ctx = ctx.shape[1]
    n_layers = w_mod.shape[0]
    p = n_ctx + s_len
    assert n_ctx == TOKEN_TILE and s_len % TOKEN_TILE == 0 and (b * p) % MOE_TILE == 0 and b + 1 <= 8
    assert d == 1024 and w_in.shape[2] == (MLA_Q_LORA + MLA_KV_LORA + MLA_ROPE + POOL_WIDTH + SWA_WIDTH
                                           + 2 * SWA_KV_WIDTH + 3 * d)

    cm, sm = _rope_pattern(s_len, n_ctx, MLA_ROPE)
    tail = jnp.zeros((p, MLA_HEAD_PAD - MLA_NOPE - MLA_ROPE), F32)
    tabm = jnp.concatenate([jnp.ones((p, MLA_NOPE), F32), cm, tail, jnp.zeros((p, MLA_NOPE), F32), sm, tail], axis=1)
    cs, ss = _rope_pattern(s_len, n_ctx, SWA_HEAD_DIM)
    tabs = jnp.concatenate([cs, cs, ss, ss], axis=1)

    cvec = jnp.concatenate([c, c_ctx[None, :], jnp.zeros((8 - b - 1, d), F32)], axis=0)
    mod_all = _modulation(cvec, w_mod, b_mod).reshape(n_layers, 8, N_MOD, d)

    stream = (x, ctx)
    wg_all, wu_all, wd_all = w_exp_gate.astype(BF16), w_exp_up.astype(BF16), w_exp_down.astype(BF16)
    for i in range(n_layers):
        keep_ctx = i < n_layers - 1
        lat = mod_all[i, :b]
        ctx_rows = jnp.broadcast_to(mod_all[i, b][None], (b, N_MOD, d))
        modtab = jnp.pad(jnp.stack([lat, ctx_rows], axis=1), ((0, 0), (0, 0), (0, 8 - N_MOD), (0, 0)))
        w1, wq, wkn, wv = _layer_weights(w_in, i, w_mla_uq[i], w_mla_ukv[i])
        q, k, v, u, qs, ks, vs, gates = _input_stage(
            stream, b, modtab, g_mix[i][None], w1, g_mla_q[i][None], wq, g_mla_kv[i][None], wkn, wv, tabm, tabs)
        y_mla, y_mla_ctx = _mla_attention(q, k, v, n_ctx)
        y_swa = _swa_attention(swa_sink[i], qs, ks, vs, n_ctx)
        y_pool = _pool_stage(u, w_pool[i].astype(BF16), pool_scale[i][None], n_ctx)
        xs, fp, lg_t = _merge_stage(
            stream, y_mla, y_mla_ctx, y_pool, y_swa, gates, modtab, g_ffn[i][None], w_br_mla[i].astype(BF16),
            w_br_pool[i].astype(BF16), w_br_swa[i].astype(BF16), w_out[i].astype(BF16),
            w_sh_gate[i].astype(BF16), w_sh_up[i].astype(BF16), w_sh_down[i].astype(BF16),
            w_router[i].T.astype(BF16), keep_ctx)
        dest, wts, seg = _route_stage(lg_t, router_bias[i][:, None])
        gt2 = modtab[:, :, 5, :].reshape(b * 2, 8, d // 8)
        stream = (_moe_stage(dest, wts, seg, fp, xs, gt2, g_final[None], wg_all, wu_all, wd_all, i,
                             (p if keep_ctx else s_len) // TOKEN_TILE, keep_ctx, final_norm=not keep_ctx),)
    return stream[0]
```

```python
import functools

import jax
import jax.numpy as jnp
from jax import lax
from jax.experimental import pallas as pl
from jax.experimental.pallas import tpu as pltpu

F32 = jnp.float32
BF16 = jnp.bfloat16
U32 = jnp.uint32
I32 = jnp.int32

NORM_EPS = 1e-6
ROPE_BASE = 10000.0
GRID_W = 64
N_MOD = 6

MLA_HEADS = 8
MLA_Q_LORA = 384
MLA_KV_LORA = 256
MLA_NOPE = 64
MLA_ROPE = 32
MLA_V = 64
MLA_SCALE = (MLA_NOPE + MLA_ROPE) ** -0.5
MLA_HEAD_PAD = 128
LOG2_E = 1.4426950408889634

POOL_WINDOWS = (2, 4, 8, 16)
POOL_GROUP_DIM = 128
POOL_WIDTH = 512
POOL_HALO = 8

SWA_Q_HEADS = 8
SWA_KV_HEADS = 2
SWA_HEAD_DIM = 64
SWA_WINDOW = 128
SWA_BLOCK = 128
SWA_SCALE = SWA_HEAD_DIM ** -0.5
SWA_WIDTH = SWA_Q_HEADS * SWA_HEAD_DIM
SWA_KV_WIDTH = SWA_KV_HEADS * SWA_HEAD_DIM

N_EXPERTS = 64
TOP_K = 8
N_GROUPS = 8
TOPK_GROUPS = 4
EXPERTS_PER_GROUP = 8
D_EXPERT = 256
ROUTED_SCALE = 2.5

TOKEN_TILE = 256
SWA_Q_TILE = 256
MLA_Q_TILE = 1024
MLA_KEY_CHUNK = 256
MOE_TILE = 1024
MOE_CHUNK = 256
MOE_BODY_ROWS = (128, 144, 160, 176, 192, 224, 256)
MOE_EXPERTS_PER_STEP = 2
MOE_TILES_PER_STEP = 2
MOE_STAGE_SLOTS = 4
SEG_ALIGN = 8
SLAB_ROWS = 8
PACKED_ROWS = 4
MASK_VALUE = -1e30
HI16 = 0xFFFF0000

VMEM_LIMIT = 56 * 1024 * 1024

_SEG_WIDTHS = (("cq", 384), ("ckv", 256), ("u", 512), ("qs", 512), ("ks", 256), ("vs", 256), ("kr", 128),
               ("gl", 3072))
_SEG = {}
_o = 0
for _n, _w in _SEG_WIDTHS:
    _SEG[_n] = (_o, _o + _w)
    _o += _w
FUSED_IN_WIDTH = _o


def _cparams(sem):
    return pltpu.CompilerParams(dimension_semantics=sem, vmem_limit_bytes=VMEM_LIMIT)


def _dot(a, b):
    return jnp.dot(a, b, preferred_element_type=F32)


def _dot_nt(a, b):
    return lax.dot_general(a, b, (((1,), (1,)), ((), ())), preferred_element_type=F32)


def _sigmoid(x):
    return 1.0 / (1.0 + jnp.exp(-x))


def _rms(x, g):
    return x * lax.rsqrt(jnp.mean(x * x, axis=-1, keepdims=True) + NORM_EPS) * g


def _pack_bf16_pair(v):
    n = v.shape[1] // 2
    bits = pltpu.bitcast(v.astype(BF16).astype(F32), U32)
    return (bits[:, :n] >> 16) | (bits[:, n:] & jnp.uint32(HI16))


def _slab_rows_to_matrix(ref, n_tok, per_tok):
    return jnp.concatenate([ref[pl.ds(c, n_tok, stride=per_tok), :] for c in range(per_tok)], axis=1)


def _matrix_to_slab_rows(ref, val, per_tok):
    n_tok = val.shape[0]
    for c in range(per_tok):
        ref[pl.ds(c, n_tok, stride=per_tok), :] = val[:, c * 128:(c + 1) * 128]


def _unpack_lo(w):
    return pltpu.bitcast(w << 16, F32)


def _unpack_hi(w):
    return pltpu.bitcast(w & jnp.uint32(HI16), F32)


def _mod_kernel(c_ref, w_ref, b_ref, o_ref):
    c = c_ref[...]
    a = (c * _sigmoid(c)).astype(BF16)
    o_ref[0] = _dot(a, w_ref[0].astype(BF16)) + b_ref[0]


def _modulation(cvec, w_mod, b_mod):
    n_layers, d, width = w_mod.shape
    tn = width // 4
    return pl.pallas_call(
        _mod_kernel,
        grid=(n_layers, width // tn),
        in_specs=[pl.BlockSpec((8, d), lambda l, n: (0, 0)),
                  pl.BlockSpec((1, d, tn), lambda l, n: (l, 0, n)),
                  pl.BlockSpec((1, 1, tn), lambda l, n: (l, 0, n))],
        out_specs=pl.BlockSpec((1, 8, tn), lambda l, n: (l, 0, n)),
        out_shape=jax.ShapeDtypeStruct((n_layers, 8, width), F32),
        compiler_params=_cparams(("arbitrary", "arbitrary")),
        name="modulation",
    )(cvec, w_mod, b_mod.reshape(n_layers, 1, width))


def _stream_tile(refs, n_tok, is_ctx):
    if len(refs) == 1:
        return _slab_rows_to_matrix(refs[0], n_tok, SLAB_ROWS)
    return jnp.where(is_ctx, refs[1][0], refs[0][0])


def _stream_tokens(stream, b):
    return stream[0].shape[0] // (b * SLAB_ROWS) if len(stream) == 1 else stream[0].shape[1] + stream[1].shape[1]


INPUT_TILES = 2


def _in_kernel(*refs, n_stream, ctx_tile, tiles_per_row):
    n_in = n_stream + 3
    tiles = [refs[t * n_in:(t + 1) * n_in] for t in range(INPUT_TILES)]
    (g_ref, w1_ref, gq_ref, wq_ref, gkv_ref, wkn_ref, wv_ref,
     q_ref, k_ref, v_ref, u_ref, qs_ref, ks_ref, vs_ref, gate_ref) = refs[INPUT_TILES * n_in:]
    tm = tiles[0][n_stream + 1].shape[0]
    both = range(INPUT_TILES)
    rows = [slice(t * tm, (t + 1) * tm) for t in both]
    hb, tabs = [], []
    for t in both:
        is_ctx = (pl.program_id(0) * INPUT_TILES + t) % tiles_per_row == ctx_tile
        mod = tiles[t][n_stream][0, 0]
        x = _stream_tile(tiles[t][:n_stream], tm, is_ctx)
        hb.append((_rms(x, g_ref[...]) * (1.0 + mod[1:2]) + mod[0:1]).astype(BF16))
        tabm_ref, tabs_ref = tiles[t][n_stream + 1:n_stream + 3]
        tabs.append((tabm_ref[:, 0:128], tabm_ref[:, 128:256], tabs_ref[:, 0:128], tabs_ref[:, 128:256]))

    def seg(name):
        a, b = _SEG[name]
        return [_dot(hb[t], w1_ref[:, a:b]) for t in both]

    def rotary(val, cos, sin, head_dim):
        pieces = [val[:, c:c + 128] for c in range(0, val.shape[1], 128)]
        return jnp.concatenate([p * cos + _rot_lanes(p, head_dim) * sin for p in pieces], axis=1)

    cq, ckv, kr, u, qs, ks, vs = (seg(n) for n in ("cq", "ckv", "kr", "u", "qs", "ks", "vs"))
    cqn = [_rms(cq[t], gq_ref[...]).astype(BF16) for t in both]
    ckvn = [_rms(ckv[t], gkv_ref[...]).astype(BF16) for t in both]
    q_up = [_dot(cqn[t], wq_ref[...]) for t in both]
    k_up = [_dot(ckvn[t], wkn_ref[...]) for t in both]
    v_up = [_dot(ckvn[t], wv_ref[...]) for t in both]
    ones_m = (lax.broadcasted_iota(I32, (1, MLA_HEADS * MLA_HEAD_PAD), 1) % MLA_HEAD_PAD == MLA_V).astype(F32)
    ones_s = (lax.broadcasted_iota(I32, (1, SWA_KV_HEADS * 128), 1) % 128 == SWA_HEAD_DIM).astype(F32)
    for t in both:
        cos_m, sin_m, cos_s, sin_s = tabs[t]
        q_ref[rows[t], :] = rotary(q_up[t], cos_m, sin_m, MLA_ROPE).astype(BF16)
        k_ref[rows[t], :] = (k_up[t] + jnp.tile(rotary(kr[t], cos_m, sin_m, MLA_ROPE), (1, MLA_HEADS))).astype(BF16)
        v_ref[rows[t], :] = (v_up[t] + ones_m).astype(BF16)
        u_ref[rows[t], :] = u[t]
        qs_ref[rows[t], :] = rotary(qs[t], cos_s, sin_s, SWA_HEAD_DIM).astype(BF16)
        ks_rot = rotary(ks[t], cos_s, sin_s, SWA_HEAD_DIM)
        for hk in range(SWA_KV_HEADS):
            k_lo = ks_rot[:, hk * 128:(hk + 1) * 128]
            ks_ref[rows[t], (2 * hk) * 128:(2 * hk + 1) * 128] = k_lo.astype(BF16)
            ks_ref[rows[t], (2 * hk + 1) * 128:(2 * hk + 2) * 128] = pltpu.roll(k_lo, SWA_HEAD_DIM, 1).astype(BF16)
        vs_ref[rows[t], :] = (vs[t] + ones_s).astype(BF16)

    g0, _ = _SEG["gl"]
    for p in range(6):
        for t in both:
            gate_ref[rows[t], p * 512:(p + 1) * 512] = _sigmoid(
                _dot(hb[t], w1_ref[:, g0 + p * 512:g0 + (p + 1) * 512])).astype(BF16)


def _input_stage(stream, b, modtab, g_mix, w1, g_q, wq, g_kv, wkn, wv, tabm, tabs):
    d = SLAB_ROWS * 128
    p = _stream_tokens(stream, b)
    tm = TOKEN_TILE
    nj = p // tm
    per = INPUT_TILES
    assert (b * nj) % per == 0
    full = lambda a: pl.BlockSpec(a.shape, lambda s: (0,) * a.ndim)
    outs = [(1024, BF16), (1024, BF16), (1024, BF16), (512, F32), (512, BF16), (512, BF16), (256, BF16),
            (3072, BF16)]

    def tile_specs(t):
        row = lambda s: (s * per + t) // nj
        col = lambda s: (s * per + t) % nj
        if len(stream) == 1:
            x_specs = [pl.BlockSpec((tm * SLAB_ROWS, 128), lambda s: (s * per + t, 0))]
        else:
            x_specs = [pl.BlockSpec((1, tm, d), lambda s: (row(s), jnp.minimum(col(s), nj - 2), 0)),
                       pl.BlockSpec((1, tm, d), lambda s: (row(s), 0, 0))]
        return x_specs + [pl.BlockSpec((1, 1, 8, d), lambda s: (row(s), col(s) // (nj - 1), 0, 0)),
                          pl.BlockSpec((tm, 256), lambda s: (col(s), 0)),
                          pl.BlockSpec((tm, 256), lambda s: (col(s), 0))]

    tile_args = list(stream) + [modtab, tabm, tabs]
    res = pl.pallas_call(
        functools.partial(_in_kernel, n_stream=len(stream), ctx_tile=nj - 1, tiles_per_row=nj),
        grid=(b * nj // per,),
        in_specs=[spec for t in range(per) for spec in tile_specs(t)] + [
            full(g_mix), full(w1), full(g_q), full(wq), full(g_kv), full(wkn), full(wv)],
        out_specs=[pl.BlockSpec((per * tm, w), lambda s: (s, 0)) for w, _ in outs],
        out_shape=[jax.ShapeDtypeStruct((b * p, w), dt) for w, dt in outs],
        compiler_params=_cparams(("arbitrary",)),
        name="input_stage",
    )(*(tile_args * per), g_mix, w1, g_q, wq, g_kv, wkn, wv)
    return [r.reshape(b, p, r.shape[1]) for r in res]


def _mla_kernel(q_ref, k_ref, v_ref, *rest, tk, n_main, tail_rows):
    y_ref, m_ref, acc_ref = rest[-3:]
    hw = MLA_HEAD_PAD
    m_ref[...] = jnp.full(m_ref.shape, MASK_VALUE, F32)
    acc_ref[...] = jnp.zeros(acc_ref.shape, F32)

    def attend(r0, rows):
        for h in range(MLA_HEADS):
            q = q_ref[0, :, h * hw:(h + 1) * hw]
            k = k_ref[0, pl.ds(r0, rows), h * hw:(h + 1) * hw]
            v = v_ref[0, pl.ds(r0, rows), h * hw:(h + 1) * hw]
            s = _dot_nt(q, k)
            m_prev = m_ref[h]
            m_new = jnp.maximum(m_prev, jnp.max(s, axis=1, keepdims=True))
            m_ref[h] = m_new
            pr = jnp.exp2(s - jnp.tile(m_new, (1, rows // hw)))
            acc_ref[h] = jnp.exp2(m_prev - m_new) * acc_ref[h] + _dot(pr.astype(BF16), v)

    if n_main:
        def step(c, carry):
            attend(pl.multiple_of(c * tk, tk), tk)
            return carry

        lax.fori_loop(0, n_main, step, 0)
    if tail_rows:
        attend(n_main * tk, tail_rows)

    first = lax.broadcasted_iota(I32, (q_ref.shape[1], hw), 1) < MLA_V
    for hp in range(MLA_HEADS // 2):
        o = []
        for h in (2 * hp, 2 * hp + 1):
            acc = acc_ref[h]
            o.append(acc / acc[:, MLA_V:MLA_V + 1])
        y_ref[0, :, hp * hw:(hp + 1) * hw] = jnp.where(first, o[0], pltpu.roll(o[1], MLA_V, 1)).astype(BF16)


def _mla_attention(q, k, v, n_ctx):
    b, p, width = q.shape
    s_len = p - n_ctx
    tq = MLA_Q_TILE
    tk = MLA_KEY_CHUNK
    assert s_len % tq == 0 and s_len % tk == 0 and s_len % n_ctx == 0
    out_w = MLA_HEADS * MLA_V
    state = lambda rows: pltpu.VMEM((MLA_HEADS, rows, MLA_HEAD_PAD), F32)
    y_lat = pl.pallas_call(
        functools.partial(_mla_kernel, tk=tk, n_main=p // tk, tail_rows=p % tk),
        grid=(b, s_len // tq),
        in_specs=[pl.BlockSpec((1, tq, width), lambda bi, j: (bi, j, 0)),
                  pl.BlockSpec((1, p, width), lambda bi, j: (bi, 0, 0), pipeline_mode=pl.Buffered(1)),
                  pl.BlockSpec((1, p, width), lambda bi, j: (bi, 0, 0), pipeline_mode=pl.Buffered(1))],
        out_specs=pl.BlockSpec((1, tq, out_w), lambda bi, j: (bi, j, 0)),
        out_shape=jax.ShapeDtypeStruct((b, s_len, out_w), BF16),
        scratch_shapes=[state(tq), state(tq)],
        compiler_params=_cparams(("arbitrary", "arbitrary")),
        name="mla_attention",
    )(q, k, v)
    cblk = s_len // n_ctx
    ctx_rows = lambda w: pl.BlockSpec((1, n_ctx, w), lambda bi: (bi, cblk, 0))
    y_ctx = pl.pallas_call(
        functools.partial(_mla_kernel, tk=tk, n_main=0, tail_rows=n_ctx),
        grid=(b,),
        in_specs=[ctx_rows(width), ctx_rows(width), ctx_rows(width)],
        out_specs=pl.BlockSpec((1, n_ctx, out_w), lambda bi: (bi, 0, 0)),
        out_shape=jax.ShapeDtypeStruct((b, n_ctx, out_w), BF16),
        scratch_shapes=[state(n_ctx), state(n_ctx)],
        compiler_params=_cparams(("arbitrary",)),
        name="mla_attention_ctx",
    )(q, k, v)
    return y_lat, y_ctx


def _swa_kernel(sink_ref, q_ref, k_ref, v_ref, y_ref, *, n_ctx):
    p_len = k_ref.shape[1]
    s_len = p_len - n_ctx
    band = 3 * SWA_BLOCK
    n_lat_tiles = s_len // SWA_BLOCK
    first = lax.broadcasted_iota(I32, (SWA_BLOCK, 128), 1) < SWA_HEAD_DIM
    top = lax.broadcasted_iota(I32, (2 * SWA_BLOCK, 1), 0) < SWA_BLOCK
    n_blocks = q_ref.shape[1] // SWA_BLOCK
    chains = [(t, hk, par) for t in range(n_blocks) for hk in range(SWA_KV_HEADS) for par in range(2)]
    kstart, valid = {}, {}
    for t in range(n_blocks):
        j = pl.program_id(1) * n_blocks + t
        n = jnp.minimum(j, n_lat_tiles - 1)
        ws = jnp.clip((n - 1) * SWA_BLOCK, 0, s_len - band)
        kstart[t] = pl.multiple_of(ws, SWA_BLOCK)
        qpos = n * SWA_BLOCK + lax.broadcasted_iota(I32, (2 * SWA_BLOCK, band), 0) % SWA_BLOCK
        kpos = ws + lax.broadcasted_iota(I32, (2 * SWA_BLOCK, band), 1)
        valid[t] = (jnp.abs(qpos - kpos) <= SWA_WINDOW) & (j < n_lat_tiles)
    scores = {}
    for t, hk, par in chains:
        qrows = slice(t * SWA_BLOCK, (t + 1) * SWA_BLOCK)
        qpair = jnp.concatenate([q_ref[0, qrows, (2 * hk) * 128:(2 * hk + 1) * 128],
                                 q_ref[0, qrows, (2 * hk + 1) * 128:(2 * hk + 2) * 128]], axis=0)
        kcols = slice((2 * hk + par) * 128, (2 * hk + par + 1) * 128)
        s_c = _dot_nt(qpair, k_ref[0, s_len:p_len, kcols])
        s_b = jnp.where(valid[t], _dot_nt(qpair, k_ref[0, pl.ds(kstart[t], band), kcols]), MASK_VALUE)
        scores[t, hk, par] = (s_c, s_b)
    probs = {}
    for t, hk, par in chains:
        s_c, s_b = scores[t, hk, par]
        sink = jnp.where(top, sink_ref[4 * hk + par] * LOG2_E, sink_ref[4 * hk + 2 + par] * LOG2_E)
        m = jnp.maximum(jnp.maximum(jnp.max(s_c, axis=1, keepdims=True), jnp.max(s_b, axis=1, keepdims=True)), sink)
        probs[t, hk, par] = (jnp.exp2(s_c - m).astype(BF16), jnp.exp2(s_b - m).astype(BF16), jnp.exp2(sink - m))
    res = {}
    for t, hk, par in chains:
        p_c, p_b, p_sink = probs[t, hk, par]
        vcols = slice(hk * 128, (hk + 1) * 128)
        o = _dot(p_c, v_ref[0, s_len:p_len, vcols]) + _dot(p_b, v_ref[0, pl.ds(kstart[t], band), vcols])
        o = o / (o[:, SWA_HEAD_DIM:SWA_HEAD_DIM + 1] + p_sink)
        res[t, 4 * hk + par] = o[0:SWA_BLOCK]
        res[t, 4 * hk + 2 + par] = o[SWA_BLOCK:2 * SWA_BLOCK]
    for t in range(n_blocks):
        for pair in range(SWA_Q_HEADS // 2):
            y_ref[0, t * SWA_BLOCK:(t + 1) * SWA_BLOCK, pair * 128:(pair + 1) * 128] = jnp.where(
                first, res[t, 2 * pair], pltpu.roll(res[t, 2 * pair + 1], SWA_HEAD_DIM, 1)).astype(BF16)


def _swa_attention(sink, qs, ks, vs, n_ctx):
    b, p, _ = qs.shape
    tq = SWA_Q_TILE
    assert n_ctx % tq == 0 and (p - n_ctx) % tq == 0
    return pl.pallas_call(
        functools.partial(_swa_kernel, n_ctx=n_ctx),
        grid=(b, p // tq),
        in_specs=[pl.BlockSpec(memory_space=pltpu.SMEM),
                  pl.BlockSpec((1, tq, SWA_WIDTH), lambda bi, j: (bi, j, 0)),
                  pl.BlockSpec((1, p, ks.shape[2]), lambda bi, j: (bi, 0, 0)),
                  pl.BlockSpec((1, p, vs.shape[2]), lambda bi, j: (bi, 0, 0))],
        out_specs=pl.BlockSpec((1, tq, SWA_WIDTH), lambda bi, j: (bi, j, 0)),
        out_shape=jax.ShapeDtypeStruct((b, p, SWA_WIDTH), BF16),
        compiler_params=_cparams(("arbitrary", "arbitrary")),
        name="swa_attention",
    )(sink, qs, ks, vs)


def _pool_kernel(prev_ref, cur_ref, next_ref, w_ref, scale_ref, y_ref, ext_ref, *, n_ctx):
    j = pl.program_id(1)
    tm = cur_ref.shape[1]
    nj = pl.num_programs(1)
    s_len = (nj - 1) * tm
    is_ctx = j == nj - 1
    has_prev = (j >= 1) & (j < nj - 1)
    has_next = j < nj - 2
    ext_ref[0:POOL_HALO, :] = jnp.where(has_prev, prev_ref[0], 0.0)
    ext_ref[POOL_HALO:POOL_HALO + tm, :] = cur_ref[0]
    ext_ref[POOL_HALO + tm:POOL_HALO + tm + POOL_HALO, :] = jnp.where(has_next, next_ref[0], 0.0)
    t = lax.broadcasted_iota(I32, (tm, 1), 0)
    pos = jnp.where(is_ctx, t, j * tm + t)
    seg_len = jnp.where(is_ctx, n_ctx, s_len)
    for g, w in enumerate(POOL_WINDOWS):
        cols = slice(g * POOL_GROUP_DIM, (g + 1) * POOL_GROUP_DIM)
        acc = jnp.zeros((tm, POOL_GROUP_DIM), F32)
        for off in range(-(w // 2), w - w // 2):
            acc = acc + ext_ref[POOL_HALO + off:POOL_HALO + off + tm, cols]
        lo = jnp.maximum(pos - w // 2, 0)
        hi = jnp.minimum(pos + w - w // 2, seg_len)
        cnt = (hi - lo).astype(F32)
        pooled = acc / cnt - cur_ref[0, :, cols]
        y_ref[0, :, cols] = (_dot(pooled.astype(BF16), w_ref[g]) * scale_ref[:, cols]).astype(BF16)


def _pool_stage(u, w_pool, pool_scale, n_ctx):
    b, p, width = u.shape
    tm = TOKEN_TILE
    hb = tm // POOL_HALO
    n_halo_blocks = p // POOL_HALO
    return pl.pallas_call(
        functools.partial(_pool_kernel, n_ctx=n_ctx),
        grid=(b, p // tm),
        in_specs=[pl.BlockSpec((1, POOL_HALO, width), lambda bi, j: (bi, jnp.maximum(j * hb - 1, 0), 0)),
                  pl.BlockSpec((1, tm, width), lambda bi, j: (bi, j, 0)),
                  pl.BlockSpec((1, POOL_HALO, width),
                               lambda bi, j: (bi, jnp.minimum((j + 1) * hb, n_halo_blocks - 1), 0)),
                  pl.BlockSpec(w_pool.shape, lambda bi, j: (0, 0, 0)),
                  pl.BlockSpec(pool_scale.shape, lambda bi, j: (0, 0))],
        out_specs=pl.BlockSpec((1, tm, width), lambda bi, j: (bi, j, 0)),
        out_shape=jax.ShapeDtypeStruct((b, p, width), BF16),
        scratch_shapes=[pltpu.VMEM((tm + 2 * POOL_HALO, width), F32)],
        compiler_params=_cparams(("arbitrary", "arbitrary")),
        name="pool_stage",
    )(u, u, u, w_pool, pool_scale)


MERGE_TILES = 2


def _merge_kernel(*refs, n_stream, ctx_tile, tiles_per_row):
    n_in = n_stream + 6
    tiles = [refs[t * n_in:(t + 1) * n_in] for t in range(MERGE_TILES)]
    (g_ref, wbm_ref, wbp_ref, wbs_ref, wout_ref, wsg_ref, wsu_ref, wsd_ref, wrt_ref,
     xs_ref, fp_ref, lg_ref) = refs[MERGE_TILES * n_in:]
    d = SLAB_ROWS * 128
    tm = tiles[0][n_stream + 2].shape[1]
    both = range(MERGE_TILES)
    is_ctx = [(pl.program_id(0) * MERGE_TILES + t) % tiles_per_row == ctx_tile for t in both]
    mod = [tiles[t][n_stream + 5][0, 0] for t in both]
    gate = [tiles[t][n_stream + 4][0] for t in both]
    branches = []
    for t in both:
        yml_ref, ymc_ref, yp_ref, ys_ref = tiles[t][n_stream:n_stream + 4]
        ym = jnp.where(is_ctx[t], ymc_ref[0], yml_ref[0])
        branches.append((_dot(ym, wbm_ref[...]), _dot(yp_ref[0], wbp_ref[...]), _dot(ys_ref[0], wbs_ref[...])))
    merged = [(gate[t][:, 0:d].astype(F32) * branches[t][0] + gate[t][:, d:2 * d].astype(F32) * branches[t][1]
               + gate[t][:, 2 * d:3 * d].astype(F32) * branches[t][2]).astype(BF16) for t in both]
    proj = [_dot(merged[t], wout_ref[...]) for t in both]
    x_mid = [_stream_tile(tiles[t][:n_stream], tm, is_ctx[t]) + mod[t][2:3] * proj[t] for t in both]
    f = [_rms(x_mid[t], g_ref[...]) * (1.0 + mod[t][4:5]) + mod[t][3:4] for t in both]
    fb = [f[t].astype(BF16) for t in both]
    up = [(_dot(fb[t], wsg_ref[...]), _dot(fb[t], wsu_ref[...])) for t in both]
    for t in both:
        lg_ref[:, t * tm:(t + 1) * tm] = _dot_nt(wrt_ref[...], fb[t])
    hmid = [(up[t][0] * _sigmoid(up[t][0]) * up[t][1]).astype(BF16) for t in both]
    shared = [_dot(hmid[t], wsd_ref[...]) for t in both]
    for t in both:
        _matrix_to_slab_rows(xs_ref.at[t * tm * SLAB_ROWS:(t + 1) * tm * SLAB_ROWS, :],
                             x_mid[t] + mod[t][5:6] * shared[t], SLAB_ROWS)
        _matrix_to_slab_rows(fp_ref.at[t * tm * PACKED_ROWS:(t + 1) * tm * PACKED_ROWS, :],
                             _pack_bf16_pair(f[t]), PACKED_ROWS)


def _merge_stage(stream, ym_lat, ym_ctx, yp, ys, gates, modtab, g_ffn, wbm, wbp, wbs, wout, wsg, wsu, wsd, wrt,
                 keep_ctx):
    b, p, _ = yp.shape
    assert ym_ctx.shape[1] == TOKEN_TILE
    d = SLAB_ROWS * 128
    tm = TOKEN_TILE
    nj = p // tm
    nk = nj if keep_ctx else nj - 1
    per = MERGE_TILES
    assert (b * nk) % per == 0
    full = lambda a: pl.BlockSpec(a.shape, lambda s: (0,) * a.ndim)

    def tile_specs(t):
        row = lambda s: (s * per + t) // nk
        col = lambda s: (s * per + t) % nk
        tok = lambda w: pl.BlockSpec((1, tm, w), lambda s: (row(s), col(s), 0))
        lat = lambda w: pl.BlockSpec((1, tm, w), lambda s: (row(s), jnp.minimum(col(s), nj - 2), 0))
        ctx = lambda w: pl.BlockSpec((1, tm, w), lambda s: (row(s), 0, 0))
        if len(stream) == 1:
            x_specs = [pl.BlockSpec((tm * SLAB_ROWS, 128), lambda s: (row(s) * nj + col(s), 0))]
        else:
            x_specs = [lat(d), ctx(d)]
        return x_specs + [lat(ym_lat.shape[2]), ctx(ym_ctx.shape[2]), tok(yp.shape[2]), tok(ys.shape[2]),
                          tok(gates.shape[2]),
                          pl.BlockSpec((1, 1, 8, d), lambda s: (row(s), col(s) // (nj - 1), 0, 0))]

    tile_args = list(stream) + [ym_lat, ym_ctx, yp, ys, gates, modtab]
    return pl.pallas_call(
        functools.partial(_merge_kernel, n_stream=len(stream), ctx_tile=nj - 1, tiles_per_row=nk),
        grid=(b * nk // per,),
        in_specs=[spec for t in range(per) for spec in tile_specs(t)] + [
            full(g_ffn), full(wbm), full(wbp), full(wbs), full(wout), full(wsg), full(wsu), full(wsd), full(wrt)],
        out_specs=[pl.BlockSpec((per * tm * SLAB_ROWS, 128), lambda s: (s, 0)),
                   pl.BlockSpec((per * tm * PACKED_ROWS, 128), lambda s: (s, 0)),
                   pl.BlockSpec((N_EXPERTS, per * tm), lambda s: (0, s))],
        out_shape=[jax.ShapeDtypeStruct((b * nk * tm * SLAB_ROWS, 128), F32),
                   jax.ShapeDtypeStruct((b * nk * tm * PACKED_ROWS, 128), U32),
                   jax.ShapeDtypeStruct((N_EXPERTS, b * nk * tm), F32)],
        compiler_params=_cparams(("arbitrary",)),
        name="merge_stage",
    )(*(tile_args * per), g_ffn, wbm, wbp, wbs, wout, wsg, wsu, wsd, wrt)


def _route_kernel(lg_ref, bias_ref, dest_ref, wts_ref, seg_ref):
    tm = lg_ref.shape[1]
    ne = N_EXPERTS
    neg_inf = -jnp.inf
    scores = _sigmoid(lg_ref[...])
    sel = scores + bias_ref[...]
    iota_g = lax.broadcasted_iota(I32, (EXPERTS_PER_GROUP, tm), 0)
    gscore = []
    for g in range(N_GROUPS):
        sg = sel[g * EXPERTS_PER_GROUP:(g + 1) * EXPERTS_PER_GROUP]
        m1 = jnp.max(sg, axis=0, keepdims=True)
        i1 = jnp.min(jnp.where(sg == m1, iota_g, EXPERTS_PER_GROUP), axis=0, keepdims=True)
        m2 = jnp.max(jnp.where(iota_g == i1, neg_inf, sg), axis=0, keepdims=True)
        gscore.append(m1 + m2)
    rows = []
    for g in range(N_GROUPS):
        rank = jnp.zeros((1, tm), I32)
        for g2 in range(N_GROUPS):
            if g2 == g:
                continue
            beats = (gscore[g2] >= gscore[g]) if g2 < g else (gscore[g2] > gscore[g])
            rank = rank + beats.astype(I32)
        rows.append(jnp.where(rank < TOPK_GROUPS, sel[g * EXPERTS_PER_GROUP:(g + 1) * EXPERTS_PER_GROUP], neg_inf))
    cur = jnp.concatenate(rows, axis=0)
    iota_e = lax.broadcasted_iota(I32, (ne, tm), 0)
    picks = []
    member = jnp.zeros((ne, tm), F32)
    for _ in range(TOP_K):
        m = jnp.max(cur, axis=0, keepdims=True)
        idx = jnp.min(jnp.where(cur == m, iota_e, ne), axis=0, keepdims=True)
        hit = iota_e == idx
        picks.append(hit)
        member = member + hit.astype(F32)
        cur = jnp.where(hit, neg_inf, cur)
    earlier = (lax.broadcasted_iota(I32, (tm, tm), 0) < lax.broadcasted_iota(I32, (tm, tm), 1)).astype(BF16)
    pos = _dot(member.astype(BF16), earlier)
    cnt_col = jnp.sum(member, axis=1, keepdims=True)
    blocks_col = jnp.floor((cnt_col + (SEG_ALIGN - 1)) * (1.0 / SEG_ALIGN))
    lower = (lax.broadcasted_iota(I32, (ne, ne), 1) < lax.broadcasted_iota(I32, (ne, ne), 0)).astype(BF16)
    off_col = _dot(lower, jnp.broadcast_to(blocks_col, (ne, 128)).astype(BF16))[:, 0:1] * SEG_ALIGN
    base = off_col + pos
    w_rows = [jnp.sum(jnp.where(hit, scores, 0.0), axis=0, keepdims=True) for hit in picks]
    denom = w_rows[0]
    for w in w_rows[1:]:
        denom = denom + w
    for k, hit in enumerate(picks):
        dest_ref[0, k:k + 1, :] = jnp.sum(jnp.where(hit, base, 0.0), axis=0, keepdims=True).astype(I32)
        wts_ref[0, k:k + 1, :] = w_rows[k] / denom * ROUTED_SCALE
    member_pad = jnp.concatenate([member, jnp.zeros((128 - ne, tm), F32)], axis=0).astype(BF16)
    cnt_row = _dot_nt(jnp.ones((8, tm), BF16), member_pad)
    blocks_row = jnp.floor((cnt_row + (SEG_ALIGN - 1)) * (1.0 / SEG_ALIGN))
    before = (lax.broadcasted_iota(I32, (128, 128), 0) < lax.broadcasted_iota(I32, (128, 128), 1)).astype(BF16)
    off_row = _dot(blocks_row.astype(BF16), before) * SEG_ALIGN
    r = lax.broadcasted_iota(I32, (8, 128), 0)
    seg_ref[0] = jnp.where(r == 0, off_row, jnp.where(r == 1, cnt_row, 0.0)).astype(I32)


def _route_stage(lg_t, bias_col):
    ne, t_all = lg_t.shape
    tm = MOE_TILE
    nt = t_all // tm
    return pl.pallas_call(
        _route_kernel,
        grid=(nt,),
        in_specs=[pl.BlockSpec((ne, tm), lambda i: (0, i)),
                  pl.BlockSpec((ne, 1), lambda i: (0, 0))],
        out_specs=[pl.BlockSpec((1, TOP_K, tm), lambda i: (i, 0, 0)),
                   pl.BlockSpec((1, TOP_K, tm), lambda i: (i, 0, 0)),
                   pl.BlockSpec((1, 8, 128), lambda i: (i, 0, 0))],
        out_shape=[jax.ShapeDtypeStruct((nt, TOP_K, tm), I32),
                   jax.ShapeDtypeStruct((nt, TOP_K, tm), F32),
                   jax.ShapeDtypeStruct((nt, 8, 128), I32)],
        compiler_params=_cparams(("arbitrary",)),
        name="route_stage",
    )(lg_t, bias_col)


def _moe_row_stride(tm):
    cap = TOP_K * tm + N_EXPERTS * SEG_ALIGN + MOE_CHUNK
    blocks = cap // 8 + 1
    return 8 * (blocks + 1 - blocks % 2)


def _moe_kernel(dest_ref, wts_ref, seg_ref, fp_hbm, xs_hbm, gt_ref, gfin_ref, wg_ref, wu_ref, wd_ref, out_hbm,
                bufs_ref, fp_stage, xs_stage, out_stage, slab_tmp, fp_sem, xs_sem, out_sem,
                *, srow, sub, n_sub_per_batch, ctx_sub, n_tiles, final_norm):
    pr = pl.program_id(0)
    eg = pl.program_id(1)
    tm = dest_ref.shape[2]
    ch = MOE_CHUNK
    group = wg_ref.shape[0]
    lane_blk = 128
    n_blk = tm // lane_blk
    per = bufs_ref.shape[0]
    n_slots = fp_stage.shape[0]
    n_here = jnp.minimum(per, n_tiles - pr * per)

    def table_block(ref, half, blk):
        return ref.at[half, :, pl.ds(pl.multiple_of(blk * lane_blk, lane_blk), lane_blk)]

    def table_rows(ref, half, blk):
        cols = pl.ds(pl.multiple_of(blk * lane_blk, lane_blk), lane_blk)
        return [ref.at[half, k, cols] for k in range(TOP_K)]

    def token_rows(half, blk, per_tok):
        first = ((pr * per + half) * tm + blk * lane_blk) * per_tok
        return pl.ds(pl.multiple_of(first, lane_blk * per_tok), lane_blk * per_tok)

    def fp_copy(half, blk, slot):
        return pltpu.make_async_copy(fp_hbm.at[token_rows(half, blk, PACKED_ROWS), :], fp_stage.at[slot],
                                     fp_sem.at[slot])

    def xs_copy(half, blk, slot):
        return pltpu.make_async_copy(xs_hbm.at[token_rows(half, blk, SLAB_ROWS), :], xs_stage.at[slot],
                                     xs_sem.at[slot])

    def out_copy(half, blk, slot):
        if not final_norm:
            dst = out_hbm.at[token_rows(half, blk, SLAB_ROWS), :]
        else:
            first = (pr * per + half) * tm + blk * lane_blk
            row_len = out_hbm.shape[1]
            bi = first // row_len
            dst = out_hbm.at[bi, pl.ds(pl.multiple_of(first - bi * row_len, lane_blk), lane_blk), :]
        return pltpu.make_async_copy(out_stage.at[slot], dst, out_sem.at[slot])

    def group_rows(half):
        buf_ref = bufs_ref.at[half]
        for blk in range(n_slots - 1):
            fp_copy(half, blk, blk).start()
        buf_ref[...] = jnp.zeros(buf_ref.shape, U32)

        def body(blk, carry):
            slot = blk % n_slots
            fp_copy(half, blk, slot).wait()

            @pl.when(blk + n_slots - 1 < n_blk)
            def _prefetch():
                fp_copy(half, blk + n_slots - 1, (blk + n_slots - 1) % n_slots).start()

            dest = table_rows(dest_ref, half, blk)
            rows = fp_stage.at[slot]
            for u in range(lane_blk):
                slab = rows[u * PACKED_ROWS:(u + 1) * PACKED_ROWS, :]
                for k in range(TOP_K):
                    buf_ref[pl.ds(dest[k][u], PACKED_ROWS, stride=srow), :] = slab
            return carry

        lax.fori_loop(0, n_blk, body, 0)

    def experts():
        segments = [(ge, h) for ge in range(group) for h in range(per)]
        counts = {(ge, h): seg_ref[h, 1, eg * group + ge] for ge, h in segments}
        offsets = {(ge, h): seg_ref[h, 0, eg * group + ge] for ge, h in segments}

        def ffn_rows(segs, rows, chunk_idx):
            first = {s: offsets[s] + chunk_idx * rows for s in segs}
            words = {s: [bufs_ref[s[1], pl.ds(pl.multiple_of(q * srow + first[s], 8), rows), :] for q in range(4)]
                     for s in segs}
            xb = {}
            for s in segs:
                w = jnp.concatenate(words[s], axis=1)
                xb[s] = jnp.concatenate([_unpack_lo(w), _unpack_hi(w)], axis=1).astype(BF16)
            gates = {s: (_dot(xb[s], wg_ref[s[0]]), _dot(xb[s], wu_ref[s[0]])) for s in segs}
            hmid = {s: (gates[s][0] * _sigmoid(gates[s][0]) * gates[s][1]).astype(BF16) for s in segs}
            packed = {s: _pack_bf16_pair(_dot(hmid[s], wd_ref[s[0]])) for s in segs}
            for s in segs:
                keep = lax.broadcasted_iota(I32, (rows, 1), 0) < counts[s] - chunk_idx * rows
                for q in range(4):
                    bufs_ref[s[1], pl.ds(pl.multiple_of(q * srow + first[s], 8), rows), :] = jnp.where(
                        keep, packed[s][:, q * 128:(q + 1) * 128], words[s][q])

        most = functools.reduce(jnp.maximum, counts.values())
        lo = 0
        for rows in MOE_BODY_ROWS:
            pl.when((most > lo) & (most <= rows))(functools.partial(ffn_rows, tuple(segments), rows, 0))
            lo = rows

        @pl.when(most > ch)
        def _long_segments():
            for s in segments:
                def chunk(c, carry, s=s):
                    ffn_rows((s,), ch, c)
                    return carry

                lax.fori_loop(0, (counts[s] + ch - 1) // ch, chunk, 0)

    def combine(half):
        buf_ref = bufs_ref.at[half]
        for blk in range(n_slots - 1):
            xs_copy(half, blk, blk).start()

        def body(blk, carry):
            slot = blk % n_slots
            xs_copy(half, blk, slot).wait()

            @pl.when(blk + n_slots - 1 < n_blk)
            def _prefetch():
                xs_copy(half, blk + n_slots - 1, (blk + n_slots - 1) % n_slots).start()

            @pl.when(blk >= n_slots)
            def _slot_free():
                out_copy(half, blk - n_slots, slot).wait()

            sub_blk = ((pr * per + half) * tm + blk * lane_blk) // sub
            bi = sub_blk // n_sub_per_batch
            gate = gt_ref[bi * 2 + (sub_blk - bi * n_sub_per_batch) // ctx_sub]
            dest = table_block(dest_ref, half, blk)
            wts = table_block(wts_ref, half, blk)
            xs = xs_stage.at[slot]
            out = slab_tmp if final_norm else out_stage.at[slot]
            for u in range(lane_blk):
                acc_lo = jnp.zeros((4, 128), F32)
                acc_hi = jnp.zeros((4, 128), F32)
                for k in range(TOP_K):
                    words = buf_ref[pl.ds(dest[k, u], PACKED_ROWS, stride=srow), :]
                    wk = wts[k, u]
                    acc_lo = acc_lo + wk * _unpack_lo(words)
                    acc_hi = acc_hi + wk * _unpack_hi(words)
                out[u * 8:u * 8 + 4, :] = xs[u * 8:u * 8 + 4, :] + gate[0:4] * acc_lo
                out[u * 8 + 4:u * 8 + 8, :] = xs[u * 8 + 4:u * 8 + 8, :] + gate[4:8] * acc_hi
            if final_norm:
                out_stage[slot] = _rms(_slab_rows_to_matrix(slab_tmp, lane_blk, SLAB_ROWS), gfin_ref[...])
            out_copy(half, blk, slot).start()
            return carry

        lax.fori_loop(0, n_blk, body, 0)
        for blk in range(n_blk - n_slots, n_blk):
            out_copy(half, blk, blk % n_slots).wait()

    def each_tile(fn):
        def body(half, carry):
            fn(half)
            return carry

        lax.fori_loop(0, n_here, body, 0)

    @pl.when(eg == 0)
    def _first_step():
        each_tile(group_rows)
        for half in range(1, per):
            @pl.when(half >= n_here)
            def _clear(half=half):
                bufs_ref[half] = jnp.zeros(bufs_ref.shape[1:], U32)

    experts()
    pl.when(eg == pl.num_programs(1) - 1)(functools.partial(each_tile, combine))


def _moe_stage(dest, wts, seg, fp4, xs8, gt2, g_final, wg, wu, wd, layer, n_sub_per_batch, keep_ctx, final_norm):
    ctx_sub = n_sub_per_batch - 1 if keep_ctx else n_sub_per_batch
    d = SLAB_ROWS * 128
    tokens_per_row = n_sub_per_batch * TOKEN_TILE
    out_shape = (xs8.shape[0] // (SLAB_ROWS * tokens_per_row), tokens_per_row, d) if final_norm else xs8.shape
    out_block = (128, d) if final_norm else (128 * SLAB_ROWS, 128)
    nt, _, tm = dest.shape
    ne = wg.shape[1]
    group = MOE_EXPERTS_PER_STEP
    per = MOE_TILES_PER_STEP
    assert ne % group == 0 and tm % 256 == 0 and TOKEN_TILE % 128 == 0 and per == 2
    n_rows = -(-nt // per)
    pad = ((0, n_rows * per - nt), (0, 0), (0, 0))
    dest, wts, seg = jnp.pad(dest, pad), jnp.pad(wts, pad), jnp.pad(seg, pad)
    srow = _moe_row_stride(tm)
    smem = lambda shape: pl.BlockSpec(shape, lambda i, e: (i, 0, 0), memory_space=pltpu.SMEM)
    hbm = pl.BlockSpec(memory_space=pl.ANY)
    slots = MOE_STAGE_SLOTS
    assert tm // 128 >= slots
    stage = lambda per_tok, dt: pltpu.VMEM((slots, 128 * per_tok, 128), dt)
    return pl.pallas_call(
        functools.partial(_moe_kernel, srow=srow, sub=TOKEN_TILE, n_sub_per_batch=n_sub_per_batch, ctx_sub=ctx_sub,
                          n_tiles=nt, final_norm=final_norm),
        grid=(n_rows, ne // group),
        in_specs=[smem((per, TOP_K, tm)), smem((per, TOP_K, tm)), smem((per, 8, 128)),
                  hbm, hbm,
                  pl.BlockSpec(gt2.shape, lambda i, e: (0, 0, 0)),
                  pl.BlockSpec(g_final.shape, lambda i, e: (0, 0)),
                  pl.BlockSpec((None, group) + wg.shape[2:], lambda i, e: (layer, e, 0, 0)),
                  pl.BlockSpec((None, group) + wu.shape[2:], lambda i, e: (layer, e, 0, 0)),
                  pl.BlockSpec((None, group) + wd.shape[2:], lambda i, e: (layer, e, 0, 0))],
        out_specs=hbm,
        out_shape=jax.ShapeDtypeStruct(out_shape, F32),
        scratch_shapes=[pltpu.VMEM((per, 4 * srow, 128), U32),
                        stage(PACKED_ROWS, U32), stage(SLAB_ROWS, F32), pltpu.VMEM((slots,) + out_block, F32),
                        pltpu.VMEM((128 * SLAB_ROWS, 128), F32),
                        pltpu.SemaphoreType.DMA((slots,)), pltpu.SemaphoreType.DMA((slots,)),
                        pltpu.SemaphoreType.DMA((slots,))],
        compiler_params=_cparams(("arbitrary", "arbitrary")),
        name="moe_stage",
    )(dest, wts, seg, fp4, xs8, gt2, g_final, wg, wu, wd)


def _rot_lanes(x, head_dim):
    quarter = head_dim // 4
    lane = lax.broadcasted_iota(I32, x.shape, 1)
    width = x.shape[1]
    return jnp.where(lane % (2 * quarter) < quarter, -pltpu.roll(x, width - quarter, 1), pltpu.roll(x, quarter, 1))


def _w1_kernel(w_ref, o_ref):
    offs = [0]
    for w in (MLA_Q_LORA, MLA_KV_LORA, MLA_ROPE, POOL_WIDTH, SWA_WIDTH, SWA_KV_WIDTH, SWA_KV_WIDTH):
        offs.append(offs[-1] + w)
    o_cq, o_ckv, o_kr, o_u, o_qs, o_ks, o_vs, o_gl = offs
    lane = lax.broadcasted_iota(I32, (w_ref.shape[0], 128), 1)

    def put(name, val):
        a, b = _SEG[name]
        o_ref[:, a:b] = val.astype(BF16)

    def kv_slots(x):
        low = lane < SWA_HEAD_DIM
        return jnp.concatenate([jnp.where(low, x, 0.0), jnp.where(low, pltpu.roll(x, SWA_HEAD_DIM, 1), 0.0)], axis=1)

    def kr_slot(x):
        return jnp.where((lane >= MLA_NOPE) & (lane < MLA_NOPE + MLA_ROPE), pltpu.roll(x, MLA_NOPE, 1), 0.0)

    put("cq", w_ref[:, o_cq:o_ckv])
    put("ckv", w_ref[:, o_ckv:o_kr])
    put("u", w_ref[:, o_u:o_qs])
    put("qs", w_ref[:, o_qs:o_ks] * (SWA_SCALE * LOG2_E))
    put("ks", kv_slots(w_ref[:, o_ks:o_vs]))
    put("vs", kv_slots(w_ref[:, o_vs:o_gl]))
    put("kr", kr_slot(w_ref[:, o_kr:o_kr + 128]))
    put("gl", w_ref[:, o_gl:w_ref.shape[1]])


def _fused_in_weight(w_in, layer):
    _, d, width = w_in.shape
    tr = 256
    assert d % tr == 0 and SWA_KV_WIDTH == 128 and MLA_KV_LORA + MLA_Q_LORA == 5 * 128
    return pl.pallas_call(
        _w1_kernel,
        grid=(d // tr,),
        in_specs=[pl.BlockSpec((None, tr, width), lambda i: (layer, i, 0))],
        out_specs=pl.BlockSpec((tr, FUSED_IN_WIDTH), lambda i: (i, 0)),
        out_shape=jax.ShapeDtypeStruct((d, FUSED_IN_WIDTH), BF16),
        compiler_params=_cparams(("arbitrary",)),
        name="fused_in_weight",
    )(w_in)


def _rope_pattern(s_len, n_ctx, rot_dim):
    t = jnp.arange(s_len)
    row = (t // GRID_W).astype(F32)
    col = (t % GRID_W).astype(F32)
    n_freq = rot_dim // 4
    inv_freq = ROPE_BASE ** (-jnp.arange(n_freq, dtype=F32) / n_freq)
    ang_r = row[:, None] * inv_freq[None, :]
    ang_c = col[:, None] * inv_freq[None, :]
    cos = jnp.concatenate([jnp.cos(ang_r), jnp.cos(ang_r), jnp.cos(ang_c), jnp.cos(ang_c)], axis=1)
    sin = jnp.concatenate([jnp.sin(ang_r), jnp.sin(ang_r), jnp.sin(ang_c), jnp.sin(ang_c)], axis=1)
    cos = jnp.concatenate([cos, jnp.ones((n_ctx, rot_dim), F32)], axis=0)
    sin = jnp.concatenate([sin, jnp.zeros((n_ctx, rot_dim), F32)], axis=0)
    return cos, sin


def _layer_weights(w_in, layer, w_uq, w_ukv):
    w1 = _fused_in_weight(w_in, layer)
    lq = w_uq.shape[0]
    wq3 = (w_uq * (MLA_SCALE * LOG2_E)).reshape(lq, MLA_HEADS, MLA_NOPE + MLA_ROPE)
    pad = jnp.zeros((lq, MLA_HEADS, MLA_HEAD_PAD - MLA_NOPE - MLA_ROPE), F32)
    wq = jnp.concatenate([wq3, pad], axis=2).reshape(lq, MLA_HEADS * MLA_HEAD_PAD).astype(BF16)
    lkv = w_ukv.shape[0]
    wkv3 = w_ukv.reshape(lkv, MLA_HEADS, MLA_NOPE + MLA_V)
    wkn = jnp.concatenate([wkv3[:, :, :MLA_NOPE], jnp.zeros((lkv, MLA_HEADS, MLA_HEAD_PAD - MLA_NOPE), F32)],
                          axis=2).reshape(lkv, MLA_HEADS * MLA_HEAD_PAD).astype(BF16)
    wv = jnp.concatenate([wkv3[:, :, MLA_NOPE:], jnp.zeros((lkv, MLA_HEADS, MLA_HEAD_PAD - MLA_V), F32)],
                         axis=2).reshape(lkv, MLA_HEADS * MLA_HEAD_PAD).astype(BF16)
    return w1, wq, wkn, wv


def kernel(x, c, ctx, c_ctx, w_mod, b_mod, g_mix, g_ffn, w_in, g_mla_q, g_mla_kv, w_mla_uq, w_mla_ukv, w_pool,
           pool_scale, swa_sink, w_br_mla, w_br_pool, w_br_swa, w_out, w_router, router_bias, w_exp_gate,
           w_exp_up, w_exp_down, w_sh_gate, w_sh_up, w_sh_down, g_final):
    b, s_len, d = x.shape
    n_ctx = ctx.shape[1]
    n_layers = w_mod.shape[0]
    p = n_ctx + s_len
    assert n_ctx == TOKEN_TILE and s_len % TOKEN_TILE == 0 and (b * p) % MOE_TILE == 0 and b + 1 <= 8
    assert d == 1024 and w_in.shape[2] == (MLA_Q_LORA + MLA_KV_LORA + MLA_ROPE + POOL_WIDTH + SWA_WIDTH
                                           + 2 * SWA_KV_WIDTH + 3 * d)

    cm, sm = _rope_pattern(s_len, n_ctx, MLA_ROPE)
    tail = jnp.zeros((p, MLA_HEAD_PAD - MLA_NOPE - MLA_ROPE), F32)
    tabm = jnp.concatenate([jnp.ones((p, MLA_NOPE), F32), cm, tail, jnp.zeros((p, MLA_NOPE), F32), sm, tail], axis=1)
    cs, ss = _rope_pattern(s_len, n_ctx, SWA_HEAD_DIM)
    tabs = jnp.concatenate([cs, cs, ss, ss], axis=1)

    cvec = jnp.concatenate([c, c_ctx[None, :], jnp.zeros((8 - b - 1, d), F32)], axis=0)
    mod_all = _modulation(cvec, w_mod, b_mod).reshape(n_layers, 8, N_MOD, d)

    stream = (x, ctx)
    wg_all, wu_all, wd_all = w_exp_gate.astype(BF16), w_exp_up.astype(BF16), w_exp_down.astype(BF16)
    for i in range(n_layers):
        keep_ctx = i < n_layers - 1
        lat = mod_all[i, :b]
        ctx_rows = jnp.broadcast_to(mod_all[i, b][None], (b, N_MOD, d))
        modtab = jnp.pad(jnp.stack([lat, ctx_rows], axis=1), ((0, 0), (0, 0), (0, 8 - N_MOD), (0, 0)))
        w1, wq, wkn, wv = _layer_weights(w_in, i, w_mla_uq[i], w_mla_ukv[i])
        q, k, v, u, qs, ks, vs, gates = _input_stage(
            stream, b, modtab, g_mix[i][None], w1, g_mla_q[i][None], wq, g_mla_kv[i][None], wkn, wv, tabm, tabs)
        y_mla, y_mla_ctx = _mla_attention(q, k, v, n_ctx)
        y_swa = _swa_attention(swa_sink[i], qs, ks, vs, n_ctx)
        y_pool = _pool_stage(u, w_pool[i].astype(BF16), pool_scale[i][None], n_ctx)
        xs, fp, lg_t = _merge_stage(
            stream, y_mla, y_mla_ctx, y_pool, y_swa, gates, modtab, g_ffn[i][None], w_br_mla[i].astype(BF16),
            w_br_pool[i].astype(BF16), w_br_swa[i].astype(BF16), w_out[i].astype(BF16),
            w_sh_gate[i].astype(BF16), w_sh_up[i].astype(BF16), w_sh_down[i].astype(BF16),
            w_router[i].T.astype(BF16), keep_ctx)
        dest, wts, seg = _route_stage(lg_t, router_bias[i][:, None])
        gt2 = modtab[:, :, 5, :].reshape(b * 2, 8, d // 8)
        stream = (_moe_stage(dest, wts, seg, fp, xs, gt2, g_final[None], wg_all, wu_all, wd_all, i,
                             (p if keep_ctx else s_len) // TOKEN_TILE, keep_ctx, final_norm=not keep_ctx),)
    return stream[0]
```

```python
import functools

import jax
import jax.numpy as jnp
from jax import lax
from jax.experimental import pallas as pl
from jax.experimental.pallas import tpu as pltpu

F32 = jnp.float32
BF16 = jnp.bfloat16
U32 = jnp.uint32
I32 = jnp.int32

NORM_EPS = 1e-6
ROPE_BASE = 10000.0
GRID_W = 64
N_MOD = 6

MLA_HEADS = 8
MLA_Q_LORA = 384
MLA_KV_LORA = 256
MLA_NOPE = 64
MLA_ROPE = 32
MLA_V = 64
MLA_SCALE = (MLA_NOPE + MLA_ROPE) ** -0.5
MLA_HEAD_PAD = 128
LOG2_E = 1.4426950408889634

POOL_WINDOWS = (2, 4, 8, 16)
POOL_GROUP_DIM = 128
POOL_WIDTH = 512
POOL_HALO = 8

SWA_Q_HEADS = 8
SWA_KV_HEADS = 2
SWA_HEAD_DIM = 64
SWA_WINDOW = 128
SWA_BLOCK = 128
SWA_SCALE = SWA_HEAD_DIM ** -0.5
SWA_WIDTH = SWA_Q_HEADS * SWA_HEAD_DIM
SWA_KV_WIDTH = SWA_KV_HEADS * SWA_HEAD_DIM

N_EXPERTS = 64
TOP_K = 8
N_GROUPS = 8
TOPK_GROUPS = 4
EXPERTS_PER_GROUP = 8
D_EXPERT = 256
ROUTED_SCALE = 2.5

TOKEN_TILE = 256
SWA_Q_TILE = 256
MLA_Q_TILE = 1024
MLA_KEY_CHUNK = 256
MOE_TILE = 1024
MOE_CHUNK = 256
MOE_BODY_ROWS = (128, 144, 160, 176, 192, 224, 256)
MOE_EXPERTS_PER_STEP = 2
MOE_TILES_PER_STEP = 2
MOE_STAGE_SLOTS = 4
SEG_ALIGN = 8
SLAB_ROWS = 8
PACKED_ROWS = 4
MASK_VALUE = -1e30
HI16 = 0xFFFF0000

VMEM_LIMIT = 56 * 1024 * 1024

_SEG_WIDTHS = (("cq", 384), ("ckv", 256), ("u", 512), ("qs", 512), ("ks", 256), ("vs", 256), ("kr", 128),
               ("gl", 3072))
_SEG = {}
_o = 0
for _n, _w in _SEG_WIDTHS:
    _SEG[_n] = (_o, _o + _w)
    _o += _w
FUSED_IN_WIDTH = _o


def _cparams(sem):
    return pltpu.CompilerParams(dimension_semantics=sem, vmem_limit_bytes=VMEM_LIMIT)


def _dot(a, b):
    return jnp.dot(a, b, preferred_element_type=F32)


def _dot_nt(a, b):
    return lax.dot_general(a, b, (((1,), (1,)), ((), ())), preferred_element_type=F32)


def _sigmoid(x):
    return 1.0 / (1.0 + jnp.exp(-x))


def _rms(x, g):
    return x * lax.rsqrt(jnp.mean(x * x, axis=-1, keepdims=True) + NORM_EPS) * g


def _pack_bf16_pair(v):
    n = v.shape[1] // 2
    bits = pltpu.bitcast(v.astype(BF16).astype(F32), U32)
    return (bits[:, :n] >> 16) | (bits[:, n:] & jnp.uint32(HI16))


def _slab_rows_to_matrix(ref, n_tok, per_tok):
    return jnp.concatenate([ref[pl.ds(c, n_tok, stride=per_tok), :] for c in range(per_tok)], axis=1)


def _matrix_to_slab_rows(ref, val, per_tok):
    n_tok = val.shape[0]
    for c in range(per_tok):
        ref[pl.ds(c, n_tok, stride=per_tok), :] = val[:, c * 128:(c + 1) * 128]


def _unpack_lo(w):
    return pltpu.bitcast(w << 16, F32)


def _unpack_hi(w):
    return pltpu.bitcast(w & jnp.uint32(HI16), F32)


def _mod_kernel(c_ref, w_ref, b_ref, o_ref):
    c = c_ref[...]
    a = (c * _sigmoid(c)).astype(BF16)
    o_ref[0] = _dot(a, w_ref[0].astype(BF16)) + b_ref[0]


def _modulation(cvec, w_mod, b_mod):
    n_layers, d, width = w_mod.shape
    tn = width // 4
    return pl.pallas_call(
        _mod_kernel,
        grid=(n_layers, width // tn),
        in_specs=[pl.BlockSpec((8, d), lambda l, n: (0, 0)),
                  pl.BlockSpec((1, d, tn), lambda l, n: (l, 0, n)),
                  pl.BlockSpec((1, 1, tn), lambda l, n: (l, 0, n))],
        out_specs=pl.BlockSpec((1, 8, tn), lambda l, n: (l, 0, n)),
        out_shape=jax.ShapeDtypeStruct((n_layers, 8, width), F32),
        compiler_params=_cparams(("arbitrary", "arbitrary")),
        name="modulation",
    )(cvec, w_mod, b_mod.reshape(n_layers, 1, width))


def _stream_tile(refs, n_tok, is_ctx):
    if len(refs) == 1:
        return _slab_rows_to_matrix(refs[0], n_tok, SLAB_ROWS)
    return jnp.where(is_ctx, refs[1][0], refs[0][0])


def _stream_tokens(stream, b):
    return stream[0].shape[0] // (b * SLAB_ROWS) if len(stream) == 1 else stream[0].shape[1] + stream[1].shape[1]


INPUT_TILES = 2


def _in_kernel(*refs, n_stream, ctx_tile, tiles_per_row):
    n_in = n_stream + 3
    tiles = [refs[t * n_in:(t + 1) * n_in] for t in range(INPUT_TILES)]
    (g_ref, w1_ref, gq_ref, wq_ref, gkv_ref, wkn_ref, wv_ref,
     q_ref, k_ref, v_ref, u_ref, qs_ref, ks_ref, vs_ref, gate_ref) = refs[INPUT_TILES * n_in:]
    tm = tiles[0][n_stream + 1].shape[0]
    both = range(INPUT_TILES)
    rows = [slice(t * tm, (t + 1) * tm) for t in both]
    hb, tabs = [], []
    for t in both:
        is_ctx = (pl.program_id(0) * INPUT_TILES + t) % tiles_per_row == ctx_tile
        mod = tiles[t][n_stream][0, 0]
        x = _stream_tile(tiles[t][:n_stream], tm, is_ctx)
        hb.append((_rms(x, g_ref[...]) * (1.0 + mod[1:2]) + mod[0:1]).astype(BF16))
        tabm_ref, tabs_ref = tiles[t][n_stream + 1:n_stream + 3]
        tabs.append((tabm_ref[:, 0:128], tabm_ref[:, 128:256], tabs_ref[:, 0:128], tabs_ref[:, 128:256]))

    def seg(name):
        a, b = _SEG[name]
        return [_dot(hb[t], w1_ref[:, a:b]) for t in both]

    def rotary(val, cos, sin, head_dim):
        pieces = [val[:, c:c + 128] for c in range(0, val.shape[1], 128)]
        return jnp.concatenate([p * cos + _rot_lanes(p, head_dim) * sin for p in pieces], axis=1)

    cq, ckv, kr, u, qs, ks, vs = (seg(n) for n in ("cq", "ckv", "kr", "u", "qs", "ks", "vs"))
    cqn = [_rms(cq[t], gq_ref[...]).astype(BF16) for t in both]
    ckvn = [_rms(ckv[t], gkv_ref[...]).astype(BF16) for t in both]
    q_up = [_dot(cqn[t], wq_ref[...]) for t in both]
    k_up = [_dot(ckvn[t], wkn_ref[...]) for t in both]
    v_up = [_dot(ckvn[t], wv_ref[...]) for t in both]
    ones_m = (lax.broadcasted_iota(I32, (1, MLA_HEADS * MLA_HEAD_PAD), 1) % MLA_HEAD_PAD == MLA_V).astype(F32)
    ones_s = (lax.broadcasted_iota(I32, (1, SWA_KV_HEADS * 128), 1) % 128 == SWA_HEAD_DIM).astype(F32)
    for t in both:
        cos_m, sin_m, cos_s, sin_s = tabs[t]
        q_ref[rows[t], :] = rotary(q_up[t], cos_m, sin_m, MLA_ROPE).astype(BF16)
        k_ref[rows[t], :] = (k_up[t] + jnp.tile(rotary(kr[t], cos_m, sin_m, MLA_ROPE), (1, MLA_HEADS))).astype(BF16)
        v_ref[rows[t], :] = (v_up[t] + ones_m).astype(BF16)
        u_ref[rows[t], :] = u[t]
        qs_ref[rows[t], :] = rotary(qs[t], cos_s, sin_s, SWA_HEAD_DIM).astype(BF16)
        ks_rot = rotary(ks[t], cos_s, sin_s, SWA_HEAD_DIM)
        for hk in range(SWA_KV_HEADS):
            k_lo = ks_rot[:, hk * 128:(hk + 1) * 128]
            ks_ref[rows[t], (2 * hk) * 128:(2 * hk + 1) * 128] = k_lo.astype(BF16)
            ks_ref[rows[t], (2 * hk + 1) * 128:(2 * hk + 2) * 128] = pltpu.roll(k_lo, SWA_HEAD_DIM, 1).astype(BF16)
        vs_ref[rows[t], :] = (vs[t] + ones_s).astype(BF16)

    g0, _ = _SEG["gl"]
    for p in range(6):
        for t in both:
            gate_ref[rows[t], p * 512:(p + 1) * 512] = _sigmoid(
                _dot(hb[t], w1_ref[:, g0 + p * 512:g0 + (p + 1) * 512])).astype(BF16)


def _input_stage(stream, b, modtab, g_mix, w1, g_q, wq, g_kv, wkn, wv, tabm, tabs):
    d = SLAB_ROWS * 128
    p = _stream_tokens(stream, b)
    tm = TOKEN_TILE
    nj = p // tm
    per = INPUT_TILES
    assert (b * nj) % per == 0
    full = lambda a: pl.BlockSpec(a.shape, lambda s: (0,) * a.ndim)
    outs = [(1024, BF16), (1024, BF16), (1024, BF16), (512, F32), (512, BF16), (512, BF16), (256, BF16),
            (3072, BF16)]

    def tile_specs(t):
        row = lambda s: (s * per + t) // nj
        col = lambda s: (s * per + t) % nj
        if len(stream) == 1:
            x_specs = [pl.BlockSpec((tm * SLAB_ROWS, 128), lambda s: (s * per + t, 0))]
        else:
            x_specs = [pl.BlockSpec((1, tm, d), lambda s: (row(s), jnp.minimum(col(s), nj - 2), 0)),
                       pl.BlockSpec((1, tm, d), lambda s: (row(s), 0, 0))]
        return x_specs + [pl.BlockSpec((1, 1, 8, d), lambda s: (row(s), col(s) // (nj - 1), 0, 0)),
                          pl.BlockSpec((tm, 256), lambda s: (col(s), 0)),
                          pl.BlockSpec((tm, 256), lambda s: (col(s), 0))]

    tile_args = list(stream) + [modtab, tabm, tabs]
    res = pl.pallas_call(
        functools.partial(_in_kernel, n_stream=len(stream), ctx_tile=nj - 1, tiles_per_row=nj),
        grid=(b * nj // per,),
        in_specs=[spec for t in range(per) for spec in tile_specs(t)] + [
            full(g_mix), full(w1), full(g_q), full(wq), full(g_kv), full(wkn), full(wv)],
        out_specs=[pl.BlockSpec((per * tm, w), lambda s: (s, 0)) for w, _ in outs],
        out_shape=[jax.ShapeDtypeStruct((b * p, w), dt) for w, dt in outs],
        compiler_params=_cparams(("arbitrary",)),
        name="input_stage",
    )(*(tile_args * per), g_mix, w1, g_q, wq, g_kv, wkn, wv)
    return [r.reshape(b, p, r.shape[1]) for r in res]


def _mla_kernel(q_ref, k_ref, v_ref, *rest, tk, n_main, tail_rows):
    y_ref, m_ref, acc_ref = rest[-3:]
    hw = MLA_HEAD_PAD
    m_ref[...] = jnp.full(m_ref.shape, MASK_VALUE, F32)
    acc_ref[...] = jnp.zeros(acc_ref.shape, F32)

    def attend(r0, rows):
        for h in range(MLA_HEADS):
            q = q_ref[0, :, h * hw:(h + 1) * hw]
            k = k_ref[0, pl.ds(r0, rows), h * hw:(h + 1) * hw]
            v = v_ref[0, pl.ds(r0, rows), h * hw:(h + 1) * hw]
            s = _dot_nt(q, k)
            m_prev = m_ref[h]
            m_new = jnp.maximum(m_prev, jnp.max(s, axis=1, keepdims=True))
            m_ref[h] = m_new
            pr = jnp.exp2(s - jnp.tile(m_new, (1, rows // hw)))
            acc_ref[h] = jnp.exp2(m_prev - m_new) * acc_ref[h] + _dot(pr.astype(BF16), v)

    if n_main:
        def step(c, carry):
            attend(pl.multiple_of(c * tk, tk), tk)
            return carry

        lax.fori_loop(0, n_main, step, 0)
    if tail_rows:
        attend(n_main * tk, tail_rows)

    first = lax.broadcasted_iota(I32, (q_ref.shape[1], hw), 1) < MLA_V
    for hp in range(MLA_HEADS // 2):
        o = []
        for h in (2 * hp, 2 * hp + 1):
            acc = acc_ref[h]
            o.append(acc / acc[:, MLA_V:MLA_V + 1])
        y_ref[0, :, hp * hw:(hp + 1) * hw] = jnp.where(first, o[0], pltpu.roll(o[1], MLA_V, 1)).astype(BF16)


def _mla_attention(q, k, v, n_ctx):
    b, p, width = q.shape
    s_len = p - n_ctx
    tq = MLA_Q_TILE
    tk = MLA_KEY_CHUNK
    assert s_len % tq == 0 and s_len % tk == 0 and s_len % n_ctx == 0
    out_w = MLA_HEADS * MLA_V
    state = lambda rows: pltpu.VMEM((MLA_HEADS, rows, MLA_HEAD_PAD), F32)
    y_lat = pl.pallas_call(
        functools.partial(_mla_kernel, tk=tk, n_main=p // tk, tail_rows=p % tk),
        grid=(b, s_len // tq),
        in_specs=[pl.BlockSpec((1, tq, width), lambda bi, j: (bi, j, 0)),
                  pl.BlockSpec((1, p, width), lambda bi, j: (bi, 0, 0), pipeline_mode=pl.Buffered(1)),
                  pl.BlockSpec((1, p, width), lambda bi, j: (bi, 0, 0), pipeline_mode=pl.Buffered(1))],
        out_specs=pl.BlockSpec((1, tq, out_w), lambda bi, j: (bi, j, 0)),
        out_shape=jax.ShapeDtypeStruct((b, s_len, out_w), BF16),
        scratch_shapes=[state(tq), state(tq)],
        compiler_params=_cparams(("arbitrary", "arbitrary")),
        name="mla_attention",
    )(q, k, v)
    cblk = s_len // n_ctx
    ctx_rows = lambda w: pl.BlockSpec((1, n_ctx, w), lambda bi: (bi, cblk, 0))
    y_ctx = pl.pallas_call(
        functools.partial(_mla_kernel, tk=tk, n_main=0, tail_rows=n_ctx),
        grid=(b,),
        in_specs=[ctx_rows(width), ctx_rows(width), ctx_rows(width)],
        out_specs=pl.BlockSpec((1, n_ctx, out_w), lambda bi: (bi, 0, 0)),
        out_shape=jax.ShapeDtypeStruct((b, n_ctx, out_w), BF16),
        scratch_shapes=[state(n_ctx), state(n_ctx)],
        compiler_params=_cparams(("arbitrary",)),
        name="mla_attention_ctx",
    )(q, k, v)
    return y_lat, y_ctx


def _swa_kernel(sink_ref, q_ref, k_ref, v_ref, y_ref, *, n_ctx):
    p_len = k_ref.shape[1]
    s_len = p_len - n_ctx
    band = 3 * SWA_BLOCK
    n_lat_tiles = s_len // SWA_BLOCK
    first = lax.broadcasted_iota(I32, (SWA_BLOCK, 128), 1) < SWA_HEAD_DIM
    top = lax.broadcasted_iota(I32, (2 * SWA_BLOCK, 1), 0) < SWA_BLOCK
    n_blocks = q_ref.shape[1] // SWA_BLOCK
    chains = [(t, hk, par) for t in range(n_blocks) for hk in range(SWA_KV_HEADS) for par in range(2)]
    kstart, valid = {}, {}
    for t in range(n_blocks):
        j = pl.program_id(1) * n_blocks + t
        n = jnp.minimum(j, n_lat_tiles - 1)
        ws = jnp.clip((n - 1) * SWA_BLOCK, 0, s_len - band)
        kstart[t] = pl.multiple_of(ws, SWA_BLOCK)
        qpos = n * SWA_BLOCK + lax.broadcasted_iota(I32, (2 * SWA_BLOCK, band), 0) % SWA_BLOCK
        kpos = ws + lax.broadcasted_iota(I32, (2 * SWA_BLOCK, band), 1)
        valid[t] = (jnp.abs(qpos - kpos) <= SWA_WINDOW) & (j < n_lat_tiles)
    scores = {}
    for t, hk, par in chains:
        qrows = slice(t * SWA_BLOCK, (t + 1) * SWA_BLOCK)
        qpair = jnp.concatenate([q_ref[0, qrows, (2 * hk) * 128:(2 * hk + 1) * 128],
                                 q_ref[0, qrows, (2 * hk + 1) * 128:(2 * hk + 2) * 128]], axis=0)
        kcols = slice((2 * hk + par) * 128, (2 * hk + par + 1) * 128)
        s_c = _dot_nt(qpair, k_ref[0, s_len:p_len, kcols])
        s_b = jnp.where(valid[t], _dot_nt(qpair, k_ref[0, pl.ds(kstart[t], band), kcols]), MASK_VALUE)
        scores[t, hk, par] = (s_c, s_b)
    probs = {}
    for t, hk, par in chains:
        s_c, s_b = scores[t, hk, par]
        sink = jnp.where(top, sink_ref[4 * hk + par] * LOG2_E, sink_ref[4 * hk + 2 + par] * LOG2_E)
        m = jnp.maximum(jnp.maximum(jnp.max(s_c, axis=1, keepdims=True), jnp.max(s_b, axis=1, keepdims=True)), sink)
        probs[t, hk, par] = (jnp.exp2(s_c - m).astype(BF16), jnp.exp2(s_b - m).astype(BF16), jnp.exp2(sink - m))
    res = {}
    for t, hk, par in chains:
        p_c, p_b, p_sink = probs[t, hk, par]
        vcols = slice(hk * 128, (hk + 1) * 128)
        o = _dot(p_c, v_ref[0, s_len:p_len, vcols]) + _dot(p_b, v_ref[0, pl.ds(kstart[t], band), vcols])
        o = o / (o[:, SWA_HEAD_DIM:SWA_HEAD_DIM + 1] + p_sink)
        res[t, 4 * hk + par] = o[0:SWA_BLOCK]
        res[t, 4 * hk + 2 + par] = o[SWA_BLOCK:2 * SWA_BLOCK]
    for t in range(n_blocks):
        for pair in range(SWA_Q_HEADS // 2):
            y_ref[0, t * SWA_BLOCK:(t + 1) * SWA_BLOCK, pair * 128:(pair + 1) * 128] = jnp.where(
                first, res[t, 2 * pair], pltpu.roll(res[t, 2 * pair + 1], SWA_HEAD_DIM, 1)).astype(BF16)


def _swa_attention(sink, qs, ks, vs, n_ctx):
    b, p, _ = qs.shape
    tq = SWA_Q_TILE
    assert n_ctx % tq == 0 and (p - n_ctx) % tq == 0
    return pl.pallas_call(
        functools.partial(_swa_kernel, n_ctx=n_ctx),
        grid=(b, p // tq),
        in_specs=[pl.BlockSpec(memory_space=pltpu.SMEM),
                  pl.BlockSpec((1, tq, SWA_WIDTH), lambda bi, j: (bi, j, 0)),
                  pl.BlockSpec((1, p, ks.shape[2]), lambda bi, j: (bi, 0, 0)),
                  pl.BlockSpec((1, p, vs.shape[2]), lambda bi, j: (bi, 0, 0))],
        out_specs=pl.BlockSpec((1, tq, SWA_WIDTH), lambda bi, j: (bi, j, 0)),
        out_shape=jax.ShapeDtypeStruct((b, p, SWA_WIDTH), BF16),
        compiler_params=_cparams(("arbitrary", "arbitrary")),
        name="swa_attention",
    )(sink, qs, ks, vs)


def _pool_kernel(prev_ref, cur_ref, next_ref, w_ref, scale_ref, y_ref, ext_ref, *, n_ctx):
    j = pl.program_id(1)
    tm = cur_ref.shape[1]
    nj = pl.num_programs(1)
    s_len = (nj - 1) * tm
    is_ctx = j == nj - 1
    has_prev = (j >= 1) & (j < nj - 1)
    has_next = j < nj - 2
    ext_ref[0:POOL_HALO, :] = jnp.where(has_prev, prev_ref[0], 0.0)
    ext_ref[POOL_HALO:POOL_HALO + tm, :] = cur_ref[0]
    ext_ref[POOL_HALO + tm:POOL_HALO + tm + POOL_HALO, :] = jnp.where(has_next, next_ref[0], 0.0)
    t = lax.broadcasted_iota(I32, (tm, 1), 0)
    pos = jnp.where(is_ctx, t, j * tm + t)
    seg_len = jnp.where(is_ctx, n_ctx, s_len)
    for g, w in enumerate(POOL_WINDOWS):
        cols = slice(g * POOL_GROUP_DIM, (g + 1) * POOL_GROUP_DIM)
        acc = jnp.zeros((tm, POOL_GROUP_DIM), F32)
        for off in range(-(w // 2), w - w // 2):
            acc = acc + ext_ref[POOL_HALO + off:POOL_HALO + off + tm, cols]
        lo = jnp.maximum(pos - w // 2, 0)
        hi = jnp.minimum(pos + w - w // 2, seg_len)
        cnt = (hi - lo).astype(F32)
        pooled = acc / cnt - cur_ref[0, :, cols]
        y_ref[0, :, cols] = (_dot(pooled.astype(BF16), w_ref[g]) * scale_ref[:, cols]).astype(BF16)


def _pool_stage(u, w_pool, pool_scale, n_ctx):
    b, p, width = u.shape
    tm = TOKEN_TILE
    hb = tm // POOL_HALO
    n_halo_blocks = p // POOL_HALO
    return pl.pallas_call(
        functools.partial(_pool_kernel, n_ctx=n_ctx),
        grid=(b, p // tm),
        in_specs=[pl.BlockSpec((1, POOL_HALO, width), lambda bi, j: (bi, jnp.maximum(j * hb - 1, 0), 0)),
                  pl.BlockSpec((1, tm, width), lambda bi, j: (bi, j, 0)),
                  pl.BlockSpec((1, POOL_HALO, width),
                               lambda bi, j: (bi, jnp.minimum((j + 1) * hb, n_halo_blocks - 1), 0)),
                  pl.BlockSpec(w_pool.shape, lambda bi, j: (0, 0, 0)),
                  pl.BlockSpec(pool_scale.shape, lambda bi, j: (0, 0))],
        out_specs=pl.BlockSpec((1, tm, width), lambda bi, j: (bi, j, 0)),
        out_shape=jax.ShapeDtypeStruct((b, p, width), BF16),
        scratch_shapes=[pltpu.VMEM((tm + 2 * POOL_HALO, width), F32)],
        compiler_params=_cparams(("arbitrary", "arbitrary")),
        name="pool_stage",
    )(u, u, u, w_pool, pool_scale)


MERGE_TILES = 2


def _merge_kernel(*refs, n_stream, ctx_tile, tiles_per_row):
    n_in = n_stream + 6
    tiles = [refs[t * n_in:(t + 1) * n_in] for t in range(MERGE_TILES)]
    (g_ref, wbm_ref, wbp_ref, wbs_ref, wout_ref, wsg_ref, wsu_ref, wsd_ref, wrt_ref,
     xs_ref, fp_ref, lg_ref) = refs[MERGE_TILES * n_in:]
    d = SLAB_ROWS * 128
    tm = tiles[0][n_stream + 2].shape[1]
    both = range(MERGE_TILES)
    is_ctx = [(pl.program_id(0) * MERGE_TILES + t) % tiles_per_row == ctx_tile for t in both]
    mod = [tiles[t][n_stream + 5][0, 0] for t in both]
    gate = [tiles[t][n_stream + 4][0] for t in both]
    branches = []
    for t in both:
        yml_ref, ymc_ref, yp_ref, ys_ref = tiles[t][n_stream:n_stream + 4]
        ym = jnp.where(is_ctx[t], ymc_ref[0], yml_ref[0])
        branches.append((_dot(ym, wbm_ref[...]), _dot(yp_ref[0], wbp_ref[...]), _dot(ys_ref[0], wbs_ref[...])))
    merged = [(gate[t][:, 0:d].astype(F32) * branches[t][0] + gate[t][:, d:2 * d].astype(F32) * branches[t][1]
               + gate[t][:, 2 * d:3 * d].astype(F32) * branches[t][2]).astype(BF16) for t in both]
    proj = [_dot(merged[t], wout_ref[...]) for t in both]
    x_mid = [_stream_tile(tiles[t][:n_stream], tm, is_ctx[t]) + mod[t][2:3] * proj[t] for t in both]
    f = [_rms(x_mid[t], g_ref[...]) * (1.0 + mod[t][4:5]) + mod[t][3:4] for t in both]
    fb = [f[t].astype(BF16) for t in both]
    up = [(_dot(fb[t], wsg_ref[...]), _dot(fb[t], wsu_ref[...])) for t in both]
    for t in both:
        lg_ref[:, t * tm:(t + 1) * tm] = _dot_nt(wrt_ref[...], fb[t])
    hmid = [(up[t][0] * _sigmoid(up[t][0]) * up[t][1]).astype(BF16) for t in both]
    shared = [_dot(hmid[t], wsd_ref[...]) for t in both]
    for t in both:
        _matrix_to_slab_rows(xs_ref.at[t * tm * SLAB_ROWS:(t + 1) * tm * SLAB_ROWS, :],
                             x_mid[t] + mod[t][5:6] * shared[t], SLAB_ROWS)
        _matrix_to_slab_rows(fp_ref.at[t * tm * PACKED_ROWS:(t + 1) * tm * PACKED_ROWS, :],
                             _pack_bf16_pair(f[t]), PACKED_ROWS)


def _merge_stage(stream, ym_lat, ym_ctx, yp, ys, gates, modtab, g_ffn, wbm, wbp, wbs, wout, wsg, wsu, wsd, wrt,
                 keep_ctx):
    b, p, _ = yp.shape
    assert ym_ctx.shape[1] == TOKEN_TILE
    d = SLAB_ROWS * 128
    tm = TOKEN_TILE
    nj = p // tm
    nk = nj if keep_ctx else nj - 1
    per = MERGE_TILES
    assert (b * nk) % per == 0
    full = lambda a: pl.BlockSpec(a.shape, lambda s: (0,) * a.ndim)

    def tile_specs(t):
        row = lambda s: (s * per + t) // nk
        col = lambda s: (s * per + t) % nk
        tok = lambda w: pl.BlockSpec((1, tm, w), lambda s: (row(s), col(s), 0))
        lat = lambda w: pl.BlockSpec((1, tm, w), lambda s: (row(s), jnp.minimum(col(s), nj - 2), 0))
        ctx = lambda w: pl.BlockSpec((1, tm, w), lambda s: (row(s), 0, 0))
        if len(stream) == 1:
            x_specs = [pl.BlockSpec((tm * SLAB_ROWS, 128), lambda s: (row(s) * nj + col(s), 0))]
        else:
            x_specs = [lat(d), ctx(d)]
        return x_specs + [lat(ym_lat.shape[2]), ctx(ym_ctx.shape[2]), tok(yp.shape[2]), tok(ys.shape[2]),
                          tok(gates.shape[2]),
                          pl.BlockSpec((1, 1, 8, d), lambda s: (row(s), col(s) // (nj - 1), 0, 0))]

    tile_args = list(stream) + [ym_lat, ym_ctx, yp, ys, gates, modtab]
    return pl.pallas_call(
        functools.partial(_merge_kernel, n_stream=len(stream), ctx_tile=nj - 1, tiles_per_row=nk),
        grid=(b * nk // per,),
        in_specs=[spec for t in range(per) for spec in tile_specs(t)] + [
            full(g_ffn), full(wbm), full(wbp), full(wbs), full(wout), full(wsg), full(wsu), full(wsd), full(wrt)],
        out_specs=[pl.BlockSpec((per * tm * SLAB_ROWS, 128), lambda s: (s, 0)),
                   pl.BlockSpec((per * tm * PACKED_ROWS, 128), lambda s: (s, 0)),
                   pl.BlockSpec((N_EXPERTS, per * tm), lambda s: (0, s))],
        out_shape=[jax.ShapeDtypeStruct((b * nk * tm * SLAB_ROWS, 128), F32),
                   jax.ShapeDtypeStruct((b * nk * tm * PACKED_ROWS, 128), U32),
                   jax.ShapeDtypeStruct((N_EXPERTS, b * nk * tm), F32)],
        compiler_params=_cparams(("arbitrary",)),
        name="merge_stage",
    )(*(tile_args * per), g_ffn, wbm, wbp, wbs, wout, wsg, wsu, wsd, wrt)


def _route_kernel(lg_ref, bias_ref, dest_ref, wts_ref, seg_ref):
    tm = lg_ref.shape[1]
    ne = N_EXPERTS
    neg_inf = -jnp.inf
    scores = _sigmoid(lg_ref[...])
    sel = scores + bias_ref[...]
    iota_g = lax.broadcasted_iota(I32, (EXPERTS_PER_GROUP, tm), 0)
    gscore = []
    for g in range(N_GROUPS):
        sg = sel[g * EXPERTS_PER_GROUP:(g + 1) * EXPERTS_PER_GROUP]
        m1 = jnp.max(sg, axis=0, keepdims=True)
        i1 = jnp.min(jnp.where(sg == m1, iota_g, EXPERTS_PER_GROUP), axis=0, keepdims=True)
        m2 = jnp.max(jnp.where(iota_g == i1, neg_inf, sg), axis=0, keepdims=True)
        gscore.append(m1 + m2)
    rows = []
    for g in range(N_GROUPS):
        rank = jnp.zeros((1, tm), I32)
        for g2 in range(N_GROUPS):
            if g2 == g:
                continue
            beats = (gscore[g2] >= gscore[g]) if g2 < g else (gscore[g2] > gscore[g])
            rank = rank + beats.astype(I32)
        rows.append(jnp.where(rank < TOPK_GROUPS, sel[g * EXPERTS_PER_GROUP:(g + 1) * EXPERTS_PER_GROUP], neg_inf))
    cur = jnp.concatenate(rows, axis=0)
    iota_e = lax.broadcasted_iota(I32, (ne, tm), 0)
    picks = []
    member = jnp.zeros((ne, tm), F32)
    for _ in range(TOP_K):
        m = jnp.max(cur, axis=0, keepdims=True)
        idx = jnp.min(jnp.where(cur == m, iota_e, ne), axis=0, keepdims=True)
        hit = iota_e == idx
        picks.append(hit)
        member = member + hit.astype(F32)
        cur = jnp.where(hit, neg_inf, cur)
    earlier = (lax.broadcasted_iota(I32, (tm, tm), 0) < lax.broadcasted_iota(I32, (tm, tm), 1)).astype(BF16)
    pos = _dot(member.astype(BF16), earlier)
    cnt_col = jnp.sum(member, axis=1, keepdims=True)
    blocks_col = jnp.floor((cnt_col + (SEG_ALIGN - 1)) * (1.0 / SEG_ALIGN))
    lower = (lax.broadcasted_iota(I32, (ne, ne), 1) < lax.broadcasted_iota(I32, (ne, ne), 0)).astype(BF16)
    off_col = _dot(lower, jnp.broadcast_to(blocks_col, (ne, 128)).astype(BF16))[:, 0:1] * SEG_ALIGN
    base = off_col + pos
    w_rows = [jnp.sum(jnp.where(hit, scores, 0.0), axis=0, keepdims=True) for hit in picks]
    denom = w_rows[0]
    for w in w_rows[1:]:
        denom = denom + w
    for k, hit in enumerate(picks):
        dest_ref[0, k:k + 1, :] = jnp.sum(jnp.where(hit, base, 0.0), axis=0, keepdims=True).astype(I32)
        wts_ref[0, k:k + 1, :] = w_rows[k] / denom * ROUTED_SCALE
    member_pad = jnp.concatenate([member, jnp.zeros((128 - ne, tm), F32)], axis=0).astype(BF16)
    cnt_row = _dot_nt(jnp.ones((8, tm), BF16), member_pad)
    blocks_row = jnp.floor((cnt_row + (SEG_ALIGN - 1)) * (1.0 / SEG_ALIGN))
    before = (lax.broadcasted_iota(I32, (128, 128), 0) < lax.broadcasted_iota(I32, (128, 128), 1)).astype(BF16)
    off_row = _dot(blocks_row.astype(BF16), before) * SEG_ALIGN
    r = lax.broadcasted_iota(I32, (8, 128), 0)
    seg_ref[0] = jnp.where(r == 0, off_row, jnp.where(r == 1, cnt_row, 0.0)).astype(I32)


def _route_stage(lg_t, bias_col):
    ne, t_all = lg_t.shape
    tm = MOE_TILE
    nt = t_all // tm
    return pl.pallas_call(
        _route_kernel,
        grid=(nt,),
        in_specs=[pl.BlockSpec((ne, tm), lambda i: (0, i)),
                  pl.BlockSpec((ne, 1), lambda i: (0, 0))],
        out_specs=[pl.BlockSpec((1, TOP_K, tm), lambda i: (i, 0, 0)),
                   pl.BlockSpec((1, TOP_K, tm), lambda i: (i, 0, 0)),
                   pl.BlockSpec((1, 8, 128), lambda i: (i, 0, 0))],
        out_shape=[jax.ShapeDtypeStruct((nt, TOP_K, tm), I32),
                   jax.ShapeDtypeStruct((nt, TOP_K, tm), F32),
                   jax.ShapeDtypeStruct((nt, 8, 128), I32)],
        compiler_params=_cparams(("arbitrary",)),
        name="route_stage",
    )(lg_t, bias_col)


def _moe_row_stride(tm):
    cap = TOP_K * tm + N_EXPERTS * SEG_ALIGN + MOE_CHUNK
    blocks = cap // 8 + 1
    return 8 * (blocks + 1 - blocks % 2)


def _moe_kernel(dest_ref, wts_ref, seg_ref, fp_hbm, xs_hbm, gt_ref, gfin_ref, wg_ref, wu_ref, wd_ref, out_hbm,
                bufs_ref, fp_stage, xs_stage, out_stage, slab_tmp, fp_sem, xs_sem, out_sem,
                *, srow, sub, n_sub_per_batch, ctx_sub, n_tiles, final_norm):
    pr = pl.program_id(0)
    eg = pl.program_id(1)
    tm = dest_ref.shape[2]
    ch = MOE_CHUNK
    group = wg_ref.shape[0]
    lane_blk = 128
    n_blk = tm // lane_blk
    per = bufs_ref.shape[0]
    n_slots = fp_stage.shape[0]
    n_here = jnp.minimum(per, n_tiles - pr * per)

    def table_block(ref, half, blk):
        return ref.at[half, :, pl.ds(pl.multiple_of(blk * lane_blk, lane_blk), lane_blk)]

    def table_rows(ref, half, blk):
        cols = pl.ds(pl.multiple_of(blk * lane_blk, lane_blk), lane_blk)
        return [ref.at[half, k, cols] for k in range(TOP_K)]

    def token_rows(half, blk, per_tok):
        first = ((pr * per + half) * tm + blk * lane_blk) * per_tok
        return pl.ds(pl.multiple_of(first, lane_blk * per_tok), lane_blk * per_tok)

    def fp_copy(half, blk, slot):
        return pltpu.make_async_copy(fp_hbm.at[token_rows(half, blk, PACKED_ROWS), :], fp_stage.at[slot],
                                     fp_sem.at[slot])

    def xs_copy(half, blk, slot):
        return pltpu.make_async_copy(xs_hbm.at[token_rows(half, blk, SLAB_ROWS), :], xs_stage.at[slot],
                                     xs_sem.at[slot])

    def out_copy(half, blk, slot):
        if not final_norm:
            dst = out_hbm.at[token_rows(half, blk, SLAB_ROWS), :]
        else:
            first = (pr * per + half) * tm + blk * lane_blk
            row_len = out_hbm.shape[1]
            bi = first // row_len
            dst = out_hbm.at[bi, pl.ds(pl.multiple_of(first - bi * row_len, lane_blk), lane_blk), :]
        return pltpu.make_async_copy(out_stage.at[slot], dst, out_sem.at[slot])

    def group_rows(half):
        buf_ref = bufs_ref.at[half]
        for blk in range(n_slots - 1):
            fp_copy(half, blk, blk).start()
        zeros8 = jnp.zeros((SEG_ALIGN, 128), U32)
        for e in range(N_EXPERTS):
            pad0 = seg_ref[half, 0, e] + (seg_ref[half, 1, e] // SEG_ALIGN) * SEG_ALIGN
            for q in range(PACKED_ROWS):
                buf_ref[pl.ds(pl.multiple_of(q * srow + pad0, SEG_ALIGN), SEG_ALIGN), :] = zeros8
        last = N_EXPERTS - 1
        end = seg_ref[half, 0, last] + (seg_ref[half, 1, last] + SEG_ALIGN - 1) // SEG_ALIGN * SEG_ALIGN
        for q in range(PACKED_ROWS):
            buf_ref[pl.ds(pl.multiple_of(q * srow + end, SEG_ALIGN), ch), :] = jnp.zeros((ch, 128), U32)

        def body(blk, carry):
            slot = blk % n_slots
            fp_copy(half, blk, slot).wait()

            @pl.when(blk + n_slots - 1 < n_blk)
            def _prefetch():
                fp_copy(half, blk + n_slots - 1, (blk + n_slots - 1) % n_slots).start()

            dest = table_rows(dest_ref, half, blk)
            rows = fp_stage.at[slot]
            for u in range(lane_blk):
                slab = rows[u * PACKED_ROWS:(u + 1) * PACKED_ROWS, :]
                for k in range(TOP_K):
                    buf_ref[pl.ds(dest[k][u], PACKED_ROWS, stride=srow), :] = slab
            return carry

        lax.fori_loop(0, n_blk, body, 0)

    def experts():
        segments = [(ge, h) for ge in range(group) for h in range(per)]
        counts = {(ge, h): seg_ref[h, 1, eg * group + ge] for ge, h in segments}
        offsets = {(ge, h): seg_ref[h, 0, eg * group + ge] for ge, h in segments}

        def ffn_rows(segs, rows, chunk_idx):
            first = {s: offsets[s] + chunk_idx * rows for s in segs}
            words = {s: [bufs_ref[s[1], pl.ds(pl.multiple_of(q * srow + first[s], 8), rows), :] for q in range(4)]
                     for s in segs}
            xb = {}
            for s in segs:
                w = jnp.concatenate(words[s], axis=1)
                xb[s] = jnp.concatenate([_unpack_lo(w), _unpack_hi(w)], axis=1).astype(BF16)
            gates = {s: (_dot(xb[s], wg_ref[s[0]]), _dot(xb[s], wu_ref[s[0]])) for s in segs}
            hmid = {s: (gates[s][0] * _sigmoid(gates[s][0]) * gates[s][1]).astype(BF16) for s in segs}
            packed = {s: _pack_bf16_pair(_dot(hmid[s], wd_ref[s[0]])) for s in segs}
            for s in segs:
                keep = lax.broadcasted_iota(I32, (rows, 1), 0) < counts[s] - chunk_idx * rows
                for q in range(4):
                    bufs_ref[s[1], pl.ds(pl.multiple_of(q * srow + first[s], 8), rows), :] = jnp.where(
                        keep, packed[s][:, q * 128:(q + 1) * 128], words[s][q])

        most = functools.reduce(jnp.maximum, counts.values())
        lo = 0
        for rows in MOE_BODY_ROWS:
            pl.when((most > lo) & (most <= rows))(functools.partial(ffn_rows, tuple(segments), rows, 0))
            lo = rows

        @pl.when(most > ch)
        def _long_segments():
            for s in segments:
                def chunk(c, carry, s=s):
                    ffn_rows((s,), ch, c)
                    return carry

                lax.fori_loop(0, (counts[s] + ch - 1) // ch, chunk, 0)

    def combine(half):
        buf_ref = bufs_ref.at[half]
        for blk in range(n_slots - 1):
            xs_copy(half, blk, blk).start()

        def body(blk, carry):
            slot = blk % n_slots
            xs_copy(half, blk, slot).wait()

            @pl.when(blk + n_slots - 1 < n_blk)
            def _prefetch():
                xs_copy(half, blk + n_slots - 1, (blk + n_slots - 1) % n_slots).start()

            @pl.when(blk >= n_slots)
            def _slot_free():
                out_copy(half, blk - n_slots, slot).wait()

            sub_blk = ((pr * per + half) * tm + blk * lane_blk) // sub
            bi = sub_blk // n_sub_per_batch
            gate = gt_ref[bi * 2 + (sub_blk - bi * n_sub_per_batch) // ctx_sub]
            dest = table_block(dest_ref, half, blk)
            wts = table_block(wts_ref, half, blk)
            xs = xs_stage.at[slot]
            out = slab_tmp if final_norm else out_stage.at[slot]
            for u in range(lane_blk):
                acc_lo = jnp.zeros((4, 128), F32)
                acc_hi = jnp.zeros((4, 128), F32)
                for k in range(TOP_K):
                    words = buf_ref[pl.ds(dest[k, u], PACKED_ROWS, stride=srow), :]
                    wk = wts[k, u]
                    acc_lo = acc_lo + wk * _unpack_lo(words)
                    acc_hi = acc_hi + wk * _unpack_hi(words)
                out[u * 8:u * 8 + 4, :] = xs[u * 8:u * 8 + 4, :] + gate[0:4] * acc_lo
                out[u * 8 + 4:u * 8 + 8, :] = xs[u * 8 + 4:u * 8 + 8, :] + gate[4:8] * acc_hi
            if final_norm:
                out_stage[slot] = _rms(_slab_rows_to_matrix(slab_tmp, lane_blk, SLAB_ROWS), gfin_ref[...])
            out_copy(half, blk, slot).start()
            return carry

        lax.fori_loop(0, n_blk, body, 0)
        for blk in range(n_blk - n_slots, n_blk):
            out_copy(half, blk, blk % n_slots).wait()

    def each_tile(fn):
        def body(half, carry):
            fn(half)
            return carry

        lax.fori_loop(0, n_here, body, 0)

    @pl.when(eg == 0)
    def _first_step():
        each_tile(group_rows)
        for half in range(1, per):
            @pl.when(half >= n_here)
            def _clear(half=half):
                bufs_ref[half] = jnp.zeros(bufs_ref.shape[1:], U32)

    experts()
    pl.when(eg == pl.num_programs(1) - 1)(functools.partial(each_tile, combine))


def _moe_stage(dest, wts, seg, fp4, xs8, gt2, g_final, wg, wu, wd, layer, n_sub_per_batch, keep_ctx, final_norm):
    ctx_sub = n_sub_per_batch - 1 if keep_ctx else n_sub_per_batch
    d = SLAB_ROWS * 128
    tokens_per_row = n_sub_per_batch * TOKEN_TILE
    out_shape = (xs8.shape[0] // (SLAB_ROWS * tokens_per_row), tokens_per_row, d) if final_norm else xs8.shape
    out_block = (128, d) if final_norm else (128 * SLAB_ROWS, 128)
    nt, _, tm = dest.shape
    ne = wg.shape[1]
    group = MOE_EXPERTS_PER_STEP
    per = MOE_TILES_PER_STEP
    assert ne % group == 0 and tm % 256 == 0 and TOKEN_TILE % 128 == 0 and per == 2
    n_rows = -(-nt // per)
    pad = ((0, n_rows * per - nt), (0, 0), (0, 0))
    dest, wts, seg = jnp.pad(dest, pad), jnp.pad(wts, pad), jnp.pad(seg, pad)
    srow = _moe_row_stride(tm)
    smem = lambda shape: pl.BlockSpec(shape, lambda i, e: (i, 0, 0), memory_space=pltpu.SMEM)
    hbm = pl.BlockSpec(memory_space=pl.ANY)
    slots = MOE_STAGE_SLOTS
    assert tm // 128 >= slots
    stage = lambda per_tok, dt: pltpu.VMEM((slots, 128 * per_tok, 128), dt)
    return pl.pallas_call(
        functools.partial(_moe_kernel, srow=srow, sub=TOKEN_TILE, n_sub_per_batch=n_sub_per_batch, ctx_sub=ctx_sub,
                          n_tiles=nt, final_norm=final_norm),
        grid=(n_rows, ne // group),
        in_specs=[smem((per, TOP_K, tm)), smem((per, TOP_K, tm)), smem((per, 8, 128)),
                  hbm, hbm,
                  pl.BlockSpec(gt2.shape, lambda i, e: (0, 0, 0)),
                  pl.BlockSpec(g_final.shape, lambda i, e: (0, 0)),
                  pl.BlockSpec((None, group) + wg.shape[2:], lambda i, e: (layer, e, 0, 0)),
                  pl.BlockSpec((None, group) + wu.shape[2:], lambda i, e: (layer, e, 0, 0)),
                  pl.BlockSpec((None, group) + wd.shape[2:], lambda i, e: (layer, e, 0, 0))],
        out_specs=hbm,
        out_shape=jax.ShapeDtypeStruct(out_shape, F32),
        scratch_shapes=[pltpu.VMEM((per, 4 * srow, 128), U32),
                        stage(PACKED_ROWS, U32), stage(SLAB_ROWS, F32), pltpu.VMEM((slots,) + out_block, F32),
                        pltpu.VMEM((128 * SLAB_ROWS, 128), F32),
                        pltpu.SemaphoreType.DMA((slots,)), pltpu.SemaphoreType.DMA((slots,)),
                        pltpu.SemaphoreType.DMA((slots,))],
        compiler_params=_cparams(("arbitrary", "arbitrary")),
        name="moe_stage",
    )(dest, wts, seg, fp4, xs8, gt2, g_final, wg, wu, wd)


def _rot_lanes(x, head_dim):
    quarter = head_dim // 4
    lane = lax.broadcasted_iota(I32, x.shape, 1)
    width = x.shape[1]
    return jnp.where(lane % (2 * quarter) < quarter, -pltpu.roll(x, width - quarter, 1), pltpu.roll(x, quarter, 1))


def _w1_kernel(w_ref, o_ref):
    offs = [0]
    for w in (MLA_Q_LORA, MLA_KV_LORA, MLA_ROPE, POOL_WIDTH, SWA_WIDTH, SWA_KV_WIDTH, SWA_KV_WIDTH):
        offs.append(offs[-1] + w)
    o_cq, o_ckv, o_kr, o_u, o_qs, o_ks, o_vs, o_gl = offs
    lane = lax.broadcasted_iota(I32, (w_ref.shape[0], 128), 1)

    def put(name, val):
        a, b = _SEG[name]
        o_ref[:, a:b] = val.astype(BF16)

    def kv_slots(x):
        low = lane < SWA_HEAD_DIM
        return jnp.concatenate([jnp.where(low, x, 0.0), jnp.where(low, pltpu.roll(x, SWA_HEAD_DIM, 1), 0.0)], axis=1)

    def kr_slot(x):
        return jnp.where((lane >= MLA_NOPE) & (lane < MLA_NOPE + MLA_ROPE), pltpu.roll(x, MLA_NOPE, 1), 0.0)

    put("cq", w_ref[:, o_cq:o_ckv])
    put("ckv", w_ref[:, o_ckv:o_kr])
    put("u", w_ref[:, o_u:o_qs])
    put("qs", w_ref[:, o_qs:o_ks] * (SWA_SCALE * LOG2_E))
    put("ks", kv_slots(w_ref[:, o_ks:o_vs]))
    put("vs", kv_slots(w_ref[:, o_vs:o_gl]))
    put("kr", kr_slot(w_ref[:, o_kr:o_kr + 128]))
    put("gl", w_ref[:, o_gl:w_ref.shape[1]])


def _fused_in_weight(w_in, layer):
    _, d, width = w_in.shape
    tr = 256
    assert d % tr == 0 and SWA_KV_WIDTH == 128 and MLA_KV_LORA + MLA_Q_LORA == 5 * 128
    return pl.pallas_call(
        _w1_kernel,
        grid=(d // tr,),
        in_specs=[pl.BlockSpec((None, tr, width), lambda i: (layer, i, 0))],
        out_specs=pl.BlockSpec((tr, FUSED_IN_WIDTH), lambda i: (i, 0)),
        out_shape=jax.ShapeDtypeStruct((d, FUSED_IN_WIDTH), BF16),
        compiler_params=_cparams(("arbitrary",)),
        name="fused_in_weight",
    )(w_in)


def _rope_pattern(s_len, n_ctx, rot_dim):
    t = jnp.arange(s_len)
    row = (t // GRID_W).astype(F32)
    col = (t % GRID_W).astype(F32)
    n_freq = rot_dim // 4
    inv_freq = ROPE_BASE ** (-jnp.arange(n_freq, dtype=F32) / n_freq)
    ang_r = row[:, None] * inv_freq[None, :]
    ang_c = col[:, None] * inv_freq[None, :]
    cos = jnp.concatenate([jnp.cos(ang_r), jnp.cos(ang_r), jnp.cos(ang_c), jnp.cos(ang_c)], axis=1)
    sin = jnp.concatenate([jnp.sin(ang_r), jnp.sin(ang_r), jnp.sin(ang_c), jnp.sin(ang_c)], axis=1)
    cos = jnp.concatenate([cos, jnp.ones((n_ctx, rot_dim), F32)], axis=0)
    sin = jnp.concatenate([sin, jnp.zeros((n_ctx, rot_dim), F32)], axis=0)
    return cos, sin


def _layer_weights(w_in, layer, w_uq, w_ukv):
    w1 = _fused_in_weight(w_in, layer)
    lq = w_uq.shape[0]
    wq3 = (w_uq * (MLA_SCALE * LOG2_E)).reshape(lq, MLA_HEADS, MLA_NOPE + MLA_ROPE)
    pad = jnp.zeros((lq, MLA_HEADS, MLA_HEAD_PAD - MLA_NOPE - MLA_ROPE), F32)
    wq = jnp.concatenate([wq3, pad], axis=2).reshape(lq, MLA_HEADS * MLA_HEAD_PAD).astype(BF16)
    lkv = w_ukv.shape[0]
    wkv3 = w_ukv.reshape(lkv, MLA_HEADS, MLA_NOPE + MLA_V)
    wkn = jnp.concatenate([wkv3[:, :, :MLA_NOPE], jnp.zeros((lkv, MLA_HEADS, MLA_HEAD_PAD - MLA_NOPE), F32)],
                          axis=2).reshape(lkv, MLA_HEADS * MLA_HEAD_PAD).astype(BF16)
    wv = jnp.concatenate([wkv3[:, :, MLA_NOPE:], jnp.zeros((lkv, MLA_HEADS, MLA_HEAD_PAD - MLA_V), F32)],
                         axis=2).reshape(lkv, MLA_HEADS * MLA_HEAD_PAD).astype(BF16)
    return w1, wq, wkn, wv


def kernel(x, c, ctx, c_ctx, w_mod, b_mod, g_mix, g_ffn, w_in, g_mla_q, g_mla_kv, w_mla_uq, w_mla_ukv, w_pool,
           pool_scale, swa_sink, w_br_mla, w_br_pool, w_br_swa, w_out, w_router, router_bias, w_exp_gate,
           w_exp_up, w_exp_down, w_sh_gate, w_sh_up, w_sh_down, g_final):
    b, s_len, d = x.shape
    n_ctx = ctx.shape[1]
    n_layers = w_mod.shape[0]
    p = n_ctx + s_len
    assert n_ctx == TOKEN_TILE and s_len % TOKEN_TILE == 0 and (b * p) % MOE_TILE == 0 and b + 1 <= 8
    assert d == 1024 and w_in.shape[2] == (MLA_Q_LORA + MLA_KV_LORA + MLA_ROPE + POOL_WIDTH + SWA_WIDTH
                                           + 2 * SWA_KV_WIDTH + 3 * d)

    cm, sm = _rope_pattern(s_len, n_ctx, MLA_ROPE)
    tail = jnp.zeros((p, MLA_HEAD_PAD - MLA_NOPE - MLA_ROPE), F32)
    tabm = jnp.concatenate([jnp.ones((p, MLA_NOPE), F32), cm, tail, jnp.zeros((p, MLA_NOPE), F32), sm, tail], axis=1)
    cs, ss = _rope_pattern(s_len, n_ctx, SWA_HEAD_DIM)
    tabs = jnp.concatenate([cs, cs, ss, ss], axis=1)

    cvec = jnp.concatenate([c, c_ctx[None, :], jnp.zeros((8 - b - 1, d), F32)], axis=0)
    mod_all = _modulation(cvec, w_mod, b_mod).reshape(n_layers, 8, N_MOD, d)

    stream = (x, ctx)
    wg_all, wu_all, wd_all = w_exp_gate.astype(BF16), w_exp_up.astype(BF16), w_exp_down.astype(BF16)
    for i in range(n_layers):
        keep_ctx = i < n_layers - 1
        lat = mod_all[i, :b]
        ctx_rows = jnp.broadcast_to(mod_all[i, b][None], (b, N_MOD, d))
        modtab = jnp.pad(jnp.stack([lat, ctx_rows], axis=1), ((0, 0), (0, 0), (0, 8 - N_MOD), (0, 0)))
        w1, wq, wkn, wv = _layer_weights(w_in, i, w_mla_uq[i], w_mla_ukv[i])
        q, k, v, u, qs, ks, vs, gates = _input_stage(
            stream, b, modtab, g_mix[i][None], w1, g_mla_q[i][None], wq, g_mla_kv[i][None], wkn, wv, tabm, tabs)
        y_mla, y_mla_ctx = _mla_attention(q, k, v, n_ctx)
        y_swa = _swa_attention(swa_sink[i], qs, ks, vs, n_ctx)
        y_pool = _pool_stage(u, w_pool[i].astype(BF16), pool_scale[i][None], n_ctx)
        xs, fp, lg_t = _merge_stage(
            stream, y_mla, y_mla_ctx, y_pool, y_swa, gates, modtab, g_ffn[i][None], w_br_mla[i].astype(BF16),
            w_br_pool[i].astype(BF16), w_br_swa[i].astype(BF16), w_out[i].astype(BF16),
            w_sh_gate[i].astype(BF16), w_sh_up[i].astype(BF16), w_sh_down[i].astype(BF16),
            w_router[i].T.astype(BF16), keep_ctx)
        dest, wts, seg = _route_stage(lg_t, router_bias[i][:, None])
        gt2 = modtab[:, :, 5, :].reshape(b * 2, 8, d // 8)
        stream = (_moe_stage(dest, wts, seg, fp, xs, gt2, g_final[None], wg_all, wu_all, wd_all, i,
                             (p if keep_ctx else s_len) // TOKEN_TILE, keep_ctx, final_norm=not keep_ctx),)
    return stream[0]
```

```python
import functools

import jax
import jax.numpy as jnp
from jax import lax
from jax.experimental import pallas as pl
from jax.experimental.pallas import tpu as pltpu

F32 = jnp.float32
BF16 = jnp.bfloat16
U32 = jnp.uint32
I32 = jnp.int32

NORM_EPS = 1e-6
ROPE_BASE = 10000.0
GRID_W = 64
N_MOD = 6

MLA_HEADS = 8
MLA_Q_LORA = 384
MLA_KV_LORA = 256
MLA_NOPE = 64
MLA_ROPE = 32
MLA_V = 64
MLA_SCALE = (MLA_NOPE + MLA_ROPE) ** -0.5
MLA_HEAD_PAD = 128
LOG2_E = 1.4426950408889634

POOL_WINDOWS = (2, 4, 8, 16)
POOL_GROUP_DIM = 128
POOL_WIDTH = 512
POOL_HALO = 8

SWA_Q_HEADS = 8
SWA_KV_HEADS = 2
SWA_HEAD_DIM = 64
SWA_WINDOW = 128
SWA_BLOCK = 128
SWA_SCALE = SWA_HEAD_DIM ** -0.5
SWA_WIDTH = SWA_Q_HEADS * SWA_HEAD_DIM
SWA_KV_WIDTH = SWA_KV_HEADS * SWA_HEAD_DIM

N_EXPERTS = 64
TOP_K = 8
N_GROUPS = 8
TOPK_GROUPS = 4
EXPERTS_PER_GROUP = 8
D_EXPERT = 256
ROUTED_SCALE = 2.5

TOKEN_TILE = 256
SWA_Q_TILE = 256
MLA_Q_TILE = 2048
MLA_KEY_CHUNK = 256
MOE_TILE = 1024
MOE_CHUNK = 256
MOE_BODY_ROWS = (128, 144, 160, 176, 192, 224, 256)
MOE_EXPERTS_PER_STEP = 2
MOE_TILES_PER_STEP = 2
MOE_STAGE_SLOTS = 4
SEG_ALIGN = 8
SLAB_ROWS = 8
PACKED_ROWS = 4
MASK_VALUE = -1e30
HI16 = 0xFFFF0000

VMEM_LIMIT = 56 * 1024 * 1024

_SEG_WIDTHS = (("cq", 384), ("ckv", 256), ("u", 512), ("qs", 512), ("ks", 256), ("vs", 256), ("kr", 128),
               ("gl", 3072))
_SEG = {}
_o = 0
for _n, _w in _SEG_WIDTHS:
    _SEG[_n] = (_o, _o + _w)
    _o += _w
FUSED_IN_WIDTH = _o


def _cparams(sem):
    return pltpu.CompilerParams(dimension_semantics=sem, vmem_limit_bytes=VMEM_LIMIT)


def _dot(a, b):
    return jnp.dot(a, b, preferred_element_type=F32)


def _dot_nt(a, b):
    return lax.dot_general(a, b, (((1,), (1,)), ((), ())), preferred_element_type=F32)


def _sigmoid(x):
    return 1.0 / (1.0 + jnp.exp(-x))


def _rms(x, g):
    return x * lax.rsqrt(jnp.mean(x * x, axis=-1, keepdims=True) + NORM_EPS) * g


def _pack_bf16_pair(v):
    n = v.shape[1] // 2
    bits = pltpu.bitcast(v.astype(BF16).astype(F32), U32)
    return (bits[:, :n] >> 16) | (bits[:, n:] & jnp.uint32(HI16))


def _slab_rows_to_matrix(ref, n_tok, per_tok):
    return jnp.concatenate([ref[pl.ds(c, n_tok, stride=per_tok), :] for c in range(per_tok)], axis=1)


def _matrix_to_slab_rows(ref, val, per_tok):
    n_tok = val.shape[0]
    for c in range(per_tok):
        ref[pl.ds(c, n_tok, stride=per_tok), :] = val[:, c * 128:(c + 1) * 128]


def _unpack_lo(w):
    return pltpu.bitcast(w << 16, F32)


def _unpack_hi(w):
    return pltpu.bitcast(w & jnp.uint32(HI16), F32)


def _mod_kernel(c_ref, w_ref, b_ref, o_ref):
    c = c_ref[...]
    a = (c * _sigmoid(c)).astype(BF16)
    o_ref[0] = _dot(a, w_ref[0].astype(BF16)) + b_ref[0]


def _modulation(cvec, w_mod, b_mod):
    n_layers, d, width = w_mod.shape
    tn = width // 4
    return pl.pallas_call(
        _mod_kernel,
        grid=(n_layers, width // tn),
        in_specs=[pl.BlockSpec((8, d), lambda l, n: (0, 0)),
                  pl.BlockSpec((1, d, tn), lambda l, n: (l, 0, n)),
                  pl.BlockSpec((1, 1, tn), lambda l, n: (l, 0, n))],
        out_specs=pl.BlockSpec((1, 8, tn), lambda l, n: (l, 0, n)),
        out_shape=jax.ShapeDtypeStruct((n_layers, 8, width), F32),
        compiler_params=_cparams(("arbitrary", "arbitrary")),
        name="modulation",
    )(cvec, w_mod, b_mod.reshape(n_layers, 1, width))


def _stream_tile(refs, n_tok, is_ctx):
    if len(refs) == 1:
        return _slab_rows_to_matrix(refs[0], n_tok, SLAB_ROWS)
    return jnp.where(is_ctx, refs[1][0], refs[0][0])


def _stream_tokens(stream, b):
    return stream[0].shape[0] // (b * SLAB_ROWS) if len(stream) == 1 else stream[0].shape[1] + stream[1].shape[1]


INPUT_TILES = 2


def _in_kernel(*refs, n_stream, ctx_tile, tiles_per_row):
    n_in = n_stream + 3
    tiles = [refs[t * n_in:(t + 1) * n_in] for t in range(INPUT_TILES)]
    (g_ref, w1_ref, gq_ref, wq_ref, gkv_ref, wkn_ref, wv_ref,
     q_ref, k_ref, v_ref, u_ref, qs_ref, ks_ref, vs_ref, gate_ref) = refs[INPUT_TILES * n_in:]
    tm = tiles[0][n_stream + 1].shape[0]
    both = range(INPUT_TILES)
    rows = [slice(t * tm, (t + 1) * tm) for t in both]
    hb, tabs = [], []
    for t in both:
        is_ctx = (pl.program_id(0) * INPUT_TILES + t) % tiles_per_row == ctx_tile
        mod = tiles[t][n_stream][0, 0]
        x = _stream_tile(tiles[t][:n_stream], tm, is_ctx)
        hb.append((_rms(x, g_ref[...]) * (1.0 + mod[1:2]) + mod[0:1]).astype(BF16))
        tabm_ref, tabs_ref = tiles[t][n_stream + 1:n_stream + 3]
        tabs.append((tabm_ref[:, 0:128], tabm_ref[:, 128:256], tabs_ref[:, 0:128], tabs_ref[:, 128:256]))

    def seg(name):
        a, b = _SEG[name]
        return [_dot(hb[t], w1_ref[:, a:b]) for t in both]

    def rotary(val, cos, sin, head_dim):
        pieces = [val[:, c:c + 128] for c in range(0, val.shape[1], 128)]
        return jnp.concatenate([p * cos + _rot_lanes(p, head_dim) * sin for p in pieces], axis=1)

    cq, ckv, kr, u, qs, ks, vs = (seg(n) for n in ("cq", "ckv", "kr", "u", "qs", "ks", "vs"))
    cqn = [_rms(cq[t], gq_ref[...]).astype(BF16) for t in both]
    ckvn = [_rms(ckv[t], gkv_ref[...]).astype(BF16) for t in both]
    q_up = [_dot(cqn[t], wq_ref[...]) for t in both]
    k_up = [_dot(ckvn[t], wkn_ref[...]) for t in both]
    v_up = [_dot(ckvn[t], wv_ref[...]) for t in both]
    ones_m = (lax.broadcasted_iota(I32, (1, MLA_HEADS * MLA_HEAD_PAD), 1) % MLA_HEAD_PAD == MLA_V).astype(F32)
    ones_s = (lax.broadcasted_iota(I32, (1, SWA_KV_HEADS * 128), 1) % 128 == SWA_HEAD_DIM).astype(F32)
    for t in both:
        cos_m, sin_m, cos_s, sin_s = tabs[t]
        q_ref[rows[t], :] = rotary(q_up[t], cos_m, sin_m, MLA_ROPE).astype(BF16)
        k_ref[rows[t], :] = (k_up[t] + jnp.tile(rotary(kr[t], cos_m, sin_m, MLA_ROPE), (1, MLA_HEADS))).astype(BF16)
        v_ref[rows[t], :] = (v_up[t] + ones_m).astype(BF16)
        u_ref[rows[t], :] = u[t]
        qs_ref[rows[t], :] = rotary(qs[t], cos_s, sin_s, SWA_HEAD_DIM).astype(BF16)
        ks_rot = rotary(ks[t], cos_s, sin_s, SWA_HEAD_DIM)
        for hk in range(SWA_KV_HEADS):
            k_lo = ks_rot[:, hk * 128:(hk + 1) * 128]
            ks_ref[rows[t], (2 * hk) * 128:(2 * hk + 1) * 128] = k_lo.astype(BF16)
            ks_ref[rows[t], (2 * hk + 1) * 128:(2 * hk + 2) * 128] = pltpu.roll(k_lo, SWA_HEAD_DIM, 1).astype(BF16)
        vs_ref[rows[t], :] = (vs[t] + ones_s).astype(BF16)

    g0, _ = _SEG["gl"]
    for p in range(6):
        for t in both:
            gate_ref[rows[t], p * 512:(p + 1) * 512] = _sigmoid(
                _dot(hb[t], w1_ref[:, g0 + p * 512:g0 + (p + 1) * 512])).astype(BF16)


def _input_stage(stream, b, modtab, g_mix, w1, g_q, wq, g_kv, wkn, wv, tabm, tabs):
    d = SLAB_ROWS * 128
    p = _stream_tokens(stream, b)
    tm = TOKEN_TILE
    nj = p // tm
    per = INPUT_TILES
    assert (b * nj) % per == 0
    full = lambda a: pl.BlockSpec(a.shape, lambda s: (0,) * a.ndim)
    outs = [(1024, BF16), (1024, BF16), (1024, BF16), (512, F32), (512, BF16), (512, BF16), (256, BF16),
            (3072, BF16)]

    def tile_specs(t):
        row = lambda s: (s * per + t) // nj
        col = lambda s: (s * per + t) % nj
        if len(stream) == 1:
            x_specs = [pl.BlockSpec((tm * SLAB_ROWS, 128), lambda s: (s * per + t, 0))]
        else:
            x_specs = [pl.BlockSpec((1, tm, d), lambda s: (row(s), jnp.minimum(col(s), nj - 2), 0)),
                       pl.BlockSpec((1, tm, d), lambda s: (row(s), 0, 0))]
        return x_specs + [pl.BlockSpec((1, 1, 8, d), lambda s: (row(s), col(s) // (nj - 1), 0, 0)),
                          pl.BlockSpec((tm, 256), lambda s: (col(s), 0)),
                          pl.BlockSpec((tm, 256), lambda s: (col(s), 0))]

    tile_args = list(stream) + [modtab, tabm, tabs]
    res = pl.pallas_call(
        functools.partial(_in_kernel, n_stream=len(stream), ctx_tile=nj - 1, tiles_per_row=nj),
        grid=(b * nj // per,),
        in_specs=[spec for t in range(per) for spec in tile_specs(t)] + [
            full(g_mix), full(w1), full(g_q), full(wq), full(g_kv), full(wkn), full(wv)],
        out_specs=[pl.BlockSpec((per * tm, w), lambda s: (s, 0)) for w, _ in outs],
        out_shape=[jax.ShapeDtypeStruct((b * p, w), dt) for w, dt in outs],
        compiler_params=_cparams(("arbitrary",)),
        name="input_stage",
    )(*(tile_args * per), g_mix, w1, g_q, wq, g_kv, wkn, wv)
    return [r.reshape(b, p, r.shape[1]) for r in res]


def _mla_kernel(q_ref, k_ref, v_ref, *rest, tk, n_main, tail_rows):
    y_ref, m_ref, acc_ref = rest[-3:]
    hw = MLA_HEAD_PAD
    m_ref[...] = jnp.full(m_ref.shape, MASK_VALUE, F32)
    acc_ref[...] = jnp.zeros(acc_ref.shape, F32)

    def attend(r0, rows):
        for h in range(MLA_HEADS):
            q = q_ref[0, :, h * hw:(h + 1) * hw]
            k = k_ref[0, pl.ds(r0, rows), h * hw:(h + 1) * hw]
            v = v_ref[0, pl.ds(r0, rows), h * hw:(h + 1) * hw]
            s = _dot_nt(q, k)
            m_prev = m_ref[h]
            m_new = jnp.maximum(m_prev, jnp.max(s, axis=1, keepdims=True))
            m_ref[h] = m_new
            pr = jnp.exp2(s - jnp.tile(m_new, (1, rows // hw)))
            acc_ref[h] = jnp.exp2(m_prev - m_new) * acc_ref[h] + _dot(pr.astype(BF16), v)

    if n_main:
        def step(c, carry):
            attend(pl.multiple_of(c * tk, tk), tk)
            return carry

        lax.fori_loop(0, n_main, step, 0)
    if tail_rows:
        attend(n_main * tk, tail_rows)

    first = lax.broadcasted_iota(I32, (q_ref.shape[1], hw), 1) < MLA_V
    for hp in range(MLA_HEADS // 2):
        o = []
        for h in (2 * hp, 2 * hp + 1):
            acc = acc_ref[h]
            o.append(acc / acc[:, MLA_V:MLA_V + 1])
        y_ref[0, :, hp * hw:(hp + 1) * hw] = jnp.where(first, o[0], pltpu.roll(o[1], MLA_V, 1)).astype(BF16)


def _mla_attention(q, k, v, n_ctx):
    b, p, width = q.shape
    s_len = p - n_ctx
    tq = MLA_Q_TILE
    tk = MLA_KEY_CHUNK
    assert s_len % tq == 0 and s_len % tk == 0 and s_len % n_ctx == 0
    out_w = MLA_HEADS * MLA_V
    state = lambda rows: pltpu.VMEM((MLA_HEADS, rows, MLA_HEAD_PAD), F32)
    y_lat = pl.pallas_call(
        functools.partial(_mla_kernel, tk=tk, n_main=p // tk, tail_rows=p % tk),
        grid=(b, s_len // tq),
        in_specs=[pl.BlockSpec((1, tq, width), lambda bi, j: (bi, j, 0)),
                  pl.BlockSpec((1, p, width), lambda bi, j: (bi, 0, 0), pipeline_mode=pl.Buffered(1)),
                  pl.BlockSpec((1, p, width), lambda bi, j: (bi, 0, 0), pipeline_mode=pl.Buffered(1))],
        out_specs=pl.BlockSpec((1, tq, out_w), lambda bi, j: (bi, j, 0)),
        out_shape=jax.ShapeDtypeStruct((b, s_len, out_w), BF16),
        scratch_shapes=[state(tq), state(tq)],
        compiler_params=_cparams(("arbitrary", "arbitrary")),
        name="mla_attention",
    )(q, k, v)
    cblk = s_len // n_ctx
    ctx_rows = lambda w: pl.BlockSpec((1, n_ctx, w), lambda bi: (bi, cblk, 0))
    y_ctx = pl.pallas_call(
        functools.partial(_mla_kernel, tk=tk, n_main=0, tail_rows=n_ctx),
        grid=(b,),
        in_specs=[ctx_rows(width), ctx_rows(width), ctx_rows(width)],
        out_specs=pl.BlockSpec((1, n_ctx, out_w), lambda bi: (bi, 0, 0)),
        out_shape=jax.ShapeDtypeStruct((b, n_ctx, out_w), BF16),
        scratch_shapes=[state(n_ctx), state(n_ctx)],
        compiler_params=_cparams(("arbitrary",)),
        name="mla_attention_ctx",
    )(q, k, v)
    return y_lat, y_ctx


def _swa_kernel(sink_ref, q_ref, k_ref, v_ref, y_ref, *, n_ctx):
    p_len = k_ref.shape[1]
    s_len = p_len - n_ctx
    band = 3 * SWA_BLOCK
    n_lat_tiles = s_len // SWA_BLOCK
    first = lax.broadcasted_iota(I32, (SWA_BLOCK, 128), 1) < SWA_HEAD_DIM
    top = lax.broadcasted_iota(I32, (2 * SWA_BLOCK, 1), 0) < SWA_BLOCK
    n_blocks = q_ref.shape[1] // SWA_BLOCK
    chains = [(t, hk, par) for t in range(n_blocks) for hk in range(SWA_KV_HEADS) for par in range(2)]
    kstart, valid = {}, {}
    for t in range(n_blocks):
        j = pl.program_id(1) * n_blocks + t
        n = jnp.minimum(j, n_lat_tiles - 1)
        ws = jnp.clip((n - 1) * SWA_BLOCK, 0, s_len - band)
        kstart[t] = pl.multiple_of(ws, SWA_BLOCK)
        qpos = n * SWA_BLOCK + lax.broadcasted_iota(I32, (2 * SWA_BLOCK, band), 0) % SWA_BLOCK
        kpos = ws + lax.broadcasted_iota(I32, (2 * SWA_BLOCK, band), 1)
        valid[t] = (jnp.abs(qpos - kpos) <= SWA_WINDOW) & (j < n_lat_tiles)
    scores = {}
    for t, hk, par in chains:
        qrows = slice(t * SWA_BLOCK, (t + 1) * SWA_BLOCK)
        qpair = jnp.concatenate([q_ref[0, qrows, (2 * hk) * 128:(2 * hk + 1) * 128],
                                 q_ref[0, qrows, (2 * hk + 1) * 128:(2 * hk + 2) * 128]], axis=0)
        kcols = slice((2 * hk + par) * 128, (2 * hk + par + 1) * 128)
        s_c = _dot_nt(qpair, k_ref[0, s_len:p_len, kcols])
        s_b = jnp.where(valid[t], _dot_nt(qpair, k_ref[0, pl.ds(kstart[t], band), kcols]), MASK_VALUE)
        scores[t, hk, par] = (s_c, s_b)
    probs = {}
    for t, hk, par in chains:
        s_c, s_b = scores[t, hk, par]
        sink = jnp.where(top, sink_ref[4 * hk + par] * LOG2_E, sink_ref[4 * hk + 2 + par] * LOG2_E)
        m = jnp.maximum(jnp.maximum(jnp.max(s_c, axis=1, keepdims=True), jnp.max(s_b, axis=1, keepdims=True)), sink)
        probs[t, hk, par] = (jnp.exp2(s_c - m).astype(BF16), jnp.exp2(s_b - m).astype(BF16), jnp.exp2(sink - m))
    res = {}
    for t, hk, par in chains:
        p_c, p_b, p_sink = probs[t, hk, par]
        vcols = slice(hk * 128, (hk + 1) * 128)
        o = _dot(p_c, v_ref[0, s_len:p_len, vcols]) + _dot(p_b, v_ref[0, pl.ds(kstart[t], band), vcols])
        o = o / (o[:, SWA_HEAD_DIM:SWA_HEAD_DIM + 1] + p_sink)
        res[t, 4 * hk + par] = o[0:SWA_BLOCK]
        res[t, 4 * hk + 2 + par] = o[SWA_BLOCK:2 * SWA_BLOCK]
    for t in range(n_blocks):
        for pair in range(SWA_Q_HEADS // 2):
            y_ref[0, t * SWA_BLOCK:(t + 1) * SWA_BLOCK, pair * 128:(pair + 1) * 128] = jnp.where(
                first, res[t, 2 * pair], pltpu.roll(res[t, 2 * pair + 1], SWA_HEAD_DIM, 1)).astype(BF16)


def _swa_attention(sink, qs, ks, vs, n_ctx):
    b, p, _ = qs.shape
    tq = SWA_Q_TILE
    assert n_ctx % tq == 0 and (p - n_ctx) % tq == 0
    return pl.pallas_call(
        functools.partial(_swa_kernel, n_ctx=n_ctx),
        grid=(b, p // tq),
        in_specs=[pl.BlockSpec(memory_space=pltpu.SMEM),
                  pl.BlockSpec((1, tq, SWA_WIDTH), lambda bi, j: (bi, j, 0)),
                  pl.BlockSpec((1, p, ks.shape[2]), lambda bi, j: (bi, 0, 0)),
                  pl.BlockSpec((1, p, vs.shape[2]), lambda bi, j: (bi, 0, 0))],
        out_specs=pl.BlockSpec((1, tq, SWA_WIDTH), lambda bi, j: (bi, j, 0)),
        out_shape=jax.ShapeDtypeStruct((b, p, SWA_WIDTH), BF16),
        compiler_params=_cparams(("arbitrary", "arbitrary")),
        name="swa_attention",
    )(sink, qs, ks, vs)


def _pool_kernel(prev_ref, cur_ref, next_ref, w_ref, scale_ref, y_ref, ext_ref, *, n_ctx):
    j = pl.program_id(1)
    tm = cur_ref.shape[1]
    nj = pl.num_programs(1)
    s_len = (nj - 1) * tm
    is_ctx = j == nj - 1
    has_prev = (j >= 1) & (j < nj - 1)
    has_next = j < nj - 2
    ext_ref[0:POOL_HALO, :] = jnp.where(has_prev, prev_ref[0], 0.0)
    ext_ref[POOL_HALO:POOL_HALO + tm, :] = cur_ref[0]
    ext_ref[POOL_HALO + tm:POOL_HALO + tm + POOL_HALO, :] = jnp.where(has_next, next_ref[0], 0.0)
    t = lax.broadcasted_iota(I32, (tm, 1), 0)
    pos = jnp.where(is_ctx, t, j * tm + t)
    seg_len = jnp.where(is_ctx, n_ctx, s_len)
    for g, w in enumerate(POOL_WINDOWS):
        cols = slice(g * POOL_GROUP_DIM, (g + 1) * POOL_GROUP_DIM)
        acc = jnp.zeros((tm, POOL_GROUP_DIM), F32)
        for off in range(-(w // 2), w - w // 2):
            acc = acc + ext_ref[POOL_HALO + off:POOL_HALO + off + tm, cols]
        lo = jnp.maximum(pos - w // 2, 0)
        hi = jnp.minimum(pos + w - w // 2, seg_len)
        cnt = (hi - lo).astype(F32)
        pooled = acc / cnt - cur_ref[0, :, cols]
        y_ref[0, :, cols] = (_dot(pooled.astype(BF16), w_ref[g]) * scale_ref[:, cols]).astype(BF16)


def _pool_stage(u, w_pool, pool_scale, n_ctx):
    b, p, width = u.shape
    tm = TOKEN_TILE
    hb = tm // POOL_HALO
    n_halo_blocks = p // POOL_HALO
    return pl.pallas_call(
        functools.partial(_pool_kernel, n_ctx=n_ctx),
        grid=(b, p // tm),
        in_specs=[pl.BlockSpec((1, POOL_HALO, width), lambda bi, j: (bi, jnp.maximum(j * hb - 1, 0), 0)),
                  pl.BlockSpec((1, tm, width), lambda bi, j: (bi, j, 0)),
                  pl.BlockSpec((1, POOL_HALO, width),
                               lambda bi, j: (bi, jnp.minimum((j + 1) * hb, n_halo_blocks - 1), 0)),
                  pl.BlockSpec(w_pool.shape, lambda bi, j: (0, 0, 0)),
                  pl.BlockSpec(pool_scale.shape, lambda bi, j: (0, 0))],
        out_specs=pl.BlockSpec((1, tm, width), lambda bi, j: (bi, j, 0)),
        out_shape=jax.ShapeDtypeStruct((b, p, width), BF16),
        scratch_shapes=[pltpu.VMEM((tm + 2 * POOL_HALO, width), F32)],
        compiler_params=_cparams(("arbitrary", "arbitrary")),
        name="pool_stage",
    )(u, u, u, w_pool, pool_scale)


MERGE_TILES = 2


def _merge_kernel(*refs, n_stream, ctx_tile, tiles_per_row):
    n_in = n_stream + 6
    tiles = [refs[t * n_in:(t + 1) * n_in] for t in range(MERGE_TILES)]
    (g_ref, wbm_ref, wbp_ref, wbs_ref, wout_ref, wsg_ref, wsu_ref, wsd_ref, wrt_ref,
     xs_ref, fp_ref, lg_ref) = refs[MERGE_TILES * n_in:]
    d = SLAB_ROWS * 128
    tm = tiles[0][n_stream + 2].shape[1]
    both = range(MERGE_TILES)
    is_ctx = [(pl.program_id(0) * MERGE_TILES + t) % tiles_per_row == ctx_tile for t in both]
    mod = [tiles[t][n_stream + 5][0, 0] for t in both]
    gate = [tiles[t][n_stream + 4][0] for t in both]
    branches = []
    for t in both:
        yml_ref, ymc_ref, yp_ref, ys_ref = tiles[t][n_stream:n_stream + 4]
        ym = jnp.where(is_ctx[t], ymc_ref[0], yml_ref[0])
        branches.append((_dot(ym, wbm_ref[...]), _dot(yp_ref[0], wbp_ref[...]), _dot(ys_ref[0], wbs_ref[...])))
    merged = [(gate[t][:, 0:d].astype(F32) * branches[t][0] + gate[t][:, d:2 * d].astype(F32) * branches[t][1]
               + gate[t][:, 2 * d:3 * d].astype(F32) * branches[t][2]).astype(BF16) for t in both]
    proj = [_dot(merged[t], wout_ref[...]) for t in both]
    x_mid = [_stream_tile(tiles[t][:n_stream], tm, is_ctx[t]) + mod[t][2:3] * proj[t] for t in both]
    f = [_rms(x_mid[t], g_ref[...]) * (1.0 + mod[t][4:5]) + mod[t][3:4] for t in both]
    fb = [f[t].astype(BF16) for t in both]
    up = [(_dot(fb[t], wsg_ref[...]), _dot(fb[t], wsu_ref[...])) for t in both]
    for t in both:
        lg_ref[:, t * tm:(t + 1) * tm] = _dot_nt(wrt_ref[...], fb[t])
    hmid = [(up[t][0] * _sigmoid(up[t][0]) * up[t][1]).astype(BF16) for t in both]
    shared = [_dot(hmid[t], wsd_ref[...]) for t in both]
    for t in both:
        _matrix_to_slab_rows(xs_ref.at[t * tm * SLAB_ROWS:(t + 1) * tm * SLAB_ROWS, :],
                             x_mid[t] + mod[t][5:6] * shared[t], SLAB_ROWS)
        _matrix_to_slab_rows(fp_ref.at[t * tm * PACKED_ROWS:(t + 1) * tm * PACKED_ROWS, :],
                             _pack_bf16_pair(f[t]), PACKED_ROWS)


def _merge_stage(stream, ym_lat, ym_ctx, yp, ys, gates, modtab, g_ffn, wbm, wbp, wbs, wout, wsg, wsu, wsd, wrt,
                 keep_ctx):
    b, p, _ = yp.shape
    assert ym_ctx.shape[1] == TOKEN_TILE
    d = SLAB_ROWS * 128
    tm = TOKEN_TILE
    nj = p // tm
    nk = nj if keep_ctx else nj - 1
    per = MERGE_TILES
    assert (b * nk) % per == 0
    full = lambda a: pl.BlockSpec(a.shape, lambda s: (0,) * a.ndim)

    def tile_specs(t):
        row = lambda s: (s * per + t) // nk
        col = lambda s: (s * per + t) % nk
        tok = lambda w: pl.BlockSpec((1, tm, w), lambda s: (row(s), col(s), 0))
        lat = lambda w: pl.BlockSpec((1, tm, w), lambda s: (row(s), jnp.minimum(col(s), nj - 2), 0))
        ctx = lambda w: pl.BlockSpec((1, tm, w), lambda s: (row(s), 0, 0))
        if len(stream) == 1:
            x_specs = [pl.BlockSpec((tm * SLAB_ROWS, 128), lambda s: (row(s) * nj + col(s), 0))]
        else:
            x_specs = [lat(d), ctx(d)]
        return x_specs + [lat(ym_lat.shape[2]), ctx(ym_ctx.shape[2]), tok(yp.shape[2]), tok(ys.shape[2]),
                          tok(gates.shape[2]),
                          pl.BlockSpec((1, 1, 8, d), lambda s: (row(s), col(s) // (nj - 1), 0, 0))]

    tile_args = list(stream) + [ym_lat, ym_ctx, yp, ys, gates, modtab]
    return pl.pallas_call(
        functools.partial(_merge_kernel, n_stream=len(stream), ctx_tile=nj - 1, tiles_per_row=nk),
        grid=(b * nk // per,),
        in_specs=[spec for t in range(per) for spec in tile_specs(t)] + [
            full(g_ffn), full(wbm), full(wbp), full(wbs), full(wout), full(wsg), full(wsu), full(wsd), full(wrt)],
        out_specs=[pl.BlockSpec((per * tm * SLAB_ROWS, 128), lambda s: (s, 0)),
                   pl.BlockSpec((per * tm * PACKED_ROWS, 128), lambda s: (s, 0)),
                   pl.BlockSpec((N_EXPERTS, per * tm), lambda s: (0, s))],
        out_shape=[jax.ShapeDtypeStruct((b * nk * tm * SLAB_ROWS, 128), F32),
                   jax.ShapeDtypeStruct((b * nk * tm * PACKED_ROWS, 128), U32),
                   jax.ShapeDtypeStruct((N_EXPERTS, b * nk * tm), F32)],
        compiler_params=_cparams(("arbitrary",)),
        name="merge_stage",
    )(*(tile_args * per), g_ffn, wbm, wbp, wbs, wout, wsg, wsu, wsd, wrt)


def _route_kernel(lg_ref, bias_ref, dest_ref, wts_ref, seg_ref):
    tm = lg_ref.shape[1]
    ne = N_EXPERTS
    neg_inf = -jnp.inf
    scores = _sigmoid(lg_ref[...])
    sel = scores + bias_ref[...]
    iota_g = lax.broadcasted_iota(I32, (EXPERTS_PER_GROUP, tm), 0)
    gscore = []
    for g in range(N_GROUPS):
        sg = sel[g * EXPERTS_PER_GROUP:(g + 1) * EXPERTS_PER_GROUP]
        m1 = jnp.max(sg, axis=0, keepdims=True)
        i1 = jnp.min(jnp.where(sg == m1, iota_g, EXPERTS_PER_GROUP), axis=0, keepdims=True)
        m2 = jnp.max(jnp.where(iota_g == i1, neg_inf, sg), axis=0, keepdims=True)
        gscore.append(m1 + m2)
    rows = []
    for g in range(N_GROUPS):
        rank = jnp.zeros((1, tm), I32)
        for g2 in range(N_GROUPS):
            if g2 == g:
                continue
            beats = (gscore[g2] >= gscore[g]) if g2 < g else (gscore[g2] > gscore[g])
            rank = rank + beats.astype(I32)
        rows.append(jnp.where(rank < TOPK_GROUPS, sel[g * EXPERTS_PER_GROUP:(g + 1) * EXPERTS_PER_GROUP], neg_inf))
    cur = jnp.concatenate(rows, axis=0)
    iota_e = lax.broadcasted_iota(I32, (ne, tm), 0)
    picks = []
    member = jnp.zeros((ne, tm), F32)
    for _ in range(TOP_K):
        m = jnp.max(cur, axis=0, keepdims=True)
        idx = jnp.min(jnp.where(cur == m, iota_e, ne), axis=0, keepdims=True)
        hit = iota_e == idx
        picks.append(hit)
        member = member + hit.astype(F32)
        cur = jnp.where(hit, neg_inf, cur)
    earlier = (lax.broadcasted_iota(I32, (tm, tm), 0) < lax.broadcasted_iota(I32, (tm, tm), 1)).astype(BF16)
    pos = _dot(member.astype(BF16), earlier)
    cnt_col = jnp.sum(member, axis=1, keepdims=True)
    blocks_col = jnp.floor((cnt_col + (SEG_ALIGN - 1)) * (1.0 / SEG_ALIGN))
    lower = (lax.broadcasted_iota(I32, (ne, ne), 1) < lax.broadcasted_iota(I32, (ne, ne), 0)).astype(BF16)
    off_col = _dot(lower, jnp.broadcast_to(blocks_col, (ne, 128)).astype(BF16))[:, 0:1] * SEG_ALIGN
    base = off_col + pos
    w_rows = [jnp.sum(jnp.where(hit, scores, 0.0), axis=0, keepdims=True) for hit in picks]
    denom = w_rows[0]
    for w in w_rows[1:]:
        denom = denom + w
    for k, hit in enumerate(picks):
        dest_ref[0, k:k + 1, :] = jnp.sum(jnp.where(hit, base, 0.0), axis=0, keepdims=True).astype(I32)
        wts_ref[0, k:k + 1, :] = w_rows[k] / denom * ROUTED_SCALE
    member_pad = jnp.concatenate([member, jnp.zeros((128 - ne, tm), F32)], axis=0).astype(BF16)
    cnt_row = _dot_nt(jnp.ones((8, tm), BF16), member_pad)
    blocks_row = jnp.floor((cnt_row + (SEG_ALIGN - 1)) * (1.0 / SEG_ALIGN))
    before = (lax.broadcasted_iota(I32, (128, 128), 0) < lax.broadcasted_iota(I32, (128, 128), 1)).astype(BF16)
    off_row = _dot(blocks_row.astype(BF16), before) * SEG_ALIGN
    r = lax.broadcasted_iota(I32, (8, 128), 0)
    seg_ref[0] = jnp.where(r == 0, off_row, jnp.where(r == 1, cnt_row, 0.0)).astype(I32)


def _route_stage(lg_t, bias_col):
    ne, t_all = lg_t.shape
    tm = MOE_TILE
    nt = t_all // tm
    return pl.pallas_call(
        _route_kernel,
        grid=(nt,),
        in_specs=[pl.BlockSpec((ne, tm), lambda i: (0, i)),
                  pl.BlockSpec((ne, 1), lambda i: (0, 0))],
        out_specs=[pl.BlockSpec((1, TOP_K, tm), lambda i: (i, 0, 0)),
                   pl.BlockSpec((1, TOP_K, tm), lambda i: (i, 0, 0)),
                   pl.BlockSpec((1, 8, 128), lambda i: (i, 0, 0))],
        out_shape=[jax.ShapeDtypeStruct((nt, TOP_K, tm), I32),
                   jax.ShapeDtypeStruct((nt, TOP_K, tm), F32),
                   jax.ShapeDtypeStruct((nt, 8, 128), I32)],
        compiler_params=_cparams(("arbitrary",)),
        name="route_stage",
    )(lg_t, bias_col)


def _moe_row_stride(tm):
    cap = TOP_K * tm + N_EXPERTS * SEG_ALIGN + MOE_CHUNK
    blocks = cap // 8 + 1
    return 8 * (blocks + 1 - blocks % 2)


def _moe_kernel(dest_ref, wts_ref, seg_ref, fp_hbm, xs_hbm, gt_ref, gfin_ref, wg_ref, wu_ref, wd_ref, out_hbm,
                bufs_ref, fp_stage, xs_stage, out_stage, slab_tmp, fp_sem, xs_sem, out_sem,
                *, srow, sub, n_sub_per_batch, ctx_sub, n_tiles, final_norm):
    pr = pl.program_id(0)
    eg = pl.program_id(1)
    tm = dest_ref.shape[2]
    ch = MOE_CHUNK
    group = wg_ref.shape[0]
    lane_blk = 128
    n_blk = tm // lane_blk
    per = bufs_ref.shape[0]
    n_slots = fp_stage.shape[0]
    n_here = jnp.minimum(per, n_tiles - pr * per)

    def table_block(ref, half, blk):
        return ref.at[half, :, pl.ds(pl.multiple_of(blk * lane_blk, lane_blk), lane_blk)]

    def table_rows(ref, half, blk):
        cols = pl.ds(pl.multiple_of(blk * lane_blk, lane_blk), lane_blk)
        return [ref.at[half, k, cols] for k in range(TOP_K)]

    def token_rows(half, blk, per_tok):
        first = ((pr * per + half) * tm + blk * lane_blk) * per_tok
        return pl.ds(pl.multiple_of(first, lane_blk * per_tok), lane_blk * per_tok)

    def fp_copy(half, blk, slot):
        return pltpu.make_async_copy(fp_hbm.at[token_rows(half, blk, PACKED_ROWS), :], fp_stage.at[slot],
                                     fp_sem.at[slot])

    def xs_copy(half, blk, slot):
        return pltpu.make_async_copy(xs_hbm.at[token_rows(half, blk, SLAB_ROWS), :], xs_stage.at[slot],
                                     xs_sem.at[slot])

    def out_copy(half, blk, slot):
        if not final_norm:
            dst = out_hbm.at[token_rows(half, blk, SLAB_ROWS), :]
        else:
            first = (pr * per + half) * tm + blk * lane_blk
            row_len = out_hbm.shape[1]
            bi = first // row_len
            dst = out_hbm.at[bi, pl.ds(pl.multiple_of(first - bi * row_len, lane_blk), lane_blk), :]
        return pltpu.make_async_copy(out_stage.at[slot], dst, out_sem.at[slot])

    def group_rows(half):
        buf_ref = bufs_ref.at[half]
        for blk in range(n_slots - 1):
            fp_copy(half, blk, blk).start()
        zeros8 = jnp.zeros((SEG_ALIGN, 128), U32)
        for e in range(N_EXPERTS):
            pad0 = seg_ref[half, 0, e] + (seg_ref[half, 1, e] // SEG_ALIGN) * SEG_ALIGN
            for q in range(PACKED_ROWS):
                buf_ref[pl.ds(pl.multiple_of(q * srow + pad0, SEG_ALIGN), SEG_ALIGN), :] = zeros8
        last = N_EXPERTS - 1
        end = seg_ref[half, 0, last] + (seg_ref[half, 1, last] + SEG_ALIGN - 1) // SEG_ALIGN * SEG_ALIGN
        for q in range(PACKED_ROWS):
            buf_ref[pl.ds(pl.multiple_of(q * srow + end, SEG_ALIGN), ch), :] = jnp.zeros((ch, 128), U32)

        def body(blk, carry):
            slot = blk % n_slots
            fp_copy(half, blk, slot).wait()

            @pl.when(blk + n_slots - 1 < n_blk)
            def _prefetch():
                fp_copy(half, blk + n_slots - 1, (blk + n_slots - 1) % n_slots).start()

            dest = table_rows(dest_ref, half, blk)
            rows = fp_stage.at[slot]
            for u in range(lane_blk):
                slab = rows[u * PACKED_ROWS:(u + 1) * PACKED_ROWS, :]
                for k in range(TOP_K):
                    buf_ref[pl.ds(dest[k][u], PACKED_ROWS, stride=srow), :] = slab
            return carry

        lax.fori_loop(0, n_blk, body, 0)

    def experts():
        segments = [(ge, h) for ge in range(group) for h in range(per)]
        counts = {(ge, h): seg_ref[h, 1, eg * group + ge] for ge, h in segments}
        offsets = {(ge, h): seg_ref[h, 0, eg * group + ge] for ge, h in segments}

        def ffn_rows(segs, rows, chunk_idx):
            first = {s: offsets[s] + chunk_idx * rows for s in segs}
            words = {s: [bufs_ref[s[1], pl.ds(pl.multiple_of(q * srow + first[s], 8), rows), :] for q in range(4)]
                     for s in segs}
            xb = {}
            for s in segs:
                w = jnp.concatenate(words[s], axis=1)
                xb[s] = jnp.concatenate([_unpack_lo(w), _unpack_hi(w)], axis=1).astype(BF16)
            gates = {s: (_dot(xb[s], wg_ref[s[0]]), _dot(xb[s], wu_ref[s[0]])) for s in segs}
            hmid = {s: (gates[s][0] * _sigmoid(gates[s][0]) * gates[s][1]).astype(BF16) for s in segs}
            packed = {s: _pack_bf16_pair(_dot(hmid[s], wd_ref[s[0]])) for s in segs}
            for s in segs:
                keep = lax.broadcasted_iota(I32, (rows, 1), 0) < counts[s] - chunk_idx * rows
                for q in range(4):
                    bufs_ref[s[1], pl.ds(pl.multiple_of(q * srow + first[s], 8), rows), :] = jnp.where(
                        keep, packed[s][:, q * 128:(q + 1) * 128], words[s][q])

        most = functools.reduce(jnp.maximum, counts.values())
        lo = 0
        for rows in MOE_BODY_ROWS:
            pl.when((most > lo) & (most <= rows))(functools.partial(ffn_rows, tuple(segments), rows, 0))
            lo = rows

        @pl.when(most > ch)
        def _long_segments():
            for s in segments:
                def chunk(c, carry, s=s):
                    ffn_rows((s,), ch, c)
                    return carry

                lax.fori_loop(0, (counts[s] + ch - 1) // ch, chunk, 0)

    def combine(half):
        buf_ref = bufs_ref.at[half]
        for blk in range(n_slots - 1):
            xs_copy(half, blk, blk).start()

        def body(blk, carry):
            slot = blk % n_slots
            xs_copy(half, blk, slot).wait()

            @pl.when(blk + n_slots - 1 < n_blk)
            def _prefetch():
                xs_copy(half, blk + n_slots - 1, (blk + n_slots - 1) % n_slots).start()

            @pl.when(blk >= n_slots)
            def _slot_free():
                out_copy(half, blk - n_slots, slot).wait()

            sub_blk = ((pr * per + half) * tm + blk * lane_blk) // sub
            bi = sub_blk // n_sub_per_batch
            gate = gt_ref[bi * 2 + (sub_blk - bi * n_sub_per_batch) // ctx_sub]
            dest = table_block(dest_ref, half, blk)
            wts = table_block(wts_ref, half, blk)
            xs = xs_stage.at[slot]
            out = slab_tmp if final_norm else out_stage.at[slot]
            for u in range(lane_blk):
                acc_lo = jnp.zeros((4, 128), F32)
                acc_hi = jnp.zeros((4, 128), F32)
                for k in range(TOP_K):
                    words = buf_ref[pl.ds(dest[k, u], PACKED_ROWS, stride=srow), :]
                    wk = wts[k, u]
                    acc_lo = acc_lo + wk * _unpack_lo(words)
                    acc_hi = acc_hi + wk * _unpack_hi(words)
                out[u * 8:u * 8 + 4, :] = xs[u * 8:u * 8 + 4, :] + gate[0:4] * acc_lo
                out[u * 8 + 4:u * 8 + 8, :] = xs[u * 8 + 4:u * 8 + 8, :] + gate[4:8] * acc_hi
            if final_norm:
                out_stage[slot] = _rms(_slab_rows_to_matrix(slab_tmp, lane_blk, SLAB_ROWS), gfin_ref[...])
            out_copy(half, blk, slot).start()
            return carry

        lax.fori_loop(0, n_blk, body, 0)
        for blk in range(n_blk - n_slots, n_blk):
            out_copy(half, blk, blk % n_slots).wait()

    def each_tile(fn):
        def body(half, carry):
            fn(half)
            return carry

        lax.fori_loop(0, n_here, body, 0)

    @pl.when(eg == 0)
    def _first_step():
        each_tile(group_rows)
        for half in range(1, per):
            @pl.when(half >= n_here)
            def _clear(half=half):
                bufs_ref[half] = jnp.zeros(bufs_ref.shape[1:], U32)

    experts()
    pl.when(eg == pl.num_programs(1) - 1)(functools.partial(each_tile, combine))


def _moe_stage(dest, wts, seg, fp4, xs8, gt2, g_final, wg, wu, wd, layer, n_sub_per_batch, keep_ctx, final_norm):
    ctx_sub = n_sub_per_batch - 1 if keep_ctx else n_sub_per_batch
    d = SLAB_ROWS * 128
    tokens_per_row = n_sub_per_batch * TOKEN_TILE
    out_shape = (xs8.shape[0] // (SLAB_ROWS * tokens_per_row), tokens_per_row, d) if final_norm else xs8.shape
    out_block = (128, d) if final_norm else (128 * SLAB_ROWS, 128)
    nt, _, tm = dest.shape
    ne = wg.shape[1]
    group = MOE_EXPERTS_PER_STEP
    per = MOE_TILES_PER_STEP
    assert ne % group == 0 and tm % 256 == 0 and TOKEN_TILE % 128 == 0 and per == 2
    n_rows = -(-nt // per)
    pad = ((0, n_rows * per - nt), (0, 0), (0, 0))
    dest, wts, seg = jnp.pad(dest, pad), jnp.pad(wts, pad), jnp.pad(seg, pad)
    srow = _moe_row_stride(tm)
    smem = lambda shape: pl.BlockSpec(shape, lambda i, e: (i, 0, 0), memory_space=pltpu.SMEM)
    hbm = pl.BlockSpec(memory_space=pl.ANY)
    slots = MOE_STAGE_SLOTS
    assert tm // 128 >= slots
    stage = lambda per_tok, dt: pltpu.VMEM((slots, 128 * per_tok, 128), dt)
    return pl.pallas_call(
        functools.partial(_moe_kernel, srow=srow, sub=TOKEN_TILE, n_sub_per_batch=n_sub_per_batch, ctx_sub=ctx_sub,
                          n_tiles=nt, final_norm=final_norm),
        grid=(n_rows, ne // group),
        in_specs=[smem((per, TOP_K, tm)), smem((per, TOP_K, tm)), smem((per, 8, 128)),
                  hbm, hbm,
                  pl.BlockSpec(gt2.shape, lambda i, e: (0, 0, 0)),
                  pl.BlockSpec(g_final.shape, lambda i, e: (0, 0)),
                  pl.BlockSpec((None, group) + wg.shape[2:], lambda i, e: (layer, e, 0, 0)),
                  pl.BlockSpec((None, group) + wu.shape[2:], lambda i, e: (layer, e, 0, 0)),
                  pl.BlockSpec((None, group) + wd.shape[2:], lambda i, e: (layer, e, 0, 0))],
        out_specs=hbm,
        out_shape=jax.ShapeDtypeStruct(out_shape, F32),
        scratch_shapes=[pltpu.VMEM((per, 4 * srow, 128), U32),
                        stage(PACKED_ROWS, U32), stage(SLAB_ROWS, F32), pltpu.VMEM((slots,) + out_block, F32),
                        pltpu.VMEM((128 * SLAB_ROWS, 128), F32),
                        pltpu.SemaphoreType.DMA((slots,)), pltpu.SemaphoreType.DMA((slots,)),
                        pltpu.SemaphoreType.DMA((slots,))],
        compiler_params=_cparams(("arbitrary", "arbitrary")),
        name="moe_stage",
    )(dest, wts, seg, fp4, xs8, gt2, g_final, wg, wu, wd)


def _rot_lanes(x, head_dim):
    quarter = head_dim // 4
    lane = lax.broadcasted_iota(I32, x.shape, 1)
    width = x.shape[1]
    return jnp.where(lane % (2 * quarter) < quarter, -pltpu.roll(x, width - quarter, 1), pltpu.roll(x, quarter, 1))


def _w1_kernel(w_ref, o_ref):
    offs = [0]
    for w in (MLA_Q_LORA, MLA_KV_LORA, MLA_ROPE, POOL_WIDTH, SWA_WIDTH, SWA_KV_WIDTH, SWA_KV_WIDTH):
        offs.append(offs[-1] + w)
    o_cq, o_ckv, o_kr, o_u, o_qs, o_ks, o_vs, o_gl = offs
    lane = lax.broadcasted_iota(I32, (w_ref.shape[0], 128), 1)

    def put(name, val):
        a, b = _SEG[name]
        o_ref[:, a:b] = val.astype(BF16)

    def kv_slots(x):
        low = lane < SWA_HEAD_DIM
        return jnp.concatenate([jnp.where(low, x, 0.0), jnp.where(low, pltpu.roll(x, SWA_HEAD_DIM, 1), 0.0)], axis=1)

    def kr_slot(x):
        return jnp.where((lane >= MLA_NOPE) & (lane < MLA_NOPE + MLA_ROPE), pltpu.roll(x, MLA_NOPE, 1), 0.0)

    put("cq", w_ref[:, o_cq:o_ckv])
    put("ckv", w_ref[:, o_ckv:o_kr])
    put("u", w_ref[:, o_u:o_qs])
    put("qs", w_ref[:, o_qs:o_ks] * (SWA_SCALE * LOG2_E))
    put("ks", kv_slots(w_ref[:, o_ks:o_vs]))
    put("vs", kv_slots(w_ref[:, o_vs:o_gl]))
    put("kr", kr_slot(w_ref[:, o_kr:o_kr + 128]))
    put("gl", w_ref[:, o_gl:w_ref.shape[1]])


def _fused_in_weight(w_in, layer):
    _, d, width = w_in.shape
    tr = 256
    assert d % tr == 0 and SWA_KV_WIDTH == 128 and MLA_KV_LORA + MLA_Q_LORA == 5 * 128
    return pl.pallas_call(
        _w1_kernel,
        grid=(d // tr,),
        in_specs=[pl.BlockSpec((None, tr, width), lambda i: (layer, i, 0))],
        out_specs=pl.BlockSpec((tr, FUSED_IN_WIDTH), lambda i: (i, 0)),
        out_shape=jax.ShapeDtypeStruct((d, FUSED_IN_WIDTH), BF16),
        compiler_params=_cparams(("arbitrary",)),
        name="fused_in_weight",
    )(w_in)


def _rope_pattern(s_len, n_ctx, rot_dim):
    t = jnp.arange(s_len)
    row = (t // GRID_W).astype(F32)
    col = (t % GRID_W).astype(F32)
    n_freq = rot_dim // 4
    inv_freq = ROPE_BASE ** (-jnp.arange(n_freq, dtype=F32) / n_freq)
    ang_r = row[:, None] * inv_freq[None, :]
    ang_c = col[:, None] * inv_freq[None, :]
    cos = jnp.concatenate([jnp.cos(ang_r), jnp.cos(ang_r), jnp.cos(ang_c), jnp.cos(ang_c)], axis=1)
    sin = jnp.concatenate([jnp.sin(ang_r), jnp.sin(ang_r), jnp.sin(ang_c), jnp.sin(ang_c)], axis=1)
    cos = jnp.concatenate([cos, jnp.ones((n_ctx, rot_dim), F32)], axis=0)
    sin = jnp.concatenate([sin, jnp.zeros((n_ctx, rot_dim), F32)], axis=0)
    return cos, sin


def _layer_weights(w_in, layer, w_uq, w_ukv):
    w1 = _fused_in_weight(w_in, layer)
    lq = w_uq.shape[0]
    wq3 = (w_uq * (MLA_SCALE * LOG2_E)).reshape(lq, MLA_HEADS, MLA_NOPE + MLA_ROPE)
    pad = jnp.zeros((lq, MLA_HEADS, MLA_HEAD_PAD - MLA_NOPE - MLA_ROPE), F32)
    wq = jnp.concatenate([wq3, pad], axis=2).reshape(lq, MLA_HEADS * MLA_HEAD_PAD).astype(BF16)
    lkv = w_ukv.shape[0]
    wkv3 = w_ukv.reshape(lkv, MLA_HEADS, MLA_NOPE + MLA_V)
    wkn = jnp.concatenate([wkv3[:, :, :MLA_NOPE], jnp.zeros((lkv, MLA_HEADS, MLA_HEAD_PAD - MLA_NOPE), F32)],
                          axis=2).reshape(lkv, MLA_HEADS * MLA_HEAD_PAD).astype(BF16)
    wv = jnp.concatenate([wkv3[:, :, MLA_NOPE:], jnp.zeros((lkv, MLA_HEADS, MLA_HEAD_PAD - MLA_V), F32)],
                         axis=2).reshape(lkv, MLA_HEADS * MLA_HEAD_PAD).astype(BF16)
    return w1, wq, wkn, wv


def kernel(x, c, ctx, c_ctx, w_mod, b_mod, g_mix, g_ffn, w_in, g_mla_q, g_mla_kv, w_mla_uq, w_mla_ukv, w_pool,
           pool_scale, swa_sink, w_br_mla, w_br_pool, w_br_swa, w_out, w_router, router_bias, w_exp_gate,
           w_exp_up, w_exp_down, w_sh_gate, w_sh_up, w_sh_down, g_final):
    b, s_len, d = x.shape
    n_ctx = ctx.shape[1]
    n_layers = w_mod.shape[0]
    p = n_ctx + s_len
    assert n_ctx == TOKEN_TILE and s_len % TOKEN_TILE == 0 and (b * p) % MOE_TILE == 0 and b + 1 <= 8
    assert d == 1024 and w_in.shape[2] == (MLA_Q_LORA + MLA_KV_LORA + MLA_ROPE + POOL_WIDTH + SWA_WIDTH
                                           + 2 * SWA_KV_WIDTH + 3 * d)

    cm, sm = _rope_pattern(s_len, n_ctx, MLA_ROPE)
    tail = jnp.zeros((p, MLA_HEAD_PAD - MLA_NOPE - MLA_ROPE), F32)
    tabm = jnp.concatenate([jnp.ones((p, MLA_NOPE), F32), cm, tail, jnp.zeros((p, MLA_NOPE), F32), sm, tail], axis=1)
    cs, ss = _rope_pattern(s_len, n_ctx, SWA_HEAD_DIM)
    tabs = jnp.concatenate([cs, cs, ss, ss], axis=1)

    cvec = jnp.concatenate([c, c_ctx[None, :], jnp.zeros((8 - b - 1, d), F32)], axis=0)
    mod_all = _modulation(cvec, w_mod, b_mod).reshape(n_layers, 8, N_MOD, d)

    stream = (x, ctx)
    wg_all, wu_all, wd_all = w_exp_gate.astype(BF16), w_exp_up.astype(BF16), w_exp_down.astype(BF16)
    for i in range(n_layers):
        keep_ctx = i < n_layers - 1
        lat = mod_all[i, :b]
        ctx_rows = jnp.broadcast_to(mod_all[i, b][None], (b, N_MOD, d))
        modtab = jnp.pad(jnp.stack([lat, ctx_rows], axis=1), ((0, 0), (0, 0), (0, 8 - N_MOD), (0, 0)))
        w1, wq, wkn, wv = _layer_weights(w_in, i, w_mla_uq[i], w_mla_ukv[i])
        q, k, v, u, qs, ks, vs, gates = _input_stage(
            stream, b, modtab, g_mix[i][None], w1, g_mla_q[i][None], wq, g_mla_kv[i][None], wkn, wv, tabm, tabs)
        y_mla, y_mla_ctx = _mla_attention(q, k, v, n_ctx)
        y_swa = _swa_attention(swa_sink[i], qs, ks, vs, n_ctx)
        y_pool = _pool_stage(u, w_pool[i].astype(BF16), pool_scale[i][None], n_ctx)
        xs, fp, lg_t = _merge_stage(
            stream, y_mla, y_mla_ctx, y_pool, y_swa, gates, modtab, g_ffn[i][None], w_br_mla[i].astype(BF16),
            w_br_pool[i].astype(BF16), w_br_swa[i].astype(BF16), w_out[i].astype(BF16),
            w_sh_gate[i].astype(BF16), w_sh_up[i].astype(BF16), w_sh_down[i].astype(BF16),
            w_router[i].T.astype(BF16), keep_ctx)
        dest, wts, seg = _route_stage(lg_t, router_bias[i][:, None])
        gt2 = modtab[:, :, 5, :].reshape(b * 2, 8, d // 8)
        stream = (_moe_stage(dest, wts, seg, fp, xs, gt2, g_final[None], wg_all, wu_all, wd_all, i,
                             (p if keep_ctx else s_len) // TOKEN_TILE, keep_ctx, final_norm=not keep_ctx),)
    return stream[0]
```
